```python
import math
import jax, jax.numpy as jnp
from jax import lax
import numpy as np

D_MODEL = 1024
BATCH = 8
SEQ = 8192
DEPTH = 2

N_META = 16
N_A_LAYERS = DEPTH // 2
N_B_LAYERS = DEPTH - N_A_LAYERS
NORM_EPS = 1e-6

GDN_QK_HEADS = 8
GDN_V_HEADS = 16
GDN_DK = 128
GDN_DV = 128
GDN_CONV = 4
GDN_CHUNK = 64
GDN_QK_W = GDN_QK_HEADS * GDN_DK
GDN_V_W = GDN_V_HEADS * GDN_DV
GDN_CONV_W = 2 * GDN_QK_W + GDN_V_W
GDN_IN_W = GDN_CONV_W + GDN_V_W + 2 * GDN_V_HEADS

MLA_HEADS = 16
MLA_NOPE = 128
MLA_ROPE = 64
MLA_V = 128
MLA_Q_RANK = 256
MLA_KV_RANK = 128
MLA_QK = MLA_NOPE + MLA_ROPE
MLA_V_W = MLA_HEADS * MLA_V
MLA_IN_W = MLA_Q_RANK + MLA_V_W
ROPE_THETA = 10000.0
Q_BLOCK = 128

kernel_name = "yoco_gdn_mla_hybrid"


def rmsnorm(x, g):
    xf = x.astype(jnp.float32)
    y = xf * lax.rsqrt(jnp.mean(xf * xf, axis=-1, keepdims=True) + NORM_EPS)
    return (y * g.astype(jnp.float32)).astype(x.dtype)


def l2norm(x):
    xf = x.astype(jnp.float32)
    return (xf * lax.rsqrt(jnp.sum(xf * xf, axis=-1, keepdims=True) + NORM_EPS)).astype(x.dtype)


def causal_depthwise_conv(x, w):
    k_len, ch = w.shape
    return lax.conv_general_dilated(x, w[:, None, :].astype(x.dtype), window_strides=(1,),
                                    padding=[(k_len - 1, 0)],
                                    dimension_numbers=('NWC', 'WIO', 'NWC'),
                                    feature_group_count=ch)


def rope_tables(length):
    inv = ROPE_THETA ** (-jnp.arange(0, MLA_ROPE, 2, dtype=jnp.float32) / MLA_ROPE)
    ang = jnp.arange(length, dtype=jnp.float32)[:, None] * inv[None, :]
    return jnp.cos(ang), jnp.sin(ang)


def apply_rope(x, cos, sin):
    xf = x.astype(jnp.float32)
    half = MLA_ROPE // 2
    x1, x2 = xf[..., :half], xf[..., half:]
    return jnp.concatenate([x1 * cos - x2 * sin, x2 * cos + x1 * sin], axis=-1).astype(x.dtype)


def gated_delta_rule_chunked(q, k, v, beta, g):
    B, L, H, dk = q.shape
    dv = v.shape[-1]
    C = GDN_CHUNK
    pad = (-L) % C
    n_chunks = (L + pad) // C

    def blocks(t):
        t = jnp.pad(t.astype(jnp.float32), [(0, 0), (pad, 0)] + [(0, 0)] * (t.ndim - 2))
        t = t.reshape((B, n_chunks, C) + t.shape[2:])
        return jnp.moveaxis(t, 3, 1)

    q, k, v, beta, g = blocks(q), blocks(k), blocks(v), blocks(beta), blocks(g)
    gc = jnp.cumsum(g, axis=-1)
    idx = jnp.arange(C)
    incl = idx[:, None] >= idx[None, :]
    strict = idx[:, None] > idx[None, :]
    decay = jnp.exp(jnp.where(incl, gc[..., :, None] - gc[..., None, :], -jnp.inf))

    kb = k * beta[..., None]
    vb = v * beta[..., None]
    m = jnp.einsum('bhnid,bhnjd->bhnij', kb, k) * jnp.where(strict, decay, 0.0)
    eye = jnp.eye(C, dtype=jnp.float32)
    rhs = jnp.concatenate([vb, kb * jnp.exp(gc)[..., None]], axis=-1)
    sol = lax.linalg.triangular_solve(m + eye, rhs, left_side=True, lower=True, unit_diagonal=True)
    u, w = sol[..., :dv], sol[..., dv:]

    attn = jnp.einsum('bhnid,bhnjd->bhnij', q, k) * decay
    q_dec = q * jnp.exp(gc)[..., None]
    k_dec = k * jnp.exp(gc[..., -1:] - gc)[..., None]
    g_last = jnp.exp(gc[..., -1])

    def step(state, xs):
        u_c, w_c, qd_c, kd_c, a_c, gl_c = xs
        v_new = u_c - jnp.einsum('bhcd,bhde->bhce', w_c, state)
        o_c = jnp.einsum('bhcd,bhde->bhce', qd_c, state) + jnp.einsum('bhij,bhje->bhie', a_c, v_new)
        state = state * gl_c[..., None, None] + jnp.einsum('bhcd,bhce->bhde', kd_c, v_new)
        return state, o_c

    xs = tuple(jnp.moveaxis(t, 2, 0) for t in (u, w, q_dec, k_dec, attn, g_last))
    s0 = jnp.zeros((B, H, dk, dv), jnp.float32)
    _, o = lax.scan(step, s0, xs)
    o = jnp.transpose(o, (1, 0, 3, 2, 4)).reshape(B, n_chunks * C, H, dv)
    return o[:, pad:]


def gdn_mixer(h, w_in, conv_w, a_log, dt_bias, out_norm, w_out):
    B, L, _ = h.shape
    proj = h @ w_in
    s1, s2, s3 = GDN_CONV_W, GDN_CONV_W + GDN_V_W, GDN_CONV_W + GDN_V_W + GDN_V_HEADS
    qkv, z, b, a = proj[..., :s1], proj[..., s1:s2], proj[..., s2:s3], proj[..., s3:]
    qkv = jax.nn.silu(causal_depthwise_conv(qkv, conv_w))
    q = l2norm(qkv[..., :GDN_QK_W].reshape(B, L, GDN_QK_HEADS, GDN_DK)) * (GDN_DK ** -0.5)
    k = l2norm(qkv[..., GDN_QK_W:2 * GDN_QK_W].reshape(B, L, GDN_QK_HEADS, GDN_DK))
    v = qkv[..., 2 * GDN_QK_W:].reshape(B, L, GDN_V_HEADS, GDN_DV)
    rep = GDN_V_HEADS // GDN_QK_HEADS
    q = jnp.repeat(q, rep, axis=2)
    k = jnp.repeat(k, rep, axis=2)
    beta = jax.nn.sigmoid(b.astype(jnp.float32))
    g = -jnp.exp(a_log.astype(jnp.float32)) * jax.nn.softplus(a.astype(jnp.float32) + dt_bias.astype(jnp.float32))
    o = gated_delta_rule_chunked(q, k, v, beta, g)
    o = rmsnorm(o, out_norm) * jax.nn.silu(z.astype(jnp.float32).reshape(B, L, GDN_V_HEADS, GDN_DV))
    return o.reshape(B, L, GDN_V_W).astype(h.dtype) @ w_out


def mla_shared_kv(h, kv_norm, kv_w_down, kv_latent_norm, kv_w_up, cos, sin):
    B, L, _ = h.shape
    ckr = rmsnorm(h, kv_norm) @ kv_w_down
    c_kv = rmsnorm(ckr[..., :MLA_KV_RANK], kv_latent_norm)
    k_rope = apply_rope(ckr[..., MLA_KV_RANK:], cos, sin)
    kv = (c_kv @ kv_w_up).reshape(B, L, MLA_HEADS, MLA_NOPE + MLA_V)
    return kv[..., :MLA_NOPE], k_rope, kv[..., MLA_NOPE:]


def causal_block_attention(q_nope, q_rope, k_nope, k_rope, v):
    B, L, H, _ = q_nope.shape
    n_blocks = -(-L // Q_BLOCK)
    pad = n_blocks * Q_BLOCK - L

    def blocks(t):
        t = jnp.pad(t, [(0, 0), (0, pad)] + [(0, 0)] * (t.ndim - 2))
        return jnp.moveaxis(t.reshape((B, n_blocks, Q_BLOCK) + t.shape[2:]), 1, 0)

    scale = MLA_QK ** -0.5
    k_pos = jnp.arange(L)

    def one_block(args):
        qn_b, qr_b, blk = args
        s = (jnp.einsum('bqhd,bkhd->bhqk', qn_b, k_nope, preferred_element_type=jnp.float32)
             + jnp.einsum('bqhr,bkr->bhqk', qr_b, k_rope, preferred_element_type=jnp.float32))
        q_pos = blk * Q_BLOCK + jnp.arange(Q_BLOCK)
        s = jnp.where(q_pos[:, None] >= k_pos[None, :], s * scale, -jnp.inf)
        p = jax.nn.softmax(s, axis=-1)
        return jnp.einsum('bhqk,bkhd->bqhd', p.astype(v.dtype), v)

    o = lax.map(one_block, (blocks(q_nope), blocks(q_rope), jnp.arange(n_blocks)))
    return jnp.moveaxis(o, 0, 1).reshape(B, n_blocks * Q_BLOCK, H, MLA_V)[:, :L]


def mla_mixer(h, w_in, q_latent_norm, w_q_up, w_out, k_nope, k_rope, v, cos, sin):
    B, L, _ = h.shape
    proj = h @ w_in
    c_q = rmsnorm(proj[..., :MLA_Q_RANK], q_latent_norm)
    z = proj[..., MLA_Q_RANK:]
    q = (c_q @ w_q_up).reshape(B, L, MLA_HEADS, MLA_QK)
    q_nope = q[..., :MLA_NOPE]
    q_rope = apply_rope(q[..., MLA_NOPE:], cos[:, None, :], sin[:, None, :])
    o = causal_block_attention(q_nope, q_rope, k_nope, k_rope, v).reshape(B, L, MLA_V_W)
    return (o * jax.nn.silu(z)) @ w_out


def _fwd_setup_inputs(seed: int = 0) -> dict:
    key = jax.random.key(seed)
    ks = jax.random.split(key, 20)
    nrm = lambda k, shape, s: jax.random.normal(k, shape, jnp.float32) * s
    gain = lambda k, shape: 1.0 + 0.02 * jax.random.normal(k, shape, jnp.float32)
    dt = jnp.exp(jax.random.uniform(ks[5], (N_A_LAYERS, GDN_V_HEADS), jnp.float32,
                                    math.log(1e-3), math.log(1e-1)))
    return {
        "x": nrm(ks[0], (BATCH, SEQ, D_MODEL), 1.0),
        "meta_tokens": nrm(ks[1], (N_META, D_MODEL), 1.0),
        "pre_norm": gain(ks[2], (DEPTH, D_MODEL)),
        "post_norm": gain(ks[3], (DEPTH, D_MODEL)),
        "gdn_w_in": nrm(ks[4], (N_A_LAYERS, D_MODEL, GDN_IN_W), D_MODEL ** -0.5),
        "gdn_conv_w": nrm(ks[6], (N_A_LAYERS, GDN_CONV, GDN_CONV_W), GDN_CONV ** -0.5),
        "gdn_a_log": jnp.log(jax.random.uniform(ks[7], (N_A_LAYERS, GDN_V_HEADS), jnp.float32, 1.0, 16.0)),
        "gdn_dt_bias": dt + jnp.log(-jnp.expm1(-dt)),
        "gdn_out_norm": gain(ks[8], (N_A_LAYERS, GDN_DV)),
        "gdn_w_out": nrm(ks[9], (N_A_LAYERS, GDN_V_W, D_MODEL), GDN_V_W ** -0.5),
        "kv_norm": gain(ks[10], (D_MODEL,)),
        "kv_w_down": nrm(ks[11], (D_MODEL, MLA_KV_RANK + MLA_ROPE), D_MODEL ** -0.5),
        "kv_latent_norm": gain(ks[12], (MLA_KV_RANK,)),
        "kv_w_up": nrm(ks[13], (MLA_KV_RANK, MLA_HEADS * (MLA_NOPE + MLA_V)), MLA_KV_RANK ** -0.5),
        "mla_w_in": nrm(ks[14], (N_B_LAYERS, D_MODEL, MLA_IN_W), D_MODEL ** -0.5),
        "mla_q_latent_norm": gain(ks[15], (N_B_LAYERS, MLA_Q_RANK)),
        "mla_w_q_up": nrm(ks[16], (N_B_LAYERS, MLA_Q_RANK, MLA_HEADS * MLA_QK), MLA_Q_RANK ** -0.5),
        "mla_w_out": nrm(ks[17], (N_B_LAYERS, MLA_V_W, D_MODEL), MLA_V_W ** -0.5),
    }


def _fwd_reference(x, meta_tokens, pre_norm, post_norm, gdn_w_in, gdn_conv_w, gdn_a_log, gdn_dt_bias,
              gdn_out_norm, gdn_w_out, kv_norm, kv_w_down, kv_latent_norm, kv_w_up,
              mla_w_in, mla_q_latent_norm, mla_w_q_up, mla_w_out):
    B = x.shape[0]
    meta = jnp.broadcast_to(meta_tokens[None].astype(x.dtype), (B, N_META, D_MODEL))
    h = jnp.concatenate([meta, x], axis=1)
    cos, sin = rope_tables(h.shape[1])
    shared_kv = None
    for layer in range(DEPTH):
        hn = rmsnorm(h, pre_norm[layer])
        if layer < N_A_LAYERS:
            y = gdn_mixer(hn, gdn_w_in[layer], gdn_conv_w[layer], gdn_a_log[layer], gdn_dt_bias[layer],
                          gdn_out_norm[layer], gdn_w_out[layer])
        else:
            if layer == N_A_LAYERS:
                shared_kv = mla_shared_kv(h, kv_norm, kv_w_down, kv_latent_norm, kv_w_up, cos, sin)
            j = layer - N_A_LAYERS
            k_nope, k_rope, v = shared_kv
            y = mla_mixer(hn, mla_w_in[j], mla_q_latent_norm[j], mla_w_q_up[j], mla_w_out[j],
                          k_nope, k_rope, v, cos, sin)
        h = h + rmsnorm(y, post_norm[layer])
    return h[:, N_META:]


import jax as _jax
import jax.numpy as _jnp

TWIN_FORMAT = 'train_step'
FWD_PARAMS = ['x', 'meta_tokens', 'pre_norm', 'post_norm', 'gdn_w_in', 'gdn_conv_w', 'gdn_a_log', 'gdn_dt_bias', 'gdn_out_norm', 'gdn_w_out', 'kv_norm', 'kv_w_down', 'kv_latent_norm', 'kv_w_up', 'mla_w_in', 'mla_q_latent_norm', 'mla_w_q_up', 'mla_w_out']
TWIN_WEIGHTS = ['meta_tokens', 'pre_norm', 'post_norm', 'gdn_w_in', 'gdn_conv_w', 'gdn_a_log', 'gdn_dt_bias', 'gdn_out_norm', 'gdn_w_out', 'kv_norm', 'kv_w_down', 'kv_latent_norm', 'kv_w_up', 'mla_w_in', 'mla_q_latent_norm', 'mla_w_q_up', 'mla_w_out']
TWIN_DIFF_INPUT = 'x'
TWIN_INPUTS = ['x', 'meta_tokens', 'pre_norm', 'post_norm', 'gdn_w_in', 'gdn_conv_w', 'gdn_a_log', 'gdn_dt_bias', 'gdn_out_norm', 'gdn_w_out', 'kv_norm', 'kv_w_down', 'kv_latent_norm', 'kv_w_up', 'mla_w_in', 'mla_q_latent_norm', 'mla_w_q_up', 'mla_w_out', 'loss_target', 'm_meta_tokens', 'm_pre_norm', 'm_post_norm', 'm_gdn_w_in', 'm_gdn_conv_w', 'm_gdn_a_log', 'm_gdn_dt_bias', 'm_gdn_out_norm', 'm_gdn_w_out', 'm_kv_norm', 'm_kv_w_down', 'm_kv_latent_norm', 'm_kv_w_up', 'm_mla_w_in', 'm_mla_q_latent_norm', 'm_mla_w_q_up', 'm_mla_w_out', 'v_meta_tokens', 'v_pre_norm', 'v_post_norm', 'v_gdn_w_in', 'v_gdn_conv_w', 'v_gdn_a_log', 'v_gdn_dt_bias', 'v_gdn_out_norm', 'v_gdn_w_out', 'v_kv_norm', 'v_kv_w_down', 'v_kv_latent_norm', 'v_kv_w_up', 'v_mla_w_in', 'v_mla_q_latent_norm', 'v_mla_w_q_up', 'v_mla_w_out']
TWIN_OUTPUTS = ['loss', 'grad_x', 'grad_meta_tokens', 'grad_pre_norm', 'grad_post_norm', 'grad_gdn_w_in', 'grad_gdn_conv_w', 'grad_gdn_a_log', 'grad_gdn_dt_bias', 'grad_gdn_out_norm', 'grad_gdn_w_out', 'grad_kv_norm', 'grad_kv_w_down', 'grad_kv_latent_norm', 'grad_kv_w_up', 'grad_mla_w_in', 'grad_mla_q_latent_norm', 'grad_mla_w_q_up', 'grad_mla_w_out', 'delta_meta_tokens', 'delta_pre_norm', 'delta_post_norm', 'delta_gdn_w_in', 'delta_gdn_conv_w', 'delta_gdn_a_log', 'delta_gdn_dt_bias', 'delta_gdn_out_norm', 'delta_gdn_w_out', 'delta_kv_norm', 'delta_kv_w_down', 'delta_kv_latent_norm', 'delta_kv_w_up', 'delta_mla_w_in', 'delta_mla_q_latent_norm', 'delta_mla_w_q_up', 'delta_mla_w_out', 'new_m_meta_tokens', 'new_m_pre_norm', 'new_m_post_norm', 'new_m_gdn_w_in', 'new_m_gdn_conv_w', 'new_m_gdn_a_log', 'new_m_gdn_dt_bias', 'new_m_gdn_out_norm', 'new_m_gdn_w_out', 'new_m_kv_norm', 'new_m_kv_w_down', 'new_m_kv_latent_norm', 'new_m_kv_w_up', 'new_m_mla_w_in', 'new_m_mla_q_latent_norm', 'new_m_mla_w_q_up', 'new_m_mla_w_out', 'new_v_meta_tokens', 'new_v_pre_norm', 'new_v_post_norm', 'new_v_gdn_w_in', 'new_v_gdn_conv_w', 'new_v_gdn_a_log', 'new_v_gdn_dt_bias', 'new_v_gdn_out_norm', 'new_v_gdn_w_out', 'new_v_kv_norm', 'new_v_kv_w_down', 'new_v_kv_latent_norm', 'new_v_kv_w_up', 'new_v_mla_w_in', 'new_v_mla_q_latent_norm', 'new_v_mla_w_q_up', 'new_v_mla_w_out']
TWIN_LEAF_KINDS = {'loss': 'loss', 'grad_x': 'grad_x', 'grad_meta_tokens': 'grad_w', 'grad_pre_norm': 'grad_w', 'grad_post_norm': 'grad_w', 'grad_gdn_w_in': 'grad_w', 'grad_gdn_conv_w': 'grad_w', 'grad_gdn_a_log': 'grad_w', 'grad_gdn_dt_bias': 'grad_w', 'grad_gdn_out_norm': 'grad_w', 'grad_gdn_w_out': 'grad_w', 'grad_kv_norm': 'grad_w', 'grad_kv_w_down': 'grad_w', 'grad_kv_latent_norm': 'grad_w', 'grad_kv_w_up': 'grad_w', 'grad_mla_w_in': 'grad_w', 'grad_mla_q_latent_norm': 'grad_w', 'grad_mla_w_q_up': 'grad_w', 'grad_mla_w_out': 'grad_w', 'delta_meta_tokens': 'delta_w', 'delta_pre_norm': 'delta_w', 'delta_post_norm': 'delta_w', 'delta_gdn_w_in': 'delta_w', 'delta_gdn_conv_w': 'delta_w', 'delta_gdn_a_log': 'delta_w', 'delta_gdn_dt_bias': 'delta_w', 'delta_gdn_out_norm': 'delta_w', 'delta_gdn_w_out': 'delta_w', 'delta_kv_norm': 'delta_w', 'delta_kv_w_down': 'delta_w', 'delta_kv_latent_norm': 'delta_w', 'delta_kv_w_up': 'delta_w', 'delta_mla_w_in': 'delta_w', 'delta_mla_q_latent_norm': 'delta_w', 'delta_mla_w_q_up': 'delta_w', 'delta_mla_w_out': 'delta_w', 'new_m_meta_tokens': 'new_m', 'new_m_pre_norm': 'new_m', 'new_m_post_norm': 'new_m', 'new_m_gdn_w_in': 'new_m', 'new_m_gdn_conv_w': 'new_m', 'new_m_gdn_a_log': 'new_m', 'new_m_gdn_dt_bias': 'new_m', 'new_m_gdn_out_norm': 'new_m', 'new_m_gdn_w_out': 'new_m', 'new_m_kv_norm': 'new_m', 'new_m_kv_w_down': 'new_m', 'new_m_kv_latent_norm': 'new_m', 'new_m_kv_w_up': 'new_m', 'new_m_mla_w_in': 'new_m', 'new_m_mla_q_latent_norm': 'new_m', 'new_m_mla_w_q_up': 'new_m', 'new_m_mla_w_out': 'new_m', 'new_v_meta_tokens': 'new_v', 'new_v_pre_norm': 'new_v', 'new_v_post_norm': 'new_v', 'new_v_gdn_w_in': 'new_v', 'new_v_gdn_conv_w': 'new_v', 'new_v_gdn_a_log': 'new_v', 'new_v_gdn_dt_bias': 'new_v', 'new_v_gdn_out_norm': 'new_v', 'new_v_gdn_w_out': 'new_v', 'new_v_kv_norm': 'new_v', 'new_v_kv_w_down': 'new_v', 'new_v_kv_latent_norm': 'new_v', 'new_v_kv_w_up': 'new_v', 'new_v_mla_w_in': 'new_v', 'new_v_mla_q_latent_norm': 'new_v', 'new_v_mla_w_q_up': 'new_v', 'new_v_mla_w_out': 'new_v'}


def _forward(args):
    return _fwd_reference(*[args[k] for k in FWD_PARAMS])


def _output_shape():
    def fwd():
        inp = _fwd_setup_inputs(0)
        return _fwd_reference(*[inp[k] for k in FWD_PARAMS])
    out = _jax.eval_shape(fwd)
    return out.shape, out.dtype

N_MICROBATCH = 1
ADAM_LR = 0.001
ADAM_B1 = 0.9
ADAM_B2 = 0.999
ADAM_EPS = 1e-08
ADAM_WD = 0.01
ADAM_STEP = 10
PER_EXAMPLE_BATCH_AXIS = {'x': 0, 'loss_target': 0}
SHARED_INPUTS = []
_WEIGHT_DTYPES = {'meta_tokens': _jnp.float32, 'pre_norm': _jnp.float32, 'post_norm': _jnp.float32, 'gdn_w_in': _jnp.float32, 'gdn_conv_w': _jnp.float32, 'gdn_a_log': _jnp.float32, 'gdn_dt_bias': _jnp.float32, 'gdn_out_norm': _jnp.float32, 'gdn_w_out': _jnp.float32, 'kv_norm': _jnp.float32, 'kv_w_down': _jnp.float32, 'kv_latent_norm': _jnp.float32, 'kv_w_up': _jnp.float32, 'mla_w_in': _jnp.float32, 'mla_q_latent_norm': _jnp.float32, 'mla_w_q_up': _jnp.float32, 'mla_w_out': _jnp.float32}
MOMENT_SCALE = {'meta_tokens': 9.827744e-02, 'pre_norm': 1.441085e+00, 'post_norm': 6.458684e+01, 'gdn_w_in': 4.835545e-01, 'gdn_conv_w': 1.734948e+00, 'gdn_a_log': 3.852168e+00, 'gdn_dt_bias': 3.660974e+00, 'gdn_out_norm': 1.707232e+01, 'gdn_w_out': 6.018674e+00, 'kv_norm': 3.964642e+00, 'kv_w_down': 8.925948e+00, 'kv_latent_norm': 1.042892e+01, 'kv_w_up': 2.338469e+00, 'mla_w_in': 1.428959e+00, 'mla_q_latent_norm': 6.544326e-01, 'mla_w_q_up': 2.053778e-01, 'mla_w_out': 4.441948e+00}


def _to_microbatches(a, axis):
    t = _jnp.moveaxis(a, axis, 0)
    t = t.reshape((N_MICROBATCH, t.shape[0] // N_MICROBATCH) + t.shape[1:])
    return _jnp.moveaxis(t, 1, axis + 1)


def setup_inputs(seed: int = 0) -> dict:
    inp = _fwd_setup_inputs(seed)
    key = _jax.random.fold_in(_jax.random.key(seed), 7919)
    shape, _ = _output_shape()
    out = dict(inp)
    out["loss_target"] = _jax.random.normal(_jax.random.fold_in(key, 0), shape, _jnp.float32)
    for i, name in enumerate(TWIN_WEIGHTS):
        w = inp[name].astype(_jnp.float32)
        if MOMENT_SCALE is None:
            s = _jnp.sqrt(_jnp.mean(_jnp.square(w)) + 1e-30)
        else:
            s = MOMENT_SCALE[name]
        km, kv = _jax.random.split(_jax.random.fold_in(key, i + 1))
        out[name] = w
        out["m_" + name] = s * _jax.random.normal(km, w.shape, _jnp.float32)
        out["v_" + name] = (s * s) * _jax.random.uniform(kv, w.shape, _jnp.float32, 0.5, 1.5)
    if N_MICROBATCH > 1:
        for name, axis in PER_EXAMPLE_BATCH_AXIS.items():
            out[name] = _to_microbatches(out[name], axis)
    return {'x': out['x'], 'meta_tokens': out['meta_tokens'], 'pre_norm': out['pre_norm'], 'post_norm': out['post_norm'], 'gdn_w_in': out['gdn_w_in'], 'gdn_conv_w': out['gdn_conv_w'], 'gdn_a_log': out['gdn_a_log'], 'gdn_dt_bias': out['gdn_dt_bias'], 'gdn_out_norm': out['gdn_out_norm'], 'gdn_w_out': out['gdn_w_out'], 'kv_norm': out['kv_norm'], 'kv_w_down': out['kv_w_down'], 'kv_latent_norm': out['kv_latent_norm'], 'kv_w_up': out['kv_w_up'], 'mla_w_in': out['mla_w_in'], 'mla_q_latent_norm': out['mla_q_latent_norm'], 'mla_w_q_up': out['mla_w_q_up'], 'mla_w_out': out['mla_w_out'], 'loss_target': out['loss_target'], 'm_meta_tokens': out['m_meta_tokens'], 'm_pre_norm': out['m_pre_norm'], 'm_post_norm': out['m_post_norm'], 'm_gdn_w_in': out['m_gdn_w_in'], 'm_gdn_conv_w': out['m_gdn_conv_w'], 'm_gdn_a_log': out['m_gdn_a_log'], 'm_gdn_dt_bias': out['m_gdn_dt_bias'], 'm_gdn_out_norm': out['m_gdn_out_norm'], 'm_gdn_w_out': out['m_gdn_w_out'], 'm_kv_norm': out['m_kv_norm'], 'm_kv_w_down': out['m_kv_w_down'], 'm_kv_latent_norm': out['m_kv_latent_norm'], 'm_kv_w_up': out['m_kv_w_up'], 'm_mla_w_in': out['m_mla_w_in'], 'm_mla_q_latent_norm': out['m_mla_q_latent_norm'], 'm_mla_w_q_up': out['m_mla_w_q_up'], 'm_mla_w_out': out['m_mla_w_out'], 'v_meta_tokens': out['v_meta_tokens'], 'v_pre_norm': out['v_pre_norm'], 'v_post_norm': out['v_post_norm'], 'v_gdn_w_in': out['v_gdn_w_in'], 'v_gdn_conv_w': out['v_gdn_conv_w'], 'v_gdn_a_log': out['v_gdn_a_log'], 'v_gdn_dt_bias': out['v_gdn_dt_bias'], 'v_gdn_out_norm': out['v_gdn_out_norm'], 'v_gdn_w_out': out['v_gdn_w_out'], 'v_kv_norm': out['v_kv_norm'], 'v_kv_w_down': out['v_kv_w_down'], 'v_kv_latent_norm': out['v_kv_latent_norm'], 'v_kv_w_up': out['v_kv_w_up'], 'v_mla_w_in': out['v_mla_w_in'], 'v_mla_q_latent_norm': out['v_mla_q_latent_norm'], 'v_mla_w_q_up': out['v_mla_w_q_up'], 'v_mla_w_out': out['v_mla_w_out']}


def _loss(weights, diff, rest, loss_target):
    with _jax.named_scope("forward"):
        args = {**rest, TWIN_DIFF_INPUT: diff, **{k: w.astype(_WEIGHT_DTYPES[k]) for k, w in weights.items()}}
        y = _forward(args)
    with _jax.named_scope("loss_head"):
        err = _jnp.square(y.astype(_jnp.float32) - loss_target)
        return 0.5 * _jnp.sum(_jnp.mean(err, axis=-1)) if err.ndim else 0.5 * err


def _adamw(w, g, m, v):
    m = ADAM_B1 * m + (1.0 - ADAM_B1) * g
    v = ADAM_B2 * v + (1.0 - ADAM_B2) * _jnp.square(g)
    m_hat = m / (1.0 - ADAM_B1 ** ADAM_STEP)
    v_hat = v / (1.0 - ADAM_B2 ** ADAM_STEP)
    delta = -ADAM_LR * (m_hat / (_jnp.sqrt(v_hat) + ADAM_EPS) + ADAM_WD * w)
    return delta, m, v


def reference(x, meta_tokens, pre_norm, post_norm, gdn_w_in, gdn_conv_w, gdn_a_log, gdn_dt_bias, gdn_out_norm, gdn_w_out, kv_norm, kv_w_down, kv_latent_norm, kv_w_up, mla_w_in, mla_q_latent_norm, mla_w_q_up, mla_w_out, loss_target, m_meta_tokens, m_pre_norm, m_post_norm, m_gdn_w_in, m_gdn_conv_w, m_gdn_a_log, m_gdn_dt_bias, m_gdn_out_norm, m_gdn_w_out, m_kv_norm, m_kv_w_down, m_kv_latent_norm, m_kv_w_up, m_mla_w_in, m_mla_q_latent_norm, m_mla_w_q_up, m_mla_w_out, v_meta_tokens, v_pre_norm, v_post_norm, v_gdn_w_in, v_gdn_conv_w, v_gdn_a_log, v_gdn_dt_bias, v_gdn_out_norm, v_gdn_w_out, v_kv_norm, v_kv_w_down, v_kv_latent_norm, v_kv_w_up, v_mla_w_in, v_mla_q_latent_norm, v_mla_w_q_up, v_mla_w_out):
    given = dict(x=x, meta_tokens=meta_tokens, pre_norm=pre_norm, post_norm=post_norm, gdn_w_in=gdn_w_in, gdn_conv_w=gdn_conv_w, gdn_a_log=gdn_a_log, gdn_dt_bias=gdn_dt_bias, gdn_out_norm=gdn_out_norm, gdn_w_out=gdn_w_out, kv_norm=kv_norm, kv_w_down=kv_w_down, kv_latent_norm=kv_latent_norm, kv_w_up=kv_w_up, mla_w_in=mla_w_in, mla_q_latent_norm=mla_q_latent_norm, mla_w_q_up=mla_w_q_up, mla_w_out=mla_w_out, loss_target=loss_target, m_meta_tokens=m_meta_tokens, m_pre_norm=m_pre_norm, m_post_norm=m_post_norm, m_gdn_w_in=m_gdn_w_in, m_gdn_conv_w=m_gdn_conv_w, m_gdn_a_log=m_gdn_a_log, m_gdn_dt_bias=m_gdn_dt_bias, m_gdn_out_norm=m_gdn_out_norm, m_gdn_w_out=m_gdn_w_out, m_kv_norm=m_kv_norm, m_kv_w_down=m_kv_w_down, m_kv_latent_norm=m_kv_latent_norm, m_kv_w_up=m_kv_w_up, m_mla_w_in=m_mla_w_in, m_mla_q_latent_norm=m_mla_q_latent_norm, m_mla_w_q_up=m_mla_w_q_up, m_mla_w_out=m_mla_w_out, v_meta_tokens=v_meta_tokens, v_pre_norm=v_pre_norm, v_post_norm=v_post_norm, v_gdn_w_in=v_gdn_w_in, v_gdn_conv_w=v_gdn_conv_w, v_gdn_a_log=v_gdn_a_log, v_gdn_dt_bias=v_gdn_dt_bias, v_gdn_out_norm=v_gdn_out_norm, v_gdn_w_out=v_gdn_w_out, v_kv_norm=v_kv_norm, v_kv_w_down=v_kv_w_down, v_kv_latent_norm=v_kv_latent_norm, v_kv_w_up=v_kv_w_up, v_mla_w_in=v_mla_w_in, v_mla_q_latent_norm=v_mla_q_latent_norm, v_mla_w_q_up=v_mla_w_q_up, v_mla_w_out=v_mla_w_out)
    weights = {n: given[n] for n in TWIN_WEIGHTS}
    shared = {n: given[n] for n in SHARED_INPUTS}
    per_example = {n: given[n] for n in ['x']}
    grad_fn = _jax.value_and_grad(_loss, argnums=(0, 1))

    def one_microbatch(ex, loss_target):
        ex = dict(ex)
        diff = ex.pop(TWIN_DIFF_INPUT)
        return grad_fn(weights, diff, {**shared, **ex}, loss_target)

    if N_MICROBATCH == 1:
        loss, (grad_w, grad_x) = one_microbatch(per_example, given["loss_target"])
    else:
        def body(carry, xs):
            loss_sum, grad_sum = carry
            l_k, (gw_k, gx_k) = one_microbatch(xs[0], xs[1])
            with _jax.named_scope("update"):
                return (loss_sum + l_k, _jax.tree.map(_jnp.add, grad_sum, gw_k)), gx_k

        init = (_jnp.zeros((), _jnp.float32), _jax.tree.map(_jnp.zeros_like, weights))
        (loss, grad_w), grad_x = _jax.lax.scan(body, init, (per_example, given["loss_target"]))
    with _jax.named_scope("update"):
        delta_w, new_m, new_v = {}, {}, {}
        for n in TWIN_WEIGHTS:
            delta_w[n], new_m[n], new_v[n] = _adamw(weights[n], grad_w[n], given["m_" + n], given["v_" + n])
    return (loss, grad_x, *[grad_w[n] for n in TWIN_WEIGHTS], *[delta_w[n] for n in TWIN_WEIGHTS],
            *[new_m[n] for n in TWIN_WEIGHTS], *[new_v[n] for n in TWIN_WEIGHTS])
```

```python
import functools
import math

import jax
import jax.numpy as jnp
from jax import lax
from jax.experimental import pallas as pl
from jax.experimental.pallas import tpu as pltpu

F32 = jnp.float32
BF16 = jnp.bfloat16
MESH = pl.DeviceIdType.MESH

D_MODEL = 1024
N_META = 16
FRONT = 48
ROW0 = FRONT + N_META
ROW_ALIGN = 640
NORM_EPS = 1e-6
LANE = 128

GDN_QK_HEADS = 8
GDN_V_HEADS = 16
GDN_DK = 128
GDN_CHUNK = 64
GDN_QK_W = 1024
GDN_V_W = 2048
GDN_CONV_W = 4096

MLA_HEADS = 16
MLA_NOPE = 128
MLA_ROPE = 64
MLA_QK = 192
MLA_Q_RANK = 256
MLA_KV_RANK = 128
ROPE_THETA = 10000.0

ADAM_LR = 0.001
ADAM_B1 = 0.9
ADAM_B2 = 0.999
ADAM_EPS = 1e-08
ADAM_WD = 0.01
ADAM_STEP = 10

VMEM_LIMIT_V7X = 56 * 1024 * 1024
NEG = -1e30

_NN = ((1,), (0,))
_NT = ((1,), (1,))
_TN = ((0,), (0,))
_HI = lax.Precision.HIGHEST


def _pcall(body, **kw):
    return pl.pallas_call(body, **kw)


def _params(n_axes):
    return pltpu.CompilerParams(dimension_semantics=("arbitrary",) * n_axes, vmem_limit_bytes=VMEM_LIMIT_V7X)


def _dot(a, b, dims, prec=None):
    return lax.dot_general(a, b, (dims, ((), ())), precision=prec, preferred_element_type=F32)


def _bdot(a, b, dims):
    return _dot(a.astype(BF16), b.astype(BF16), dims)


def _hdot(a, b, dims=_NN):
    return _dot(a, b, dims, _HI)


def _fdot(a, b, dims):
    return _dot(a, b, dims)


def _tile(n):
    for t in (640, 512, 256, 128):
        if n % t == 0:
            return t
    raise ValueError(n)


def _mm(a, b, mode, name, out_dtype=F32):
    if mode == "nn":
        (m, k), (k2, n) = a.shape, b.shape
    elif mode == "nt":
        (m, k), (n, k2) = a.shape, b.shape
    else:
        (k, m), (k2, n) = a.shape, b.shape
    assert k == k2, (a.shape, b.shape, mode)
    tm, tn, tk = _tile(m), _tile(n), _tile(k)
    nk = k // tk
    dims = {"nn": _NN, "nt": _NT, "tn": _TN}[mode]

    def body(a_ref, b_ref, o_ref, acc):
        kk = pl.program_id(2)

        @pl.when(kk == 0)
        def _():
            acc[...] = jnp.zeros_like(acc)

        acc[...] += _bdot(a_ref[...], b_ref[...], dims)

        @pl.when(kk == nk - 1)
        def _():
            o_ref[...] = acc[...].astype(out_dtype)

    if mode == "tn":
        a_spec = pl.BlockSpec((tk, tm), lambda i, j, kk: (kk, i))
    else:
        a_spec = pl.BlockSpec((tm, tk), lambda i, j, kk: (i, kk))
    if mode == "nt":
        b_spec = pl.BlockSpec((tn, tk), lambda i, j, kk: (j, kk))
    else:
        b_spec = pl.BlockSpec((tk, tn), lambda i, j, kk: (kk, j))
    return _pcall(
        body, name=name, grid=(m // tm, n // tn, nk),
        in_specs=[a_spec, b_spec],
        out_specs=pl.BlockSpec((tm, tn), lambda i, j, kk: (i, j)),
        out_shape=jax.ShapeDtypeStruct((m, n), out_dtype),
        scratch_shapes=[pltpu.VMEM((tm, tn), F32)],
        compiler_params=_params(3),
    )(a, b)


class _In:
    def __init__(self, arr, kind="row", grouped=False, goff=0):
        self.arr, self.kind, self.grouped, self.goff = arr, kind, grouped, goff


class _Out:
    def __init__(self, kind, shape, dtype=F32, grouped=False):
        self.kind, self.shape, self.dtype, self.grouped = kind, shape, dtype, grouped


def _rowwise(name, fn, ins, outs, *, groups=1, tr=320):
    lp = next(i.arr.shape[0] for i in ins if i.kind == "row")
    nr = lp // tr
    assert lp % tr == 0

    def in_spec(i):
        w = i.arr.shape[1]
        if i.kind == "row":
            if i.grouped:
                return pl.BlockSpec((tr, LANE), lambda g, r, o=i.goff: (r, g + o))
            return pl.BlockSpec((tr, w), lambda g, r: (r, 0))
        if i.grouped:
            return pl.BlockSpec((i.arr.shape[0], LANE), lambda g, r, o=i.goff: (0, g + o))
        return pl.BlockSpec(i.arr.shape, lambda g, r: (0, 0))

    def out_spec(o):
        if o.kind == "row":
            if o.grouped:
                return pl.BlockSpec((tr, LANE), lambda g, r: (r, g))
            assert groups == 1
            return pl.BlockSpec((tr, o.shape[1]), lambda g, r: (r, 0))
        if o.grouped:
            return pl.BlockSpec((o.shape[0], LANE), lambda g, r: (0, g))
        return pl.BlockSpec(o.shape, lambda g, r: (0, 0))

    n_in = len(ins)

    def body(*refs):
        g = pl.program_id(0)
        r = pl.program_id(1)
        ridx = r * tr + lax.broadcasted_iota(jnp.int32, (tr, 1), 0)
        res = fn(ridx, g, *[ref[...] for ref in refs[:n_in]])
        assert len(res) == len(outs), (name, len(res), len(outs))
        for o, ref, val in zip(outs, refs[n_in:], res):
            if o.kind == "row":
                ref[...] = val.astype(o.dtype)
            else:
                first = (r == 0) if o.grouped else jnp.logical_and(r == 0, g == 0)

                @pl.when(first)
                def _(ref=ref, val=val):
                    ref[...] = val.astype(F32)

                @pl.when(jnp.logical_not(first))
                def _(ref=ref, val=val):
                    ref[...] += val.astype(F32)

    res = _pcall(
        body, name=name, grid=(groups, nr),
        in_specs=[in_spec(i) for i in ins],
        out_specs=[out_spec(o) for o in outs],
        out_shape=[jax.ShapeDtypeStruct(o.shape, o.dtype) for o in outs],
        compiler_params=_params(2),
    )(*[i.arr for i in ins])
    return res


def _rowwise_vjp(name, fn, ins, cots, diff, *, groups=1, tr=320):
    n_in = len(ins)
    grouped = groups > 1
    cot_ins = []
    counts = []
    for arrs in cots:
        counts.append(len(arrs))
        for a in arrs:
            cot_ins.append(_In(a, "row", grouped=grouped and a.shape[1] > LANE))
    lp = next(i.arr.shape[0] for i in ins if i.kind == "row")
    outs = []
    for d in diff:
        i = ins[d]
        if i.kind == "row":
            w = groups * LANE if i.grouped else i.arr.shape[1]
            outs.append(_Out("row", (lp, w), F32, grouped=i.grouped))
        else:
            outs.append(_Out("acc", i.arr.shape, F32, grouped=i.grouped))

    def bfn(ridx, g, *allvals):
        vals = list(allvals[:n_in])
        cvals = allvals[n_in:]

        def f(*dv):
            full = list(vals)
            for i, v in zip(diff, dv):
                full[i] = v
            return tuple(fn(ridx, g, *full))

        primal, vjp = jax.vjp(f, *[vals[i].astype(F32) for i in diff])
        cts = []
        pos = 0
        for k, cnt in enumerate(counts):
            if cnt == 0:
                cts.append(jnp.zeros_like(primal[k]))
            else:
                c = cvals[pos].astype(F32)
                for extra in cvals[pos + 1:pos + cnt]:
                    c = c + extra.astype(F32)
                cts.append(c.astype(primal[k].dtype))
            pos += cnt
        return vjp(tuple(cts))

    return _rowwise(name, bfn, list(ins) + cot_ins, outs, groups=groups, tr=tr)


def _rms(x, g):
    return x * lax.rsqrt(jnp.mean(x * x, axis=-1, keepdims=True) + NORM_EPS) * g


def _silu(x):
    return x * jax.nn.sigmoid(x)


def _softplus(x):
    return jnp.maximum(x, 0.0) + jnp.log(1.0 + jnp.exp(-jnp.abs(x)))


def _swap_halves(x):
    lane = lax.broadcasted_iota(jnp.int32, x.shape, x.ndim - 1)
    return jnp.where(lane < 32, pltpu.roll(x, LANE - 32, x.ndim - 1), pltpu.roll(x, 32, x.ndim - 1))


@jax.custom_vjp
def _rope(x, c, s):
    return x * c + _swap_halves(x) * s


def _rope_fwd(x, c, s):
    return _rope(x, c, s), (c, s)


def _rope_bwd(res, dy):
    c, s = res
    return dy * c + _swap_halves(dy * s), jnp.zeros_like(c), jnp.zeros_like(s)


_rope.defvjp(_rope_fwd, _rope_bwd)


def _conv_post(c, g):
    s = _silu(c)
    n = s * lax.rsqrt(jnp.sum(s * s, axis=-1, keepdims=True) + NORM_EPS)
    return jnp.where(g < GDN_QK_HEADS, n * (GDN_DK ** -0.5), jnp.where(g < 2 * GDN_QK_HEADS, n, s))


def _conv_taps(xe, w):
    c = xe[8:] * w[3]
    for s in (1, 2, 3):
        c = c + pltpu.roll(xe, s, 0)[8:] * w[3 - s]
    return c


def _conv_fwd(x, w, name, tr=640):
    lp, width = x.shape
    groups = width // LANE
    nr = lp // tr

    def body(x_ref, prev_ref, w_ref, o_ref):
        g = pl.program_id(0)
        r = pl.program_id(1)
        prev = jnp.where(r > 0, prev_ref[...], 0.0)
        xe = jnp.concatenate([prev, x_ref[...]], axis=0)
        o_ref[...] = _conv_post(_conv_taps(xe, [w_ref[t:t + 1, :] for t in range(4)]), g)

    return _pcall(
        body, name=name, grid=(groups, nr),
        in_specs=[pl.BlockSpec((tr, LANE), lambda g, r: (r, g)),
                  pl.BlockSpec((8, LANE), lambda g, r: (jnp.maximum(r * (tr // 8) - 1, 0), g)),
                  pl.BlockSpec((4, LANE), lambda g, r: (0, g))],
        out_specs=pl.BlockSpec((tr, LANE), lambda g, r: (r, g)),
        out_shape=jax.ShapeDtypeStruct((lp, width), F32),
        compiler_params=_params(2),
    )(x, x, w)


def _conv_bwd(x, w, dq2, dk2, dv, name, tr=640):
    lp, width = x.shape
    groups = width // LANE
    nr = lp // tr
    last8 = lp // 8 - 1
    nqk = GDN_QK_HEADS

    def body(x_ref, prev_ref, next_ref, w_ref,
             qe_ref, qo_ref, ke_ref, ko_ref, v_ref,
             qe_n, qo_n, ke_n, ko_n, v_n, dx_ref, dw_ref):
        g = pl.program_id(0)
        r = pl.program_id(1)
        w = [w_ref[t:t + 1, :] for t in range(4)]
        not_last = r < nr - 1

        def pick(a, b, c, d, e):
            return jnp.where(g < nqk, a[...] + b[...], jnp.where(g < 2 * nqk, c[...] + d[...], e[...]))

        dy = pick(qe_ref, qo_ref, ke_ref, ko_ref, v_ref)
        dyn = jnp.where(not_last, pick(qe_n, qo_n, ke_n, ko_n, v_n), 0.0)
        prev = jnp.where(r > 0, prev_ref[...], 0.0)
        nxt = jnp.where(not_last, next_ref[...], 0.0)
        xe = jnp.concatenate([prev, x_ref[...], nxt], axis=0)
        ce = _conv_taps(xe, w)
        _, vjp = jax.vjp(lambda c: _conv_post(c, g), ce)
        (dce,) = vjp(jnp.concatenate([dy, dyn], axis=0))
        n = tr + 8
        dx = dce * w[3]
        for s in (1, 2, 3):
            dx = dx + pltpu.roll(dce, n - s, 0) * w[3 - s]
        dx_ref[...] = dx[:tr]
        dc = dce[:tr]
        row4 = lax.broadcasted_iota(jnp.int32, (4, LANE), 0)
        dw = jnp.zeros((4, LANE), F32)
        for s in (0, 1, 2, 3):
            xs = xe[8:8 + tr] if s == 0 else pltpu.roll(xe, s, 0)[8:8 + tr]
            dw = dw + jnp.where(row4 == 3 - s, jnp.sum(dc * xs, axis=0, keepdims=True), 0.0)

        @pl.when(r == 0)
        def _():
            dw_ref[...] = dw

        @pl.when(r > 0)
        def _():
            dw_ref[...] += dw

    def col_e(g):
        return jnp.where(g < nqk, 2 * g, jnp.where(g < 2 * nqk, 2 * (g - nqk), 0))

    def col_v(g):
        return jnp.maximum(g - 2 * nqk, 0)

    def blk(colf, odd=0):
        return pl.BlockSpec((tr, LANE), lambda g, r: (r, colf(g) + odd))

    def nblk(colf, odd=0):
        return pl.BlockSpec((8, LANE), lambda g, r: (jnp.minimum((r + 1) * (tr // 8), last8), colf(g) + odd))

    return _pcall(
        body, name=name, grid=(groups, nr),
        in_specs=[pl.BlockSpec((tr, LANE), lambda g, r: (r, g)),
                  pl.BlockSpec((8, LANE), lambda g, r: (jnp.maximum(r * (tr // 8) - 1, 0), g)),
                  pl.BlockSpec((8, LANE), lambda g, r: (jnp.minimum((r + 1) * (tr // 8), last8), g)),
                  pl.BlockSpec((4, LANE), lambda g, r: (0, g)),
                  blk(col_e), blk(col_e, 1), blk(col_e), blk(col_e, 1), blk(col_v),
                  nblk(col_e), nblk(col_e, 1), nblk(col_e), nblk(col_e, 1), nblk(col_v)],
        out_specs=[pl.BlockSpec((tr, LANE), lambda g, r: (r, g)),
                   pl.BlockSpec((4, LANE), lambda g, r: (0, g))],
        out_shape=[jax.ShapeDtypeStruct((lp, width), F32), jax.ShapeDtypeStruct((4, width), F32)],
        compiler_params=_params(2),
    )(x, x, x, w, dq2, dq2, dk2, dk2, dv, dq2, dq2, dk2, dk2, dv)


def _inv_impl(m):
    c = m.shape[0]
    ii = lax.broadcasted_iota(jnp.int32, (c, c), 0)
    jj = lax.broadcasted_iota(jnp.int32, (c, c), 1)
    eye = (ii == jj).astype(F32)

    def same_block(shift):
        return (ii >> shift) == (jj >> shift)

    n1 = jnp.where(same_block(3), -m, 0.0)
    n2 = _hdot(n1, n1)
    n4 = _hdot(n2, n2)
    d = _hdot(_hdot(eye + n1, eye + n2), eye + n4)
    shift = 3
    while (1 << shift) < c:
        low = jnp.where(jnp.logical_and(same_block(shift + 1), jnp.logical_not(same_block(shift))), m, 0.0)
        d = d - _hdot(d, _hdot(low, d))
        shift += 1
    return d


@jax.custom_vjp
def _inv_unit_lower(m):
    return _inv_impl(m)


def _inv_f(m):
    t = _inv_impl(m)
    return t, t


def _inv_b(t, dt):
    c = t.shape[0]
    ii = lax.broadcasted_iota(jnp.int32, (c, c), 0)
    jj = lax.broadcasted_iota(jnp.int32, (c, c), 1)
    gm = _hdot(t, _hdot(dt, t, _NT), _TN)
    return (jnp.where(ii > jj, -gm, 0.0),)


_inv_unit_lower.defvjp(_inv_f, _inv_b)


def _gdn_chunk(q, k, v, beta_blk, gc_blk, state, h):
    c = q.shape[0]
    lane = lax.broadcasted_iota(jnp.int32, (1, LANE), 1)
    onehot = (lane == h).astype(F32)
    beta = jnp.sum(beta_blk * onehot, axis=1, keepdims=True)
    gcc = jnp.sum(gc_blk * onehot, axis=1, keepdims=True)
    gcr = _hdot(jnp.broadcast_to(onehot, (8, LANE)), gc_blk, _NT)[0:1]
    ii = lax.broadcasted_iota(jnp.int32, (c, c), 0)
    jj = lax.broadcasted_iota(jnp.int32, (c, c), 1)
    incl = ii >= jj
    dec = jnp.where(incl, jnp.exp(jnp.where(incl, gcc - gcr, 0.0)), 0.0)
    kb = k * beta
    vb = v * beta
    eg = jnp.exp(gcc)
    m = _fdot(kb, k, _NT) * jnp.where(ii > jj, dec, 0.0)
    t = _inv_unit_lower(m)
    u = _hdot(t, vb)
    w = _hdot(t, kb * eg)
    attn = _fdot(q, k, _NT) * dec
    rows = lax.broadcasted_iota(jnp.int32, (c, 1), 0)
    gl = jnp.sum(jnp.where(rows == c - 1, gcc, 0.0), axis=0, keepdims=True)
    qd = q * eg
    kd = k * jnp.exp(gl - gcc)
    v_new = u - _fdot(w, state, _NN)
    o = _fdot(qd, state, _NN) + _fdot(attn, v_new, _NN)
    new_state = state * jnp.exp(gl) + _fdot(kd, v_new, _TN)
    return o, new_state


def _gdn_specs(nc, rev):
    def cidx(n):
        return (nc - 1 - n) if rev else n
    nqk = GDN_QK_HEADS
    c = GDN_CHUNK
    q_spec = pl.BlockSpec((c, LANE), lambda n, h: (cidx(n), h // 2))
    k_spec = pl.BlockSpec((c, LANE), lambda n, h: (cidx(n), nqk + h // 2))
    v_spec = pl.BlockSpec((c, LANE), lambda n, h: (cidx(n), 2 * nqk + h))
    s_spec = pl.BlockSpec((c, LANE), lambda n, h: (cidx(n), 0))
    o_spec = pl.BlockSpec((c, LANE), lambda n, h: (cidx(n), h))
    ck_spec = pl.BlockSpec((1, 1, GDN_DK, LANE), lambda n, h: (h, cidx(n), 0, 0))
    return q_spec, k_spec, v_spec, s_spec, o_spec, ck_spec


def _gdn_fwd(qkv, beta, gc, name):
    lp = qkv.shape[0]
    nc = lp // GDN_CHUNK
    nh = GDN_V_HEADS
    q_spec, k_spec, v_spec, s_spec, o_spec, ck_spec = _gdn_specs(nc, False)

    def body(q_ref, k_ref, v_ref, b_ref, g_ref, o_ref, ck_ref, state):
        n = pl.program_id(0)
        h = pl.program_id(1)

        @pl.when(n == 0)
        def _():
            state[h] = jnp.zeros((GDN_DK, LANE), F32)

        s = state[h]
        ck_ref[0, 0] = s
        o, s2 = _gdn_chunk(q_ref[...], k_ref[...], v_ref[...], b_ref[...], g_ref[...], s, h)
        o_ref[...] = o
        state[h] = s2

    return _pcall(
        body, name=name, grid=(nc, nh),
        in_specs=[q_spec, k_spec, v_spec, s_spec, s_spec],
        out_specs=[o_spec, ck_spec],
        out_shape=[jax.ShapeDtypeStruct((lp, GDN_V_W), F32),
                   jax.ShapeDtypeStruct((nh, nc, GDN_DK, LANE), F32)],
        scratch_shapes=[pltpu.VMEM((nh, GDN_DK, LANE), F32)],
        compiler_params=_params(2),
    )(qkv, qkv, qkv, beta, gc)


def _gdn_bwd(qkv, beta, gc, ckpt, do, name):
    lp = qkv.shape[0]
    nc = lp // GDN_CHUNK
    nh = GDN_V_HEADS
    q_spec, k_spec, v_spec, s_spec, o_spec, ck_spec = _gdn_specs(nc, True)

    def body(q_ref, k_ref, v_ref, b_ref, g_ref, ck_ref, do_ref,
             dq_ref, dk_ref, dv_ref, db_ref, dg_ref, dstate):
        n = pl.program_id(0)
        h = pl.program_id(1)

        @pl.when(n == 0)
        def _():
            dstate[h] = jnp.zeros((GDN_DK, LANE), F32)

        _, vjp = jax.vjp(lambda q, k, v, b, g, s: _gdn_chunk(q, k, v, b, g, s, h),
                         q_ref[...], k_ref[...], v_ref[...], b_ref[...], g_ref[...], ck_ref[0, 0])
        dq, dk, dv, db, dg, ds = vjp((do_ref[...], dstate[h]))
        dq_ref[...] = dq
        dk_ref[...] = dk
        dv_ref[...] = dv
        dstate[h] = ds

        @pl.when(h == 0)
        def _():
            db_ref[...] = db
            dg_ref[...] = dg

        @pl.when(h > 0)
        def _():
            db_ref[...] += db
            dg_ref[...] += dg

    big = jax.ShapeDtypeStruct((lp, GDN_V_W), F32)
    small = jax.ShapeDtypeStruct((lp, LANE), F32)
    return _pcall(
        body, name=name, grid=(nc, nh),
        in_specs=[q_spec, k_spec, v_spec, s_spec, s_spec, ck_spec, o_spec],
        out_specs=[o_spec, o_spec, o_spec, s_spec, s_spec],
        out_shape=[big, big, big, small, small],
        scratch_shapes=[pltpu.VMEM((nh, GDN_DK, LANE), F32)],
        compiler_params=_params(2),
    )(qkv, qkv, qkv, beta, gc, ckpt, do)


_ATT_SCALE = MLA_QK ** -0.5


def _att_scores(qn, qr, kn, kr, i, j, tb):
    s = (_bdot(qn, kn, _NT) + _bdot(qr, kr, _NT)) * _ATT_SCALE
    qpos = i * tb + lax.broadcasted_iota(jnp.int32, (tb, tb), 0)
    kpos = j * tb + lax.broadcasted_iota(jnp.int32, (tb, tb), 1)
    return jnp.where(jnp.logical_and(kpos <= qpos, kpos >= FRONT), s, NEG)


def _head_lane(blk, h):
    lane = lax.broadcasted_iota(jnp.int32, (1, LANE), 1)
    return jnp.sum(jnp.where(lane == h, blk, 0.0), axis=1, keepdims=True)


def _flash_fwd(qn, qr, kn, kr, v, name, tb=ROW_ALIGN):
    lp = qn.shape[0]
    nb = lp // tb
    nh = MLA_HEADS

    def body(qn_ref, qr_ref, kn_ref, kr_ref, v_ref, o_ref, lse_ref, m_s, l_s, acc):
        i, h, j = pl.program_id(0), pl.program_id(1), pl.program_id(2)

        @pl.when(j == 0)
        def _():
            m_s[...] = jnp.full_like(m_s, NEG)
            l_s[...] = jnp.zeros_like(l_s)
            acc[...] = jnp.zeros_like(acc)

        @pl.when(j <= i)
        def _():
            s = _att_scores(qn_ref[...], qr_ref[...], kn_ref[...], kr_ref[...], i, j, tb)
            m_new = jnp.maximum(m_s[...], jnp.max(s, axis=1, keepdims=True))
            alpha = jnp.exp(m_s[...] - m_new)
            p = jnp.exp(s - m_new)
            l_s[...] = alpha * l_s[...] + jnp.sum(p, axis=1, keepdims=True)
            acc[...] = alpha * acc[...] + _bdot(p, v_ref[...], _NN)
            m_s[...] = m_new

        @pl.when(j == nb - 1)
        def _():
            o_ref[...] = acc[...] / l_s[...]
            lane = lax.broadcasted_iota(jnp.int32, (1, LANE), 1)
            mine = jnp.where(lane == h, m_s[...] + jnp.log(l_s[...]), 0.0)

            @pl.when(h == 0)
            def _():
                lse_ref[...] = mine

            @pl.when(h > 0)
            def _():
                lse_ref[...] += mine

    qspec = pl.BlockSpec((tb, LANE), lambda i, h, j: (i, h))
    kspec = pl.BlockSpec((tb, LANE), lambda i, h, j: (jnp.minimum(j, i), h))
    krspec = pl.BlockSpec((tb, LANE), lambda i, h, j: (jnp.minimum(j, i), 0))
    return _pcall(
        body, name=name, grid=(nb, nh, nb),
        in_specs=[qspec, qspec, kspec, krspec, kspec],
        out_specs=[qspec, pl.BlockSpec((tb, LANE), lambda i, h, j: (i, 0))],
        out_shape=[jax.ShapeDtypeStruct((lp, nh * LANE), F32), jax.ShapeDtypeStruct((lp, LANE), F32)],
        scratch_shapes=[pltpu.VMEM((tb, 1), F32), pltpu.VMEM((tb, 1), F32), pltpu.VMEM((tb, LANE), F32)],
        compiler_params=_params(3),
    )(qn, qr, kn, kr, v)


def _flash_dq(qn, qr, kn, kr, v, o, do, lse, name, tb=ROW_ALIGN):
    lp = qn.shape[0]
    nb = lp // tb
    nh = MLA_HEADS

    def body(qn_ref, qr_ref, kn_ref, kr_ref, v_ref, o_ref, do_ref, lse_ref,
             dqn_ref, dqr_ref, dqn_acc, dqr_acc, delta_s, lse_s):
        i, h, j = pl.program_id(0), pl.program_id(1), pl.program_id(2)

        @pl.when(j == 0)
        def _():
            dqn_acc[...] = jnp.zeros_like(dqn_acc)
            dqr_acc[...] = jnp.zeros_like(dqr_acc)
            delta_s[...] = jnp.sum(do_ref[...] * o_ref[...], axis=1, keepdims=True)
            lse_s[...] = _head_lane(lse_ref[...], h)

        @pl.when(j <= i)
        def _():
            s = _att_scores(qn_ref[...], qr_ref[...], kn_ref[...], kr_ref[...], i, j, tb)
            p = jnp.exp(s - lse_s[...])
            dp = _bdot(do_ref[...], v_ref[...], _NT)
            ds = p * (dp - delta_s[...]) * _ATT_SCALE
            dqn_acc[...] += _bdot(ds, kn_ref[...], _NN)
            dqr_acc[...] += _bdot(ds, kr_ref[...], _NN)

        @pl.when(j == nb - 1)
        def _():
            dqn_ref[...] = dqn_acc[...]
            dqr_ref[...] = dqr_acc[...]

    qspec = pl.BlockSpec((tb, LANE), lambda i, h, j: (i, h))
    kspec = pl.BlockSpec((tb, LANE), lambda i, h, j: (jnp.minimum(j, i), h))
    krspec = pl.BlockSpec((tb, LANE), lambda i, h, j: (jnp.minimum(j, i), 0))
    lspec = pl.BlockSpec((tb, LANE), lambda i, h, j: (i, 0))
    shp = jax.ShapeDtypeStruct((lp, nh * LANE), F32)
    return _pcall(
        body, name=name, grid=(nb, nh, nb),
        in_specs=[qspec, qspec, kspec, krspec, kspec, qspec, qspec, lspec],
        out_specs=[qspec, qspec],
        out_shape=[shp, shp],
        scratch_shapes=[pltpu.VMEM((tb, LANE), F32), pltpu.VMEM((tb, LANE), F32),
                        pltpu.VMEM((tb, 1), F32), pltpu.VMEM((tb, 1), F32)],
        compiler_params=_params(3),
    )(qn, qr, kn, kr, v, o, do, lse)


def _flash_dkv(qn, qr, kn, kr, v, o, do, lse, name, tb=ROW_ALIGN):
    lp = qn.shape[0]
    nb = lp // tb
    nh = MLA_HEADS

    def body(qn_ref, qr_ref, kn_ref, kr_ref, v_ref, o_ref, do_ref, lse_ref,
             dkn_ref, dkr_ref, dv_ref, dkn_acc, dkr_acc, dv_acc):
        j, h, i = pl.program_id(0), pl.program_id(1), pl.program_id(2)

        @pl.when(i == 0)
        def _():
            dkn_acc[...] = jnp.zeros_like(dkn_acc)
            dv_acc[...] = jnp.zeros_like(dv_acc)

        @pl.when(jnp.logical_and(i == 0, h == 0))
        def _():
            dkr_acc[...] = jnp.zeros_like(dkr_acc)

        @pl.when(i >= j)
        def _():
            s = _att_scores(qn_ref[...], qr_ref[...], kn_ref[...], kr_ref[...], i, j, tb)
            p = jnp.exp(s - _head_lane(lse_ref[...], h))
            do_blk = do_ref[...]
            delta = jnp.sum(do_blk * o_ref[...], axis=1, keepdims=True)
            dv_acc[...] += _bdot(p, do_blk, _TN)
            dp = _bdot(do_blk, v_ref[...], _NT)
            ds = p * (dp - delta) * _ATT_SCALE
            dkn_acc[...] += _bdot(ds, qn_ref[...], _TN)
            dkr_acc[...] += _bdot(ds, qr_ref[...], _TN)

        @pl.when(i == nb - 1)
        def _():
            dkn_ref[...] = dkn_acc[...]
            dv_ref[...] = dv_acc[...]

        @pl.when(jnp.logical_and(i == nb - 1, h == nh - 1))
        def _():
            dkr_ref[...] = dkr_acc[...]

    qspec = pl.BlockSpec((tb, LANE), lambda j, h, i: (jnp.maximum(i, j), h))
    lspec = pl.BlockSpec((tb, LANE), lambda j, h, i: (jnp.maximum(i, j), 0))
    kspec = pl.BlockSpec((tb, LANE), lambda j, h, i: (j, h))
    krspec = pl.BlockSpec((tb, LANE), lambda j, h, i: (j, 0))
    shp = jax.ShapeDtypeStruct((lp, nh * LANE), F32)
    return _pcall(
        body, name=name, grid=(nb, nh, nb),
        in_specs=[qspec, qspec, kspec, krspec, kspec, qspec, qspec, lspec],
        out_specs=[kspec, krspec, kspec],
        out_shape=[shp, jax.ShapeDtypeStruct((lp, LANE), F32), shp],
        scratch_shapes=[pltpu.VMEM((tb, LANE), F32), pltpu.VMEM((tb, LANE), F32), pltpu.VMEM((tb, LANE), F32)],
        compiler_params=_params(3),
    )(qn, qr, kn, kr, v, o, do, lse)


def _adamw(w, g, m, v, name):
    rows, width = w.shape
    tr = next((t for t in (512, 256, 128, 64, 32, 16, 8) if rows % t == 0), rows)

    def body(w_ref, g_ref, m_ref, v_ref, d_ref, nm_ref, nv_ref):
        gg = g_ref[...]
        nm = ADAM_B1 * m_ref[...] + (1.0 - ADAM_B1) * gg
        nv = ADAM_B2 * v_ref[...] + (1.0 - ADAM_B2) * jnp.square(gg)
        m_hat = nm / (1.0 - ADAM_B1 ** ADAM_STEP)
        v_hat = nv / (1.0 - ADAM_B2 ** ADAM_STEP)
        d_ref[...] = -ADAM_LR * (m_hat / (jnp.sqrt(v_hat) + ADAM_EPS) + ADAM_WD * w_ref[...])
        nm_ref[...] = nm
        nv_ref[...] = nv

    spec = pl.BlockSpec((tr, width), lambda r: (r, 0))
    shp = jax.ShapeDtypeStruct((rows, width), F32)
    return _pcall(body, name=name, grid=(rows // tr,), in_specs=[spec] * 4, out_specs=[spec] * 3,
                  out_shape=[shp] * 3, compiler_params=_params(1))(w, g, m, v)


def _add_pair(a, b, name):
    s, rows, width = a.shape
    tr = next(t for t in (512, 256, 128, 64, 32, 16, 8) if rows % t == 0)

    def body(a_ref, b_ref, o_ref):
        o_ref[...] = a_ref[...] + b_ref[...]

    spec = pl.BlockSpec((1, tr, width), lambda i, r: (i, r, 0))
    return _pcall(body, name=name, grid=(s, rows // tr), in_specs=[spec, spec], out_specs=spec,
                  out_shape=jax.ShapeDtypeStruct(a.shape, F32), compiler_params=_params(2))(a, b)


def _sum_slots(a, name):
    s, rows, width = a.shape
    tr = next(t for t in (512, 256, 128, 64, 32, 16, 8) if rows % t == 0)

    def body(a_ref, o_ref):
        tot = a_ref[0]
        for k in range(1, s):
            tot = tot + a_ref[k]
        o_ref[...] = tot

    return _pcall(body, name=name, grid=(rows // tr,),
                  in_specs=[pl.BlockSpec((s, tr, width), lambda r: (0, r, 0))],
                  out_specs=pl.BlockSpec((tr, width), lambda r: (r, 0)),
                  out_shape=jax.ShapeDtypeStruct((rows, width), F32), compiler_params=_params(1))(a)


_ANY = pl.BlockSpec(memory_space=pl.ANY)


def _my_place():
    return lax.axis_index("x"), lax.axis_index("y"), lax.axis_index("c")


def _other_chips(x, y):
    return [(1 - x, y), (x, 1 - y), (1 - x, 1 - y)]


def _gather_shards(flat, name):
    rows, width = flat.shape

    def body(x_ref, out_ref, send_sems, recv_sems, local_sem):
        x, y, c = _my_place()
        mine = pltpu.make_async_copy(x_ref, out_ref.at[2 * x + y], local_sem)
        mine.start()
        sends = []
        for k, (px, py) in enumerate(_other_chips(x, y)):
            cp = pltpu.make_async_remote_copy(
                src_ref=x_ref, dst_ref=out_ref.at[2 * x + y], send_sem=send_sems.at[k], recv_sem=recv_sems.at[k],
                device_id=(px, py, c), device_id_type=MESH)
            cp.start()
            sends.append(cp)
        for k, (px, py) in enumerate(_other_chips(x, y)):
            pltpu.make_async_remote_copy(
                src_ref=x_ref, dst_ref=out_ref.at[2 * px + py], send_sem=send_sems.at[k], recv_sem=recv_sems.at[k],
                device_id=(px, py, c), device_id_type=MESH).wait_recv()
        for cp in sends:
            cp.wait_send()
        mine.wait()

    return _pcall(
        body, name=name, in_specs=[_ANY], out_specs=_ANY,
        out_shape=jax.ShapeDtypeStruct((4, rows, width), flat.dtype),
        scratch_shapes=[pltpu.SemaphoreType.DMA((3,)), pltpu.SemaphoreType.DMA((3,)), pltpu.SemaphoreType.DMA],
    )(flat)


def _sibling_split(g, name):
    s, rows, width = g.shape
    half = rows // 2

    def body(g_ref, own_ref, got_ref, send_sem, recv_sem, local_sem):
        x, y, c = _my_place()
        mine = pltpu.make_async_copy(g_ref.at[:, pl.ds(pl.multiple_of(c * half, 8), half), :], own_ref, local_sem)
        mine.start()
        cp = pltpu.make_async_remote_copy(
            src_ref=g_ref.at[:, pl.ds(pl.multiple_of((1 - c) * half, 8), half), :], dst_ref=got_ref,
            send_sem=send_sem, recv_sem=recv_sem, device_id=(x, y, 1 - c), device_id_type=MESH)
        cp.start()
        cp.wait_recv()
        cp.wait_send()
        mine.wait()

    shp = jax.ShapeDtypeStruct((s, half, width), g.dtype)
    return _pcall(
        body, name=name, in_specs=[_ANY], out_specs=[_ANY, _ANY], out_shape=[shp, shp],
        scratch_shapes=[pltpu.SemaphoreType.DMA, pltpu.SemaphoreType.DMA, pltpu.SemaphoreType.DMA],
    )(g)


def _chip_scatter(p, name):
    s, rows, width = p.shape

    def body(p_ref, out_ref, send_sems, recv_sems, local_sem):
        x, y, c = _my_place()
        me = 2 * x + y
        mine = pltpu.make_async_copy(p_ref.at[me], out_ref.at[me], local_sem)
        mine.start()
        sends = []
        for k, (px, py) in enumerate(_other_chips(x, y)):
            cp = pltpu.make_async_remote_copy(
                src_ref=p_ref.at[2 * px + py], dst_ref=out_ref.at[me], send_sem=send_sems.at[k],
                recv_sem=recv_sems.at[k], device_id=(px, py, c), device_id_type=MESH)
            cp.start()
            sends.append(cp)
        for k, (px, py) in enumerate(_other_chips(x, y)):
            pltpu.make_async_remote_copy(
                src_ref=p_ref.at[me], dst_ref=out_ref.at[2 * px + py], send_sem=send_sems.at[k],
                recv_sem=recv_sems.at[k], device_id=(px, py, c), device_id_type=MESH).wait_recv()
        for cp in sends:
            cp.wait_send()
        mine.wait()

    return _pcall(
        body, name=name, in_specs=[_ANY], out_specs=_ANY,
        out_shape=jax.ShapeDtypeStruct(p.shape, p.dtype),
        scratch_shapes=[pltpu.SemaphoreType.DMA((3,)), pltpu.SemaphoreType.DMA((3,)), pltpu.SemaphoreType.DMA],
    )(p)


def _sibling_join(qh, name):
    half, width = qh.shape

    def body(q_ref, out_ref, send_sem, recv_sem, local_sem):
        x, y, c = _my_place()
        my_rows = pl.ds(pl.multiple_of(c * half, 8), half)
        sib_rows = pl.ds(pl.multiple_of((1 - c) * half, 8), half)
        mine = pltpu.make_async_copy(q_ref, out_ref.at[my_rows, :], local_sem)
        mine.start()
        cp = pltpu.make_async_remote_copy(
            src_ref=q_ref, dst_ref=out_ref.at[my_rows, :], send_sem=send_sem, recv_sem=recv_sem,
            device_id=(x, y, 1 - c), device_id_type=MESH)
        cp.start()
        pltpu.make_async_remote_copy(
            src_ref=q_ref, dst_ref=out_ref.at[sib_rows, :], send_sem=send_sem, recv_sem=recv_sem,
            device_id=(x, y, 1 - c), device_id_type=MESH).wait_recv()
        cp.wait_send()
        mine.wait()

    return _pcall(
        body, name=name, in_specs=[_ANY], out_specs=_ANY,
        out_shape=jax.ShapeDtypeStruct((2 * half, width), qh.dtype),
        scratch_shapes=[pltpu.SemaphoreType.DMA, pltpu.SemaphoreType.DMA, pltpu.SemaphoreType.DMA],
    )(qh)


def _all_sum_small(part, name):
    rows, width = part.shape

    def body(p_ref, out_ref, land, send_sems, recv_sems):
        x, y, c = _my_place()
        me = 4 * x + 2 * y + c
        land[me] = p_ref[...]
        sends = []
        for k in range(1, 8):
            peer = (x ^ (k >> 2), y ^ ((k >> 1) & 1), c ^ (k & 1))
            cp = pltpu.make_async_remote_copy(
                src_ref=p_ref, dst_ref=land.at[me], send_sem=send_sems.at[k - 1], recv_sem=recv_sems.at[k - 1],
                device_id=peer, device_id_type=MESH)
            cp.start()
            sends.append(cp)
        for k in range(1, 8):
            px, py, pc = x ^ (k >> 2), y ^ ((k >> 1) & 1), c ^ (k & 1)
            pltpu.make_async_remote_copy(
                src_ref=p_ref, dst_ref=land.at[4 * px + 2 * py + pc], send_sem=send_sems.at[k - 1],
                recv_sem=recv_sems.at[k - 1], device_id=(px, py, pc), device_id_type=MESH).wait_recv()
        for cp in sends:
            cp.wait_send()
        tot = land[0]
        for k in range(1, 8):
            tot = tot + land[k]
        out_ref[...] = tot

    vmem = pl.BlockSpec(memory_space=pltpu.VMEM)
    return _pcall(
        body, name=name, in_specs=[vmem], out_specs=vmem,
        out_shape=jax.ShapeDtypeStruct((rows, width), F32),
        scratch_shapes=[pltpu.VMEM((8, rows, width), F32), pltpu.SemaphoreType.DMA((7,)),
                        pltpu.SemaphoreType.DMA((7,))],
    )(part)


def _big_layout(shards):
    return [(a.shape[0], a.shape[1], ax) for a, ax in shards]


def _pack_shards(arrs):
    flat = jnp.concatenate([a.reshape(-1) for a in arrs])
    assert flat.shape[0] % (16 * LANE) == 0, flat.shape
    return flat.reshape(-1, LANE)


def _unpack_shards(flat, layout):
    flat = flat.reshape(-1)
    out, off = [], 0
    for r, c, _ in layout:
        out.append(flat[off:off + r * c].reshape(r, c))
        off += r * c
    return out


def _unpack_full(gathered, layout):
    g = gathered.reshape(4, -1)
    out, off = [], 0
    for r, c, ax in layout:
        seg = g[:, off:off + r * c].reshape(4, r, c)
        out.append(seg.transpose(1, 0, 2).reshape(r, 4 * c) if ax == 1 else seg.reshape(4 * r, c))
        off += r * c
    return out


def _pack_full(fulls, layout):
    parts = []
    for a, (r, c, ax) in zip(fulls, layout):
        if ax == 1:
            parts.append(a.reshape(r, 4, c).transpose(1, 0, 2).reshape(4, r * c))
        else:
            parts.append(a.reshape(4, r * c))
    return jnp.concatenate(parts, axis=1).reshape(4, -1, LANE)


def _pad_lanes(a, width=LANE):
    return jnp.pad(a, [(0, 0)] * (a.ndim - 1) + [(0, width - a.shape[-1])])


def _pack_small(arrs):
    rows = [_pad_lanes(a.reshape(1, -1), -(-a.size // LANE) * LANE).reshape(-1, LANE) for a in arrs]
    flat = jnp.concatenate(rows, axis=0)
    return jnp.pad(flat, ((0, -flat.shape[0] % 8), (0, 0)))


def _unpack_small(flat, shapes):
    out, off = [], 0
    for shp in shapes:
        n = math.prod(shp)
        nr = -(-n // LANE)
        out.append(flat[off:off + nr].reshape(-1)[:n].reshape(shp))
        off += nr
    return out


def kernel(x, meta_tokens, pre_norm, post_norm, gdn_w_in, gdn_conv_w, gdn_a_log, gdn_dt_bias, gdn_out_norm, gdn_w_out, kv_norm, kv_w_down, kv_latent_norm, kv_w_up, mla_w_in, mla_q_latent_norm, mla_w_q_up, mla_w_out, loss_target, m_meta_tokens, m_pre_norm, m_post_norm, m_gdn_w_in, m_gdn_conv_w, m_gdn_a_log, m_gdn_dt_bias, m_gdn_out_norm, m_gdn_w_out, m_kv_norm, m_kv_w_down, m_kv_latent_norm, m_kv_w_up, m_mla_w_in, m_mla_q_latent_norm, m_mla_w_q_up, m_mla_w_out, v_meta_tokens, v_pre_norm, v_post_norm, v_gdn_w_in, v_gdn_conv_w, v_gdn_a_log, v_gdn_dt_bias, v_gdn_out_norm, v_gdn_w_out, v_kv_norm, v_kv_w_down, v_kv_latent_norm, v_kv_w_up, v_mla_w_in, v_mla_q_latent_norm, v_mla_w_q_up, v_mla_w_out):
    seq = x.shape[1]
    d = D_MODEL
    lp = -(-(ROW0 + seq) // ROW_ALIGN) * ROW_ALIGN
    tail = lp - ROW0 - seq

    big_names = ["meta_tokens", "gdn_w_in", "gdn_conv_w", "gdn_w_out", "kv_w_down", "kv_w_up", "mla_w_in",
                 "mla_w_q_up", "mla_w_out"]
    big_axis = [1, 1, 1, 0, 0, 1, 1, 1, 0]
    big_w = [meta_tokens, gdn_w_in[0], gdn_conv_w[0], gdn_w_out[0], kv_w_down, kv_w_up, mla_w_in[0], mla_w_q_up[0],
             mla_w_out[0]]
    big_m = [m_meta_tokens, m_gdn_w_in[0], m_gdn_conv_w[0], m_gdn_w_out[0], m_kv_w_down, m_kv_w_up, m_mla_w_in[0],
             m_mla_w_q_up[0], m_mla_w_out[0]]
    big_v = [v_meta_tokens, v_gdn_w_in[0], v_gdn_conv_w[0], v_gdn_w_out[0], v_kv_w_down, v_kv_w_up, v_mla_w_in[0],
             v_mla_w_q_up[0], v_mla_w_out[0]]
    layout = _big_layout(list(zip(big_w, big_axis)))
    w_flat = _pack_shards(big_w)
    (meta_f, w_in0, conv_w, w_out0, kv_down, kv_up, w_in1, w_qup, w_out1) = _unpack_full(
        _gather_shards(w_flat, "gather_weights"), layout)

    nv = GDN_V_HEADS
    w_qkv = w_in0[:, :GDN_CONV_W]
    w_z0 = w_in0[:, GDN_CONV_W:GDN_CONV_W + GDN_V_W]
    w_b = _pad_lanes(w_in0[:, GDN_CONV_W + GDN_V_W:GDN_CONV_W + GDN_V_W + nv])
    w_a = _pad_lanes(w_in0[:, GDN_CONV_W + GDN_V_W + nv:])
    w_ckv = kv_down[:, :MLA_KV_RANK]
    w_kr = _pad_lanes(kv_down[:, MLA_KV_RANK:])
    kvu = kv_up.reshape(MLA_KV_RANK, MLA_HEADS, 2 * LANE)
    w_kn = kvu[:, :, :LANE].reshape(MLA_KV_RANK, MLA_HEADS * LANE)
    w_v = kvu[:, :, LANE:].reshape(MLA_KV_RANK, MLA_HEADS * LANE)
    w_cq = w_in1[:, :MLA_Q_RANK]
    w_z1 = w_in1[:, MLA_Q_RANK:]
    qu = w_qup.reshape(MLA_Q_RANK, MLA_HEADS, MLA_QK)
    w_qn = qu[:, :, :MLA_NOPE].reshape(MLA_Q_RANK, MLA_HEADS * LANE)
    w_qr = _pad_lanes(qu[:, :, MLA_NOPE:]).reshape(MLA_Q_RANK, MLA_HEADS * LANE)

    pre0, pre1 = pre_norm[0:1], pre_norm[1:2]
    post0, post1 = post_norm[0:1], post_norm[1:2]
    a_log = _pad_lanes(gdn_a_log)
    dt_bias = _pad_lanes(gdn_dt_bias)
    kvn = kv_norm.reshape(1, d)
    kvl = kv_latent_norm.reshape(1, MLA_KV_RANK)
    qln = mla_q_latent_norm

    h0 = jnp.concatenate([jnp.zeros((FRONT, d), F32), meta_f, x[0], jnp.zeros((tail, d), F32)], axis=0)
    tgt = jnp.pad(loss_target[0], ((ROW0, tail), (0, 0)))
    pos = jnp.maximum(jnp.arange(lp, dtype=jnp.int32) - FRONT, 0).astype(F32)
    inv = ROPE_THETA ** (-jnp.arange(0, MLA_ROPE, 2, dtype=F32) / MLA_ROPE)
    ang = pos[:, None] * inv[None, :]
    zeros64 = jnp.zeros((lp, LANE - MLA_ROPE), F32)
    cos_t = jnp.concatenate([jnp.cos(ang), jnp.cos(ang), zeros64], axis=1)
    sin_t = jnp.concatenate([-jnp.sin(ang), jnp.sin(ang), zeros64], axis=1)

    def valid_rows(ridx):
        return jnp.logical_and(ridx >= FRONT, ridx < ROW0 + seq)

    def f_pre0(ridx, g, h, gain):
        return _rms(h, gain), h

    (hn0,) = _rowwise("pre0", lambda *a: f_pre0(*a)[:1], [_In(h0), _In(pre0, "const")],
                      [_Out("row", (lp, d), BF16)])
    qkv_raw = _mm(hn0, w_qkv, "nn", "gdn_in_qkv")
    z0 = _mm(hn0, w_z0, "nn", "gdn_in_z")
    b_raw = _mm(hn0, w_b, "nn", "gdn_in_b")
    a_raw = _mm(hn0, w_a, "nn", "gdn_in_a")

    def f_ba(ridx, g, b, a, alog, dtb):
        tr = b.shape[0]
        ok = valid_rows(ridx).astype(F32)
        beta = jax.nn.sigmoid(b) * ok
        gate = -jnp.exp(alog) * _softplus(a + dtb) * ok
        ii = lax.broadcasted_iota(jnp.int32, (tr, tr), 0)
        jj = lax.broadcasted_iota(jnp.int32, (tr, tr), 1)
        tri = jnp.logical_and((ii >> 6) == (jj >> 6), ii >= jj).astype(F32)
        return beta, _hdot(tri, gate)

    ba_ins = [_In(b_raw), _In(a_raw), _In(a_log, "const"), _In(dt_bias, "const")]
    beta, gc = _rowwise("gdn_gates", f_ba, ba_ins, [_Out("row", (lp, LANE)), _Out("row", (lp, LANE))])
    qkv = _conv_fwd(qkv_raw, conv_w, "gdn_conv")
    o0, ckpt = _gdn_fwd(qkv, beta, gc, "gdn_scan")

    def f_gate0(ridx, g, o, z, gain):
        return (_rms(o, gain) * _silu(z),)

    gate0_ins = [_In(o0, grouped=True), _In(z0, grouped=True), _In(gdn_out_norm, "const")]
    (gated0,) = _rowwise("gdn_gate", f_gate0, gate0_ins, [_Out("row", (lp, GDN_V_W), BF16, grouped=True)],
                         groups=nv, tr=640)
    y0 = _mm(gated0, w_out0, "nn", "gdn_out")

    def f_mid(ridx, g, h, y, g_post, g_pre, g_kv):
        h1 = h + _rms(y, g_post)
        return h1, _rms(h1, g_pre), _rms(h1, g_kv)

    mid_ins = [_In(h0), _In(y0), _In(post0, "const"), _In(pre1, "const"), _In(kvn, "const")]
    h1, hn1, hkv = _rowwise("mid", f_mid, mid_ins,
                            [_Out("row", (lp, d)), _Out("row", (lp, d), BF16), _Out("row", (lp, d), BF16)])

    ckv_raw = _mm(hkv, w_ckv, "nn", "kv_down_c")
    kr_raw = _mm(hkv, w_kr, "nn", "kv_down_r")

    def f_ckv(ridx, g, c, r, cs, sn, gain):
        return _rms(c, gain), _rope(r, cs, sn)

    ckv_ins = [_In(ckv_raw), _In(kr_raw), _In(cos_t), _In(sin_t), _In(kvl, "const")]
    ckv, kr = _rowwise("kv_latent", f_ckv, ckv_ins, [_Out("row", (lp, LANE)), _Out("row", (lp, LANE))], tr=640)
    kn = _mm(ckv, w_kn, "nn", "kv_up_k")
    vv = _mm(ckv, w_v, "nn", "kv_up_v")
    cq_raw = _mm(hn1, w_cq, "nn", "mla_in_q")
    z1 = _mm(hn1, w_z1, "nn", "mla_in_z")

    def f_cq(ridx, g, c, gain):
        return (_rms(c, gain),)

    cq_ins = [_In(cq_raw), _In(qln, "const")]
    (cq,) = _rowwise("q_latent", f_cq, cq_ins, [_Out("row", (lp, MLA_Q_RANK))], tr=640)
    qn = _mm(cq, w_qn, "nn", "q_up_n")
    qr_raw = _mm(cq, w_qr, "nn", "q_up_r")

    def f_qrope(ridx, g, r, cs, sn):
        return (_rope(r, cs, sn),)

    qr_ins = [_In(qr_raw, grouped=True), _In(cos_t), _In(sin_t)]
    (qr,) = _rowwise("q_rope", f_qrope, qr_ins, [_Out("row", (lp, MLA_HEADS * LANE), F32, grouped=True)],
                     groups=MLA_HEADS, tr=640)
    o1, lse = _flash_fwd(qn, qr, kn, kr, vv, "attention")

    def f_gate1(ridx, g, o, z):
        return (o * _silu(z),)

    gate1_ins = [_In(o1), _In(z1)]
    (og,) = _rowwise("mla_gate", f_gate1, gate1_ins, [_Out("row", (lp, MLA_HEADS * LANE), BF16)])
    y1 = _mm(og, w_out1, "nn", "mla_out")

    def f_final(ridx, g, h, y, t, gain):
        ok = jnp.logical_and(ridx >= ROW0, ridx < ROW0 + seq).astype(F32)

        def rows_loss(h_, y_, gain_):
            err = (h_ + _rms(y_, gain_) - t) * ok
            return 0.5 * jnp.sum(jnp.sum(err * err, axis=1, keepdims=True), axis=0, keepdims=True) / d

        val, vjp = jax.vjp(rows_loss, h, y, gain)
        dh, dy, dgain = vjp(jnp.ones((1, 1), F32))
        return dh, dy, dgain, jnp.broadcast_to(val, (1, LANE))

    dh2, dy1, dpost1, loss_part = _rowwise(
        "loss_head", f_final, [_In(h1), _In(y1), _In(tgt), _In(post1, "const")],
        [_Out("row", (lp, d)), _Out("row", (lp, d)), _Out("acc", (1, d)), _Out("acc", (1, LANE))])

    dog = _mm(dy1, w_out1, "nt", "mla_out_dx")
    dw_out1 = _mm(og, dy1, "tn", "mla_out_dw")
    do1, dz1 = _rowwise_vjp("mla_gate_bwd", f_gate1, gate1_ins, [[dog]], [0, 1])
    dqn, dqr = _flash_dq(qn, qr, kn, kr, vv, o1, do1, lse, "attention_dq")
    dkn, dkr, dvv = _flash_dkv(qn, qr, kn, kr, vv, o1, do1, lse, "attention_dkv")
    (dqr_raw,) = _rowwise_vjp("q_rope_bwd", f_qrope, qr_ins, [[dqr]], [0], groups=MLA_HEADS, tr=640)
    dcq_a = _mm(dqn, w_qn, "nt", "q_up_n_dx")
    dcq_b = _mm(dqr_raw, w_qr, "nt", "q_up_r_dx")
    dw_qn = _mm(cq, dqn, "tn", "q_up_n_dw")
    dw_qr = _mm(cq, dqr_raw, "tn", "q_up_r_dw")
    dcq_raw, dqln = _rowwise_vjp("q_latent_bwd", f_cq, cq_ins, [[dcq_a, dcq_b]], [0, 1], tr=640)
    dhn1_a = _mm(dcq_raw, w_cq, "nt", "mla_in_q_dx")
    dhn1_b = _mm(dz1, w_z1, "nt", "mla_in_z_dx")
    dw_cq = _mm(hn1, dcq_raw, "tn", "mla_in_q_dw")
    dw_z1 = _mm(hn1, dz1, "tn", "mla_in_z_dw")
    dckv_a = _mm(dkn, w_kn, "nt", "kv_up_k_dx")
    dckv_b = _mm(dvv, w_v, "nt", "kv_up_v_dx")
    dw_kn = _mm(ckv, dkn, "tn", "kv_up_k_dw")
    dw_v = _mm(ckv, dvv, "tn", "kv_up_v_dw")
    dckv_raw, dkr_raw, dkvl = _rowwise_vjp("kv_latent_bwd", f_ckv, ckv_ins, [[dckv_a, dckv_b], [dkr]], [0, 1, 4],
                                           tr=640)
    dhkv_a = _mm(dckv_raw, w_ckv, "nt", "kv_down_c_dx")
    dhkv_b = _mm(dkr_raw, w_kr, "nt", "kv_down_r_dx")
    dw_ckv = _mm(hkv, dckv_raw, "tn", "kv_down_c_dw")
    dw_kr = _mm(hkv, dkr_raw, "tn", "kv_down_r_dw")
    dh0_res, dy0, dpost0, dpre1, dkvn = _rowwise_vjp(
        "mid_bwd", f_mid, mid_ins, [[dh2], [dhn1_a, dhn1_b], [dhkv_a, dhkv_b]], [0, 1, 2, 3, 4])

    dgated0 = _mm(dy0, w_out0, "nt", "gdn_out_dx")
    dw_out0 = _mm(gated0, dy0, "tn", "gdn_out_dw")
    do0, dz0, doutn = _rowwise_vjp("gdn_gate_bwd", f_gate0, gate0_ins, [[dgated0]], [0, 1, 2], groups=nv, tr=640)
    dq2, dk2, dv0, dbeta, dgc = _gdn_bwd(qkv, beta, gc, ckpt, do0, "gdn_scan_bwd")
    db_raw, da_raw, dalog, ddtb = _rowwise_vjp("gdn_gates_bwd", f_ba, ba_ins, [[dbeta], [dgc]], [0, 1, 2, 3])
    dqkv_raw, dconv = _conv_bwd(qkv_raw, conv_w, dq2, dk2, dv0, "gdn_conv_bwd")
    dhn0_a = _mm(dqkv_raw, w_qkv, "nt", "gdn_in_qkv_dx")
    dhn0_b = _mm(dz0, w_z0, "nt", "gdn_in_z_dx")
    dhn0_c = _mm(db_raw, w_b, "nt", "gdn_in_b_dx")
    dhn0_d = _mm(da_raw, w_a, "nt", "gdn_in_a_dx")
    dw_qkv = _mm(hn0, dqkv_raw, "tn", "gdn_in_qkv_dw")
    dw_z0 = _mm(hn0, dz0, "tn", "gdn_in_z_dw")
    dw_b = _mm(hn0, db_raw, "tn", "gdn_in_b_dw")
    dw_a = _mm(hn0, da_raw, "tn", "gdn_in_a_dw")
    dh0, dpre0 = _rowwise_vjp("pre0_bwd", f_pre0, [_In(h0), _In(pre0, "const")],
                              [[dhn0_a, dhn0_b, dhn0_c, dhn0_d], [dh0_res]], [0, 1])

    grad_x = dh0[ROW0:ROW0 + seq][None]
    g_meta = dh0[FRONT:ROW0]
    g_w_in0 = jnp.concatenate([dw_qkv, dw_z0, dw_b[:, :nv], dw_a[:, :nv]], axis=1)
    g_kv_down = jnp.concatenate([dw_ckv, dw_kr[:, :MLA_ROPE]], axis=1)
    g_kv_up = jnp.concatenate([dw_kn.reshape(MLA_KV_RANK, MLA_HEADS, LANE), dw_v.reshape(MLA_KV_RANK, MLA_HEADS, LANE)],
                              axis=2).reshape(MLA_KV_RANK, MLA_HEADS * 2 * LANE)
    g_w_in1 = jnp.concatenate([dw_cq, dw_z1], axis=1)
    g_qup = jnp.concatenate([dw_qn.reshape(MLA_Q_RANK, MLA_HEADS, LANE),
                             dw_qr.reshape(MLA_Q_RANK, MLA_HEADS, LANE)[:, :, :MLA_ROPE]],
                            axis=2).reshape(MLA_Q_RANK, MLA_HEADS * MLA_QK)
    big_g = [g_meta, g_w_in0, dconv, dw_out0, g_kv_down, g_kv_up, g_w_in1, g_qup, dw_out1]

    g_all = _pack_full(big_g, layout)
    own, got = _sibling_split(g_all, "grads_sibling_split")
    chip_part = _add_pair(own, got, "grads_chip_sum")
    from_chips = _chip_scatter(chip_part, "grads_chip_scatter")
    half_sum = _sum_slots(from_chips, "grads_total")
    g_flat = _sibling_join(half_sum, "grads_sibling_join")

    small_shapes = [(2, d), (2, d), (1, nv), (1, nv), (1, GDN_DK), (d,), (MLA_KV_RANK,), (1, MLA_Q_RANK), (1, LANE)]
    small_part = _pack_small([jnp.concatenate([dpre0, dpre1], axis=0), jnp.concatenate([dpost0, dpost1], axis=0),
                              dalog[:, :nv], ddtb[:, :nv], doutn, dkvn, dkvl, dqln, loss_part])
    small_tot = _all_sum_small(small_part, "small_sum")
    small_g = _unpack_small(small_tot, small_shapes)
    loss = small_g[-1][0, 0]

    d_flat, m_flat, v_flat = _adamw(w_flat, g_flat, _pack_shards(big_m), _pack_shards(big_v), "adamw_sharded")
    small_w = [pre_norm, post_norm, gdn_a_log, gdn_dt_bias, gdn_out_norm, kv_norm, kv_latent_norm, mla_q_latent_norm]
    small_m = [m_pre_norm, m_post_norm, m_gdn_a_log, m_gdn_dt_bias, m_gdn_out_norm, m_kv_norm, m_kv_latent_norm,
               m_mla_q_latent_norm]
    small_v = [v_pre_norm, v_post_norm, v_gdn_a_log, v_gdn_dt_bias, v_gdn_out_norm, v_kv_norm, v_kv_latent_norm,
               v_mla_q_latent_norm]
    g_small_flat = _pack_small(small_g[:-1])
    ds_flat, ms_flat, vs_flat = _adamw(_pack_small(small_w), g_small_flat, _pack_small(small_m), _pack_small(small_v),
                                       "adamw_replicated")

    def assemble(big_flat, small_flat):
        bigs = dict(zip(big_names, [a.reshape(w.shape) for a, w in zip(
            _unpack_shards(big_flat, layout),
            [meta_tokens, gdn_w_in, gdn_conv_w, gdn_w_out, kv_w_down, kv_w_up, mla_w_in, mla_w_q_up, mla_w_out])]))
        smalls = dict(zip(["pre_norm", "post_norm", "gdn_a_log", "gdn_dt_bias", "gdn_out_norm", "kv_norm",
                           "kv_latent_norm", "mla_q_latent_norm"], _unpack_small(small_flat, small_shapes[:-1])))
        both = {**bigs, **smalls}
        order = ["meta_tokens", "pre_norm", "post_norm", "gdn_w_in", "gdn_conv_w", "gdn_a_log", "gdn_dt_bias",
                 "gdn_out_norm", "gdn_w_out", "kv_norm", "kv_w_down", "kv_latent_norm", "kv_w_up", "mla_w_in",
                 "mla_q_latent_norm", "mla_w_q_up", "mla_w_out"]
        return [both[n] for n in order]

    grads = assemble(g_flat, g_small_flat)
    deltas = assemble(d_flat, ds_flat)
    new_m = assemble(m_flat, ms_flat)
    new_v = assemble(v_flat, vs_flat)
    return (loss, grad_x, *grads, *deltas, *new_m, *new_v)
```

```python
import functools
import math

import jax
import jax.numpy as jnp
from jax import lax
from jax.experimental import pallas as pl
from jax.experimental.pallas import tpu as pltpu

F32 = jnp.float32
BF16 = jnp.bfloat16
MESH = pl.DeviceIdType.MESH

D_MODEL = 1024
N_META = 16
FRONT = 48
ROW0 = FRONT + N_META
ROW_ALIGN = 640
NORM_EPS = 1e-6
LANE = 128

GDN_QK_HEADS = 8
GDN_V_HEADS = 16
GDN_DK = 128
GDN_CHUNK = 64
GDN_QK_W = 1024
GDN_V_W = 2048
GDN_CONV_W = 4096

MLA_HEADS = 16
MLA_NOPE = 128
MLA_ROPE = 64
MLA_QK = 192
MLA_Q_RANK = 256
MLA_KV_RANK = 128
ROPE_THETA = 10000.0

ADAM_LR = 0.001
ADAM_B1 = 0.9
ADAM_B2 = 0.999
ADAM_EPS = 1e-08
ADAM_WD = 0.01
ADAM_STEP = 10

VMEM_LIMIT_V7X = 56 * 1024 * 1024
NEG = -1e30

_NN = ((1,), (0,))
_NT = ((1,), (1,))
_TN = ((0,), (0,))
_HI = lax.Precision.HIGHEST


def _pcall(body, **kw):
    return pl.pallas_call(body, **kw)


def _params(n_axes):
    return pltpu.CompilerParams(dimension_semantics=("arbitrary",) * n_axes, vmem_limit_bytes=VMEM_LIMIT_V7X)


def _dot(a, b, dims, prec=None):
    return lax.dot_general(a, b, (dims, ((), ())), precision=prec, preferred_element_type=F32)


def _bdot(a, b, dims):
    return _dot(a.astype(BF16), b.astype(BF16), dims)


def _hdot(a, b, dims=_NN):
    return _dot(a, b, dims, _HI)


def _fdot(a, b, dims):
    return _dot(a, b, dims)


def _tile(n):
    for t in (640, 512, 256, 128):
        if n % t == 0:
            return t
    raise ValueError(n)


def _mm(a, b, mode, name, out_dtype=F32):
    if mode == "nn":
        (m, k), (k2, n) = a.shape, b.shape
    elif mode == "nt":
        (m, k), (n, k2) = a.shape, b.shape
    else:
        (k, m), (k2, n) = a.shape, b.shape
    assert k == k2, (a.shape, b.shape, mode)
    tm, tn, tk = _tile(m), _tile(n), _tile(k)
    nk = k // tk
    dims = {"nn": _NN, "nt": _NT, "tn": _TN}[mode]

    def body(a_ref, b_ref, o_ref, acc):
        kk = pl.program_id(2)

        @pl.when(kk == 0)
        def _():
            acc[...] = jnp.zeros_like(acc)

        acc[...] += _bdot(a_ref[...], b_ref[...], dims)

        @pl.when(kk == nk - 1)
        def _():
            o_ref[...] = acc[...].astype(out_dtype)

    if mode == "tn":
        a_spec = pl.BlockSpec((tk, tm), lambda i, j, kk: (kk, i))
    else:
        a_spec = pl.BlockSpec((tm, tk), lambda i, j, kk: (i, kk))
    if mode == "nt":
        b_spec = pl.BlockSpec((tn, tk), lambda i, j, kk: (j, kk))
    else:
        b_spec = pl.BlockSpec((tk, tn), lambda i, j, kk: (kk, j))
    return _pcall(
        body, name=name, grid=(m // tm, n // tn, nk),
        in_specs=[a_spec, b_spec],
        out_specs=pl.BlockSpec((tm, tn), lambda i, j, kk: (i, j)),
        out_shape=jax.ShapeDtypeStruct((m, n), out_dtype),
        scratch_shapes=[pltpu.VMEM((tm, tn), F32)],
        compiler_params=_params(3),
    )(a, b)


class _In:
    def __init__(self, arr, kind="row", grouped=False, goff=0):
        self.arr, self.kind, self.grouped, self.goff = arr, kind, grouped, goff


class _Out:
    def __init__(self, kind, shape, dtype=F32, grouped=False):
        self.kind, self.shape, self.dtype, self.grouped = kind, shape, dtype, grouped


def _rowwise(name, fn, ins, outs, *, groups=1, tr=320):
    lp = next(i.arr.shape[0] for i in ins if i.kind == "row")
    nr = lp // tr
    assert lp % tr == 0

    def in_spec(i):
        w = i.arr.shape[1]
        if i.kind == "row":
            if i.grouped:
                return pl.BlockSpec((tr, LANE), lambda g, r, o=i.goff: (r, g + o))
            return pl.BlockSpec((tr, w), lambda g, r: (r, 0))
        if i.grouped:
            return pl.BlockSpec((i.arr.shape[0], LANE), lambda g, r, o=i.goff: (0, g + o))
        return pl.BlockSpec(i.arr.shape, lambda g, r: (0, 0))

    def out_spec(o):
        if o.kind == "row":
            if o.grouped:
                return pl.BlockSpec((tr, LANE), lambda g, r: (r, g))
            assert groups == 1
            return pl.BlockSpec((tr, o.shape[1]), lambda g, r: (r, 0))
        if o.grouped:
            return pl.BlockSpec((o.shape[0], LANE), lambda g, r: (0, g))
        return pl.BlockSpec(o.shape, lambda g, r: (0, 0))

    n_in = len(ins)

    def body(*refs):
        g = pl.program_id(0)
        r = pl.program_id(1)
        ridx = r * tr + lax.broadcasted_iota(jnp.int32, (tr, 1), 0)
        res = fn(ridx, g, *[ref[...] for ref in refs[:n_in]])
        assert len(res) == len(outs), (name, len(res), len(outs))
        for o, ref, val in zip(outs, refs[n_in:], res):
            if o.kind == "row":
                ref[...] = val.astype(o.dtype)
            else:
                first = (r == 0) if o.grouped else jnp.logical_and(r == 0, g == 0)

                @pl.when(first)
                def _(ref=ref, val=val):
                    ref[...] = val.astype(F32)

                @pl.when(jnp.logical_not(first))
                def _(ref=ref, val=val):
                    ref[...] += val.astype(F32)

    res = _pcall(
        body, name=name, grid=(groups, nr),
        in_specs=[in_spec(i) for i in ins],
        out_specs=[out_spec(o) for o in outs],
        out_shape=[jax.ShapeDtypeStruct(o.shape, o.dtype) for o in outs],
        compiler_params=_params(2),
    )(*[i.arr for i in ins])
    return res


def _rowwise_vjp(name, fn, ins, cots, diff, *, groups=1, tr=320):
    n_in = len(ins)
    grouped = groups > 1
    cot_ins = []
    counts = []
    for arrs in cots:
        counts.append(len(arrs))
        for a in arrs:
            cot_ins.append(_In(a, "row", grouped=grouped and a.shape[1] > LANE))
    lp = next(i.arr.shape[0] for i in ins if i.kind == "row")
    outs = []
    for d in diff:
        i = ins[d]
        if i.kind == "row":
            w = groups * LANE if i.grouped else i.arr.shape[1]
            outs.append(_Out("row", (lp, w), F32, grouped=i.grouped))
        else:
            outs.append(_Out("acc", i.arr.shape, F32, grouped=i.grouped))

    def bfn(ridx, g, *allvals):
        vals = list(allvals[:n_in])
        cvals = allvals[n_in:]

        def f(*dv):
            full = list(vals)
            for i, v in zip(diff, dv):
                full[i] = v
            return tuple(fn(ridx, g, *full))

        primal, vjp = jax.vjp(f, *[vals[i].astype(F32) for i in diff])
        cts = []
        pos = 0
        for k, cnt in enumerate(counts):
            if cnt == 0:
                cts.append(jnp.zeros_like(primal[k]))
            else:
                c = cvals[pos].astype(F32)
                for extra in cvals[pos + 1:pos + cnt]:
                    c = c + extra.astype(F32)
                cts.append(c.astype(primal[k].dtype))
            pos += cnt
        return vjp(tuple(cts))

    return _rowwise(name, bfn, list(ins) + cot_ins, outs, groups=groups, tr=tr)


def _rms(x, g):
    return x * lax.rsqrt(jnp.mean(x * x, axis=-1, keepdims=True) + NORM_EPS) * g


def _silu(x):
    return x * jax.nn.sigmoid(x)


def _softplus(x):
    return jnp.maximum(x, 0.0) + jnp.log(1.0 + jnp.exp(-jnp.abs(x)))


def _swap_halves(x):
    lane = lax.broadcasted_iota(jnp.int32, x.shape, x.ndim - 1)
    return jnp.where(lane < 32, pltpu.roll(x, LANE - 32, x.ndim - 1), pltpu.roll(x, 32, x.ndim - 1))


@jax.custom_vjp
def _rope(x, c, s):
    return x * c + _swap_halves(x) * s


def _rope_fwd(x, c, s):
    return _rope(x, c, s), (c, s)


def _rope_bwd(res, dy):
    c, s = res
    return dy * c + _swap_halves(dy * s), jnp.zeros_like(c), jnp.zeros_like(s)


_rope.defvjp(_rope_fwd, _rope_bwd)


def _conv_post(c, g):
    s = _silu(c)
    n = s * lax.rsqrt(jnp.sum(s * s, axis=-1, keepdims=True) + NORM_EPS)
    return jnp.where(g < GDN_QK_HEADS, n * (GDN_DK ** -0.5), jnp.where(g < 2 * GDN_QK_HEADS, n, s))


def _conv_taps(xe, w):
    c = xe[8:] * w[3]
    for s in (1, 2, 3):
        c = c + pltpu.roll(xe, s, 0)[8:] * w[3 - s]
    return c


def _conv_fwd(x, w, name, tr=640):
    lp, width = x.shape
    groups = width // LANE
    nr = lp // tr

    def body(x_ref, prev_ref, w_ref, o_ref):
        g = pl.program_id(0)
        r = pl.program_id(1)
        prev = jnp.where(r > 0, prev_ref[...], 0.0)
        xe = jnp.concatenate([prev, x_ref[...]], axis=0)
        o_ref[...] = _conv_post(_conv_taps(xe, [w_ref[t:t + 1, :] for t in range(4)]), g)

    return _pcall(
        body, name=name, grid=(groups, nr),
        in_specs=[pl.BlockSpec((tr, LANE), lambda g, r: (r, g)),
                  pl.BlockSpec((8, LANE), lambda g, r: (jnp.maximum(r * (tr // 8) - 1, 0), g)),
                  pl.BlockSpec((4, LANE), lambda g, r: (0, g))],
        out_specs=pl.BlockSpec((tr, LANE), lambda g, r: (r, g)),
        out_shape=jax.ShapeDtypeStruct((lp, width), F32),
        compiler_params=_params(2),
    )(x, x, w)


def _conv_bwd(x, w, dq, dk, dv, name, tr=640):
    lp, width = x.shape
    groups = width // LANE
    nr = lp // tr
    last8 = lp // 8 - 1
    nqk = GDN_QK_HEADS

    def body(x_ref, prev_ref, next_ref, w_ref, q_ref, k_ref, v_ref, q_n, k_n, v_n, dx_ref, dw_ref):
        g = pl.program_id(0)
        r = pl.program_id(1)
        w = [w_ref[t:t + 1, :] for t in range(4)]
        not_last = r < nr - 1

        def pick(a, b, c):
            return jnp.where(g < nqk, a[...], jnp.where(g < 2 * nqk, b[...], c[...]))

        dy = pick(q_ref, k_ref, v_ref)
        dyn = jnp.where(not_last, pick(q_n, k_n, v_n), 0.0)
        prev = jnp.where(r > 0, prev_ref[...], 0.0)
        nxt = jnp.where(not_last, next_ref[...], 0.0)
        xe = jnp.concatenate([prev, x_ref[...], nxt], axis=0)
        ce = _conv_taps(xe, w)
        _, vjp = jax.vjp(lambda c: _conv_post(c, g), ce)
        (dce,) = vjp(jnp.concatenate([dy, dyn], axis=0))
        n = tr + 8
        dx = dce * w[3]
        for s in (1, 2, 3):
            dx = dx + pltpu.roll(dce, n - s, 0) * w[3 - s]
        dx_ref[...] = dx[:tr]
        dc = dce[:tr]
        row4 = lax.broadcasted_iota(jnp.int32, (4, LANE), 0)
        dw = jnp.zeros((4, LANE), F32)
        for s in (0, 1, 2, 3):
            xs = xe[8:8 + tr] if s == 0 else pltpu.roll(xe, s, 0)[8:8 + tr]
            dw = dw + jnp.where(row4 == 3 - s, jnp.sum(dc * xs, axis=0, keepdims=True), 0.0)

        @pl.when(r == 0)
        def _():
            dw_ref[...] = dw

        @pl.when(r > 0)
        def _():
            dw_ref[...] += dw

    def col_q(g):
        return jnp.minimum(g, nqk - 1)

    def col_k(g):
        return jnp.clip(g - nqk, 0, nqk - 1)

    def col_v(g):
        return jnp.maximum(g - 2 * nqk, 0)

    def blk(colf):
        return pl.BlockSpec((tr, LANE), lambda g, r: (r, colf(g)))

    def nblk(colf):
        return pl.BlockSpec((8, LANE), lambda g, r: (jnp.minimum((r + 1) * (tr // 8), last8), colf(g)))

    return _pcall(
        body, name=name, grid=(groups, nr),
        in_specs=[pl.BlockSpec((tr, LANE), lambda g, r: (r, g)),
                  pl.BlockSpec((8, LANE), lambda g, r: (jnp.maximum(r * (tr // 8) - 1, 0), g)),
                  pl.BlockSpec((8, LANE), lambda g, r: (jnp.minimum((r + 1) * (tr // 8), last8), g)),
                  pl.BlockSpec((4, LANE), lambda g, r: (0, g)),
                  blk(col_q), blk(col_k), blk(col_v), nblk(col_q), nblk(col_k), nblk(col_v)],
        out_specs=[pl.BlockSpec((tr, LANE), lambda g, r: (r, g)),
                   pl.BlockSpec((4, LANE), lambda g, r: (0, g))],
        out_shape=[jax.ShapeDtypeStruct((lp, width), F32), jax.ShapeDtypeStruct((4, width), F32)],
        compiler_params=_params(2),
    )(x, x, x, w, dq, dk, dv, dq, dk, dv)


def _inv_impl(m):
    c = m.shape[0]
    ii = lax.broadcasted_iota(jnp.int32, (c, c), 0)
    jj = lax.broadcasted_iota(jnp.int32, (c, c), 1)
    eye = (ii == jj).astype(F32)

    def same_block(shift):
        return (ii >> shift) == (jj >> shift)

    n1 = jnp.where(same_block(3), -m, 0.0)
    n2 = _hdot(n1, n1)
    n4 = _hdot(n2, n2)
    d = _hdot(_hdot(eye + n1, eye + n2), eye + n4)
    shift = 3
    while (1 << shift) < c:
        low = jnp.where(jnp.logical_and(same_block(shift + 1), jnp.logical_not(same_block(shift))), m, 0.0)
        d = d - _hdot(d, _hdot(low, d))
        shift += 1
    return d


@jax.custom_vjp
def _inv_unit_lower(m):
    return _inv_impl(m)


def _inv_f(m):
    t = _inv_impl(m)
    return t, t


def _inv_b(t, dt):
    c = t.shape[0]
    ii = lax.broadcasted_iota(jnp.int32, (c, c), 0)
    jj = lax.broadcasted_iota(jnp.int32, (c, c), 1)
    gm = _hdot(t, _hdot(dt, t, _NT), _TN)
    return (jnp.where(ii > jj, -gm, 0.0),)


_inv_unit_lower.defvjp(_inv_f, _inv_b)


GDN_HEADS_PER_STEP = 4


def _gdn_head(q, k, kk, qk, v, beta_blk, gc_blk, state, h):
    c = q.shape[0]
    lane = lax.broadcasted_iota(jnp.int32, (1, LANE), 1)
    onehot = (lane == h).astype(F32)
    beta = jnp.sum(beta_blk * onehot, axis=1, keepdims=True)
    gcc = jnp.sum(gc_blk * onehot, axis=1, keepdims=True)
    gcr = _hdot(jnp.broadcast_to(onehot, (8, LANE)), gc_blk, _NT)[0:1]
    ii = lax.broadcasted_iota(jnp.int32, (c, c), 0)
    jj = lax.broadcasted_iota(jnp.int32, (c, c), 1)
    incl = ii >= jj
    dec = jnp.where(incl, jnp.exp(jnp.where(incl, gcc - gcr, 0.0)), 0.0)
    eg = jnp.exp(gcc)
    m = kk * beta * jnp.where(ii > jj, dec, 0.0)
    t = _inv_unit_lower(m)
    u = _hdot(t, v * beta)
    w = _hdot(t, k * (beta * eg))
    attn = qk * dec
    rows = lax.broadcasted_iota(jnp.int32, (c, 1), 0)
    gl = jnp.sum(jnp.where(rows == c - 1, gcc, 0.0), axis=0, keepdims=True)
    v_new = u - _fdot(w, state, _NN)
    o = _fdot(q * eg, state, _NN) + _fdot(attn, v_new, _NN)
    new_state = state * jnp.exp(gl) + _fdot(k * jnp.exp(gl - gcc), v_new, _TN)
    return o, new_state


def _gdn_group(q, k, v, beta_blk, gc_blk, states, h0):
    outs, new_states = [], []
    for p in range(GDN_HEADS_PER_STEP // 2):
        qp = q[:, p * LANE:(p + 1) * LANE]
        kp = k[:, p * LANE:(p + 1) * LANE]
        kk = _fdot(kp, kp, _NT)
        qk = _fdot(qp, kp, _NT)
        for e in range(2):
            i = 2 * p + e
            o, s2 = _gdn_head(qp, kp, kk, qk, v[:, i * LANE:(i + 1) * LANE], beta_blk, gc_blk, states[i], h0 + i)
            outs.append(o)
            new_states.append(s2)
    return jnp.concatenate(outs, axis=1), tuple(new_states)


def _gdn_specs(nc, rev):
    def cidx(n):
        return (nc - 1 - n) if rev else n
    hp = GDN_HEADS_PER_STEP
    nqk = GDN_QK_HEADS
    c = GDN_CHUNK
    nq = 2 * nqk // hp
    q_spec = pl.BlockSpec((c, hp // 2 * LANE), lambda n, g: (cidx(n), g))
    k_spec = pl.BlockSpec((c, hp // 2 * LANE), lambda n, g: (cidx(n), nq + g))
    v_spec = pl.BlockSpec((c, hp * LANE), lambda n, g: (cidx(n), nq + g))
    s_spec = pl.BlockSpec((c, LANE), lambda n, g: (cidx(n), 0))
    o_spec = pl.BlockSpec((c, hp * LANE), lambda n, g: (cidx(n), g))
    ck_spec = pl.BlockSpec((hp, 1, GDN_DK, LANE), lambda n, g: (g, cidx(n), 0, 0))
    return q_spec, k_spec, v_spec, s_spec, o_spec, ck_spec


def _gdn_fwd(qkv, beta, gc, name):
    lp = qkv.shape[0]
    nc = lp // GDN_CHUNK
    nh = GDN_V_HEADS
    hp = GDN_HEADS_PER_STEP
    q_spec, k_spec, v_spec, s_spec, o_spec, ck_spec = _gdn_specs(nc, False)

    def body(q_ref, k_ref, v_ref, b_ref, g_ref, o_ref, ck_ref, state):
        n = pl.program_id(0)
        g = pl.program_id(1)

        @pl.when(n == 0)
        def _():
            for i in range(hp):
                state[g * hp + i] = jnp.zeros((GDN_DK, LANE), F32)

        states = tuple(state[g * hp + i] for i in range(hp))
        for i in range(hp):
            ck_ref[i, 0] = states[i]
        o, new_states = _gdn_group(q_ref[...], k_ref[...], v_ref[...], b_ref[...], g_ref[...], states, g * hp)
        o_ref[...] = o
        for i in range(hp):
            state[g * hp + i] = new_states[i]

    return _pcall(
        body, name=name, grid=(nc, nh // hp),
        in_specs=[q_spec, k_spec, v_spec, s_spec, s_spec],
        out_specs=[o_spec, ck_spec],
        out_shape=[jax.ShapeDtypeStruct((lp, GDN_V_W), F32),
                   jax.ShapeDtypeStruct((nh, nc, GDN_DK, LANE), F32)],
        scratch_shapes=[pltpu.VMEM((nh, GDN_DK, LANE), F32)],
        compiler_params=_params(2),
    )(qkv, qkv, qkv, beta, gc)


def _gdn_bwd(qkv, beta, gc, ckpt, do, name):
    lp = qkv.shape[0]
    nc = lp // GDN_CHUNK
    nh = GDN_V_HEADS
    hp = GDN_HEADS_PER_STEP
    q_spec, k_spec, v_spec, s_spec, o_spec, ck_spec = _gdn_specs(nc, True)

    def body(q_ref, k_ref, v_ref, b_ref, g_ref, ck_ref, do_ref,
             dq_ref, dk_ref, dv_ref, db_ref, dg_ref, dstate):
        n = pl.program_id(0)
        g = pl.program_id(1)

        @pl.when(n == 0)
        def _():
            for i in range(hp):
                dstate[g * hp + i] = jnp.zeros((GDN_DK, LANE), F32)

        states = tuple(ck_ref[i, 0] for i in range(hp))
        _, vjp = jax.vjp(lambda q, k, v, b, gg, s: _gdn_group(q, k, v, b, gg, s, g * hp),
                         q_ref[...], k_ref[...], v_ref[...], b_ref[...], g_ref[...], states)
        dq, dk, dv, db, dg, ds = vjp((do_ref[...], tuple(dstate[g * hp + i] for i in range(hp))))
        dq_ref[...] = dq
        dk_ref[...] = dk
        dv_ref[...] = dv
        for i in range(hp):
            dstate[g * hp + i] = ds[i]

        @pl.when(g == 0)
        def _():
            db_ref[...] = db
            dg_ref[...] = dg

        @pl.when(g > 0)
        def _():
            db_ref[...] += db
            dg_ref[...] += dg

    qk_shape = jax.ShapeDtypeStruct((lp, GDN_QK_W), F32)
    big = jax.ShapeDtypeStruct((lp, GDN_V_W), F32)
    small = jax.ShapeDtypeStruct((lp, LANE), F32)
    dq_spec = pl.BlockSpec((GDN_CHUNK, hp // 2 * LANE), lambda n, g: (nc - 1 - n, g))
    return _pcall(
        body, name=name, grid=(nc, nh // hp),
        in_specs=[q_spec, k_spec, v_spec, s_spec, s_spec, ck_spec, o_spec],
        out_specs=[dq_spec, dq_spec, o_spec, s_spec, s_spec],
        out_shape=[qk_shape, qk_shape, big, small, small],
        scratch_shapes=[pltpu.VMEM((nh, GDN_DK, LANE), F32)],
        compiler_params=_params(2),
    )(qkv, qkv, qkv, beta, gc, ckpt, do)


LOG2E = 1.4426950408889634
LN2 = 0.6931471805599453
Q_PRESCALE = MLA_QK ** -0.5 * LOG2E


def _att_mask(i, j, tb, transposed):
    r = lax.broadcasted_iota(jnp.int32, (tb, tb), 0)
    c = lax.broadcasted_iota(jnp.int32, (tb, tb), 1)
    qpos, kpos = (i * tb + c, j * tb + r) if transposed else (i * tb + r, j * tb + c)
    return jnp.logical_and(kpos <= qpos, kpos >= FRONT)


def _masked_and_plain(i, j, step):
    edge = jnp.logical_or(j == i, j == 0)

    @pl.when(jnp.logical_and(edge, j <= i))
    def _():
        step(True)

    @pl.when(jnp.logical_and(jnp.logical_not(edge), j < i))
    def _():
        step(False)


def _cat(a_ref, b_ref):
    return jnp.concatenate([a_ref[...], b_ref[...]], axis=1)


def _head_lane(blk, h):
    lane = lax.broadcasted_iota(jnp.int32, (1, LANE), 1)
    return jnp.sum(jnp.where(lane == h, blk, 0.0), axis=1, keepdims=True)


def _flash_fwd(qn, qr, kn, kr, v, name, tb=ROW_ALIGN):
    lp = qn.shape[0]
    nb = lp // tb
    nh = MLA_HEADS

    def body(qn_ref, qr_ref, kn_ref, kr_ref, v_ref, o_ref, lse_ref, m_s, l_s, acc):
        i, h, j = pl.program_id(0), pl.program_id(1), pl.program_id(2)

        @pl.when(j == 0)
        def _():
            m_s[...] = jnp.full_like(m_s, NEG)
            l_s[...] = jnp.zeros_like(l_s)
            acc[...] = jnp.zeros_like(acc)

        def step(masked):
            s = _dot(_cat(qn_ref, qr_ref), _cat(kn_ref, kr_ref), _NT)
            if masked:
                s = jnp.where(_att_mask(i, j, tb, False), s, NEG)
            m_new = jnp.maximum(m_s[...], jnp.max(s, axis=1, keepdims=True))
            alpha = jnp.exp2(m_s[...] - m_new)
            p = jnp.exp2(s - m_new)
            l_s[...] = alpha * l_s[...] + jnp.sum(p, axis=1, keepdims=True)
            acc[...] = alpha * acc[...] + _dot(p.astype(BF16), v_ref[...], _NN)
            m_s[...] = m_new

        _masked_and_plain(i, j, step)

        @pl.when(j == nb - 1)
        def _():
            o_ref[...] = acc[...] / l_s[...]
            lane = lax.broadcasted_iota(jnp.int32, (1, LANE), 1)
            mine = jnp.where(lane == h, m_s[...] + jnp.log(l_s[...]) * LOG2E, 0.0)

            @pl.when(h == 0)
            def _():
                lse_ref[...] = mine

            @pl.when(h > 0)
            def _():
                lse_ref[...] += mine

    qspec = pl.BlockSpec((tb, LANE), lambda i, h, j: (i, h))
    kspec = pl.BlockSpec((tb, LANE), lambda i, h, j: (jnp.minimum(j, i), h))
    krspec = pl.BlockSpec((tb, LANE), lambda i, h, j: (jnp.minimum(j, i), 0))
    return _pcall(
        body, name=name, grid=(nb, nh, nb),
        in_specs=[qspec, qspec, kspec, krspec, kspec],
        out_specs=[qspec, pl.BlockSpec((tb, LANE), lambda i, h, j: (i, 0))],
        out_shape=[jax.ShapeDtypeStruct((lp, nh * LANE), F32), jax.ShapeDtypeStruct((lp, LANE), F32)],
        scratch_shapes=[pltpu.VMEM((tb, 1), F32), pltpu.VMEM((tb, 1), F32), pltpu.VMEM((tb, LANE), F32)],
        compiler_params=_params(3),
    )(qn, qr, kn, kr, v)


def _flash_dq(qn, qr, kn, kr, v, o, do, lse, name, tb=ROW_ALIGN):
    lp = qn.shape[0]
    nb = lp // tb
    nh = MLA_HEADS

    def body(qn_ref, qr_ref, kn_ref, kr_ref, v_ref, o_ref, do_ref, lse_ref,
             dqn_ref, dqr_ref, dq_acc, delta_s, lse_s):
        i, h, j = pl.program_id(0), pl.program_id(1), pl.program_id(2)

        @pl.when(j == 0)
        def _():
            dq_acc[...] = jnp.zeros_like(dq_acc)
            delta_s[...] = jnp.sum(do_ref[...] * o_ref[...], axis=1, keepdims=True)
            lse_s[...] = _head_lane(lse_ref[...], h)

        def step(masked):
            k = _cat(kn_ref, kr_ref)
            s = _dot(_cat(qn_ref, qr_ref), k, _NT)
            if masked:
                s = jnp.where(_att_mask(i, j, tb, False), s, NEG)
            p = jnp.exp2(s - lse_s[...])
            dp = _dot(do_ref[...].astype(BF16), v_ref[...], _NT)
            ds = p * (dp - delta_s[...])
            dq_acc[...] += _dot(ds.astype(BF16), k, _NN)

        _masked_and_plain(i, j, step)

        @pl.when(j == nb - 1)
        def _():
            dqn_ref[...] = dq_acc[:, :LANE] * LN2
            dqr_ref[...] = dq_acc[:, LANE:] * LN2

    qspec = pl.BlockSpec((tb, LANE), lambda i, h, j: (i, h))
    kspec = pl.BlockSpec((tb, LANE), lambda i, h, j: (jnp.minimum(j, i), h))
    krspec = pl.BlockSpec((tb, LANE), lambda i, h, j: (jnp.minimum(j, i), 0))
    lspec = pl.BlockSpec((tb, LANE), lambda i, h, j: (i, 0))
    shp = jax.ShapeDtypeStruct((lp, nh * LANE), F32)
    return _pcall(
        body, name=name, grid=(nb, nh, nb),
        in_specs=[qspec, qspec, kspec, krspec, kspec, qspec, qspec, lspec],
        out_specs=[qspec, qspec],
        out_shape=[shp, shp],
        scratch_shapes=[pltpu.VMEM((tb, 2 * LANE), F32), pltpu.VMEM((tb, 1), F32), pltpu.VMEM((tb, 1), F32)],
        compiler_params=_params(3),
    )(qn, qr, kn, kr, v, o, do, lse)


def _flash_dkv(qn, qr, kn, kr, v, o, do, lse, name, tb=ROW_ALIGN):
    lp = qn.shape[0]
    nb = lp // tb
    nh = MLA_HEADS

    def body(qn_ref, qr_ref, kn_ref, kr_ref, v_ref, o_ref, do_ref, lse_ref,
             dkn_ref, dkr_ref, dv_ref, dk_acc, dkr_acc, dv_acc):
        j, h, i = pl.program_id(0), pl.program_id(1), pl.program_id(2)

        @pl.when(i == 0)
        def _():
            dk_acc[...] = jnp.zeros_like(dk_acc)
            dv_acc[...] = jnp.zeros_like(dv_acc)

        @pl.when(jnp.logical_and(i == 0, h == 0))
        def _():
            dkr_acc[...] = jnp.zeros_like(dkr_acc)

        def step(masked):
            q = _cat(qn_ref, qr_ref)
            st = _dot(_cat(kn_ref, kr_ref), q, _NT)
            if masked:
                st = jnp.where(_att_mask(i, j, tb, True), st, NEG)
            do_blk = do_ref[...]
            lane = lax.broadcasted_iota(jnp.int32, (8, LANE), 1)
            lse_row = _hdot((lane == h).astype(F32), lse_ref[...], _NT)[0:1]
            delta_row = _hdot(jnp.ones((8, LANE), F32), do_blk * o_ref[...], _NT)[0:1]
            pt = jnp.exp2(st - lse_row)
            do_b = do_blk.astype(BF16)
            dv_acc[...] += _dot(pt.astype(BF16), do_b, _NN)
            dpt = _dot(v_ref[...], do_b, _NT)
            dst = pt * (dpt - delta_row)
            dk_acc[...] += _dot(dst.astype(BF16), q, _NN)

        _masked_and_plain(i, j, step)

        @pl.when(i == nb - 1)
        def _():
            dkn_ref[...] = dk_acc[:, :LANE] * LN2
            dv_ref[...] = dv_acc[...]
            dkr_acc[...] += dk_acc[:, LANE:]

            @pl.when(h == nh - 1)
            def _():
                dkr_ref[...] = dkr_acc[...] * LN2

    qspec = pl.BlockSpec((tb, LANE), lambda j, h, i: (jnp.maximum(i, j), h))
    lspec = pl.BlockSpec((tb, LANE), lambda j, h, i: (jnp.maximum(i, j), 0))
    kspec = pl.BlockSpec((tb, LANE), lambda j, h, i: (j, h))
    krspec = pl.BlockSpec((tb, LANE), lambda j, h, i: (j, 0))
    shp = jax.ShapeDtypeStruct((lp, nh * LANE), F32)
    return _pcall(
        body, name=name, grid=(nb, nh, nb),
        in_specs=[qspec, qspec, kspec, krspec, kspec, qspec, qspec, lspec],
        out_specs=[kspec, krspec, kspec],
        out_shape=[shp, jax.ShapeDtypeStruct((lp, LANE), F32), shp],
        scratch_shapes=[pltpu.VMEM((tb, 2 * LANE), F32), pltpu.VMEM((tb, LANE), F32), pltpu.VMEM((tb, LANE), F32)],
        compiler_params=_params(3),
    )(qn, qr, kn, kr, v, o, do, lse)


def _adamw(w, g, m, v, name):
    rows, width = w.shape
    tr = next((t for t in (512, 256, 128, 64, 32, 16, 8) if rows % t == 0), rows)

    def body(w_ref, g_ref, m_ref, v_ref, d_ref, nm_ref, nv_ref):
        gg = g_ref[...]
        nm = ADAM_B1 * m_ref[...] + (1.0 - ADAM_B1) * gg
        nv = ADAM_B2 * v_ref[...] + (1.0 - ADAM_B2) * jnp.square(gg)
        m_hat = nm / (1.0 - ADAM_B1 ** ADAM_STEP)
        v_hat = nv / (1.0 - ADAM_B2 ** ADAM_STEP)
        d_ref[...] = -ADAM_LR * (m_hat / (jnp.sqrt(v_hat) + ADAM_EPS) + ADAM_WD * w_ref[...])
        nm_ref[...] = nm
        nv_ref[...] = nv

    spec = pl.BlockSpec((tr, width), lambda r: (r, 0))
    shp = jax.ShapeDtypeStruct((rows, width), F32)
    return _pcall(body, name=name, grid=(rows // tr,), in_specs=[spec] * 4, out_specs=[spec] * 3,
                  out_shape=[shp] * 3, compiler_params=_params(1))(w, g, m, v)


def _add_pair(a, b, name):
    s, rows, width = a.shape
    tr = next(t for t in (512, 256, 128, 64, 32, 16, 8) if rows % t == 0)

    def body(a_ref, b_ref, o_ref):
        o_ref[...] = a_ref[...] + b_ref[...]

    spec = pl.BlockSpec((1, tr, width), lambda i, r: (i, r, 0))
    return _pcall(body, name=name, grid=(s, rows // tr), in_specs=[spec, spec], out_specs=spec,
                  out_shape=jax.ShapeDtypeStruct(a.shape, F32), compiler_params=_params(2))(a, b)


def _sum_slots(a, name):
    s, rows, width = a.shape
    tr = next(t for t in (512, 256, 128, 64, 32, 16, 8) if rows % t == 0)

    def body(a_ref, o_ref):
        tot = a_ref[0]
        for k in range(1, s):
            tot = tot + a_ref[k]
        o_ref[...] = tot

    return _pcall(body, name=name, grid=(rows // tr,),
                  in_specs=[pl.BlockSpec((s, tr, width), lambda r: (0, r, 0))],
                  out_specs=pl.BlockSpec((tr, width), lambda r: (r, 0)),
                  out_shape=jax.ShapeDtypeStruct((rows, width), F32), compiler_params=_params(1))(a)


_ANY = pl.BlockSpec(memory_space=pl.ANY)


def _my_place():
    return lax.axis_index("x"), lax.axis_index("y"), lax.axis_index("c")


def _other_chips(x, y):
    return [(1 - x, y), (x, 1 - y), (1 - x, 1 - y)]


def _gather_shards(flat, name):
    rows, width = flat.shape

    def body(x_ref, out_ref, send_sems, recv_sems, local_sem):
        x, y, c = _my_place()
        mine = pltpu.make_async_copy(x_ref, out_ref.at[2 * x + y], local_sem)
        mine.start()
        sends = []
        for k, (px, py) in enumerate(_other_chips(x, y)):
            cp = pltpu.make_async_remote_copy(
                src_ref=x_ref, dst_ref=out_ref.at[2 * x + y], send_sem=send_sems.at[k], recv_sem=recv_sems.at[k],
                device_id=(px, py, c), device_id_type=MESH)
            cp.start()
            sends.append(cp)
        for k, (px, py) in enumerate(_other_chips(x, y)):
            pltpu.make_async_remote_copy(
                src_ref=x_ref, dst_ref=out_ref.at[2 * px + py], send_sem=send_sems.at[k], recv_sem=recv_sems.at[k],
                device_id=(px, py, c), device_id_type=MESH).wait_recv()
        for cp in sends:
            cp.wait_send()
        mine.wait()

    return _pcall(
        body, name=name, in_specs=[_ANY], out_specs=_ANY,
        out_shape=jax.ShapeDtypeStruct((4, rows, width), flat.dtype),
        scratch_shapes=[pltpu.SemaphoreType.DMA((3,)), pltpu.SemaphoreType.DMA((3,)), pltpu.SemaphoreType.DMA],
    )(flat)


def _sibling_split(g, name):
    s, rows, width = g.shape
    half = rows // 2

    def body(g_ref, own_ref, got_ref, send_sem, recv_sem, local_sem):
        x, y, c = _my_place()
        mine = pltpu.make_async_copy(g_ref.at[:, pl.ds(pl.multiple_of(c * half, 8), half), :], own_ref, local_sem)
        mine.start()
        cp = pltpu.make_async_remote_copy(
            src_ref=g_ref.at[:, pl.ds(pl.multiple_of((1 - c) * half, 8), half), :], dst_ref=got_ref,
            send_sem=send_sem, recv_sem=recv_sem, device_id=(x, y, 1 - c), device_id_type=MESH)
        cp.start()
        cp.wait_recv()
        cp.wait_send()
        mine.wait()

    shp = jax.ShapeDtypeStruct((s, half, width), g.dtype)
    return _pcall(
        body, name=name, in_specs=[_ANY], out_specs=[_ANY, _ANY], out_shape=[shp, shp],
        scratch_shapes=[pltpu.SemaphoreType.DMA, pltpu.SemaphoreType.DMA, pltpu.SemaphoreType.DMA],
    )(g)


def _chip_scatter(p, name):
    s, rows, width = p.shape

    def body(p_ref, out_ref, send_sems, recv_sems, local_sem):
        x, y, c = _my_place()
        me = 2 * x + y
        mine = pltpu.make_async_copy(p_ref.at[me], out_ref.at[me], local_sem)
        mine.start()
        sends = []
        for k, (px, py) in enumerate(_other_chips(x, y)):
            cp = pltpu.make_async_remote_copy(
                src_ref=p_ref.at[2 * px + py], dst_ref=out_ref.at[me], send_sem=send_sems.at[k],
                recv_sem=recv_sems.at[k], device_id=(px, py, c), device_id_type=MESH)
            cp.start()
            sends.append(cp)
        for k, (px, py) in enumerate(_other_chips(x, y)):
            pltpu.make_async_remote_copy(
                src_ref=p_ref.at[me], dst_ref=out_ref.at[2 * px + py], send_sem=send_sems.at[k],
                recv_sem=recv_sems.at[k], device_id=(px, py, c), device_id_type=MESH).wait_recv()
        for cp in sends:
            cp.wait_send()
        mine.wait()

    return _pcall(
        body, name=name, in_specs=[_ANY], out_specs=_ANY,
        out_shape=jax.ShapeDtypeStruct(p.shape, p.dtype),
        scratch_shapes=[pltpu.SemaphoreType.DMA((3,)), pltpu.SemaphoreType.DMA((3,)), pltpu.SemaphoreType.DMA],
    )(p)


def _sibling_join(qh, name):
    half, width = qh.shape

    def body(q_ref, out_ref, send_sem, recv_sem, local_sem):
        x, y, c = _my_place()
        my_rows = pl.ds(pl.multiple_of(c * half, 8), half)
        sib_rows = pl.ds(pl.multiple_of((1 - c) * half, 8), half)
        mine = pltpu.make_async_copy(q_ref, out_ref.at[my_rows, :], local_sem)
        mine.start()
        cp = pltpu.make_async_remote_copy(
            src_ref=q_ref, dst_ref=out_ref.at[my_rows, :], send_sem=send_sem, recv_sem=recv_sem,
            device_id=(x, y, 1 - c), device_id_type=MESH)
        cp.start()
        pltpu.make_async_remote_copy(
            src_ref=q_ref, dst_ref=out_ref.at[sib_rows, :], send_sem=send_sem, recv_sem=recv_sem,
            device_id=(x, y, 1 - c), device_id_type=MESH).wait_recv()
        cp.wait_send()
        mine.wait()

    return _pcall(
        body, name=name, in_specs=[_ANY], out_specs=_ANY,
        out_shape=jax.ShapeDtypeStruct((2 * half, width), qh.dtype),
        scratch_shapes=[pltpu.SemaphoreType.DMA, pltpu.SemaphoreType.DMA, pltpu.SemaphoreType.DMA],
    )(qh)


def _all_sum_small(part, name):
    rows, width = part.shape

    def body(p_ref, out_ref, land, send_sems, recv_sems):
        x, y, c = _my_place()
        me = 4 * x + 2 * y + c
        land[me] = p_ref[...]
        sends = []
        for k in range(1, 8):
            peer = (x ^ (k >> 2), y ^ ((k >> 1) & 1), c ^ (k & 1))
            cp = pltpu.make_async_remote_copy(
                src_ref=p_ref, dst_ref=land.at[me], send_sem=send_sems.at[k - 1], recv_sem=recv_sems.at[k - 1],
                device_id=peer, device_id_type=MESH)
            cp.start()
            sends.append(cp)
        for k in range(1, 8):
            px, py, pc = x ^ (k >> 2), y ^ ((k >> 1) & 1), c ^ (k & 1)
            pltpu.make_async_remote_copy(
                src_ref=p_ref, dst_ref=land.at[4 * px + 2 * py + pc], send_sem=send_sems.at[k - 1],
                recv_sem=recv_sems.at[k - 1], device_id=(px, py, pc), device_id_type=MESH).wait_recv()
        for cp in sends:
            cp.wait_send()
        tot = land[0]
        for k in range(1, 8):
            tot = tot + land[k]
        out_ref[...] = tot

    vmem = pl.BlockSpec(memory_space=pltpu.VMEM)
    return _pcall(
        body, name=name, in_specs=[vmem], out_specs=vmem,
        out_shape=jax.ShapeDtypeStruct((rows, width), F32),
        scratch_shapes=[pltpu.VMEM((8, rows, width), F32), pltpu.SemaphoreType.DMA((7,)),
                        pltpu.SemaphoreType.DMA((7,))],
    )(part)


def _big_layout(shards):
    return [(a.shape[0], a.shape[1], ax) for a, ax in shards]


def _pack_shards(arrs):
    flat = jnp.concatenate([a.reshape(-1) for a in arrs])
    assert flat.shape[0] % (16 * LANE) == 0, flat.shape
    return flat.reshape(-1, LANE)


def _unpack_shards(flat, layout):
    flat = flat.reshape(-1)
    out, off = [], 0
    for r, c, _ in layout:
        out.append(flat[off:off + r * c].reshape(r, c))
        off += r * c
    return out


def _unpack_full(gathered, layout):
    g = gathered.reshape(4, -1)
    out, off = [], 0
    for r, c, ax in layout:
        seg = g[:, off:off + r * c].reshape(4, r, c)
        out.append(seg.transpose(1, 0, 2).reshape(r, 4 * c) if ax == 1 else seg.reshape(4 * r, c))
        off += r * c
    return out


def _pack_full(fulls, layout):
    parts = []
    for a, (r, c, ax) in zip(fulls, layout):
        if ax == 1:
            parts.append(a.reshape(r, 4, c).transpose(1, 0, 2).reshape(4, r * c))
        else:
            parts.append(a.reshape(4, r * c))
    return jnp.concatenate(parts, axis=1).reshape(4, -1, LANE)


def _pad_lanes(a, width=LANE):
    return jnp.pad(a, [(0, 0)] * (a.ndim - 1) + [(0, width - a.shape[-1])])


def _pack_small(arrs):
    rows = [_pad_lanes(a.reshape(1, -1), -(-a.size // LANE) * LANE).reshape(-1, LANE) for a in arrs]
    flat = jnp.concatenate(rows, axis=0)
    return jnp.pad(flat, ((0, -flat.shape[0] % 8), (0, 0)))


def _unpack_small(flat, shapes):
    out, off = [], 0
    for shp in shapes:
        n = math.prod(shp)
        nr = -(-n // LANE)
        out.append(flat[off:off + nr].reshape(-1)[:n].reshape(shp))
        off += nr
    return out


def kernel(x, meta_tokens, pre_norm, post_norm, gdn_w_in, gdn_conv_w, gdn_a_log, gdn_dt_bias, gdn_out_norm, gdn_w_out, kv_norm, kv_w_down, kv_latent_norm, kv_w_up, mla_w_in, mla_q_latent_norm, mla_w_q_up, mla_w_out, loss_target, m_meta_tokens, m_pre_norm, m_post_norm, m_gdn_w_in, m_gdn_conv_w, m_gdn_a_log, m_gdn_dt_bias, m_gdn_out_norm, m_gdn_w_out, m_kv_norm, m_kv_w_down, m_kv_latent_norm, m_kv_w_up, m_mla_w_in, m_mla_q_latent_norm, m_mla_w_q_up, m_mla_w_out, v_meta_tokens, v_pre_norm, v_post_norm, v_gdn_w_in, v_gdn_conv_w, v_gdn_a_log, v_gdn_dt_bias, v_gdn_out_norm, v_gdn_w_out, v_kv_norm, v_kv_w_down, v_kv_latent_norm, v_kv_w_up, v_mla_w_in, v_mla_q_latent_norm, v_mla_w_q_up, v_mla_w_out):
    seq = x.shape[1]
    d = D_MODEL
    lp = -(-(ROW0 + seq) // ROW_ALIGN) * ROW_ALIGN
    tail = lp - ROW0 - seq

    big_names = ["meta_tokens", "gdn_w_in", "gdn_conv_w", "gdn_w_out", "kv_w_down", "kv_w_up", "mla_w_in",
                 "mla_w_q_up", "mla_w_out"]
    big_axis = [1, 1, 1, 0, 0, 1, 1, 1, 0]
    big_w = [meta_tokens, gdn_w_in[0], gdn_conv_w[0], gdn_w_out[0], kv_w_down, kv_w_up, mla_w_in[0], mla_w_q_up[0],
             mla_w_out[0]]
    big_m = [m_meta_tokens, m_gdn_w_in[0], m_gdn_conv_w[0], m_gdn_w_out[0], m_kv_w_down, m_kv_w_up, m_mla_w_in[0],
             m_mla_w_q_up[0], m_mla_w_out[0]]
    big_v = [v_meta_tokens, v_gdn_w_in[0], v_gdn_conv_w[0], v_gdn_w_out[0], v_kv_w_down, v_kv_w_up, v_mla_w_in[0],
             v_mla_w_q_up[0], v_mla_w_out[0]]
    layout = _big_layout(list(zip(big_w, big_axis)))
    w_flat = _pack_shards(big_w)
    (meta_f, w_in0, conv_w, w_out0, kv_down, kv_up, w_in1, w_qup, w_out1) = _unpack_full(
        _gather_shards(w_flat, "gather_weights"), layout)

    nv = GDN_V_HEADS
    w_qkv = w_in0[:, :GDN_CONV_W]
    w_z0 = w_in0[:, GDN_CONV_W:GDN_CONV_W + GDN_V_W]
    w_b = _pad_lanes(w_in0[:, GDN_CONV_W + GDN_V_W:GDN_CONV_W + GDN_V_W + nv])
    w_a = _pad_lanes(w_in0[:, GDN_CONV_W + GDN_V_W + nv:])
    w_ckv = kv_down[:, :MLA_KV_RANK]
    w_kr = _pad_lanes(kv_down[:, MLA_KV_RANK:])
    kvu = kv_up.reshape(MLA_KV_RANK, MLA_HEADS, 2 * LANE)
    w_kn = kvu[:, :, :LANE].reshape(MLA_KV_RANK, MLA_HEADS * LANE)
    w_v = kvu[:, :, LANE:].reshape(MLA_KV_RANK, MLA_HEADS * LANE)
    w_cq = w_in1[:, :MLA_Q_RANK]
    w_z1 = w_in1[:, MLA_Q_RANK:]
    qu = w_qup.reshape(MLA_Q_RANK, MLA_HEADS, MLA_QK)
    w_qn = qu[:, :, :MLA_NOPE].reshape(MLA_Q_RANK, MLA_HEADS * LANE) * Q_PRESCALE
    w_qr = _pad_lanes(qu[:, :, MLA_NOPE:]).reshape(MLA_Q_RANK, MLA_HEADS * LANE) * Q_PRESCALE

    pre0, pre1 = pre_norm[0:1], pre_norm[1:2]
    post0, post1 = post_norm[0:1], post_norm[1:2]
    a_log = _pad_lanes(gdn_a_log)
    dt_bias = _pad_lanes(gdn_dt_bias)
    kvn = kv_norm.reshape(1, d)
    kvl = kv_latent_norm.reshape(1, MLA_KV_RANK)
    qln = mla_q_latent_norm

    h0 = jnp.concatenate([jnp.zeros((FRONT, d), F32), meta_f, x[0], jnp.zeros((tail, d), F32)], axis=0)
    tgt = jnp.pad(loss_target[0], ((ROW0, tail), (0, 0)))
    pos = jnp.maximum(jnp.arange(lp, dtype=jnp.int32) - FRONT, 0).astype(F32)
    inv = ROPE_THETA ** (-jnp.arange(0, MLA_ROPE, 2, dtype=F32) / MLA_ROPE)
    ang = pos[:, None] * inv[None, :]
    zeros64 = jnp.zeros((lp, LANE - MLA_ROPE), F32)
    cos_t = jnp.concatenate([jnp.cos(ang), jnp.cos(ang), zeros64], axis=1)
    sin_t = jnp.concatenate([-jnp.sin(ang), jnp.sin(ang), zeros64], axis=1)

    def valid_rows(ridx):
        return jnp.logical_and(ridx >= FRONT, ridx < ROW0 + seq)

    def f_pre0(ridx, g, h, gain):
        return _rms(h, gain), h

    (hn0,) = _rowwise("pre0", lambda *a: f_pre0(*a)[:1], [_In(h0), _In(pre0, "const")],
                      [_Out("row", (lp, d), BF16)])
    qkv_raw = _mm(hn0, w_qkv, "nn", "gdn_in_qkv")
    z0 = _mm(hn0, w_z0, "nn", "gdn_in_z")
    b_raw = _mm(hn0, w_b, "nn", "gdn_in_b")
    a_raw = _mm(hn0, w_a, "nn", "gdn_in_a")

    def f_ba(ridx, g, b, a, alog, dtb):
        tr = b.shape[0]
        ok = valid_rows(ridx).astype(F32)
        beta = jax.nn.sigmoid(b) * ok
        gate = -jnp.exp(alog) * _softplus(a + dtb) * ok
        ii = lax.broadcasted_iota(jnp.int32, (tr, tr), 0)
        jj = lax.broadcasted_iota(jnp.int32, (tr, tr), 1)
        tri = jnp.logical_and((ii >> 6) == (jj >> 6), ii >= jj).astype(F32)
        return beta, _hdot(tri, gate)

    ba_ins = [_In(b_raw), _In(a_raw), _In(a_log, "const"), _In(dt_bias, "const")]
    beta, gc = _rowwise("gdn_gates", f_ba, ba_ins, [_Out("row", (lp, LANE)), _Out("row", (lp, LANE))])
    qkv = _conv_fwd(qkv_raw, conv_w, "gdn_conv")
    o0, ckpt = _gdn_fwd(qkv, beta, gc, "gdn_scan")

    def f_gate0(ridx, g, o, z, gain):
        return (_rms(o, gain) * _silu(z),)

    gate0_ins = [_In(o0, grouped=True), _In(z0, grouped=True), _In(gdn_out_norm, "const")]
    (gated0,) = _rowwise("gdn_gate", f_gate0, gate0_ins, [_Out("row", (lp, GDN_V_W), BF16, grouped=True)],
                         groups=nv, tr=640)
    y0 = _mm(gated0, w_out0, "nn", "gdn_out")

    def f_mid(ridx, g, h, y, g_post, g_pre, g_kv):
        h1 = h + _rms(y, g_post)
        return h1, _rms(h1, g_pre), _rms(h1, g_kv)

    mid_ins = [_In(h0), _In(y0), _In(post0, "const"), _In(pre1, "const"), _In(kvn, "const")]
    h1, hn1, hkv = _rowwise("mid", f_mid, mid_ins,
                            [_Out("row", (lp, d)), _Out("row", (lp, d), BF16), _Out("row", (lp, d), BF16)])

    ckv_raw = _mm(hkv, w_ckv, "nn", "kv_down_c")
    kr_raw = _mm(hkv, w_kr, "nn", "kv_down_r")

    def f_ckv(ridx, g, c, r, cs, sn, gain):
        return _rms(c, gain), _rope(r, cs, sn)

    ckv_ins = [_In(ckv_raw), _In(kr_raw), _In(cos_t), _In(sin_t), _In(kvl, "const")]
    ckv, kr = _rowwise("kv_latent", f_ckv, ckv_ins, [_Out("row", (lp, LANE)), _Out("row", (lp, LANE), BF16)],
                       tr=640)
    kn = _mm(ckv, w_kn, "nn", "kv_up_k", BF16)
    vv = _mm(ckv, w_v, "nn", "kv_up_v", BF16)
    cq_raw = _mm(hn1, w_cq, "nn", "mla_in_q")
    z1 = _mm(hn1, w_z1, "nn", "mla_in_z")

    def f_cq(ridx, g, c, gain):
        return (_rms(c, gain),)

    cq_ins = [_In(cq_raw), _In(qln, "const")]
    (cq,) = _rowwise("q_latent", f_cq, cq_ins, [_Out("row", (lp, MLA_Q_RANK))], tr=640)
    qn = _mm(cq, w_qn, "nn", "q_up_n", BF16)
    qr_raw = _mm(cq, w_qr, "nn", "q_up_r")

    def f_qrope(ridx, g, r, cs, sn):
        return (_rope(r, cs, sn),)

    qr_ins = [_In(qr_raw, grouped=True), _In(cos_t), _In(sin_t)]
    (qr,) = _rowwise("q_rope", f_qrope, qr_ins, [_Out("row", (lp, MLA_HEADS * LANE), BF16, grouped=True)],
                     groups=MLA_HEADS, tr=640)
    o1, lse = _flash_fwd(qn, qr, kn, kr, vv, "attention")

    def f_gate1(ridx, g, o, z):
        return (o * _silu(z),)

    gate1_ins = [_In(o1), _In(z1)]
    (og,) = _rowwise("mla_gate", f_gate1, gate1_ins, [_Out("row", (lp, MLA_HEADS * LANE), BF16)])
    y1 = _mm(og, w_out1, "nn", "mla_out")

    def f_final(ridx, g, h, y, t, gain):
        ok = jnp.logical_and(ridx >= ROW0, ridx < ROW0 + seq).astype(F32)

        def rows_loss(h_, y_, gain_):
            err = (h_ + _rms(y_, gain_) - t) * ok
            return 0.5 * jnp.sum(jnp.sum(err * err, axis=1, keepdims=True), axis=0, keepdims=True) / d

        val, vjp = jax.vjp(rows_loss, h, y, gain)
        dh, dy, dgain = vjp(jnp.ones((1, 1), F32))
        return dh, dy, dgain, jnp.broadcast_to(val, (1, LANE))

    dh2, dy1, dpost1, loss_part = _rowwise(
        "loss_head", f_final, [_In(h1), _In(y1), _In(tgt), _In(post1, "const")],
        [_Out("row", (lp, d)), _Out("row", (lp, d)), _Out("acc", (1, d)), _Out("acc", (1, LANE))])

    dog = _mm(dy1, w_out1, "nt", "mla_out_dx")
    dw_out1 = _mm(og, dy1, "tn", "mla_out_dw")
    do1, dz1 = _rowwise_vjp("mla_gate_bwd", f_gate1, gate1_ins, [[dog]], [0, 1])
    dqn, dqr = _flash_dq(qn, qr, kn, kr, vv, o1, do1, lse, "attention_dq")
    dkn, dkr, dvv = _flash_dkv(qn, qr, kn, kr, vv, o1, do1, lse, "attention_dkv")
    (dqr_raw,) = _rowwise_vjp("q_rope_bwd", f_qrope, qr_ins, [[dqr]], [0], groups=MLA_HEADS, tr=640)
    dcq_a = _mm(dqn, w_qn, "nt", "q_up_n_dx")
    dcq_b = _mm(dqr_raw, w_qr, "nt", "q_up_r_dx")
    dw_qn = _mm(cq, dqn, "tn", "q_up_n_dw") * Q_PRESCALE
    dw_qr = _mm(cq, dqr_raw, "tn", "q_up_r_dw") * Q_PRESCALE
    dcq_raw, dqln = _rowwise_vjp("q_latent_bwd", f_cq, cq_ins, [[dcq_a, dcq_b]], [0, 1], tr=640)
    dhn1_a = _mm(dcq_raw, w_cq, "nt", "mla_in_q_dx")
    dhn1_b = _mm(dz1, w_z1, "nt", "mla_in_z_dx")
    dw_cq = _mm(hn1, dcq_raw, "tn", "mla_in_q_dw")
    dw_z1 = _mm(hn1, dz1, "tn", "mla_in_z_dw")
    dckv_a = _mm(dkn, w_kn, "nt", "kv_up_k_dx")
    dckv_b = _mm(dvv, w_v, "nt", "kv_up_v_dx")
    dw_kn = _mm(ckv, dkn, "tn", "kv_up_k_dw")
    dw_v = _mm(ckv, dvv, "tn", "kv_up_v_dw")
    dckv_raw, dkr_raw, dkvl = _rowwise_vjp("kv_latent_bwd", f_ckv, ckv_ins, [[dckv_a, dckv_b], [dkr]], [0, 1, 4],
                                           tr=640)
    dhkv_a = _mm(dckv_raw, w_ckv, "nt", "kv_down_c_dx")
    dhkv_b = _mm(dkr_raw, w_kr, "nt", "kv_down_r_dx")
    dw_ckv = _mm(hkv, dckv_raw, "tn", "kv_down_c_dw")
    dw_kr = _mm(hkv, dkr_raw, "tn", "kv_down_r_dw")
    dh0_res, dy0, dpost0, dpre1, dkvn = _rowwise_vjp(
        "mid_bwd", f_mid, mid_ins, [[dh2], [dhn1_a, dhn1_b], [dhkv_a, dhkv_b]], [0, 1, 2, 3, 4])

    dgated0 = _mm(dy0, w_out0, "nt", "gdn_out_dx")
    dw_out0 = _mm(gated0, dy0, "tn", "gdn_out_dw")
    do0, dz0, doutn = _rowwise_vjp("gdn_gate_bwd", f_gate0, gate0_ins, [[dgated0]], [0, 1, 2], groups=nv, tr=640)
    dq0, dk0, dv0, dbeta, dgc = _gdn_bwd(qkv, beta, gc, ckpt, do0, "gdn_scan_bwd")
    db_raw, da_raw, dalog, ddtb = _rowwise_vjp("gdn_gates_bwd", f_ba, ba_ins, [[dbeta], [dgc]], [0, 1, 2, 3])
    dqkv_raw, dconv = _conv_bwd(qkv_raw, conv_w, dq0, dk0, dv0, "gdn_conv_bwd")
    dhn0_a = _mm(dqkv_raw, w_qkv, "nt", "gdn_in_qkv_dx")
    dhn0_b = _mm(dz0, w_z0, "nt", "gdn_in_z_dx")
    dhn0_c = _mm(db_raw, w_b, "nt", "gdn_in_b_dx")
    dhn0_d = _mm(da_raw, w_a, "nt", "gdn_in_a_dx")
    dw_qkv = _mm(hn0, dqkv_raw, "tn", "gdn_in_qkv_dw")
    dw_z0 = _mm(hn0, dz0, "tn", "gdn_in_z_dw")
    dw_b = _mm(hn0, db_raw, "tn", "gdn_in_b_dw")
    dw_a = _mm(hn0, da_raw, "tn", "gdn_in_a_dw")
    dh0, dpre0 = _rowwise_vjp("pre0_bwd", f_pre0, [_In(h0), _In(pre0, "const")],
                              [[dhn0_a, dhn0_b, dhn0_c, dhn0_d], [dh0_res]], [0, 1])

    grad_x = dh0[ROW0:ROW0 + seq][None]
    g_meta = dh0[FRONT:ROW0]
    g_w_in0 = jnp.concatenate([dw_qkv, dw_z0, dw_b[:, :nv], dw_a[:, :nv]], axis=1)
    g_kv_down = jnp.concatenate([dw_ckv, dw_kr[:, :MLA_ROPE]], axis=1)
    g_kv_up = jnp.concatenate([dw_kn.reshape(MLA_KV_RANK, MLA_HEADS, LANE), dw_v.reshape(MLA_KV_RANK, MLA_HEADS, LANE)],
                              axis=2).reshape(MLA_KV_RANK, MLA_HEADS * 2 * LANE)
    g_w_in1 = jnp.concatenate([dw_cq, dw_z1], axis=1)
    g_qup = jnp.concatenate([dw_qn.reshape(MLA_Q_RANK, MLA_HEADS, LANE),
                             dw_qr.reshape(MLA_Q_RANK, MLA_HEADS, LANE)[:, :, :MLA_ROPE]],
                            axis=2).reshape(MLA_Q_RANK, MLA_HEADS * MLA_QK)
    big_g = [g_meta, g_w_in0, dconv, dw_out0, g_kv_down, g_kv_up, g_w_in1, g_qup, dw_out1]

    g_all = _pack_full(big_g, layout)
    own, got = _sibling_split(g_all, "grads_sibling_split")
    chip_part = _add_pair(own, got, "grads_chip_sum")
    from_chips = _chip_scatter(chip_part, "grads_chip_scatter")
    half_sum = _sum_slots(from_chips, "grads_total")
    g_flat = _sibling_join(half_sum, "grads_sibling_join")

    small_shapes = [(2, d), (2, d), (1, nv), (1, nv), (1, GDN_DK), (d,), (MLA_KV_RANK,), (1, MLA_Q_RANK), (1, LANE)]
    small_part = _pack_small([jnp.concatenate([dpre0, dpre1], axis=0), jnp.concatenate([dpost0, dpost1], axis=0),
                              dalog[:, :nv], ddtb[:, :nv], doutn, dkvn, dkvl, dqln, loss_part])
    small_tot = _all_sum_small(small_part, "small_sum")
    small_g = _unpack_small(small_tot, small_shapes)
    loss = small_g[-1][0, 0]

    d_flat, m_flat, v_flat = _adamw(w_flat, g_flat, _pack_shards(big_m), _pack_shards(big_v), "adamw_sharded")
    small_w = [pre_norm, post_norm, gdn_a_log, gdn_dt_bias, gdn_out_norm, kv_norm, kv_latent_norm, mla_q_latent_norm]
    small_m = [m_pre_norm, m_post_norm, m_gdn_a_log, m_gdn_dt_bias, m_gdn_out_norm, m_kv_norm, m_kv_latent_norm,
               m_mla_q_latent_norm]
    small_v = [v_pre_norm, v_post_norm, v_gdn_a_log, v_gdn_dt_bias, v_gdn_out_norm, v_kv_norm, v_kv_latent_norm,
               v_mla_q_latent_norm]
    g_small_flat = _pack_small(small_g[:-1])
    ds_flat, ms_flat, vs_flat = _adamw(_pack_small(small_w), g_small_flat, _pack_small(small_m), _pack_small(small_v),
                                       "adamw_replicated")

    def assemble(big_flat, small_flat):
        bigs = dict(zip(big_names, [a.reshape(w.shape) for a, w in zip(
            _unpack_shards(big_flat, layout),
            [meta_tokens, gdn_w_in, gdn_conv_w, gdn_w_out, kv_w_down, kv_w_up, mla_w_in, mla_w_q_up, mla_w_out])]))
        smalls = dict(zip(["pre_norm", "post_norm", "gdn_a_log", "gdn_dt_bias", "gdn_out_norm", "kv_norm",
                           "kv_latent_norm", "mla_q_latent_norm"], _unpack_small(small_flat, small_shapes[:-1])))
        both = {**bigs, **smalls}
        order = ["meta_tokens", "pre_norm", "post_norm", "gdn_w_in", "gdn_conv_w", "gdn_a_log", "gdn_dt_bias",
                 "gdn_out_norm", "gdn_w_out", "kv_norm", "kv_w_down", "kv_latent_norm", "kv_w_up", "mla_w_in",
                 "mla_q_latent_norm", "mla_w_q_up", "mla_w_out"]
        return [both[n] for n in order]

    grads = assemble(g_flat, g_small_flat)
    deltas = assemble(d_flat, ds_flat)
    new_m = assemble(m_flat, ms_flat)
    new_v = assemble(v_flat, vs_flat)
    return (loss, grad_x, *grads, *deltas, *new_m, *new_v)
```

```python
import functools
import math

import jax
import jax.numpy as jnp
from jax import lax
from jax.experimental import pallas as pl
from jax.experimental.pallas import tpu as pltpu

F32 = jnp.float32
BF16 = jnp.bfloat16
MESH = pl.DeviceIdType.MESH

D_MODEL = 1024
N_META = 16
FRONT = 48
ROW0 = FRONT + N_META
ROW_ALIGN = 640
NORM_EPS = 1e-6
LANE = 128

GDN_QK_HEADS = 8
GDN_V_HEADS = 16
GDN_DK = 128
GDN_CHUNK = 64
GDN_QK_W = 1024
GDN_V_W = 2048
GDN_CONV_W = 4096

MLA_HEADS = 16
MLA_NOPE = 128
MLA_ROPE = 64
MLA_QK = 192
MLA_Q_RANK = 256
MLA_KV_RANK = 128
ROPE_THETA = 10000.0

ADAM_LR = 0.001
ADAM_B1 = 0.9
ADAM_B2 = 0.999
ADAM_EPS = 1e-08
ADAM_WD = 0.01
ADAM_STEP = 10

VMEM_LIMIT_V7X = 56 * 1024 * 1024
NEG = -1e30

_NN = ((1,), (0,))
_NT = ((1,), (1,))
_TN = ((0,), (0,))
_HI = lax.Precision.HIGHEST
_X3 = lax.Precision.HIGH


def _pcall(body, **kw):
    return pl.pallas_call(body, **kw)


def _params(n_axes):
    return pltpu.CompilerParams(dimension_semantics=("arbitrary",) * n_axes, vmem_limit_bytes=VMEM_LIMIT_V7X)


def _dot(a, b, dims, prec=None):
    return lax.dot_general(a, b, (dims, ((), ())), precision=prec, preferred_element_type=F32)


def _bdot(a, b, dims):
    return _dot(a.astype(BF16), b.astype(BF16), dims)


def _hdot(a, b, dims=_NN):
    return _dot(a, b, dims, _HI)


def _fdot(a, b, dims):
    return _dot(a, b, dims)


def _tile(n):
    for t in (640, 512, 256, 128):
        if n % t == 0:
            return t
    raise ValueError(n)


def _mm(a, b, mode, name, out_dtype=F32):
    if mode == "nn":
        (m, k), (k2, n) = a.shape, b.shape
    elif mode == "nt":
        (m, k), (n, k2) = a.shape, b.shape
    else:
        (k, m), (k2, n) = a.shape, b.shape
    assert k == k2, (a.shape, b.shape, mode)
    tm, tn, tk = _tile(m), _tile(n), _tile(k)
    nk = k // tk
    dims = {"nn": _NN, "nt": _NT, "tn": _TN}[mode]

    def body(a_ref, b_ref, o_ref, acc):
        kk = pl.program_id(2)

        @pl.when(kk == 0)
        def _():
            acc[...] = jnp.zeros_like(acc)

        acc[...] += _bdot(a_ref[...], b_ref[...], dims)

        @pl.when(kk == nk - 1)
        def _():
            o_ref[...] = acc[...].astype(out_dtype)

    if mode == "tn":
        a_spec = pl.BlockSpec((tk, tm), lambda i, j, kk: (kk, i))
    else:
        a_spec = pl.BlockSpec((tm, tk), lambda i, j, kk: (i, kk))
    if mode == "nt":
        b_spec = pl.BlockSpec((tn, tk), lambda i, j, kk: (j, kk))
    else:
        b_spec = pl.BlockSpec((tk, tn), lambda i, j, kk: (kk, j))
    return _pcall(
        body, name=name, grid=(m // tm, n // tn, nk),
        in_specs=[a_spec, b_spec],
        out_specs=pl.BlockSpec((tm, tn), lambda i, j, kk: (i, j)),
        out_shape=jax.ShapeDtypeStruct((m, n), out_dtype),
        scratch_shapes=[pltpu.VMEM((tm, tn), F32)],
        compiler_params=_params(3),
    )(a, b)


class _In:
    def __init__(self, arr, kind="row", grouped=False, goff=0):
        self.arr, self.kind, self.grouped, self.goff = arr, kind, grouped, goff


class _Out:
    def __init__(self, kind, shape, dtype=F32, grouped=False):
        self.kind, self.shape, self.dtype, self.grouped = kind, shape, dtype, grouped


def _rowwise(name, fn, ins, outs, *, groups=1, tr=320):
    lp = next(i.arr.shape[0] for i in ins if i.kind == "row")
    nr = lp // tr
    assert lp % tr == 0

    def in_spec(i):
        w = i.arr.shape[1]
        if i.kind == "row":
            if i.grouped:
                return pl.BlockSpec((tr, LANE), lambda g, r, o=i.goff: (r, g + o))
            return pl.BlockSpec((tr, w), lambda g, r: (r, 0))
        if i.grouped:
            return pl.BlockSpec((i.arr.shape[0], LANE), lambda g, r, o=i.goff: (0, g + o))
        return pl.BlockSpec(i.arr.shape, lambda g, r: (0, 0))

    def out_spec(o):
        if o.kind == "row":
            if o.grouped:
                return pl.BlockSpec((tr, LANE), lambda g, r: (r, g))
            assert groups == 1
            return pl.BlockSpec((tr, o.shape[1]), lambda g, r: (r, 0))
        if o.grouped:
            return pl.BlockSpec((o.shape[0], LANE), lambda g, r: (0, g))
        return pl.BlockSpec(o.shape, lambda g, r: (0, 0))

    n_in = len(ins)

    def body(*refs):
        g = pl.program_id(0)
        r = pl.program_id(1)
        ridx = r * tr + lax.broadcasted_iota(jnp.int32, (tr, 1), 0)
        res = fn(ridx, g, *[ref[...] for ref in refs[:n_in]])
        assert len(res) == len(outs), (name, len(res), len(outs))
        for o, ref, val in zip(outs, refs[n_in:], res):
            if o.kind == "row":
                ref[...] = val.astype(o.dtype)
            else:
                first = (r == 0) if o.grouped else jnp.logical_and(r == 0, g == 0)

                @pl.when(first)
                def _(ref=ref, val=val):
                    ref[...] = val.astype(F32)

                @pl.when(jnp.logical_not(first))
                def _(ref=ref, val=val):
                    ref[...] += val.astype(F32)

    res = _pcall(
        body, name=name, grid=(groups, nr),
        in_specs=[in_spec(i) for i in ins],
        out_specs=[out_spec(o) for o in outs],
        out_shape=[jax.ShapeDtypeStruct(o.shape, o.dtype) for o in outs],
        compiler_params=_params(2),
    )(*[i.arr for i in ins])
    return res


def _rowwise_vjp(name, fn, ins, cots, diff, *, groups=1, tr=320):
    n_in = len(ins)
    grouped = groups > 1
    cot_ins = []
    counts = []
    for arrs in cots:
        counts.append(len(arrs))
        for a in arrs:
            cot_ins.append(_In(a, "row", grouped=grouped and a.shape[1] > LANE))
    lp = next(i.arr.shape[0] for i in ins if i.kind == "row")
    outs = []
    for d in diff:
        i = ins[d]
        if i.kind == "row":
            w = groups * LANE if i.grouped else i.arr.shape[1]
            outs.append(_Out("row", (lp, w), F32, grouped=i.grouped))
        else:
            outs.append(_Out("acc", i.arr.shape, F32, grouped=i.grouped))

    def bfn(ridx, g, *allvals):
        vals = list(allvals[:n_in])
        cvals = allvals[n_in:]

        def f(*dv):
            full = list(vals)
            for i, v in zip(diff, dv):
                full[i] = v
            return tuple(fn(ridx, g, *full))

        primal, vjp = jax.vjp(f, *[vals[i].astype(F32) for i in diff])
        cts = []
        pos = 0
        for k, cnt in enumerate(counts):
            if cnt == 0:
                cts.append(jnp.zeros_like(primal[k]))
            else:
                c = cvals[pos].astype(F32)
                for extra in cvals[pos + 1:pos + cnt]:
                    c = c + extra.astype(F32)
                cts.append(c.astype(primal[k].dtype))
            pos += cnt
        return vjp(tuple(cts))

    return _rowwise(name, bfn, list(ins) + cot_ins, outs, groups=groups, tr=tr)


def _rms(x, g):
    return x * lax.rsqrt(jnp.mean(x * x, axis=-1, keepdims=True) + NORM_EPS) * g


def _silu(x):
    return x * jax.nn.sigmoid(x)


def _softplus(x):
    return jnp.maximum(x, 0.0) + jnp.log(1.0 + jnp.exp(-jnp.abs(x)))


def _swap_halves(x):
    lane = lax.broadcasted_iota(jnp.int32, x.shape, x.ndim - 1)
    return jnp.where(lane < 32, pltpu.roll(x, LANE - 32, x.ndim - 1), pltpu.roll(x, 32, x.ndim - 1))


@jax.custom_vjp
def _rope(x, c, s):
    return x * c + _swap_halves(x) * s


def _rope_fwd(x, c, s):
    return _rope(x, c, s), (c, s)


def _rope_bwd(res, dy):
    c, s = res
    return dy * c + _swap_halves(dy * s), jnp.zeros_like(c), jnp.zeros_like(s)


_rope.defvjp(_rope_fwd, _rope_bwd)


def _conv_post(c, g):
    s = _silu(c)
    n = s * lax.rsqrt(jnp.sum(s * s, axis=-1, keepdims=True) + NORM_EPS)
    return jnp.where(g < GDN_QK_HEADS, n * (GDN_DK ** -0.5), jnp.where(g < 2 * GDN_QK_HEADS, n, s))


def _conv_taps(xe, w):
    c = xe[8:] * w[3]
    for s in (1, 2, 3):
        c = c + pltpu.roll(xe, s, 0)[8:] * w[3 - s]
    return c


def _conv_fwd(x, w, name, tr=640):
    lp, width = x.shape
    groups = width // LANE
    nr = lp // tr

    def body(x_ref, prev_ref, w_ref, o_ref):
        g = pl.program_id(0)
        r = pl.program_id(1)
        prev = jnp.where(r > 0, prev_ref[...], 0.0)
        xe = jnp.concatenate([prev, x_ref[...]], axis=0)
        o_ref[...] = _conv_post(_conv_taps(xe, [w_ref[t:t + 1, :] for t in range(4)]), g)

    return _pcall(
        body, name=name, grid=(groups, nr),
        in_specs=[pl.BlockSpec((tr, LANE), lambda g, r: (r, g)),
                  pl.BlockSpec((8, LANE), lambda g, r: (jnp.maximum(r * (tr // 8) - 1, 0), g)),
                  pl.BlockSpec((4, LANE), lambda g, r: (0, g))],
        out_specs=pl.BlockSpec((tr, LANE), lambda g, r: (r, g)),
        out_shape=jax.ShapeDtypeStruct((lp, width), F32),
        compiler_params=_params(2),
    )(x, x, w)


def _conv_bwd(x, w, dq, dk, dv, name, tr=640):
    lp, width = x.shape
    groups = width // LANE
    nr = lp // tr
    last8 = lp // 8 - 1
    nqk = GDN_QK_HEADS

    def body(x_ref, prev_ref, next_ref, w_ref, q_ref, k_ref, v_ref, q_n, k_n, v_n, dx_ref, dw_ref):
        g = pl.program_id(0)
        r = pl.program_id(1)
        w = [w_ref[t:t + 1, :] for t in range(4)]
        not_last = r < nr - 1

        def pick(a, b, c):
            return jnp.where(g < nqk, a[...], jnp.where(g < 2 * nqk, b[...], c[...]))

        dy = pick(q_ref, k_ref, v_ref)
        dyn = jnp.where(not_last, pick(q_n, k_n, v_n), 0.0)
        prev = jnp.where(r > 0, prev_ref[...], 0.0)
        nxt = jnp.where(not_last, next_ref[...], 0.0)
        xe = jnp.concatenate([prev, x_ref[...], nxt], axis=0)
        ce = _conv_taps(xe, w)
        _, vjp = jax.vjp(lambda c: _conv_post(c, g), ce)
        (dce,) = vjp(jnp.concatenate([dy, dyn], axis=0))
        n = tr + 8
        dx = dce * w[3]
        for s in (1, 2, 3):
            dx = dx + pltpu.roll(dce, n - s, 0) * w[3 - s]
        dx_ref[...] = dx[:tr]
        dc = dce[:tr]
        row4 = lax.broadcasted_iota(jnp.int32, (4, LANE), 0)
        dw = jnp.zeros((4, LANE), F32)
        for s in (0, 1, 2, 3):
            xs = xe[8:8 + tr] if s == 0 else pltpu.roll(xe, s, 0)[8:8 + tr]
            dw = dw + jnp.where(row4 == 3 - s, jnp.sum(dc * xs, axis=0, keepdims=True), 0.0)

        @pl.when(r == 0)
        def _():
            dw_ref[...] = dw

        @pl.when(r > 0)
        def _():
            dw_ref[...] += dw

    def col_q(g):
        return jnp.minimum(g, nqk - 1)

    def col_k(g):
        return jnp.clip(g - nqk, 0, nqk - 1)

    def col_v(g):
        return jnp.maximum(g - 2 * nqk, 0)

    def blk(colf):
        return pl.BlockSpec((tr, LANE), lambda g, r: (r, colf(g)))

    def nblk(colf):
        return pl.BlockSpec((8, LANE), lambda g, r: (jnp.minimum((r + 1) * (tr // 8), last8), colf(g)))

    return _pcall(
        body, name=name, grid=(groups, nr),
        in_specs=[pl.BlockSpec((tr, LANE), lambda g, r: (r, g)),
                  pl.BlockSpec((8, LANE), lambda g, r: (jnp.maximum(r * (tr // 8) - 1, 0), g)),
                  pl.BlockSpec((8, LANE), lambda g, r: (jnp.minimum((r + 1) * (tr // 8), last8), g)),
                  pl.BlockSpec((4, LANE), lambda g, r: (0, g)),
                  blk(col_q), blk(col_k), blk(col_v), nblk(col_q), nblk(col_k), nblk(col_v)],
        out_specs=[pl.BlockSpec((tr, LANE), lambda g, r: (r, g)),
                   pl.BlockSpec((4, LANE), lambda g, r: (0, g))],
        out_shape=[jax.ShapeDtypeStruct((lp, width), F32), jax.ShapeDtypeStruct((4, width), F32)],
        compiler_params=_params(2),
    )(x, x, x, w, dq, dk, dv, dq, dk, dv)


def _bmm(a, b, dims, prec=None):
    (ca,), (cb,) = dims
    return lax.dot_general(a, b, (((ca + 1,), (cb + 1,)), ((0,), (0,))), precision=prec,
                           preferred_element_type=F32)


def _inv_impl(m):
    c = m.shape[-1]
    ii = lax.broadcasted_iota(jnp.int32, (c, c), 0)
    jj = lax.broadcasted_iota(jnp.int32, (c, c), 1)
    eye = (ii == jj).astype(F32)

    def same_block(shift):
        return (ii >> shift) == (jj >> shift)

    n1 = jnp.where(same_block(3), -m, 0.0)
    n2 = _bmm(n1, n1, _NN, _X3)
    n4 = _bmm(n2, n2, _NN, _X3)
    d = _bmm(_bmm(eye + n1, eye + n2, _NN, _X3), eye + n4, _NN, _X3)
    shift = 3
    while (1 << shift) < c:
        low = jnp.where(jnp.logical_and(same_block(shift + 1), jnp.logical_not(same_block(shift))), m, 0.0)
        d = d - _bmm(d, _bmm(low, d, _NN, _X3), _NN, _X3)
        shift += 1
    return d


@jax.custom_vjp
def _inv_unit_lower(m):
    return _inv_impl(m)


def _inv_f(m):
    t = _inv_impl(m)
    return t, t


def _inv_b(t, dt):
    c = t.shape[-1]
    ii = lax.broadcasted_iota(jnp.int32, (c, c), 0)
    jj = lax.broadcasted_iota(jnp.int32, (c, c), 1)
    gm = _bmm(t, _bmm(dt, t, _NT, _X3), _TN, _X3)
    return (jnp.where(ii > jj, -gm, 0.0),)


_inv_unit_lower.defvjp(_inv_f, _inv_b)


GDN_HEADS_PER_STEP = 16


def _gdn_group(q, k, v, beta_blk, gc_blk, states, h0):
    hp = GDN_HEADS_PER_STEP
    c = q.shape[0]
    lane = lax.broadcasted_iota(jnp.int32, (1, LANE), 1)
    row8 = lax.broadcasted_iota(jnp.int32, (max(8, hp), LANE), 0)
    lane8 = lax.broadcasted_iota(jnp.int32, (max(8, hp), LANE), 1)
    gcr_all = _hdot((lane8 == h0 + row8).astype(F32), gc_blk, _NT)
    betas, gccs = [], []
    for i in range(hp):
        onehot = (lane == h0 + i).astype(F32)
        betas.append(jnp.sum(beta_blk * onehot, axis=1, keepdims=True))
        gccs.append(jnp.sum(gc_blk * onehot, axis=1, keepdims=True))
    beta = jnp.stack(betas)
    gcc = jnp.stack(gccs)
    gcr = jnp.stack([gcr_all[i:i + 1] for i in range(hp)])
    qh = jnp.stack([q[:, (i // 2) * LANE:(i // 2 + 1) * LANE] for i in range(hp)])
    kh = jnp.stack([k[:, (i // 2) * LANE:(i // 2 + 1) * LANE] for i in range(hp)])
    vh = jnp.stack([v[:, i * LANE:(i + 1) * LANE] for i in range(hp)])
    state = jnp.stack(states)
    ii = lax.broadcasted_iota(jnp.int32, (c, c), 0)
    jj = lax.broadcasted_iota(jnp.int32, (c, c), 1)
    incl = ii >= jj
    dec = jnp.where(incl, jnp.exp(jnp.where(incl, gcc - gcr, 0.0)), 0.0)
    eg = jnp.exp(gcc)
    m = _bmm(kh, kh, _NT) * beta * jnp.where(ii > jj, dec, 0.0)
    t = _inv_unit_lower(m)
    u = _bmm(t, vh * beta, _NN, _X3)
    w = _bmm(t, kh * (beta * eg), _NN, _X3)
    attn = _bmm(qh, kh, _NT) * dec
    rows = lax.broadcasted_iota(jnp.int32, (c, 1), 0)
    gl = jnp.sum(jnp.where(rows == c - 1, gcc, 0.0), axis=1, keepdims=True)
    v_new = u - _bmm(w, state, _NN)
    o = _bmm(qh * eg, state, _NN) + _bmm(attn, v_new, _NN)
    new_state = state * jnp.exp(gl) + _bmm(kh * jnp.exp(gl - gcc), v_new, _TN)
    return jnp.concatenate([o[i] for i in range(hp)], axis=1), tuple(new_state[i] for i in range(hp))


def _gdn_specs(nc, rev):
    def cidx(n):
        return (nc - 1 - n) if rev else n
    hp = GDN_HEADS_PER_STEP
    nqk = GDN_QK_HEADS
    c = GDN_CHUNK
    nq = 2 * nqk // hp
    q_spec = pl.BlockSpec((c, hp // 2 * LANE), lambda n, g: (cidx(n), g))
    k_spec = pl.BlockSpec((c, hp // 2 * LANE), lambda n, g: (cidx(n), nq + g))
    v_spec = pl.BlockSpec((c, hp * LANE), lambda n, g: (cidx(n), nq + g))
    s_spec = pl.BlockSpec((c, LANE), lambda n, g: (cidx(n), 0))
    o_spec = pl.BlockSpec((c, hp * LANE), lambda n, g: (cidx(n), g))
    ck_spec = pl.BlockSpec((hp, 1, GDN_DK, LANE), lambda n, g: (g, cidx(n), 0, 0))
    return q_spec, k_spec, v_spec, s_spec, o_spec, ck_spec


def _gdn_fwd(qkv, beta, gc, name):
    lp = qkv.shape[0]
    nc = lp // GDN_CHUNK
    nh = GDN_V_HEADS
    hp = GDN_HEADS_PER_STEP
    q_spec, k_spec, v_spec, s_spec, o_spec, ck_spec = _gdn_specs(nc, False)

    def body(q_ref, k_ref, v_ref, b_ref, g_ref, o_ref, ck_ref, state):
        n = pl.program_id(0)
        g = pl.program_id(1)

        @pl.when(n == 0)
        def _():
            for i in range(hp):
                state[g * hp + i] = jnp.zeros((GDN_DK, LANE), F32)

        states = tuple(state[g * hp + i] for i in range(hp))
        for i in range(hp):
            ck_ref[i, 0] = states[i]
        o, new_states = _gdn_group(q_ref[...], k_ref[...], v_ref[...], b_ref[...], g_ref[...], states, g * hp)
        o_ref[...] = o
        for i in range(hp):
            state[g * hp + i] = new_states[i]

    return _pcall(
        body, name=name, grid=(nc, nh // hp),
        in_specs=[q_spec, k_spec, v_spec, s_spec, s_spec],
        out_specs=[o_spec, ck_spec],
        out_shape=[jax.ShapeDtypeStruct((lp, GDN_V_W), F32),
                   jax.ShapeDtypeStruct((nh, nc, GDN_DK, LANE), F32)],
        scratch_shapes=[pltpu.VMEM((nh, GDN_DK, LANE), F32)],
        compiler_params=_params(2),
    )(qkv, qkv, qkv, beta, gc)


def _gdn_bwd(qkv, beta, gc, ckpt, do, name):
    lp = qkv.shape[0]
    nc = lp // GDN_CHUNK
    nh = GDN_V_HEADS
    hp = GDN_HEADS_PER_STEP
    q_spec, k_spec, v_spec, s_spec, o_spec, ck_spec = _gdn_specs(nc, True)

    def body(q_ref, k_ref, v_ref, b_ref, g_ref, ck_ref, do_ref,
             dq_ref, dk_ref, dv_ref, db_ref, dg_ref, dstate):
        n = pl.program_id(0)
        g = pl.program_id(1)

        @pl.when(n == 0)
        def _():
            for i in range(hp):
                dstate[g * hp + i] = jnp.zeros((GDN_DK, LANE), F32)

        states = tuple(ck_ref[i, 0] for i in range(hp))
        _, vjp = jax.vjp(lambda q, k, v, b, gg, s: _gdn_group(q, k, v, b, gg, s, g * hp),
                         q_ref[...], k_ref[...], v_ref[...], b_ref[...], g_ref[...], states)
        dq, dk, dv, db, dg, ds = vjp((do_ref[...], tuple(dstate[g * hp + i] for i in range(hp))))
        dq_ref[...] = dq
        dk_ref[...] = dk
        dv_ref[...] = dv
        for i in range(hp):
            dstate[g * hp + i] = ds[i]

        @pl.when(g == 0)
        def _():
            db_ref[...] = db
            dg_ref[...] = dg

        @pl.when(g > 0)
        def _():
            db_ref[...] += db
            dg_ref[...] += dg

    qk_shape = jax.ShapeDtypeStruct((lp, GDN_QK_W), F32)
    big = jax.ShapeDtypeStruct((lp, GDN_V_W), F32)
    small = jax.ShapeDtypeStruct((lp, LANE), F32)
    dq_spec = pl.BlockSpec((GDN_CHUNK, hp // 2 * LANE), lambda n, g: (nc - 1 - n, g))
    return _pcall(
        body, name=name, grid=(nc, nh // hp),
        in_specs=[q_spec, k_spec, v_spec, s_spec, s_spec, ck_spec, o_spec],
        out_specs=[dq_spec, dq_spec, o_spec, s_spec, s_spec],
        out_shape=[qk_shape, qk_shape, big, small, small],
        scratch_shapes=[pltpu.VMEM((nh, GDN_DK, LANE), F32)],
        compiler_params=_params(2),
    )(qkv, qkv, qkv, beta, gc, ckpt, do)


LOG2E = 1.4426950408889634
LN2 = 0.6931471805599453
Q_PRESCALE = MLA_QK ** -0.5 * LOG2E


def _att_mask(i, j, tb, transposed):
    r = lax.broadcasted_iota(jnp.int32, (tb, tb), 0)
    c = lax.broadcasted_iota(jnp.int32, (tb, tb), 1)
    qpos, kpos = (i * tb + c, j * tb + r) if transposed else (i * tb + r, j * tb + c)
    return jnp.logical_and(kpos <= qpos, kpos >= FRONT)


def _masked_and_plain(i, j, step):
    edge = jnp.logical_or(j == i, j == 0)

    @pl.when(jnp.logical_and(edge, j <= i))
    def _():
        step(True)

    @pl.when(jnp.logical_and(jnp.logical_not(edge), j < i))
    def _():
        step(False)


def _cat(a_ref, b_ref):
    return jnp.concatenate([a_ref[...], b_ref[...]], axis=1)


def _head_lane(blk, h):
    lane = lax.broadcasted_iota(jnp.int32, (1, LANE), 1)
    return jnp.sum(jnp.where(lane == h, blk, 0.0), axis=1, keepdims=True)


def _flash_fwd(qn, qr, kn, kr, v, name, tb=ROW_ALIGN):
    lp = qn.shape[0]
    nb = lp // tb
    nh = MLA_HEADS

    def body(qn_ref, qr_ref, kn_ref, kr_ref, v_ref, o_ref, lse_ref, m_s, l_s, acc):
        i, h, j = pl.program_id(0), pl.program_id(1), pl.program_id(2)

        @pl.when(j == 0)
        def _():
            m_s[...] = jnp.full_like(m_s, NEG)
            l_s[...] = jnp.zeros_like(l_s)
            acc[...] = jnp.zeros_like(acc)

        def step(masked):
            s = _dot(_cat(qn_ref, qr_ref), _cat(kn_ref, kr_ref), _NT)
            if masked:
                s = jnp.where(_att_mask(i, j, tb, False), s, NEG)
            m_new = jnp.maximum(m_s[...], jnp.max(s, axis=1, keepdims=True))
            alpha = jnp.exp2(m_s[...] - m_new)
            p = jnp.exp2(s - m_new)
            l_s[...] = alpha * l_s[...] + jnp.sum(p, axis=1, keepdims=True)
            acc[...] = alpha * acc[...] + _dot(p.astype(BF16), v_ref[...], _NN)
            m_s[...] = m_new

        _masked_and_plain(i, j, step)

        @pl.when(j == nb - 1)
        def _():
            o_ref[...] = acc[...] / l_s[...]
            lane = lax.broadcasted_iota(jnp.int32, (1, LANE), 1)
            mine = jnp.where(lane == h, m_s[...] + jnp.log(l_s[...]) * LOG2E, 0.0)

            @pl.when(h == 0)
            def _():
                lse_ref[...] = mine

            @pl.when(h > 0)
            def _():
                lse_ref[...] += mine

    qspec = pl.BlockSpec((tb, LANE), lambda i, h, j: (i, h))
    kspec = pl.BlockSpec((tb, LANE), lambda i, h, j: (jnp.minimum(j, i), h))
    krspec = pl.BlockSpec((tb, LANE), lambda i, h, j: (jnp.minimum(j, i), 0))
    return _pcall(
        body, name=name, grid=(nb, nh, nb),
        in_specs=[qspec, qspec, kspec, krspec, kspec],
        out_specs=[qspec, pl.BlockSpec((tb, LANE), lambda i, h, j: (i, 0))],
        out_shape=[jax.ShapeDtypeStruct((lp, nh * LANE), F32), jax.ShapeDtypeStruct((lp, LANE), F32)],
        scratch_shapes=[pltpu.VMEM((tb, 1), F32), pltpu.VMEM((tb, 1), F32), pltpu.VMEM((tb, LANE), F32)],
        compiler_params=_params(3),
    )(qn, qr, kn, kr, v)


def _flash_dq(qn, qr, kn, kr, v, o, do, lse, name, tb=ROW_ALIGN):
    lp = qn.shape[0]
    nb = lp // tb
    nh = MLA_HEADS

    def body(qn_ref, qr_ref, kn_ref, kr_ref, v_ref, o_ref, do_ref, lse_ref,
             dqn_ref, dqr_ref, dq_acc, delta_s, lse_s):
        i, h, j = pl.program_id(0), pl.program_id(1), pl.program_id(2)

        @pl.when(j == 0)
        def _():
            dq_acc[...] = jnp.zeros_like(dq_acc)
            delta_s[...] = jnp.sum(do_ref[...] * o_ref[...], axis=1, keepdims=True)
            lse_s[...] = _head_lane(lse_ref[...], h)

        def step(masked):
            k = _cat(kn_ref, kr_ref)
            s = _dot(_cat(qn_ref, qr_ref), k, _NT)
            if masked:
                s = jnp.where(_att_mask(i, j, tb, False), s, NEG)
            p = jnp.exp2(s - lse_s[...])
            dp = _dot(do_ref[...].astype(BF16), v_ref[...], _NT)
            ds = p * (dp - delta_s[...])
            dq_acc[...] += _dot(ds.astype(BF16), k, _NN)

        _masked_and_plain(i, j, step)

        @pl.when(j == nb - 1)
        def _():
            dqn_ref[...] = dq_acc[:, :LANE] * LN2
            dqr_ref[...] = dq_acc[:, LANE:] * LN2

    qspec = pl.BlockSpec((tb, LANE), lambda i, h, j: (i, h))
    kspec = pl.BlockSpec((tb, LANE), lambda i, h, j: (jnp.minimum(j, i), h))
    krspec = pl.BlockSpec((tb, LANE), lambda i, h, j: (jnp.minimum(j, i), 0))
    lspec = pl.BlockSpec((tb, LANE), lambda i, h, j: (i, 0))
    shp = jax.ShapeDtypeStruct((lp, nh * LANE), F32)
    return _pcall(
        body, name=name, grid=(nb, nh, nb),
        in_specs=[qspec, qspec, kspec, krspec, kspec, qspec, qspec, lspec],
        out_specs=[qspec, qspec],
        out_shape=[shp, shp],
        scratch_shapes=[pltpu.VMEM((tb, 2 * LANE), F32), pltpu.VMEM((tb, 1), F32), pltpu.VMEM((tb, 1), F32)],
        compiler_params=_params(3),
    )(qn, qr, kn, kr, v, o, do, lse)


def _flash_dkv(qn, qr, kn, kr, v, o, do, lse, name, tb=ROW_ALIGN):
    lp = qn.shape[0]
    nb = lp // tb
    nh = MLA_HEADS

    def body(qn_ref, qr_ref, kn_ref, kr_ref, v_ref, o_ref, do_ref, lse_ref,
             dkn_ref, dkr_ref, dv_ref, dk_acc, dkr_acc, dv_acc):
        j, h, i = pl.program_id(0), pl.program_id(1), pl.program_id(2)

        @pl.when(i == 0)
        def _():
            dk_acc[...] = jnp.zeros_like(dk_acc)
            dv_acc[...] = jnp.zeros_like(dv_acc)

        @pl.when(jnp.logical_and(i == 0, h == 0))
        def _():
            dkr_acc[...] = jnp.zeros_like(dkr_acc)

        def step(masked):
            q = _cat(qn_ref, qr_ref)
            st = _dot(_cat(kn_ref, kr_ref), q, _NT)
            if masked:
                st = jnp.where(_att_mask(i, j, tb, True), st, NEG)
            do_blk = do_ref[...]
            lane = lax.broadcasted_iota(jnp.int32, (8, LANE), 1)
            lse_row = _hdot((lane == h).astype(F32), lse_ref[...], _NT)[0:1]
            delta_row = _hdot(jnp.ones((8, LANE), F32), do_blk * o_ref[...], _NT)[0:1]
            pt = jnp.exp2(st - lse_row)
            do_b = do_blk.astype(BF16)
            dv_acc[...] += _dot(pt.astype(BF16), do_b, _NN)
            dpt = _dot(v_ref[...], do_b, _NT)
            dst = pt * (dpt - delta_row)
            dk_acc[...] += _dot(dst.astype(BF16), q, _NN)

        _masked_and_plain(i, j, step)

        @pl.when(i == nb - 1)
        def _():
            dkn_ref[...] = dk_acc[:, :LANE] * LN2
            dv_ref[...] = dv_acc[...]
            dkr_acc[...] += dk_acc[:, LANE:]

            @pl.when(h == nh - 1)
            def _():
                dkr_ref[...] = dkr_acc[...] * LN2

    qspec = pl.BlockSpec((tb, LANE), lambda j, h, i: (jnp.maximum(i, j), h))
    lspec = pl.BlockSpec((tb, LANE), lambda j, h, i: (jnp.maximum(i, j), 0))
    kspec = pl.BlockSpec((tb, LANE), lambda j, h, i: (j, h))
    krspec = pl.BlockSpec((tb, LANE), lambda j, h, i: (j, 0))
    shp = jax.ShapeDtypeStruct((lp, nh * LANE), F32)
    return _pcall(
        body, name=name, grid=(nb, nh, nb),
        in_specs=[qspec, qspec, kspec, krspec, kspec, qspec, qspec, lspec],
        out_specs=[kspec, krspec, kspec],
        out_shape=[shp, jax.ShapeDtypeStruct((lp, LANE), F32), shp],
        scratch_shapes=[pltpu.VMEM((tb, 2 * LANE), F32), pltpu.VMEM((tb, LANE), F32), pltpu.VMEM((tb, LANE), F32)],
        compiler_params=_params(3),
    )(qn, qr, kn, kr, v, o, do, lse)


ELEMENTWISE_BLOCK_BYTES = 1 << 20


def _row_tile(rows, width, copies=1):
    for t in (1024, 512, 256, 128, 64, 32, 16, 8):
        if rows % t == 0 and t * width * 4 * copies <= ELEMENTWISE_BLOCK_BYTES:
            return t
    return rows


def _adamw(w, g, m, v, name):
    rows, width = w.shape
    tr = _row_tile(rows, width)

    def body(w_ref, g_ref, m_ref, v_ref, d_ref, nm_ref, nv_ref):
        gg = g_ref[...]
        nm = ADAM_B1 * m_ref[...] + (1.0 - ADAM_B1) * gg
        nv = ADAM_B2 * v_ref[...] + (1.0 - ADAM_B2) * jnp.square(gg)
        m_hat = nm / (1.0 - ADAM_B1 ** ADAM_STEP)
        v_hat = nv / (1.0 - ADAM_B2 ** ADAM_STEP)
        d_ref[...] = -ADAM_LR * (m_hat / (jnp.sqrt(v_hat) + ADAM_EPS) + ADAM_WD * w_ref[...])
        nm_ref[...] = nm
        nv_ref[...] = nv

    spec = pl.BlockSpec((tr, width), lambda r: (r, 0))
    shp = jax.ShapeDtypeStruct((rows, width), F32)
    return _pcall(body, name=name, grid=(rows // tr,), in_specs=[spec] * 4, out_specs=[spec] * 3,
                  out_shape=[shp] * 3, compiler_params=_params(1))(w, g, m, v)


def _add_pair(a, b, name):
    s, rows, width = a.shape
    tr = _row_tile(rows, width)

    def body(a_ref, b_ref, o_ref):
        o_ref[...] = a_ref[...] + b_ref[...]

    spec = pl.BlockSpec((1, tr, width), lambda i, r: (i, r, 0))
    return _pcall(body, name=name, grid=(s, rows // tr), in_specs=[spec, spec], out_specs=spec,
                  out_shape=jax.ShapeDtypeStruct(a.shape, F32), compiler_params=_params(2))(a, b)


def _sum_slots(a, name):
    s, rows, width = a.shape
    tr = _row_tile(rows, width, copies=s)

    def body(a_ref, o_ref):
        tot = a_ref[0]
        for k in range(1, s):
            tot = tot + a_ref[k]
        o_ref[...] = tot

    return _pcall(body, name=name, grid=(rows // tr,),
                  in_specs=[pl.BlockSpec((s, tr, width), lambda r: (0, r, 0))],
                  out_specs=pl.BlockSpec((tr, width), lambda r: (r, 0)),
                  out_shape=jax.ShapeDtypeStruct((rows, width), F32), compiler_params=_params(1))(a)


_ANY = pl.BlockSpec(memory_space=pl.ANY)


def _my_place():
    return lax.axis_index("x"), lax.axis_index("y"), lax.axis_index("c")


def _other_chips(x, y):
    return [(1 - x, y), (x, 1 - y), (1 - x, 1 - y)]


def _gather_shards(flat, name):
    rows, width = flat.shape

    def body(x_ref, out_ref, send_sems, recv_sems, local_sem):
        x, y, c = _my_place()
        mine = pltpu.make_async_copy(x_ref, out_ref.at[2 * x + y], local_sem)
        mine.start()
        sends = []
        for k, (px, py) in enumerate(_other_chips(x, y)):
            cp = pltpu.make_async_remote_copy(
                src_ref=x_ref, dst_ref=out_ref.at[2 * x + y], send_sem=send_sems.at[k], recv_sem=recv_sems.at[k],
                device_id=(px, py, c), device_id_type=MESH)
            cp.start()
            sends.append(cp)
        for k, (px, py) in enumerate(_other_chips(x, y)):
            pltpu.make_async_remote_copy(
                src_ref=x_ref, dst_ref=out_ref.at[2 * px + py], send_sem=send_sems.at[k], recv_sem=recv_sems.at[k],
                device_id=(px, py, c), device_id_type=MESH).wait_recv()
        for cp in sends:
            cp.wait_send()
        mine.wait()

    return _pcall(
        body, name=name, in_specs=[_ANY], out_specs=_ANY,
        out_shape=jax.ShapeDtypeStruct((4, rows, width), flat.dtype),
        scratch_shapes=[pltpu.SemaphoreType.DMA((3,)), pltpu.SemaphoreType.DMA((3,)), pltpu.SemaphoreType.DMA],
    )(flat)


def _sibling_split(g, name):
    s, rows, width = g.shape
    half = rows // 2

    def body(g_ref, own_ref, got_ref, send_sems, recv_sems, local_sems):
        x, y, c = _my_place()
        mine_rows = pl.ds(pl.multiple_of(c * half, 8), half)
        sib_rows = pl.ds(pl.multiple_of((1 - c) * half, 8), half)
        local, sends = [], []
        for k in range(s):
            cp = pltpu.make_async_remote_copy(
                src_ref=g_ref.at[k, sib_rows, :], dst_ref=got_ref.at[k],
                send_sem=send_sems.at[k], recv_sem=recv_sems.at[k], device_id=(x, y, 1 - c), device_id_type=MESH)
            cp.start()
            sends.append(cp)
            mine = pltpu.make_async_copy(g_ref.at[k, mine_rows, :], own_ref.at[k], local_sems.at[k])
            mine.start()
            local.append(mine)
        for cp in sends:
            cp.wait_recv()
        for cp in sends:
            cp.wait_send()
        for mine in local:
            mine.wait()

    shp = jax.ShapeDtypeStruct((s, half, width), g.dtype)
    return _pcall(
        body, name=name, in_specs=[_ANY], out_specs=[_ANY, _ANY], out_shape=[shp, shp],
        scratch_shapes=[pltpu.SemaphoreType.DMA((s,)), pltpu.SemaphoreType.DMA((s,)), pltpu.SemaphoreType.DMA((s,))],
    )(g)


def _chip_scatter(p, name):
    s, rows, width = p.shape

    def body(p_ref, out_ref, send_sems, recv_sems, local_sem):
        x, y, c = _my_place()
        me = 2 * x + y
        mine = pltpu.make_async_copy(p_ref.at[me], out_ref.at[me], local_sem)
        mine.start()
        sends = []
        for k, (px, py) in enumerate(_other_chips(x, y)):
            cp = pltpu.make_async_remote_copy(
                src_ref=p_ref.at[2 * px + py], dst_ref=out_ref.at[me], send_sem=send_sems.at[k],
                recv_sem=recv_sems.at[k], device_id=(px, py, c), device_id_type=MESH)
            cp.start()
            sends.append(cp)
        for k, (px, py) in enumerate(_other_chips(x, y)):
            pltpu.make_async_remote_copy(
                src_ref=p_ref.at[me], dst_ref=out_ref.at[2 * px + py], send_sem=send_sems.at[k],
                recv_sem=recv_sems.at[k], device_id=(px, py, c), device_id_type=MESH).wait_recv()
        for cp in sends:
            cp.wait_send()
        mine.wait()

    return _pcall(
        body, name=name, in_specs=[_ANY], out_specs=_ANY,
        out_shape=jax.ShapeDtypeStruct(p.shape, p.dtype),
        scratch_shapes=[pltpu.SemaphoreType.DMA((3,)), pltpu.SemaphoreType.DMA((3,)), pltpu.SemaphoreType.DMA],
    )(p)


def _sibling_join(qh, name):
    half, width = qh.shape
    pieces = 4 if half % 32 == 0 else 1
    rows = half // pieces

    def body(q_ref, out_ref, send_sems, recv_sems, local_sems):
        x, y, c = _my_place()
        local, sends = [], []
        for k in range(pieces):
            src = q_ref.at[pl.ds(k * rows, rows), :]
            dst = out_ref.at[pl.ds(pl.multiple_of(c * half + k * rows, 8), rows), :]
            cp = pltpu.make_async_remote_copy(
                src_ref=src, dst_ref=dst, send_sem=send_sems.at[k], recv_sem=recv_sems.at[k],
                device_id=(x, y, 1 - c), device_id_type=MESH)
            cp.start()
            sends.append(cp)
            mine = pltpu.make_async_copy(src, dst, local_sems.at[k])
            mine.start()
            local.append(mine)
        for k in range(pieces):
            theirs = out_ref.at[pl.ds(pl.multiple_of((1 - c) * half + k * rows, 8), rows), :]
            pltpu.make_async_remote_copy(
                src_ref=q_ref.at[pl.ds(k * rows, rows), :], dst_ref=theirs, send_sem=send_sems.at[k],
                recv_sem=recv_sems.at[k], device_id=(x, y, 1 - c), device_id_type=MESH).wait_recv()
        for cp in sends:
            cp.wait_send()
        for mine in local:
            mine.wait()

    return _pcall(
        body, name=name, in_specs=[_ANY], out_specs=_ANY,
        out_shape=jax.ShapeDtypeStruct((2 * half, width), qh.dtype),
        scratch_shapes=[pltpu.SemaphoreType.DMA((pieces,)), pltpu.SemaphoreType.DMA((pieces,)),
                        pltpu.SemaphoreType.DMA((pieces,))],
    )(qh)


def _all_sum_small(part, name):
    rows, width = part.shape

    def body(p_ref, out_ref, land, send_sems, recv_sems):
        x, y, c = _my_place()
        me = 4 * x + 2 * y + c
        land[me] = p_ref[...]
        sends = []
        for k in range(1, 8):
            peer = (x ^ (k >> 2), y ^ ((k >> 1) & 1), c ^ (k & 1))
            cp = pltpu.make_async_remote_copy(
                src_ref=p_ref, dst_ref=land.at[me], send_sem=send_sems.at[k - 1], recv_sem=recv_sems.at[k - 1],
                device_id=peer, device_id_type=MESH)
            cp.start()
            sends.append(cp)
        for k in range(1, 8):
            px, py, pc = x ^ (k >> 2), y ^ ((k >> 1) & 1), c ^ (k & 1)
            pltpu.make_async_remote_copy(
                src_ref=p_ref, dst_ref=land.at[4 * px + 2 * py + pc], send_sem=send_sems.at[k - 1],
                recv_sem=recv_sems.at[k - 1], device_id=(px, py, pc), device_id_type=MESH).wait_recv()
        for cp in sends:
            cp.wait_send()
        tot = land[0]
        for k in range(1, 8):
            tot = tot + land[k]
        out_ref[...] = tot

    vmem = pl.BlockSpec(memory_space=pltpu.VMEM)
    return _pcall(
        body, name=name, in_specs=[vmem], out_specs=vmem,
        out_shape=jax.ShapeDtypeStruct((rows, width), F32),
        scratch_shapes=[pltpu.VMEM((8, rows, width), F32), pltpu.SemaphoreType.DMA((7,)),
                        pltpu.SemaphoreType.DMA((7,))],
    )(part)


def _big_layout(shards):
    return [(a.shape[0], a.shape[1], ax) for a, ax in shards]


def _pack_shards(arrs):
    flat = jnp.concatenate([a.reshape(-1) for a in arrs])
    assert flat.shape[0] % (16 * LANE) == 0, flat.shape
    return flat.reshape(-1, LANE)


def _unpack_shards(flat, layout):
    flat = flat.reshape(-1)
    out, off = [], 0
    for r, c, _ in layout:
        out.append(flat[off:off + r * c].reshape(r, c))
        off += r * c
    return out


def _unpack_full(gathered, layout):
    g = gathered.reshape(4, -1)
    out, off = [], 0
    for r, c, ax in layout:
        seg = g[:, off:off + r * c].reshape(4, r, c)
        out.append(seg.transpose(1, 0, 2).reshape(r, 4 * c) if ax == 1 else seg.reshape(4 * r, c))
        off += r * c
    return out


def _pack_full(fulls, layout):
    parts = []
    for a, (r, c, ax) in zip(fulls, layout):
        if ax == 1:
            parts.append(a.reshape(r, 4, c).transpose(1, 0, 2).reshape(4, r * c))
        else:
            parts.append(a.reshape(4, r * c))
    return jnp.concatenate(parts, axis=1).reshape(4, -1, LANE)


def _pad_lanes(a, width=LANE):
    return jnp.pad(a, [(0, 0)] * (a.ndim - 1) + [(0, width - a.shape[-1])])


def _pack_small(arrs):
    rows = [_pad_lanes(a.reshape(1, -1), -(-a.size // LANE) * LANE).reshape(-1, LANE) for a in arrs]
    flat = jnp.concatenate(rows, axis=0)
    return jnp.pad(flat, ((0, -flat.shape[0] % 8), (0, 0)))


def _unpack_small(flat, shapes):
    out, off = [], 0
    for shp in shapes:
        n = math.prod(shp)
        nr = -(-n // LANE)
        out.append(flat[off:off + nr].reshape(-1)[:n].reshape(shp))
        off += nr
    return out


def kernel(x, meta_tokens, pre_norm, post_norm, gdn_w_in, gdn_conv_w, gdn_a_log, gdn_dt_bias, gdn_out_norm, gdn_w_out, kv_norm, kv_w_down, kv_latent_norm, kv_w_up, mla_w_in, mla_q_latent_norm, mla_w_q_up, mla_w_out, loss_target, m_meta_tokens, m_pre_norm, m_post_norm, m_gdn_w_in, m_gdn_conv_w, m_gdn_a_log, m_gdn_dt_bias, m_gdn_out_norm, m_gdn_w_out, m_kv_norm, m_kv_w_down, m_kv_latent_norm, m_kv_w_up, m_mla_w_in, m_mla_q_latent_norm, m_mla_w_q_up, m_mla_w_out, v_meta_tokens, v_pre_norm, v_post_norm, v_gdn_w_in, v_gdn_conv_w, v_gdn_a_log, v_gdn_dt_bias, v_gdn_out_norm, v_gdn_w_out, v_kv_norm, v_kv_w_down, v_kv_latent_norm, v_kv_w_up, v_mla_w_in, v_mla_q_latent_norm, v_mla_w_q_up, v_mla_w_out):
    seq = x.shape[1]
    d = D_MODEL
    lp = -(-(ROW0 + seq) // ROW_ALIGN) * ROW_ALIGN
    tail = lp - ROW0 - seq

    big_names = ["meta_tokens", "gdn_conv_w", "gdn_w_out", "kv_w_down", "kv_w_up", "mla_w_in", "mla_w_q_up",
                 "mla_w_out"]
    big_axis = [1, 1, 0, 0, 1, 1, 1, 0]
    big_w = [meta_tokens, gdn_conv_w[0], gdn_w_out[0], kv_w_down, kv_w_up, mla_w_in[0], mla_w_q_up[0], mla_w_out[0]]
    big_m = [m_meta_tokens, m_gdn_conv_w[0], m_gdn_w_out[0], m_kv_w_down, m_kv_w_up, m_mla_w_in[0], m_mla_w_q_up[0],
             m_mla_w_out[0]]
    big_v = [v_meta_tokens, v_gdn_conv_w[0], v_gdn_w_out[0], v_kv_w_down, v_kv_w_up, v_mla_w_in[0], v_mla_w_q_up[0],
             v_mla_w_out[0]]
    layout = _big_layout(list(zip(big_w, big_axis)))
    w_flat = _pack_shards(big_w)
    (meta_f, conv_w, w_out0, kv_down, kv_up, w_in1, w_qup, w_out1) = _unpack_full(
        _gather_shards(w_flat, "gather_weights"), layout)
    w_in0_shards = _gather_shards(gdn_w_in[0], "gather_gdn_w_in")
    w_in0 = jnp.concatenate([w_in0_shards[s] for s in range(4)], axis=1)
    win_cols = gdn_w_in.shape[2]

    nv = GDN_V_HEADS
    w_qkv = w_in0[:, :GDN_CONV_W]
    w_z0 = w_in0[:, GDN_CONV_W:GDN_CONV_W + GDN_V_W]
    w_b = _pad_lanes(w_in0[:, GDN_CONV_W + GDN_V_W:GDN_CONV_W + GDN_V_W + nv])
    w_a = _pad_lanes(w_in0[:, GDN_CONV_W + GDN_V_W + nv:])
    w_ckv = kv_down[:, :MLA_KV_RANK]
    w_kr = _pad_lanes(kv_down[:, MLA_KV_RANK:])
    kvu = kv_up.reshape(MLA_KV_RANK, MLA_HEADS, 2 * LANE)
    w_kn = kvu[:, :, :LANE].reshape(MLA_KV_RANK, MLA_HEADS * LANE)
    w_v = kvu[:, :, LANE:].reshape(MLA_KV_RANK, MLA_HEADS * LANE)
    w_cq = w_in1[:, :MLA_Q_RANK]
    w_z1 = w_in1[:, MLA_Q_RANK:]
    qu = w_qup.reshape(MLA_Q_RANK, MLA_HEADS, MLA_QK)
    w_qn = qu[:, :, :MLA_NOPE].reshape(MLA_Q_RANK, MLA_HEADS * LANE) * Q_PRESCALE
    w_qr = _pad_lanes(qu[:, :, MLA_NOPE:]).reshape(MLA_Q_RANK, MLA_HEADS * LANE) * Q_PRESCALE

    pre0, pre1 = pre_norm[0:1], pre_norm[1:2]
    post0, post1 = post_norm[0:1], post_norm[1:2]
    a_log = _pad_lanes(gdn_a_log)
    dt_bias = _pad_lanes(gdn_dt_bias)
    kvn = kv_norm.reshape(1, d)
    kvl = kv_latent_norm.reshape(1, MLA_KV_RANK)
    qln = mla_q_latent_norm

    h0 = jnp.concatenate([jnp.zeros((FRONT, d), F32), meta_f, x[0], jnp.zeros((tail, d), F32)], axis=0)
    tgt = jnp.pad(loss_target[0], ((ROW0, tail), (0, 0)))
    pos = jnp.maximum(jnp.arange(lp, dtype=jnp.int32) - FRONT, 0).astype(F32)
    inv = ROPE_THETA ** (-jnp.arange(0, MLA_ROPE, 2, dtype=F32) / MLA_ROPE)
    ang = pos[:, None] * inv[None, :]
    zeros64 = jnp.zeros((lp, LANE - MLA_ROPE), F32)
    cos_t = jnp.concatenate([jnp.cos(ang), jnp.cos(ang), zeros64], axis=1)
    sin_t = jnp.concatenate([-jnp.sin(ang), jnp.sin(ang), zeros64], axis=1)

    def valid_rows(ridx):
        return jnp.logical_and(ridx >= FRONT, ridx < ROW0 + seq)

    def f_pre0(ridx, g, h, gain):
        return _rms(h, gain), h

    (hn0,) = _rowwise("pre0", lambda *a: f_pre0(*a)[:1], [_In(h0), _In(pre0, "const")],
                      [_Out("row", (lp, d), BF16)])
    qkv_raw = _mm(hn0, w_qkv, "nn", "gdn_in_qkv")
    z0 = _mm(hn0, w_z0, "nn", "gdn_in_z")
    b_raw = _mm(hn0, w_b, "nn", "gdn_in_b")
    a_raw = _mm(hn0, w_a, "nn", "gdn_in_a")

    def f_ba(ridx, g, b, a, alog, dtb):
        tr = b.shape[0]
        ok = valid_rows(ridx).astype(F32)
        beta = jax.nn.sigmoid(b) * ok
        gate = -jnp.exp(alog) * _softplus(a + dtb) * ok
        ii = lax.broadcasted_iota(jnp.int32, (tr, tr), 0)
        jj = lax.broadcasted_iota(jnp.int32, (tr, tr), 1)
        tri = jnp.logical_and((ii >> 6) == (jj >> 6), ii >= jj).astype(F32)
        return beta, _hdot(tri, gate)

    ba_ins = [_In(b_raw), _In(a_raw), _In(a_log, "const"), _In(dt_bias, "const")]
    beta, gc = _rowwise("gdn_gates", f_ba, ba_ins, [_Out("row", (lp, LANE)), _Out("row", (lp, LANE))])
    qkv = _conv_fwd(qkv_raw, conv_w, "gdn_conv")
    o0, ckpt = _gdn_fwd(qkv, beta, gc, "gdn_scan")

    def f_gate0(ridx, g, o, z, gain):
        return (_rms(o, gain) * _silu(z),)

    gate0_ins = [_In(o0, grouped=True), _In(z0, grouped=True), _In(gdn_out_norm, "const")]
    (gated0,) = _rowwise("gdn_gate", f_gate0, gate0_ins, [_Out("row", (lp, GDN_V_W), BF16, grouped=True)],
                         groups=nv, tr=640)
    y0 = _mm(gated0, w_out0, "nn", "gdn_out")

    def f_mid(ridx, g, h, y, g_post, g_pre, g_kv):
        h1 = h + _rms(y, g_post)
        return h1, _rms(h1, g_pre), _rms(h1, g_kv)

    mid_ins = [_In(h0), _In(y0), _In(post0, "const"), _In(pre1, "const"), _In(kvn, "const")]
    h1, hn1, hkv = _rowwise("mid", f_mid, mid_ins,
                            [_Out("row", (lp, d)), _Out("row", (lp, d), BF16), _Out("row", (lp, d), BF16)])

    ckv_raw = _mm(hkv, w_ckv, "nn", "kv_down_c")
    kr_raw = _mm(hkv, w_kr, "nn", "kv_down_r")

    def f_ckv(ridx, g, c, r, cs, sn, gain):
        return _rms(c, gain), _rope(r, cs, sn)

    ckv_ins = [_In(ckv_raw), _In(kr_raw), _In(cos_t), _In(sin_t), _In(kvl, "const")]
    ckv, kr = _rowwise("kv_latent", f_ckv, ckv_ins, [_Out("row", (lp, LANE)), _Out("row", (lp, LANE), BF16)],
                       tr=640)
    kn = _mm(ckv, w_kn, "nn", "kv_up_k", BF16)
    vv = _mm(ckv, w_v, "nn", "kv_up_v", BF16)
    cq_raw = _mm(hn1, w_cq, "nn", "mla_in_q")
    z1 = _mm(hn1, w_z1, "nn", "mla_in_z")

    def f_cq(ridx, g, c, gain):
        return (_rms(c, gain),)

    cq_ins = [_In(cq_raw), _In(qln, "const")]
    (cq,) = _rowwise("q_latent", f_cq, cq_ins, [_Out("row", (lp, MLA_Q_RANK))], tr=640)
    qn = _mm(cq, w_qn, "nn", "q_up_n", BF16)
    qr_raw = _mm(cq, w_qr, "nn", "q_up_r")

    def f_qrope(ridx, g, r, cs, sn):
        return (_rope(r, cs, sn),)

    qr_ins = [_In(qr_raw, grouped=True), _In(cos_t), _In(sin_t)]
    (qr,) = _rowwise("q_rope", f_qrope, qr_ins, [_Out("row", (lp, MLA_HEADS * LANE), BF16, grouped=True)],
                     groups=MLA_HEADS, tr=640)
    o1, lse = _flash_fwd(qn, qr, kn, kr, vv, "attention")

    def f_gate1(ridx, g, o, z):
        return (o * _silu(z),)

    gate1_ins = [_In(o1), _In(z1)]
    (og,) = _rowwise("mla_gate", f_gate1, gate1_ins, [_Out("row", (lp, MLA_HEADS * LANE), BF16)])
    y1 = _mm(og, w_out1, "nn", "mla_out")

    def f_final(ridx, g, h, y, t, gain):
        ok = jnp.logical_and(ridx >= ROW0, ridx < ROW0 + seq).astype(F32)

        def rows_loss(h_, y_, gain_):
            err = (h_ + _rms(y_, gain_) - t) * ok
            return 0.5 * jnp.sum(jnp.sum(err * err, axis=1, keepdims=True), axis=0, keepdims=True) / d

        val, vjp = jax.vjp(rows_loss, h, y, gain)
        dh, dy, dgain = vjp(jnp.ones((1, 1), F32))
        return dh, dy, dgain, jnp.broadcast_to(val, (1, LANE))

    dh2, dy1, dpost1, loss_part = _rowwise(
        "loss_head", f_final, [_In(h1), _In(y1), _In(tgt), _In(post1, "const")],
        [_Out("row", (lp, d)), _Out("row", (lp, d)), _Out("acc", (1, d)), _Out("acc", (1, LANE))])

    dog = _mm(dy1, w_out1, "nt", "mla_out_dx")
    dw_out1 = _mm(og, dy1, "tn", "mla_out_dw")
    do1, dz1 = _rowwise_vjp("mla_gate_bwd", f_gate1, gate1_ins, [[dog]], [0, 1])
    dqn, dqr = _flash_dq(qn, qr, kn, kr, vv, o1, do1, lse, "attention_dq")
    dkn, dkr, dvv = _flash_dkv(qn, qr, kn, kr, vv, o1, do1, lse, "attention_dkv")
    (dqr_raw,) = _rowwise_vjp("q_rope_bwd", f_qrope, qr_ins, [[dqr]], [0], groups=MLA_HEADS, tr=640)
    dcq_a = _mm(dqn, w_qn, "nt", "q_up_n_dx")
    dcq_b = _mm(dqr_raw, w_qr, "nt", "q_up_r_dx")
    dw_qn = _mm(cq, dqn, "tn", "q_up_n_dw") * Q_PRESCALE
    dw_qr = _mm(cq, dqr_raw, "tn", "q_up_r_dw") * Q_PRESCALE
    dcq_raw, dqln = _rowwise_vjp("q_latent_bwd", f_cq, cq_ins, [[dcq_a, dcq_b]], [0, 1], tr=640)
    dhn1_a = _mm(dcq_raw, w_cq, "nt", "mla_in_q_dx")
    dhn1_b = _mm(dz1, w_z1, "nt", "mla_in_z_dx")
    dw_cq = _mm(hn1, dcq_raw, "tn", "mla_in_q_dw")
    dw_z1 = _mm(hn1, dz1, "tn", "mla_in_z_dw")
    dckv_a = _mm(dkn, w_kn, "nt", "kv_up_k_dx")
    dckv_b = _mm(dvv, w_v, "nt", "kv_up_v_dx")
    dw_kn = _mm(ckv, dkn, "tn", "kv_up_k_dw")
    dw_v = _mm(ckv, dvv, "tn", "kv_up_v_dw")
    dckv_raw, dkr_raw, dkvl = _rowwise_vjp("kv_latent_bwd", f_ckv, ckv_ins, [[dckv_a, dckv_b], [dkr]], [0, 1, 4],
                                           tr=640)
    dhkv_a = _mm(dckv_raw, w_ckv, "nt", "kv_down_c_dx")
    dhkv_b = _mm(dkr_raw, w_kr, "nt", "kv_down_r_dx")
    dw_ckv = _mm(hkv, dckv_raw, "tn", "kv_down_c_dw")
    dw_kr = _mm(hkv, dkr_raw, "tn", "kv_down_r_dw")
    dh0_res, dy0, dpost0, dpre1, dkvn = _rowwise_vjp(
        "mid_bwd", f_mid, mid_ins, [[dh2], [dhn1_a, dhn1_b], [dhkv_a, dhkv_b]], [0, 1, 2, 3, 4])

    dgated0 = _mm(dy0, w_out0, "nt", "gdn_out_dx")
    dw_out0 = _mm(gated0, dy0, "tn", "gdn_out_dw")
    do0, dz0, doutn = _rowwise_vjp("gdn_gate_bwd", f_gate0, gate0_ins, [[dgated0]], [0, 1, 2], groups=nv, tr=640)
    dq0, dk0, dv0, dbeta, dgc = _gdn_bwd(qkv, beta, gc, ckpt, do0, "gdn_scan_bwd")
    db_raw, da_raw, dalog, ddtb = _rowwise_vjp("gdn_gates_bwd", f_ba, ba_ins, [[dbeta], [dgc]], [0, 1, 2, 3])
    dqkv_raw, dconv = _conv_bwd(qkv_raw, conv_w, dq0, dk0, dv0, "gdn_conv_bwd")
    dhn0_a = _mm(dqkv_raw, w_qkv, "nt", "gdn_in_qkv_dx")
    dhn0_b = _mm(dz0, w_z0, "nt", "gdn_in_z_dx")
    dhn0_c = _mm(db_raw, w_b, "nt", "gdn_in_b_dx")
    dhn0_d = _mm(da_raw, w_a, "nt", "gdn_in_a_dx")
    dw_qkv = _mm(hn0, dqkv_raw, "tn", "gdn_in_qkv_dw")
    dw_z0 = _mm(hn0, dz0, "tn", "gdn_in_z_dw")
    dw_b = _mm(hn0, db_raw, "tn", "gdn_in_b_dw")
    dw_a = _mm(hn0, da_raw, "tn", "gdn_in_a_dw")
    dh0, dpre0 = _rowwise_vjp("pre0_bwd", f_pre0, [_In(h0), _In(pre0, "const")],
                              [[dhn0_a, dhn0_b, dhn0_c, dhn0_d], [dh0_res]], [0, 1])

    grad_x = dh0[ROW0:ROW0 + seq][None]
    g_meta = dh0[FRONT:ROW0]
    g_w_in0 = jnp.concatenate([dw_qkv, dw_z0, dw_b[:, :nv], dw_a[:, :nv]], axis=1)
    g_kv_down = jnp.concatenate([dw_ckv, dw_kr[:, :MLA_ROPE]], axis=1)
    g_kv_up = jnp.concatenate([dw_kn.reshape(MLA_KV_RANK, MLA_HEADS, LANE), dw_v.reshape(MLA_KV_RANK, MLA_HEADS, LANE)],
                              axis=2).reshape(MLA_KV_RANK, MLA_HEADS * 2 * LANE)
    g_w_in1 = jnp.concatenate([dw_cq, dw_z1], axis=1)
    g_qup = jnp.concatenate([dw_qn.reshape(MLA_Q_RANK, MLA_HEADS, LANE),
                             dw_qr.reshape(MLA_Q_RANK, MLA_HEADS, LANE)[:, :, :MLA_ROPE]],
                            axis=2).reshape(MLA_Q_RANK, MLA_HEADS * MLA_QK)
    big_g = [g_meta, dconv, dw_out0, g_kv_down, g_kv_up, g_w_in1, g_qup, dw_out1]

    def reduce_to_shard(g_by_chip, tag):
        own, got = _sibling_split(g_by_chip, "grads_sibling_split" + tag)
        chip_part = _add_pair(own, got, "grads_chip_sum" + tag)
        from_chips = _chip_scatter(chip_part, "grads_chip_scatter" + tag)
        half_sum = _sum_slots(from_chips, "grads_total" + tag)
        return _sibling_join(half_sum, "grads_sibling_join" + tag)

    g_flat = reduce_to_shard(_pack_full(big_g, layout), "")
    g_win = reduce_to_shard(jnp.stack([g_w_in0[:, s * win_cols:(s + 1) * win_cols] for s in range(4)]), "_gdn_w_in")

    small_shapes = [(2, d), (2, d), (1, nv), (1, nv), (1, GDN_DK), (d,), (MLA_KV_RANK,), (1, MLA_Q_RANK), (1, LANE)]
    small_part = _pack_small([jnp.concatenate([dpre0, dpre1], axis=0), jnp.concatenate([dpost0, dpost1], axis=0),
                              dalog[:, :nv], ddtb[:, :nv], doutn, dkvn, dkvl, dqln, loss_part])
    small_tot = _all_sum_small(small_part, "small_sum")
    small_g = _unpack_small(small_tot, small_shapes)
    loss = small_g[-1][0, 0]

    d_flat, m_flat, v_flat = _adamw(w_flat, g_flat, _pack_shards(big_m), _pack_shards(big_v), "adamw_sharded")
    win_step = _adamw(gdn_w_in[0], g_win, m_gdn_w_in[0], v_gdn_w_in[0], "adamw_gdn_w_in")
    small_w = [pre_norm, post_norm, gdn_a_log, gdn_dt_bias, gdn_out_norm, kv_norm, kv_latent_norm, mla_q_latent_norm]
    small_m = [m_pre_norm, m_post_norm, m_gdn_a_log, m_gdn_dt_bias, m_gdn_out_norm, m_kv_norm, m_kv_latent_norm,
               m_mla_q_latent_norm]
    small_v = [v_pre_norm, v_post_norm, v_gdn_a_log, v_gdn_dt_bias, v_gdn_out_norm, v_kv_norm, v_kv_latent_norm,
               v_mla_q_latent_norm]
    g_small_flat = _pack_small(small_g[:-1])
    ds_flat, ms_flat, vs_flat = _adamw(_pack_small(small_w), g_small_flat, _pack_small(small_m), _pack_small(small_v),
                                       "adamw_replicated")

    def assemble(big_flat, small_flat, win):
        bigs = dict(zip(big_names, [a.reshape(w.shape) for a, w in zip(
            _unpack_shards(big_flat, layout),
            [meta_tokens, gdn_conv_w, gdn_w_out, kv_w_down, kv_w_up, mla_w_in, mla_w_q_up, mla_w_out])]))
        smalls = dict(zip(["pre_norm", "post_norm", "gdn_a_log", "gdn_dt_bias", "gdn_out_norm", "kv_norm",
                           "kv_latent_norm", "mla_q_latent_norm"], _unpack_small(small_flat, small_shapes[:-1])))
        both = {**bigs, **smalls, "gdn_w_in": win[None]}
        order = ["meta_tokens", "pre_norm", "post_norm", "gdn_w_in", "gdn_conv_w", "gdn_a_log", "gdn_dt_bias",
                 "gdn_out_norm", "gdn_w_out", "kv_norm", "kv_w_down", "kv_latent_norm", "kv_w_up", "mla_w_in",
                 "mla_q_latent_norm", "mla_w_q_up", "mla_w_out"]
        return [both[n] for n in order]

    grads = assemble(g_flat, g_small_flat, g_win)
    deltas = assemble(d_flat, ds_flat, win_step[0])
    new_m = assemble(m_flat, ms_flat, win_step[1])
    new_v = assemble(v_flat, vs_flat, win_step[2])
    return (loss, grad_x, *grads, *deltas, *new_m, *new_v)
```

```python
import functools
import math

import jax
import jax.numpy as jnp
from jax import lax
from jax.experimental import pallas as pl
from jax.experimental.pallas import tpu as pltpu

F32 = jnp.float32
BF16 = jnp.bfloat16
MESH = pl.DeviceIdType.MESH

D_MODEL = 1024
N_META = 16
FRONT = 48
ROW0 = FRONT + N_META
ROW_ALIGN = 640
NORM_EPS = 1e-6
LANE = 128

GDN_QK_HEADS = 8
GDN_V_HEADS = 16
GDN_DK = 128
GDN_CHUNK = 64
GDN_QK_W = 1024
GDN_V_W = 2048
GDN_CONV_W = 4096

MLA_HEADS = 16
MLA_NOPE = 128
MLA_ROPE = 64
MLA_QK = 192
MLA_Q_RANK = 256
MLA_KV_RANK = 128
ROPE_THETA = 10000.0

ADAM_LR = 0.001
ADAM_B1 = 0.9
ADAM_B2 = 0.999
ADAM_EPS = 1e-08
ADAM_WD = 0.01
ADAM_STEP = 10

VMEM_LIMIT_V7X = 56 * 1024 * 1024
NEG = -1e30

_NN = ((1,), (0,))
_NT = ((1,), (1,))
_TN = ((0,), (0,))
_HI = lax.Precision.HIGHEST
_X3 = lax.Precision.HIGH


def _pcall(body, **kw):
    return pl.pallas_call(body, **kw)


def _params(n_axes):
    return pltpu.CompilerParams(dimension_semantics=("arbitrary",) * n_axes, vmem_limit_bytes=VMEM_LIMIT_V7X)


def _dot(a, b, dims, prec=None):
    return lax.dot_general(a, b, (dims, ((), ())), precision=prec, preferred_element_type=F32)


def _bdot(a, b, dims):
    return _dot(a.astype(BF16), b.astype(BF16), dims)


def _hdot(a, b, dims=_NN):
    return _dot(a, b, dims, _HI)


def _fdot(a, b, dims):
    return _dot(a, b, dims)


def _tile(n):
    if n % ROW_ALIGN == 0:
        return ROW_ALIGN
    for t in (1024, 512, 256, 128):
        if n % t == 0:
            return t
    raise ValueError(n)


def _mm(a, b, mode, name, out_dtype=F32):
    if mode == "nn":
        (m, k), (k2, n) = a.shape, b.shape
    elif mode == "nt":
        (m, k), (n, k2) = a.shape, b.shape
    else:
        (k, m), (k2, n) = a.shape, b.shape
    assert k == k2, (a.shape, b.shape, mode)
    tm, tn, tk = _tile(m), _tile(n), _tile(k)
    nk = k // tk
    dims = {"nn": _NN, "nt": _NT, "tn": _TN}[mode]

    def body(a_ref, b_ref, o_ref, acc):
        kk = pl.program_id(2)

        @pl.when(kk == 0)
        def _():
            acc[...] = jnp.zeros_like(acc)

        acc[...] += _bdot(a_ref[...], b_ref[...], dims)

        @pl.when(kk == nk - 1)
        def _():
            o_ref[...] = acc[...].astype(out_dtype)

    if mode == "tn":
        a_spec = pl.BlockSpec((tk, tm), lambda i, j, kk: (kk, i))
    else:
        a_spec = pl.BlockSpec((tm, tk), lambda i, j, kk: (i, kk))
    if mode == "nt":
        b_spec = pl.BlockSpec((tn, tk), lambda i, j, kk: (j, kk))
    else:
        b_spec = pl.BlockSpec((tk, tn), lambda i, j, kk: (kk, j))
    return _pcall(
        body, name=name, grid=(m // tm, n // tn, nk),
        in_specs=[a_spec, b_spec],
        out_specs=pl.BlockSpec((tm, tn), lambda i, j, kk: (i, j)),
        out_shape=jax.ShapeDtypeStruct((m, n), out_dtype),
        scratch_shapes=[pltpu.VMEM((tm, tn), F32)],
        compiler_params=_params(3),
    )(a, b)


class _In:
    def __init__(self, arr, kind="row", grouped=False, goff=0):
        self.arr, self.kind, self.grouped, self.goff = arr, kind, grouped, goff


class _Out:
    def __init__(self, kind, shape, dtype=F32, grouped=False):
        self.kind, self.shape, self.dtype, self.grouped = kind, shape, dtype, grouped


def _rowwise(name, fn, ins, outs, *, groups=1, tr=320):
    lp = next(i.arr.shape[0] for i in ins if i.kind == "row")
    nr = lp // tr
    assert lp % tr == 0

    def in_spec(i):
        w = i.arr.shape[1]
        if i.kind == "row":
            if i.grouped:
                return pl.BlockSpec((tr, LANE), lambda g, r, o=i.goff: (r, g + o))
            return pl.BlockSpec((tr, w), lambda g, r: (r, 0))
        if i.grouped:
            return pl.BlockSpec((i.arr.shape[0], LANE), lambda g, r, o=i.goff: (0, g + o))
        return pl.BlockSpec(i.arr.shape, lambda g, r: (0, 0))

    def out_spec(o):
        if o.kind == "row":
            if o.grouped:
                return pl.BlockSpec((tr, LANE), lambda g, r: (r, g))
            assert groups == 1
            return pl.BlockSpec((tr, o.shape[1]), lambda g, r: (r, 0))
        if o.grouped:
            return pl.BlockSpec((o.shape[0], LANE), lambda g, r: (0, g))
        return pl.BlockSpec(o.shape, lambda g, r: (0, 0))

    n_in = len(ins)

    def body(*refs):
        g = pl.program_id(0)
        r = pl.program_id(1)
        ridx = r * tr + lax.broadcasted_iota(jnp.int32, (tr, 1), 0)
        res = fn(ridx, g, *[ref[...] for ref in refs[:n_in]])
        assert len(res) == len(outs), (name, len(res), len(outs))
        for o, ref, val in zip(outs, refs[n_in:], res):
            if o.kind == "row":
                ref[...] = val.astype(o.dtype)
            else:
                first = (r == 0) if o.grouped else jnp.logical_and(r == 0, g == 0)

                @pl.when(first)
                def _(ref=ref, val=val):
                    ref[...] = val.astype(F32)

                @pl.when(jnp.logical_not(first))
                def _(ref=ref, val=val):
                    ref[...] += val.astype(F32)

    res = _pcall(
        body, name=name, grid=(groups, nr),
        in_specs=[in_spec(i) for i in ins],
        out_specs=[out_spec(o) for o in outs],
        out_shape=[jax.ShapeDtypeStruct(o.shape, o.dtype) for o in outs],
        compiler_params=_params(2),
    )(*[i.arr for i in ins])
    return res


def _rowwise_vjp(name, fn, ins, cots, diff, *, groups=1, tr=320):
    n_in = len(ins)
    grouped = groups > 1
    cot_ins = []
    counts = []
    for arrs in cots:
        counts.append(len(arrs))
        for a in arrs:
            cot_ins.append(_In(a, "row", grouped=grouped and a.shape[1] > LANE))
    lp = next(i.arr.shape[0] for i in ins if i.kind == "row")
    outs = []
    for d in diff:
        i = ins[d]
        if i.kind == "row":
            w = groups * LANE if i.grouped else i.arr.shape[1]
            outs.append(_Out("row", (lp, w), F32, grouped=i.grouped))
        else:
            outs.append(_Out("acc", i.arr.shape, F32, grouped=i.grouped))

    def bfn(ridx, g, *allvals):
        vals = list(allvals[:n_in])
        cvals = allvals[n_in:]

        def f(*dv):
            full = list(vals)
            for i, v in zip(diff, dv):
                full[i] = v
            return tuple(fn(ridx, g, *full))

        primal, vjp = jax.vjp(f, *[vals[i].astype(F32) for i in diff])
        cts = []
        pos = 0
        for k, cnt in enumerate(counts):
            if cnt == 0:
                cts.append(jnp.zeros_like(primal[k]))
            else:
                c = cvals[pos].astype(F32)
                for extra in cvals[pos + 1:pos + cnt]:
                    c = c + extra.astype(F32)
                w = primal[k].shape[1]
                if c.shape[1] != w:
                    c = functools.reduce(jnp.add, [c[:, i * w:(i + 1) * w] for i in range(c.shape[1] // w)])
                cts.append(c.astype(primal[k].dtype))
            pos += cnt
        return vjp(tuple(cts))

    return _rowwise(name, bfn, list(ins) + cot_ins, outs, groups=groups, tr=tr)


def _rms(x, g):
    return x * lax.rsqrt(jnp.mean(x * x, axis=-1, keepdims=True) + NORM_EPS) * g


def _silu(x):
    return x * jax.nn.sigmoid(x)


def _softplus(x):
    return jnp.maximum(x, 0.0) + jnp.log(1.0 + jnp.exp(-jnp.abs(x)))


def _swap_halves(x):
    lane = lax.broadcasted_iota(jnp.int32, x.shape, x.ndim - 1)
    return jnp.where(lane < 32, pltpu.roll(x, LANE - 32, x.ndim - 1), pltpu.roll(x, 32, x.ndim - 1))


@jax.custom_vjp
def _rope(x, c, s):
    return x * c + _swap_halves(x) * s


def _rope_fwd(x, c, s):
    return _rope(x, c, s), (c, s)


def _rope_bwd(res, dy):
    c, s = res
    return dy * c + _swap_halves(dy * s), jnp.zeros_like(c), jnp.zeros_like(s)


_rope.defvjp(_rope_fwd, _rope_bwd)


def _conv_post(c, g):
    s = _silu(c)
    n = s * lax.rsqrt(jnp.sum(s * s, axis=-1, keepdims=True) + NORM_EPS)
    return jnp.where(g < GDN_QK_HEADS, n * (GDN_DK ** -0.5), jnp.where(g < 2 * GDN_QK_HEADS, n, s))


def _conv_taps(xe, w):
    c = xe[8:] * w[3]
    for s in (1, 2, 3):
        c = c + pltpu.roll(xe, s, 0)[8:] * w[3 - s]
    return c


CONV_LANES = 512
CONV_HEADS = CONV_LANES // LANE


def _conv_post_block(c, g):
    return jnp.concatenate([_conv_post(c[:, i * LANE:(i + 1) * LANE], g * CONV_HEADS + i)
                            for i in range(CONV_HEADS)], axis=1)


def _conv_fwd(x, w, name, tr=640):
    lp, width = x.shape
    cl = CONV_LANES
    nr = lp // tr

    def body(x_ref, prev_ref, w_ref, o_ref):
        g = pl.program_id(0)
        r = pl.program_id(1)
        prev = jnp.where(r > 0, prev_ref[...], 0.0)
        xe = jnp.concatenate([prev, x_ref[...]], axis=0)
        o_ref[...] = _conv_post_block(_conv_taps(xe, [w_ref[t:t + 1, :] for t in range(4)]), g)

    return _pcall(
        body, name=name, grid=(width // cl, nr),
        in_specs=[pl.BlockSpec((tr, cl), lambda g, r: (r, g)),
                  pl.BlockSpec((8, cl), lambda g, r: (jnp.maximum(r * (tr // 8) - 1, 0), g)),
                  pl.BlockSpec((4, cl), lambda g, r: (0, g))],
        out_specs=pl.BlockSpec((tr, cl), lambda g, r: (r, g)),
        out_shape=jax.ShapeDtypeStruct((lp, width), F32),
        compiler_params=_params(2),
    )(x, x, w)


def _conv_bwd(x, w, dq, dk, dv, name, tr=640):
    lp, width = x.shape
    cl = CONV_LANES
    nr = lp // tr
    last8 = lp // 8 - 1
    nq = GDN_QK_W // cl

    def body(x_ref, prev_ref, next_ref, w_ref, q_ref, k_ref, v_ref, q_n, k_n, v_n, dx_ref, dw_ref):
        g = pl.program_id(0)
        r = pl.program_id(1)
        w = [w_ref[t:t + 1, :] for t in range(4)]
        not_last = r < nr - 1

        def pick(a, b, c):
            return jnp.where(g < nq, a[...], jnp.where(g < 2 * nq, b[...], c[...]))

        dy = pick(q_ref, k_ref, v_ref)
        dyn = jnp.where(not_last, pick(q_n, k_n, v_n), 0.0)
        prev = jnp.where(r > 0, prev_ref[...], 0.0)
        nxt = jnp.where(not_last, next_ref[...], 0.0)
        xe = jnp.concatenate([prev, x_ref[...], nxt], axis=0)
        ce = _conv_taps(xe, w)
        _, vjp = jax.vjp(lambda c: _conv_post_block(c, g), ce)
        (dce,) = vjp(jnp.concatenate([dy, dyn], axis=0))
        n = tr + 8
        dx = dce * w[3]
        for s in (1, 2, 3):
            dx = dx + pltpu.roll(dce, n - s, 0) * w[3 - s]
        dx_ref[...] = dx[:tr]
        dc = dce[:tr]
        row4 = lax.broadcasted_iota(jnp.int32, (4, cl), 0)
        dw = jnp.zeros((4, cl), F32)
        for s in (0, 1, 2, 3):
            xs = xe[8:8 + tr] if s == 0 else pltpu.roll(xe, s, 0)[8:8 + tr]
            dw = dw + jnp.where(row4 == 3 - s, jnp.sum(dc * xs, axis=0, keepdims=True), 0.0)

        @pl.when(r == 0)
        def _():
            dw_ref[...] = dw

        @pl.when(r > 0)
        def _():
            dw_ref[...] += dw

    def col_q(g):
        return jnp.minimum(g, nq - 1)

    def col_k(g):
        return jnp.clip(g - nq, 0, nq - 1)

    def col_v(g):
        return jnp.maximum(g - 2 * nq, 0)

    def blk(colf):
        return pl.BlockSpec((tr, cl), lambda g, r: (r, colf(g)))

    def nblk(colf):
        return pl.BlockSpec((8, cl), lambda g, r: (jnp.minimum((r + 1) * (tr // 8), last8), colf(g)))

    return _pcall(
        body, name=name, grid=(width // cl, nr),
        in_specs=[pl.BlockSpec((tr, cl), lambda g, r: (r, g)),
                  pl.BlockSpec((8, cl), lambda g, r: (jnp.maximum(r * (tr // 8) - 1, 0), g)),
                  pl.BlockSpec((8, cl), lambda g, r: (jnp.minimum((r + 1) * (tr // 8), last8), g)),
                  pl.BlockSpec((4, cl), lambda g, r: (0, g)),
                  blk(col_q), blk(col_k), blk(col_v), nblk(col_q), nblk(col_k), nblk(col_v)],
        out_specs=[pl.BlockSpec((tr, cl), lambda g, r: (r, g)),
                   pl.BlockSpec((4, cl), lambda g, r: (0, g))],
        out_shape=[jax.ShapeDtypeStruct((lp, width), F32), jax.ShapeDtypeStruct((4, width), F32)],
        compiler_params=_params(2),
    )(x, x, x, w, dq, dk, dv, dq, dk, dv)


def _bmm(a, b, dims, prec=None):
    (ca,), (cb,) = dims
    return lax.dot_general(a, b, (((ca + 1,), (cb + 1,)), ((0,), (0,))), precision=prec,
                           preferred_element_type=F32)


def _inv_impl(m):
    c = m.shape[-1]
    ii = lax.broadcasted_iota(jnp.int32, (c, c), 0)
    jj = lax.broadcasted_iota(jnp.int32, (c, c), 1)
    eye = (ii == jj).astype(F32)

    def same_block(shift):
        return (ii >> shift) == (jj >> shift)

    n1 = jnp.where(same_block(3), -m, 0.0)
    n2 = _bmm(n1, n1, _NN, _X3)
    n4 = _bmm(n2, n2, _NN, _X3)
    d = _bmm(_bmm(eye + n1, eye + n2, _NN, _X3), eye + n4, _NN, _X3)
    shift = 3
    while (1 << shift) < c:
        low = jnp.where(jnp.logical_and(same_block(shift + 1), jnp.logical_not(same_block(shift))), m, 0.0)
        d = d - _bmm(d, _bmm(low, d, _NN, _X3), _NN, _X3)
        shift += 1
    return d


@jax.custom_vjp
def _inv_unit_lower(m):
    return _inv_impl(m)


def _inv_f(m):
    t = _inv_impl(m)
    return t, t


def _inv_b(t, dt):
    c = t.shape[-1]
    ii = lax.broadcasted_iota(jnp.int32, (c, c), 0)
    jj = lax.broadcasted_iota(jnp.int32, (c, c), 1)
    gm = _bmm(t, _bmm(dt, t, _NT, _X3), _TN, _X3)
    return (jnp.where(ii > jj, -gm, 0.0),)


_inv_unit_lower.defvjp(_inv_f, _inv_b)


GDN_HEADS_PER_STEP = 16


def _gdn_group(q, k, v, beta_blk, gc_blk, states, h0):
    hp = GDN_HEADS_PER_STEP
    c = q.shape[0]
    lane = lax.broadcasted_iota(jnp.int32, (1, LANE), 1)
    row8 = lax.broadcasted_iota(jnp.int32, (max(8, hp), LANE), 0)
    lane8 = lax.broadcasted_iota(jnp.int32, (max(8, hp), LANE), 1)
    gcr_all = _hdot((lane8 == h0 + row8).astype(F32), gc_blk, _NT)
    betas, gccs = [], []
    for i in range(hp):
        onehot = (lane == h0 + i).astype(F32)
        betas.append(jnp.sum(beta_blk * onehot, axis=1, keepdims=True))
        gccs.append(jnp.sum(gc_blk * onehot, axis=1, keepdims=True))
    beta = jnp.stack(betas)
    gcc = jnp.stack(gccs)
    gcr = jnp.stack([gcr_all[i:i + 1] for i in range(hp)])
    qh = jnp.stack([q[:, (i // 2) * LANE:(i // 2 + 1) * LANE] for i in range(hp)])
    kh = jnp.stack([k[:, (i // 2) * LANE:(i // 2 + 1) * LANE] for i in range(hp)])
    vh = jnp.stack([v[:, i * LANE:(i + 1) * LANE] for i in range(hp)])
    state = jnp.stack(states)
    ii = lax.broadcasted_iota(jnp.int32, (c, c), 0)
    jj = lax.broadcasted_iota(jnp.int32, (c, c), 1)
    incl = ii >= jj
    dec = jnp.where(incl, jnp.exp(jnp.where(incl, gcc - gcr, 0.0)), 0.0)
    eg = jnp.exp(gcc)
    m = _bmm(kh, kh, _NT) * beta * jnp.where(ii > jj, dec, 0.0)
    t = _inv_unit_lower(m)
    u = _bmm(t, vh * beta, _NN, _X3)
    w = _bmm(t, kh * (beta * eg), _NN, _X3)
    attn = _bmm(qh, kh, _NT) * dec
    rows = lax.broadcasted_iota(jnp.int32, (c, 1), 0)
    gl = jnp.sum(jnp.where(rows == c - 1, gcc, 0.0), axis=1, keepdims=True)
    v_new = u - _bmm(w, state, _NN)
    o = _bmm(qh * eg, state, _NN) + _bmm(attn, v_new, _NN)
    new_state = state * jnp.exp(gl) + _bmm(kh * jnp.exp(gl - gcc), v_new, _TN)
    return jnp.concatenate([o[i] for i in range(hp)], axis=1), tuple(new_state[i] for i in range(hp))


def _gdn_specs(nc, rev):
    def cidx(n):
        return (nc - 1 - n) if rev else n
    hp = GDN_HEADS_PER_STEP
    nqk = GDN_QK_HEADS
    c = GDN_CHUNK
    nq = 2 * nqk // hp
    q_spec = pl.BlockSpec((c, hp // 2 * LANE), lambda n, g: (cidx(n), g))
    k_spec = pl.BlockSpec((c, hp // 2 * LANE), lambda n, g: (cidx(n), nq + g))
    v_spec = pl.BlockSpec((c, hp * LANE), lambda n, g: (cidx(n), nq + g))
    s_spec = pl.BlockSpec((c, LANE), lambda n, g: (cidx(n), 0))
    o_spec = pl.BlockSpec((c, hp * LANE), lambda n, g: (cidx(n), g))
    ck_spec = pl.BlockSpec((hp, 1, GDN_DK, LANE), lambda n, g: (g, cidx(n), 0, 0))
    return q_spec, k_spec, v_spec, s_spec, o_spec, ck_spec


def _gdn_fwd(qkv, beta, gc, name):
    lp = qkv.shape[0]
    nc = lp // GDN_CHUNK
    nh = GDN_V_HEADS
    hp = GDN_HEADS_PER_STEP
    q_spec, k_spec, v_spec, s_spec, o_spec, ck_spec = _gdn_specs(nc, False)

    def body(q_ref, k_ref, v_ref, b_ref, g_ref, o_ref, ck_ref, state):
        n = pl.program_id(0)
        g = pl.program_id(1)

        @pl.when(n == 0)
        def _():
            for i in range(hp):
                state[g * hp + i] = jnp.zeros((GDN_DK, LANE), F32)

        states = tuple(state[g * hp + i] for i in range(hp))
        for i in range(hp):
            ck_ref[i, 0] = states[i]
        o, new_states = _gdn_group(q_ref[...], k_ref[...], v_ref[...], b_ref[...], g_ref[...], states, g * hp)
        o_ref[...] = o
        for i in range(hp):
            state[g * hp + i] = new_states[i]

    return _pcall(
        body, name=name, grid=(nc, nh // hp),
        in_specs=[q_spec, k_spec, v_spec, s_spec, s_spec],
        out_specs=[o_spec, ck_spec],
        out_shape=[jax.ShapeDtypeStruct((lp, GDN_V_W), F32),
                   jax.ShapeDtypeStruct((nh, nc, GDN_DK, LANE), F32)],
        scratch_shapes=[pltpu.VMEM((nh, GDN_DK, LANE), F32)],
        compiler_params=_params(2),
    )(qkv, qkv, qkv, beta, gc)


def _gdn_bwd(qkv, beta, gc, ckpt, do, name):
    lp = qkv.shape[0]
    nc = lp // GDN_CHUNK
    nh = GDN_V_HEADS
    hp = GDN_HEADS_PER_STEP
    q_spec, k_spec, v_spec, s_spec, o_spec, ck_spec = _gdn_specs(nc, True)

    def body(q_ref, k_ref, v_ref, b_ref, g_ref, ck_ref, do_ref,
             dq_ref, dk_ref, dv_ref, db_ref, dg_ref, dstate):
        n = pl.program_id(0)
        g = pl.program_id(1)

        @pl.when(n == 0)
        def _():
            for i in range(hp):
                dstate[g * hp + i] = jnp.zeros((GDN_DK, LANE), F32)

        states = tuple(ck_ref[i, 0] for i in range(hp))
        _, vjp = jax.vjp(lambda q, k, v, b, gg, s: _gdn_group(q, k, v, b, gg, s, g * hp),
                         q_ref[...], k_ref[...], v_ref[...], b_ref[...], g_ref[...], states)
        dq, dk, dv, db, dg, ds = vjp((do_ref[...], tuple(dstate[g * hp + i] for i in range(hp))))
        dq_ref[...] = dq
        dk_ref[...] = dk
        dv_ref[...] = dv
        for i in range(hp):
            dstate[g * hp + i] = ds[i]

        @pl.when(g == 0)
        def _():
            db_ref[...] = db
            dg_ref[...] = dg

        @pl.when(g > 0)
        def _():
            db_ref[...] += db
            dg_ref[...] += dg

    qk_shape = jax.ShapeDtypeStruct((lp, GDN_QK_W), F32)
    big = jax.ShapeDtypeStruct((lp, GDN_V_W), F32)
    small = jax.ShapeDtypeStruct((lp, LANE), F32)
    dq_spec = pl.BlockSpec((GDN_CHUNK, hp // 2 * LANE), lambda n, g: (nc - 1 - n, g))
    return _pcall(
        body, name=name, grid=(nc, nh // hp),
        in_specs=[q_spec, k_spec, v_spec, s_spec, s_spec, ck_spec, o_spec],
        out_specs=[dq_spec, dq_spec, o_spec, s_spec, s_spec],
        out_shape=[qk_shape, qk_shape, big, small, small],
        scratch_shapes=[pltpu.VMEM((nh, GDN_DK, LANE), F32)],
        compiler_params=_params(2),
    )(qkv, qkv, qkv, beta, gc, ckpt, do)


LOG2E = 1.4426950408889634
LN2 = 0.6931471805599453
Q_PRESCALE = MLA_QK ** -0.5 * LOG2E


def _att_mask(i, j, tb, transposed):
    r = lax.broadcasted_iota(jnp.int32, (tb, tb), 0)
    c = lax.broadcasted_iota(jnp.int32, (tb, tb), 1)
    qpos, kpos = (i * tb + c, j * tb + r) if transposed else (i * tb + r, j * tb + c)
    return jnp.logical_and(kpos <= qpos, kpos >= FRONT)


def _causal_pairs(nb, by_key):
    if by_key:
        pairs = [(i, j) for j in range(nb) for i in range(j, nb)]
    else:
        pairs = [(i, j) for i in range(nb) for j in range(i + 1)]
    return jnp.array([p[0] for p in pairs], jnp.int32), jnp.array([p[1] for p in pairs], jnp.int32)


def _masked_and_plain(i, j, step):
    edge = jnp.logical_or(j == i, j == 0)

    @pl.when(jnp.logical_and(edge, j <= i))
    def _():
        step(True)

    @pl.when(jnp.logical_and(jnp.logical_not(edge), j < i))
    def _():
        step(False)


def _cat(a_ref, b_ref):
    return jnp.concatenate([a_ref[...], b_ref[...]], axis=1)


def _flash_fwd(qn, qr, kn, kr, v, name, tb=ROW_ALIGN):
    lp = qn.shape[0]
    nb = lp // tb
    nh = MLA_HEADS
    qi, kj = _causal_pairs(nb, by_key=False)

    def body(qi_ref, kj_ref, qn_ref, qr_ref, kn_ref, kr_ref, v_ref, o_ref, lse_ref, m_s, l_s, acc):
        t = pl.program_id(1)
        i, j = qi_ref[t], kj_ref[t]

        @pl.when(j == 0)
        def _():
            m_s[...] = jnp.full_like(m_s, NEG)
            l_s[...] = jnp.zeros_like(l_s)
            acc[...] = jnp.zeros_like(acc)

        def step(masked):
            s = _dot(_cat(qn_ref, qr_ref), _cat(kn_ref, kr_ref), _NT)
            if masked:
                s = jnp.where(_att_mask(i, j, tb, False), s, NEG)
            m_new = jnp.maximum(m_s[...], jnp.max(s, axis=1, keepdims=True))
            alpha = jnp.exp2(m_s[...] - m_new)
            p = jnp.exp2(s - m_new)
            l_s[...] = alpha * l_s[...] + jnp.sum(p, axis=1, keepdims=True)
            acc[...] = alpha * acc[...] + _dot(p.astype(BF16), v_ref[...], _NN)
            m_s[...] = m_new

        _masked_and_plain(i, j, step)

        @pl.when(j == i)
        def _():
            o_ref[...] = acc[...] / l_s[...]
            lse_ref[...] = jnp.broadcast_to(m_s[...] + jnp.log(l_s[...]) * LOG2E, (tb, LANE))

    qspec = pl.BlockSpec((tb, LANE), lambda h, t, qi_, kj_: (qi_[t], h))
    kspec = pl.BlockSpec((tb, LANE), lambda h, t, qi_, kj_: (kj_[t], h))
    krspec = pl.BlockSpec((tb, LANE), lambda h, t, qi_, kj_: (kj_[t], 0))
    shp = jax.ShapeDtypeStruct((lp, nh * LANE), F32)
    return _pcall(
        body, name=name, out_shape=[shp, shp],
        grid_spec=pltpu.PrefetchScalarGridSpec(
            num_scalar_prefetch=2, grid=(nh, qi.shape[0]),
            in_specs=[qspec, qspec, kspec, krspec, kspec], out_specs=[qspec, qspec],
            scratch_shapes=[pltpu.VMEM((tb, 1), F32), pltpu.VMEM((tb, 1), F32), pltpu.VMEM((tb, LANE), F32)]),
        compiler_params=_params(2),
    )(qi, kj, qn, qr, kn, kr, v)


def _flash_dq(qn, qr, kn, kr, v, o, do, lse, name, tb=ROW_ALIGN):
    lp = qn.shape[0]
    nb = lp // tb
    nh = MLA_HEADS
    qi, kj = _causal_pairs(nb, by_key=False)

    def body(qi_ref, kj_ref, qn_ref, qr_ref, kn_ref, kr_ref, v_ref, o_ref, do_ref, lse_ref,
             dqn_ref, dqr_ref, dq_acc, delta_s, lse_s):
        t = pl.program_id(1)
        i, j = qi_ref[t], kj_ref[t]

        @pl.when(j == 0)
        def _():
            dq_acc[...] = jnp.zeros_like(dq_acc)
            delta_s[...] = jnp.sum(do_ref[...] * o_ref[...], axis=1, keepdims=True)
            lse_s[...] = lse_ref[:, 0:1]

        def step(masked):
            k = _cat(kn_ref, kr_ref)
            s = _dot(_cat(qn_ref, qr_ref), k, _NT)
            if masked:
                s = jnp.where(_att_mask(i, j, tb, False), s, NEG)
            p = jnp.exp2(s - lse_s[...])
            dp = _dot(do_ref[...].astype(BF16), v_ref[...], _NT)
            ds = p * (dp - delta_s[...])
            dq_acc[...] += _dot(ds.astype(BF16), k, _NN)

        _masked_and_plain(i, j, step)

        @pl.when(j == i)
        def _():
            dqn_ref[...] = dq_acc[:, :LANE] * LN2
            dqr_ref[...] = dq_acc[:, LANE:] * LN2

    qspec = pl.BlockSpec((tb, LANE), lambda h, t, qi_, kj_: (qi_[t], h))
    kspec = pl.BlockSpec((tb, LANE), lambda h, t, qi_, kj_: (kj_[t], h))
    krspec = pl.BlockSpec((tb, LANE), lambda h, t, qi_, kj_: (kj_[t], 0))
    shp = jax.ShapeDtypeStruct((lp, nh * LANE), F32)
    return _pcall(
        body, name=name, out_shape=[shp, shp],
        grid_spec=pltpu.PrefetchScalarGridSpec(
            num_scalar_prefetch=2, grid=(nh, qi.shape[0]),
            in_specs=[qspec, qspec, kspec, krspec, kspec, qspec, qspec, qspec], out_specs=[qspec, qspec],
            scratch_shapes=[pltpu.VMEM((tb, 2 * LANE), F32), pltpu.VMEM((tb, 1), F32), pltpu.VMEM((tb, 1), F32)]),
        compiler_params=_params(2),
    )(qi, kj, qn, qr, kn, kr, v, o, do, lse)


def _flash_dkv(qn, qr, kn, kr, v, o, do, lse, name, tb=ROW_ALIGN):
    lp = qn.shape[0]
    nb = lp // tb
    nh = MLA_HEADS
    qi, kj = _causal_pairs(nb, by_key=True)

    def body(qi_ref, kj_ref, qn_ref, qr_ref, kn_ref, kr_ref, v_ref, o_ref, do_ref, lse_ref,
             dkn_ref, dkr_ref, dv_ref, dk_acc, dv_acc):
        t = pl.program_id(1)
        i, j = qi_ref[t], kj_ref[t]

        @pl.when(i == j)
        def _():
            dk_acc[...] = jnp.zeros_like(dk_acc)
            dv_acc[...] = jnp.zeros_like(dv_acc)

        def step(masked):
            q = _cat(qn_ref, qr_ref)
            st = _dot(_cat(kn_ref, kr_ref), q, _NT)
            if masked:
                st = jnp.where(_att_mask(i, j, tb, True), st, NEG)
            do_blk = do_ref[...]
            lane = lax.broadcasted_iota(jnp.int32, (8, LANE), 1)
            lse_row = _hdot((lane == 0).astype(F32), lse_ref[...], _NT)[0:1]
            delta_row = _hdot(jnp.ones((8, LANE), F32), do_blk * o_ref[...], _NT)[0:1]
            pt = jnp.exp2(st - lse_row)
            do_b = do_blk.astype(BF16)
            dv_acc[...] += _dot(pt.astype(BF16), do_b, _NN)
            dpt = _dot(v_ref[...], do_b, _NT)
            dst = pt * (dpt - delta_row)
            dk_acc[...] += _dot(dst.astype(BF16), q, _NN)

        _masked_and_plain(i, j, step)

        @pl.when(i == nb - 1)
        def _():
            dkn_ref[...] = dk_acc[:, :LANE] * LN2
            dkr_ref[...] = dk_acc[:, LANE:] * LN2
            dv_ref[...] = dv_acc[...]

    qspec = pl.BlockSpec((tb, LANE), lambda h, t, qi_, kj_: (qi_[t], h))
    kspec = pl.BlockSpec((tb, LANE), lambda h, t, qi_, kj_: (kj_[t], h))
    krspec = pl.BlockSpec((tb, LANE), lambda h, t, qi_, kj_: (kj_[t], 0))
    shp = jax.ShapeDtypeStruct((lp, nh * LANE), F32)
    return _pcall(
        body, name=name, out_shape=[shp, shp, shp],
        grid_spec=pltpu.PrefetchScalarGridSpec(
            num_scalar_prefetch=2, grid=(nh, qi.shape[0]),
            in_specs=[qspec, qspec, kspec, krspec, kspec, qspec, qspec, qspec], out_specs=[kspec, kspec, kspec],
            scratch_shapes=[pltpu.VMEM((tb, 2 * LANE), F32), pltpu.VMEM((tb, LANE), F32)]),
        compiler_params=_params(2),
    )(qi, kj, qn, qr, kn, kr, v, o, do, lse)


ELEMENTWISE_BLOCK_BYTES = 1 << 20


def _row_tile(rows, width, copies=1):
    for t in (1024, 512, 256, 128, 64, 32, 16, 8):
        if rows % t == 0 and t * width * 4 * copies <= ELEMENTWISE_BLOCK_BYTES:
            return t
    return rows


def _adamw(w, g, m, v, name):
    rows, width = w.shape
    tr = _row_tile(rows, width)

    def body(w_ref, g_ref, m_ref, v_ref, d_ref, nm_ref, nv_ref):
        gg = g_ref[...]
        nm = ADAM_B1 * m_ref[...] + (1.0 - ADAM_B1) * gg
        nv = ADAM_B2 * v_ref[...] + (1.0 - ADAM_B2) * jnp.square(gg)
        m_hat = nm / (1.0 - ADAM_B1 ** ADAM_STEP)
        v_hat = nv / (1.0 - ADAM_B2 ** ADAM_STEP)
        d_ref[...] = -ADAM_LR * (m_hat / (jnp.sqrt(v_hat) + ADAM_EPS) + ADAM_WD * w_ref[...])
        nm_ref[...] = nm
        nv_ref[...] = nv

    spec = pl.BlockSpec((tr, width), lambda r: (r, 0))
    shp = jax.ShapeDtypeStruct((rows, width), F32)
    return _pcall(body, name=name, grid=(rows // tr,), in_specs=[spec] * 4, out_specs=[spec] * 3,
                  out_shape=[shp] * 3, compiler_params=_params(1))(w, g, m, v)


def _add_pair(a, b, name):
    s, rows, width = a.shape
    tr = _row_tile(rows, width)

    def body(a_ref, b_ref, o_ref):
        o_ref[...] = a_ref[...] + b_ref[...]

    spec = pl.BlockSpec((1, tr, width), lambda i, r: (i, r, 0))
    return _pcall(body, name=name, grid=(s, rows // tr), in_specs=[spec, spec], out_specs=spec,
                  out_shape=jax.ShapeDtypeStruct(a.shape, F32), compiler_params=_params(2))(a, b)


def _sum_slots(a, name):
    s, rows, width = a.shape
    tr = _row_tile(rows, width, copies=s)

    def body(a_ref, o_ref):
        tot = a_ref[0]
        for k in range(1, s):
            tot = tot + a_ref[k]
        o_ref[...] = tot

    return _pcall(body, name=name, grid=(rows // tr,),
                  in_specs=[pl.BlockSpec((s, tr, width), lambda r: (0, r, 0))],
                  out_specs=pl.BlockSpec((tr, width), lambda r: (r, 0)),
                  out_shape=jax.ShapeDtypeStruct((rows, width), F32), compiler_params=_params(1))(a)


_ANY = pl.BlockSpec(memory_space=pl.ANY)


def _my_place():
    return lax.axis_index("x"), lax.axis_index("y"), lax.axis_index("c")


def _other_chips(x, y):
    return [(1 - x, y), (x, 1 - y), (1 - x, 1 - y)]


def _gather_shards(flat, name):
    rows, width = flat.shape

    def body(x_ref, out_ref, send_sems, recv_sems, local_sem):
        x, y, c = _my_place()
        mine = pltpu.make_async_copy(x_ref, out_ref.at[2 * x + y], local_sem)
        mine.start()
        sends = []
        for k, (px, py) in enumerate(_other_chips(x, y)):
            cp = pltpu.make_async_remote_copy(
                src_ref=x_ref, dst_ref=out_ref.at[2 * x + y], send_sem=send_sems.at[k], recv_sem=recv_sems.at[k],
                device_id=(px, py, c), device_id_type=MESH)
            cp.start()
            sends.append(cp)
        for k, (px, py) in enumerate(_other_chips(x, y)):
            pltpu.make_async_remote_copy(
                src_ref=x_ref, dst_ref=out_ref.at[2 * px + py], send_sem=send_sems.at[k], recv_sem=recv_sems.at[k],
                device_id=(px, py, c), device_id_type=MESH).wait_recv()
        for cp in sends:
            cp.wait_send()
        mine.wait()

    return _pcall(
        body, name=name, in_specs=[_ANY], out_specs=_ANY,
        out_shape=jax.ShapeDtypeStruct((4, rows, width), flat.dtype),
        scratch_shapes=[pltpu.SemaphoreType.DMA((3,)), pltpu.SemaphoreType.DMA((3,)), pltpu.SemaphoreType.DMA],
    )(flat)


def _sibling_split(g, name):
    s, rows, width = g.shape
    half = rows // 2

    def body(g_ref, own_ref, got_ref, send_sems, recv_sems, local_sems):
        x, y, c = _my_place()
        mine_rows = pl.ds(pl.multiple_of(c * half, 8), half)
        sib_rows = pl.ds(pl.multiple_of((1 - c) * half, 8), half)
        local, sends = [], []
        for k in range(s):
            cp = pltpu.make_async_remote_copy(
                src_ref=g_ref.at[k, sib_rows, :], dst_ref=got_ref.at[k],
                send_sem=send_sems.at[k], recv_sem=recv_sems.at[k], device_id=(x, y, 1 - c), device_id_type=MESH)
            cp.start()
            sends.append(cp)
            mine = pltpu.make_async_copy(g_ref.at[k, mine_rows, :], own_ref.at[k], local_sems.at[k])
            mine.start()
            local.append(mine)
        for cp in sends:
            cp.wait_recv()
        for cp in sends:
            cp.wait_send()
        for mine in local:
            mine.wait()

    shp = jax.ShapeDtypeStruct((s, half, width), g.dtype)
    return _pcall(
        body, name=name, in_specs=[_ANY], out_specs=[_ANY, _ANY], out_shape=[shp, shp],
        scratch_shapes=[pltpu.SemaphoreType.DMA((s,)), pltpu.SemaphoreType.DMA((s,)), pltpu.SemaphoreType.DMA((s,))],
    )(g)


def _chip_scatter(p, name):
    s, rows, width = p.shape

    def body(p_ref, out_ref, send_sems, recv_sems, local_sem):
        x, y, c = _my_place()
        me = 2 * x + y
        mine = pltpu.make_async_copy(p_ref.at[me], out_ref.at[me], local_sem)
        mine.start()
        sends = []
        for k, (px, py) in enumerate(_other_chips(x, y)):
            cp = pltpu.make_async_remote_copy(
                src_ref=p_ref.at[2 * px + py], dst_ref=out_ref.at[me], send_sem=send_sems.at[k],
                recv_sem=recv_sems.at[k], device_id=(px, py, c), device_id_type=MESH)
            cp.start()
            sends.append(cp)
        for k, (px, py) in enumerate(_other_chips(x, y)):
            pltpu.make_async_remote_copy(
                src_ref=p_ref.at[me], dst_ref=out_ref.at[2 * px + py], send_sem=send_sems.at[k],
                recv_sem=recv_sems.at[k], device_id=(px, py, c), device_id_type=MESH).wait_recv()
        for cp in sends:
            cp.wait_send()
        mine.wait()

    return _pcall(
        body, name=name, in_specs=[_ANY], out_specs=_ANY,
        out_shape=jax.ShapeDtypeStruct(p.shape, p.dtype),
        scratch_shapes=[pltpu.SemaphoreType.DMA((3,)), pltpu.SemaphoreType.DMA((3,)), pltpu.SemaphoreType.DMA],
    )(p)


def _sibling_join(qh, name):
    half, width = qh.shape
    pieces = 4 if half % 32 == 0 else 1
    rows = half // pieces

    def body(q_ref, out_ref, send_sems, recv_sems, local_sems):
        x, y, c = _my_place()
        local, sends = [], []
        for k in range(pieces):
            src = q_ref.at[pl.ds(k * rows, rows), :]
            dst = out_ref.at[pl.ds(pl.multiple_of(c * half + k * rows, 8), rows), :]
            cp = pltpu.make_async_remote_copy(
                src_ref=src, dst_ref=dst, send_sem=send_sems.at[k], recv_sem=recv_sems.at[k],
                device_id=(x, y, 1 - c), device_id_type=MESH)
            cp.start()
            sends.append(cp)
            mine = pltpu.make_async_copy(src, dst, local_sems.at[k])
            mine.start()
            local.append(mine)
        for k in range(pieces):
            theirs = out_ref.at[pl.ds(pl.multiple_of((1 - c) * half + k * rows, 8), rows), :]
            pltpu.make_async_remote_copy(
                src_ref=q_ref.at[pl.ds(k * rows, rows), :], dst_ref=theirs, send_sem=send_sems.at[k],
                recv_sem=recv_sems.at[k], device_id=(x, y, 1 - c), device_id_type=MESH).wait_recv()
        for cp in sends:
            cp.wait_send()
        for mine in local:
            mine.wait()

    return _pcall(
        body, name=name, in_specs=[_ANY], out_specs=_ANY,
        out_shape=jax.ShapeDtypeStruct((2 * half, width), qh.dtype),
        scratch_shapes=[pltpu.SemaphoreType.DMA((pieces,)), pltpu.SemaphoreType.DMA((pieces,)),
                        pltpu.SemaphoreType.DMA((pieces,))],
    )(qh)


def _all_sum_small(part, name):
    rows, width = part.shape

    def body(p_ref, out_ref, land, send_sems, recv_sems):
        x, y, c = _my_place()
        me = 4 * x + 2 * y + c
        land[me] = p_ref[...]
        sends = []
        for k in range(1, 8):
            peer = (x ^ (k >> 2), y ^ ((k >> 1) & 1), c ^ (k & 1))
            cp = pltpu.make_async_remote_copy(
                src_ref=p_ref, dst_ref=land.at[me], send_sem=send_sems.at[k - 1], recv_sem=recv_sems.at[k - 1],
                device_id=peer, device_id_type=MESH)
            cp.start()
            sends.append(cp)
        for k in range(1, 8):
            px, py, pc = x ^ (k >> 2), y ^ ((k >> 1) & 1), c ^ (k & 1)
            pltpu.make_async_remote_copy(
                src_ref=p_ref, dst_ref=land.at[4 * px + 2 * py + pc], send_sem=send_sems.at[k - 1],
                recv_sem=recv_sems.at[k - 1], device_id=(px, py, pc), device_id_type=MESH).wait_recv()
        for cp in sends:
            cp.wait_send()
        tot = land[0]
        for k in range(1, 8):
            tot = tot + land[k]
        out_ref[...] = tot

    vmem = pl.BlockSpec(memory_space=pltpu.VMEM)
    return _pcall(
        body, name=name, in_specs=[vmem], out_specs=vmem,
        out_shape=jax.ShapeDtypeStruct((rows, width), F32),
        scratch_shapes=[pltpu.VMEM((8, rows, width), F32), pltpu.SemaphoreType.DMA((7,)),
                        pltpu.SemaphoreType.DMA((7,))],
    )(part)


def _big_layout(shards):
    return [(a.shape[0], a.shape[1], ax) for a, ax in shards]


FLAT_ROW_MULTIPLE = 2048


def _pack_shards(arrs):
    flat = jnp.concatenate([a.reshape(-1) for a in arrs])
    return jnp.pad(flat, (0, -flat.shape[0] % (FLAT_ROW_MULTIPLE * LANE))).reshape(-1, LANE)


def _unpack_shards(flat, layout):
    flat = flat.reshape(-1)
    out, off = [], 0
    for r, c, _ in layout:
        out.append(flat[off:off + r * c].reshape(r, c))
        off += r * c
    return out


def _unpack_full(gathered, layout):
    g = gathered.reshape(4, -1)
    out, off = [], 0
    for r, c, ax in layout:
        seg = g[:, off:off + r * c].reshape(4, r, c)
        out.append(seg.transpose(1, 0, 2).reshape(r, 4 * c) if ax == 1 else seg.reshape(4 * r, c))
        off += r * c
    return out


def _pack_full(fulls, layout):
    parts = []
    for a, (r, c, ax) in zip(fulls, layout):
        if ax == 1:
            parts.append(a.reshape(r, 4, c).transpose(1, 0, 2).reshape(4, r * c))
        else:
            parts.append(a.reshape(4, r * c))
    flat = jnp.concatenate(parts, axis=1)
    return jnp.pad(flat, ((0, 0), (0, -flat.shape[1] % (FLAT_ROW_MULTIPLE * LANE)))).reshape(4, -1, LANE)


def _pad_lanes(a, width=LANE):
    return jnp.pad(a, [(0, 0)] * (a.ndim - 1) + [(0, width - a.shape[-1])])


def _pack_small(arrs):
    rows = [_pad_lanes(a.reshape(1, -1), -(-a.size // LANE) * LANE).reshape(-1, LANE) for a in arrs]
    flat = jnp.concatenate(rows, axis=0)
    return jnp.pad(flat, ((0, -flat.shape[0] % 8), (0, 0)))


def _unpack_small(flat, shapes):
    out, off = [], 0
    for shp in shapes:
        n = math.prod(shp)
        nr = -(-n // LANE)
        out.append(flat[off:off + nr].reshape(-1)[:n].reshape(shp))
        off += nr
    return out


def kernel(x, meta_tokens, pre_norm, post_norm, gdn_w_in, gdn_conv_w, gdn_a_log, gdn_dt_bias, gdn_out_norm, gdn_w_out, kv_norm, kv_w_down, kv_latent_norm, kv_w_up, mla_w_in, mla_q_latent_norm, mla_w_q_up, mla_w_out, loss_target, m_meta_tokens, m_pre_norm, m_post_norm, m_gdn_w_in, m_gdn_conv_w, m_gdn_a_log, m_gdn_dt_bias, m_gdn_out_norm, m_gdn_w_out, m_kv_norm, m_kv_w_down, m_kv_latent_norm, m_kv_w_up, m_mla_w_in, m_mla_q_latent_norm, m_mla_w_q_up, m_mla_w_out, v_meta_tokens, v_pre_norm, v_post_norm, v_gdn_w_in, v_gdn_conv_w, v_gdn_a_log, v_gdn_dt_bias, v_gdn_out_norm, v_gdn_w_out, v_kv_norm, v_kv_w_down, v_kv_latent_norm, v_kv_w_up, v_mla_w_in, v_mla_q_latent_norm, v_mla_w_q_up, v_mla_w_out):
    seq = x.shape[1]
    d = D_MODEL
    lp = -(-(ROW0 + seq) // ROW_ALIGN) * ROW_ALIGN
    tail = lp - ROW0 - seq

    big_names = ["meta_tokens", "gdn_conv_w", "gdn_w_out", "kv_w_down", "kv_w_up", "mla_w_in", "mla_w_q_up",
                 "mla_w_out"]
    big_axis = [1, 1, 0, 0, 1, 1, 1, 0]
    big_w = [meta_tokens, gdn_conv_w[0], gdn_w_out[0], kv_w_down, kv_w_up, mla_w_in[0], mla_w_q_up[0], mla_w_out[0]]
    big_m = [m_meta_tokens, m_gdn_conv_w[0], m_gdn_w_out[0], m_kv_w_down, m_kv_w_up, m_mla_w_in[0], m_mla_w_q_up[0],
             m_mla_w_out[0]]
    big_v = [v_meta_tokens, v_gdn_conv_w[0], v_gdn_w_out[0], v_kv_w_down, v_kv_w_up, v_mla_w_in[0], v_mla_w_q_up[0],
             v_mla_w_out[0]]
    layout = _big_layout(list(zip(big_w, big_axis)))
    w_flat = _pack_shards(big_w)
    (meta_f, conv_w, w_out0, kv_down, kv_up, w_in1, w_qup, w_out1) = _unpack_full(
        _gather_shards(w_flat, "gather_weights"), layout)
    w_in0_shards = _gather_shards(gdn_w_in[0], "gather_gdn_w_in")
    w_in0 = jnp.concatenate([w_in0_shards[s] for s in range(4)], axis=1)
    win_cols = gdn_w_in.shape[2]

    nv = GDN_V_HEADS
    w_qkv = w_in0[:, :GDN_CONV_W]
    w_z0 = w_in0[:, GDN_CONV_W:GDN_CONV_W + GDN_V_W]
    w_b = _pad_lanes(w_in0[:, GDN_CONV_W + GDN_V_W:GDN_CONV_W + GDN_V_W + nv])
    w_a = _pad_lanes(w_in0[:, GDN_CONV_W + GDN_V_W + nv:])
    w_ckv = kv_down[:, :MLA_KV_RANK]
    w_kr = _pad_lanes(kv_down[:, MLA_KV_RANK:])
    kvu = kv_up.reshape(MLA_KV_RANK, MLA_HEADS, 2 * LANE)
    w_kn = kvu[:, :, :LANE].reshape(MLA_KV_RANK, MLA_HEADS * LANE)
    w_v = kvu[:, :, LANE:].reshape(MLA_KV_RANK, MLA_HEADS * LANE)
    w_cq = w_in1[:, :MLA_Q_RANK]
    w_z1 = w_in1[:, MLA_Q_RANK:]
    qu = w_qup.reshape(MLA_Q_RANK, MLA_HEADS, MLA_QK)
    w_qn = qu[:, :, :MLA_NOPE].reshape(MLA_Q_RANK, MLA_HEADS * LANE) * Q_PRESCALE
    w_qr = _pad_lanes(qu[:, :, MLA_NOPE:]).reshape(MLA_Q_RANK, MLA_HEADS * LANE) * Q_PRESCALE
    (w_qkv, w_z0, w_b, w_a, w_out0, w_ckv, w_kr, w_kn, w_v, w_cq, w_z1, w_qn, w_qr, w_out1) = [
        w.astype(BF16) for w in (w_qkv, w_z0, w_b, w_a, w_out0, w_ckv, w_kr, w_kn, w_v, w_cq, w_z1, w_qn, w_qr,
                                 w_out1)]

    pre0, pre1 = pre_norm[0:1], pre_norm[1:2]
    post0, post1 = post_norm[0:1], post_norm[1:2]
    a_log = _pad_lanes(gdn_a_log)
    dt_bias = _pad_lanes(gdn_dt_bias)
    kvn = kv_norm.reshape(1, d)
    kvl = kv_latent_norm.reshape(1, MLA_KV_RANK)
    qln = mla_q_latent_norm

    h0 = jnp.concatenate([jnp.zeros((FRONT, d), F32), meta_f, x[0], jnp.zeros((tail, d), F32)], axis=0)
    tgt = jnp.pad(loss_target[0], ((ROW0, tail), (0, 0)))
    pos = jnp.maximum(jnp.arange(lp, dtype=jnp.int32) - FRONT, 0).astype(F32)
    inv = ROPE_THETA ** (-jnp.arange(0, MLA_ROPE, 2, dtype=F32) / MLA_ROPE)
    ang = pos[:, None] * inv[None, :]
    zeros64 = jnp.zeros((lp, LANE - MLA_ROPE), F32)
    cos_t = jnp.concatenate([jnp.cos(ang), jnp.cos(ang), zeros64], axis=1)
    sin_t = jnp.concatenate([-jnp.sin(ang), jnp.sin(ang), zeros64], axis=1)

    def valid_rows(ridx):
        return jnp.logical_and(ridx >= FRONT, ridx < ROW0 + seq)

    def f_pre0(ridx, g, h, gain):
        return _rms(h, gain), h

    (hn0,) = _rowwise("pre0", lambda *a: f_pre0(*a)[:1], [_In(h0), _In(pre0, "const")],
                      [_Out("row", (lp, d), BF16)])
    qkv_raw = _mm(hn0, w_qkv, "nn", "gdn_in_qkv")
    z0 = _mm(hn0, w_z0, "nn", "gdn_in_z")
    b_raw = _mm(hn0, w_b, "nn", "gdn_in_b")
    a_raw = _mm(hn0, w_a, "nn", "gdn_in_a")

    def f_ba(ridx, g, b, a, alog, dtb):
        tr = b.shape[0]
        ok = valid_rows(ridx).astype(F32)
        beta = jax.nn.sigmoid(b) * ok
        gate = -jnp.exp(alog) * _softplus(a + dtb) * ok
        ii = lax.broadcasted_iota(jnp.int32, (tr, tr), 0)
        jj = lax.broadcasted_iota(jnp.int32, (tr, tr), 1)
        tri = jnp.logical_and((ii >> 6) == (jj >> 6), ii >= jj).astype(F32)
        return beta, _hdot(tri, gate)

    ba_ins = [_In(b_raw), _In(a_raw), _In(a_log, "const"), _In(dt_bias, "const")]
    beta, gc = _rowwise("gdn_gates", f_ba, ba_ins, [_Out("row", (lp, LANE)), _Out("row", (lp, LANE))])
    qkv = _conv_fwd(qkv_raw, conv_w, "gdn_conv")
    o0, ckpt = _gdn_fwd(qkv, beta, gc, "gdn_scan")

    def per_head(fn, *arrs):
        n = arrs[0].shape[1] // LANE
        return jnp.concatenate([fn(*[a[:, i * LANE:(i + 1) * LANE] for a in arrs]) for i in range(n)], axis=1)

    def f_gate0(ridx, g, o, z, gain):
        return (per_head(lambda oh, zh: _rms(oh, gain) * _silu(zh), o, z),)

    gate0_ins = [_In(o0), _In(z0), _In(gdn_out_norm, "const")]
    (gated0,) = _rowwise("gdn_gate", f_gate0, gate0_ins, [_Out("row", (lp, GDN_V_W), BF16)])
    y0 = _mm(gated0, w_out0, "nn", "gdn_out")

    def f_mid(ridx, g, h, y, g_post, g_pre, g_kv):
        h1 = h + _rms(y, g_post)
        return h1, _rms(h1, g_pre), _rms(h1, g_kv)

    mid_ins = [_In(h0), _In(y0), _In(post0, "const"), _In(pre1, "const"), _In(kvn, "const")]
    h1, hn1, hkv = _rowwise("mid", f_mid, mid_ins,
                            [_Out("row", (lp, d)), _Out("row", (lp, d), BF16), _Out("row", (lp, d), BF16)])

    ckv_raw = _mm(hkv, w_ckv, "nn", "kv_down_c")
    kr_raw = _mm(hkv, w_kr, "nn", "kv_down_r")

    def f_ckv(ridx, g, c, r, cs, sn, gain):
        return _rms(c, gain), _rope(r, cs, sn)

    ckv_ins = [_In(ckv_raw), _In(kr_raw), _In(cos_t), _In(sin_t), _In(kvl, "const")]
    ckv, kr = _rowwise("kv_latent", f_ckv, ckv_ins, [_Out("row", (lp, LANE)), _Out("row", (lp, LANE), BF16)],
                       tr=640)
    kn = _mm(ckv, w_kn, "nn", "kv_up_k", BF16)
    vv = _mm(ckv, w_v, "nn", "kv_up_v", BF16)
    cq_raw = _mm(hn1, w_cq, "nn", "mla_in_q")
    z1 = _mm(hn1, w_z1, "nn", "mla_in_z")

    def f_cq(ridx, g, c, gain):
        return (_rms(c, gain),)

    cq_ins = [_In(cq_raw), _In(qln, "const")]
    (cq,) = _rowwise("q_latent", f_cq, cq_ins, [_Out("row", (lp, MLA_Q_RANK))], tr=640)
    qn = _mm(cq, w_qn, "nn", "q_up_n", BF16)
    qr_raw = _mm(cq, w_qr, "nn", "q_up_r")

    def f_qrope(ridx, g, r, cs, sn):
        return (per_head(lambda rh: _rope(rh, cs, sn), r),)

    qr_ins = [_In(qr_raw), _In(cos_t), _In(sin_t)]
    (qr,) = _rowwise("q_rope", f_qrope, qr_ins, [_Out("row", (lp, MLA_HEADS * LANE), BF16)])
    o1, lse = _flash_fwd(qn, qr, kn, kr, vv, "attention")

    def f_gate1(ridx, g, o, z):
        return (o * _silu(z),)

    gate1_ins = [_In(o1), _In(z1)]
    (og,) = _rowwise("mla_gate", f_gate1, gate1_ins, [_Out("row", (lp, MLA_HEADS * LANE), BF16)])
    y1 = _mm(og, w_out1, "nn", "mla_out")

    def f_final(ridx, g, h, y, t, gain):
        ok = jnp.logical_and(ridx >= ROW0, ridx < ROW0 + seq).astype(F32)

        def rows_loss(h_, y_, gain_):
            err = (h_ + _rms(y_, gain_) - t) * ok
            return 0.5 * jnp.sum(jnp.sum(err * err, axis=1, keepdims=True), axis=0, keepdims=True) / d

        val, vjp = jax.vjp(rows_loss, h, y, gain)
        dh, dy, dgain = vjp(jnp.ones((1, 1), F32))
        return dh, dy, dgain, jnp.broadcast_to(val, (1, LANE))

    dh2, dy1, dpost1, loss_part = _rowwise(
        "loss_head", f_final, [_In(h1), _In(y1), _In(tgt), _In(post1, "const")],
        [_Out("row", (lp, d)), _Out("row", (lp, d)), _Out("acc", (1, d)), _Out("acc", (1, LANE))])

    dog = _mm(dy1, w_out1, "nt", "mla_out_dx")
    dw_out1 = _mm(og, dy1, "tn", "mla_out_dw")
    do1, dz1 = _rowwise_vjp("mla_gate_bwd", f_gate1, gate1_ins, [[dog]], [0, 1])
    dqn, dqr = _flash_dq(qn, qr, kn, kr, vv, o1, do1, lse, "attention_dq")
    dkn, dkr, dvv = _flash_dkv(qn, qr, kn, kr, vv, o1, do1, lse, "attention_dkv")
    (dqr_raw,) = _rowwise_vjp("q_rope_bwd", f_qrope, qr_ins, [[dqr]], [0])
    dcq_a = _mm(dqn, w_qn, "nt", "q_up_n_dx")
    dcq_b = _mm(dqr_raw, w_qr, "nt", "q_up_r_dx")
    dw_qn = _mm(cq, dqn, "tn", "q_up_n_dw") * Q_PRESCALE
    dw_qr = _mm(cq, dqr_raw, "tn", "q_up_r_dw") * Q_PRESCALE
    dcq_raw, dqln = _rowwise_vjp("q_latent_bwd", f_cq, cq_ins, [[dcq_a, dcq_b]], [0, 1], tr=640)
    dhn1_a = _mm(dcq_raw, w_cq, "nt", "mla_in_q_dx")
    dhn1_b = _mm(dz1, w_z1, "nt", "mla_in_z_dx")
    dw_cq = _mm(hn1, dcq_raw, "tn", "mla_in_q_dw")
    dw_z1 = _mm(hn1, dz1, "tn", "mla_in_z_dw")
    dckv_a = _mm(dkn, w_kn, "nt", "kv_up_k_dx")
    dckv_b = _mm(dvv, w_v, "nt", "kv_up_v_dx")
    dw_kn = _mm(ckv, dkn, "tn", "kv_up_k_dw")
    dw_v = _mm(ckv, dvv, "tn", "kv_up_v_dw")
    dckv_raw, dkr_raw, dkvl = _rowwise_vjp("kv_latent_bwd", f_ckv, ckv_ins, [[dckv_a, dckv_b], [dkr]], [0, 1, 4],
                                           tr=640)
    dhkv_a = _mm(dckv_raw, w_ckv, "nt", "kv_down_c_dx")
    dhkv_b = _mm(dkr_raw, w_kr, "nt", "kv_down_r_dx")
    dw_ckv = _mm(hkv, dckv_raw, "tn", "kv_down_c_dw")
    dw_kr = _mm(hkv, dkr_raw, "tn", "kv_down_r_dw")
    dh0_res, dy0, dpost0, dpre1, dkvn = _rowwise_vjp(
        "mid_bwd", f_mid, mid_ins, [[dh2], [dhn1_a, dhn1_b], [dhkv_a, dhkv_b]], [0, 1, 2, 3, 4])

    dgated0 = _mm(dy0, w_out0, "nt", "gdn_out_dx")
    dw_out0 = _mm(gated0, dy0, "tn", "gdn_out_dw")
    do0, dz0, doutn = _rowwise_vjp("gdn_gate_bwd", f_gate0, gate0_ins, [[dgated0]], [0, 1, 2], tr=160)
    dq0, dk0, dv0, dbeta, dgc = _gdn_bwd(qkv, beta, gc, ckpt, do0, "gdn_scan_bwd")
    db_raw, da_raw, dalog, ddtb = _rowwise_vjp("gdn_gates_bwd", f_ba, ba_ins, [[dbeta], [dgc]], [0, 1, 2, 3])
    dqkv_raw, dconv = _conv_bwd(qkv_raw, conv_w, dq0, dk0, dv0, "gdn_conv_bwd")
    dhn0_a = _mm(dqkv_raw, w_qkv, "nt", "gdn_in_qkv_dx")
    dhn0_b = _mm(dz0, w_z0, "nt", "gdn_in_z_dx")
    dhn0_c = _mm(db_raw, w_b, "nt", "gdn_in_b_dx")
    dhn0_d = _mm(da_raw, w_a, "nt", "gdn_in_a_dx")
    dw_qkv = _mm(hn0, dqkv_raw, "tn", "gdn_in_qkv_dw")
    dw_z0 = _mm(hn0, dz0, "tn", "gdn_in_z_dw")
    dw_b = _mm(hn0, db_raw, "tn", "gdn_in_b_dw")
    dw_a = _mm(hn0, da_raw, "tn", "gdn_in_a_dw")
    dh0, dpre0 = _rowwise_vjp("pre0_bwd", f_pre0, [_In(h0), _In(pre0, "const")],
                              [[dhn0_a, dhn0_b, dhn0_c, dhn0_d], [dh0_res]], [0, 1])

    grad_x = dh0[ROW0:ROW0 + seq][None]
    g_meta = dh0[FRONT:ROW0]
    g_w_in0 = jnp.concatenate([dw_qkv, dw_z0, dw_b[:, :nv], dw_a[:, :nv]], axis=1)
    g_kv_down = jnp.concatenate([dw_ckv, dw_kr[:, :MLA_ROPE]], axis=1)
    g_kv_up = jnp.concatenate([dw_kn.reshape(MLA_KV_RANK, MLA_HEADS, LANE), dw_v.reshape(MLA_KV_RANK, MLA_HEADS, LANE)],
                              axis=2).reshape(MLA_KV_RANK, MLA_HEADS * 2 * LANE)
    g_w_in1 = jnp.concatenate([dw_cq, dw_z1], axis=1)
    g_qup = jnp.concatenate([dw_qn.reshape(MLA_Q_RANK, MLA_HEADS, LANE),
                             dw_qr.reshape(MLA_Q_RANK, MLA_HEADS, LANE)[:, :, :MLA_ROPE]],
                            axis=2).reshape(MLA_Q_RANK, MLA_HEADS * MLA_QK)
    big_g = [g_meta, dconv, dw_out0, g_kv_down, g_kv_up, g_w_in1, g_qup, dw_out1]

    def reduce_to_shard(g_by_chip, tag):
        own, got = _sibling_split(g_by_chip, "grads_sibling_split" + tag)
        chip_part = _add_pair(own, got, "grads_chip_sum" + tag)
        from_chips = _chip_scatter(chip_part, "grads_chip_scatter" + tag)
        half_sum = _sum_slots(from_chips, "grads_total" + tag)
        return _sibling_join(half_sum, "grads_sibling_join" + tag)

    g_flat = reduce_to_shard(_pack_full(big_g, layout), "")
    g_win = reduce_to_shard(jnp.stack([g_w_in0[:, s * win_cols:(s + 1) * win_cols] for s in range(4)]), "_gdn_w_in")

    small_shapes = [(2, d), (2, d), (1, nv), (1, nv), (1, GDN_DK), (d,), (MLA_KV_RANK,), (1, MLA_Q_RANK), (1, LANE)]
    small_part = _pack_small([jnp.concatenate([dpre0, dpre1], axis=0), jnp.concatenate([dpost0, dpost1], axis=0),
                              dalog[:, :nv], ddtb[:, :nv], doutn, dkvn, dkvl, dqln, loss_part])
    small_tot = _all_sum_small(small_part, "small_sum")
    small_g = _unpack_small(small_tot, small_shapes)
    loss = small_g[-1][0, 0]

    d_flat, m_flat, v_flat = _adamw(w_flat, g_flat, _pack_shards(big_m), _pack_shards(big_v), "adamw_sharded")
    win_step = _adamw(gdn_w_in[0], g_win, m_gdn_w_in[0], v_gdn_w_in[0], "adamw_gdn_w_in")
    small_w = [pre_norm, post_norm, gdn_a_log, gdn_dt_bias, gdn_out_norm, kv_norm, kv_latent_norm, mla_q_latent_norm]
    small_m = [m_pre_norm, m_post_norm, m_gdn_a_log, m_gdn_dt_bias, m_gdn_out_norm, m_kv_norm, m_kv_latent_norm,
               m_mla_q_latent_norm]
    small_v = [v_pre_norm, v_post_norm, v_gdn_a_log, v_gdn_dt_bias, v_gdn_out_norm, v_kv_norm, v_kv_latent_norm,
               v_mla_q_latent_norm]
    g_small_flat = _pack_small(small_g[:-1])
    ds_flat, ms_flat, vs_flat = _adamw(_pack_small(small_w), g_small_flat, _pack_small(small_m), _pack_small(small_v),
                                       "adamw_replicated")

    def assemble(big_flat, small_flat, win):
        bigs = dict(zip(big_names, [a.reshape(w.shape) for a, w in zip(
            _unpack_shards(big_flat, layout),
            [meta_tokens, gdn_conv_w, gdn_w_out, kv_w_down, kv_w_up, mla_w_in, mla_w_q_up, mla_w_out])]))
        smalls = dict(zip(["pre_norm", "post_norm", "gdn_a_log", "gdn_dt_bias", "gdn_out_norm", "kv_norm",
                           "kv_latent_norm", "mla_q_latent_norm"], _unpack_small(small_flat, small_shapes[:-1])))
        both = {**bigs, **smalls, "gdn_w_in": win[None]}
        order = ["meta_tokens", "pre_norm", "post_norm", "gdn_w_in", "gdn_conv_w", "gdn_a_log", "gdn_dt_bias",
                 "gdn_out_norm", "gdn_w_out", "kv_norm", "kv_w_down", "kv_latent_norm", "kv_w_up", "mla_w_in",
                 "mla_q_latent_norm", "mla_w_q_up", "mla_w_out"]
        return [both[n] for n in order]

    grads = assemble(g_flat, g_small_flat, g_win)
    deltas = assemble(d_flat, ds_flat, win_step[0])
    new_m = assemble(m_flat, ms_flat, win_step[1])
    new_v = assemble(v_flat, vs_flat, win_step[2])
    return (loss, grad_x, *grads, *deltas, *new_m, *new_v)
```

```python
import functools
import math

import jax
import jax.numpy as jnp
from jax import lax
from jax.experimental import pallas as pl
from jax.experimental.pallas import tpu as pltpu

F32 = jnp.float32
BF16 = jnp.bfloat16
MESH = pl.DeviceIdType.MESH

D_MODEL = 1024
N_META = 16
FRONT = 48
ROW0 = FRONT + N_META
ROW_ALIGN = 640
NORM_EPS = 1e-6
LANE = 128

GDN_QK_HEADS = 8
GDN_V_HEADS = 16
GDN_DK = 128
GDN_CHUNK = 64
GDN_QK_W = 1024
GDN_V_W = 2048
GDN_CONV_W = 4096

MLA_HEADS = 16
MLA_NOPE = 128
MLA_ROPE = 64
MLA_QK = 192
MLA_Q_RANK = 256
MLA_KV_RANK = 128
ROPE_THETA = 10000.0

ADAM_LR = 0.001
ADAM_B1 = 0.9
ADAM_B2 = 0.999
ADAM_EPS = 1e-08
ADAM_WD = 0.01
ADAM_STEP = 10

VMEM_LIMIT_V7X = 56 * 1024 * 1024
NEG = -1e30

_NN = ((1,), (0,))
_NT = ((1,), (1,))
_TN = ((0,), (0,))
_HI = lax.Precision.HIGHEST
_X3 = lax.Precision.HIGH


def _pcall(body, **kw):
    return pl.pallas_call(body, **kw)


def _params(n_axes):
    return pltpu.CompilerParams(dimension_semantics=("arbitrary",) * n_axes, vmem_limit_bytes=VMEM_LIMIT_V7X)


def _dot(a, b, dims, prec=None):
    return lax.dot_general(a, b, (dims, ((), ())), precision=prec, preferred_element_type=F32)


def _bdot(a, b, dims):
    return _dot(a.astype(BF16), b.astype(BF16), dims)


def _hdot(a, b, dims=_NN):
    return _dot(a, b, dims, _HI)


def _fdot(a, b, dims):
    return _dot(a, b, dims)


def _tile(n):
    if n % ROW_ALIGN == 0:
        return ROW_ALIGN
    for t in (1024, 512, 256, 128):
        if n % t == 0:
            return t
    raise ValueError(n)


def _mm(a, b, mode, name, out_dtype=F32):
    if mode == "nn":
        (m, k), (k2, n) = a.shape, b.shape
    elif mode == "nt":
        (m, k), (n, k2) = a.shape, b.shape
    else:
        (k, m), (k2, n) = a.shape, b.shape
    assert k == k2, (a.shape, b.shape, mode)
    tm, tn, tk = _tile(m), _tile(n), _tile(k)
    nk = k // tk
    dims = {"nn": _NN, "nt": _NT, "tn": _TN}[mode]

    def body(a_ref, b_ref, o_ref, acc):
        kk = pl.program_id(2)

        @pl.when(kk == 0)
        def _():
            acc[...] = jnp.zeros_like(acc)

        acc[...] += _bdot(a_ref[...], b_ref[...], dims)

        @pl.when(kk == nk - 1)
        def _():
            o_ref[...] = acc[...].astype(out_dtype)

    if mode == "tn":
        a_spec = pl.BlockSpec((tk, tm), lambda i, j, kk: (kk, i))
    else:
        a_spec = pl.BlockSpec((tm, tk), lambda i, j, kk: (i, kk))
    if mode == "nt":
        b_spec = pl.BlockSpec((tn, tk), lambda i, j, kk: (j, kk))
    else:
        b_spec = pl.BlockSpec((tk, tn), lambda i, j, kk: (kk, j))
    return _pcall(
        body, name=name, grid=(m // tm, n // tn, nk),
        in_specs=[a_spec, b_spec],
        out_specs=pl.BlockSpec((tm, tn), lambda i, j, kk: (i, j)),
        out_shape=jax.ShapeDtypeStruct((m, n), out_dtype),
        scratch_shapes=[pltpu.VMEM((tm, tn), F32)],
        compiler_params=_params(3),
    )(a, b)


class _In:
    def __init__(self, arr, kind="row", grouped=False, goff=0):
        self.arr, self.kind, self.grouped, self.goff = arr, kind, grouped, goff


class _Out:
    def __init__(self, kind, shape, dtype=F32, grouped=False):
        self.kind, self.shape, self.dtype, self.grouped = kind, shape, dtype, grouped


def _rowwise(name, fn, ins, outs, *, groups=1, tr=320):
    lp = next(i.arr.shape[0] for i in ins if i.kind == "row")
    nr = lp // tr
    assert lp % tr == 0

    def in_spec(i):
        w = i.arr.shape[1]
        if i.kind == "row":
            if i.grouped:
                return pl.BlockSpec((tr, LANE), lambda g, r, o=i.goff: (r, g + o))
            return pl.BlockSpec((tr, w), lambda g, r: (r, 0))
        if i.grouped:
            return pl.BlockSpec((i.arr.shape[0], LANE), lambda g, r, o=i.goff: (0, g + o))
        return pl.BlockSpec(i.arr.shape, lambda g, r: (0, 0))

    def out_spec(o):
        if o.kind == "row":
            if o.grouped:
                return pl.BlockSpec((tr, LANE), lambda g, r: (r, g))
            assert groups == 1
            return pl.BlockSpec((tr, o.shape[1]), lambda g, r: (r, 0))
        if o.grouped:
            return pl.BlockSpec((o.shape[0], LANE), lambda g, r: (0, g))
        return pl.BlockSpec(o.shape, lambda g, r: (0, 0))

    n_in = len(ins)

    def body(*refs):
        g = pl.program_id(0)
        r = pl.program_id(1)
        ridx = r * tr + lax.broadcasted_iota(jnp.int32, (tr, 1), 0)
        res = fn(ridx, g, *[ref[...] for ref in refs[:n_in]])
        assert len(res) == len(outs), (name, len(res), len(outs))
        for o, ref, val in zip(outs, refs[n_in:], res):
            if o.kind == "row":
                ref[...] = val.astype(o.dtype)
            else:
                first = (r == 0) if o.grouped else jnp.logical_and(r == 0, g == 0)

                @pl.when(first)
                def _(ref=ref, val=val):
                    ref[...] = val.astype(F32)

                @pl.when(jnp.logical_not(first))
                def _(ref=ref, val=val):
                    ref[...] += val.astype(F32)

    res = _pcall(
        body, name=name, grid=(groups, nr),
        in_specs=[in_spec(i) for i in ins],
        out_specs=[out_spec(o) for o in outs],
        out_shape=[jax.ShapeDtypeStruct(o.shape, o.dtype) for o in outs],
        compiler_params=_params(2),
    )(*[i.arr for i in ins])
    return res


def _rowwise_vjp(name, fn, ins, cots, diff, *, groups=1, tr=320):
    n_in = len(ins)
    grouped = groups > 1
    cot_ins = []
    counts = []
    for arrs in cots:
        counts.append(len(arrs))
        for a in arrs:
            cot_ins.append(_In(a, "row", grouped=grouped and a.shape[1] > LANE))
    lp = next(i.arr.shape[0] for i in ins if i.kind == "row")
    outs = []
    for d in diff:
        i = ins[d]
        if i.kind == "row":
            w = groups * LANE if i.grouped else i.arr.shape[1]
            outs.append(_Out("row", (lp, w), F32, grouped=i.grouped))
        else:
            outs.append(_Out("acc", i.arr.shape, F32, grouped=i.grouped))

    def bfn(ridx, g, *allvals):
        vals = list(allvals[:n_in])
        cvals = allvals[n_in:]

        def f(*dv):
            full = list(vals)
            for i, v in zip(diff, dv):
                full[i] = v
            return tuple(fn(ridx, g, *full))

        primal, vjp = jax.vjp(f, *[vals[i].astype(F32) for i in diff])
        cts = []
        pos = 0
        for k, cnt in enumerate(counts):
            if cnt == 0:
                cts.append(jnp.zeros_like(primal[k]))
            else:
                c = cvals[pos].astype(F32)
                for extra in cvals[pos + 1:pos + cnt]:
                    c = c + extra.astype(F32)
                w = primal[k].shape[1]
                if c.shape[1] != w:
                    c = functools.reduce(jnp.add, [c[:, i * w:(i + 1) * w] for i in range(c.shape[1] // w)])
                cts.append(c.astype(primal[k].dtype))
            pos += cnt
        return vjp(tuple(cts))

    return _rowwise(name, bfn, list(ins) + cot_ins, outs, groups=groups, tr=tr)


def _rms(x, g):
    return x * lax.rsqrt(jnp.mean(x * x, axis=-1, keepdims=True) + NORM_EPS) * g


def _silu(x):
    return x * jax.nn.sigmoid(x)


def _softplus(x):
    return jnp.maximum(x, 0.0) + jnp.log(1.0 + jnp.exp(-jnp.abs(x)))


def _swap_halves(x):
    lane = lax.broadcasted_iota(jnp.int32, x.shape, x.ndim - 1)
    return jnp.where(lane < 32, pltpu.roll(x, LANE - 32, x.ndim - 1), pltpu.roll(x, 32, x.ndim - 1))


@jax.custom_vjp
def _rope(x, c, s):
    return x * c + _swap_halves(x) * s


def _rope_fwd(x, c, s):
    return _rope(x, c, s), (c, s)


def _rope_bwd(res, dy):
    c, s = res
    return dy * c + _swap_halves(dy * s), jnp.zeros_like(c), jnp.zeros_like(s)


_rope.defvjp(_rope_fwd, _rope_bwd)


def _conv_post(c, g):
    s = _silu(c)
    n = s * lax.rsqrt(jnp.sum(s * s, axis=-1, keepdims=True) + NORM_EPS)
    return jnp.where(g < GDN_QK_HEADS, n * (GDN_DK ** -0.5), jnp.where(g < 2 * GDN_QK_HEADS, n, s))


def _conv_taps(xe, w):
    c = xe[8:] * w[3]
    for s in (1, 2, 3):
        c = c + pltpu.roll(xe, s, 0)[8:] * w[3 - s]
    return c


CONV_LANES = 512
CONV_HEADS = CONV_LANES // LANE


def _conv_post_block(c, g):
    return jnp.concatenate([_conv_post(c[:, i * LANE:(i + 1) * LANE], g * CONV_HEADS + i)
                            for i in range(CONV_HEADS)], axis=1)


def _conv_fwd(x, w, name, tr=640):
    lp, width = x.shape
    cl = CONV_LANES
    nr = lp // tr

    def body(x_ref, prev_ref, w_ref, o_ref):
        g = pl.program_id(0)
        r = pl.program_id(1)
        prev = jnp.where(r > 0, prev_ref[...], 0.0)
        xe = jnp.concatenate([prev, x_ref[...]], axis=0)
        o_ref[...] = _conv_post_block(_conv_taps(xe, [w_ref[t:t + 1, :] for t in range(4)]), g)

    return _pcall(
        body, name=name, grid=(width // cl, nr),
        in_specs=[pl.BlockSpec((tr, cl), lambda g, r: (r, g)),
                  pl.BlockSpec((8, cl), lambda g, r: (jnp.maximum(r * (tr // 8) - 1, 0), g)),
                  pl.BlockSpec((4, cl), lambda g, r: (0, g))],
        out_specs=pl.BlockSpec((tr, cl), lambda g, r: (r, g)),
        out_shape=jax.ShapeDtypeStruct((lp, width), F32),
        compiler_params=_params(2),
    )(x, x, w)


def _conv_bwd(x, w, dq, dk, dv, name, tr=640):
    lp, width = x.shape
    cl = CONV_LANES
    nr = lp // tr
    last8 = lp // 8 - 1
    nq = GDN_QK_W // cl

    def body(x_ref, prev_ref, next_ref, w_ref, q_ref, k_ref, v_ref, q_n, k_n, v_n, dx_ref, dw_ref):
        g = pl.program_id(0)
        r = pl.program_id(1)
        w = [w_ref[t:t + 1, :] for t in range(4)]
        not_last = r < nr - 1

        def pick(a, b, c):
            return jnp.where(g < nq, a[...], jnp.where(g < 2 * nq, b[...], c[...]))

        dy = pick(q_ref, k_ref, v_ref)
        dyn = jnp.where(not_last, pick(q_n, k_n, v_n), 0.0)
        prev = jnp.where(r > 0, prev_ref[...], 0.0)
        nxt = jnp.where(not_last, next_ref[...], 0.0)
        xe = jnp.concatenate([prev, x_ref[...], nxt], axis=0)
        ce = _conv_taps(xe, w)
        _, vjp = jax.vjp(lambda c: _conv_post_block(c, g), ce)
        (dce,) = vjp(jnp.concatenate([dy, dyn], axis=0))
        n = tr + 8
        dx = dce * w[3]
        for s in (1, 2, 3):
            dx = dx + pltpu.roll(dce, n - s, 0) * w[3 - s]
        dx_ref[...] = dx[:tr]
        dc = dce[:tr]
        row4 = lax.broadcasted_iota(jnp.int32, (4, cl), 0)
        dw = jnp.zeros((4, cl), F32)
        for s in (0, 1, 2, 3):
            xs = xe[8:8 + tr] if s == 0 else pltpu.roll(xe, s, 0)[8:8 + tr]
            dw = dw + jnp.where(row4 == 3 - s, jnp.sum(dc * xs, axis=0, keepdims=True), 0.0)

        @pl.when(r == 0)
        def _():
            dw_ref[...] = dw

        @pl.when(r > 0)
        def _():
            dw_ref[...] += dw

    def col_q(g):
        return jnp.minimum(g, nq - 1)

    def col_k(g):
        return jnp.clip(g - nq, 0, nq - 1)

    def col_v(g):
        return jnp.maximum(g - 2 * nq, 0)

    def blk(colf):
        return pl.BlockSpec((tr, cl), lambda g, r: (r, colf(g)))

    def nblk(colf):
        return pl.BlockSpec((8, cl), lambda g, r: (jnp.minimum((r + 1) * (tr // 8), last8), colf(g)))

    return _pcall(
        body, name=name, grid=(width // cl, nr),
        in_specs=[pl.BlockSpec((tr, cl), lambda g, r: (r, g)),
                  pl.BlockSpec((8, cl), lambda g, r: (jnp.maximum(r * (tr // 8) - 1, 0), g)),
                  pl.BlockSpec((8, cl), lambda g, r: (jnp.minimum((r + 1) * (tr // 8), last8), g)),
                  pl.BlockSpec((4, cl), lambda g, r: (0, g)),
                  blk(col_q), blk(col_k), blk(col_v), nblk(col_q), nblk(col_k), nblk(col_v)],
        out_specs=[pl.BlockSpec((tr, cl), lambda g, r: (r, g)),
                   pl.BlockSpec((4, cl), lambda g, r: (0, g))],
        out_shape=[jax.ShapeDtypeStruct((lp, width), F32), jax.ShapeDtypeStruct((4, width), F32)],
        compiler_params=_params(2),
    )(x, x, x, w, dq, dk, dv, dq, dk, dv)


def _bmm(a, b, dims, prec=None):
    (ca,), (cb,) = dims
    return lax.dot_general(a, b, (((ca + 1,), (cb + 1,)), ((0,), (0,))), precision=prec,
                           preferred_element_type=F32)


def _inv_impl(m):
    c = m.shape[-1]
    ii = lax.broadcasted_iota(jnp.int32, (c, c), 0)
    jj = lax.broadcasted_iota(jnp.int32, (c, c), 1)
    eye = (ii == jj).astype(F32)

    def same_block(shift):
        return (ii >> shift) == (jj >> shift)

    n1 = jnp.where(same_block(3), -m, 0.0)
    n2 = _bmm(n1, n1, _NN, _X3)
    n4 = _bmm(n2, n2, _NN, _X3)
    d = _bmm(_bmm(eye + n1, eye + n2, _NN, _X3), eye + n4, _NN, _X3)
    shift = 3
    while (1 << shift) < c:
        low = jnp.where(jnp.logical_and(same_block(shift + 1), jnp.logical_not(same_block(shift))), m, 0.0)
        d = d - _bmm(d, _bmm(low, d, _NN, _X3), _NN, _X3)
        shift += 1
    return d


@jax.custom_vjp
def _inv_unit_lower(m):
    return _inv_impl(m)


def _inv_f(m):
    t = _inv_impl(m)
    return t, t


def _inv_b(t, dt):
    c = t.shape[-1]
    ii = lax.broadcasted_iota(jnp.int32, (c, c), 0)
    jj = lax.broadcasted_iota(jnp.int32, (c, c), 1)
    gm = _bmm(t, _bmm(dt, t, _NT, _X3), _TN, _X3)
    return (jnp.where(ii > jj, -gm, 0.0),)


_inv_unit_lower.defvjp(_inv_f, _inv_b)


GDN_HEADS_PER_STEP = 16


def _gdn_group(q, k, v, beta_blk, gc_blk, states, h0):
    hp = GDN_HEADS_PER_STEP
    c = q.shape[0]
    lane = lax.broadcasted_iota(jnp.int32, (1, LANE), 1)
    row8 = lax.broadcasted_iota(jnp.int32, (max(8, hp), LANE), 0)
    lane8 = lax.broadcasted_iota(jnp.int32, (max(8, hp), LANE), 1)
    gcr_all = _hdot((lane8 == h0 + row8).astype(F32), gc_blk, _NT)
    betas, gccs = [], []
    for i in range(hp):
        onehot = (lane == h0 + i).astype(F32)
        betas.append(jnp.sum(beta_blk * onehot, axis=1, keepdims=True))
        gccs.append(jnp.sum(gc_blk * onehot, axis=1, keepdims=True))
    beta = jnp.stack(betas)
    gcc = jnp.stack(gccs)
    gcr = jnp.stack([gcr_all[i:i + 1] for i in range(hp)])
    qh = jnp.stack([q[:, (i // 2) * LANE:(i // 2 + 1) * LANE] for i in range(hp)])
    kh = jnp.stack([k[:, (i // 2) * LANE:(i // 2 + 1) * LANE] for i in range(hp)])
    vh = jnp.stack([v[:, i * LANE:(i + 1) * LANE] for i in range(hp)])
    state = jnp.stack(states)
    ii = lax.broadcasted_iota(jnp.int32, (c, c), 0)
    jj = lax.broadcasted_iota(jnp.int32, (c, c), 1)
    incl = ii >= jj
    dec = jnp.where(incl, jnp.exp(jnp.where(incl, gcc - gcr, 0.0)), 0.0)
    eg = jnp.exp(gcc)
    m = _bmm(kh, kh, _NT) * beta * jnp.where(ii > jj, dec, 0.0)
    t = _inv_unit_lower(m)
    u = _bmm(t, vh * beta, _NN, _X3)
    w = _bmm(t, kh * (beta * eg), _NN, _X3)
    attn = _bmm(qh, kh, _NT) * dec
    rows = lax.broadcasted_iota(jnp.int32, (c, 1), 0)
    gl = jnp.sum(jnp.where(rows == c - 1, gcc, 0.0), axis=1, keepdims=True)
    v_new = u - _bmm(w, state, _NN)
    o = _bmm(qh * eg, state, _NN) + _bmm(attn, v_new, _NN)
    new_state = state * jnp.exp(gl) + _bmm(kh * jnp.exp(gl - gcc), v_new, _TN)
    return jnp.concatenate([o[i] for i in range(hp)], axis=1), tuple(new_state[i] for i in range(hp))


def _gdn_specs(nc, rev):
    def cidx(n):
        return (nc - 1 - n) if rev else n
    hp = GDN_HEADS_PER_STEP
    nqk = GDN_QK_HEADS
    c = GDN_CHUNK
    nq = 2 * nqk // hp
    q_spec = pl.BlockSpec((c, hp // 2 * LANE), lambda n, g: (cidx(n), g))
    k_spec = pl.BlockSpec((c, hp // 2 * LANE), lambda n, g: (cidx(n), nq + g))
    v_spec = pl.BlockSpec((c, hp * LANE), lambda n, g: (cidx(n), nq + g))
    s_spec = pl.BlockSpec((c, LANE), lambda n, g: (cidx(n), 0))
    o_spec = pl.BlockSpec((c, hp * LANE), lambda n, g: (cidx(n), g))
    ck_spec = pl.BlockSpec((hp, 1, GDN_DK, LANE), lambda n, g: (g, cidx(n), 0, 0))
    return q_spec, k_spec, v_spec, s_spec, o_spec, ck_spec


def _gdn_fwd(qkv, beta, gc, name):
    lp = qkv.shape[0]
    nc = lp // GDN_CHUNK
    nh = GDN_V_HEADS
    hp = GDN_HEADS_PER_STEP
    q_spec, k_spec, v_spec, s_spec, o_spec, ck_spec = _gdn_specs(nc, False)

    def body(q_ref, k_ref, v_ref, b_ref, g_ref, o_ref, ck_ref, state):
        n = pl.program_id(0)
        g = pl.program_id(1)

        @pl.when(n == 0)
        def _():
            for i in range(hp):
                state[g * hp + i] = jnp.zeros((GDN_DK, LANE), F32)

        states = tuple(state[g * hp + i] for i in range(hp))
        for i in range(hp):
            ck_ref[i, 0] = states[i]
        o, new_states = _gdn_group(q_ref[...], k_ref[...], v_ref[...], b_ref[...], g_ref[...], states, g * hp)
        o_ref[...] = o
        for i in range(hp):
            state[g * hp + i] = new_states[i]

    return _pcall(
        body, name=name, grid=(nc, nh // hp),
        in_specs=[q_spec, k_spec, v_spec, s_spec, s_spec],
        out_specs=[o_spec, ck_spec],
        out_shape=[jax.ShapeDtypeStruct((lp, GDN_V_W), F32),
                   jax.ShapeDtypeStruct((nh, nc, GDN_DK, LANE), F32)],
        scratch_shapes=[pltpu.VMEM((nh, GDN_DK, LANE), F32)],
        compiler_params=_params(2),
    )(qkv, qkv, qkv, beta, gc)


def _gdn_bwd(qkv, beta, gc, ckpt, do, name):
    lp = qkv.shape[0]
    nc = lp // GDN_CHUNK
    nh = GDN_V_HEADS
    hp = GDN_HEADS_PER_STEP
    q_spec, k_spec, v_spec, s_spec, o_spec, ck_spec = _gdn_specs(nc, True)

    def body(q_ref, k_ref, v_ref, b_ref, g_ref, ck_ref, do_ref,
             dq_ref, dk_ref, dv_ref, db_ref, dg_ref, dstate):
        n = pl.program_id(0)
        g = pl.program_id(1)

        @pl.when(n == 0)
        def _():
            for i in range(hp):
                dstate[g * hp + i] = jnp.zeros((GDN_DK, LANE), F32)

        states = tuple(ck_ref[i, 0] for i in range(hp))
        _, vjp = jax.vjp(lambda q, k, v, b, gg, s: _gdn_group(q, k, v, b, gg, s, g * hp),
                         q_ref[...], k_ref[...], v_ref[...], b_ref[...], g_ref[...], states)
        dq, dk, dv, db, dg, ds = vjp((do_ref[...], tuple(dstate[g * hp + i] for i in range(hp))))
        dq_ref[...] = dq
        dk_ref[...] = dk
        dv_ref[...] = dv
        for i in range(hp):
            dstate[g * hp + i] = ds[i]

        @pl.when(g == 0)
        def _():
            db_ref[...] = db
            dg_ref[...] = dg

        @pl.when(g > 0)
        def _():
            db_ref[...] += db
            dg_ref[...] += dg

    qk_shape = jax.ShapeDtypeStruct((lp, GDN_QK_W), F32)
    big = jax.ShapeDtypeStruct((lp, GDN_V_W), F32)
    small = jax.ShapeDtypeStruct((lp, LANE), F32)
    dq_spec = pl.BlockSpec((GDN_CHUNK, hp // 2 * LANE), lambda n, g: (nc - 1 - n, g))
    return _pcall(
        body, name=name, grid=(nc, nh // hp),
        in_specs=[q_spec, k_spec, v_spec, s_spec, s_spec, ck_spec, o_spec],
        out_specs=[dq_spec, dq_spec, o_spec, s_spec, s_spec],
        out_shape=[qk_shape, qk_shape, big, small, small],
        scratch_shapes=[pltpu.VMEM((nh, GDN_DK, LANE), F32)],
        compiler_params=_params(2),
    )(qkv, qkv, qkv, beta, gc, ckpt, do)


LOG2E = 1.4426950408889634
LN2 = 0.6931471805599453
Q_PRESCALE = MLA_QK ** -0.5 * LOG2E


def _att_mask(i, j, tb, transposed):
    r = lax.broadcasted_iota(jnp.int32, (tb, tb), 0)
    c = lax.broadcasted_iota(jnp.int32, (tb, tb), 1)
    qpos, kpos = (i * tb + c, j * tb + r) if transposed else (i * tb + r, j * tb + c)
    return jnp.logical_and(kpos <= qpos, kpos >= FRONT)


def _causal_pairs(nb, by_key):
    if by_key:
        pairs = [(i, j) for j in range(nb) for i in range(j, nb)]
    else:
        pairs = [(i, j) for i in range(nb) for j in range(i + 1)]
    return jnp.array([p[0] for p in pairs], jnp.int32), jnp.array([p[1] for p in pairs], jnp.int32)


def _masked_and_plain(i, j, step):
    edge = jnp.logical_or(j == i, j == 0)

    @pl.when(jnp.logical_and(edge, j <= i))
    def _():
        step(True)

    @pl.when(jnp.logical_and(jnp.logical_not(edge), j < i))
    def _():
        step(False)


def _cat(a_ref, b_ref):
    return jnp.concatenate([a_ref[...], b_ref[...]], axis=1)


def _flash_fwd(qn, qr, kn, kr, v, name, tb=ROW_ALIGN):
    lp = qn.shape[0]
    nb = lp // tb
    nh = MLA_HEADS
    qi, kj = _causal_pairs(nb, by_key=False)

    def body(qi_ref, kj_ref, qn_ref, qr_ref, kn_ref, kr_ref, v_ref, o_ref, lse_ref, m_s, l_s, acc):
        t = pl.program_id(1)
        i, j = qi_ref[t], kj_ref[t]

        @pl.when(j == 0)
        def _():
            m_s[...] = jnp.full_like(m_s, NEG)
            l_s[...] = jnp.zeros_like(l_s)
            acc[...] = jnp.zeros_like(acc)

        def step(masked):
            s = _dot(_cat(qn_ref, qr_ref), _cat(kn_ref, kr_ref), _NT)
            if masked:
                s = jnp.where(_att_mask(i, j, tb, False), s, NEG)
            m_new = jnp.maximum(m_s[...], jnp.max(s, axis=1, keepdims=True))
            alpha = jnp.exp2(m_s[...] - m_new)
            p = jnp.exp2(s - m_new)
            l_s[...] = alpha * l_s[...] + jnp.sum(p, axis=1, keepdims=True)
            acc[...] = alpha * acc[...] + _dot(p.astype(BF16), v_ref[...], _NN)
            m_s[...] = m_new

        _masked_and_plain(i, j, step)

        @pl.when(j == i)
        def _():
            o_ref[...] = acc[...] / l_s[...]
            lse_ref[...] = jnp.broadcast_to(m_s[...] + jnp.log(l_s[...]) * LOG2E, (tb, LANE))

    qspec = pl.BlockSpec((tb, LANE), lambda h, t, qi_, kj_: (qi_[t], h))
    kspec = pl.BlockSpec((tb, LANE), lambda h, t, qi_, kj_: (kj_[t], h))
    krspec = pl.BlockSpec((tb, LANE), lambda h, t, qi_, kj_: (kj_[t], 0))
    shp = jax.ShapeDtypeStruct((lp, nh * LANE), F32)
    return _pcall(
        body, name=name, out_shape=[shp, shp],
        grid_spec=pltpu.PrefetchScalarGridSpec(
            num_scalar_prefetch=2, grid=(nh, qi.shape[0]),
            in_specs=[qspec, qspec, kspec, krspec, kspec], out_specs=[qspec, qspec],
            scratch_shapes=[pltpu.VMEM((tb, 1), F32), pltpu.VMEM((tb, 1), F32), pltpu.VMEM((tb, LANE), F32)]),
        compiler_params=_params(2),
    )(qi, kj, qn, qr, kn, kr, v)


def _flash_dq(qn, qr, kn, kr, v, o, do, lse, name, tb=ROW_ALIGN):
    lp = qn.shape[0]
    nb = lp // tb
    nh = MLA_HEADS
    qi, kj = _causal_pairs(nb, by_key=False)

    def body(qi_ref, kj_ref, qn_ref, qr_ref, kn_ref, kr_ref, v_ref, o_ref, do_ref, lse_ref,
             dqn_ref, dqr_ref, dq_acc, delta_s, lse_s):
        t = pl.program_id(1)
        i, j = qi_ref[t], kj_ref[t]

        @pl.when(j == 0)
        def _():
            dq_acc[...] = jnp.zeros_like(dq_acc)
            delta_s[...] = jnp.sum(do_ref[...] * o_ref[...], axis=1, keepdims=True)
            lse_s[...] = lse_ref[:, 0:1]

        def step(masked):
            k = _cat(kn_ref, kr_ref)
            s = _dot(_cat(qn_ref, qr_ref), k, _NT)
            if masked:
                s = jnp.where(_att_mask(i, j, tb, False), s, NEG)
            p = jnp.exp2(s - lse_s[...])
            dp = _dot(do_ref[...].astype(BF16), v_ref[...], _NT)
            ds = p * (dp - delta_s[...])
            dq_acc[...] += _dot(ds.astype(BF16), k, _NN)

        _masked_and_plain(i, j, step)

        @pl.when(j == i)
        def _():
            dqn_ref[...] = dq_acc[:, :LANE] * LN2
            dqr_ref[...] = dq_acc[:, LANE:] * LN2

    qspec = pl.BlockSpec((tb, LANE), lambda h, t, qi_, kj_: (qi_[t], h))
    kspec = pl.BlockSpec((tb, LANE), lambda h, t, qi_, kj_: (kj_[t], h))
    krspec = pl.BlockSpec((tb, LANE), lambda h, t, qi_, kj_: (kj_[t], 0))
    shp = jax.ShapeDtypeStruct((lp, nh * LANE), F32)
    return _pcall(
        body, name=name, out_shape=[shp, shp],
        grid_spec=pltpu.PrefetchScalarGridSpec(
            num_scalar_prefetch=2, grid=(nh, qi.shape[0]),
            in_specs=[qspec, qspec, kspec, krspec, kspec, qspec, qspec, qspec], out_specs=[qspec, qspec],
            scratch_shapes=[pltpu.VMEM((tb, 2 * LANE), F32), pltpu.VMEM((tb, 1), F32), pltpu.VMEM((tb, 1), F32)]),
        compiler_params=_params(2),
    )(qi, kj, qn, qr, kn, kr, v, o, do, lse)


def _flash_dkv(qn, qr, kn, kr, v, o, do, lse, name, tb=ROW_ALIGN):
    lp = qn.shape[0]
    nb = lp // tb
    nh = MLA_HEADS
    qi, kj = _causal_pairs(nb, by_key=True)

    def body(qi_ref, kj_ref, qn_ref, qr_ref, kn_ref, kr_ref, v_ref, o_ref, do_ref, lse_ref,
             dkn_ref, dkr_ref, dv_ref, dk_acc, dv_acc):
        t = pl.program_id(1)
        i, j = qi_ref[t], kj_ref[t]

        @pl.when(i == j)
        def _():
            dk_acc[...] = jnp.zeros_like(dk_acc)
            dv_acc[...] = jnp.zeros_like(dv_acc)

        def step(masked):
            q = _cat(qn_ref, qr_ref)
            st = _dot(_cat(kn_ref, kr_ref), q, _NT)
            if masked:
                st = jnp.where(_att_mask(i, j, tb, True), st, NEG)
            do_blk = do_ref[...]
            lane = lax.broadcasted_iota(jnp.int32, (8, LANE), 1)
            lse_row = _hdot((lane == 0).astype(F32), lse_ref[...], _NT)[0:1]
            delta_row = _hdot(jnp.ones((8, LANE), F32), do_blk * o_ref[...], _NT)[0:1]
            pt = jnp.exp2(st - lse_row)
            do_b = do_blk.astype(BF16)
            dv_acc[...] += _dot(pt.astype(BF16), do_b, _NN)
            dpt = _dot(v_ref[...], do_b, _NT)
            dst = pt * (dpt - delta_row)
            dk_acc[...] += _dot(dst.astype(BF16), q, _NN)

        _masked_and_plain(i, j, step)

        @pl.when(i == nb - 1)
        def _():
            dkn_ref[...] = dk_acc[:, :LANE] * LN2
            dkr_ref[...] = dk_acc[:, LANE:] * LN2
            dv_ref[...] = dv_acc[...]

    qspec = pl.BlockSpec((tb, LANE), lambda h, t, qi_, kj_: (qi_[t], h))
    kspec = pl.BlockSpec((tb, LANE), lambda h, t, qi_, kj_: (kj_[t], h))
    krspec = pl.BlockSpec((tb, LANE), lambda h, t, qi_, kj_: (kj_[t], 0))
    shp = jax.ShapeDtypeStruct((lp, nh * LANE), F32)
    return _pcall(
        body, name=name, out_shape=[shp, shp, shp],
        grid_spec=pltpu.PrefetchScalarGridSpec(
            num_scalar_prefetch=2, grid=(nh, qi.shape[0]),
            in_specs=[qspec, qspec, kspec, krspec, kspec, qspec, qspec, qspec], out_specs=[kspec, kspec, kspec],
            scratch_shapes=[pltpu.VMEM((tb, 2 * LANE), F32), pltpu.VMEM((tb, LANE), F32)]),
        compiler_params=_params(2),
    )(qi, kj, qn, qr, kn, kr, v, o, do, lse)


ELEMENTWISE_BLOCK_BYTES = 1 << 20


def _row_tile(rows, width, copies=1):
    for t in (1024, 512, 256, 128, 64, 32, 16, 8):
        if rows % t == 0 and t * width * 4 * copies <= ELEMENTWISE_BLOCK_BYTES:
            return t
    return rows


def _adamw(w, g, m, v, name):
    rows, width = w.shape
    tr = _row_tile(rows, width)

    def body(w_ref, g_ref, m_ref, v_ref, d_ref, nm_ref, nv_ref):
        gg = g_ref[...]
        nm = ADAM_B1 * m_ref[...] + (1.0 - ADAM_B1) * gg
        nv = ADAM_B2 * v_ref[...] + (1.0 - ADAM_B2) * jnp.square(gg)
        m_hat = nm / (1.0 - ADAM_B1 ** ADAM_STEP)
        v_hat = nv / (1.0 - ADAM_B2 ** ADAM_STEP)
        d_ref[...] = -ADAM_LR * (m_hat / (jnp.sqrt(v_hat) + ADAM_EPS) + ADAM_WD * w_ref[...])
        nm_ref[...] = nm
        nv_ref[...] = nv

    spec = pl.BlockSpec((tr, width), lambda r: (r, 0))
    shp = jax.ShapeDtypeStruct((rows, width), F32)
    return _pcall(body, name=name, grid=(rows // tr,), in_specs=[spec] * 4, out_specs=[spec] * 3,
                  out_shape=[shp] * 3, compiler_params=_params(1))(w, g, m, v)


def _add_pair(a, b, name):
    s, rows, width = b.shape
    tr = _row_tile(rows, width)
    nt = rows // tr

    def body(c_ref, a_ref, b_ref, o_ref):
        o_ref[...] = a_ref[...] + b_ref[...]

    spec = pl.BlockSpec((1, tr, width), lambda i, r, c_ref: (i, r, 0))
    return _pcall(
        body, name=name, out_shape=jax.ShapeDtypeStruct(b.shape, F32),
        grid_spec=pltpu.PrefetchScalarGridSpec(
            num_scalar_prefetch=1, grid=(s, nt),
            in_specs=[pl.BlockSpec((1, tr, width), lambda i, r, c_ref: (i, c_ref[0] * nt + r, 0)), spec],
            out_specs=spec),
        compiler_params=_params(2),
    )(_core_index(), a, b)


def _sum_slots(a, name):
    s, rows, width = a.shape
    tr = _row_tile(rows, width, copies=s)

    def body(a_ref, o_ref):
        tot = a_ref[0]
        for k in range(1, s):
            tot = tot + a_ref[k]
        o_ref[...] = tot

    return _pcall(body, name=name, grid=(rows // tr,),
                  in_specs=[pl.BlockSpec((s, tr, width), lambda r: (0, r, 0))],
                  out_specs=pl.BlockSpec((tr, width), lambda r: (r, 0)),
                  out_shape=jax.ShapeDtypeStruct((rows, width), F32), compiler_params=_params(1))(a)


_ANY = pl.BlockSpec(memory_space=pl.ANY)


def _my_place():
    return lax.axis_index("x"), lax.axis_index("y"), lax.axis_index("c")


def _core_index():
    return lax.axis_index("c").astype(jnp.int32).reshape(1)


def _other_chips(x, y):
    return [(1 - x, y), (x, 1 - y), (1 - x, 1 - y)]


def _gather_shards(flat, name):
    rows, width = flat.shape

    def body(x_ref, out_ref, send_sems, recv_sems, local_sem):
        x, y, c = _my_place()
        mine = pltpu.make_async_copy(x_ref, out_ref.at[2 * x + y], local_sem)
        mine.start()
        sends = []
        for k, (px, py) in enumerate(_other_chips(x, y)):
            cp = pltpu.make_async_remote_copy(
                src_ref=x_ref, dst_ref=out_ref.at[2 * x + y], send_sem=send_sems.at[k], recv_sem=recv_sems.at[k],
                device_id=(px, py, c), device_id_type=MESH)
            cp.start()
            sends.append(cp)
        for k, (px, py) in enumerate(_other_chips(x, y)):
            pltpu.make_async_remote_copy(
                src_ref=x_ref, dst_ref=out_ref.at[2 * px + py], send_sem=send_sems.at[k], recv_sem=recv_sems.at[k],
                device_id=(px, py, c), device_id_type=MESH).wait_recv()
        for cp in sends:
            cp.wait_send()
        mine.wait()

    return _pcall(
        body, name=name, in_specs=[_ANY], out_specs=_ANY,
        out_shape=jax.ShapeDtypeStruct((4, rows, width), flat.dtype),
        scratch_shapes=[pltpu.SemaphoreType.DMA((3,)), pltpu.SemaphoreType.DMA((3,)), pltpu.SemaphoreType.DMA],
    )(flat)


def _sibling_split(g, name):
    s, rows, width = g.shape
    half = rows // 2
    tr = _row_tile(half, width)
    nt = half // tr

    def body(c_ref, g_blk, got_ref, send_sem, recv_sem):
        k = pl.program_id(0)
        t = pl.program_id(1)
        x, y, c = _my_place()
        cp = pltpu.make_async_remote_copy(
            src_ref=g_blk.at[0], dst_ref=got_ref.at[k, pl.ds(pl.multiple_of(t * tr, 8), tr), :],
            send_sem=send_sem, recv_sem=recv_sem, device_id=(x, y, 1 - c), device_id_type=MESH)
        cp.start()
        cp.wait_send()

        @pl.when(jnp.logical_and(k == s - 1, t == nt - 1))
        def _():
            pltpu.make_async_remote_copy(
                src_ref=got_ref, dst_ref=got_ref, send_sem=send_sem, recv_sem=recv_sem,
                device_id=(x, y, 1 - c), device_id_type=MESH).wait_recv()

    return _pcall(
        body, name=name, out_shape=jax.ShapeDtypeStruct((s, half, width), g.dtype),
        grid_spec=pltpu.PrefetchScalarGridSpec(
            num_scalar_prefetch=1, grid=(s, nt),
            in_specs=[pl.BlockSpec((1, tr, width), lambda k, t, c_ref: (k, (1 - c_ref[0]) * nt + t, 0))],
            out_specs=_ANY,
            scratch_shapes=[pltpu.SemaphoreType.DMA, pltpu.SemaphoreType.DMA]),
        compiler_params=_params(2),
    )(_core_index(), g)


def _chip_scatter(p, name):
    s, rows, width = p.shape

    def body(p_ref, out_ref, send_sems, recv_sems, local_sem):
        x, y, c = _my_place()
        me = 2 * x + y
        mine = pltpu.make_async_copy(p_ref.at[me], out_ref.at[me], local_sem)
        mine.start()
        sends = []
        for k, (px, py) in enumerate(_other_chips(x, y)):
            cp = pltpu.make_async_remote_copy(
                src_ref=p_ref.at[2 * px + py], dst_ref=out_ref.at[me], send_sem=send_sems.at[k],
                recv_sem=recv_sems.at[k], device_id=(px, py, c), device_id_type=MESH)
            cp.start()
            sends.append(cp)
        for k, (px, py) in enumerate(_other_chips(x, y)):
            pltpu.make_async_remote_copy(
                src_ref=p_ref.at[me], dst_ref=out_ref.at[2 * px + py], send_sem=send_sems.at[k],
                recv_sem=recv_sems.at[k], device_id=(px, py, c), device_id_type=MESH).wait_recv()
        for cp in sends:
            cp.wait_send()
        mine.wait()

    return _pcall(
        body, name=name, in_specs=[_ANY], out_specs=_ANY,
        out_shape=jax.ShapeDtypeStruct(p.shape, p.dtype),
        scratch_shapes=[pltpu.SemaphoreType.DMA((3,)), pltpu.SemaphoreType.DMA((3,)), pltpu.SemaphoreType.DMA],
    )(p)


def _sibling_join(qh, name):
    half, width = qh.shape
    tr = _row_tile(half, width)
    nt = half // tr

    def body(q_blk, out_ref, send_sem, recv_sem, local_sem):
        t = pl.program_id(0)
        x, y, c = _my_place()
        dst = out_ref.at[pl.ds(pl.multiple_of(c * half + t * tr, 8), tr), :]
        cp = pltpu.make_async_remote_copy(
            src_ref=q_blk, dst_ref=dst, send_sem=send_sem, recv_sem=recv_sem,
            device_id=(x, y, 1 - c), device_id_type=MESH)
        cp.start()
        mine = pltpu.make_async_copy(q_blk, dst, local_sem)
        mine.start()
        cp.wait_send()
        mine.wait()

        @pl.when(t == nt - 1)
        def _():
            theirs = out_ref.at[pl.ds(pl.multiple_of((1 - c) * half, 8), half), :]
            pltpu.make_async_remote_copy(
                src_ref=theirs, dst_ref=theirs, send_sem=send_sem, recv_sem=recv_sem,
                device_id=(x, y, 1 - c), device_id_type=MESH).wait_recv()

    return _pcall(
        body, name=name, grid=(nt,),
        in_specs=[pl.BlockSpec((tr, width), lambda t: (t, 0))], out_specs=_ANY,
        out_shape=jax.ShapeDtypeStruct((2 * half, width), qh.dtype),
        scratch_shapes=[pltpu.SemaphoreType.DMA, pltpu.SemaphoreType.DMA, pltpu.SemaphoreType.DMA],
        compiler_params=_params(1),
    )(qh)


def _all_sum_small(part, name):
    rows, width = part.shape

    def body(p_ref, out_ref, land, send_sems, recv_sems):
        x, y, c = _my_place()
        me = 4 * x + 2 * y + c
        land[me] = p_ref[...]
        sends = []
        for k in range(1, 8):
            peer = (x ^ (k >> 2), y ^ ((k >> 1) & 1), c ^ (k & 1))
            cp = pltpu.make_async_remote_copy(
                src_ref=p_ref, dst_ref=land.at[me], send_sem=send_sems.at[k - 1], recv_sem=recv_sems.at[k - 1],
                device_id=peer, device_id_type=MESH)
            cp.start()
            sends.append(cp)
        for k in range(1, 8):
            px, py, pc = x ^ (k >> 2), y ^ ((k >> 1) & 1), c ^ (k & 1)
            pltpu.make_async_remote_copy(
                src_ref=p_ref, dst_ref=land.at[4 * px + 2 * py + pc], send_sem=send_sems.at[k - 1],
                recv_sem=recv_sems.at[k - 1], device_id=(px, py, pc), device_id_type=MESH).wait_recv()
        for cp in sends:
            cp.wait_send()
        tot = land[0]
        for k in range(1, 8):
            tot = tot + land[k]
        out_ref[...] = tot

    vmem = pl.BlockSpec(memory_space=pltpu.VMEM)
    return _pcall(
        body, name=name, in_specs=[vmem], out_specs=vmem,
        out_shape=jax.ShapeDtypeStruct((rows, width), F32),
        scratch_shapes=[pltpu.VMEM((8, rows, width), F32), pltpu.SemaphoreType.DMA((7,)),
                        pltpu.SemaphoreType.DMA((7,))],
    )(part)


def _big_layout(shards):
    return [(a.shape[0], a.shape[1], ax) for a, ax in shards]


FLAT_ROW_MULTIPLE = 2048


def _pack_shards(arrs):
    flat = jnp.concatenate([a.reshape(-1) for a in arrs])
    return jnp.pad(flat, (0, -flat.shape[0] % (FLAT_ROW_MULTIPLE * LANE))).reshape(-1, LANE)


def _unpack_shards(flat, layout):
    flat = flat.reshape(-1)
    out, off = [], 0
    for r, c, _ in layout:
        out.append(flat[off:off + r * c].reshape(r, c))
        off += r * c
    return out


def _unpack_full(gathered, layout):
    g = gathered.reshape(4, -1)
    out, off = [], 0
    for r, c, ax in layout:
        seg = g[:, off:off + r * c].reshape(4, r, c)
        out.append(seg.transpose(1, 0, 2).reshape(r, 4 * c) if ax == 1 else seg.reshape(4 * r, c))
        off += r * c
    return out


def _pack_full(fulls, layout):
    parts = []
    for a, (r, c, ax) in zip(fulls, layout):
        if ax == 1:
            parts.append(a.reshape(r, 4, c).transpose(1, 0, 2).reshape(4, r * c))
        else:
            parts.append(a.reshape(4, r * c))
    flat = jnp.concatenate(parts, axis=1)
    return jnp.pad(flat, ((0, 0), (0, -flat.shape[1] % (FLAT_ROW_MULTIPLE * LANE)))).reshape(4, -1, LANE)


def _pad_lanes(a, width=LANE):
    return jnp.pad(a, [(0, 0)] * (a.ndim - 1) + [(0, width - a.shape[-1])])


def _pack_small(arrs):
    rows = [_pad_lanes(a.reshape(1, -1), -(-a.size // LANE) * LANE).reshape(-1, LANE) for a in arrs]
    flat = jnp.concatenate(rows, axis=0)
    return jnp.pad(flat, ((0, -flat.shape[0] % 8), (0, 0)))


def _unpack_small(flat, shapes):
    out, off = [], 0
    for shp in shapes:
        n = math.prod(shp)
        nr = -(-n // LANE)
        out.append(flat[off:off + nr].reshape(-1)[:n].reshape(shp))
        off += nr
    return out


def kernel(x, meta_tokens, pre_norm, post_norm, gdn_w_in, gdn_conv_w, gdn_a_log, gdn_dt_bias, gdn_out_norm, gdn_w_out, kv_norm, kv_w_down, kv_latent_norm, kv_w_up, mla_w_in, mla_q_latent_norm, mla_w_q_up, mla_w_out, loss_target, m_meta_tokens, m_pre_norm, m_post_norm, m_gdn_w_in, m_gdn_conv_w, m_gdn_a_log, m_gdn_dt_bias, m_gdn_out_norm, m_gdn_w_out, m_kv_norm, m_kv_w_down, m_kv_latent_norm, m_kv_w_up, m_mla_w_in, m_mla_q_latent_norm, m_mla_w_q_up, m_mla_w_out, v_meta_tokens, v_pre_norm, v_post_norm, v_gdn_w_in, v_gdn_conv_w, v_gdn_a_log, v_gdn_dt_bias, v_gdn_out_norm, v_gdn_w_out, v_kv_norm, v_kv_w_down, v_kv_latent_norm, v_kv_w_up, v_mla_w_in, v_mla_q_latent_norm, v_mla_w_q_up, v_mla_w_out):
    seq = x.shape[1]
    d = D_MODEL
    lp = -(-(ROW0 + seq) // ROW_ALIGN) * ROW_ALIGN
    tail = lp - ROW0 - seq

    big_names = ["meta_tokens", "gdn_conv_w", "gdn_w_out", "kv_w_down", "kv_w_up", "mla_w_in", "mla_w_q_up",
                 "mla_w_out"]
    big_axis = [1, 1, 0, 0, 1, 1, 1, 0]
    big_w = [meta_tokens, gdn_conv_w[0], gdn_w_out[0], kv_w_down, kv_w_up, mla_w_in[0], mla_w_q_up[0], mla_w_out[0]]
    big_m = [m_meta_tokens, m_gdn_conv_w[0], m_gdn_w_out[0], m_kv_w_down, m_kv_w_up, m_mla_w_in[0], m_mla_w_q_up[0],
             m_mla_w_out[0]]
    big_v = [v_meta_tokens, v_gdn_conv_w[0], v_gdn_w_out[0], v_kv_w_down, v_kv_w_up, v_mla_w_in[0], v_mla_w_q_up[0],
             v_mla_w_out[0]]
    layout = _big_layout(list(zip(big_w, big_axis)))
    w_flat = _pack_shards(big_w)
    (meta_f, conv_w, w_out0, kv_down, kv_up, w_in1, w_qup, w_out1) = _unpack_full(
        _gather_shards(w_flat, "gather_weights"), layout)
    w_in0_shards = _gather_shards(gdn_w_in[0], "gather_gdn_w_in")
    w_in0 = jnp.concatenate([w_in0_shards[s] for s in range(4)], axis=1)
    win_cols = gdn_w_in.shape[2]

    nv = GDN_V_HEADS
    w_qkv = w_in0[:, :GDN_CONV_W]
    w_z0 = w_in0[:, GDN_CONV_W:GDN_CONV_W + GDN_V_W]
    w_b = _pad_lanes(w_in0[:, GDN_CONV_W + GDN_V_W:GDN_CONV_W + GDN_V_W + nv])
    w_a = _pad_lanes(w_in0[:, GDN_CONV_W + GDN_V_W + nv:])
    w_ckv = kv_down[:, :MLA_KV_RANK]
    w_kr = _pad_lanes(kv_down[:, MLA_KV_RANK:])
    kvu = kv_up.reshape(MLA_KV_RANK, MLA_HEADS, 2 * LANE)
    w_kn = kvu[:, :, :LANE].reshape(MLA_KV_RANK, MLA_HEADS * LANE)
    w_v = kvu[:, :, LANE:].reshape(MLA_KV_RANK, MLA_HEADS * LANE)
    w_cq = w_in1[:, :MLA_Q_RANK]
    w_z1 = w_in1[:, MLA_Q_RANK:]
    qu = w_qup.reshape(MLA_Q_RANK, MLA_HEADS, MLA_QK)
    w_qn = qu[:, :, :MLA_NOPE].reshape(MLA_Q_RANK, MLA_HEADS * LANE) * Q_PRESCALE
    w_qr = _pad_lanes(qu[:, :, MLA_NOPE:]).reshape(MLA_Q_RANK, MLA_HEADS * LANE) * Q_PRESCALE
    (w_qkv, w_z0, w_b, w_a, w_out0, w_ckv, w_kr, w_kn, w_v, w_cq, w_z1, w_qn, w_qr, w_out1) = [
        w.astype(BF16) for w in (w_qkv, w_z0, w_b, w_a, w_out0, w_ckv, w_kr, w_kn, w_v, w_cq, w_z1, w_qn, w_qr,
                                 w_out1)]

    pre0, pre1 = pre_norm[0:1], pre_norm[1:2]
    post0, post1 = post_norm[0:1], post_norm[1:2]
    a_log = _pad_lanes(gdn_a_log)
    dt_bias = _pad_lanes(gdn_dt_bias)
    kvn = kv_norm.reshape(1, d)
    kvl = kv_latent_norm.reshape(1, MLA_KV_RANK)
    qln = mla_q_latent_norm

    h0 = jnp.concatenate([jnp.zeros((FRONT, d), F32), meta_f, x[0], jnp.zeros((tail, d), F32)], axis=0)
    tgt = jnp.pad(loss_target[0], ((ROW0, tail), (0, 0)))
    pos = jnp.maximum(jnp.arange(lp, dtype=jnp.int32) - FRONT, 0).astype(F32)
    inv = ROPE_THETA ** (-jnp.arange(0, MLA_ROPE, 2, dtype=F32) / MLA_ROPE)
    ang = pos[:, None] * inv[None, :]
    zeros64 = jnp.zeros((lp, LANE - MLA_ROPE), F32)
    cos_t = jnp.concatenate([jnp.cos(ang), jnp.cos(ang), zeros64], axis=1)
    sin_t = jnp.concatenate([-jnp.sin(ang), jnp.sin(ang), zeros64], axis=1)

    def valid_rows(ridx):
        return jnp.logical_and(ridx >= FRONT, ridx < ROW0 + seq)

    def f_pre0(ridx, g, h, gain):
        return _rms(h, gain), h

    (hn0,) = _rowwise("pre0", lambda *a: f_pre0(*a)[:1], [_In(h0), _In(pre0, "const")],
                      [_Out("row", (lp, d), BF16)])
    qkv_raw = _mm(hn0, w_qkv, "nn", "gdn_in_qkv")
    z0 = _mm(hn0, w_z0, "nn", "gdn_in_z")
    b_raw = _mm(hn0, w_b, "nn", "gdn_in_b")
    a_raw = _mm(hn0, w_a, "nn", "gdn_in_a")

    def f_ba(ridx, g, b, a, alog, dtb):
        tr = b.shape[0]
        ok = valid_rows(ridx).astype(F32)
        beta = jax.nn.sigmoid(b) * ok
        gate = -jnp.exp(alog) * _softplus(a + dtb) * ok
        ii = lax.broadcasted_iota(jnp.int32, (tr, tr), 0)
        jj = lax.broadcasted_iota(jnp.int32, (tr, tr), 1)
        tri = jnp.logical_and((ii >> 6) == (jj >> 6), ii >= jj).astype(F32)
        return beta, _hdot(tri, gate)

    ba_ins = [_In(b_raw), _In(a_raw), _In(a_log, "const"), _In(dt_bias, "const")]
    beta, gc = _rowwise("gdn_gates", f_ba, ba_ins, [_Out("row", (lp, LANE)), _Out("row", (lp, LANE))])
    qkv = _conv_fwd(qkv_raw, conv_w, "gdn_conv")
    o0, ckpt = _gdn_fwd(qkv, beta, gc, "gdn_scan")

    def per_head(fn, *arrs):
        n = arrs[0].shape[1] // LANE
        return jnp.concatenate([fn(*[a[:, i * LANE:(i + 1) * LANE] for a in arrs]) for i in range(n)], axis=1)

    def f_gate0(ridx, g, o, z, gain):
        return (per_head(lambda oh, zh: _rms(oh, gain) * _silu(zh), o, z),)

    gate0_ins = [_In(o0), _In(z0), _In(gdn_out_norm, "const")]
    (gated0,) = _rowwise("gdn_gate", f_gate0, gate0_ins, [_Out("row", (lp, GDN_V_W), BF16)])
    y0 = _mm(gated0, w_out0, "nn", "gdn_out")

    def f_mid(ridx, g, h, y, g_post, g_pre, g_kv):
        h1 = h + _rms(y, g_post)
        return h1, _rms(h1, g_pre), _rms(h1, g_kv)

    mid_ins = [_In(h0), _In(y0), _In(post0, "const"), _In(pre1, "const"), _In(kvn, "const")]
    h1, hn1, hkv = _rowwise("mid", f_mid, mid_ins,
                            [_Out("row", (lp, d)), _Out("row", (lp, d), BF16), _Out("row", (lp, d), BF16)])

    ckv_raw = _mm(hkv, w_ckv, "nn", "kv_down_c")
    kr_raw = _mm(hkv, w_kr, "nn", "kv_down_r")

    def f_ckv(ridx, g, c, r, cs, sn, gain):
        return _rms(c, gain), _rope(r, cs, sn)

    ckv_ins = [_In(ckv_raw), _In(kr_raw), _In(cos_t), _In(sin_t), _In(kvl, "const")]
    ckv, kr = _rowwise("kv_latent", f_ckv, ckv_ins, [_Out("row", (lp, LANE)), _Out("row", (lp, LANE), BF16)],
                       tr=640)
    kn = _mm(ckv, w_kn, "nn", "kv_up_k", BF16)
    vv = _mm(ckv, w_v, "nn", "kv_up_v", BF16)
    cq_raw = _mm(hn1, w_cq, "nn", "mla_in_q")
    z1 = _mm(hn1, w_z1, "nn", "mla_in_z")

    def f_cq(ridx, g, c, gain):
        return (_rms(c, gain),)

    cq_ins = [_In(cq_raw), _In(qln, "const")]
    (cq,) = _rowwise("q_latent", f_cq, cq_ins, [_Out("row", (lp, MLA_Q_RANK))], tr=640)
    qn = _mm(cq, w_qn, "nn", "q_up_n", BF16)
    qr_raw = _mm(cq, w_qr, "nn", "q_up_r")

    def f_qrope(ridx, g, r, cs, sn):
        return (per_head(lambda rh: _rope(rh, cs, sn), r),)

    qr_ins = [_In(qr_raw), _In(cos_t), _In(sin_t)]
    (qr,) = _rowwise("q_rope", f_qrope, qr_ins, [_Out("row", (lp, MLA_HEADS * LANE), BF16)])
    o1, lse = _flash_fwd(qn, qr, kn, kr, vv, "attention")

    def f_gate1(ridx, g, o, z):
        return (o * _silu(z),)

    gate1_ins = [_In(o1), _In(z1)]
    (og,) = _rowwise("mla_gate", f_gate1, gate1_ins, [_Out("row", (lp, MLA_HEADS * LANE), BF16)])
    y1 = _mm(og, w_out1, "nn", "mla_out")

    def f_final(ridx, g, h, y, t, gain):
        ok = jnp.logical_and(ridx >= ROW0, ridx < ROW0 + seq).astype(F32)

        def rows_loss(h_, y_, gain_):
            err = (h_ + _rms(y_, gain_) - t) * ok
            return 0.5 * jnp.sum(jnp.sum(err * err, axis=1, keepdims=True), axis=0, keepdims=True) / d

        val, vjp = jax.vjp(rows_loss, h, y, gain)
        dh, dy, dgain = vjp(jnp.ones((1, 1), F32))
        return dh, dy, dgain, jnp.broadcast_to(val, (1, LANE))

    dh2, dy1, dpost1, loss_part = _rowwise(
        "loss_head", f_final, [_In(h1), _In(y1), _In(tgt), _In(post1, "const")],
        [_Out("row", (lp, d)), _Out("row", (lp, d)), _Out("acc", (1, d)), _Out("acc", (1, LANE))])

    dog = _mm(dy1, w_out1, "nt", "mla_out_dx")
    dw_out1 = _mm(og, dy1, "tn", "mla_out_dw")
    do1, dz1 = _rowwise_vjp("mla_gate_bwd", f_gate1, gate1_ins, [[dog]], [0, 1])
    dqn, dqr = _flash_dq(qn, qr, kn, kr, vv, o1, do1, lse, "attention_dq")
    dkn, dkr, dvv = _flash_dkv(qn, qr, kn, kr, vv, o1, do1, lse, "attention_dkv")
    (dqr_raw,) = _rowwise_vjp("q_rope_bwd", f_qrope, qr_ins, [[dqr]], [0])
    dcq_a = _mm(dqn, w_qn, "nt", "q_up_n_dx")
    dcq_b = _mm(dqr_raw, w_qr, "nt", "q_up_r_dx")
    dw_qn = _mm(cq, dqn, "tn", "q_up_n_dw") * Q_PRESCALE
    dw_qr = _mm(cq, dqr_raw, "tn", "q_up_r_dw") * Q_PRESCALE
    dcq_raw, dqln = _rowwise_vjp("q_latent_bwd", f_cq, cq_ins, [[dcq_a, dcq_b]], [0, 1], tr=640)
    dhn1_a = _mm(dcq_raw, w_cq, "nt", "mla_in_q_dx")
    dhn1_b = _mm(dz1, w_z1, "nt", "mla_in_z_dx")
    dw_cq = _mm(hn1, dcq_raw, "tn", "mla_in_q_dw")
    dw_z1 = _mm(hn1, dz1, "tn", "mla_in_z_dw")
    dckv_a = _mm(dkn, w_kn, "nt", "kv_up_k_dx")
    dckv_b = _mm(dvv, w_v, "nt", "kv_up_v_dx")
    dw_kn = _mm(ckv, dkn, "tn", "kv_up_k_dw")
    dw_v = _mm(ckv, dvv, "tn", "kv_up_v_dw")
    dckv_raw, dkr_raw, dkvl = _rowwise_vjp("kv_latent_bwd", f_ckv, ckv_ins, [[dckv_a, dckv_b], [dkr]], [0, 1, 4],
                                           tr=640)
    dhkv_a = _mm(dckv_raw, w_ckv, "nt", "kv_down_c_dx")
    dhkv_b = _mm(dkr_raw, w_kr, "nt", "kv_down_r_dx")
    dw_ckv = _mm(hkv, dckv_raw, "tn", "kv_down_c_dw")
    dw_kr = _mm(hkv, dkr_raw, "tn", "kv_down_r_dw")
    dh0_res, dy0, dpost0, dpre1, dkvn = _rowwise_vjp(
        "mid_bwd", f_mid, mid_ins, [[dh2], [dhn1_a, dhn1_b], [dhkv_a, dhkv_b]], [0, 1, 2, 3, 4])

    dgated0 = _mm(dy0, w_out0, "nt", "gdn_out_dx")
    dw_out0 = _mm(gated0, dy0, "tn", "gdn_out_dw")
    do0, dz0, doutn = _rowwise_vjp("gdn_gate_bwd", f_gate0, gate0_ins, [[dgated0]], [0, 1, 2], tr=160)
    dq0, dk0, dv0, dbeta, dgc = _gdn_bwd(qkv, beta, gc, ckpt, do0, "gdn_scan_bwd")
    db_raw, da_raw, dalog, ddtb = _rowwise_vjp("gdn_gates_bwd", f_ba, ba_ins, [[dbeta], [dgc]], [0, 1, 2, 3])
    dqkv_raw, dconv = _conv_bwd(qkv_raw, conv_w, dq0, dk0, dv0, "gdn_conv_bwd")
    dhn0_a = _mm(dqkv_raw, w_qkv, "nt", "gdn_in_qkv_dx")
    dhn0_b = _mm(dz0, w_z0, "nt", "gdn_in_z_dx")
    dhn0_c = _mm(db_raw, w_b, "nt", "gdn_in_b_dx")
    dhn0_d = _mm(da_raw, w_a, "nt", "gdn_in_a_dx")
    dw_qkv = _mm(hn0, dqkv_raw, "tn", "gdn_in_qkv_dw")
    dw_z0 = _mm(hn0, dz0, "tn", "gdn_in_z_dw")
    dw_b = _mm(hn0, db_raw, "tn", "gdn_in_b_dw")
    dw_a = _mm(hn0, da_raw, "tn", "gdn_in_a_dw")
    dh0, dpre0 = _rowwise_vjp("pre0_bwd", f_pre0, [_In(h0), _In(pre0, "const")],
                              [[dhn0_a, dhn0_b, dhn0_c, dhn0_d], [dh0_res]], [0, 1])

    grad_x = dh0[ROW0:ROW0 + seq][None]
    g_meta = dh0[FRONT:ROW0]
    g_w_in0 = jnp.concatenate([dw_qkv, dw_z0, dw_b[:, :nv], dw_a[:, :nv]], axis=1)
    g_kv_down = jnp.concatenate([dw_ckv, dw_kr[:, :MLA_ROPE]], axis=1)
    g_kv_up = jnp.concatenate([dw_kn.reshape(MLA_KV_RANK, MLA_HEADS, LANE), dw_v.reshape(MLA_KV_RANK, MLA_HEADS, LANE)],
                              axis=2).reshape(MLA_KV_RANK, MLA_HEADS * 2 * LANE)
    g_w_in1 = jnp.concatenate([dw_cq, dw_z1], axis=1)
    g_qup = jnp.concatenate([dw_qn.reshape(MLA_Q_RANK, MLA_HEADS, LANE),
                             dw_qr.reshape(MLA_Q_RANK, MLA_HEADS, LANE)[:, :, :MLA_ROPE]],
                            axis=2).reshape(MLA_Q_RANK, MLA_HEADS * MLA_QK)
    big_g = [g_meta, dconv, dw_out0, g_kv_down, g_kv_up, g_w_in1, g_qup, dw_out1]

    def reduce_to_shard(g_by_chip, tag):
        got = _sibling_split(g_by_chip, "grads_sibling_split" + tag)
        chip_part = _add_pair(g_by_chip, got, "grads_chip_sum" + tag)
        from_chips = _chip_scatter(chip_part, "grads_chip_scatter" + tag)
        half_sum = _sum_slots(from_chips, "grads_total" + tag)
        return _sibling_join(half_sum, "grads_sibling_join" + tag)

    g_flat = reduce_to_shard(_pack_full(big_g, layout), "")
    g_win = reduce_to_shard(jnp.stack([g_w_in0[:, s * win_cols:(s + 1) * win_cols] for s in range(4)]), "_gdn_w_in")

    small_shapes = [(2, d), (2, d), (1, nv), (1, nv), (1, GDN_DK), (d,), (MLA_KV_RANK,), (1, MLA_Q_RANK), (1, LANE)]
    small_part = _pack_small([jnp.concatenate([dpre0, dpre1], axis=0), jnp.concatenate([dpost0, dpost1], axis=0),
                              dalog[:, :nv], ddtb[:, :nv], doutn, dkvn, dkvl, dqln, loss_part])
    small_tot = _all_sum_small(small_part, "small_sum")
    small_g = _unpack_small(small_tot, small_shapes)
    loss = small_g[-1][0, 0]

    d_flat, m_flat, v_flat = _adamw(w_flat, g_flat, _pack_shards(big_m), _pack_shards(big_v), "adamw_sharded")
    win_step = _adamw(gdn_w_in[0], g_win, m_gdn_w_in[0], v_gdn_w_in[0], "adamw_gdn_w_in")
    small_w = [pre_norm, post_norm, gdn_a_log, gdn_dt_bias, gdn_out_norm, kv_norm, kv_latent_norm, mla_q_latent_norm]
    small_m = [m_pre_norm, m_post_norm, m_gdn_a_log, m_gdn_dt_bias, m_gdn_out_norm, m_kv_norm, m_kv_latent_norm,
               m_mla_q_latent_norm]
    small_v = [v_pre_norm, v_post_norm, v_gdn_a_log, v_gdn_dt_bias, v_gdn_out_norm, v_kv_norm, v_kv_latent_norm,
               v_mla_q_latent_norm]
    g_small_flat = _pack_small(small_g[:-1])
    ds_flat, ms_flat, vs_flat = _adamw(_pack_small(small_w), g_small_flat, _pack_small(small_m), _pack_small(small_v),
                                       "adamw_replicated")

    def assemble(big_flat, small_flat, win):
        bigs = dict(zip(big_names, [a.reshape(w.shape) for a, w in zip(
            _unpack_shards(big_flat, layout),
            [meta_tokens, gdn_conv_w, gdn_w_out, kv_w_down, kv_w_up, mla_w_in, mla_w_q_up, mla_w_out])]))
        smalls = dict(zip(["pre_norm", "post_norm", "gdn_a_log", "gdn_dt_bias", "gdn_out_norm", "kv_norm",
                           "kv_latent_norm", "mla_q_latent_norm"], _unpack_small(small_flat, small_shapes[:-1])))
        both = {**bigs, **smalls, "gdn_w_in": win[None]}
        order = ["meta_tokens", "pre_norm", "post_norm", "gdn_w_in", "gdn_conv_w", "gdn_a_log", "gdn_dt_bias",
                 "gdn_out_norm", "gdn_w_out", "kv_norm", "kv_w_down", "kv_latent_norm", "kv_w_up", "mla_w_in",
                 "mla_q_latent_norm", "mla_w_q_up", "mla_w_out"]
        return [both[n] for n in order]

    grads = assemble(g_flat, g_small_flat, g_win)
    deltas = assemble(d_flat, ds_flat, win_step[0])
    new_m = assemble(m_flat, ms_flat, win_step[1])
    new_v = assemble(v_flat, vs_flat, win_step[2])
    return (loss, grad_x, *grads, *deltas, *new_m, *new_v)
```

```python
import functools
import math

import jax
import jax.numpy as jnp
from jax import lax
from jax.experimental import pallas as pl
from jax.experimental.pallas import tpu as pltpu

F32 = jnp.float32
BF16 = jnp.bfloat16
MESH = pl.DeviceIdType.MESH

D_MODEL = 1024
N_META = 16
FRONT = 48
ROW0 = FRONT + N_META
ROW_ALIGN = 640
NORM_EPS = 1e-6
LANE = 128

GDN_QK_HEADS = 8
GDN_V_HEADS = 16
GDN_DK = 128
GDN_CHUNK = 64
GDN_QK_W = 1024
GDN_V_W = 2048
GDN_CONV_W = 4096

MLA_HEADS = 16
MLA_NOPE = 128
MLA_ROPE = 64
MLA_QK = 192
MLA_Q_RANK = 256
MLA_KV_RANK = 128
ROPE_THETA = 10000.0

ADAM_LR = 0.001
ADAM_B1 = 0.9
ADAM_B2 = 0.999
ADAM_EPS = 1e-08
ADAM_WD = 0.01
ADAM_STEP = 10

VMEM_LIMIT_V7X = 56 * 1024 * 1024
NEG = -1e30

_NN = ((1,), (0,))
_NT = ((1,), (1,))
_TN = ((0,), (0,))
_HI = lax.Precision.HIGHEST
_X3 = lax.Precision.HIGH


def _pcall(body, **kw):
    return pl.pallas_call(body, **kw)


def _params(n_axes):
    return pltpu.CompilerParams(dimension_semantics=("arbitrary",) * n_axes, vmem_limit_bytes=VMEM_LIMIT_V7X)


def _dot(a, b, dims, prec=None):
    return lax.dot_general(a, b, (dims, ((), ())), precision=prec, preferred_element_type=F32)


def _bdot(a, b, dims):
    return _dot(a.astype(BF16), b.astype(BF16), dims)


def _hdot(a, b, dims=_NN):
    return _dot(a, b, dims, _HI)


def _fdot(a, b, dims):
    return _dot(a, b, dims)


def _tile(n):
    if n % ROW_ALIGN == 0:
        return ROW_ALIGN
    for t in (1024, 512, 256, 128):
        if n % t == 0:
            return t
    raise ValueError(n)


def _mm(a, b, mode, name, out_dtype=F32):
    if mode == "nn":
        (m, k), (k2, n) = a.shape, b.shape
    elif mode == "nt":
        (m, k), (n, k2) = a.shape, b.shape
    else:
        (k, m), (k2, n) = a.shape, b.shape
    assert k == k2, (a.shape, b.shape, mode)
    tm, tn, tk = _tile(m), _tile(n), _tile(k)
    nk = k // tk
    dims = {"nn": _NN, "nt": _NT, "tn": _TN}[mode]

    def body(a_ref, b_ref, o_ref, acc):
        kk = pl.program_id(2)

        @pl.when(kk == 0)
        def _():
            acc[...] = jnp.zeros_like(acc)

        acc[...] += _bdot(a_ref[...], b_ref[...], dims)

        @pl.when(kk == nk - 1)
        def _():
            o_ref[...] = acc[...].astype(out_dtype)

    if mode == "tn":
        a_spec = pl.BlockSpec((tk, tm), lambda i, j, kk: (kk, i))
    else:
        a_spec = pl.BlockSpec((tm, tk), lambda i, j, kk: (i, kk))
    if mode == "nt":
        b_spec = pl.BlockSpec((tn, tk), lambda i, j, kk: (j, kk))
    else:
        b_spec = pl.BlockSpec((tk, tn), lambda i, j, kk: (kk, j))
    return _pcall(
        body, name=name, grid=(m // tm, n // tn, nk),
        in_specs=[a_spec, b_spec],
        out_specs=pl.BlockSpec((tm, tn), lambda i, j, kk: (i, j)),
        out_shape=jax.ShapeDtypeStruct((m, n), out_dtype),
        scratch_shapes=[pltpu.VMEM((tm, tn), F32)],
        compiler_params=_params(3),
    )(a, b)


class _In:
    def __init__(self, arr, kind="row", grouped=False, goff=0):
        self.arr, self.kind, self.grouped, self.goff = arr, kind, grouped, goff


class _Out:
    def __init__(self, kind, shape, dtype=F32, grouped=False):
        self.kind, self.shape, self.dtype, self.grouped = kind, shape, dtype, grouped


def _rowwise(name, fn, ins, outs, *, groups=1, tr=320):
    lp = next(i.arr.shape[0] for i in ins if i.kind == "row")
    nr = lp // tr
    assert lp % tr == 0

    def in_spec(i):
        w = i.arr.shape[1]
        if i.kind == "row":
            if i.grouped:
                return pl.BlockSpec((tr, LANE), lambda g, r, o=i.goff: (r, g + o))
            return pl.BlockSpec((tr, w), lambda g, r: (r, 0))
        if i.grouped:
            return pl.BlockSpec((i.arr.shape[0], LANE), lambda g, r, o=i.goff: (0, g + o))
        return pl.BlockSpec(i.arr.shape, lambda g, r: (0, 0))

    def out_spec(o):
        if o.kind == "row":
            if o.grouped:
                return pl.BlockSpec((tr, LANE), lambda g, r: (r, g))
            assert groups == 1
            return pl.BlockSpec((tr, o.shape[1]), lambda g, r: (r, 0))
        if o.grouped:
            return pl.BlockSpec((o.shape[0], LANE), lambda g, r: (0, g))
        return pl.BlockSpec(o.shape, lambda g, r: (0, 0))

    n_in = len(ins)

    def body(*refs):
        g = pl.program_id(0)
        r = pl.program_id(1)
        ridx = r * tr + lax.broadcasted_iota(jnp.int32, (tr, 1), 0)
        res = fn(ridx, g, *[ref[...] for ref in refs[:n_in]])
        assert len(res) == len(outs), (name, len(res), len(outs))
        for o, ref, val in zip(outs, refs[n_in:], res):
            if o.kind == "row":
                ref[...] = val.astype(o.dtype)
            else:
                first = (r == 0) if o.grouped else jnp.logical_and(r == 0, g == 0)

                @pl.when(first)
                def _(ref=ref, val=val):
                    ref[...] = val.astype(F32)

                @pl.when(jnp.logical_not(first))
                def _(ref=ref, val=val):
                    ref[...] += val.astype(F32)

    res = _pcall(
        body, name=name, grid=(groups, nr),
        in_specs=[in_spec(i) for i in ins],
        out_specs=[out_spec(o) for o in outs],
        out_shape=[jax.ShapeDtypeStruct(o.shape, o.dtype) for o in outs],
        compiler_params=_params(2),
    )(*[i.arr for i in ins])
    return res


def _rowwise_vjp(name, fn, ins, cots, diff, *, groups=1, tr=320):
    n_in = len(ins)
    grouped = groups > 1
    cot_ins = []
    counts = []
    for arrs in cots:
        counts.append(len(arrs))
        for a in arrs:
            cot_ins.append(_In(a, "row", grouped=grouped and a.shape[1] > LANE))
    lp = next(i.arr.shape[0] for i in ins if i.kind == "row")
    outs = []
    for d in diff:
        i = ins[d]
        if i.kind == "row":
            w = groups * LANE if i.grouped else i.arr.shape[1]
            outs.append(_Out("row", (lp, w), F32, grouped=i.grouped))
        else:
            outs.append(_Out("acc", i.arr.shape, F32, grouped=i.grouped))

    def bfn(ridx, g, *allvals):
        vals = list(allvals[:n_in])
        cvals = allvals[n_in:]

        def f(*dv):
            full = list(vals)
            for i, v in zip(diff, dv):
                full[i] = v
            return tuple(fn(ridx, g, *full))

        primal, vjp = jax.vjp(f, *[vals[i].astype(F32) for i in diff])
        cts = []
        pos = 0
        for k, cnt in enumerate(counts):
            if cnt == 0:
                cts.append(jnp.zeros_like(primal[k]))
            else:
                c = cvals[pos].astype(F32)
                for extra in cvals[pos + 1:pos + cnt]:
                    c = c + extra.astype(F32)
                w = primal[k].shape[1]
                if c.shape[1] != w:
                    c = functools.reduce(jnp.add, [c[:, i * w:(i + 1) * w] for i in range(c.shape[1] // w)])
                cts.append(c.astype(primal[k].dtype))
            pos += cnt
        return vjp(tuple(cts))

    return _rowwise(name, bfn, list(ins) + cot_ins, outs, groups=groups, tr=tr)


def _rms(x, g):
    return x * lax.rsqrt(jnp.mean(x * x, axis=-1, keepdims=True) + NORM_EPS) * g


def _silu(x):
    return x * jax.nn.sigmoid(x)


def _softplus(x):
    return jnp.maximum(x, 0.0) + jnp.log(1.0 + jnp.exp(-jnp.abs(x)))


def _swap_halves(x):
    lane = lax.broadcasted_iota(jnp.int32, x.shape, x.ndim - 1)
    return jnp.where(lane < 32, pltpu.roll(x, LANE - 32, x.ndim - 1), pltpu.roll(x, 32, x.ndim - 1))


@jax.custom_vjp
def _rope(x, c, s):
    return x * c + _swap_halves(x) * s


def _rope_fwd(x, c, s):
    return _rope(x, c, s), (c, s)


def _rope_bwd(res, dy):
    c, s = res
    return dy * c + _swap_halves(dy * s), jnp.zeros_like(c), jnp.zeros_like(s)


_rope.defvjp(_rope_fwd, _rope_bwd)


def _conv_post(c, g):
    s = _silu(c)
    n = s * lax.rsqrt(jnp.sum(s * s, axis=-1, keepdims=True) + NORM_EPS)
    return jnp.where(g < GDN_QK_HEADS, n * (GDN_DK ** -0.5), jnp.where(g < 2 * GDN_QK_HEADS, n, s))


def _conv_taps(xe, w):
    c = xe[8:] * w[3]
    for s in (1, 2, 3):
        c = c + pltpu.roll(xe, s, 0)[8:] * w[3 - s]
    return c


CONV_LANES = 512
CONV_HEADS = CONV_LANES // LANE


def _conv_post_block(c, g):
    return jnp.concatenate([_conv_post(c[:, i * LANE:(i + 1) * LANE], g * CONV_HEADS + i)
                            for i in range(CONV_HEADS)], axis=1)


def _conv_fwd(x, w, name, tr=640):
    lp, width = x.shape
    cl = CONV_LANES
    nr = lp // tr

    def body(x_ref, prev_ref, w_ref, o_ref):
        g = pl.program_id(0)
        r = pl.program_id(1)
        prev = jnp.where(r > 0, prev_ref[...], 0.0)
        xe = jnp.concatenate([prev, x_ref[...]], axis=0)
        o_ref[...] = _conv_post_block(_conv_taps(xe, [w_ref[t:t + 1, :] for t in range(4)]), g)

    return _pcall(
        body, name=name, grid=(width // cl, nr),
        in_specs=[pl.BlockSpec((tr, cl), lambda g, r: (r, g)),
                  pl.BlockSpec((8, cl), lambda g, r: (jnp.maximum(r * (tr // 8) - 1, 0), g)),
                  pl.BlockSpec((4, cl), lambda g, r: (0, g))],
        out_specs=pl.BlockSpec((tr, cl), lambda g, r: (r, g)),
        out_shape=jax.ShapeDtypeStruct((lp, width), F32),
        compiler_params=_params(2),
    )(x, x, w)


def _conv_bwd(x, w, dq, dk, dv, name, tr=640):
    lp, width = x.shape
    cl = CONV_LANES
    nr = lp // tr
    last8 = lp // 8 - 1
    nq = GDN_QK_W // cl

    def body(x_ref, prev_ref, next_ref, w_ref, q_ref, k_ref, v_ref, q_n, k_n, v_n, dx_ref, dw_ref):
        g = pl.program_id(0)
        r = pl.program_id(1)
        w = [w_ref[t:t + 1, :] for t in range(4)]
        not_last = r < nr - 1

        def pick(a, b, c):
            return jnp.where(g < nq, a[...], jnp.where(g < 2 * nq, b[...], c[...]))

        dy = pick(q_ref, k_ref, v_ref)
        dyn = jnp.where(not_last, pick(q_n, k_n, v_n), 0.0)
        prev = jnp.where(r > 0, prev_ref[...], 0.0)
        nxt = jnp.where(not_last, next_ref[...], 0.0)
        xe = jnp.concatenate([prev, x_ref[...], nxt], axis=0)
        ce = _conv_taps(xe, w)
        _, vjp = jax.vjp(lambda c: _conv_post_block(c, g), ce)
        (dce,) = vjp(jnp.concatenate([dy, dyn], axis=0))
        n = tr + 8
        dx = dce * w[3]
        for s in (1, 2, 3):
            dx = dx + pltpu.roll(dce, n - s, 0) * w[3 - s]
        dx_ref[...] = dx[:tr]
        dc = dce[:tr]
        row4 = lax.broadcasted_iota(jnp.int32, (4, cl), 0)
        dw = jnp.zeros((4, cl), F32)
        for s in (0, 1, 2, 3):
            xs = xe[8:8 + tr] if s == 0 else pltpu.roll(xe, s, 0)[8:8 + tr]
            dw = dw + jnp.where(row4 == 3 - s, jnp.sum(dc * xs, axis=0, keepdims=True), 0.0)

        @pl.when(r == 0)
        def _():
            dw_ref[...] = dw

        @pl.when(r > 0)
        def _():
            dw_ref[...] += dw

    def col_q(g):
        return jnp.minimum(g, nq - 1)

    def col_k(g):
        return jnp.clip(g - nq, 0, nq - 1)

    def col_v(g):
        return jnp.maximum(g - 2 * nq, 0)

    def blk(colf):
        return pl.BlockSpec((tr, cl), lambda g, r: (r, colf(g)))

    def nblk(colf):
        return pl.BlockSpec((8, cl), lambda g, r: (jnp.minimum((r + 1) * (tr // 8), last8), colf(g)))

    return _pcall(
        body, name=name, grid=(width // cl, nr),
        in_specs=[pl.BlockSpec((tr, cl), lambda g, r: (r, g)),
                  pl.BlockSpec((8, cl), lambda g, r: (jnp.maximum(r * (tr // 8) - 1, 0), g)),
                  pl.BlockSpec((8, cl), lambda g, r: (jnp.minimum((r + 1) * (tr // 8), last8), g)),
                  pl.BlockSpec((4, cl), lambda g, r: (0, g)),
                  blk(col_q), blk(col_k), blk(col_v), nblk(col_q), nblk(col_k), nblk(col_v)],
        out_specs=[pl.BlockSpec((tr, cl), lambda g, r: (r, g)),
                   pl.BlockSpec((4, cl), lambda g, r: (0, g))],
        out_shape=[jax.ShapeDtypeStruct((lp, width), F32), jax.ShapeDtypeStruct((4, width), F32)],
        compiler_params=_params(2),
    )(x, x, x, w, dq, dk, dv, dq, dk, dv)


def _bmm(a, b, dims, prec=None):
    (ca,), (cb,) = dims
    return lax.dot_general(a, b, (((ca + 1,), (cb + 1,)), ((0,), (0,))), precision=prec,
                           preferred_element_type=F32)


def _inv_impl(m):
    c = m.shape[-1]
    ii = lax.broadcasted_iota(jnp.int32, (c, c), 0)
    jj = lax.broadcasted_iota(jnp.int32, (c, c), 1)
    eye = (ii == jj).astype(F32)

    def same_block(shift):
        return (ii >> shift) == (jj >> shift)

    n1 = jnp.where(same_block(3), -m, 0.0)
    n2 = _bmm(n1, n1, _NN, _X3)
    n4 = _bmm(n2, n2, _NN, _X3)
    d = _bmm(_bmm(eye + n1, eye + n2, _NN, _X3), eye + n4, _NN, _X3)
    shift = 3
    while (1 << shift) < c:
        low = jnp.where(jnp.logical_and(same_block(shift + 1), jnp.logical_not(same_block(shift))), m, 0.0)
        d = d - _bmm(d, _bmm(low, d, _NN, _X3), _NN, _X3)
        shift += 1
    return d


@jax.custom_vjp
def _inv_unit_lower(m):
    return _inv_impl(m)


def _inv_f(m):
    t = _inv_impl(m)
    return t, t


def _inv_b(t, dt):
    c = t.shape[-1]
    ii = lax.broadcasted_iota(jnp.int32, (c, c), 0)
    jj = lax.broadcasted_iota(jnp.int32, (c, c), 1)
    gm = _bmm(t, _bmm(dt, t, _NT, _X3), _TN, _X3)
    return (jnp.where(ii > jj, -gm, 0.0),)


_inv_unit_lower.defvjp(_inv_f, _inv_b)


GDN_HEADS_PER_STEP = 16


def _gdn_group(q, k, v, beta_blk, gc_blk, states, h0):
    hp = GDN_HEADS_PER_STEP
    c = q.shape[0]
    lane = lax.broadcasted_iota(jnp.int32, (1, LANE), 1)
    row8 = lax.broadcasted_iota(jnp.int32, (max(8, hp), LANE), 0)
    lane8 = lax.broadcasted_iota(jnp.int32, (max(8, hp), LANE), 1)
    gcr_all = _hdot((lane8 == h0 + row8).astype(F32), gc_blk, _NT)
    betas, gccs = [], []
    for i in range(hp):
        onehot = (lane == h0 + i).astype(F32)
        betas.append(jnp.sum(beta_blk * onehot, axis=1, keepdims=True))
        gccs.append(jnp.sum(gc_blk * onehot, axis=1, keepdims=True))
    beta = jnp.stack(betas)
    gcc = jnp.stack(gccs)
    gcr = jnp.stack([gcr_all[i:i + 1] for i in range(hp)])
    qh = jnp.stack([q[:, (i // 2) * LANE:(i // 2 + 1) * LANE] for i in range(hp)])
    kh = jnp.stack([k[:, (i // 2) * LANE:(i // 2 + 1) * LANE] for i in range(hp)])
    vh = jnp.stack([v[:, i * LANE:(i + 1) * LANE] for i in range(hp)])
    state = jnp.stack(states)
    ii = lax.broadcasted_iota(jnp.int32, (c, c), 0)
    jj = lax.broadcasted_iota(jnp.int32, (c, c), 1)
    incl = ii >= jj
    dec = jnp.where(incl, jnp.exp(jnp.where(incl, gcc - gcr, 0.0)), 0.0)
    eg = jnp.exp(gcc)
    m = _bmm(kh, kh, _NT) * beta * jnp.where(ii > jj, dec, 0.0)
    t = _inv_unit_lower(m)
    u = _bmm(t, vh * beta, _NN, _X3)
    w = _bmm(t, kh * (beta * eg), _NN, _X3)
    attn = _bmm(qh, kh, _NT) * dec
    rows = lax.broadcasted_iota(jnp.int32, (c, 1), 0)
    gl = jnp.sum(jnp.where(rows == c - 1, gcc, 0.0), axis=1, keepdims=True)
    v_new = u - _bmm(w, state, _NN)
    o = _bmm(qh * eg, state, _NN) + _bmm(attn, v_new, _NN)
    new_state = state * jnp.exp(gl) + _bmm(kh * jnp.exp(gl - gcc), v_new, _TN)
    return jnp.concatenate([o[i] for i in range(hp)], axis=1), tuple(new_state[i] for i in range(hp))


def _gdn_specs(nc, rev):
    def cidx(n):
        return (nc - 1 - n) if rev else n
    hp = GDN_HEADS_PER_STEP
    nqk = GDN_QK_HEADS
    c = GDN_CHUNK
    nq = 2 * nqk // hp
    q_spec = pl.BlockSpec((c, hp // 2 * LANE), lambda n, g: (cidx(n), g))
    k_spec = pl.BlockSpec((c, hp // 2 * LANE), lambda n, g: (cidx(n), nq + g))
    v_spec = pl.BlockSpec((c, hp * LANE), lambda n, g: (cidx(n), nq + g))
    s_spec = pl.BlockSpec((c, LANE), lambda n, g: (cidx(n), 0))
    o_spec = pl.BlockSpec((c, hp * LANE), lambda n, g: (cidx(n), g))
    ck_spec = pl.BlockSpec((hp, 1, GDN_DK, LANE), lambda n, g: (g, cidx(n), 0, 0))
    return q_spec, k_spec, v_spec, s_spec, o_spec, ck_spec


def _gdn_fwd(qkv, beta, gc, name):
    lp = qkv.shape[0]
    nc = lp // GDN_CHUNK
    nh = GDN_V_HEADS
    hp = GDN_HEADS_PER_STEP
    q_spec, k_spec, v_spec, s_spec, o_spec, ck_spec = _gdn_specs(nc, False)

    def body(q_ref, k_ref, v_ref, b_ref, g_ref, o_ref, ck_ref, state):
        n = pl.program_id(0)
        g = pl.program_id(1)

        @pl.when(n == 0)
        def _():
            for i in range(hp):
                state[g * hp + i] = jnp.zeros((GDN_DK, LANE), F32)

        states = tuple(state[g * hp + i] for i in range(hp))
        for i in range(hp):
            ck_ref[i, 0] = states[i]
        o, new_states = _gdn_group(q_ref[...], k_ref[...], v_ref[...], b_ref[...], g_ref[...], states, g * hp)
        o_ref[...] = o
        for i in range(hp):
            state[g * hp + i] = new_states[i]

    return _pcall(
        body, name=name, grid=(nc, nh // hp),
        in_specs=[q_spec, k_spec, v_spec, s_spec, s_spec],
        out_specs=[o_spec, ck_spec],
        out_shape=[jax.ShapeDtypeStruct((lp, GDN_V_W), F32),
                   jax.ShapeDtypeStruct((nh, nc, GDN_DK, LANE), F32)],
        scratch_shapes=[pltpu.VMEM((nh, GDN_DK, LANE), F32)],
        compiler_params=_params(2),
    )(qkv, qkv, qkv, beta, gc)


def _gdn_bwd(qkv, beta, gc, ckpt, do, name):
    lp = qkv.shape[0]
    nc = lp // GDN_CHUNK
    nh = GDN_V_HEADS
    hp = GDN_HEADS_PER_STEP
    q_spec, k_spec, v_spec, s_spec, o_spec, ck_spec = _gdn_specs(nc, True)

    def body(q_ref, k_ref, v_ref, b_ref, g_ref, ck_ref, do_ref,
             dq_ref, dk_ref, dv_ref, db_ref, dg_ref, dstate):
        n = pl.program_id(0)
        g = pl.program_id(1)

        @pl.when(n == 0)
        def _():
            for i in range(hp):
                dstate[g * hp + i] = jnp.zeros((GDN_DK, LANE), F32)

        states = tuple(ck_ref[i, 0] for i in range(hp))
        _, vjp = jax.vjp(lambda q, k, v, b, gg, s: _gdn_group(q, k, v, b, gg, s, g * hp),
                         q_ref[...], k_ref[...], v_ref[...], b_ref[...], g_ref[...], states)
        dq, dk, dv, db, dg, ds = vjp((do_ref[...], tuple(dstate[g * hp + i] for i in range(hp))))
        dq_ref[...] = dq
        dk_ref[...] = dk
        dv_ref[...] = dv
        for i in range(hp):
            dstate[g * hp + i] = ds[i]

        @pl.when(g == 0)
        def _():
            db_ref[...] = db
            dg_ref[...] = dg

        @pl.when(g > 0)
        def _():
            db_ref[...] += db
            dg_ref[...] += dg

    qk_shape = jax.ShapeDtypeStruct((lp, GDN_QK_W), F32)
    big = jax.ShapeDtypeStruct((lp, GDN_V_W), F32)
    small = jax.ShapeDtypeStruct((lp, LANE), F32)
    dq_spec = pl.BlockSpec((GDN_CHUNK, hp // 2 * LANE), lambda n, g: (nc - 1 - n, g))
    return _pcall(
        body, name=name, grid=(nc, nh // hp),
        in_specs=[q_spec, k_spec, v_spec, s_spec, s_spec, ck_spec, o_spec],
        out_specs=[dq_spec, dq_spec, o_spec, s_spec, s_spec],
        out_shape=[qk_shape, qk_shape, big, small, small],
        scratch_shapes=[pltpu.VMEM((nh, GDN_DK, LANE), F32)],
        compiler_params=_params(2),
    )(qkv, qkv, qkv, beta, gc, ckpt, do)


LOG2E = 1.4426950408889634
LN2 = 0.6931471805599453
Q_PRESCALE = MLA_QK ** -0.5 * LOG2E


def _att_mask(i, j, tb, transposed):
    r = lax.broadcasted_iota(jnp.int32, (tb, tb), 0)
    c = lax.broadcasted_iota(jnp.int32, (tb, tb), 1)
    qpos, kpos = (i * tb + c, j * tb + r) if transposed else (i * tb + r, j * tb + c)
    return jnp.logical_and(kpos <= qpos, kpos >= FRONT)


def _causal_pairs(nb, by_key):
    if by_key:
        pairs = [(i, j) for j in range(nb) for i in range(j, nb)]
    else:
        pairs = [(i, j) for i in range(nb) for j in range(i + 1)]
    return jnp.array([p[0] for p in pairs], jnp.int32), jnp.array([p[1] for p in pairs], jnp.int32)


def _masked_and_plain(i, j, step):
    edge = jnp.logical_or(j == i, j == 0)

    @pl.when(jnp.logical_and(edge, j <= i))
    def _():
        step(True)

    @pl.when(jnp.logical_and(jnp.logical_not(edge), j < i))
    def _():
        step(False)


def _cat(a_ref, b_ref):
    return jnp.concatenate([a_ref[...], b_ref[...]], axis=1)


def _flash_fwd(qn, qr, kn, kr, v, name, tb=ROW_ALIGN):
    lp = qn.shape[0]
    nb = lp // tb
    nh = MLA_HEADS
    qi, kj = _causal_pairs(nb, by_key=False)

    def body(qi_ref, kj_ref, qn_ref, qr_ref, kn_ref, kr_ref, v_ref, o_ref, lse_ref, m_s, l_s, acc):
        t = pl.program_id(1)
        i, j = qi_ref[t], kj_ref[t]

        @pl.when(j == 0)
        def _():
            m_s[...] = jnp.full_like(m_s, NEG)
            l_s[...] = jnp.zeros_like(l_s)
            acc[...] = jnp.zeros_like(acc)

        def step(masked):
            s = _dot(_cat(qn_ref, qr_ref), _cat(kn_ref, kr_ref), _NT)
            if masked:
                s = jnp.where(_att_mask(i, j, tb, False), s, NEG)
            m_new = jnp.maximum(m_s[...], jnp.max(s, axis=1, keepdims=True))
            alpha = jnp.exp2(m_s[...] - m_new)
            p = jnp.exp2(s - m_new)
            l_s[...] = alpha * l_s[...] + jnp.sum(p, axis=1, keepdims=True)
            acc[...] = alpha * acc[...] + _dot(p.astype(BF16), v_ref[...], _NN)
            m_s[...] = m_new

        _masked_and_plain(i, j, step)

        @pl.when(j == i)
        def _():
            o_ref[...] = acc[...] / l_s[...]
            lse_ref[...] = jnp.broadcast_to(m_s[...] + jnp.log(l_s[...]) * LOG2E, (tb, LANE))

    qspec = pl.BlockSpec((tb, LANE), lambda h, t, qi_, kj_: (qi_[t], h))
    kspec = pl.BlockSpec((tb, LANE), lambda h, t, qi_, kj_: (kj_[t], h))
    krspec = pl.BlockSpec((tb, LANE), lambda h, t, qi_, kj_: (kj_[t], 0))
    shp = jax.ShapeDtypeStruct((lp, nh * LANE), F32)
    return _pcall(
        body, name=name, out_shape=[shp, shp],
        grid_spec=pltpu.PrefetchScalarGridSpec(
            num_scalar_prefetch=2, grid=(nh, qi.shape[0]),
            in_specs=[qspec, qspec, kspec, krspec, kspec], out_specs=[qspec, qspec],
            scratch_shapes=[pltpu.VMEM((tb, 1), F32), pltpu.VMEM((tb, 1), F32), pltpu.VMEM((tb, LANE), F32)]),
        compiler_params=_params(2),
    )(qi, kj, qn, qr, kn, kr, v)


def _flash_bwd(qn, qr, kn, kr, v, o, do, lse, name, tb=ROW_ALIGN):
    lp = qn.shape[0]
    nb = lp // tb
    nh = MLA_HEADS
    qi, kj = _causal_pairs(nb, by_key=True)
    knt, krt = kn.T, kr.T

    def body(qi_ref, kj_ref, qn_ref, qr_ref, kn_ref, kr_ref, knt_ref, krt_ref, v_ref, o_ref, do_ref, lse_ref,
             dqnt_ref, dqrt_ref, dkn_ref, dkr_ref, dv_ref, dk_acc, dv_acc):
        t = pl.program_id(1)
        i, j = qi_ref[t], kj_ref[t]

        @pl.when(t == 0)
        def _():
            dqnt_ref[...] = jnp.zeros_like(dqnt_ref)
            dqrt_ref[...] = jnp.zeros_like(dqrt_ref)

        @pl.when(i == j)
        def _():
            dk_acc[...] = jnp.zeros_like(dk_acc)
            dv_acc[...] = jnp.zeros_like(dv_acc)

        def step(masked):
            q = _cat(qn_ref, qr_ref)
            st = _dot(_cat(kn_ref, kr_ref), q, _NT)
            if masked:
                st = jnp.where(_att_mask(i, j, tb, True), st, NEG)
            do_blk = do_ref[...]
            lane = lax.broadcasted_iota(jnp.int32, (8, LANE), 1)
            lse_row = _hdot((lane == 0).astype(F32), lse_ref[...], _NT)[0:1]
            delta_row = _hdot(jnp.ones((8, LANE), F32), do_blk * o_ref[...], _NT)[0:1]
            pt = jnp.exp2(st - lse_row)
            do_b = do_blk.astype(BF16)
            dv_acc[...] += _dot(pt.astype(BF16), do_b, _NN)
            dpt = _dot(v_ref[...], do_b, _NT)
            dst = (pt * (dpt - delta_row)).astype(BF16)
            dk_acc[...] += _dot(dst, q, _NN)
            dqnt_ref[i] += _dot(knt_ref[...], dst, _NN) * LN2
            dqrt_ref[i] += _dot(krt_ref[...], dst, _NN) * LN2

        _masked_and_plain(i, j, step)

        @pl.when(i == nb - 1)
        def _():
            dkn_ref[...] = dk_acc[:, :LANE] * LN2
            dkr_ref[...] = dk_acc[:, LANE:] * LN2
            dv_ref[...] = dv_acc[...]

    qspec = pl.BlockSpec((tb, LANE), lambda h, t, qi_, kj_: (qi_[t], h))
    kspec = pl.BlockSpec((tb, LANE), lambda h, t, qi_, kj_: (kj_[t], h))
    krspec = pl.BlockSpec((tb, LANE), lambda h, t, qi_, kj_: (kj_[t], 0))
    ktspec = pl.BlockSpec((LANE, tb), lambda h, t, qi_, kj_: (h, kj_[t]))
    krtspec = pl.BlockSpec((LANE, tb), lambda h, t, qi_, kj_: (0, kj_[t]))
    dqtspec = pl.BlockSpec((nb, LANE, tb), lambda h, t, qi_, kj_: (h, 0, 0))
    shp = jax.ShapeDtypeStruct((lp, nh * LANE), F32)
    dqt_shape = jax.ShapeDtypeStruct((nh * nb, LANE, tb), F32)
    dqnt, dqrt, dkn, dkr, dv = _pcall(
        body, name=name, out_shape=[dqt_shape, dqt_shape, shp, shp, shp],
        grid_spec=pltpu.PrefetchScalarGridSpec(
            num_scalar_prefetch=2, grid=(nh, qi.shape[0]),
            in_specs=[qspec, qspec, kspec, krspec, ktspec, krtspec, kspec, qspec, qspec, qspec],
            out_specs=[dqtspec, dqtspec, kspec, kspec, kspec],
            scratch_shapes=[pltpu.VMEM((tb, 2 * LANE), F32), pltpu.VMEM((tb, LANE), F32)]),
        compiler_params=_params(2),
    )(qi, kj, qn, qr, kn, kr, knt, krt, v, o, do, lse)

    def rows_major(a):
        return a.reshape(nh, nb, LANE, tb).transpose(1, 3, 0, 2).reshape(lp, nh * LANE)

    return rows_major(dqnt), rows_major(dqrt), dkn, dkr, dv


ELEMENTWISE_BLOCK_BYTES = 1 << 20


def _row_tile(rows, width, copies=1):
    for t in (1024, 512, 256, 128, 64, 32, 16, 8):
        if rows % t == 0 and t * width * 4 * copies <= ELEMENTWISE_BLOCK_BYTES:
            return t
    return rows


def _adamw(w, g, m, v, name):
    rows, width = w.shape
    tr = _row_tile(rows, width)

    def body(w_ref, g_ref, m_ref, v_ref, d_ref, nm_ref, nv_ref):
        gg = g_ref[...]
        nm = ADAM_B1 * m_ref[...] + (1.0 - ADAM_B1) * gg
        nv = ADAM_B2 * v_ref[...] + (1.0 - ADAM_B2) * jnp.square(gg)
        m_hat = nm / (1.0 - ADAM_B1 ** ADAM_STEP)
        v_hat = nv / (1.0 - ADAM_B2 ** ADAM_STEP)
        d_ref[...] = -ADAM_LR * (m_hat / (jnp.sqrt(v_hat) + ADAM_EPS) + ADAM_WD * w_ref[...])
        nm_ref[...] = nm
        nv_ref[...] = nv

    spec = pl.BlockSpec((tr, width), lambda r: (r, 0))
    shp = jax.ShapeDtypeStruct((rows, width), F32)
    return _pcall(body, name=name, grid=(rows // tr,), in_specs=[spec] * 4, out_specs=[spec] * 3,
                  out_shape=[shp] * 3, compiler_params=_params(1))(w, g, m, v)


def _add_pair(a, b, name):
    s, rows, width = b.shape
    tr = _row_tile(rows, width)
    nt = rows // tr

    def body(c_ref, a_ref, b_ref, o_ref):
        o_ref[...] = a_ref[...] + b_ref[...]

    spec = pl.BlockSpec((1, tr, width), lambda i, r, c_ref: (i, r, 0))
    return _pcall(
        body, name=name, out_shape=jax.ShapeDtypeStruct(b.shape, F32),
        grid_spec=pltpu.PrefetchScalarGridSpec(
            num_scalar_prefetch=1, grid=(s, nt),
            in_specs=[pl.BlockSpec((1, tr, width), lambda i, r, c_ref: (i, c_ref[0] * nt + r, 0)), spec],
            out_specs=spec),
        compiler_params=_params(2),
    )(_core_index(), a, b)


def _sum_slots(a, name):
    s, rows, width = a.shape
    tr = _row_tile(rows, width, copies=s)

    def body(a_ref, o_ref):
        tot = a_ref[0]
        for k in range(1, s):
            tot = tot + a_ref[k]
        o_ref[...] = tot

    return _pcall(body, name=name, grid=(rows // tr,),
                  in_specs=[pl.BlockSpec((s, tr, width), lambda r: (0, r, 0))],
                  out_specs=pl.BlockSpec((tr, width), lambda r: (r, 0)),
                  out_shape=jax.ShapeDtypeStruct((rows, width), F32), compiler_params=_params(1))(a)


_ANY = pl.BlockSpec(memory_space=pl.ANY)


def _my_place():
    return lax.axis_index("x"), lax.axis_index("y"), lax.axis_index("c")


def _core_index():
    return lax.axis_index("c").astype(jnp.int32).reshape(1)


def _other_chips(x, y):
    return [(1 - x, y), (x, 1 - y), (1 - x, 1 - y)]


def _gather_shards(flat, name):
    rows, width = flat.shape

    def body(x_ref, out_ref, send_sems, recv_sems, local_sem):
        x, y, c = _my_place()
        mine = pltpu.make_async_copy(x_ref, out_ref.at[2 * x + y], local_sem)
        mine.start()
        sends = []
        for k, (px, py) in enumerate(_other_chips(x, y)):
            cp = pltpu.make_async_remote_copy(
                src_ref=x_ref, dst_ref=out_ref.at[2 * x + y], send_sem=send_sems.at[k], recv_sem=recv_sems.at[k],
                device_id=(px, py, c), device_id_type=MESH)
            cp.start()
            sends.append(cp)
        for k, (px, py) in enumerate(_other_chips(x, y)):
            pltpu.make_async_remote_copy(
                src_ref=x_ref, dst_ref=out_ref.at[2 * px + py], send_sem=send_sems.at[k], recv_sem=recv_sems.at[k],
                device_id=(px, py, c), device_id_type=MESH).wait_recv()
        for cp in sends:
            cp.wait_send()
        mine.wait()

    return _pcall(
        body, name=name, in_specs=[_ANY], out_specs=_ANY,
        out_shape=jax.ShapeDtypeStruct((4, rows, width), flat.dtype),
        scratch_shapes=[pltpu.SemaphoreType.DMA((3,)), pltpu.SemaphoreType.DMA((3,)), pltpu.SemaphoreType.DMA],
    )(flat)


def _sibling_split(g, name):
    s, rows, width = g.shape
    half = rows // 2
    tr = _row_tile(half, width)
    nt = half // tr

    def body(c_ref, g_blk, got_ref, send_sem, recv_sem):
        k = pl.program_id(0)
        t = pl.program_id(1)
        x, y, c = _my_place()
        cp = pltpu.make_async_remote_copy(
            src_ref=g_blk.at[0], dst_ref=got_ref.at[k, pl.ds(pl.multiple_of(t * tr, 8), tr), :],
            send_sem=send_sem, recv_sem=recv_sem, device_id=(x, y, 1 - c), device_id_type=MESH)
        cp.start()
        cp.wait_send()

        @pl.when(jnp.logical_and(k == s - 1, t == nt - 1))
        def _():
            pltpu.make_async_remote_copy(
                src_ref=got_ref, dst_ref=got_ref, send_sem=send_sem, recv_sem=recv_sem,
                device_id=(x, y, 1 - c), device_id_type=MESH).wait_recv()

    return _pcall(
        body, name=name, out_shape=jax.ShapeDtypeStruct((s, half, width), g.dtype),
        grid_spec=pltpu.PrefetchScalarGridSpec(
            num_scalar_prefetch=1, grid=(s, nt),
            in_specs=[pl.BlockSpec((1, tr, width), lambda k, t, c_ref: (k, (1 - c_ref[0]) * nt + t, 0))],
            out_specs=_ANY,
            scratch_shapes=[pltpu.SemaphoreType.DMA, pltpu.SemaphoreType.DMA]),
        compiler_params=_params(2),
    )(_core_index(), g)


def _chip_scatter(p, name):
    s, rows, width = p.shape

    def body(p_ref, out_ref, send_sems, recv_sems, local_sem):
        x, y, c = _my_place()
        me = 2 * x + y
        mine = pltpu.make_async_copy(p_ref.at[me], out_ref.at[me], local_sem)
        mine.start()
        sends = []
        for k, (px, py) in enumerate(_other_chips(x, y)):
            cp = pltpu.make_async_remote_copy(
                src_ref=p_ref.at[2 * px + py], dst_ref=out_ref.at[me], send_sem=send_sems.at[k],
                recv_sem=recv_sems.at[k], device_id=(px, py, c), device_id_type=MESH)
            cp.start()
            sends.append(cp)
        for k, (px, py) in enumerate(_other_chips(x, y)):
            pltpu.make_async_remote_copy(
                src_ref=p_ref.at[me], dst_ref=out_ref.at[2 * px + py], send_sem=send_sems.at[k],
                recv_sem=recv_sems.at[k], device_id=(px, py, c), device_id_type=MESH).wait_recv()
        for cp in sends:
            cp.wait_send()
        mine.wait()

    return _pcall(
        body, name=name, in_specs=[_ANY], out_specs=_ANY,
        out_shape=jax.ShapeDtypeStruct(p.shape, p.dtype),
        scratch_shapes=[pltpu.SemaphoreType.DMA((3,)), pltpu.SemaphoreType.DMA((3,)), pltpu.SemaphoreType.DMA],
    )(p)


def _sibling_join(qh, name):
    half, width = qh.shape
    tr = _row_tile(half, width)
    nt = half // tr

    def body(q_blk, out_ref, send_sem, recv_sem, local_sem):
        t = pl.program_id(0)
        x, y, c = _my_place()
        dst = out_ref.at[pl.ds(pl.multiple_of(c * half + t * tr, 8), tr), :]
        cp = pltpu.make_async_remote_copy(
            src_ref=q_blk, dst_ref=dst, send_sem=send_sem, recv_sem=recv_sem,
            device_id=(x, y, 1 - c), device_id_type=MESH)
        cp.start()
        mine = pltpu.make_async_copy(q_blk, dst, local_sem)
        mine.start()
        cp.wait_send()
        mine.wait()

        @pl.when(t == nt - 1)
        def _():
            theirs = out_ref.at[pl.ds(pl.multiple_of((1 - c) * half, 8), half), :]
            pltpu.make_async_remote_copy(
                src_ref=theirs, dst_ref=theirs, send_sem=send_sem, recv_sem=recv_sem,
                device_id=(x, y, 1 - c), device_id_type=MESH).wait_recv()

    return _pcall(
        body, name=name, grid=(nt,),
        in_specs=[pl.BlockSpec((tr, width), lambda t: (t, 0))], out_specs=_ANY,
        out_shape=jax.ShapeDtypeStruct((2 * half, width), qh.dtype),
        scratch_shapes=[pltpu.SemaphoreType.DMA, pltpu.SemaphoreType.DMA, pltpu.SemaphoreType.DMA],
        compiler_params=_params(1),
    )(qh)


def _all_sum_small(part, name):
    rows, width = part.shape

    def body(p_ref, out_ref, land, send_sems, recv_sems):
        x, y, c = _my_place()
        me = 4 * x + 2 * y + c
        land[me] = p_ref[...]
        sends = []
        for k in range(1, 8):
            peer = (x ^ (k >> 2), y ^ ((k >> 1) & 1), c ^ (k & 1))
            cp = pltpu.make_async_remote_copy(
                src_ref=p_ref, dst_ref=land.at[me], send_sem=send_sems.at[k - 1], recv_sem=recv_sems.at[k - 1],
                device_id=peer, device_id_type=MESH)
            cp.start()
            sends.append(cp)
        for k in range(1, 8):
            px, py, pc = x ^ (k >> 2), y ^ ((k >> 1) & 1), c ^ (k & 1)
            pltpu.make_async_remote_copy(
                src_ref=p_ref, dst_ref=land.at[4 * px + 2 * py + pc], send_sem=send_sems.at[k - 1],
                recv_sem=recv_sems.at[k - 1], device_id=(px, py, pc), device_id_type=MESH).wait_recv()
        for cp in sends:
            cp.wait_send()
        tot = land[0]
        for k in range(1, 8):
            tot = tot + land[k]
        out_ref[...] = tot

    vmem = pl.BlockSpec(memory_space=pltpu.VMEM)
    return _pcall(
        body, name=name, in_specs=[vmem], out_specs=vmem,
        out_shape=jax.ShapeDtypeStruct((rows, width), F32),
        scratch_shapes=[pltpu.VMEM((8, rows, width), F32), pltpu.SemaphoreType.DMA((7,)),
                        pltpu.SemaphoreType.DMA((7,))],
    )(part)


def _big_layout(shards):
    return [(a.shape[0], a.shape[1], ax) for a, ax in shards]


FLAT_ROW_MULTIPLE = 2048


def _pack_shards(arrs):
    flat = jnp.concatenate([a.reshape(-1) for a in arrs])
    return jnp.pad(flat, (0, -flat.shape[0] % (FLAT_ROW_MULTIPLE * LANE))).reshape(-1, LANE)


def _unpack_shards(flat, layout):
    flat = flat.reshape(-1)
    out, off = [], 0
    for r, c, _ in layout:
        out.append(flat[off:off + r * c].reshape(r, c))
        off += r * c
    return out


def _unpack_full(gathered, layout):
    g = gathered.reshape(4, -1)
    out, off = [], 0
    for r, c, ax in layout:
        seg = g[:, off:off + r * c].reshape(4, r, c)
        out.append(seg.transpose(1, 0, 2).reshape(r, 4 * c) if ax == 1 else seg.reshape(4 * r, c))
        off += r * c
    return out


def _pack_full(fulls, layout):
    parts = []
    for a, (r, c, ax) in zip(fulls, layout):
        if ax == 1:
            parts.append(a.reshape(r, 4, c).transpose(1, 0, 2).reshape(4, r * c))
        else:
            parts.append(a.reshape(4, r * c))
    flat = jnp.concatenate(parts, axis=1)
    return jnp.pad(flat, ((0, 0), (0, -flat.shape[1] % (FLAT_ROW_MULTIPLE * LANE)))).reshape(4, -1, LANE)


def _pad_lanes(a, width=LANE):
    return jnp.pad(a, [(0, 0)] * (a.ndim - 1) + [(0, width - a.shape[-1])])


def _pack_small(arrs):
    rows = [_pad_lanes(a.reshape(1, -1), -(-a.size // LANE) * LANE).reshape(-1, LANE) for a in arrs]
    flat = jnp.concatenate(rows, axis=0)
    return jnp.pad(flat, ((0, -flat.shape[0] % 8), (0, 0)))


def _unpack_small(flat, shapes):
    out, off = [], 0
    for shp in shapes:
        n = math.prod(shp)
        nr = -(-n // LANE)
        out.append(flat[off:off + nr].reshape(-1)[:n].reshape(shp))
        off += nr
    return out


def kernel(x, meta_tokens, pre_norm, post_norm, gdn_w_in, gdn_conv_w, gdn_a_log, gdn_dt_bias, gdn_out_norm, gdn_w_out, kv_norm, kv_w_down, kv_latent_norm, kv_w_up, mla_w_in, mla_q_latent_norm, mla_w_q_up, mla_w_out, loss_target, m_meta_tokens, m_pre_norm, m_post_norm, m_gdn_w_in, m_gdn_conv_w, m_gdn_a_log, m_gdn_dt_bias, m_gdn_out_norm, m_gdn_w_out, m_kv_norm, m_kv_w_down, m_kv_latent_norm, m_kv_w_up, m_mla_w_in, m_mla_q_latent_norm, m_mla_w_q_up, m_mla_w_out, v_meta_tokens, v_pre_norm, v_post_norm, v_gdn_w_in, v_gdn_conv_w, v_gdn_a_log, v_gdn_dt_bias, v_gdn_out_norm, v_gdn_w_out, v_kv_norm, v_kv_w_down, v_kv_latent_norm, v_kv_w_up, v_mla_w_in, v_mla_q_latent_norm, v_mla_w_q_up, v_mla_w_out):
    seq = x.shape[1]
    d = D_MODEL
    lp = -(-(ROW0 + seq) // ROW_ALIGN) * ROW_ALIGN
    tail = lp - ROW0 - seq

    big_names = ["meta_tokens", "gdn_conv_w", "gdn_w_out", "kv_w_down", "kv_w_up", "mla_w_in", "mla_w_q_up",
                 "mla_w_out"]
    big_axis = [1, 1, 0, 0, 1, 1, 1, 0]
    big_w = [meta_tokens, gdn_conv_w[0], gdn_w_out[0], kv_w_down, kv_w_up, mla_w_in[0], mla_w_q_up[0], mla_w_out[0]]
    big_m = [m_meta_tokens, m_gdn_conv_w[0], m_gdn_w_out[0], m_kv_w_down, m_kv_w_up, m_mla_w_in[0], m_mla_w_q_up[0],
             m_mla_w_out[0]]
    big_v = [v_meta_tokens, v_gdn_conv_w[0], v_gdn_w_out[0], v_kv_w_down, v_kv_w_up, v_mla_w_in[0], v_mla_w_q_up[0],
             v_mla_w_out[0]]
    layout = _big_layout(list(zip(big_w, big_axis)))
    w_flat = _pack_shards(big_w)
    (meta_f, conv_w, w_out0, kv_down, kv_up, w_in1, w_qup, w_out1) = _unpack_full(
        _gather_shards(w_flat, "gather_weights"), layout)
    w_in0_shards = _gather_shards(gdn_w_in[0], "gather_gdn_w_in")
    w_in0 = jnp.concatenate([w_in0_shards[s] for s in range(4)], axis=1)
    win_cols = gdn_w_in.shape[2]

    nv = GDN_V_HEADS
    w_qkv = w_in0[:, :GDN_CONV_W]
    w_z0 = w_in0[:, GDN_CONV_W:GDN_CONV_W + GDN_V_W]
    w_b = _pad_lanes(w_in0[:, GDN_CONV_W + GDN_V_W:GDN_CONV_W + GDN_V_W + nv])
    w_a = _pad_lanes(w_in0[:, GDN_CONV_W + GDN_V_W + nv:])
    w_ckv = kv_down[:, :MLA_KV_RANK]
    w_kr = _pad_lanes(kv_down[:, MLA_KV_RANK:])
    kvu = kv_up.reshape(MLA_KV_RANK, MLA_HEADS, 2 * LANE)
    w_kn = kvu[:, :, :LANE].reshape(MLA_KV_RANK, MLA_HEADS * LANE)
    w_v = kvu[:, :, LANE:].reshape(MLA_KV_RANK, MLA_HEADS * LANE)
    w_cq = w_in1[:, :MLA_Q_RANK]
    w_z1 = w_in1[:, MLA_Q_RANK:]
    qu = w_qup.reshape(MLA_Q_RANK, MLA_HEADS, MLA_QK)
    w_qn = qu[:, :, :MLA_NOPE].reshape(MLA_Q_RANK, MLA_HEADS * LANE) * Q_PRESCALE
    w_qr = _pad_lanes(qu[:, :, MLA_NOPE:]).reshape(MLA_Q_RANK, MLA_HEADS * LANE) * Q_PRESCALE
    (w_qkv, w_z0, w_b, w_a, w_out0, w_ckv, w_kr, w_kn, w_v, w_cq, w_z1, w_qn, w_qr, w_out1) = [
        w.astype(BF16) for w in (w_qkv, w_z0, w_b, w_a, w_out0, w_ckv, w_kr, w_kn, w_v, w_cq, w_z1, w_qn, w_qr,
                                 w_out1)]

    pre0, pre1 = pre_norm[0:1], pre_norm[1:2]
    post0, post1 = post_norm[0:1], post_norm[1:2]
    a_log = _pad_lanes(gdn_a_log)
    dt_bias = _pad_lanes(gdn_dt_bias)
    kvn = kv_norm.reshape(1, d)
    kvl = kv_latent_norm.reshape(1, MLA_KV_RANK)
    qln = mla_q_latent_norm

    h0 = jnp.concatenate([jnp.zeros((FRONT, d), F32), meta_f, x[0], jnp.zeros((tail, d), F32)], axis=0)
    tgt = jnp.pad(loss_target[0], ((ROW0, tail), (0, 0)))
    pos = jnp.maximum(jnp.arange(lp, dtype=jnp.int32) - FRONT, 0).astype(F32)
    inv = ROPE_THETA ** (-jnp.arange(0, MLA_ROPE, 2, dtype=F32) / MLA_ROPE)
    ang = pos[:, None] * inv[None, :]
    zeros64 = jnp.zeros((lp, LANE - MLA_ROPE), F32)
    cos_t = jnp.concatenate([jnp.cos(ang), jnp.cos(ang), zeros64], axis=1)
    sin_t = jnp.concatenate([-jnp.sin(ang), jnp.sin(ang), zeros64], axis=1)

    def valid_rows(ridx):
        return jnp.logical_and(ridx >= FRONT, ridx < ROW0 + seq)

    def f_pre0(ridx, g, h, gain):
        return _rms(h, gain), h

    (hn0,) = _rowwise("pre0", lambda *a: f_pre0(*a)[:1], [_In(h0), _In(pre0, "const")],
                      [_Out("row", (lp, d), BF16)])
    qkv_raw = _mm(hn0, w_qkv, "nn", "gdn_in_qkv")
    z0 = _mm(hn0, w_z0, "nn", "gdn_in_z")
    b_raw = _mm(hn0, w_b, "nn", "gdn_in_b")
    a_raw = _mm(hn0, w_a, "nn", "gdn_in_a")

    def f_ba(ridx, g, b, a, alog, dtb):
        tr = b.shape[0]
        ok = valid_rows(ridx).astype(F32)
        beta = jax.nn.sigmoid(b) * ok
        gate = -jnp.exp(alog) * _softplus(a + dtb) * ok
        ii = lax.broadcasted_iota(jnp.int32, (tr, tr), 0)
        jj = lax.broadcasted_iota(jnp.int32, (tr, tr), 1)
        tri = jnp.logical_and((ii >> 6) == (jj >> 6), ii >= jj).astype(F32)
        return beta, _hdot(tri, gate)

    ba_ins = [_In(b_raw), _In(a_raw), _In(a_log, "const"), _In(dt_bias, "const")]
    beta, gc = _rowwise("gdn_gates", f_ba, ba_ins, [_Out("row", (lp, LANE)), _Out("row", (lp, LANE))])
    qkv = _conv_fwd(qkv_raw, conv_w, "gdn_conv")
    o0, ckpt = _gdn_fwd(qkv, beta, gc, "gdn_scan")

    def per_head(fn, *arrs):
        n = arrs[0].shape[1] // LANE
        return jnp.concatenate([fn(*[a[:, i * LANE:(i + 1) * LANE] for a in arrs]) for i in range(n)], axis=1)

    def f_gate0(ridx, g, o, z, gain):
        return (per_head(lambda oh, zh: _rms(oh, gain) * _silu(zh), o, z),)

    gate0_ins = [_In(o0), _In(z0), _In(gdn_out_norm, "const")]
    (gated0,) = _rowwise("gdn_gate", f_gate0, gate0_ins, [_Out("row", (lp, GDN_V_W), BF16)])
    y0 = _mm(gated0, w_out0, "nn", "gdn_out")

    def f_mid(ridx, g, h, y, g_post, g_pre, g_kv):
        h1 = h + _rms(y, g_post)
        return h1, _rms(h1, g_pre), _rms(h1, g_kv)

    mid_ins = [_In(h0), _In(y0), _In(post0, "const"), _In(pre1, "const"), _In(kvn, "const")]
    h1, hn1, hkv = _rowwise("mid", f_mid, mid_ins,
                            [_Out("row", (lp, d)), _Out("row", (lp, d), BF16), _Out("row", (lp, d), BF16)])

    ckv_raw = _mm(hkv, w_ckv, "nn", "kv_down_c")
    kr_raw = _mm(hkv, w_kr, "nn", "kv_down_r")

    def f_ckv(ridx, g, c, r, cs, sn, gain):
        return _rms(c, gain), _rope(r, cs, sn)

    ckv_ins = [_In(ckv_raw), _In(kr_raw), _In(cos_t), _In(sin_t), _In(kvl, "const")]
    ckv, kr = _rowwise("kv_latent", f_ckv, ckv_ins, [_Out("row", (lp, LANE)), _Out("row", (lp, LANE), BF16)],
                       tr=640)
    kn = _mm(ckv, w_kn, "nn", "kv_up_k", BF16)
    vv = _mm(ckv, w_v, "nn", "kv_up_v", BF16)
    cq_raw = _mm(hn1, w_cq, "nn", "mla_in_q")
    z1 = _mm(hn1, w_z1, "nn", "mla_in_z")

    def f_cq(ridx, g, c, gain):
        return (_rms(c, gain),)

    cq_ins = [_In(cq_raw), _In(qln, "const")]
    (cq,) = _rowwise("q_latent", f_cq, cq_ins, [_Out("row", (lp, MLA_Q_RANK))], tr=640)
    qn = _mm(cq, w_qn, "nn", "q_up_n", BF16)
    qr_raw = _mm(cq, w_qr, "nn", "q_up_r")

    def f_qrope(ridx, g, r, cs, sn):
        return (per_head(lambda rh: _rope(rh, cs, sn), r),)

    qr_ins = [_In(qr_raw), _In(cos_t), _In(sin_t)]
    (qr,) = _rowwise("q_rope", f_qrope, qr_ins, [_Out("row", (lp, MLA_HEADS * LANE), BF16)])
    o1, lse = _flash_fwd(qn, qr, kn, kr, vv, "attention")

    def f_gate1(ridx, g, o, z):
        return (o * _silu(z),)

    gate1_ins = [_In(o1), _In(z1)]
    (og,) = _rowwise("mla_gate", f_gate1, gate1_ins, [_Out("row", (lp, MLA_HEADS * LANE), BF16)])
    y1 = _mm(og, w_out1, "nn", "mla_out")

    def f_final(ridx, g, h, y, t, gain):
        ok = jnp.logical_and(ridx >= ROW0, ridx < ROW0 + seq).astype(F32)

        def rows_loss(h_, y_, gain_):
            err = (h_ + _rms(y_, gain_) - t) * ok
            return 0.5 * jnp.sum(jnp.sum(err * err, axis=1, keepdims=True), axis=0, keepdims=True) / d

        val, vjp = jax.vjp(rows_loss, h, y, gain)
        dh, dy, dgain = vjp(jnp.ones((1, 1), F32))
        return dh, dy, dgain, jnp.broadcast_to(val, (1, LANE))

    dh2, dy1, dpost1, loss_part = _rowwise(
        "loss_head", f_final, [_In(h1), _In(y1), _In(tgt), _In(post1, "const")],
        [_Out("row", (lp, d)), _Out("row", (lp, d)), _Out("acc", (1, d)), _Out("acc", (1, LANE))])

    dog = _mm(dy1, w_out1, "nt", "mla_out_dx")
    dw_out1 = _mm(og, dy1, "tn", "mla_out_dw")
    do1, dz1 = _rowwise_vjp("mla_gate_bwd", f_gate1, gate1_ins, [[dog]], [0, 1])
    dqn, dqr, dkn, dkr, dvv = _flash_bwd(qn, qr, kn, kr, vv, o1, do1, lse, "attention_bwd")
    (dqr_raw,) = _rowwise_vjp("q_rope_bwd", f_qrope, qr_ins, [[dqr]], [0])
    dcq_a = _mm(dqn, w_qn, "nt", "q_up_n_dx")
    dcq_b = _mm(dqr_raw, w_qr, "nt", "q_up_r_dx")
    dw_qn = _mm(cq, dqn, "tn", "q_up_n_dw") * Q_PRESCALE
    dw_qr = _mm(cq, dqr_raw, "tn", "q_up_r_dw") * Q_PRESCALE
    dcq_raw, dqln = _rowwise_vjp("q_latent_bwd", f_cq, cq_ins, [[dcq_a, dcq_b]], [0, 1], tr=640)
    dhn1_a = _mm(dcq_raw, w_cq, "nt", "mla_in_q_dx")
    dhn1_b = _mm(dz1, w_z1, "nt", "mla_in_z_dx")
    dw_cq = _mm(hn1, dcq_raw, "tn", "mla_in_q_dw")
    dw_z1 = _mm(hn1, dz1, "tn", "mla_in_z_dw")
    dckv_a = _mm(dkn, w_kn, "nt", "kv_up_k_dx")
    dckv_b = _mm(dvv, w_v, "nt", "kv_up_v_dx")
    dw_kn = _mm(ckv, dkn, "tn", "kv_up_k_dw")
    dw_v = _mm(ckv, dvv, "tn", "kv_up_v_dw")
    dckv_raw, dkr_raw, dkvl = _rowwise_vjp("kv_latent_bwd", f_ckv, ckv_ins, [[dckv_a, dckv_b], [dkr]], [0, 1, 4],
                                           tr=640)
    dhkv_a = _mm(dckv_raw, w_ckv, "nt", "kv_down_c_dx")
    dhkv_b = _mm(dkr_raw, w_kr, "nt", "kv_down_r_dx")
    dw_ckv = _mm(hkv, dckv_raw, "tn", "kv_down_c_dw")
    dw_kr = _mm(hkv, dkr_raw, "tn", "kv_down_r_dw")
    dh0_res, dy0, dpost0, dpre1, dkvn = _rowwise_vjp(
        "mid_bwd", f_mid, mid_ins, [[dh2], [dhn1_a, dhn1_b], [dhkv_a, dhkv_b]], [0, 1, 2, 3, 4])

    dgated0 = _mm(dy0, w_out0, "nt", "gdn_out_dx")
    dw_out0 = _mm(gated0, dy0, "tn", "gdn_out_dw")
    do0, dz0, doutn = _rowwise_vjp("gdn_gate_bwd", f_gate0, gate0_ins, [[dgated0]], [0, 1, 2], tr=160)
    dq0, dk0, dv0, dbeta, dgc = _gdn_bwd(qkv, beta, gc, ckpt, do0, "gdn_scan_bwd")
    db_raw, da_raw, dalog, ddtb = _rowwise_vjp("gdn_gates_bwd", f_ba, ba_ins, [[dbeta], [dgc]], [0, 1, 2, 3])
    dqkv_raw, dconv = _conv_bwd(qkv_raw, conv_w, dq0, dk0, dv0, "gdn_conv_bwd")
    dhn0_a = _mm(dqkv_raw, w_qkv, "nt", "gdn_in_qkv_dx")
    dhn0_b = _mm(dz0, w_z0, "nt", "gdn_in_z_dx")
    dhn0_c = _mm(db_raw, w_b, "nt", "gdn_in_b_dx")
    dhn0_d = _mm(da_raw, w_a, "nt", "gdn_in_a_dx")
    dw_qkv = _mm(hn0, dqkv_raw, "tn", "gdn_in_qkv_dw")
    dw_z0 = _mm(hn0, dz0, "tn", "gdn_in_z_dw")
    dw_b = _mm(hn0, db_raw, "tn", "gdn_in_b_dw")
    dw_a = _mm(hn0, da_raw, "tn", "gdn_in_a_dw")
    dh0, dpre0 = _rowwise_vjp("pre0_bwd", f_pre0, [_In(h0), _In(pre0, "const")],
                              [[dhn0_a, dhn0_b, dhn0_c, dhn0_d], [dh0_res]], [0, 1])

    grad_x = dh0[ROW0:ROW0 + seq][None]
    g_meta = dh0[FRONT:ROW0]
    g_w_in0 = jnp.concatenate([dw_qkv, dw_z0, dw_b[:, :nv], dw_a[:, :nv]], axis=1)
    g_kv_down = jnp.concatenate([dw_ckv, dw_kr[:, :MLA_ROPE]], axis=1)
    g_kv_up = jnp.concatenate([dw_kn.reshape(MLA_KV_RANK, MLA_HEADS, LANE), dw_v.reshape(MLA_KV_RANK, MLA_HEADS, LANE)],
                              axis=2).reshape(MLA_KV_RANK, MLA_HEADS * 2 * LANE)
    g_w_in1 = jnp.concatenate([dw_cq, dw_z1], axis=1)
    g_qup = jnp.concatenate([dw_qn.reshape(MLA_Q_RANK, MLA_HEADS, LANE),
                             dw_qr.reshape(MLA_Q_RANK, MLA_HEADS, LANE)[:, :, :MLA_ROPE]],
                            axis=2).reshape(MLA_Q_RANK, MLA_HEADS * MLA_QK)
    big_g = [g_meta, dconv, dw_out0, g_kv_down, g_kv_up, g_w_in1, g_qup, dw_out1]

    def reduce_to_shard(g_by_chip, tag):
        got = _sibling_split(g_by_chip, "grads_sibling_split" + tag)
        chip_part = _add_pair(g_by_chip, got, "grads_chip_sum" + tag)
        from_chips = _chip_scatter(chip_part, "grads_chip_scatter" + tag)
        half_sum = _sum_slots(from_chips, "grads_total" + tag)
        return _sibling_join(half_sum, "grads_sibling_join" + tag)

    g_flat = reduce_to_shard(_pack_full(big_g, layout), "")
    g_win = reduce_to_shard(jnp.stack([g_w_in0[:, s * win_cols:(s + 1) * win_cols] for s in range(4)]), "_gdn_w_in")

    small_shapes = [(2, d), (2, d), (1, nv), (1, nv), (1, GDN_DK), (d,), (MLA_KV_RANK,), (1, MLA_Q_RANK), (1, LANE)]
    small_part = _pack_small([jnp.concatenate([dpre0, dpre1], axis=0), jnp.concatenate([dpost0, dpost1], axis=0),
                              dalog[:, :nv], ddtb[:, :nv], doutn, dkvn, dkvl, dqln, loss_part])
    small_tot = _all_sum_small(small_part, "small_sum")
    small_g = _unpack_small(small_tot, small_shapes)
    loss = small_g[-1][0, 0]

    d_flat, m_flat, v_flat = _adamw(w_flat, g_flat, _pack_shards(big_m), _pack_shards(big_v), "adamw_sharded")
    win_step = _adamw(gdn_w_in[0], g_win, m_gdn_w_in[0], v_gdn_w_in[0], "adamw_gdn_w_in")
    small_w = [pre_norm, post_norm, gdn_a_log, gdn_dt_bias, gdn_out_norm, kv_norm, kv_latent_norm, mla_q_latent_norm]
    small_m = [m_pre_norm, m_post_norm, m_gdn_a_log, m_gdn_dt_bias, m_gdn_out_norm, m_kv_norm, m_kv_latent_norm,
               m_mla_q_latent_norm]
    small_v = [v_pre_norm, v_post_norm, v_gdn_a_log, v_gdn_dt_bias, v_gdn_out_norm, v_kv_norm, v_kv_latent_norm,
               v_mla_q_latent_norm]
    g_small_flat = _pack_small(small_g[:-1])
    ds_flat, ms_flat, vs_flat = _adamw(_pack_small(small_w), g_small_flat, _pack_small(small_m), _pack_small(small_v),
                                       "adamw_replicated")

    def assemble(big_flat, small_flat, win):
        bigs = dict(zip(big_names, [a.reshape(w.shape) for a, w in zip(
            _unpack_shards(big_flat, layout),
            [meta_tokens, gdn_conv_w, gdn_w_out, kv_w_down, kv_w_up, mla_w_in, mla_w_q_up, mla_w_out])]))
        smalls = dict(zip(["pre_norm", "post_norm", "gdn_a_log", "gdn_dt_bias", "gdn_out_norm", "kv_norm",
                           "kv_latent_norm", "mla_q_latent_norm"], _unpack_small(small_flat, small_shapes[:-1])))
        both = {**bigs, **smalls, "gdn_w_in": win[None]}
        order = ["meta_tokens", "pre_norm", "post_norm", "gdn_w_in", "gdn_conv_w", "gdn_a_log", "gdn_dt_bias",
                 "gdn_out_norm", "gdn_w_out", "kv_norm", "kv_w_down", "kv_latent_norm", "kv_w_up", "mla_w_in",
                 "mla_q_latent_norm", "mla_w_q_up", "mla_w_out"]
        return [both[n] for n in order]

    grads = assemble(g_flat, g_small_flat, g_win)
    deltas = assemble(d_flat, ds_flat, win_step[0])
    new_m = assemble(m_flat, ms_flat, win_step[1])
    new_v = assemble(v_flat, vs_flat, win_step[2])
    return (loss, grad_x, *grads, *deltas, *new_m, *new_v)
```

```python
import functools
import math

import jax
import jax.numpy as jnp
from jax import lax
from jax.experimental import pallas as pl
from jax.experimental.pallas import tpu as pltpu

F32 = jnp.float32
BF16 = jnp.bfloat16
MESH = pl.DeviceIdType.MESH

D_MODEL = 1024
N_META = 16
FRONT = 48
ROW0 = FRONT + N_META
ROW_ALIGN = 640
NORM_EPS = 1e-6
LANE = 128

GDN_QK_HEADS = 8
GDN_V_HEADS = 16
GDN_DK = 128
GDN_CHUNK = 64
GDN_QK_W = 1024
GDN_V_W = 2048
GDN_CONV_W = 4096

MLA_HEADS = 16
MLA_NOPE = 128
MLA_ROPE = 64
MLA_QK = 192
MLA_Q_RANK = 256
MLA_KV_RANK = 128
ROPE_THETA = 10000.0

ADAM_LR = 0.001
ADAM_B1 = 0.9
ADAM_B2 = 0.999
ADAM_EPS = 1e-08
ADAM_WD = 0.01
ADAM_STEP = 10

VMEM_LIMIT_V7X = 56 * 1024 * 1024
NEG = -1e30

_NN = ((1,), (0,))
_NT = ((1,), (1,))
_TN = ((0,), (0,))
_HI = lax.Precision.HIGHEST
_X3 = lax.Precision.HIGH


def _pcall(body, **kw):
    return pl.pallas_call(body, **kw)


def _params(n_axes):
    return pltpu.CompilerParams(dimension_semantics=("arbitrary",) * n_axes, vmem_limit_bytes=VMEM_LIMIT_V7X)


def _dot(a, b, dims, prec=None):
    return lax.dot_general(a, b, (dims, ((), ())), precision=prec, preferred_element_type=F32)


def _bdot(a, b, dims):
    return _dot(a.astype(BF16), b.astype(BF16), dims)


def _hdot(a, b, dims=_NN):
    return _dot(a, b, dims, _HI)


def _fdot(a, b, dims):
    return _dot(a, b, dims)


def _tile(n):
    if n % ROW_ALIGN == 0:
        return ROW_ALIGN
    for t in (1024, 512, 256, 128):
        if n % t == 0:
            return t
    raise ValueError(n)


def _mm(a, b, mode, name, out_dtype=F32):
    if mode == "nn":
        (m, k), (k2, n) = a.shape, b.shape
    elif mode == "nt":
        (m, k), (n, k2) = a.shape, b.shape
    else:
        (k, m), (k2, n) = a.shape, b.shape
    assert k == k2, (a.shape, b.shape, mode)
    tm, tn, tk = _tile(m), _tile(n), _tile(k)
    nk = k // tk
    dims = {"nn": _NN, "nt": _NT, "tn": _TN}[mode]

    def body(a_ref, b_ref, o_ref, acc):
        kk = pl.program_id(2)

        @pl.when(kk == 0)
        def _():
            acc[...] = jnp.zeros_like(acc)

        acc[...] += _bdot(a_ref[...], b_ref[...], dims)

        @pl.when(kk == nk - 1)
        def _():
            o_ref[...] = acc[...].astype(out_dtype)

    if mode == "tn":
        a_spec = pl.BlockSpec((tk, tm), lambda i, j, kk: (kk, i))
    else:
        a_spec = pl.BlockSpec((tm, tk), lambda i, j, kk: (i, kk))
    if mode == "nt":
        b_spec = pl.BlockSpec((tn, tk), lambda i, j, kk: (j, kk))
    else:
        b_spec = pl.BlockSpec((tk, tn), lambda i, j, kk: (kk, j))
    return _pcall(
        body, name=name, grid=(m // tm, n // tn, nk),
        in_specs=[a_spec, b_spec],
        out_specs=pl.BlockSpec((tm, tn), lambda i, j, kk: (i, j)),
        out_shape=jax.ShapeDtypeStruct((m, n), out_dtype),
        scratch_shapes=[pltpu.VMEM((tm, tn), F32)],
        compiler_params=_params(3),
    )(a, b)


class _In:
    def __init__(self, arr, kind="row", grouped=False, goff=0):
        self.arr, self.kind, self.grouped, self.goff = arr, kind, grouped, goff


class _Out:
    def __init__(self, kind, shape, dtype=F32, grouped=False):
        self.kind, self.shape, self.dtype, self.grouped = kind, shape, dtype, grouped


def _rowwise(name, fn, ins, outs, *, groups=1, tr=320):
    lp = next(i.arr.shape[0] for i in ins if i.kind == "row")
    nr = lp // tr
    assert lp % tr == 0

    def in_spec(i):
        w = i.arr.shape[1]
        if i.kind == "row":
            if i.grouped:
                return pl.BlockSpec((tr, LANE), lambda g, r, o=i.goff: (r, g + o))
            return pl.BlockSpec((tr, w), lambda g, r: (r, 0))
        if i.grouped:
            return pl.BlockSpec((i.arr.shape[0], LANE), lambda g, r, o=i.goff: (0, g + o))
        return pl.BlockSpec(i.arr.shape, lambda g, r: (0, 0))

    def out_spec(o):
        if o.kind == "row":
            if o.grouped:
                return pl.BlockSpec((tr, LANE), lambda g, r: (r, g))
            assert groups == 1
            return pl.BlockSpec((tr, o.shape[1]), lambda g, r: (r, 0))
        if o.grouped:
            return pl.BlockSpec((o.shape[0], LANE), lambda g, r: (0, g))
        return pl.BlockSpec(o.shape, lambda g, r: (0, 0))

    n_in = len(ins)

    def body(*refs):
        g = pl.program_id(0)
        r = pl.program_id(1)
        ridx = r * tr + lax.broadcasted_iota(jnp.int32, (tr, 1), 0)
        res = fn(ridx, g, *[ref[...] for ref in refs[:n_in]])
        assert len(res) == len(outs), (name, len(res), len(outs))
        for o, ref, val in zip(outs, refs[n_in:], res):
            if o.kind == "row":
                ref[...] = val.astype(o.dtype)
            else:
                first = (r == 0) if o.grouped else jnp.logical_and(r == 0, g == 0)

                @pl.when(first)
                def _(ref=ref, val=val):
                    ref[...] = val.astype(F32)

                @pl.when(jnp.logical_not(first))
                def _(ref=ref, val=val):
                    ref[...] += val.astype(F32)

    res = _pcall(
        body, name=name, grid=(groups, nr),
        in_specs=[in_spec(i) for i in ins],
        out_specs=[out_spec(o) for o in outs],
        out_shape=[jax.ShapeDtypeStruct(o.shape, o.dtype) for o in outs],
        compiler_params=_params(2),
    )(*[i.arr for i in ins])
    return res


def _rowwise_vjp(name, fn, ins, cots, diff, *, groups=1, tr=320):
    n_in = len(ins)
    grouped = groups > 1
    cot_ins = []
    counts = []
    for arrs in cots:
        counts.append(len(arrs))
        for a in arrs:
            cot_ins.append(_In(a, "row", grouped=grouped and a.shape[1] > LANE))
    lp = next(i.arr.shape[0] for i in ins if i.kind == "row")
    outs = []
    for d in diff:
        i = ins[d]
        if i.kind == "row":
            w = groups * LANE if i.grouped else i.arr.shape[1]
            outs.append(_Out("row", (lp, w), F32, grouped=i.grouped))
        else:
            outs.append(_Out("acc", i.arr.shape, F32, grouped=i.grouped))

    def bfn(ridx, g, *allvals):
        vals = list(allvals[:n_in])
        cvals = allvals[n_in:]

        def f(*dv):
            full = list(vals)
            for i, v in zip(diff, dv):
                full[i] = v
            return tuple(fn(ridx, g, *full))

        primal, vjp = jax.vjp(f, *[vals[i].astype(F32) for i in diff])
        cts = []
        pos = 0
        for k, cnt in enumerate(counts):
            if cnt == 0:
                cts.append(jnp.zeros_like(primal[k]))
            else:
                c = cvals[pos].astype(F32)
                for extra in cvals[pos + 1:pos + cnt]:
                    c = c + extra.astype(F32)
                w = primal[k].shape[1]
                if c.shape[1] != w:
                    c = functools.reduce(jnp.add, [c[:, i * w:(i + 1) * w] for i in range(c.shape[1] // w)])
                cts.append(c.astype(primal[k].dtype))
            pos += cnt
        return vjp(tuple(cts))

    return _rowwise(name, bfn, list(ins) + cot_ins, outs, groups=groups, tr=tr)


def _rms(x, g):
    return x * lax.rsqrt(jnp.mean(x * x, axis=-1, keepdims=True) + NORM_EPS) * g


def _silu(x):
    return x * jax.nn.sigmoid(x)


def _softplus(x):
    return jnp.maximum(x, 0.0) + jnp.log(1.0 + jnp.exp(-jnp.abs(x)))


def _swap_halves(x):
    lane = lax.broadcasted_iota(jnp.int32, x.shape, x.ndim - 1)
    return jnp.where(lane < 32, pltpu.roll(x, LANE - 32, x.ndim - 1), pltpu.roll(x, 32, x.ndim - 1))


@jax.custom_vjp
def _rope(x, c, s):
    return x * c + _swap_halves(x) * s


def _rope_fwd(x, c, s):
    return _rope(x, c, s), (c, s)


def _rope_bwd(res, dy):
    c, s = res
    return dy * c + _swap_halves(dy * s), jnp.zeros_like(c), jnp.zeros_like(s)


_rope.defvjp(_rope_fwd, _rope_bwd)


def _conv_post(c, g):
    s = _silu(c)
    n = s * lax.rsqrt(jnp.sum(s * s, axis=-1, keepdims=True) + NORM_EPS)
    return jnp.where(g < GDN_QK_HEADS, n * (GDN_DK ** -0.5), jnp.where(g < 2 * GDN_QK_HEADS, n, s))


def _conv_taps(xe, w):
    c = xe[8:] * w[3]
    for s in (1, 2, 3):
        c = c + pltpu.roll(xe, s, 0)[8:] * w[3 - s]
    return c


CONV_LANES = 512
CONV_HEADS = CONV_LANES // LANE


def _conv_post_block(c, g):
    return jnp.concatenate([_conv_post(c[:, i * LANE:(i + 1) * LANE], g * CONV_HEADS + i)
                            for i in range(CONV_HEADS)], axis=1)


def _conv_fwd(x, w, name, tr=640):
    lp, width = x.shape
    cl = CONV_LANES
    nr = lp // tr

    def body(x_ref, prev_ref, w_ref, o_ref):
        g = pl.program_id(0)
        r = pl.program_id(1)
        prev = jnp.where(r > 0, prev_ref[...], 0.0)
        xe = jnp.concatenate([prev, x_ref[...]], axis=0)
        o_ref[...] = _conv_post_block(_conv_taps(xe, [w_ref[t:t + 1, :] for t in range(4)]), g)

    return _pcall(
        body, name=name, grid=(width // cl, nr),
        in_specs=[pl.BlockSpec((tr, cl), lambda g, r: (r, g)),
                  pl.BlockSpec((8, cl), lambda g, r: (jnp.maximum(r * (tr // 8) - 1, 0), g)),
                  pl.BlockSpec((4, cl), lambda g, r: (0, g))],
        out_specs=pl.BlockSpec((tr, cl), lambda g, r: (r, g)),
        out_shape=jax.ShapeDtypeStruct((lp, width), F32),
        compiler_params=_params(2),
    )(x, x, w)


def _conv_bwd(x, w, dq, dk, dv, name, tr=640):
    lp, width = x.shape
    cl = CONV_LANES
    nr = lp // tr
    last8 = lp // 8 - 1
    nq = GDN_QK_W // cl

    def body(x_ref, prev_ref, next_ref, w_ref, q_ref, k_ref, v_ref, q_n, k_n, v_n, dx_ref, dw_ref):
        g = pl.program_id(0)
        r = pl.program_id(1)
        w = [w_ref[t:t + 1, :] for t in range(4)]
        not_last = r < nr - 1

        def pick(a, b, c):
            return jnp.where(g < nq, a[...], jnp.where(g < 2 * nq, b[...], c[...]))

        dy = pick(q_ref, k_ref, v_ref)
        dyn = jnp.where(not_last, pick(q_n, k_n, v_n), 0.0)
        prev = jnp.where(r > 0, prev_ref[...], 0.0)
        nxt = jnp.where(not_last, next_ref[...], 0.0)
        xe = jnp.concatenate([prev, x_ref[...], nxt], axis=0)
        ce = _conv_taps(xe, w)
        _, vjp = jax.vjp(lambda c: _conv_post_block(c, g), ce)
        (dce,) = vjp(jnp.concatenate([dy, dyn], axis=0))
        n = tr + 8
        dx = dce * w[3]
        for s in (1, 2, 3):
            dx = dx + pltpu.roll(dce, n - s, 0) * w[3 - s]
        dx_ref[...] = dx[:tr]
        dc = dce[:tr]
        row4 = lax.broadcasted_iota(jnp.int32, (4, cl), 0)
        dw = jnp.zeros((4, cl), F32)
        for s in (0, 1, 2, 3):
            xs = xe[8:8 + tr] if s == 0 else pltpu.roll(xe, s, 0)[8:8 + tr]
            dw = dw + jnp.where(row4 == 3 - s, jnp.sum(dc * xs, axis=0, keepdims=True), 0.0)

        @pl.when(r == 0)
        def _():
            dw_ref[...] = dw

        @pl.when(r > 0)
        def _():
            dw_ref[...] += dw

    def col_q(g):
        return jnp.minimum(g, nq - 1)

    def col_k(g):
        return jnp.clip(g - nq, 0, nq - 1)

    def col_v(g):
        return jnp.maximum(g - 2 * nq, 0)

    def blk(colf):
        return pl.BlockSpec((tr, cl), lambda g, r: (r, colf(g)))

    def nblk(colf):
        return pl.BlockSpec((8, cl), lambda g, r: (jnp.minimum((r + 1) * (tr // 8), last8), colf(g)))

    return _pcall(
        body, name=name, grid=(width // cl, nr),
        in_specs=[pl.BlockSpec((tr, cl), lambda g, r: (r, g)),
                  pl.BlockSpec((8, cl), lambda g, r: (jnp.maximum(r * (tr // 8) - 1, 0), g)),
                  pl.BlockSpec((8, cl), lambda g, r: (jnp.minimum((r + 1) * (tr // 8), last8), g)),
                  pl.BlockSpec((4, cl), lambda g, r: (0, g)),
                  blk(col_q), blk(col_k), blk(col_v), nblk(col_q), nblk(col_k), nblk(col_v)],
        out_specs=[pl.BlockSpec((tr, cl), lambda g, r: (r, g)),
                   pl.BlockSpec((4, cl), lambda g, r: (0, g))],
        out_shape=[jax.ShapeDtypeStruct((lp, width), F32), jax.ShapeDtypeStruct((4, width), F32)],
        compiler_params=_params(2),
    )(x, x, x, w, dq, dk, dv, dq, dk, dv)


def _bmm(a, b, dims, prec=None):
    (ca,), (cb,) = dims
    return lax.dot_general(a, b, (((ca + 1,), (cb + 1,)), ((0,), (0,))), precision=prec,
                           preferred_element_type=F32)


def _inv_impl(m):
    c = m.shape[-1]
    ii = lax.broadcasted_iota(jnp.int32, (c, c), 0)
    jj = lax.broadcasted_iota(jnp.int32, (c, c), 1)
    eye = (ii == jj).astype(F32)

    def same_block(shift):
        return (ii >> shift) == (jj >> shift)

    n1 = jnp.where(same_block(3), -m, 0.0)
    n2 = _bmm(n1, n1, _NN, _X3)
    n4 = _bmm(n2, n2, _NN, _X3)
    d = _bmm(_bmm(eye + n1, eye + n2, _NN, _X3), eye + n4, _NN, _X3)
    shift = 3
    while (1 << shift) < c:
        low = jnp.where(jnp.logical_and(same_block(shift + 1), jnp.logical_not(same_block(shift))), m, 0.0)
        d = d - _bmm(d, _bmm(low, d, _NN, _X3), _NN, _X3)
        shift += 1
    return d


@jax.custom_vjp
def _inv_unit_lower(m):
    return _inv_impl(m)


def _inv_f(m):
    t = _inv_impl(m)
    return t, t


def _inv_b(t, dt):
    c = t.shape[-1]
    ii = lax.broadcasted_iota(jnp.int32, (c, c), 0)
    jj = lax.broadcasted_iota(jnp.int32, (c, c), 1)
    gm = _bmm(t, _bmm(dt, t, _NT, _X3), _TN, _X3)
    return (jnp.where(ii > jj, -gm, 0.0),)


_inv_unit_lower.defvjp(_inv_f, _inv_b)


GDN_HEADS_PER_STEP = 16


def _gdn_group(q, k, v, beta_blk, gc_blk, states, h0):
    hp = GDN_HEADS_PER_STEP
    c = q.shape[0]
    lane = lax.broadcasted_iota(jnp.int32, (1, LANE), 1)
    row8 = lax.broadcasted_iota(jnp.int32, (max(8, hp), LANE), 0)
    lane8 = lax.broadcasted_iota(jnp.int32, (max(8, hp), LANE), 1)
    gcr_all = _hdot((lane8 == h0 + row8).astype(F32), gc_blk, _NT)
    betas, gccs = [], []
    for i in range(hp):
        onehot = (lane == h0 + i).astype(F32)
        betas.append(jnp.sum(beta_blk * onehot, axis=1, keepdims=True))
        gccs.append(jnp.sum(gc_blk * onehot, axis=1, keepdims=True))
    beta = jnp.stack(betas)
    gcc = jnp.stack(gccs)
    gcr = jnp.stack([gcr_all[i:i + 1] for i in range(hp)])
    qh = jnp.stack([q[:, (i // 2) * LANE:(i // 2 + 1) * LANE] for i in range(hp)])
    kh = jnp.stack([k[:, (i // 2) * LANE:(i // 2 + 1) * LANE] for i in range(hp)])
    vh = jnp.stack([v[:, i * LANE:(i + 1) * LANE] for i in range(hp)])
    state = jnp.stack(states)
    ii = lax.broadcasted_iota(jnp.int32, (c, c), 0)
    jj = lax.broadcasted_iota(jnp.int32, (c, c), 1)
    incl = ii >= jj
    dec = jnp.where(incl, jnp.exp(jnp.where(incl, gcc - gcr, 0.0)), 0.0)
    eg = jnp.exp(gcc)
    m = _bmm(kh, kh, _NT) * beta * jnp.where(ii > jj, dec, 0.0)
    t = _inv_unit_lower(m)
    u = _bmm(t, vh * beta, _NN, _X3)
    w = _bmm(t, kh * (beta * eg), _NN, _X3)
    attn = _bmm(qh, kh, _NT) * dec
    rows = lax.broadcasted_iota(jnp.int32, (c, 1), 0)
    gl = jnp.sum(jnp.where(rows == c - 1, gcc, 0.0), axis=1, keepdims=True)
    v_new = u - _bmm(w, state, _NN)
    o = _bmm(qh * eg, state, _NN) + _bmm(attn, v_new, _NN)
    new_state = state * jnp.exp(gl) + _bmm(kh * jnp.exp(gl - gcc), v_new, _TN)
    return jnp.concatenate([o[i] for i in range(hp)], axis=1), tuple(new_state[i] for i in range(hp))


def _gdn_specs(nc, rev):
    def cidx(n):
        return (nc - 1 - n) if rev else n
    hp = GDN_HEADS_PER_STEP
    nqk = GDN_QK_HEADS
    c = GDN_CHUNK
    nq = 2 * nqk // hp
    q_spec = pl.BlockSpec((c, hp // 2 * LANE), lambda n, g: (cidx(n), g))
    k_spec = pl.BlockSpec((c, hp // 2 * LANE), lambda n, g: (cidx(n), nq + g))
    v_spec = pl.BlockSpec((c, hp * LANE), lambda n, g: (cidx(n), nq + g))
    s_spec = pl.BlockSpec((c, LANE), lambda n, g: (cidx(n), 0))
    o_spec = pl.BlockSpec((c, hp * LANE), lambda n, g: (cidx(n), g))
    ck_spec = pl.BlockSpec((hp, 1, GDN_DK, LANE), lambda n, g: (g, cidx(n), 0, 0))
    return q_spec, k_spec, v_spec, s_spec, o_spec, ck_spec


def _gdn_fwd(qkv, beta, gc, name):
    lp = qkv.shape[0]
    nc = lp // GDN_CHUNK
    nh = GDN_V_HEADS
    hp = GDN_HEADS_PER_STEP
    q_spec, k_spec, v_spec, s_spec, o_spec, ck_spec = _gdn_specs(nc, False)

    def body(q_ref, k_ref, v_ref, b_ref, g_ref, o_ref, ck_ref, state):
        n = pl.program_id(0)
        g = pl.program_id(1)

        @pl.when(n == 0)
        def _():
            for i in range(hp):
                state[g * hp + i] = jnp.zeros((GDN_DK, LANE), F32)

        states = tuple(state[g * hp + i] for i in range(hp))
        for i in range(hp):
            ck_ref[i, 0] = states[i]
        o, new_states = _gdn_group(q_ref[...], k_ref[...], v_ref[...], b_ref[...], g_ref[...], states, g * hp)
        o_ref[...] = o
        for i in range(hp):
            state[g * hp + i] = new_states[i]

    return _pcall(
        body, name=name, grid=(nc, nh // hp),
        in_specs=[q_spec, k_spec, v_spec, s_spec, s_spec],
        out_specs=[o_spec, ck_spec],
        out_shape=[jax.ShapeDtypeStruct((lp, GDN_V_W), F32),
                   jax.ShapeDtypeStruct((nh, nc, GDN_DK, LANE), F32)],
        scratch_shapes=[pltpu.VMEM((nh, GDN_DK, LANE), F32)],
        compiler_params=_params(2),
    )(qkv, qkv, qkv, beta, gc)


def _gdn_bwd(qkv, beta, gc, ckpt, do, name):
    lp = qkv.shape[0]
    nc = lp // GDN_CHUNK
    nh = GDN_V_HEADS
    hp = GDN_HEADS_PER_STEP
    q_spec, k_spec, v_spec, s_spec, o_spec, ck_spec = _gdn_specs(nc, True)

    def body(q_ref, k_ref, v_ref, b_ref, g_ref, ck_ref, do_ref,
             dq_ref, dk_ref, dv_ref, db_ref, dg_ref, dstate):
        n = pl.program_id(0)
        g = pl.program_id(1)

        @pl.when(n == 0)
        def _():
            for i in range(hp):
                dstate[g * hp + i] = jnp.zeros((GDN_DK, LANE), F32)

        states = tuple(ck_ref[i, 0] for i in range(hp))
        _, vjp = jax.vjp(lambda q, k, v, b, gg, s: _gdn_group(q, k, v, b, gg, s, g * hp),
                         q_ref[...], k_ref[...], v_ref[...], b_ref[...], g_ref[...], states)
        dq, dk, dv, db, dg, ds = vjp((do_ref[...], tuple(dstate[g * hp + i] for i in range(hp))))
        dq_ref[...] = dq
        dk_ref[...] = dk
        dv_ref[...] = dv
        for i in range(hp):
            dstate[g * hp + i] = ds[i]

        @pl.when(g == 0)
        def _():
            db_ref[...] = db
            dg_ref[...] = dg

        @pl.when(g > 0)
        def _():
            db_ref[...] += db
            dg_ref[...] += dg

    qk_shape = jax.ShapeDtypeStruct((lp, GDN_QK_W), F32)
    big = jax.ShapeDtypeStruct((lp, GDN_V_W), F32)
    small = jax.ShapeDtypeStruct((lp, LANE), F32)
    dq_spec = pl.BlockSpec((GDN_CHUNK, hp // 2 * LANE), lambda n, g: (nc - 1 - n, g))
    return _pcall(
        body, name=name, grid=(nc, nh // hp),
        in_specs=[q_spec, k_spec, v_spec, s_spec, s_spec, ck_spec, o_spec],
        out_specs=[dq_spec, dq_spec, o_spec, s_spec, s_spec],
        out_shape=[qk_shape, qk_shape, big, small, small],
        scratch_shapes=[pltpu.VMEM((nh, GDN_DK, LANE), F32)],
        compiler_params=_params(2),
    )(qkv, qkv, qkv, beta, gc, ckpt, do)


LOG2E = 1.4426950408889634
LN2 = 0.6931471805599453
Q_PRESCALE = MLA_QK ** -0.5 * LOG2E


ATT_SUB = 128


def _att_mask(i, j, tb, transposed):
    r = lax.broadcasted_iota(jnp.int32, (tb, tb), 0)
    c = lax.broadcasted_iota(jnp.int32, (tb, tb), 1)
    qpos, kpos = (i * tb + c, j * tb + r) if transposed else (i * tb + r, j * tb + c)
    return jnp.logical_and(kpos <= qpos, kpos >= FRONT)


def _causal_pairs(nb, by_key):
    if by_key:
        pairs = [(i, j) for j in range(nb) for i in range(j, nb)]
    else:
        pairs = [(i, j) for i in range(nb) for j in range(i + 1)]
    return jnp.array([p[0] for p in pairs], jnp.int32), jnp.array([p[1] for p in pairs], jnp.int32)


def _masked_and_plain(i, j, step):
    edge = jnp.logical_or(j == i, j == 0)

    @pl.when(jnp.logical_and(edge, j <= i))
    def _():
        step(True)

    @pl.when(jnp.logical_and(jnp.logical_not(edge), j < i))
    def _():
        step(False)


def _cat(a_ref, b_ref):
    return jnp.concatenate([a_ref[...], b_ref[...]], axis=1)


def _flash_fwd(qn, qr, kn, kr, v, name, tb=ROW_ALIGN):
    lp = qn.shape[0]
    nb = lp // tb
    nh = MLA_HEADS
    qi, kj = _causal_pairs(nb, by_key=False)

    def body(qi_ref, kj_ref, qn_ref, qr_ref, kn_ref, kr_ref, v_ref, o_ref, lse_ref, m_s, l_s, acc):
        t = pl.program_id(1)
        i, j = qi_ref[t], kj_ref[t]

        @pl.when(j == 0)
        def _():
            m_s[...] = jnp.full_like(m_s, NEG)
            l_s[...] = jnp.zeros_like(l_s)
            acc[...] = jnp.zeros_like(acc)

        def step(masked):
            k = _cat(kn_ref, kr_ref)
            v = v_ref[...]
            n_sub = tb // ATT_SUB

            def scores(r):
                rows = pl.ds(r * ATT_SUB, ATT_SUB)
                return _dot(jnp.concatenate([qn_ref[rows, :], qr_ref[rows, :]], axis=1), k, _NT)

            s_next = scores(0)
            for r in range(n_sub):
                s = s_next
                if r + 1 < n_sub:
                    s_next = scores(r + 1)
                rows = pl.ds(r * ATT_SUB, ATT_SUB)
                if masked:
                    qpos = i * tb + r * ATT_SUB + lax.broadcasted_iota(jnp.int32, (ATT_SUB, tb), 0)
                    kpos = j * tb + lax.broadcasted_iota(jnp.int32, (ATT_SUB, tb), 1)
                    s = jnp.where(jnp.logical_and(kpos <= qpos, kpos >= FRONT), s, NEG)
                m_old = m_s[rows, :]
                m_new = jnp.maximum(m_old, jnp.max(s, axis=1, keepdims=True))
                alpha = jnp.exp2(m_old - m_new)
                p = jnp.exp2(s - m_new)
                l_s[rows, :] = alpha * l_s[rows, :] + jnp.sum(p, axis=1, keepdims=True)
                acc[rows, :] = alpha * acc[rows, :] + _dot(p.astype(BF16), v, _NN)
                m_s[rows, :] = m_new

        _masked_and_plain(i, j, step)

        @pl.when(j == i)
        def _():
            o_ref[...] = acc[...] / l_s[...]
            lse_ref[...] = jnp.broadcast_to(m_s[...] + jnp.log(l_s[...]) * LOG2E, (tb, LANE))

    qspec = pl.BlockSpec((tb, LANE), lambda h, t, qi_, kj_: (qi_[t], h))
    kspec = pl.BlockSpec((tb, LANE), lambda h, t, qi_, kj_: (kj_[t], h))
    krspec = pl.BlockSpec((tb, LANE), lambda h, t, qi_, kj_: (kj_[t], 0))
    shp = jax.ShapeDtypeStruct((lp, nh * LANE), F32)
    return _pcall(
        body, name=name, out_shape=[shp, shp],
        grid_spec=pltpu.PrefetchScalarGridSpec(
            num_scalar_prefetch=2, grid=(nh, qi.shape[0]),
            in_specs=[qspec, qspec, kspec, krspec, kspec], out_specs=[qspec, qspec],
            scratch_shapes=[pltpu.VMEM((tb, 1), F32), pltpu.VMEM((tb, 1), F32), pltpu.VMEM((tb, LANE), F32)]),
        compiler_params=_params(2),
    )(qi, kj, qn, qr, kn, kr, v)


def _flash_bwd(qn, qr, kn, kr, v, o, do, lse, name, tb=ROW_ALIGN):
    lp = qn.shape[0]
    nb = lp // tb
    nh = MLA_HEADS
    qi, kj = _causal_pairs(nb, by_key=True)
    knt, krt = kn.T, kr.T

    def body(qi_ref, kj_ref, qn_ref, qr_ref, kn_ref, kr_ref, knt_ref, krt_ref, v_ref, o_ref, do_ref, lse_ref,
             dqnt_ref, dqrt_ref, dkn_ref, dkr_ref, dv_ref, dk_acc, dv_acc):
        t = pl.program_id(1)
        i, j = qi_ref[t], kj_ref[t]

        @pl.when(t == 0)
        def _():
            dqnt_ref[...] = jnp.zeros_like(dqnt_ref)
            dqrt_ref[...] = jnp.zeros_like(dqrt_ref)

        @pl.when(i == j)
        def _():
            dk_acc[...] = jnp.zeros_like(dk_acc)
            dv_acc[...] = jnp.zeros_like(dv_acc)

        def step(masked):
            q = _cat(qn_ref, qr_ref)
            st = _dot(_cat(kn_ref, kr_ref), q, _NT)
            if masked:
                st = jnp.where(_att_mask(i, j, tb, True), st, NEG)
            do_blk = do_ref[...]
            lane = lax.broadcasted_iota(jnp.int32, (8, LANE), 1)
            lse_row = _hdot((lane == 0).astype(F32), lse_ref[...], _NT)[0:1]
            delta_row = _hdot(jnp.ones((8, LANE), F32), do_blk * o_ref[...], _NT)[0:1]
            pt = jnp.exp2(st - lse_row)
            do_b = do_blk.astype(BF16)
            dv_acc[...] += _dot(pt.astype(BF16), do_b, _NN)
            dpt = _dot(v_ref[...], do_b, _NT)
            dst = (pt * (dpt - delta_row)).astype(BF16)
            dk_acc[...] += _dot(dst, q, _NN)
            dqnt_ref[i] += _dot(knt_ref[...], dst, _NN) * LN2
            dqrt_ref[i] += _dot(krt_ref[...], dst, _NN) * LN2

        _masked_and_plain(i, j, step)

        @pl.when(i == nb - 1)
        def _():
            dkn_ref[...] = dk_acc[:, :LANE] * LN2
            dkr_ref[...] = dk_acc[:, LANE:] * LN2
            dv_ref[...] = dv_acc[...]

    qspec = pl.BlockSpec((tb, LANE), lambda h, t, qi_, kj_: (qi_[t], h))
    kspec = pl.BlockSpec((tb, LANE), lambda h, t, qi_, kj_: (kj_[t], h))
    krspec = pl.BlockSpec((tb, LANE), lambda h, t, qi_, kj_: (kj_[t], 0))
    ktspec = pl.BlockSpec((LANE, tb), lambda h, t, qi_, kj_: (h, kj_[t]))
    krtspec = pl.BlockSpec((LANE, tb), lambda h, t, qi_, kj_: (0, kj_[t]))
    dqtspec = pl.BlockSpec((nb, LANE, tb), lambda h, t, qi_, kj_: (h, 0, 0))
    shp = jax.ShapeDtypeStruct((lp, nh * LANE), F32)
    dqt_shape = jax.ShapeDtypeStruct((nh * nb, LANE, tb), F32)
    dqnt, dqrt, dkn, dkr, dv = _pcall(
        body, name=name, out_shape=[dqt_shape, dqt_shape, shp, shp, shp],
        grid_spec=pltpu.PrefetchScalarGridSpec(
            num_scalar_prefetch=2, grid=(nh, qi.shape[0]),
            in_specs=[qspec, qspec, kspec, krspec, ktspec, krtspec, kspec, qspec, qspec, qspec],
            out_specs=[dqtspec, dqtspec, kspec, kspec, kspec],
            scratch_shapes=[pltpu.VMEM((tb, 2 * LANE), F32), pltpu.VMEM((tb, LANE), F32)]),
        compiler_params=_params(2),
    )(qi, kj, qn, qr, kn, kr, knt, krt, v, o, do, lse)

    def rows_major(a):
        return a.reshape(nh, nb, LANE, tb).transpose(1, 3, 0, 2).reshape(lp, nh * LANE)

    return rows_major(dqnt), rows_major(dqrt), dkn, dkr, dv


ELEMENTWISE_BLOCK_BYTES = 1 << 20


def _row_tile(rows, width, copies=1):
    for t in (1024, 512, 256, 128, 64, 32, 16, 8):
        if rows % t == 0 and t * width * 4 * copies <= ELEMENTWISE_BLOCK_BYTES:
            return t
    return rows


def _adamw(w, g, m, v, name):
    rows, width = w.shape
    tr = _row_tile(rows, width)

    def body(w_ref, g_ref, m_ref, v_ref, d_ref, nm_ref, nv_ref):
        gg = g_ref[...]
        nm = ADAM_B1 * m_ref[...] + (1.0 - ADAM_B1) * gg
        nv = ADAM_B2 * v_ref[...] + (1.0 - ADAM_B2) * jnp.square(gg)
        m_hat = nm / (1.0 - ADAM_B1 ** ADAM_STEP)
        v_hat = nv / (1.0 - ADAM_B2 ** ADAM_STEP)
        d_ref[...] = -ADAM_LR * (m_hat / (jnp.sqrt(v_hat) + ADAM_EPS) + ADAM_WD * w_ref[...])
        nm_ref[...] = nm
        nv_ref[...] = nv

    spec = pl.BlockSpec((tr, width), lambda r: (r, 0))
    shp = jax.ShapeDtypeStruct((rows, width), F32)
    return _pcall(body, name=name, grid=(rows // tr,), in_specs=[spec] * 4, out_specs=[spec] * 3,
                  out_shape=[shp] * 3, compiler_params=_params(1))(w, g, m, v)


def _add_pair(a, b, name):
    s, rows, width = b.shape
    tr = _row_tile(rows, width)
    nt = rows // tr

    def body(c_ref, a_ref, b_ref, o_ref):
        o_ref[...] = a_ref[...] + b_ref[...]

    spec = pl.BlockSpec((1, tr, width), lambda i, r, c_ref: (i, r, 0))
    return _pcall(
        body, name=name, out_shape=jax.ShapeDtypeStruct(b.shape, F32),
        grid_spec=pltpu.PrefetchScalarGridSpec(
            num_scalar_prefetch=1, grid=(s, nt),
            in_specs=[pl.BlockSpec((1, tr, width), lambda i, r, c_ref: (i, c_ref[0] * nt + r, 0)), spec],
            out_specs=spec),
        compiler_params=_params(2),
    )(_core_index(), a, b)


def _sum_slots(a, name):
    s, rows, width = a.shape
    tr = _row_tile(rows, width, copies=s)

    def body(a_ref, o_ref):
        tot = a_ref[0]
        for k in range(1, s):
            tot = tot + a_ref[k]
        o_ref[...] = tot

    return _pcall(body, name=name, grid=(rows // tr,),
                  in_specs=[pl.BlockSpec((s, tr, width), lambda r: (0, r, 0))],
                  out_specs=pl.BlockSpec((tr, width), lambda r: (r, 0)),
                  out_shape=jax.ShapeDtypeStruct((rows, width), F32), compiler_params=_params(1))(a)


_ANY = pl.BlockSpec(memory_space=pl.ANY)


def _my_place():
    return lax.axis_index("x"), lax.axis_index("y"), lax.axis_index("c")


def _core_index():
    return lax.axis_index("c").astype(jnp.int32).reshape(1)


def _other_chips(x, y):
    return [(1 - x, y), (x, 1 - y), (1 - x, 1 - y)]


def _gather_shards(flat, name):
    rows, width = flat.shape

    def body(x_ref, out_ref, send_sems, recv_sems, local_sem):
        x, y, c = _my_place()
        mine = pltpu.make_async_copy(x_ref, out_ref.at[2 * x + y], local_sem)
        mine.start()
        sends = []
        for k, (px, py) in enumerate(_other_chips(x, y)):
            cp = pltpu.make_async_remote_copy(
                src_ref=x_ref, dst_ref=out_ref.at[2 * x + y], send_sem=send_sems.at[k], recv_sem=recv_sems.at[k],
                device_id=(px, py, c), device_id_type=MESH)
            cp.start()
            sends.append(cp)
        for k, (px, py) in enumerate(_other_chips(x, y)):
            pltpu.make_async_remote_copy(
                src_ref=x_ref, dst_ref=out_ref.at[2 * px + py], send_sem=send_sems.at[k], recv_sem=recv_sems.at[k],
                device_id=(px, py, c), device_id_type=MESH).wait_recv()
        for cp in sends:
            cp.wait_send()
        mine.wait()

    return _pcall(
        body, name=name, in_specs=[_ANY], out_specs=_ANY,
        out_shape=jax.ShapeDtypeStruct((4, rows, width), flat.dtype),
        scratch_shapes=[pltpu.SemaphoreType.DMA((3,)), pltpu.SemaphoreType.DMA((3,)), pltpu.SemaphoreType.DMA],
    )(flat)


def _sibling_split(g, name):
    s, rows, width = g.shape
    half = rows // 2
    tr = _row_tile(half, width)
    nt = half // tr

    def body(c_ref, g_blk, got_ref, send_sem, recv_sem):
        k = pl.program_id(0)
        t = pl.program_id(1)
        x, y, c = _my_place()
        cp = pltpu.make_async_remote_copy(
            src_ref=g_blk.at[0], dst_ref=got_ref.at[k, pl.ds(pl.multiple_of(t * tr, 8), tr), :],
            send_sem=send_sem, recv_sem=recv_sem, device_id=(x, y, 1 - c), device_id_type=MESH)
        cp.start()
        cp.wait_send()

        @pl.when(jnp.logical_and(k == s - 1, t == nt - 1))
        def _():
            pltpu.make_async_remote_copy(
                src_ref=got_ref, dst_ref=got_ref, send_sem=send_sem, recv_sem=recv_sem,
                device_id=(x, y, 1 - c), device_id_type=MESH).wait_recv()

    return _pcall(
        body, name=name, out_shape=jax.ShapeDtypeStruct((s, half, width), g.dtype),
        grid_spec=pltpu.PrefetchScalarGridSpec(
            num_scalar_prefetch=1, grid=(s, nt),
            in_specs=[pl.BlockSpec((1, tr, width), lambda k, t, c_ref: (k, (1 - c_ref[0]) * nt + t, 0))],
            out_specs=_ANY,
            scratch_shapes=[pltpu.SemaphoreType.DMA, pltpu.SemaphoreType.DMA]),
        compiler_params=_params(2),
    )(_core_index(), g)


def _chip_scatter(p, name):
    s, rows, width = p.shape

    def body(p_ref, out_ref, send_sems, recv_sems, local_sem):
        x, y, c = _my_place()
        me = 2 * x + y
        mine = pltpu.make_async_copy(p_ref.at[me], out_ref.at[me], local_sem)
        mine.start()
        sends = []
        for k, (px, py) in enumerate(_other_chips(x, y)):
            cp = pltpu.make_async_remote_copy(
                src_ref=p_ref.at[2 * px + py], dst_ref=out_ref.at[me], send_sem=send_sems.at[k],
                recv_sem=recv_sems.at[k], device_id=(px, py, c), device_id_type=MESH)
            cp.start()
            sends.append(cp)
        for k, (px, py) in enumerate(_other_chips(x, y)):
            pltpu.make_async_remote_copy(
                src_ref=p_ref.at[me], dst_ref=out_ref.at[2 * px + py], send_sem=send_sems.at[k],
                recv_sem=recv_sems.at[k], device_id=(px, py, c), device_id_type=MESH).wait_recv()
        for cp in sends:
            cp.wait_send()
        mine.wait()

    return _pcall(
        body, name=name, in_specs=[_ANY], out_specs=_ANY,
        out_shape=jax.ShapeDtypeStruct(p.shape, p.dtype),
        scratch_shapes=[pltpu.SemaphoreType.DMA((3,)), pltpu.SemaphoreType.DMA((3,)), pltpu.SemaphoreType.DMA],
    )(p)


def _sibling_join(qh, name):
    half, width = qh.shape
    tr = _row_tile(half, width)
    nt = half // tr

    def body(q_blk, out_ref, send_sem, recv_sem, local_sem):
        t = pl.program_id(0)
        x, y, c = _my_place()
        dst = out_ref.at[pl.ds(pl.multiple_of(c * half + t * tr, 8), tr), :]
        cp = pltpu.make_async_remote_copy(
            src_ref=q_blk, dst_ref=dst, send_sem=send_sem, recv_sem=recv_sem,
            device_id=(x, y, 1 - c), device_id_type=MESH)
        cp.start()
        mine = pltpu.make_async_copy(q_blk, dst, local_sem)
        mine.start()
        cp.wait_send()
        mine.wait()

        @pl.when(t == nt - 1)
        def _():
            theirs = out_ref.at[pl.ds(pl.multiple_of((1 - c) * half, 8), half), :]
            pltpu.make_async_remote_copy(
                src_ref=theirs, dst_ref=theirs, send_sem=send_sem, recv_sem=recv_sem,
                device_id=(x, y, 1 - c), device_id_type=MESH).wait_recv()

    return _pcall(
        body, name=name, grid=(nt,),
        in_specs=[pl.BlockSpec((tr, width), lambda t: (t, 0))], out_specs=_ANY,
        out_shape=jax.ShapeDtypeStruct((2 * half, width), qh.dtype),
        scratch_shapes=[pltpu.SemaphoreType.DMA, pltpu.SemaphoreType.DMA, pltpu.SemaphoreType.DMA],
        compiler_params=_params(1),
    )(qh)


def _all_sum_small(part, name):
    rows, width = part.shape

    def body(p_ref, out_ref, land, send_sems, recv_sems):
        x, y, c = _my_place()
        me = 4 * x + 2 * y + c
        land[me] = p_ref[...]
        sends = []
        for k in range(1, 8):
            peer = (x ^ (k >> 2), y ^ ((k >> 1) & 1), c ^ (k & 1))
            cp = pltpu.make_async_remote_copy(
                src_ref=p_ref, dst_ref=land.at[me], send_sem=send_sems.at[k - 1], recv_sem=recv_sems.at[k - 1],
                device_id=peer, device_id_type=MESH)
            cp.start()
            sends.append(cp)
        for k in range(1, 8):
            px, py, pc = x ^ (k >> 2), y ^ ((k >> 1) & 1), c ^ (k & 1)
            pltpu.make_async_remote_copy(
                src_ref=p_ref, dst_ref=land.at[4 * px + 2 * py + pc], send_sem=send_sems.at[k - 1],
                recv_sem=recv_sems.at[k - 1], device_id=(px, py, pc), device_id_type=MESH).wait_recv()
        for cp in sends:
            cp.wait_send()
        tot = land[0]
        for k in range(1, 8):
            tot = tot + land[k]
        out_ref[...] = tot

    vmem = pl.BlockSpec(memory_space=pltpu.VMEM)
    return _pcall(
        body, name=name, in_specs=[vmem], out_specs=vmem,
        out_shape=jax.ShapeDtypeStruct((rows, width), F32),
        scratch_shapes=[pltpu.VMEM((8, rows, width), F32), pltpu.SemaphoreType.DMA((7,)),
                        pltpu.SemaphoreType.DMA((7,))],
    )(part)


def _big_layout(shards):
    return [(a.shape[0], a.shape[1], ax) for a, ax in shards]


FLAT_ROW_MULTIPLE = 2048


def _pack_shards(arrs, row_multiple=FLAT_ROW_MULTIPLE):
    flat = jnp.concatenate([a.reshape(-1) for a in arrs])
    return jnp.pad(flat, (0, -flat.shape[0] % (row_multiple * LANE))).reshape(-1, LANE)


def _unpack_shards(flat, layout):
    flat = flat.reshape(-1)
    out, off = [], 0
    for r, c, _ in layout:
        out.append(flat[off:off + r * c].reshape(r, c))
        off += r * c
    return out


def _unpack_full(gathered, layout):
    g = gathered.reshape(4, -1)
    out, off = [], 0
    for r, c, ax in layout:
        seg = g[:, off:off + r * c].reshape(4, r, c)
        out.append(seg.transpose(1, 0, 2).reshape(r, 4 * c) if ax == 1 else seg.reshape(4 * r, c))
        off += r * c
    return out


def _pack_full(fulls, layout):
    parts = []
    for a, (r, c, ax) in zip(fulls, layout):
        if ax == 1:
            parts.append(a.reshape(r, 4, c).transpose(1, 0, 2).reshape(4, r * c))
        else:
            parts.append(a.reshape(4, r * c))
    flat = jnp.concatenate(parts, axis=1)
    return jnp.pad(flat, ((0, 0), (0, -flat.shape[1] % (FLAT_ROW_MULTIPLE * LANE)))).reshape(4, -1, LANE)


def _pad_lanes(a, width=LANE):
    return jnp.pad(a, [(0, 0)] * (a.ndim - 1) + [(0, width - a.shape[-1])])


def _pack_small(arrs):
    rows = [_pad_lanes(a.reshape(1, -1), -(-a.size // LANE) * LANE).reshape(-1, LANE) for a in arrs]
    flat = jnp.concatenate(rows, axis=0)
    return jnp.pad(flat, ((0, -flat.shape[0] % 8), (0, 0)))


def _unpack_small(flat, shapes):
    out, off = [], 0
    for shp in shapes:
        n = math.prod(shp)
        nr = -(-n // LANE)
        out.append(flat[off:off + nr].reshape(-1)[:n].reshape(shp))
        off += nr
    return out


def kernel(x, meta_tokens, pre_norm, post_norm, gdn_w_in, gdn_conv_w, gdn_a_log, gdn_dt_bias, gdn_out_norm, gdn_w_out, kv_norm, kv_w_down, kv_latent_norm, kv_w_up, mla_w_in, mla_q_latent_norm, mla_w_q_up, mla_w_out, loss_target, m_meta_tokens, m_pre_norm, m_post_norm, m_gdn_w_in, m_gdn_conv_w, m_gdn_a_log, m_gdn_dt_bias, m_gdn_out_norm, m_gdn_w_out, m_kv_norm, m_kv_w_down, m_kv_latent_norm, m_kv_w_up, m_mla_w_in, m_mla_q_latent_norm, m_mla_w_q_up, m_mla_w_out, v_meta_tokens, v_pre_norm, v_post_norm, v_gdn_w_in, v_gdn_conv_w, v_gdn_a_log, v_gdn_dt_bias, v_gdn_out_norm, v_gdn_w_out, v_kv_norm, v_kv_w_down, v_kv_latent_norm, v_kv_w_up, v_mla_w_in, v_mla_q_latent_norm, v_mla_w_q_up, v_mla_w_out):
    seq = x.shape[1]
    d = D_MODEL
    lp = -(-(ROW0 + seq) // ROW_ALIGN) * ROW_ALIGN
    tail = lp - ROW0 - seq

    big_names = ["meta_tokens", "gdn_conv_w", "gdn_w_out", "kv_w_down", "kv_w_up", "mla_w_in", "mla_w_q_up",
                 "mla_w_out"]
    big_axis = [1, 1, 0, 0, 1, 1, 1, 0]
    big_w = [meta_tokens, gdn_conv_w[0], gdn_w_out[0], kv_w_down, kv_w_up, mla_w_in[0], mla_w_q_up[0], mla_w_out[0]]
    big_m = [m_meta_tokens, m_gdn_conv_w[0], m_gdn_w_out[0], m_kv_w_down, m_kv_w_up, m_mla_w_in[0], m_mla_w_q_up[0],
             m_mla_w_out[0]]
    big_v = [v_meta_tokens, v_gdn_conv_w[0], v_gdn_w_out[0], v_kv_w_down, v_kv_w_up, v_mla_w_in[0], v_mla_w_q_up[0],
             v_mla_w_out[0]]
    layout = _big_layout(list(zip(big_w, big_axis)))
    w_flat = _pack_shards(big_w)
    meta_f, conv_w = _unpack_full(
        _gather_shards(_pack_shards(big_w[:2], row_multiple=16), "gather_meta_conv"), layout[:2])
    mm_shards = [w.astype(BF16) for w in big_w[2:6]] + [(big_w[6] * Q_PRESCALE).astype(BF16), big_w[7].astype(BF16)]
    (w_out0, kv_down, kv_up, w_in1, w_qup, w_out1) = _unpack_full(
        _gather_shards(_pack_shards(mm_shards), "gather_weights"), layout[2:])
    w_in0_shards = _gather_shards(gdn_w_in[0].astype(BF16), "gather_gdn_w_in")
    w_in0 = jnp.concatenate([w_in0_shards[s] for s in range(4)], axis=1)
    win_cols = gdn_w_in.shape[2]

    nv = GDN_V_HEADS
    w_qkv = w_in0[:, :GDN_CONV_W]
    w_z0 = w_in0[:, GDN_CONV_W:GDN_CONV_W + GDN_V_W]
    w_b = _pad_lanes(w_in0[:, GDN_CONV_W + GDN_V_W:GDN_CONV_W + GDN_V_W + nv])
    w_a = _pad_lanes(w_in0[:, GDN_CONV_W + GDN_V_W + nv:])
    w_ckv = kv_down[:, :MLA_KV_RANK]
    w_kr = _pad_lanes(kv_down[:, MLA_KV_RANK:])
    kvu = kv_up.reshape(MLA_KV_RANK, MLA_HEADS, 2 * LANE)
    w_kn = kvu[:, :, :LANE].reshape(MLA_KV_RANK, MLA_HEADS * LANE)
    w_v = kvu[:, :, LANE:].reshape(MLA_KV_RANK, MLA_HEADS * LANE)
    w_cq = w_in1[:, :MLA_Q_RANK]
    w_z1 = w_in1[:, MLA_Q_RANK:]
    qu = w_qup.reshape(MLA_Q_RANK, MLA_HEADS, MLA_QK)
    w_qn = qu[:, :, :MLA_NOPE].reshape(MLA_Q_RANK, MLA_HEADS * LANE)
    w_qr = _pad_lanes(qu[:, :, MLA_NOPE:]).reshape(MLA_Q_RANK, MLA_HEADS * LANE)

    pre0, pre1 = pre_norm[0:1], pre_norm[1:2]
    post0, post1 = post_norm[0:1], post_norm[1:2]
    a_log = _pad_lanes(gdn_a_log)
    dt_bias = _pad_lanes(gdn_dt_bias)
    kvn = kv_norm.reshape(1, d)
    kvl = kv_latent_norm.reshape(1, MLA_KV_RANK)
    qln = mla_q_latent_norm

    h0 = jnp.concatenate([jnp.zeros((FRONT, d), F32), meta_f, x[0], jnp.zeros((tail, d), F32)], axis=0)
    tgt = jnp.pad(loss_target[0], ((ROW0, tail), (0, 0)))
    pos = jnp.maximum(jnp.arange(lp, dtype=jnp.int32) - FRONT, 0).astype(F32)
    inv = ROPE_THETA ** (-jnp.arange(0, MLA_ROPE, 2, dtype=F32) / MLA_ROPE)
    ang = pos[:, None] * inv[None, :]
    zeros64 = jnp.zeros((lp, LANE - MLA_ROPE), F32)
    cos_t = jnp.concatenate([jnp.cos(ang), jnp.cos(ang), zeros64], axis=1)
    sin_t = jnp.concatenate([-jnp.sin(ang), jnp.sin(ang), zeros64], axis=1)

    def valid_rows(ridx):
        return jnp.logical_and(ridx >= FRONT, ridx < ROW0 + seq)

    def f_pre0(ridx, g, h, gain):
        return _rms(h, gain), h

    (hn0,) = _rowwise("pre0", lambda *a: f_pre0(*a)[:1], [_In(h0), _In(pre0, "const")],
                      [_Out("row", (lp, d), BF16)])
    qkv_raw = _mm(hn0, w_qkv, "nn", "gdn_in_qkv")
    z0 = _mm(hn0, w_z0, "nn", "gdn_in_z")
    b_raw = _mm(hn0, w_b, "nn", "gdn_in_b")
    a_raw = _mm(hn0, w_a, "nn", "gdn_in_a")

    def f_ba(ridx, g, b, a, alog, dtb):
        tr = b.shape[0]
        ok = valid_rows(ridx).astype(F32)
        beta = jax.nn.sigmoid(b) * ok
        gate = -jnp.exp(alog) * _softplus(a + dtb) * ok
        ii = lax.broadcasted_iota(jnp.int32, (tr, tr), 0)
        jj = lax.broadcasted_iota(jnp.int32, (tr, tr), 1)
        tri = jnp.logical_and((ii >> 6) == (jj >> 6), ii >= jj).astype(F32)
        return beta, _hdot(tri, gate)

    ba_ins = [_In(b_raw), _In(a_raw), _In(a_log, "const"), _In(dt_bias, "const")]
    beta, gc = _rowwise("gdn_gates", f_ba, ba_ins, [_Out("row", (lp, LANE)), _Out("row", (lp, LANE))])
    qkv = _conv_fwd(qkv_raw, conv_w, "gdn_conv")
    o0, ckpt = _gdn_fwd(qkv, beta, gc, "gdn_scan")

    def per_head(fn, *arrs):
        n = arrs[0].shape[1] // LANE
        return jnp.concatenate([fn(*[a[:, i * LANE:(i + 1) * LANE] for a in arrs]) for i in range(n)], axis=1)

    def f_gate0(ridx, g, o, z, gain):
        return (per_head(lambda oh, zh: _rms(oh, gain) * _silu(zh), o, z),)

    gate0_ins = [_In(o0), _In(z0), _In(gdn_out_norm, "const")]
    (gated0,) = _rowwise("gdn_gate", f_gate0, gate0_ins, [_Out("row", (lp, GDN_V_W), BF16)])
    y0 = _mm(gated0, w_out0, "nn", "gdn_out")

    def f_mid(ridx, g, h, y, g_post, g_pre, g_kv):
        h1 = h + _rms(y, g_post)
        return h1, _rms(h1, g_pre), _rms(h1, g_kv)

    mid_ins = [_In(h0), _In(y0), _In(post0, "const"), _In(pre1, "const"), _In(kvn, "const")]
    h1, hn1, hkv = _rowwise("mid", f_mid, mid_ins,
                            [_Out("row", (lp, d)), _Out("row", (lp, d), BF16), _Out("row", (lp, d), BF16)])

    ckv_raw = _mm(hkv, w_ckv, "nn", "kv_down_c")
    kr_raw = _mm(hkv, w_kr, "nn", "kv_down_r")

    def f_ckv(ridx, g, c, r, cs, sn, gain):
        return _rms(c, gain), _rope(r, cs, sn)

    ckv_ins = [_In(ckv_raw), _In(kr_raw), _In(cos_t), _In(sin_t), _In(kvl, "const")]
    ckv, kr = _rowwise("kv_latent", f_ckv, ckv_ins, [_Out("row", (lp, LANE)), _Out("row", (lp, LANE), BF16)],
                       tr=640)
    kn = _mm(ckv, w_kn, "nn", "kv_up_k", BF16)
    vv = _mm(ckv, w_v, "nn", "kv_up_v", BF16)
    cq_raw = _mm(hn1, w_cq, "nn", "mla_in_q")
    z1 = _mm(hn1, w_z1, "nn", "mla_in_z")

    def f_cq(ridx, g, c, gain):
        return (_rms(c, gain),)

    cq_ins = [_In(cq_raw), _In(qln, "const")]
    (cq,) = _rowwise("q_latent", f_cq, cq_ins, [_Out("row", (lp, MLA_Q_RANK))], tr=640)
    qn = _mm(cq, w_qn, "nn", "q_up_n", BF16)
    qr_raw = _mm(cq, w_qr, "nn", "q_up_r")

    def f_qrope(ridx, g, r, cs, sn):
        return (per_head(lambda rh: _rope(rh, cs, sn), r),)

    qr_ins = [_In(qr_raw), _In(cos_t), _In(sin_t)]
    (qr,) = _rowwise("q_rope", f_qrope, qr_ins, [_Out("row", (lp, MLA_HEADS * LANE), BF16)])
    o1, lse = _flash_fwd(qn, qr, kn, kr, vv, "attention")

    def f_gate1(ridx, g, o, z):
        return (o * _silu(z),)

    gate1_ins = [_In(o1), _In(z1)]
    (og,) = _rowwise("mla_gate", f_gate1, gate1_ins, [_Out("row", (lp, MLA_HEADS * LANE), BF16)])
    y1 = _mm(og, w_out1, "nn", "mla_out")

    def f_final(ridx, g, h, y, t, gain):
        ok = jnp.logical_and(ridx >= ROW0, ridx < ROW0 + seq).astype(F32)

        def rows_loss(h_, y_, gain_):
            err = (h_ + _rms(y_, gain_) - t) * ok
            return 0.5 * jnp.sum(jnp.sum(err * err, axis=1, keepdims=True), axis=0, keepdims=True) / d

        val, vjp = jax.vjp(rows_loss, h, y, gain)
        dh, dy, dgain = vjp(jnp.ones((1, 1), F32))
        return dh, dy, dgain, jnp.broadcast_to(val, (1, LANE))

    dh2, dy1, dpost1, loss_part = _rowwise(
        "loss_head", f_final, [_In(h1), _In(y1), _In(tgt), _In(post1, "const")],
        [_Out("row", (lp, d)), _Out("row", (lp, d)), _Out("acc", (1, d)), _Out("acc", (1, LANE))])

    dog = _mm(dy1, w_out1, "nt", "mla_out_dx")
    dw_out1 = _mm(og, dy1, "tn", "mla_out_dw")
    do1, dz1 = _rowwise_vjp("mla_gate_bwd", f_gate1, gate1_ins, [[dog]], [0, 1])
    dqn, dqr, dkn, dkr, dvv = _flash_bwd(qn, qr, kn, kr, vv, o1, do1, lse, "attention_bwd")
    (dqr_raw,) = _rowwise_vjp("q_rope_bwd", f_qrope, qr_ins, [[dqr]], [0])
    dcq_a = _mm(dqn, w_qn, "nt", "q_up_n_dx")
    dcq_b = _mm(dqr_raw, w_qr, "nt", "q_up_r_dx")
    dw_qn = _mm(cq, dqn, "tn", "q_up_n_dw") * Q_PRESCALE
    dw_qr = _mm(cq, dqr_raw, "tn", "q_up_r_dw") * Q_PRESCALE
    dcq_raw, dqln = _rowwise_vjp("q_latent_bwd", f_cq, cq_ins, [[dcq_a, dcq_b]], [0, 1], tr=640)
    dhn1_a = _mm(dcq_raw, w_cq, "nt", "mla_in_q_dx")
    dhn1_b = _mm(dz1, w_z1, "nt", "mla_in_z_dx")
    dw_cq = _mm(hn1, dcq_raw, "tn", "mla_in_q_dw")
    dw_z1 = _mm(hn1, dz1, "tn", "mla_in_z_dw")
    dckv_a = _mm(dkn, w_kn, "nt", "kv_up_k_dx")
    dckv_b = _mm(dvv, w_v, "nt", "kv_up_v_dx")
    dw_kn = _mm(ckv, dkn, "tn", "kv_up_k_dw")
    dw_v = _mm(ckv, dvv, "tn", "kv_up_v_dw")
    dckv_raw, dkr_raw, dkvl = _rowwise_vjp("kv_latent_bwd", f_ckv, ckv_ins, [[dckv_a, dckv_b], [dkr]], [0, 1, 4],
                                           tr=640)
    dhkv_a = _mm(dckv_raw, w_ckv, "nt", "kv_down_c_dx")
    dhkv_b = _mm(dkr_raw, w_kr, "nt", "kv_down_r_dx")
    dw_ckv = _mm(hkv, dckv_raw, "tn", "kv_down_c_dw")
    dw_kr = _mm(hkv, dkr_raw, "tn", "kv_down_r_dw")
    dh0_res, dy0, dpost0, dpre1, dkvn = _rowwise_vjp(
        "mid_bwd", f_mid, mid_ins, [[dh2], [dhn1_a, dhn1_b], [dhkv_a, dhkv_b]], [0, 1, 2, 3, 4])

    dgated0 = _mm(dy0, w_out0, "nt", "gdn_out_dx")
    dw_out0 = _mm(gated0, dy0, "tn", "gdn_out_dw")
    do0, dz0, doutn = _rowwise_vjp("gdn_gate_bwd", f_gate0, gate0_ins, [[dgated0]], [0, 1, 2], tr=160)
    dq0, dk0, dv0, dbeta, dgc = _gdn_bwd(qkv, beta, gc, ckpt, do0, "gdn_scan_bwd")
    db_raw, da_raw, dalog, ddtb = _rowwise_vjp("gdn_gates_bwd", f_ba, ba_ins, [[dbeta], [dgc]], [0, 1, 2, 3])
    dqkv_raw, dconv = _conv_bwd(qkv_raw, conv_w, dq0, dk0, dv0, "gdn_conv_bwd")
    dhn0_a = _mm(dqkv_raw, w_qkv, "nt", "gdn_in_qkv_dx")
    dhn0_b = _mm(dz0, w_z0, "nt", "gdn_in_z_dx")
    dhn0_c = _mm(db_raw, w_b, "nt", "gdn_in_b_dx")
    dhn0_d = _mm(da_raw, w_a, "nt", "gdn_in_a_dx")
    dw_qkv = _mm(hn0, dqkv_raw, "tn", "gdn_in_qkv_dw")
    dw_z0 = _mm(hn0, dz0, "tn", "gdn_in_z_dw")
    dw_b = _mm(hn0, db_raw, "tn", "gdn_in_b_dw")
    dw_a = _mm(hn0, da_raw, "tn", "gdn_in_a_dw")
    dh0, dpre0 = _rowwise_vjp("pre0_bwd", f_pre0, [_In(h0), _In(pre0, "const")],
                              [[dhn0_a, dhn0_b, dhn0_c, dhn0_d], [dh0_res]], [0, 1])

    grad_x = dh0[ROW0:ROW0 + seq][None]
    g_meta = dh0[FRONT:ROW0]
    g_w_in0 = jnp.concatenate([dw_qkv, dw_z0, dw_b[:, :nv], dw_a[:, :nv]], axis=1)
    g_kv_down = jnp.concatenate([dw_ckv, dw_kr[:, :MLA_ROPE]], axis=1)
    g_kv_up = jnp.concatenate([dw_kn.reshape(MLA_KV_RANK, MLA_HEADS, LANE), dw_v.reshape(MLA_KV_RANK, MLA_HEADS, LANE)],
                              axis=2).reshape(MLA_KV_RANK, MLA_HEADS * 2 * LANE)
    g_w_in1 = jnp.concatenate([dw_cq, dw_z1], axis=1)
    g_qup = jnp.concatenate([dw_qn.reshape(MLA_Q_RANK, MLA_HEADS, LANE),
                             dw_qr.reshape(MLA_Q_RANK, MLA_HEADS, LANE)[:, :, :MLA_ROPE]],
                            axis=2).reshape(MLA_Q_RANK, MLA_HEADS * MLA_QK)
    big_g = [g_meta, dconv, dw_out0, g_kv_down, g_kv_up, g_w_in1, g_qup, dw_out1]

    def reduce_to_shard(g_by_chip, tag):
        got = _sibling_split(g_by_chip, "grads_sibling_split" + tag)
        chip_part = _add_pair(g_by_chip, got, "grads_chip_sum" + tag)
        from_chips = _chip_scatter(chip_part, "grads_chip_scatter" + tag)
        half_sum = _sum_slots(from_chips, "grads_total" + tag)
        return _sibling_join(half_sum, "grads_sibling_join" + tag)

    g_flat = reduce_to_shard(_pack_full(big_g, layout), "")
    g_win = reduce_to_shard(jnp.stack([g_w_in0[:, s * win_cols:(s + 1) * win_cols] for s in range(4)]), "_gdn_w_in")

    small_shapes = [(2, d), (2, d), (1, nv), (1, nv), (1, GDN_DK), (d,), (MLA_KV_RANK,), (1, MLA_Q_RANK), (1, LANE)]
    small_part = _pack_small([jnp.concatenate([dpre0, dpre1], axis=0), jnp.concatenate([dpost0, dpost1], axis=0),
                              dalog[:, :nv], ddtb[:, :nv], doutn, dkvn, dkvl, dqln, loss_part])
    small_tot = _all_sum_small(small_part, "small_sum")
    small_g = _unpack_small(small_tot, small_shapes)
    loss = small_g[-1][0, 0]

    d_flat, m_flat, v_flat = _adamw(w_flat, g_flat, _pack_shards(big_m), _pack_shards(big_v), "adamw_sharded")
    win_step = _adamw(gdn_w_in[0], g_win, m_gdn_w_in[0], v_gdn_w_in[0], "adamw_gdn_w_in")
    small_w = [pre_norm, post_norm, gdn_a_log, gdn_dt_bias, gdn_out_norm, kv_norm, kv_latent_norm, mla_q_latent_norm]
    small_m = [m_pre_norm, m_post_norm, m_gdn_a_log, m_gdn_dt_bias, m_gdn_out_norm, m_kv_norm, m_kv_latent_norm,
               m_mla_q_latent_norm]
    small_v = [v_pre_norm, v_post_norm, v_gdn_a_log, v_gdn_dt_bias, v_gdn_out_norm, v_kv_norm, v_kv_latent_norm,
               v_mla_q_latent_norm]
    g_small_flat = _pack_small(small_g[:-1])
    ds_flat, ms_flat, vs_flat = _adamw(_pack_small(small_w), g_small_flat, _pack_small(small_m), _pack_small(small_v),
                                       "adamw_replicated")

    def assemble(big_flat, small_flat, win):
        bigs = dict(zip(big_names, [a.reshape(w.shape) for a, w in zip(
            _unpack_shards(big_flat, layout),
            [meta_tokens, gdn_conv_w, gdn_w_out, kv_w_down, kv_w_up, mla_w_in, mla_w_q_up, mla_w_out])]))
        smalls = dict(zip(["pre_norm", "post_norm", "gdn_a_log", "gdn_dt_bias", "gdn_out_norm", "kv_norm",
                           "kv_latent_norm", "mla_q_latent_norm"], _unpack_small(small_flat, small_shapes[:-1])))
        both = {**bigs, **smalls, "gdn_w_in": win[None]}
        order = ["meta_tokens", "pre_norm", "post_norm", "gdn_w_in", "gdn_conv_w", "gdn_a_log", "gdn_dt_bias",
                 "gdn_out_norm", "gdn_w_out", "kv_norm", "kv_w_down", "kv_latent_norm", "kv_w_up", "mla_w_in",
                 "mla_q_latent_norm", "mla_w_q_up", "mla_w_out"]
        return [both[n] for n in order]

    grads = assemble(g_flat, g_small_flat, g_win)
    deltas = assemble(d_flat, ds_flat, win_step[0])
    new_m = assemble(m_flat, ms_flat, win_step[1])
    new_v = assemble(v_flat, vs_flat, win_step[2])
    return (loss, grad_x, *grads, *deltas, *new_m, *new_v)
```

```python
import functools
import math

import jax
import jax.numpy as jnp
from jax import lax
from jax.experimental import pallas as pl
from jax.experimental.pallas import tpu as pltpu

F32 = jnp.float32
BF16 = jnp.bfloat16
MESH = pl.DeviceIdType.MESH

D_MODEL = 1024
N_META = 16
FRONT = 48
ROW0 = FRONT + N_META
ROW_ALIGN = 640
NORM_EPS = 1e-6
LANE = 128

GDN_QK_HEADS = 8
GDN_V_HEADS = 16
GDN_DK = 128
GDN_CHUNK = 64
GDN_QK_W = 1024
GDN_V_W = 2048
GDN_CONV_W = 4096

MLA_HEADS = 16
MLA_NOPE = 128
MLA_ROPE = 64
MLA_QK = 192
MLA_Q_RANK = 256
MLA_KV_RANK = 128
ROPE_THETA = 10000.0

ADAM_LR = 0.001
ADAM_B1 = 0.9
ADAM_B2 = 0.999
ADAM_EPS = 1e-08
ADAM_WD = 0.01
ADAM_STEP = 10

VMEM_LIMIT_V7X = 56 * 1024 * 1024
NEG = -1e30

_NN = ((1,), (0,))
_NT = ((1,), (1,))
_TN = ((0,), (0,))
_HI = lax.Precision.HIGHEST
_X3 = lax.Precision.HIGH


def _pcall(body, **kw):
    return pl.pallas_call(body, **kw)


def _params(n_axes):
    return pltpu.CompilerParams(dimension_semantics=("arbitrary",) * n_axes, vmem_limit_bytes=VMEM_LIMIT_V7X)


def _dot(a, b, dims, prec=None):
    return lax.dot_general(a, b, (dims, ((), ())), precision=prec, preferred_element_type=F32)


def _bdot(a, b, dims):
    return _dot(a.astype(BF16), b.astype(BF16), dims)


def _hdot(a, b, dims=_NN):
    return _dot(a, b, dims, _HI)


def _fdot(a, b, dims):
    return _dot(a, b, dims)


def _tile(n):
    if n % ROW_ALIGN == 0:
        return ROW_ALIGN
    for t in (1024, 512, 256, 128):
        if n % t == 0:
            return t
    raise ValueError(n)


def _mm(a, b, mode, name, out_dtype=F32):
    if mode == "nn":
        (m, k), (k2, n) = a.shape, b.shape
    elif mode == "nt":
        (m, k), (n, k2) = a.shape, b.shape
    else:
        (k, m), (k2, n) = a.shape, b.shape
    assert k == k2, (a.shape, b.shape, mode)
    tm, tn, tk = _tile(m), _tile(n), _tile(k)
    nk = k // tk
    dims = {"nn": _NN, "nt": _NT, "tn": _TN}[mode]

    def body(a_ref, b_ref, o_ref, acc):
        kk = pl.program_id(2)

        @pl.when(kk == 0)
        def _():
            acc[...] = jnp.zeros_like(acc)

        acc[...] += _bdot(a_ref[...], b_ref[...], dims)

        @pl.when(kk == nk - 1)
        def _():
            o_ref[...] = acc[...].astype(out_dtype)

    if mode == "tn":
        a_spec = pl.BlockSpec((tk, tm), lambda i, j, kk: (kk, i))
    else:
        a_spec = pl.BlockSpec((tm, tk), lambda i, j, kk: (i, kk))
    if mode == "nt":
        b_spec = pl.BlockSpec((tn, tk), lambda i, j, kk: (j, kk))
    else:
        b_spec = pl.BlockSpec((tk, tn), lambda i, j, kk: (kk, j))
    return _pcall(
        body, name=name, grid=(m // tm, n // tn, nk),
        in_specs=[a_spec, b_spec],
        out_specs=pl.BlockSpec((tm, tn), lambda i, j, kk: (i, j)),
        out_shape=jax.ShapeDtypeStruct((m, n), out_dtype),
        scratch_shapes=[pltpu.VMEM((tm, tn), F32)],
        compiler_params=_params(3),
    )(a, b)


class _In:
    def __init__(self, arr, kind="row", grouped=False, goff=0):
        self.arr, self.kind, self.grouped, self.goff = arr, kind, grouped, goff


class _Out:
    def __init__(self, kind, shape, dtype=F32, grouped=False):
        self.kind, self.shape, self.dtype, self.grouped = kind, shape, dtype, grouped


def _rowwise(name, fn, ins, outs, *, groups=1, tr=320):
    lp = next(i.arr.shape[0] for i in ins if i.kind == "row")
    nr = lp // tr
    assert lp % tr == 0

    def in_spec(i):
        w = i.arr.shape[1]
        if i.kind == "row":
            if i.grouped:
                return pl.BlockSpec((tr, LANE), lambda g, r, o=i.goff: (r, g + o))
            return pl.BlockSpec((tr, w), lambda g, r: (r, 0))
        if i.grouped:
            return pl.BlockSpec((i.arr.shape[0], LANE), lambda g, r, o=i.goff: (0, g + o))
        return pl.BlockSpec(i.arr.shape, lambda g, r: (0, 0))

    def out_spec(o):
        if o.kind == "row":
            if o.grouped:
                return pl.BlockSpec((tr, LANE), lambda g, r: (r, g))
            assert groups == 1
            return pl.BlockSpec((tr, o.shape[1]), lambda g, r: (r, 0))
        if o.grouped:
            return pl.BlockSpec((o.shape[0], LANE), lambda g, r: (0, g))
        return pl.BlockSpec(o.shape, lambda g, r: (0, 0))

    n_in = len(ins)

    def body(*refs):
        g = pl.program_id(0)
        r = pl.program_id(1)
        ridx = r * tr + lax.broadcasted_iota(jnp.int32, (tr, 1), 0)
        res = fn(ridx, g, *[ref[...] for ref in refs[:n_in]])
        assert len(res) == len(outs), (name, len(res), len(outs))
        for o, ref, val in zip(outs, refs[n_in:], res):
            if o.kind == "row":
                ref[...] = val.astype(o.dtype)
            else:
                first = (r == 0) if o.grouped else jnp.logical_and(r == 0, g == 0)

                @pl.when(first)
                def _(ref=ref, val=val):
                    ref[...] = val.astype(F32)

                @pl.when(jnp.logical_not(first))
                def _(ref=ref, val=val):
                    ref[...] += val.astype(F32)

    res = _pcall(
        body, name=name, grid=(groups, nr),
        in_specs=[in_spec(i) for i in ins],
        out_specs=[out_spec(o) for o in outs],
        out_shape=[jax.ShapeDtypeStruct(o.shape, o.dtype) for o in outs],
        compiler_params=_params(2),
    )(*[i.arr for i in ins])
    return res


def _rowwise_vjp(name, fn, ins, cots, diff, *, groups=1, tr=320):
    n_in = len(ins)
    grouped = groups > 1
    cot_ins = []
    counts = []
    for arrs in cots:
        counts.append(len(arrs))
        for a in arrs:
            cot_ins.append(_In(a, "row", grouped=grouped and a.shape[1] > LANE))
    lp = next(i.arr.shape[0] for i in ins if i.kind == "row")
    outs = []
    for d in diff:
        i = ins[d]
        if i.kind == "row":
            w = groups * LANE if i.grouped else i.arr.shape[1]
            outs.append(_Out("row", (lp, w), F32, grouped=i.grouped))
        else:
            outs.append(_Out("acc", i.arr.shape, F32, grouped=i.grouped))

    def bfn(ridx, g, *allvals):
        vals = list(allvals[:n_in])
        cvals = allvals[n_in:]

        def f(*dv):
            full = list(vals)
            for i, v in zip(diff, dv):
                full[i] = v
            return tuple(fn(ridx, g, *full))

        primal, vjp = jax.vjp(f, *[vals[i].astype(F32) for i in diff])
        cts = []
        pos = 0
        for k, cnt in enumerate(counts):
            if cnt == 0:
                cts.append(jnp.zeros_like(primal[k]))
            else:
                c = cvals[pos].astype(F32)
                for extra in cvals[pos + 1:pos + cnt]:
                    c = c + extra.astype(F32)
                w = primal[k].shape[1]
                if c.shape[1] != w:
                    c = functools.reduce(jnp.add, [c[:, i * w:(i + 1) * w] for i in range(c.shape[1] // w)])
                cts.append(c.astype(primal[k].dtype))
            pos += cnt
        return vjp(tuple(cts))

    return _rowwise(name, bfn, list(ins) + cot_ins, outs, groups=groups, tr=tr)


def _rms(x, g):
    return x * lax.rsqrt(jnp.mean(x * x, axis=-1, keepdims=True) + NORM_EPS) * g


def _silu(x):
    return x * jax.nn.sigmoid(x)


def _softplus(x):
    return jnp.maximum(x, 0.0) + jnp.log(1.0 + jnp.exp(-jnp.abs(x)))


def _swap_halves(x):
    lane = lax.broadcasted_iota(jnp.int32, x.shape, x.ndim - 1)
    return jnp.where(lane < 32, pltpu.roll(x, LANE - 32, x.ndim - 1), pltpu.roll(x, 32, x.ndim - 1))


@jax.custom_vjp
def _rope(x, c, s):
    return x * c + _swap_halves(x) * s


def _rope_fwd(x, c, s):
    return _rope(x, c, s), (c, s)


def _rope_bwd(res, dy):
    c, s = res
    return dy * c + _swap_halves(dy * s), jnp.zeros_like(c), jnp.zeros_like(s)


_rope.defvjp(_rope_fwd, _rope_bwd)


def _conv_post(c, g):
    s = _silu(c)
    n = s * lax.rsqrt(jnp.sum(s * s, axis=-1, keepdims=True) + NORM_EPS)
    return jnp.where(g < GDN_QK_HEADS, n * (GDN_DK ** -0.5), jnp.where(g < 2 * GDN_QK_HEADS, n, s))


def _conv_taps(xe, w):
    c = xe[8:] * w[3]
    for s in (1, 2, 3):
        c = c + pltpu.roll(xe, s, 0)[8:] * w[3 - s]
    return c


CONV_LANES = 512
CONV_HEADS = CONV_LANES // LANE


def _conv_post_block(c, g):
    return jnp.concatenate([_conv_post(c[:, i * LANE:(i + 1) * LANE], g * CONV_HEADS + i)
                            for i in range(CONV_HEADS)], axis=1)


def _conv_fwd(x, w, name, tr=640):
    lp, width = x.shape
    cl = CONV_LANES
    nr = lp // tr

    def body(x_ref, prev_ref, w_ref, o_ref):
        g = pl.program_id(0)
        r = pl.program_id(1)
        prev = jnp.where(r > 0, prev_ref[...], 0.0)
        xe = jnp.concatenate([prev, x_ref[...]], axis=0)
        o_ref[...] = _conv_post_block(_conv_taps(xe, [w_ref[t:t + 1, :] for t in range(4)]), g)

    return _pcall(
        body, name=name, grid=(width // cl, nr),
        in_specs=[pl.BlockSpec((tr, cl), lambda g, r: (r, g)),
                  pl.BlockSpec((8, cl), lambda g, r: (jnp.maximum(r * (tr // 8) - 1, 0), g)),
                  pl.BlockSpec((4, cl), lambda g, r: (0, g))],
        out_specs=pl.BlockSpec((tr, cl), lambda g, r: (r, g)),
        out_shape=jax.ShapeDtypeStruct((lp, width), F32),
        compiler_params=_params(2),
    )(x, x, w)


def _conv_bwd(x, w, dq, dk, dv, name, tr=640):
    lp, width = x.shape
    cl = CONV_LANES
    nr = lp // tr
    last8 = lp // 8 - 1
    nq = GDN_QK_W // cl

    def body(x_ref, prev_ref, next_ref, w_ref, q_ref, k_ref, v_ref, q_n, k_n, v_n, dx_ref, dw_ref):
        g = pl.program_id(0)
        r = pl.program_id(1)
        w = [w_ref[t:t + 1, :] for t in range(4)]
        not_last = r < nr - 1

        def pick(a, b, c):
            return jnp.where(g < nq, a[...], jnp.where(g < 2 * nq, b[...], c[...]))

        dy = pick(q_ref, k_ref, v_ref)
        dyn = jnp.where(not_last, pick(q_n, k_n, v_n), 0.0)
        prev = jnp.where(r > 0, prev_ref[...], 0.0)
        nxt = jnp.where(not_last, next_ref[...], 0.0)
        xe = jnp.concatenate([prev, x_ref[...], nxt], axis=0)
        ce = _conv_taps(xe, w)
        _, vjp = jax.vjp(lambda c: _conv_post_block(c, g), ce)
        (dce,) = vjp(jnp.concatenate([dy, dyn], axis=0))
        n = tr + 8
        dx = dce * w[3]
        for s in (1, 2, 3):
            dx = dx + pltpu.roll(dce, n - s, 0) * w[3 - s]
        dx_ref[...] = dx[:tr]
        dc = dce[:tr]
        row4 = lax.broadcasted_iota(jnp.int32, (4, cl), 0)
        dw = jnp.zeros((4, cl), F32)
        for s in (0, 1, 2, 3):
            xs = xe[8:8 + tr] if s == 0 else pltpu.roll(xe, s, 0)[8:8 + tr]
            dw = dw + jnp.where(row4 == 3 - s, jnp.sum(dc * xs, axis=0, keepdims=True), 0.0)

        @pl.when(r == 0)
        def _():
            dw_ref[...] = dw

        @pl.when(r > 0)
        def _():
            dw_ref[...] += dw

    def col_q(g):
        return jnp.minimum(g, nq - 1)

    def col_k(g):
        return jnp.clip(g - nq, 0, nq - 1)

    def col_v(g):
        return jnp.maximum(g - 2 * nq, 0)

    def blk(colf):
        return pl.BlockSpec((tr, cl), lambda g, r: (r, colf(g)))

    def nblk(colf):
        return pl.BlockSpec((8, cl), lambda g, r: (jnp.minimum((r + 1) * (tr // 8), last8), colf(g)))

    return _pcall(
        body, name=name, grid=(width // cl, nr),
        in_specs=[pl.BlockSpec((tr, cl), lambda g, r: (r, g)),
                  pl.BlockSpec((8, cl), lambda g, r: (jnp.maximum(r * (tr // 8) - 1, 0), g)),
                  pl.BlockSpec((8, cl), lambda g, r: (jnp.minimum((r + 1) * (tr // 8), last8), g)),
                  pl.BlockSpec((4, cl), lambda g, r: (0, g)),
                  blk(col_q), blk(col_k), blk(col_v), nblk(col_q), nblk(col_k), nblk(col_v)],
        out_specs=[pl.BlockSpec((tr, cl), lambda g, r: (r, g)),
                   pl.BlockSpec((4, cl), lambda g, r: (0, g))],
        out_shape=[jax.ShapeDtypeStruct((lp, width), F32), jax.ShapeDtypeStruct((4, width), F32)],
        compiler_params=_params(2),
    )(x, x, x, w, dq, dk, dv, dq, dk, dv)


def _bmm(a, b, dims, prec=None):
    (ca,), (cb,) = dims
    return lax.dot_general(a, b, (((ca + 1,), (cb + 1,)), ((0,), (0,))), precision=prec,
                           preferred_element_type=F32)


def _inv_impl(m):
    c = m.shape[-1]
    ii = lax.broadcasted_iota(jnp.int32, (c, c), 0)
    jj = lax.broadcasted_iota(jnp.int32, (c, c), 1)
    eye = (ii == jj).astype(F32)

    def same_block(shift):
        return (ii >> shift) == (jj >> shift)

    n1 = jnp.where(same_block(3), -m, 0.0)
    n2 = _bmm(n1, n1, _NN, _X3)
    n4 = _bmm(n2, n2, _NN, _X3)
    d = _bmm(_bmm(eye + n1, eye + n2, _NN, _X3), eye + n4, _NN, _X3)
    shift = 3
    while (1 << shift) < c:
        low = jnp.where(jnp.logical_and(same_block(shift + 1), jnp.logical_not(same_block(shift))), m, 0.0)
        d = d - _bmm(d, _bmm(low, d, _NN, _X3), _NN, _X3)
        shift += 1
    return d


@jax.custom_vjp
def _inv_unit_lower(m):
    return _inv_impl(m)


def _inv_f(m):
    t = _inv_impl(m)
    return t, t


def _inv_b(t, dt):
    c = t.shape[-1]
    ii = lax.broadcasted_iota(jnp.int32, (c, c), 0)
    jj = lax.broadcasted_iota(jnp.int32, (c, c), 1)
    gm = _bmm(t, _bmm(dt, t, _NT, _X3), _TN, _X3)
    return (jnp.where(ii > jj, -gm, 0.0),)


_inv_unit_lower.defvjp(_inv_f, _inv_b)


GDN_HEADS_PER_STEP = 16


def _gdn_group(q, k, v, beta_blk, gc_blk, states, h0):
    hp = GDN_HEADS_PER_STEP
    c = q.shape[0]
    lane = lax.broadcasted_iota(jnp.int32, (1, LANE), 1)
    row8 = lax.broadcasted_iota(jnp.int32, (max(8, hp), LANE), 0)
    lane8 = lax.broadcasted_iota(jnp.int32, (max(8, hp), LANE), 1)
    gcr_all = _hdot((lane8 == h0 + row8).astype(F32), gc_blk, _NT)
    betas, gccs = [], []
    for i in range(hp):
        onehot = (lane == h0 + i).astype(F32)
        betas.append(jnp.sum(beta_blk * onehot, axis=1, keepdims=True))
        gccs.append(jnp.sum(gc_blk * onehot, axis=1, keepdims=True))
    beta = jnp.stack(betas)
    gcc = jnp.stack(gccs)
    gcr = jnp.stack([gcr_all[i:i + 1] for i in range(hp)])
    qh = jnp.stack([q[:, (i // 2) * LANE:(i // 2 + 1) * LANE] for i in range(hp)])
    kh = jnp.stack([k[:, (i // 2) * LANE:(i // 2 + 1) * LANE] for i in range(hp)])
    vh = jnp.stack([v[:, i * LANE:(i + 1) * LANE] for i in range(hp)])
    state = jnp.stack(states)
    ii = lax.broadcasted_iota(jnp.int32, (c, c), 0)
    jj = lax.broadcasted_iota(jnp.int32, (c, c), 1)
    incl = ii >= jj
    dec = jnp.where(incl, jnp.exp(jnp.where(incl, gcc - gcr, 0.0)), 0.0)
    eg = jnp.exp(gcc)
    m = _bmm(kh, kh, _NT) * beta * jnp.where(ii > jj, dec, 0.0)
    t = _inv_unit_lower(m)
    u = _bmm(t, vh * beta, _NN, _X3)
    w = _bmm(t, kh * (beta * eg), _NN, _X3)
    attn = _bmm(qh, kh, _NT) * dec
    rows = lax.broadcasted_iota(jnp.int32, (c, 1), 0)
    gl = jnp.sum(jnp.where(rows == c - 1, gcc, 0.0), axis=1, keepdims=True)
    v_new = u - _bmm(w, state, _NN)
    o = _bmm(qh * eg, state, _NN) + _bmm(attn, v_new, _NN)
    new_state = state * jnp.exp(gl) + _bmm(kh * jnp.exp(gl - gcc), v_new, _TN)
    return jnp.concatenate([o[i] for i in range(hp)], axis=1), tuple(new_state[i] for i in range(hp))


def _gdn_specs(nc, rev):
    def cidx(n):
        return (nc - 1 - n) if rev else n
    hp = GDN_HEADS_PER_STEP
    nqk = GDN_QK_HEADS
    c = GDN_CHUNK
    nq = 2 * nqk // hp
    q_spec = pl.BlockSpec((c, hp // 2 * LANE), lambda n, g: (cidx(n), g))
    k_spec = pl.BlockSpec((c, hp // 2 * LANE), lambda n, g: (cidx(n), nq + g))
    v_spec = pl.BlockSpec((c, hp * LANE), lambda n, g: (cidx(n), nq + g))
    s_spec = pl.BlockSpec((c, LANE), lambda n, g: (cidx(n), 0))
    o_spec = pl.BlockSpec((c, hp * LANE), lambda n, g: (cidx(n), g))
    ck_spec = pl.BlockSpec((hp, 1, GDN_DK, LANE), lambda n, g: (g, cidx(n), 0, 0))
    return q_spec, k_spec, v_spec, s_spec, o_spec, ck_spec


def _gdn_fwd(qkv, beta, gc, name):
    lp = qkv.shape[0]
    nc = lp // GDN_CHUNK
    nh = GDN_V_HEADS
    hp = GDN_HEADS_PER_STEP
    q_spec, k_spec, v_spec, s_spec, o_spec, ck_spec = _gdn_specs(nc, False)

    def body(q_ref, k_ref, v_ref, b_ref, g_ref, o_ref, ck_ref, state):
        n = pl.program_id(0)
        g = pl.program_id(1)

        @pl.when(n == 0)
        def _():
            for i in range(hp):
                state[g * hp + i] = jnp.zeros((GDN_DK, LANE), F32)

        states = tuple(state[g * hp + i] for i in range(hp))
        for i in range(hp):
            ck_ref[i, 0] = states[i]
        o, new_states = _gdn_group(q_ref[...], k_ref[...], v_ref[...], b_ref[...], g_ref[...], states, g * hp)
        o_ref[...] = o
        for i in range(hp):
            state[g * hp + i] = new_states[i]

    return _pcall(
        body, name=name, grid=(nc, nh // hp),
        in_specs=[q_spec, k_spec, v_spec, s_spec, s_spec],
        out_specs=[o_spec, ck_spec],
        out_shape=[jax.ShapeDtypeStruct((lp, GDN_V_W), F32),
                   jax.ShapeDtypeStruct((nh, nc, GDN_DK, LANE), F32)],
        scratch_shapes=[pltpu.VMEM((nh, GDN_DK, LANE), F32)],
        compiler_params=_params(2),
    )(qkv, qkv, qkv, beta, gc)


def _gdn_bwd(qkv, beta, gc, ckpt, do, name):
    lp = qkv.shape[0]
    nc = lp // GDN_CHUNK
    nh = GDN_V_HEADS
    hp = GDN_HEADS_PER_STEP
    q_spec, k_spec, v_spec, s_spec, o_spec, ck_spec = _gdn_specs(nc, True)

    def body(q_ref, k_ref, v_ref, b_ref, g_ref, ck_ref, do_ref,
             dq_ref, dk_ref, dv_ref, db_ref, dg_ref, dstate):
        n = pl.program_id(0)
        g = pl.program_id(1)

        @pl.when(n == 0)
        def _():
            for i in range(hp):
                dstate[g * hp + i] = jnp.zeros((GDN_DK, LANE), F32)

        states = tuple(ck_ref[i, 0] for i in range(hp))
        _, vjp = jax.vjp(lambda q, k, v, b, gg, s: _gdn_group(q, k, v, b, gg, s, g * hp),
                         q_ref[...], k_ref[...], v_ref[...], b_ref[...], g_ref[...], states)
        dq, dk, dv, db, dg, ds = vjp((do_ref[...], tuple(dstate[g * hp + i] for i in range(hp))))
        dq_ref[...] = dq
        dk_ref[...] = dk
        dv_ref[...] = dv
        for i in range(hp):
            dstate[g * hp + i] = ds[i]

        @pl.when(g == 0)
        def _():
            db_ref[...] = db
            dg_ref[...] = dg

        @pl.when(g > 0)
        def _():
            db_ref[...] += db
            dg_ref[...] += dg

    qk_shape = jax.ShapeDtypeStruct((lp, GDN_QK_W), F32)
    big = jax.ShapeDtypeStruct((lp, GDN_V_W), F32)
    small = jax.ShapeDtypeStruct((lp, LANE), F32)
    dq_spec = pl.BlockSpec((GDN_CHUNK, hp // 2 * LANE), lambda n, g: (nc - 1 - n, g))
    return _pcall(
        body, name=name, grid=(nc, nh // hp),
        in_specs=[q_spec, k_spec, v_spec, s_spec, s_spec, ck_spec, o_spec],
        out_specs=[dq_spec, dq_spec, o_spec, s_spec, s_spec],
        out_shape=[qk_shape, qk_shape, big, small, small],
        scratch_shapes=[pltpu.VMEM((nh, GDN_DK, LANE), F32)],
        compiler_params=_params(2),
    )(qkv, qkv, qkv, beta, gc, ckpt, do)


LOG2E = 1.4426950408889634
LN2 = 0.6931471805599453
Q_PRESCALE = MLA_QK ** -0.5 * LOG2E


ATT_SUB = 128
ATT_HEADS_PER_STEP = 2


def _att_mask(i, j, tb, transposed):
    r = lax.broadcasted_iota(jnp.int32, (tb, tb), 0)
    c = lax.broadcasted_iota(jnp.int32, (tb, tb), 1)
    qpos, kpos = (i * tb + c, j * tb + r) if transposed else (i * tb + r, j * tb + c)
    return jnp.logical_and(kpos <= qpos, kpos >= FRONT)


def _causal_pairs(nb, by_key):
    if by_key:
        pairs = [(i, j) for j in range(nb) for i in range(j, nb)]
    else:
        pairs = [(i, j) for i in range(nb) for j in range(i + 1)]
    return jnp.array([p[0] for p in pairs], jnp.int32), jnp.array([p[1] for p in pairs], jnp.int32)


def _masked_and_plain(i, j, step):
    edge = jnp.logical_or(j == i, j == 0)

    @pl.when(jnp.logical_and(edge, j <= i))
    def _():
        step(True)

    @pl.when(jnp.logical_and(jnp.logical_not(edge), j < i))
    def _():
        step(False)


def _cat(a_ref, b_ref):
    return jnp.concatenate([a_ref[...], b_ref[...]], axis=1)


def _flash_fwd(qn, qr, kn, kr, v, name, tb=ROW_ALIGN):
    lp = qn.shape[0]
    nb = lp // tb
    nh = MLA_HEADS
    hp = ATT_HEADS_PER_STEP
    qi, kj = _causal_pairs(nb, by_key=False)

    def body(qi_ref, kj_ref, qn_ref, qr_ref, kn_ref, kr_ref, v_ref, o_ref, lse_ref, m_s, l_s, acc):
        t = pl.program_id(1)
        i, j = qi_ref[t], kj_ref[t]

        @pl.when(j == 0)
        def _():
            m_s[...] = jnp.full_like(m_s, NEG)
            l_s[...] = jnp.zeros_like(l_s)
            acc[...] = jnp.zeros_like(acc)

        def step(masked):
            n_sub = tb // ATT_SUB
            kr = kr_ref[...]
            for e in range(hp):
                lanes = pl.ds(e * LANE, LANE)
                k = jnp.concatenate([kn_ref[:, lanes], kr], axis=1)
                v = v_ref[:, lanes]

                def scores(r, lanes=lanes, k=k):
                    rows = pl.ds(r * ATT_SUB, ATT_SUB)
                    return _dot(jnp.concatenate([qn_ref[rows, lanes], qr_ref[rows, lanes]], axis=1), k, _NT)

                s_next = scores(0)
                for r in range(n_sub):
                    s = s_next
                    if r + 1 < n_sub:
                        s_next = scores(r + 1)
                    rows = pl.ds(r * ATT_SUB, ATT_SUB)
                    if masked:
                        qpos = i * tb + r * ATT_SUB + lax.broadcasted_iota(jnp.int32, (ATT_SUB, tb), 0)
                        kpos = j * tb + lax.broadcasted_iota(jnp.int32, (ATT_SUB, tb), 1)
                        s = jnp.where(jnp.logical_and(kpos <= qpos, kpos >= FRONT), s, NEG)
                    m_old = m_s[e, rows, :]
                    m_new = jnp.maximum(m_old, jnp.max(s, axis=1, keepdims=True))
                    alpha = jnp.exp2(m_old - m_new)
                    p = jnp.exp2(s - m_new)
                    l_s[e, rows, :] = alpha * l_s[e, rows, :] + jnp.sum(p, axis=1, keepdims=True)
                    acc[e, rows, :] = alpha * acc[e, rows, :] + _dot(p.astype(BF16), v, _NN)
                    m_s[e, rows, :] = m_new

        _masked_and_plain(i, j, step)

        @pl.when(j == i)
        def _():
            for e in range(hp):
                lanes = pl.ds(e * LANE, LANE)
                o_ref[:, lanes] = acc[e] / l_s[e]
                lse_ref[:, lanes] = jnp.broadcast_to(m_s[e] + jnp.log(l_s[e]) * LOG2E, (tb, LANE))

    qspec = pl.BlockSpec((tb, hp * LANE), lambda h, t, qi_, kj_: (qi_[t], h))
    kspec = pl.BlockSpec((tb, hp * LANE), lambda h, t, qi_, kj_: (kj_[t], h))
    krspec = pl.BlockSpec((tb, LANE), lambda h, t, qi_, kj_: (kj_[t], 0))
    shp = jax.ShapeDtypeStruct((lp, nh * LANE), F32)
    return _pcall(
        body, name=name, out_shape=[shp, shp],
        grid_spec=pltpu.PrefetchScalarGridSpec(
            num_scalar_prefetch=2, grid=(nh // hp, qi.shape[0]),
            in_specs=[qspec, qspec, kspec, krspec, kspec], out_specs=[qspec, qspec],
            scratch_shapes=[pltpu.VMEM((hp, tb, 1), F32), pltpu.VMEM((hp, tb, 1), F32),
                            pltpu.VMEM((hp, tb, LANE), F32)]),
        compiler_params=_params(2),
    )(qi, kj, qn, qr, kn, kr, v)


def _flash_bwd(qn, qr, kn, kr, v, o, do, lse, name, tb=ROW_ALIGN):
    lp = qn.shape[0]
    nb = lp // tb
    nh = MLA_HEADS
    qi, kj = _causal_pairs(nb, by_key=True)
    knt, krt = kn.T, kr.T

    def body(qi_ref, kj_ref, qn_ref, qr_ref, kn_ref, kr_ref, knt_ref, krt_ref, v_ref, o_ref, do_ref, lse_ref,
             dqnt_ref, dqrt_ref, dkn_ref, dkr_ref, dv_ref, dk_acc, dv_acc):
        t = pl.program_id(1)
        i, j = qi_ref[t], kj_ref[t]

        @pl.when(t == 0)
        def _():
            dqnt_ref[...] = jnp.zeros_like(dqnt_ref)
            dqrt_ref[...] = jnp.zeros_like(dqrt_ref)

        @pl.when(i == j)
        def _():
            dk_acc[...] = jnp.zeros_like(dk_acc)
            dv_acc[...] = jnp.zeros_like(dv_acc)

        def step(masked):
            q = _cat(qn_ref, qr_ref)
            st = _dot(_cat(kn_ref, kr_ref), q, _NT)
            if masked:
                st = jnp.where(_att_mask(i, j, tb, True), st, NEG)
            do_blk = do_ref[...]
            lane = lax.broadcasted_iota(jnp.int32, (8, LANE), 1)
            lse_row = _hdot((lane == 0).astype(F32), lse_ref[...], _NT)[0:1]
            delta_row = _hdot(jnp.ones((8, LANE), F32), do_blk * o_ref[...], _NT)[0:1]
            pt = jnp.exp2(st - lse_row)
            do_b = do_blk.astype(BF16)
            dv_acc[...] += _dot(pt.astype(BF16), do_b, _NN)
            dpt = _dot(v_ref[...], do_b, _NT)
            dst = (pt * (dpt - delta_row)).astype(BF16)
            dk_acc[...] += _dot(dst, q, _NN)
            dqnt_ref[i] += _dot(knt_ref[...], dst, _NN) * LN2
            dqrt_ref[i] += _dot(krt_ref[...], dst, _NN) * LN2

        _masked_and_plain(i, j, step)

        @pl.when(i == nb - 1)
        def _():
            dkn_ref[...] = dk_acc[:, :LANE] * LN2
            dkr_ref[...] = dk_acc[:, LANE:] * LN2
            dv_ref[...] = dv_acc[...]

    qspec = pl.BlockSpec((tb, LANE), lambda h, t, qi_, kj_: (qi_[t], h))
    kspec = pl.BlockSpec((tb, LANE), lambda h, t, qi_, kj_: (kj_[t], h))
    krspec = pl.BlockSpec((tb, LANE), lambda h, t, qi_, kj_: (kj_[t], 0))
    ktspec = pl.BlockSpec((LANE, tb), lambda h, t, qi_, kj_: (h, kj_[t]))
    krtspec = pl.BlockSpec((LANE, tb), lambda h, t, qi_, kj_: (0, kj_[t]))
    dqtspec = pl.BlockSpec((nb, LANE, tb), lambda h, t, qi_, kj_: (h, 0, 0))
    shp = jax.ShapeDtypeStruct((lp, nh * LANE), F32)
    dqt_shape = jax.ShapeDtypeStruct((nh * nb, LANE, tb), F32)
    dqnt, dqrt, dkn, dkr, dv = _pcall(
        body, name=name, out_shape=[dqt_shape, dqt_shape, shp, shp, shp],
        grid_spec=pltpu.PrefetchScalarGridSpec(
            num_scalar_prefetch=2, grid=(nh, qi.shape[0]),
            in_specs=[qspec, qspec, kspec, krspec, ktspec, krtspec, kspec, qspec, qspec, qspec],
            out_specs=[dqtspec, dqtspec, kspec, kspec, kspec],
            scratch_shapes=[pltpu.VMEM((tb, 2 * LANE), F32), pltpu.VMEM((tb, LANE), F32)]),
        compiler_params=_params(2),
    )(qi, kj, qn, qr, kn, kr, knt, krt, v, o, do, lse)

    def rows_major(a):
        return a.reshape(nh, nb, LANE, tb).transpose(1, 3, 0, 2).reshape(lp, nh * LANE)

    return rows_major(dqnt), rows_major(dqrt), dkn, dkr, dv


ELEMENTWISE_BLOCK_BYTES = 1 << 20


def _row_tile(rows, width, copies=1):
    for t in (1024, 512, 256, 128, 64, 32, 16, 8):
        if rows % t == 0 and t * width * 4 * copies <= ELEMENTWISE_BLOCK_BYTES:
            return t
    return rows


def _adamw(w, g, m, v, name):
    rows, width = w.shape
    tr = _row_tile(rows, width)

    def body(w_ref, g_ref, m_ref, v_ref, d_ref, nm_ref, nv_ref):
        gg = g_ref[...]
        nm = ADAM_B1 * m_ref[...] + (1.0 - ADAM_B1) * gg
        nv = ADAM_B2 * v_ref[...] + (1.0 - ADAM_B2) * jnp.square(gg)
        m_hat = nm / (1.0 - ADAM_B1 ** ADAM_STEP)
        v_hat = nv / (1.0 - ADAM_B2 ** ADAM_STEP)
        d_ref[...] = -ADAM_LR * (m_hat / (jnp.sqrt(v_hat) + ADAM_EPS) + ADAM_WD * w_ref[...])
        nm_ref[...] = nm
        nv_ref[...] = nv

    spec = pl.BlockSpec((tr, width), lambda r: (r, 0))
    shp = jax.ShapeDtypeStruct((rows, width), F32)
    return _pcall(body, name=name, grid=(rows // tr,), in_specs=[spec] * 4, out_specs=[spec] * 3,
                  out_shape=[shp] * 3, compiler_params=_params(1))(w, g, m, v)


def _add_pair(a, b, name):
    s, rows, width = b.shape
    tr = _row_tile(rows, width)
    nt = rows // tr

    def body(c_ref, a_ref, b_ref, o_ref):
        o_ref[...] = a_ref[...] + b_ref[...]

    spec = pl.BlockSpec((1, tr, width), lambda i, r, c_ref: (i, r, 0))
    return _pcall(
        body, name=name, out_shape=jax.ShapeDtypeStruct(b.shape, F32),
        grid_spec=pltpu.PrefetchScalarGridSpec(
            num_scalar_prefetch=1, grid=(s, nt),
            in_specs=[pl.BlockSpec((1, tr, width), lambda i, r, c_ref: (i, c_ref[0] * nt + r, 0)), spec],
            out_specs=spec),
        compiler_params=_params(2),
    )(_core_index(), a, b)


def _sum_slots(a, name):
    s, rows, width = a.shape
    tr = _row_tile(rows, width, copies=s)

    def body(a_ref, o_ref):
        tot = a_ref[0]
        for k in range(1, s):
            tot = tot + a_ref[k]
        o_ref[...] = tot

    return _pcall(body, name=name, grid=(rows // tr,),
                  in_specs=[pl.BlockSpec((s, tr, width), lambda r: (0, r, 0))],
                  out_specs=pl.BlockSpec((tr, width), lambda r: (r, 0)),
                  out_shape=jax.ShapeDtypeStruct((rows, width), F32), compiler_params=_params(1))(a)


_ANY = pl.BlockSpec(memory_space=pl.ANY)


def _my_place():
    return lax.axis_index("x"), lax.axis_index("y"), lax.axis_index("c")


def _core_index():
    return lax.axis_index("c").astype(jnp.int32).reshape(1)


def _other_chips(x, y):
    return [(1 - x, y), (x, 1 - y), (1 - x, 1 - y)]


def _gather_shards(flat, name):
    rows, width = flat.shape

    def body(x_ref, out_ref, send_sems, recv_sems, local_sem):
        x, y, c = _my_place()
        mine = pltpu.make_async_copy(x_ref, out_ref.at[2 * x + y], local_sem)
        mine.start()
        sends = []
        for k, (px, py) in enumerate(_other_chips(x, y)):
            cp = pltpu.make_async_remote_copy(
                src_ref=x_ref, dst_ref=out_ref.at[2 * x + y], send_sem=send_sems.at[k], recv_sem=recv_sems.at[k],
                device_id=(px, py, c), device_id_type=MESH)
            cp.start()
            sends.append(cp)
        for k, (px, py) in enumerate(_other_chips(x, y)):
            pltpu.make_async_remote_copy(
                src_ref=x_ref, dst_ref=out_ref.at[2 * px + py], send_sem=send_sems.at[k], recv_sem=recv_sems.at[k],
                device_id=(px, py, c), device_id_type=MESH).wait_recv()
        for cp in sends:
            cp.wait_send()
        mine.wait()

    return _pcall(
        body, name=name, in_specs=[_ANY], out_specs=_ANY,
        out_shape=jax.ShapeDtypeStruct((4, rows, width), flat.dtype),
        scratch_shapes=[pltpu.SemaphoreType.DMA((3,)), pltpu.SemaphoreType.DMA((3,)), pltpu.SemaphoreType.DMA],
    )(flat)


def _sibling_split(g, name):
    s, rows, width = g.shape
    half = rows // 2
    tr = _row_tile(half, width)
    nt = half // tr

    def body(c_ref, g_blk, got_ref, send_sem, recv_sem):
        k = pl.program_id(0)
        t = pl.program_id(1)
        x, y, c = _my_place()
        cp = pltpu.make_async_remote_copy(
            src_ref=g_blk.at[0], dst_ref=got_ref.at[k, pl.ds(pl.multiple_of(t * tr, 8), tr), :],
            send_sem=send_sem, recv_sem=recv_sem, device_id=(x, y, 1 - c), device_id_type=MESH)
        cp.start()
        cp.wait_send()

        @pl.when(jnp.logical_and(k == s - 1, t == nt - 1))
        def _():
            pltpu.make_async_remote_copy(
                src_ref=got_ref, dst_ref=got_ref, send_sem=send_sem, recv_sem=recv_sem,
                device_id=(x, y, 1 - c), device_id_type=MESH).wait_recv()

    return _pcall(
        body, name=name, out_shape=jax.ShapeDtypeStruct((s, half, width), g.dtype),
        grid_spec=pltpu.PrefetchScalarGridSpec(
            num_scalar_prefetch=1, grid=(s, nt),
            in_specs=[pl.BlockSpec((1, tr, width), lambda k, t, c_ref: (k, (1 - c_ref[0]) * nt + t, 0))],
            out_specs=_ANY,
            scratch_shapes=[pltpu.SemaphoreType.DMA, pltpu.SemaphoreType.DMA]),
        compiler_params=_params(2),
    )(_core_index(), g)


def _chip_scatter(p, name):
    s, rows, width = p.shape

    def body(p_ref, out_ref, send_sems, recv_sems, local_sem):
        x, y, c = _my_place()
        me = 2 * x + y
        mine = pltpu.make_async_copy(p_ref.at[me], out_ref.at[me], local_sem)
        mine.start()
        sends = []
        for k, (px, py) in enumerate(_other_chips(x, y)):
            cp = pltpu.make_async_remote_copy(
                src_ref=p_ref.at[2 * px + py], dst_ref=out_ref.at[me], send_sem=send_sems.at[k],
                recv_sem=recv_sems.at[k], device_id=(px, py, c), device_id_type=MESH)
            cp.start()
            sends.append(cp)
        for k, (px, py) in enumerate(_other_chips(x, y)):
            pltpu.make_async_remote_copy(
                src_ref=p_ref.at[me], dst_ref=out_ref.at[2 * px + py], send_sem=send_sems.at[k],
                recv_sem=recv_sems.at[k], device_id=(px, py, c), device_id_type=MESH).wait_recv()
        for cp in sends:
            cp.wait_send()
        mine.wait()

    return _pcall(
        body, name=name, in_specs=[_ANY], out_specs=_ANY,
        out_shape=jax.ShapeDtypeStruct(p.shape, p.dtype),
        scratch_shapes=[pltpu.SemaphoreType.DMA((3,)), pltpu.SemaphoreType.DMA((3,)), pltpu.SemaphoreType.DMA],
    )(p)


def _sibling_join(qh, name):
    half, width = qh.shape
    tr = _row_tile(half, width)
    nt = half // tr

    def body(q_blk, out_ref, send_sem, recv_sem, local_sem):
        t = pl.program_id(0)
        x, y, c = _my_place()
        dst = out_ref.at[pl.ds(pl.multiple_of(c * half + t * tr, 8), tr), :]
        cp = pltpu.make_async_remote_copy(
            src_ref=q_blk, dst_ref=dst, send_sem=send_sem, recv_sem=recv_sem,
            device_id=(x, y, 1 - c), device_id_type=MESH)
        cp.start()
        mine = pltpu.make_async_copy(q_blk, dst, local_sem)
        mine.start()
        cp.wait_send()
        mine.wait()

        @pl.when(t == nt - 1)
        def _():
            theirs = out_ref.at[pl.ds(pl.multiple_of((1 - c) * half, 8), half), :]
            pltpu.make_async_remote_copy(
                src_ref=theirs, dst_ref=theirs, send_sem=send_sem, recv_sem=recv_sem,
                device_id=(x, y, 1 - c), device_id_type=MESH).wait_recv()

    return _pcall(
        body, name=name, grid=(nt,),
        in_specs=[pl.BlockSpec((tr, width), lambda t: (t, 0))], out_specs=_ANY,
        out_shape=jax.ShapeDtypeStruct((2 * half, width), qh.dtype),
        scratch_shapes=[pltpu.SemaphoreType.DMA, pltpu.SemaphoreType.DMA, pltpu.SemaphoreType.DMA],
        compiler_params=_params(1),
    )(qh)


def _all_sum_small(part, name):
    rows, width = part.shape

    def body(p_ref, out_ref, land, send_sems, recv_sems):
        x, y, c = _my_place()
        me = 4 * x + 2 * y + c
        land[me] = p_ref[...]
        sends = []
        for k in range(1, 8):
            peer = (x ^ (k >> 2), y ^ ((k >> 1) & 1), c ^ (k & 1))
            cp = pltpu.make_async_remote_copy(
                src_ref=p_ref, dst_ref=land.at[me], send_sem=send_sems.at[k - 1], recv_sem=recv_sems.at[k - 1],
                device_id=peer, device_id_type=MESH)
            cp.start()
            sends.append(cp)
        for k in range(1, 8):
            px, py, pc = x ^ (k >> 2), y ^ ((k >> 1) & 1), c ^ (k & 1)
            pltpu.make_async_remote_copy(
                src_ref=p_ref, dst_ref=land.at[4 * px + 2 * py + pc], send_sem=send_sems.at[k - 1],
                recv_sem=recv_sems.at[k - 1], device_id=(px, py, pc), device_id_type=MESH).wait_recv()
        for cp in sends:
            cp.wait_send()
        tot = land[0]
        for k in range(1, 8):
            tot = tot + land[k]
        out_ref[...] = tot

    vmem = pl.BlockSpec(memory_space=pltpu.VMEM)
    return _pcall(
        body, name=name, in_specs=[vmem], out_specs=vmem,
        out_shape=jax.ShapeDtypeStruct((rows, width), F32),
        scratch_shapes=[pltpu.VMEM((8, rows, width), F32), pltpu.SemaphoreType.DMA((7,)),
                        pltpu.SemaphoreType.DMA((7,))],
    )(part)


def _big_layout(shards):
    return [(a.shape[0], a.shape[1], ax) for a, ax in shards]


FLAT_ROW_MULTIPLE = 2048


def _pack_shards(arrs, row_multiple=FLAT_ROW_MULTIPLE):
    flat = jnp.concatenate([a.reshape(-1) for a in arrs])
    return jnp.pad(flat, (0, -flat.shape[0] % (row_multiple * LANE))).reshape(-1, LANE)


def _unpack_shards(flat, layout):
    flat = flat.reshape(-1)
    out, off = [], 0
    for r, c, _ in layout:
        out.append(flat[off:off + r * c].reshape(r, c))
        off += r * c
    return out


def _unpack_full(gathered, layout):
    g = gathered.reshape(4, -1)
    out, off = [], 0
    for r, c, ax in layout:
        seg = g[:, off:off + r * c].reshape(4, r, c)
        out.append(seg.transpose(1, 0, 2).reshape(r, 4 * c) if ax == 1 else seg.reshape(4 * r, c))
        off += r * c
    return out


def _pack_full(fulls, layout):
    parts = []
    for a, (r, c, ax) in zip(fulls, layout):
        if ax == 1:
            parts.append(a.reshape(r, 4, c).transpose(1, 0, 2).reshape(4, r * c))
        else:
            parts.append(a.reshape(4, r * c))
    flat = jnp.concatenate(parts, axis=1)
    return jnp.pad(flat, ((0, 0), (0, -flat.shape[1] % (FLAT_ROW_MULTIPLE * LANE)))).reshape(4, -1, LANE)


def _pad_lanes(a, width=LANE):
    return jnp.pad(a, [(0, 0)] * (a.ndim - 1) + [(0, width - a.shape[-1])])


def _pack_small(arrs):
    rows = [_pad_lanes(a.reshape(1, -1), -(-a.size // LANE) * LANE).reshape(-1, LANE) for a in arrs]
    flat = jnp.concatenate(rows, axis=0)
    return jnp.pad(flat, ((0, -flat.shape[0] % 8), (0, 0)))


def _unpack_small(flat, shapes):
    out, off = [], 0
    for shp in shapes:
        n = math.prod(shp)
        nr = -(-n // LANE)
        out.append(flat[off:off + nr].reshape(-1)[:n].reshape(shp))
        off += nr
    return out


def kernel(x, meta_tokens, pre_norm, post_norm, gdn_w_in, gdn_conv_w, gdn_a_log, gdn_dt_bias, gdn_out_norm, gdn_w_out, kv_norm, kv_w_down, kv_latent_norm, kv_w_up, mla_w_in, mla_q_latent_norm, mla_w_q_up, mla_w_out, loss_target, m_meta_tokens, m_pre_norm, m_post_norm, m_gdn_w_in, m_gdn_conv_w, m_gdn_a_log, m_gdn_dt_bias, m_gdn_out_norm, m_gdn_w_out, m_kv_norm, m_kv_w_down, m_kv_latent_norm, m_kv_w_up, m_mla_w_in, m_mla_q_latent_norm, m_mla_w_q_up, m_mla_w_out, v_meta_tokens, v_pre_norm, v_post_norm, v_gdn_w_in, v_gdn_conv_w, v_gdn_a_log, v_gdn_dt_bias, v_gdn_out_norm, v_gdn_w_out, v_kv_norm, v_kv_w_down, v_kv_latent_norm, v_kv_w_up, v_mla_w_in, v_mla_q_latent_norm, v_mla_w_q_up, v_mla_w_out):
    seq = x.shape[1]
    d = D_MODEL
    lp = -(-(ROW0 + seq) // ROW_ALIGN) * ROW_ALIGN
    tail = lp - ROW0 - seq

    big_names = ["meta_tokens", "gdn_conv_w", "gdn_w_out", "kv_w_down", "kv_w_up", "mla_w_in", "mla_w_q_up",
                 "mla_w_out"]
    big_axis = [1, 1, 0, 0, 1, 1, 1, 0]
    big_w = [meta_tokens, gdn_conv_w[0], gdn_w_out[0], kv_w_down, kv_w_up, mla_w_in[0], mla_w_q_up[0], mla_w_out[0]]
    big_m = [m_meta_tokens, m_gdn_conv_w[0], m_gdn_w_out[0], m_kv_w_down, m_kv_w_up, m_mla_w_in[0], m_mla_w_q_up[0],
             m_mla_w_out[0]]
    big_v = [v_meta_tokens, v_gdn_conv_w[0], v_gdn_w_out[0], v_kv_w_down, v_kv_w_up, v_mla_w_in[0], v_mla_w_q_up[0],
             v_mla_w_out[0]]
    layout = _big_layout(list(zip(big_w, big_axis)))
    w_flat = _pack_shards(big_w)
    meta_f, conv_w = _unpack_full(
        _gather_shards(_pack_shards(big_w[:2], row_multiple=16), "gather_meta_conv"), layout[:2])
    mm_shards = [w.astype(BF16) for w in big_w[2:6]] + [(big_w[6] * Q_PRESCALE).astype(BF16), big_w[7].astype(BF16)]
    (w_out0, kv_down, kv_up, w_in1, w_qup, w_out1) = _unpack_full(
        _gather_shards(_pack_shards(mm_shards), "gather_weights"), layout[2:])
    w_in0_shards = _gather_shards(gdn_w_in[0].astype(BF16), "gather_gdn_w_in")
    w_in0 = jnp.concatenate([w_in0_shards[s] for s in range(4)], axis=1)
    win_cols = gdn_w_in.shape[2]

    nv = GDN_V_HEADS
    w_qkv = w_in0[:, :GDN_CONV_W]
    w_z0 = w_in0[:, GDN_CONV_W:GDN_CONV_W + GDN_V_W]
    w_b = _pad_lanes(w_in0[:, GDN_CONV_W + GDN_V_W:GDN_CONV_W + GDN_V_W + nv])
    w_a = _pad_lanes(w_in0[:, GDN_CONV_W + GDN_V_W + nv:])
    w_ckv = kv_down[:, :MLA_KV_RANK]
    w_kr = _pad_lanes(kv_down[:, MLA_KV_RANK:])
    kvu = kv_up.reshape(MLA_KV_RANK, MLA_HEADS, 2 * LANE)
    w_kn = kvu[:, :, :LANE].reshape(MLA_KV_RANK, MLA_HEADS * LANE)
    w_v = kvu[:, :, LANE:].reshape(MLA_KV_RANK, MLA_HEADS * LANE)
    w_cq = w_in1[:, :MLA_Q_RANK]
    w_z1 = w_in1[:, MLA_Q_RANK:]
    qu = w_qup.reshape(MLA_Q_RANK, MLA_HEADS, MLA_QK)
    w_qn = qu[:, :, :MLA_NOPE].reshape(MLA_Q_RANK, MLA_HEADS * LANE)
    w_qr = _pad_lanes(qu[:, :, MLA_NOPE:]).reshape(MLA_Q_RANK, MLA_HEADS * LANE)

    pre0, pre1 = pre_norm[0:1], pre_norm[1:2]
    post0, post1 = post_norm[0:1], post_norm[1:2]
    a_log = _pad_lanes(gdn_a_log)
    dt_bias = _pad_lanes(gdn_dt_bias)
    kvn = kv_norm.reshape(1, d)
    kvl = kv_latent_norm.reshape(1, MLA_KV_RANK)
    qln = mla_q_latent_norm

    h0 = jnp.concatenate([jnp.zeros((FRONT, d), F32), meta_f, x[0], jnp.zeros((tail, d), F32)], axis=0)
    tgt = jnp.pad(loss_target[0], ((ROW0, tail), (0, 0)))
    pos = jnp.maximum(jnp.arange(lp, dtype=jnp.int32) - FRONT, 0).astype(F32)
    inv = ROPE_THETA ** (-jnp.arange(0, MLA_ROPE, 2, dtype=F32) / MLA_ROPE)
    ang = pos[:, None] * inv[None, :]
    zeros64 = jnp.zeros((lp, LANE - MLA_ROPE), F32)
    cos_t = jnp.concatenate([jnp.cos(ang), jnp.cos(ang), zeros64], axis=1)
    sin_t = jnp.concatenate([-jnp.sin(ang), jnp.sin(ang), zeros64], axis=1)

    def valid_rows(ridx):
        return jnp.logical_and(ridx >= FRONT, ridx < ROW0 + seq)

    def f_pre0(ridx, g, h, gain):
        return _rms(h, gain), h

    (hn0,) = _rowwise("pre0", lambda *a: f_pre0(*a)[:1], [_In(h0), _In(pre0, "const")],
                      [_Out("row", (lp, d), BF16)])
    qkv_raw = _mm(hn0, w_qkv, "nn", "gdn_in_qkv")
    z0 = _mm(hn0, w_z0, "nn", "gdn_in_z")
    b_raw = _mm(hn0, w_b, "nn", "gdn_in_b")
    a_raw = _mm(hn0, w_a, "nn", "gdn_in_a")

    def f_ba(ridx, g, b, a, alog, dtb):
        tr = b.shape[0]
        ok = valid_rows(ridx).astype(F32)
        beta = jax.nn.sigmoid(b) * ok
        gate = -jnp.exp(alog) * _softplus(a + dtb) * ok
        ii = lax.broadcasted_iota(jnp.int32, (tr, tr), 0)
        jj = lax.broadcasted_iota(jnp.int32, (tr, tr), 1)
        tri = jnp.logical_and((ii >> 6) == (jj >> 6), ii >= jj).astype(F32)
        return beta, _hdot(tri, gate)

    ba_ins = [_In(b_raw), _In(a_raw), _In(a_log, "const"), _In(dt_bias, "const")]
    beta, gc = _rowwise("gdn_gates", f_ba, ba_ins, [_Out("row", (lp, LANE)), _Out("row", (lp, LANE))])
    qkv = _conv_fwd(qkv_raw, conv_w, "gdn_conv")
    o0, ckpt = _gdn_fwd(qkv, beta, gc, "gdn_scan")

    def per_head(fn, *arrs):
        n = arrs[0].shape[1] // LANE
        return jnp.concatenate([fn(*[a[:, i * LANE:(i + 1) * LANE] for a in arrs]) for i in range(n)], axis=1)

    def f_gate0(ridx, g, o, z, gain):
        return (per_head(lambda oh, zh: _rms(oh, gain) * _silu(zh), o, z),)

    gate0_ins = [_In(o0), _In(z0), _In(gdn_out_norm, "const")]
    (gated0,) = _rowwise("gdn_gate", f_gate0, gate0_ins, [_Out("row", (lp, GDN_V_W), BF16)])
    y0 = _mm(gated0, w_out0, "nn", "gdn_out")

    def f_mid(ridx, g, h, y, g_post, g_pre, g_kv):
        h1 = h + _rms(y, g_post)
        return h1, _rms(h1, g_pre), _rms(h1, g_kv)

    mid_ins = [_In(h0), _In(y0), _In(post0, "const"), _In(pre1, "const"), _In(kvn, "const")]
    h1, hn1, hkv = _rowwise("mid", f_mid, mid_ins,
                            [_Out("row", (lp, d)), _Out("row", (lp, d), BF16), _Out("row", (lp, d), BF16)])

    ckv_raw = _mm(hkv, w_ckv, "nn", "kv_down_c")
    kr_raw = _mm(hkv, w_kr, "nn", "kv_down_r")

    def f_ckv(ridx, g, c, r, cs, sn, gain):
        return _rms(c, gain), _rope(r, cs, sn)

    ckv_ins = [_In(ckv_raw), _In(kr_raw), _In(cos_t), _In(sin_t), _In(kvl, "const")]
    ckv, kr = _rowwise("kv_latent", f_ckv, ckv_ins, [_Out("row", (lp, LANE)), _Out("row", (lp, LANE), BF16)],
                       tr=640)
    kn = _mm(ckv, w_kn, "nn", "kv_up_k", BF16)
    vv = _mm(ckv, w_v, "nn", "kv_up_v", BF16)
    cq_raw = _mm(hn1, w_cq, "nn", "mla_in_q")
    z1 = _mm(hn1, w_z1, "nn", "mla_in_z")

    def f_cq(ridx, g, c, gain):
        return (_rms(c, gain),)

    cq_ins = [_In(cq_raw), _In(qln, "const")]
    (cq,) = _rowwise("q_latent", f_cq, cq_ins, [_Out("row", (lp, MLA_Q_RANK))], tr=640)
    qn = _mm(cq, w_qn, "nn", "q_up_n", BF16)
    qr_raw = _mm(cq, w_qr, "nn", "q_up_r")

    def f_qrope(ridx, g, r, cs, sn):
        return (per_head(lambda rh: _rope(rh, cs, sn), r),)

    qr_ins = [_In(qr_raw), _In(cos_t), _In(sin_t)]
    (qr,) = _rowwise("q_rope", f_qrope, qr_ins, [_Out("row", (lp, MLA_HEADS * LANE), BF16)])
    o1, lse = _flash_fwd(qn, qr, kn, kr, vv, "attention")

    def f_gate1(ridx, g, o, z):
        return (o * _silu(z),)

    gate1_ins = [_In(o1), _In(z1)]
    (og,) = _rowwise("mla_gate", f_gate1, gate1_ins, [_Out("row", (lp, MLA_HEADS * LANE), BF16)])
    y1 = _mm(og, w_out1, "nn", "mla_out")

    def f_final(ridx, g, h, y, t, gain):
        ok = jnp.logical_and(ridx >= ROW0, ridx < ROW0 + seq).astype(F32)

        def rows_loss(h_, y_, gain_):
            err = (h_ + _rms(y_, gain_) - t) * ok
            return 0.5 * jnp.sum(jnp.sum(err * err, axis=1, keepdims=True), axis=0, keepdims=True) / d

        val, vjp = jax.vjp(rows_loss, h, y, gain)
        dh, dy, dgain = vjp(jnp.ones((1, 1), F32))
        return dh, dy, dgain, jnp.broadcast_to(val, (1, LANE))

    dh2, dy1, dpost1, loss_part = _rowwise(
        "loss_head", f_final, [_In(h1), _In(y1), _In(tgt), _In(post1, "const")],
        [_Out("row", (lp, d)), _Out("row", (lp, d)), _Out("acc", (1, d)), _Out("acc", (1, LANE))])

    dog = _mm(dy1, w_out1, "nt", "mla_out_dx")
    dw_out1 = _mm(og, dy1, "tn", "mla_out_dw")
    do1, dz1 = _rowwise_vjp("mla_gate_bwd", f_gate1, gate1_ins, [[dog]], [0, 1])
    dqn, dqr, dkn, dkr, dvv = _flash_bwd(qn, qr, kn, kr, vv, o1, do1, lse, "attention_bwd")
    (dqr_raw,) = _rowwise_vjp("q_rope_bwd", f_qrope, qr_ins, [[dqr]], [0])
    dcq_a = _mm(dqn, w_qn, "nt", "q_up_n_dx")
    dcq_b = _mm(dqr_raw, w_qr, "nt", "q_up_r_dx")
    dw_qn = _mm(cq, dqn, "tn", "q_up_n_dw") * Q_PRESCALE
    dw_qr = _mm(cq, dqr_raw, "tn", "q_up_r_dw") * Q_PRESCALE
    dcq_raw, dqln = _rowwise_vjp("q_latent_bwd", f_cq, cq_ins, [[dcq_a, dcq_b]], [0, 1], tr=640)
    dhn1_a = _mm(dcq_raw, w_cq, "nt", "mla_in_q_dx")
    dhn1_b = _mm(dz1, w_z1, "nt", "mla_in_z_dx")
    dw_cq = _mm(hn1, dcq_raw, "tn", "mla_in_q_dw")
    dw_z1 = _mm(hn1, dz1, "tn", "mla_in_z_dw")
    dckv_a = _mm(dkn, w_kn, "nt", "kv_up_k_dx")
    dckv_b = _mm(dvv, w_v, "nt", "kv_up_v_dx")
    dw_kn = _mm(ckv, dkn, "tn", "kv_up_k_dw")
    dw_v = _mm(ckv, dvv, "tn", "kv_up_v_dw")
    dckv_raw, dkr_raw, dkvl = _rowwise_vjp("kv_latent_bwd", f_ckv, ckv_ins, [[dckv_a, dckv_b], [dkr]], [0, 1, 4],
                                           tr=640)
    dhkv_a = _mm(dckv_raw, w_ckv, "nt", "kv_down_c_dx")
    dhkv_b = _mm(dkr_raw, w_kr, "nt", "kv_down_r_dx")
    dw_ckv = _mm(hkv, dckv_raw, "tn", "kv_down_c_dw")
    dw_kr = _mm(hkv, dkr_raw, "tn", "kv_down_r_dw")
    dh0_res, dy0, dpost0, dpre1, dkvn = _rowwise_vjp(
        "mid_bwd", f_mid, mid_ins, [[dh2], [dhn1_a, dhn1_b], [dhkv_a, dhkv_b]], [0, 1, 2, 3, 4])

    dgated0 = _mm(dy0, w_out0, "nt", "gdn_out_dx")
    dw_out0 = _mm(gated0, dy0, "tn", "gdn_out_dw")
    do0, dz0, doutn = _rowwise_vjp("gdn_gate_bwd", f_gate0, gate0_ins, [[dgated0]], [0, 1, 2], tr=160)
    dq0, dk0, dv0, dbeta, dgc = _gdn_bwd(qkv, beta, gc, ckpt, do0, "gdn_scan_bwd")
    db_raw, da_raw, dalog, ddtb = _rowwise_vjp("gdn_gates_bwd", f_ba, ba_ins, [[dbeta], [dgc]], [0, 1, 2, 3])
    dqkv_raw, dconv = _conv_bwd(qkv_raw, conv_w, dq0, dk0, dv0, "gdn_conv_bwd")
    dhn0_a = _mm(dqkv_raw, w_qkv, "nt", "gdn_in_qkv_dx")
    dhn0_b = _mm(dz0, w_z0, "nt", "gdn_in_z_dx")
    dhn0_c = _mm(db_raw, w_b, "nt", "gdn_in_b_dx")
    dhn0_d = _mm(da_raw, w_a, "nt", "gdn_in_a_dx")
    dw_qkv = _mm(hn0, dqkv_raw, "tn", "gdn_in_qkv_dw")
    dw_z0 = _mm(hn0, dz0, "tn", "gdn_in_z_dw")
    dw_b = _mm(hn0, db_raw, "tn", "gdn_in_b_dw")
    dw_a = _mm(hn0, da_raw, "tn", "gdn_in_a_dw")
    dh0, dpre0 = _rowwise_vjp("pre0_bwd", f_pre0, [_In(h0), _In(pre0, "const")],
                              [[dhn0_a, dhn0_b, dhn0_c, dhn0_d], [dh0_res]], [0, 1])

    grad_x = dh0[ROW0:ROW0 + seq][None]
    g_meta = dh0[FRONT:ROW0]
    g_w_in0 = jnp.concatenate([dw_qkv, dw_z0, dw_b[:, :nv], dw_a[:, :nv]], axis=1)
    g_kv_down = jnp.concatenate([dw_ckv, dw_kr[:, :MLA_ROPE]], axis=1)
    g_kv_up = jnp.concatenate([dw_kn.reshape(MLA_KV_RANK, MLA_HEADS, LANE), dw_v.reshape(MLA_KV_RANK, MLA_HEADS, LANE)],
                              axis=2).reshape(MLA_KV_RANK, MLA_HEADS * 2 * LANE)
    g_w_in1 = jnp.concatenate([dw_cq, dw_z1], axis=1)
    g_qup = jnp.concatenate([dw_qn.reshape(MLA_Q_RANK, MLA_HEADS, LANE),
                             dw_qr.reshape(MLA_Q_RANK, MLA_HEADS, LANE)[:, :, :MLA_ROPE]],
                            axis=2).reshape(MLA_Q_RANK, MLA_HEADS * MLA_QK)
    big_g = [g_meta, dconv, dw_out0, g_kv_down, g_kv_up, g_w_in1, g_qup, dw_out1]

    def reduce_to_shard(g_by_chip, tag):
        got = _sibling_split(g_by_chip, "grads_sibling_split" + tag)
        chip_part = _add_pair(g_by_chip, got, "grads_chip_sum" + tag)
        from_chips = _chip_scatter(chip_part, "grads_chip_scatter" + tag)
        half_sum = _sum_slots(from_chips, "grads_total" + tag)
        return _sibling_join(half_sum, "grads_sibling_join" + tag)

    g_flat = reduce_to_shard(_pack_full(big_g, layout), "")
    g_win = reduce_to_shard(jnp.stack([g_w_in0[:, s * win_cols:(s + 1) * win_cols] for s in range(4)]), "_gdn_w_in")

    small_shapes = [(2, d), (2, d), (1, nv), (1, nv), (1, GDN_DK), (d,), (MLA_KV_RANK,), (1, MLA_Q_RANK), (1, LANE)]
    small_part = _pack_small([jnp.concatenate([dpre0, dpre1], axis=0), jnp.concatenate([dpost0, dpost1], axis=0),
                              dalog[:, :nv], ddtb[:, :nv], doutn, dkvn, dkvl, dqln, loss_part])
    small_tot = _all_sum_small(small_part, "small_sum")
    small_g = _unpack_small(small_tot, small_shapes)
    loss = small_g[-1][0, 0]

    d_flat, m_flat, v_flat = _adamw(w_flat, g_flat, _pack_shards(big_m), _pack_shards(big_v), "adamw_sharded")
    win_step = _adamw(gdn_w_in[0], g_win, m_gdn_w_in[0], v_gdn_w_in[0], "adamw_gdn_w_in")
    small_w = [pre_norm, post_norm, gdn_a_log, gdn_dt_bias, gdn_out_norm, kv_norm, kv_latent_norm, mla_q_latent_norm]
    small_m = [m_pre_norm, m_post_norm, m_gdn_a_log, m_gdn_dt_bias, m_gdn_out_norm, m_kv_norm, m_kv_latent_norm,
               m_mla_q_latent_norm]
    small_v = [v_pre_norm, v_post_norm, v_gdn_a_log, v_gdn_dt_bias, v_gdn_out_norm, v_kv_norm, v_kv_latent_norm,
               v_mla_q_latent_norm]
    g_small_flat = _pack_small(small_g[:-1])
    ds_flat, ms_flat, vs_flat = _adamw(_pack_small(small_w), g_small_flat, _pack_small(small_m), _pack_small(small_v),
                                       "adamw_replicated")

    def assemble(big_flat, small_flat, win):
        bigs = dict(zip(big_names, [a.reshape(w.shape) for a, w in zip(
            _unpack_shards(big_flat, layout),
            [meta_tokens, gdn_conv_w, gdn_w_out, kv_w_down, kv_w_up, mla_w_in, mla_w_q_up, mla_w_out])]))
        smalls = dict(zip(["pre_norm", "post_norm", "gdn_a_log", "gdn_dt_bias", "gdn_out_norm", "kv_norm",
                           "kv_latent_norm", "mla_q_latent_norm"], _unpack_small(small_flat, small_shapes[:-1])))
        both = {**bigs, **smalls, "gdn_w_in": win[None]}
        order = ["meta_tokens", "pre_norm", "post_norm", "gdn_w_in", "gdn_conv_w", "gdn_a_log", "gdn_dt_bias",
                 "gdn_out_norm", "gdn_w_out", "kv_norm", "kv_w_down", "kv_latent_norm", "kv_w_up", "mla_w_in",
                 "mla_q_latent_norm", "mla_w_q_up", "mla_w_out"]
        return [both[n] for n in order]

    grads = assemble(g_flat, g_small_flat, g_win)
    deltas = assemble(d_flat, ds_flat, win_step[0])
    new_m = assemble(m_flat, ms_flat, win_step[1])
    new_v = assemble(v_flat, vs_flat, win_step[2])
    return (loss, grad_x, *grads, *deltas, *new_m, *new_v)
```

```python
import functools
import math

import jax
import jax.numpy as jnp
from jax import lax
from jax.experimental import pallas as pl
from jax.experimental.pallas import tpu as pltpu

F32 = jnp.float32
BF16 = jnp.bfloat16
MESH = pl.DeviceIdType.MESH

D_MODEL = 1024
N_META = 16
FRONT = 48
ROW0 = FRONT + N_META
ROW_ALIGN = 640
NORM_EPS = 1e-6
LANE = 128

GDN_QK_HEADS = 8
GDN_V_HEADS = 16
GDN_DK = 128
GDN_CHUNK = 64
GDN_QK_W = 1024
GDN_V_W = 2048
GDN_CONV_W = 4096

MLA_HEADS = 16
MLA_NOPE = 128
MLA_ROPE = 64
MLA_QK = 192
MLA_Q_RANK = 256
MLA_KV_RANK = 128
ROPE_THETA = 10000.0

ADAM_LR = 0.001
ADAM_B1 = 0.9
ADAM_B2 = 0.999
ADAM_EPS = 1e-08
ADAM_WD = 0.01
ADAM_STEP = 10

VMEM_LIMIT_V7X = 56 * 1024 * 1024
NEG = -1e30

_NN = ((1,), (0,))
_NT = ((1,), (1,))
_TN = ((0,), (0,))
_HI = lax.Precision.HIGHEST
_X3 = lax.Precision.HIGH


def _pcall(body, **kw):
    return pl.pallas_call(body, **kw)


def _params(n_axes):
    return pltpu.CompilerParams(dimension_semantics=("arbitrary",) * n_axes, vmem_limit_bytes=VMEM_LIMIT_V7X)


def _dot(a, b, dims, prec=None):
    return lax.dot_general(a, b, (dims, ((), ())), precision=prec, preferred_element_type=F32)


def _bdot(a, b, dims):
    return _dot(a.astype(BF16), b.astype(BF16), dims)


def _hdot(a, b, dims=_NN):
    return _dot(a, b, dims, _HI)


def _fdot(a, b, dims):
    return _dot(a, b, dims)


def _tile(n):
    if n % ROW_ALIGN == 0:
        return ROW_ALIGN
    for t in (1024, 512, 256, 128):
        if n % t == 0:
            return t
    raise ValueError(n)


def _mm(a, b, mode, name, out_dtype=F32):
    if mode == "nn":
        (m, k), (k2, n) = a.shape, b.shape
    elif mode == "nt":
        (m, k), (n, k2) = a.shape, b.shape
    else:
        (k, m), (k2, n) = a.shape, b.shape
    assert k == k2, (a.shape, b.shape, mode)
    tm, tn, tk = _tile(m), _tile(n), _tile(k)
    nk = k // tk
    dims = {"nn": _NN, "nt": _NT, "tn": _TN}[mode]

    def body(a_ref, b_ref, o_ref, acc):
        kk = pl.program_id(2)

        @pl.when(kk == 0)
        def _():
            acc[...] = jnp.zeros_like(acc)

        acc[...] += _bdot(a_ref[...], b_ref[...], dims)

        @pl.when(kk == nk - 1)
        def _():
            o_ref[...] = acc[...].astype(out_dtype)

    if mode == "tn":
        a_spec = pl.BlockSpec((tk, tm), lambda i, j, kk: (kk, i))
    else:
        a_spec = pl.BlockSpec((tm, tk), lambda i, j, kk: (i, kk))
    if mode == "nt":
        b_spec = pl.BlockSpec((tn, tk), lambda i, j, kk: (j, kk))
    else:
        b_spec = pl.BlockSpec((tk, tn), lambda i, j, kk: (kk, j))
    return _pcall(
        body, name=name, grid=(m // tm, n // tn, nk),
        in_specs=[a_spec, b_spec],
        out_specs=pl.BlockSpec((tm, tn), lambda i, j, kk: (i, j)),
        out_shape=jax.ShapeDtypeStruct((m, n), out_dtype),
        scratch_shapes=[pltpu.VMEM((tm, tn), F32)],
        compiler_params=_params(3),
    )(a, b)


class _In:
    def __init__(self, arr, kind="row", grouped=False, goff=0):
        self.arr, self.kind, self.grouped, self.goff = arr, kind, grouped, goff


class _Out:
    def __init__(self, kind, shape, dtype=F32, grouped=False):
        self.kind, self.shape, self.dtype, self.grouped = kind, shape, dtype, grouped


def _rowwise(name, fn, ins, outs, *, groups=1, tr=320):
    lp = next(i.arr.shape[0] for i in ins if i.kind == "row")
    nr = lp // tr
    assert lp % tr == 0

    def in_spec(i):
        w = i.arr.shape[1]
        if i.kind == "row":
            if i.grouped:
                return pl.BlockSpec((tr, LANE), lambda g, r, o=i.goff: (r, g + o))
            return pl.BlockSpec((tr, w), lambda g, r: (r, 0))
        if i.grouped:
            return pl.BlockSpec((i.arr.shape[0], LANE), lambda g, r, o=i.goff: (0, g + o))
        return pl.BlockSpec(i.arr.shape, lambda g, r: (0, 0))

    def out_spec(o):
        if o.kind == "row":
            if o.grouped:
                return pl.BlockSpec((tr, LANE), lambda g, r: (r, g))
            assert groups == 1
            return pl.BlockSpec((tr, o.shape[1]), lambda g, r: (r, 0))
        if o.grouped:
            return pl.BlockSpec((o.shape[0], LANE), lambda g, r: (0, g))
        return pl.BlockSpec(o.shape, lambda g, r: (0, 0))

    n_in = len(ins)

    def body(*refs):
        g = pl.program_id(0)
        r = pl.program_id(1)
        ridx = r * tr + lax.broadcasted_iota(jnp.int32, (tr, 1), 0)
        res = fn(ridx, g, *[ref[...] for ref in refs[:n_in]])
        assert len(res) == len(outs), (name, len(res), len(outs))
        for o, ref, val in zip(outs, refs[n_in:], res):
            if o.kind == "row":
                ref[...] = val.astype(o.dtype)
            else:
                first = (r == 0) if o.grouped else jnp.logical_and(r == 0, g == 0)

                @pl.when(first)
                def _(ref=ref, val=val):
                    ref[...] = val.astype(F32)

                @pl.when(jnp.logical_not(first))
                def _(ref=ref, val=val):
                    ref[...] += val.astype(F32)

    res = _pcall(
        body, name=name, grid=(groups, nr),
        in_specs=[in_spec(i) for i in ins],
        out_specs=[out_spec(o) for o in outs],
        out_shape=[jax.ShapeDtypeStruct(o.shape, o.dtype) for o in outs],
        compiler_params=_params(2),
    )(*[i.arr for i in ins])
    return res


def _rowwise_vjp(name, fn, ins, cots, diff, *, groups=1, tr=320):
    n_in = len(ins)
    grouped = groups > 1
    cot_ins = []
    counts = []
    for arrs in cots:
        counts.append(len(arrs))
        for a in arrs:
            cot_ins.append(_In(a, "row", grouped=grouped and a.shape[1] > LANE))
    lp = next(i.arr.shape[0] for i in ins if i.kind == "row")
    outs = []
    for d in diff:
        i = ins[d]
        if i.kind == "row":
            w = groups * LANE if i.grouped else i.arr.shape[1]
            outs.append(_Out("row", (lp, w), F32, grouped=i.grouped))
        else:
            outs.append(_Out("acc", i.arr.shape, F32, grouped=i.grouped))

    def bfn(ridx, g, *allvals):
        vals = list(allvals[:n_in])
        cvals = allvals[n_in:]

        def f(*dv):
            full = list(vals)
            for i, v in zip(diff, dv):
                full[i] = v
            return tuple(fn(ridx, g, *full))

        primal, vjp = jax.vjp(f, *[vals[i].astype(F32) for i in diff])
        cts = []
        pos = 0
        for k, cnt in enumerate(counts):
            if cnt == 0:
                cts.append(jnp.zeros_like(primal[k]))
            else:
                c = cvals[pos].astype(F32)
                for extra in cvals[pos + 1:pos + cnt]:
                    c = c + extra.astype(F32)
                w = primal[k].shape[1]
                if c.shape[1] != w:
                    c = functools.reduce(jnp.add, [c[:, i * w:(i + 1) * w] for i in range(c.shape[1] // w)])
                cts.append(c.astype(primal[k].dtype))
            pos += cnt
        return vjp(tuple(cts))

    return _rowwise(name, bfn, list(ins) + cot_ins, outs, groups=groups, tr=tr)


def _rms(x, g):
    return x * lax.rsqrt(jnp.mean(x * x, axis=-1, keepdims=True) + NORM_EPS) * g


def _silu(x):
    return x * jax.nn.sigmoid(x)


def _softplus(x):
    return jnp.maximum(x, 0.0) + jnp.log(1.0 + jnp.exp(-jnp.abs(x)))


def _swap_halves(x):
    lane = lax.broadcasted_iota(jnp.int32, x.shape, x.ndim - 1)
    return jnp.where(lane < 32, pltpu.roll(x, LANE - 32, x.ndim - 1), pltpu.roll(x, 32, x.ndim - 1))


@jax.custom_vjp
def _rope(x, c, s):
    return x * c + _swap_halves(x) * s


def _rope_fwd(x, c, s):
    return _rope(x, c, s), (c, s)


def _rope_bwd(res, dy):
    c, s = res
    return dy * c + _swap_halves(dy * s), jnp.zeros_like(c), jnp.zeros_like(s)


_rope.defvjp(_rope_fwd, _rope_bwd)


def _conv_post(c, g):
    s = _silu(c)
    n = s * lax.rsqrt(jnp.sum(s * s, axis=-1, keepdims=True) + NORM_EPS)
    return jnp.where(g < GDN_QK_HEADS, n * (GDN_DK ** -0.5), jnp.where(g < 2 * GDN_QK_HEADS, n, s))


def _conv_taps(xe, w):
    c = xe[8:] * w[3]
    for s in (1, 2, 3):
        c = c + pltpu.roll(xe, s, 0)[8:] * w[3 - s]
    return c


CONV_LANES = 512
CONV_HEADS = CONV_LANES // LANE


def _conv_post_block(c, g):
    return jnp.concatenate([_conv_post(c[:, i * LANE:(i + 1) * LANE], g * CONV_HEADS + i)
                            for i in range(CONV_HEADS)], axis=1)


def _conv_fwd(x, w, name, tr=640):
    lp, width = x.shape
    cl = CONV_LANES
    nr = lp // tr

    def body(x_ref, prev_ref, w_ref, o_ref):
        g = pl.program_id(0)
        r = pl.program_id(1)
        prev = jnp.where(r > 0, prev_ref[...], 0.0)
        xe = jnp.concatenate([prev, x_ref[...]], axis=0)
        o_ref[...] = _conv_post_block(_conv_taps(xe, [w_ref[t:t + 1, :] for t in range(4)]), g)

    return _pcall(
        body, name=name, grid=(width // cl, nr),
        in_specs=[pl.BlockSpec((tr, cl), lambda g, r: (r, g)),
                  pl.BlockSpec((8, cl), lambda g, r: (jnp.maximum(r * (tr // 8) - 1, 0), g)),
                  pl.BlockSpec((4, cl), lambda g, r: (0, g))],
        out_specs=pl.BlockSpec((tr, cl), lambda g, r: (r, g)),
        out_shape=jax.ShapeDtypeStruct((lp, width), F32),
        compiler_params=_params(2),
    )(x, x, w)


def _conv_bwd(x, w, dq, dk, dv, name, tr=640):
    lp, width = x.shape
    cl = CONV_LANES
    nr = lp // tr
    last8 = lp // 8 - 1
    nq = GDN_QK_W // cl

    def body(x_ref, prev_ref, next_ref, w_ref, q_ref, k_ref, v_ref, q_n, k_n, v_n, dx_ref, dw_ref):
        g = pl.program_id(0)
        r = pl.program_id(1)
        w = [w_ref[t:t + 1, :] for t in range(4)]
        not_last = r < nr - 1

        def pick(a, b, c):
            return jnp.where(g < nq, a[...], jnp.where(g < 2 * nq, b[...], c[...]))

        dy = pick(q_ref, k_ref, v_ref)
        dyn = jnp.where(not_last, pick(q_n, k_n, v_n), 0.0)
        prev = jnp.where(r > 0, prev_ref[...], 0.0)
        nxt = jnp.where(not_last, next_ref[...], 0.0)
        xe = jnp.concatenate([prev, x_ref[...], nxt], axis=0)
        ce = _conv_taps(xe, w)
        _, vjp = jax.vjp(lambda c: _conv_post_block(c, g), ce)
        (dce,) = vjp(jnp.concatenate([dy, dyn], axis=0))
        n = tr + 8
        dx = dce * w[3]
        for s in (1, 2, 3):
            dx = dx + pltpu.roll(dce, n - s, 0) * w[3 - s]
        dx_ref[...] = dx[:tr]
        dc = dce[:tr]
        row4 = lax.broadcasted_iota(jnp.int32, (4, cl), 0)
        dw = jnp.zeros((4, cl), F32)
        for s in (0, 1, 2, 3):
            xs = xe[8:8 + tr] if s == 0 else pltpu.roll(xe, s, 0)[8:8 + tr]
            dw = dw + jnp.where(row4 == 3 - s, jnp.sum(dc * xs, axis=0, keepdims=True), 0.0)

        @pl.when(r == 0)
        def _():
            dw_ref[...] = dw

        @pl.when(r > 0)
        def _():
            dw_ref[...] += dw

    def col_q(g):
        return jnp.minimum(g, nq - 1)

    def col_k(g):
        return jnp.clip(g - nq, 0, nq - 1)

    def col_v(g):
        return jnp.maximum(g - 2 * nq, 0)

    def blk(colf):
        return pl.BlockSpec((tr, cl), lambda g, r: (r, colf(g)))

    def nblk(colf):
        return pl.BlockSpec((8, cl), lambda g, r: (jnp.minimum((r + 1) * (tr // 8), last8), colf(g)))

    return _pcall(
        body, name=name, grid=(width // cl, nr),
        in_specs=[pl.BlockSpec((tr, cl), lambda g, r: (r, g)),
                  pl.BlockSpec((8, cl), lambda g, r: (jnp.maximum(r * (tr // 8) - 1, 0), g)),
                  pl.BlockSpec((8, cl), lambda g, r: (jnp.minimum((r + 1) * (tr // 8), last8), g)),
                  pl.BlockSpec((4, cl), lambda g, r: (0, g)),
                  blk(col_q), blk(col_k), blk(col_v), nblk(col_q), nblk(col_k), nblk(col_v)],
        out_specs=[pl.BlockSpec((tr, cl), lambda g, r: (r, g)),
                   pl.BlockSpec((4, cl), lambda g, r: (0, g))],
        out_shape=[jax.ShapeDtypeStruct((lp, width), F32), jax.ShapeDtypeStruct((4, width), F32)],
        compiler_params=_params(2),
    )(x, x, x, w, dq, dk, dv, dq, dk, dv)


def _bmm(a, b, dims, prec=None):
    (ca,), (cb,) = dims
    return lax.dot_general(a, b, (((ca + 1,), (cb + 1,)), ((0,), (0,))), precision=prec,
                           preferred_element_type=F32)


def _inv_impl(m):
    c = m.shape[-1]
    ii = lax.broadcasted_iota(jnp.int32, (c, c), 0)
    jj = lax.broadcasted_iota(jnp.int32, (c, c), 1)
    eye = (ii == jj).astype(F32)

    def same_block(shift):
        return (ii >> shift) == (jj >> shift)

    n1 = jnp.where(same_block(3), -m, 0.0)
    n2 = _bmm(n1, n1, _NN, _X3)
    n4 = _bmm(n2, n2, _NN, _X3)
    d = _bmm(_bmm(eye + n1, eye + n2, _NN, _X3), eye + n4, _NN, _X3)
    shift = 3
    while (1 << shift) < c:
        low = jnp.where(jnp.logical_and(same_block(shift + 1), jnp.logical_not(same_block(shift))), m, 0.0)
        d = d - _bmm(d, _bmm(low, d, _NN, _X3), _NN, _X3)
        shift += 1
    return d


@jax.custom_vjp
def _inv_unit_lower(m):
    return _inv_impl(m)


def _inv_f(m):
    t = _inv_impl(m)
    return t, t


def _inv_b(t, dt):
    c = t.shape[-1]
    ii = lax.broadcasted_iota(jnp.int32, (c, c), 0)
    jj = lax.broadcasted_iota(jnp.int32, (c, c), 1)
    gm = _bmm(t, _bmm(dt, t, _NT, _X3), _TN, _X3)
    return (jnp.where(ii > jj, -gm, 0.0),)


_inv_unit_lower.defvjp(_inv_f, _inv_b)


GDN_HEADS_PER_STEP = 16


def _gdn_group(q, k, v, beta_blk, gc_blk, states, h0):
    hp = GDN_HEADS_PER_STEP
    c = q.shape[0]
    lane = lax.broadcasted_iota(jnp.int32, (1, LANE), 1)
    row8 = lax.broadcasted_iota(jnp.int32, (max(8, hp), LANE), 0)
    lane8 = lax.broadcasted_iota(jnp.int32, (max(8, hp), LANE), 1)
    gcr_all = _hdot((lane8 == h0 + row8).astype(F32), gc_blk, _NT)
    betas, gccs = [], []
    for i in range(hp):
        onehot = (lane == h0 + i).astype(F32)
        betas.append(jnp.sum(beta_blk * onehot, axis=1, keepdims=True))
        gccs.append(jnp.sum(gc_blk * onehot, axis=1, keepdims=True))
    beta = jnp.stack(betas)
    gcc = jnp.stack(gccs)
    gcr = jnp.stack([gcr_all[i:i + 1] for i in range(hp)])
    qh = jnp.stack([q[:, (i // 2) * LANE:(i // 2 + 1) * LANE] for i in range(hp)])
    kh = jnp.stack([k[:, (i // 2) * LANE:(i // 2 + 1) * LANE] for i in range(hp)])
    vh = jnp.stack([v[:, i * LANE:(i + 1) * LANE] for i in range(hp)])
    state = jnp.stack(states)
    ii = lax.broadcasted_iota(jnp.int32, (c, c), 0)
    jj = lax.broadcasted_iota(jnp.int32, (c, c), 1)
    incl = ii >= jj
    dec = jnp.where(incl, jnp.exp(jnp.where(incl, gcc - gcr, 0.0)), 0.0)
    eg = jnp.exp(gcc)
    m = _bmm(kh, kh, _NT) * beta * jnp.where(ii > jj, dec, 0.0)
    t = _inv_unit_lower(m)
    u = _bmm(t, vh * beta, _NN, _X3)
    w = _bmm(t, kh * (beta * eg), _NN, _X3)
    attn = _bmm(qh, kh, _NT) * dec
    rows = lax.broadcasted_iota(jnp.int32, (c, 1), 0)
    gl = jnp.sum(jnp.where(rows == c - 1, gcc, 0.0), axis=1, keepdims=True)
    v_new = u - _bmm(w, state, _NN)
    o = _bmm(qh * eg, state, _NN) + _bmm(attn, v_new, _NN)
    new_state = state * jnp.exp(gl) + _bmm(kh * jnp.exp(gl - gcc), v_new, _TN)
    return jnp.concatenate([o[i] for i in range(hp)], axis=1), tuple(new_state[i] for i in range(hp))


def _gdn_specs(nc, rev):
    def cidx(n):
        return (nc - 1 - n) if rev else n
    hp = GDN_HEADS_PER_STEP
    nqk = GDN_QK_HEADS
    c = GDN_CHUNK
    nq = 2 * nqk // hp
    q_spec = pl.BlockSpec((c, hp // 2 * LANE), lambda n, g: (cidx(n), g))
    k_spec = pl.BlockSpec((c, hp // 2 * LANE), lambda n, g: (cidx(n), nq + g))
    v_spec = pl.BlockSpec((c, hp * LANE), lambda n, g: (cidx(n), nq + g))
    s_spec = pl.BlockSpec((c, LANE), lambda n, g: (cidx(n), 0))
    o_spec = pl.BlockSpec((c, hp * LANE), lambda n, g: (cidx(n), g))
    ck_spec = pl.BlockSpec((hp, 1, GDN_DK, LANE), lambda n, g: (g, cidx(n), 0, 0))
    return q_spec, k_spec, v_spec, s_spec, o_spec, ck_spec


def _gdn_fwd(qkv, beta, gc, name):
    lp = qkv.shape[0]
    nc = lp // GDN_CHUNK
    nh = GDN_V_HEADS
    hp = GDN_HEADS_PER_STEP
    q_spec, k_spec, v_spec, s_spec, o_spec, ck_spec = _gdn_specs(nc, False)

    def body(q_ref, k_ref, v_ref, b_ref, g_ref, o_ref, ck_ref, state):
        n = pl.program_id(0)
        g = pl.program_id(1)

        @pl.when(n == 0)
        def _():
            for i in range(hp):
                state[g * hp + i] = jnp.zeros((GDN_DK, LANE), F32)

        states = tuple(state[g * hp + i] for i in range(hp))
        for i in range(hp):
            ck_ref[i, 0] = states[i]
        o, new_states = _gdn_group(q_ref[...], k_ref[...], v_ref[...], b_ref[...], g_ref[...], states, g * hp)
        o_ref[...] = o
        for i in range(hp):
            state[g * hp + i] = new_states[i]

    return _pcall(
        body, name=name, grid=(nc, nh // hp),
        in_specs=[q_spec, k_spec, v_spec, s_spec, s_spec],
        out_specs=[o_spec, ck_spec],
        out_shape=[jax.ShapeDtypeStruct((lp, GDN_V_W), F32),
                   jax.ShapeDtypeStruct((nh, nc, GDN_DK, LANE), F32)],
        scratch_shapes=[pltpu.VMEM((nh, GDN_DK, LANE), F32)],
        compiler_params=_params(2),
    )(qkv, qkv, qkv, beta, gc)


def _gdn_bwd(qkv, beta, gc, ckpt, do, name):
    lp = qkv.shape[0]
    nc = lp // GDN_CHUNK
    nh = GDN_V_HEADS
    hp = GDN_HEADS_PER_STEP
    q_spec, k_spec, v_spec, s_spec, o_spec, ck_spec = _gdn_specs(nc, True)

    def body(q_ref, k_ref, v_ref, b_ref, g_ref, ck_ref, do_ref,
             dq_ref, dk_ref, dv_ref, db_ref, dg_ref, dstate):
        n = pl.program_id(0)
        g = pl.program_id(1)

        @pl.when(n == 0)
        def _():
            for i in range(hp):
                dstate[g * hp + i] = jnp.zeros((GDN_DK, LANE), F32)

        states = tuple(ck_ref[i, 0] for i in range(hp))
        _, vjp = jax.vjp(lambda q, k, v, b, gg, s: _gdn_group(q, k, v, b, gg, s, g * hp),
                         q_ref[...], k_ref[...], v_ref[...], b_ref[...], g_ref[...], states)
        dq, dk, dv, db, dg, ds = vjp((do_ref[...], tuple(dstate[g * hp + i] for i in range(hp))))
        dq_ref[...] = dq
        dk_ref[...] = dk
        dv_ref[...] = dv
        for i in range(hp):
            dstate[g * hp + i] = ds[i]

        @pl.when(g == 0)
        def _():
            db_ref[...] = db
            dg_ref[...] = dg

        @pl.when(g > 0)
        def _():
            db_ref[...] += db
            dg_ref[...] += dg

    qk_shape = jax.ShapeDtypeStruct((lp, GDN_QK_W), F32)
    big = jax.ShapeDtypeStruct((lp, GDN_V_W), F32)
    small = jax.ShapeDtypeStruct((lp, LANE), F32)
    dq_spec = pl.BlockSpec((GDN_CHUNK, hp // 2 * LANE), lambda n, g: (nc - 1 - n, g))
    return _pcall(
        body, name=name, grid=(nc, nh // hp),
        in_specs=[q_spec, k_spec, v_spec, s_spec, s_spec, ck_spec, o_spec],
        out_specs=[dq_spec, dq_spec, o_spec, s_spec, s_spec],
        out_shape=[qk_shape, qk_shape, big, small, small],
        scratch_shapes=[pltpu.VMEM((nh, GDN_DK, LANE), F32)],
        compiler_params=_params(2),
    )(qkv, qkv, qkv, beta, gc, ckpt, do)


LOG2E = 1.4426950408889634
LN2 = 0.6931471805599453
Q_PRESCALE = MLA_QK ** -0.5 * LOG2E


ATT_SUB = 128
ATT_HEADS_PER_STEP = 4
ATT_BWD_HEADS_PER_STEP = 2


def _att_mask(i, j, tb, transposed):
    r = lax.broadcasted_iota(jnp.int32, (tb, tb), 0)
    c = lax.broadcasted_iota(jnp.int32, (tb, tb), 1)
    qpos, kpos = (i * tb + c, j * tb + r) if transposed else (i * tb + r, j * tb + c)
    return jnp.logical_and(kpos <= qpos, kpos >= FRONT)


def _causal_pairs(nb, by_key):
    if by_key:
        pairs = [(i, j) for j in range(nb) for i in range(j, nb)]
    else:
        pairs = [(i, j) for i in range(nb) for j in range(i + 1)]
    return jnp.array([p[0] for p in pairs], jnp.int32), jnp.array([p[1] for p in pairs], jnp.int32)


def _masked_and_plain(i, j, step):
    edge = jnp.logical_or(j == i, j == 0)

    @pl.when(jnp.logical_and(edge, j <= i))
    def _():
        step(True)

    @pl.when(jnp.logical_and(jnp.logical_not(edge), j < i))
    def _():
        step(False)


def _cat(a_ref, b_ref):
    return jnp.concatenate([a_ref[...], b_ref[...]], axis=1)


def _flash_fwd(qn, qr, kn, kr, v, name, tb=ROW_ALIGN):
    lp = qn.shape[0]
    nb = lp // tb
    nh = MLA_HEADS
    hp = ATT_HEADS_PER_STEP
    qi, kj = _causal_pairs(nb, by_key=False)

    def body(qi_ref, kj_ref, qn_ref, qr_ref, kn_ref, kr_ref, v_ref, o_ref, lse_ref, m_s, l_s, acc):
        t = pl.program_id(1)
        i, j = qi_ref[t], kj_ref[t]

        @pl.when(j == 0)
        def _():
            m_s[...] = jnp.full_like(m_s, NEG)
            l_s[...] = jnp.zeros_like(l_s)
            acc[...] = jnp.zeros_like(acc)

        def step(masked):
            n_sub = tb // ATT_SUB
            kr = kr_ref[...]
            for e in range(hp):
                lanes = pl.ds(e * LANE, LANE)
                k = jnp.concatenate([kn_ref[:, lanes], kr], axis=1)
                v = v_ref[:, lanes]

                def scores(r, lanes=lanes, k=k):
                    rows = pl.ds(r * ATT_SUB, ATT_SUB)
                    return _dot(jnp.concatenate([qn_ref[rows, lanes], qr_ref[rows, lanes]], axis=1), k, _NT)

                s_next = scores(0)
                for r in range(n_sub):
                    s = s_next
                    if r + 1 < n_sub:
                        s_next = scores(r + 1)
                    rows = pl.ds(r * ATT_SUB, ATT_SUB)
                    if masked:
                        qpos = i * tb + r * ATT_SUB + lax.broadcasted_iota(jnp.int32, (ATT_SUB, tb), 0)
                        kpos = j * tb + lax.broadcasted_iota(jnp.int32, (ATT_SUB, tb), 1)
                        s = jnp.where(jnp.logical_and(kpos <= qpos, kpos >= FRONT), s, NEG)
                    m_old = m_s[e, rows, :]
                    m_new = jnp.maximum(m_old, jnp.max(s, axis=1, keepdims=True))
                    alpha = jnp.exp2(m_old - m_new)
                    p = jnp.exp2(s - m_new)
                    l_s[e, rows, :] = alpha * l_s[e, rows, :] + jnp.sum(p, axis=1, keepdims=True)
                    acc[e, rows, :] = alpha * acc[e, rows, :] + _dot(p.astype(BF16), v, _NN)
                    m_s[e, rows, :] = m_new

        _masked_and_plain(i, j, step)

        @pl.when(j == i)
        def _():
            for e in range(hp):
                lanes = pl.ds(e * LANE, LANE)
                o_ref[:, lanes] = acc[e] / l_s[e]
                lse_ref[:, lanes] = jnp.broadcast_to(m_s[e] + jnp.log(l_s[e]) * LOG2E, (tb, LANE))

    qspec = pl.BlockSpec((tb, hp * LANE), lambda h, t, qi_, kj_: (qi_[t], h))
    kspec = pl.BlockSpec((tb, hp * LANE), lambda h, t, qi_, kj_: (kj_[t], h))
    krspec = pl.BlockSpec((tb, LANE), lambda h, t, qi_, kj_: (kj_[t], 0))
    shp = jax.ShapeDtypeStruct((lp, nh * LANE), F32)
    return _pcall(
        body, name=name, out_shape=[shp, shp],
        grid_spec=pltpu.PrefetchScalarGridSpec(
            num_scalar_prefetch=2, grid=(nh // hp, qi.shape[0]),
            in_specs=[qspec, qspec, kspec, krspec, kspec], out_specs=[qspec, qspec],
            scratch_shapes=[pltpu.VMEM((hp, tb, 1), F32), pltpu.VMEM((hp, tb, 1), F32),
                            pltpu.VMEM((hp, tb, LANE), F32)]),
        compiler_params=_params(2),
    )(qi, kj, qn, qr, kn, kr, v)


def _flash_bwd(qn, qr, kn, kr, v, o, do, lse, name, tb=ROW_ALIGN):
    lp = qn.shape[0]
    nb = lp // tb
    nh = MLA_HEADS
    hp = ATT_BWD_HEADS_PER_STEP
    qi, kj = _causal_pairs(nb, by_key=True)
    n_pairs = qi.shape[0]
    knt, krt = kn.T, kr.T

    def body(qi_ref, kj_ref, qn_ref, qr_ref, kn_ref, kr_ref, knt_ref, krt_ref, v_ref, o_ref, do_ref, lse_ref,
             dqnt_hbm, dqrt_hbm, dkn_ref, dkr_ref, dv_ref, dk_acc, dv_acc, dqn_acc, dqr_acc, out_sems):
        g = pl.program_id(0)
        t = pl.program_id(1)
        i, j = qi_ref[t], kj_ref[t]

        @pl.when(t == 0)
        def _():
            dqn_acc[...] = jnp.zeros_like(dqn_acc)
            dqr_acc[...] = jnp.zeros_like(dqr_acc)

        @pl.when(i == j)
        def _():
            dk_acc[...] = jnp.zeros_like(dk_acc)
            dv_acc[...] = jnp.zeros_like(dv_acc)

        def step(masked):
            kr = kr_ref[...]
            krt_blk = krt_ref[...]
            lane = lax.broadcasted_iota(jnp.int32, (8, LANE), 1)
            for e in range(hp):
                lanes = pl.ds(e * LANE, LANE)
                q = jnp.concatenate([qn_ref[:, lanes], qr_ref[:, lanes]], axis=1)
                st = _dot(jnp.concatenate([kn_ref[:, lanes], kr], axis=1), q, _NT)
                if masked:
                    st = jnp.where(_att_mask(i, j, tb, True), st, NEG)
                do_blk = do_ref[:, lanes]
                lse_row = _hdot((lane == 0).astype(F32), lse_ref[:, lanes], _NT)[0:1]
                delta_row = _hdot(jnp.ones((8, LANE), F32), do_blk * o_ref[:, lanes], _NT)[0:1]
                pt = jnp.exp2(st - lse_row)
                do_b = do_blk.astype(BF16)
                dv_acc[e] += _dot(pt.astype(BF16), do_b, _NN)
                dpt = _dot(v_ref[:, lanes], do_b, _NT)
                dst = (pt * (dpt - delta_row)).astype(BF16)
                dk_acc[e] += _dot(dst, q, _NN)
                dqn_acc[e * nb + i] += _dot(knt_ref[pl.ds(e * LANE, LANE), :], dst, _NN) * LN2
                dqr_acc[e * nb + i] += _dot(krt_blk, dst, _NN) * LN2

        _masked_and_plain(i, j, step)

        @pl.when(i == nb - 1)
        def _():
            for e in range(hp):
                lanes = pl.ds(e * LANE, LANE)
                dkn_ref[:, lanes] = dk_acc[e, :, :LANE] * LN2
                dkr_ref[:, lanes] = dk_acc[e, :, LANE:] * LN2
                dv_ref[:, lanes] = dv_acc[e]

        @pl.when(t == n_pairs - 1)
        def _():
            dst_rows = pl.ds(g * (hp * nb), hp * nb)
            cn = pltpu.make_async_copy(dqn_acc, dqnt_hbm.at[dst_rows], out_sems.at[0])
            cr = pltpu.make_async_copy(dqr_acc, dqrt_hbm.at[dst_rows], out_sems.at[1])
            cn.start()
            cr.start()
            cn.wait()
            cr.wait()

    qspec = pl.BlockSpec((tb, hp * LANE), lambda h, t, qi_, kj_: (qi_[t], h))
    kspec = pl.BlockSpec((tb, hp * LANE), lambda h, t, qi_, kj_: (kj_[t], h))
    krspec = pl.BlockSpec((tb, LANE), lambda h, t, qi_, kj_: (kj_[t], 0))
    ktspec = pl.BlockSpec((hp * LANE, tb), lambda h, t, qi_, kj_: (h, kj_[t]))
    krtspec = pl.BlockSpec((LANE, tb), lambda h, t, qi_, kj_: (0, kj_[t]))
    shp = jax.ShapeDtypeStruct((lp, nh * LANE), F32)
    dqt_shape = jax.ShapeDtypeStruct((nh * nb, LANE, tb), F32)
    dqnt, dqrt, dkn, dkr, dv = _pcall(
        body, name=name, out_shape=[dqt_shape, dqt_shape, shp, shp, shp],
        grid_spec=pltpu.PrefetchScalarGridSpec(
            num_scalar_prefetch=2, grid=(nh // hp, n_pairs),
            in_specs=[qspec, qspec, kspec, krspec, ktspec, krtspec, kspec, qspec, qspec, qspec],
            out_specs=[_ANY, _ANY, kspec, kspec, kspec],
            scratch_shapes=[pltpu.VMEM((hp, tb, 2 * LANE), F32), pltpu.VMEM((hp, tb, LANE), F32),
                            pltpu.VMEM((hp * nb, LANE, tb), F32), pltpu.VMEM((hp * nb, LANE, tb), F32),
                            pltpu.SemaphoreType.DMA((2,))]),
        compiler_params=_params(2),
    )(qi, kj, qn, qr, kn, kr, knt, krt, v, o, do, lse)

    def rows_major(a):
        return a.reshape(nh, nb, LANE, tb).transpose(1, 3, 0, 2).reshape(lp, nh * LANE)

    return rows_major(dqnt), rows_major(dqrt), dkn, dkr, dv


ELEMENTWISE_BLOCK_BYTES = 1 << 20


def _row_tile(rows, width, copies=1):
    for t in (1024, 512, 256, 128, 64, 32, 16, 8):
        if rows % t == 0 and t * width * 4 * copies <= ELEMENTWISE_BLOCK_BYTES:
            return t
    return rows


def _adamw(w, g, m, v, name):
    rows, width = w.shape
    tr = _row_tile(rows, width)

    def body(w_ref, g_ref, m_ref, v_ref, d_ref, nm_ref, nv_ref):
        gg = g_ref[...]
        nm = ADAM_B1 * m_ref[...] + (1.0 - ADAM_B1) * gg
        nv = ADAM_B2 * v_ref[...] + (1.0 - ADAM_B2) * jnp.square(gg)
        m_hat = nm / (1.0 - ADAM_B1 ** ADAM_STEP)
        v_hat = nv / (1.0 - ADAM_B2 ** ADAM_STEP)
        d_ref[...] = -ADAM_LR * (m_hat / (jnp.sqrt(v_hat) + ADAM_EPS) + ADAM_WD * w_ref[...])
        nm_ref[...] = nm
        nv_ref[...] = nv

    spec = pl.BlockSpec((tr, width), lambda r: (r, 0))
    shp = jax.ShapeDtypeStruct((rows, width), F32)
    return _pcall(body, name=name, grid=(rows // tr,), in_specs=[spec] * 4, out_specs=[spec] * 3,
                  out_shape=[shp] * 3, compiler_params=_params(1))(w, g, m, v)


def _add_pair(a, b, name):
    s, rows, width = b.shape
    tr = _row_tile(rows, width)
    nt = rows // tr

    def body(c_ref, a_ref, b_ref, o_ref):
        o_ref[...] = a_ref[...] + b_ref[...]

    spec = pl.BlockSpec((1, tr, width), lambda i, r, c_ref: (i, r, 0))
    return _pcall(
        body, name=name, out_shape=jax.ShapeDtypeStruct(b.shape, F32),
        grid_spec=pltpu.PrefetchScalarGridSpec(
            num_scalar_prefetch=1, grid=(s, nt),
            in_specs=[pl.BlockSpec((1, tr, width), lambda i, r, c_ref: (i, c_ref[0] * nt + r, 0)), spec],
            out_specs=spec),
        compiler_params=_params(2),
    )(_core_index(), a, b)


def _sum_slots(a, name):
    s, rows, width = a.shape
    tr = _row_tile(rows, width, copies=s)

    def body(a_ref, o_ref):
        tot = a_ref[0]
        for k in range(1, s):
            tot = tot + a_ref[k]
        o_ref[...] = tot

    return _pcall(body, name=name, grid=(rows // tr,),
                  in_specs=[pl.BlockSpec((s, tr, width), lambda r: (0, r, 0))],
                  out_specs=pl.BlockSpec((tr, width), lambda r: (r, 0)),
                  out_shape=jax.ShapeDtypeStruct((rows, width), F32), compiler_params=_params(1))(a)


_ANY = pl.BlockSpec(memory_space=pl.ANY)


def _my_place():
    return lax.axis_index("x"), lax.axis_index("y"), lax.axis_index("c")


def _core_index():
    return lax.axis_index("c").astype(jnp.int32).reshape(1)


def _other_chips(x, y):
    return [(1 - x, y), (x, 1 - y), (1 - x, 1 - y)]


def _gather_shards(flat, name):
    rows, width = flat.shape

    def body(x_ref, out_ref, send_sems, recv_sems, local_sem):
        x, y, c = _my_place()
        mine = pltpu.make_async_copy(x_ref, out_ref.at[2 * x + y], local_sem)
        mine.start()
        sends = []
        for k, (px, py) in enumerate(_other_chips(x, y)):
            cp = pltpu.make_async_remote_copy(
                src_ref=x_ref, dst_ref=out_ref.at[2 * x + y], send_sem=send_sems.at[k], recv_sem=recv_sems.at[k],
                device_id=(px, py, c), device_id_type=MESH)
            cp.start()
            sends.append(cp)
        for k, (px, py) in enumerate(_other_chips(x, y)):
            pltpu.make_async_remote_copy(
                src_ref=x_ref, dst_ref=out_ref.at[2 * px + py], send_sem=send_sems.at[k], recv_sem=recv_sems.at[k],
                device_id=(px, py, c), device_id_type=MESH).wait_recv()
        for cp in sends:
            cp.wait_send()
        mine.wait()

    return _pcall(
        body, name=name, in_specs=[_ANY], out_specs=_ANY,
        out_shape=jax.ShapeDtypeStruct((4, rows, width), flat.dtype),
        scratch_shapes=[pltpu.SemaphoreType.DMA((3,)), pltpu.SemaphoreType.DMA((3,)), pltpu.SemaphoreType.DMA],
    )(flat)


def _sibling_split(g, name):
    s, rows, width = g.shape
    half = rows // 2
    tr = _row_tile(half, width)
    nt = half // tr

    def body(c_ref, g_blk, got_ref, send_sem, recv_sem):
        k = pl.program_id(0)
        t = pl.program_id(1)
        x, y, c = _my_place()
        cp = pltpu.make_async_remote_copy(
            src_ref=g_blk.at[0], dst_ref=got_ref.at[k, pl.ds(pl.multiple_of(t * tr, 8), tr), :],
            send_sem=send_sem, recv_sem=recv_sem, device_id=(x, y, 1 - c), device_id_type=MESH)
        cp.start()
        cp.wait_send()

        @pl.when(jnp.logical_and(k == s - 1, t == nt - 1))
        def _():
            pltpu.make_async_remote_copy(
                src_ref=got_ref, dst_ref=got_ref, send_sem=send_sem, recv_sem=recv_sem,
                device_id=(x, y, 1 - c), device_id_type=MESH).wait_recv()

    return _pcall(
        body, name=name, out_shape=jax.ShapeDtypeStruct((s, half, width), g.dtype),
        grid_spec=pltpu.PrefetchScalarGridSpec(
            num_scalar_prefetch=1, grid=(s, nt),
            in_specs=[pl.BlockSpec((1, tr, width), lambda k, t, c_ref: (k, (1 - c_ref[0]) * nt + t, 0))],
            out_specs=_ANY,
            scratch_shapes=[pltpu.SemaphoreType.DMA, pltpu.SemaphoreType.DMA]),
        compiler_params=_params(2),
    )(_core_index(), g)


def _chip_scatter(p, name):
    s, rows, width = p.shape

    def body(p_ref, out_ref, send_sems, recv_sems, local_sem):
        x, y, c = _my_place()
        me = 2 * x + y
        mine = pltpu.make_async_copy(p_ref.at[me], out_ref.at[me], local_sem)
        mine.start()
        sends = []
        for k, (px, py) in enumerate(_other_chips(x, y)):
            cp = pltpu.make_async_remote_copy(
                src_ref=p_ref.at[2 * px + py], dst_ref=out_ref.at[me], send_sem=send_sems.at[k],
                recv_sem=recv_sems.at[k], device_id=(px, py, c), device_id_type=MESH)
            cp.start()
            sends.append(cp)
        for k, (px, py) in enumerate(_other_chips(x, y)):
            pltpu.make_async_remote_copy(
                src_ref=p_ref.at[me], dst_ref=out_ref.at[2 * px + py], send_sem=send_sems.at[k],
                recv_sem=recv_sems.at[k], device_id=(px, py, c), device_id_type=MESH).wait_recv()
        for cp in sends:
            cp.wait_send()
        mine.wait()

    return _pcall(
        body, name=name, in_specs=[_ANY], out_specs=_ANY,
        out_shape=jax.ShapeDtypeStruct(p.shape, p.dtype),
        scratch_shapes=[pltpu.SemaphoreType.DMA((3,)), pltpu.SemaphoreType.DMA((3,)), pltpu.SemaphoreType.DMA],
    )(p)


def _sibling_join(qh, name):
    half, width = qh.shape
    tr = _row_tile(half, width)
    nt = half // tr

    def body(q_blk, out_ref, send_sem, recv_sem, local_sem):
        t = pl.program_id(0)
        x, y, c = _my_place()
        dst = out_ref.at[pl.ds(pl.multiple_of(c * half + t * tr, 8), tr), :]
        cp = pltpu.make_async_remote_copy(
            src_ref=q_blk, dst_ref=dst, send_sem=send_sem, recv_sem=recv_sem,
            device_id=(x, y, 1 - c), device_id_type=MESH)
        cp.start()
        mine = pltpu.make_async_copy(q_blk, dst, local_sem)
        mine.start()
        cp.wait_send()
        mine.wait()

        @pl.when(t == nt - 1)
        def _():
            theirs = out_ref.at[pl.ds(pl.multiple_of((1 - c) * half, 8), half), :]
            pltpu.make_async_remote_copy(
                src_ref=theirs, dst_ref=theirs, send_sem=send_sem, recv_sem=recv_sem,
                device_id=(x, y, 1 - c), device_id_type=MESH).wait_recv()

    return _pcall(
        body, name=name, grid=(nt,),
        in_specs=[pl.BlockSpec((tr, width), lambda t: (t, 0))], out_specs=_ANY,
        out_shape=jax.ShapeDtypeStruct((2 * half, width), qh.dtype),
        scratch_shapes=[pltpu.SemaphoreType.DMA, pltpu.SemaphoreType.DMA, pltpu.SemaphoreType.DMA],
        compiler_params=_params(1),
    )(qh)


def _all_sum_small(part, name):
    rows, width = part.shape

    def body(p_ref, out_ref, land, send_sems, recv_sems):
        x, y, c = _my_place()
        me = 4 * x + 2 * y + c
        land[me] = p_ref[...]
        sends = []
        for k in range(1, 8):
            peer = (x ^ (k >> 2), y ^ ((k >> 1) & 1), c ^ (k & 1))
            cp = pltpu.make_async_remote_copy(
                src_ref=p_ref, dst_ref=land.at[me], send_sem=send_sems.at[k - 1], recv_sem=recv_sems.at[k - 1],
                device_id=peer, device_id_type=MESH)
            cp.start()
            sends.append(cp)
        for k in range(1, 8):
            px, py, pc = x ^ (k >> 2), y ^ ((k >> 1) & 1), c ^ (k & 1)
            pltpu.make_async_remote_copy(
                src_ref=p_ref, dst_ref=land.at[4 * px + 2 * py + pc], send_sem=send_sems.at[k - 1],
                recv_sem=recv_sems.at[k - 1], device_id=(px, py, pc), device_id_type=MESH).wait_recv()
        for cp in sends:
            cp.wait_send()
        tot = land[0]
        for k in range(1, 8):
            tot = tot + land[k]
        out_ref[...] = tot

    vmem = pl.BlockSpec(memory_space=pltpu.VMEM)
    return _pcall(
        body, name=name, in_specs=[vmem], out_specs=vmem,
        out_shape=jax.ShapeDtypeStruct((rows, width), F32),
        scratch_shapes=[pltpu.VMEM((8, rows, width), F32), pltpu.SemaphoreType.DMA((7,)),
                        pltpu.SemaphoreType.DMA((7,))],
    )(part)


def _big_layout(shards):
    return [(a.shape[0], a.shape[1], ax) for a, ax in shards]


FLAT_ROW_MULTIPLE = 2048


def _pack_shards(arrs, row_multiple=FLAT_ROW_MULTIPLE):
    flat = jnp.concatenate([a.reshape(-1) for a in arrs])
    return jnp.pad(flat, (0, -flat.shape[0] % (row_multiple * LANE))).reshape(-1, LANE)


def _unpack_shards(flat, layout):
    flat = flat.reshape(-1)
    out, off = [], 0
    for r, c, _ in layout:
        out.append(flat[off:off + r * c].reshape(r, c))
        off += r * c
    return out


def _unpack_full(gathered, layout):
    g = gathered.reshape(4, -1)
    out, off = [], 0
    for r, c, ax in layout:
        seg = g[:, off:off + r * c].reshape(4, r, c)
        out.append(seg.transpose(1, 0, 2).reshape(r, 4 * c) if ax == 1 else seg.reshape(4 * r, c))
        off += r * c
    return out


def _pack_full(fulls, layout):
    parts = []
    for a, (r, c, ax) in zip(fulls, layout):
        if ax == 1:
            parts.append(a.reshape(r, 4, c).transpose(1, 0, 2).reshape(4, r * c))
        else:
            parts.append(a.reshape(4, r * c))
    flat = jnp.concatenate(parts, axis=1)
    return jnp.pad(flat, ((0, 0), (0, -flat.shape[1] % (FLAT_ROW_MULTIPLE * LANE)))).reshape(4, -1, LANE)


def _pad_lanes(a, width=LANE):
    return jnp.pad(a, [(0, 0)] * (a.ndim - 1) + [(0, width - a.shape[-1])])


def _pack_small(arrs):
    rows = [_pad_lanes(a.reshape(1, -1), -(-a.size // LANE) * LANE).reshape(-1, LANE) for a in arrs]
    flat = jnp.concatenate(rows, axis=0)
    return jnp.pad(flat, ((0, -flat.shape[0] % 8), (0, 0)))


def _unpack_small(flat, shapes):
    out, off = [], 0
    for shp in shapes:
        n = math.prod(shp)
        nr = -(-n // LANE)
        out.append(flat[off:off + nr].reshape(-1)[:n].reshape(shp))
        off += nr
    return out


def kernel(x, meta_tokens, pre_norm, post_norm, gdn_w_in, gdn_conv_w, gdn_a_log, gdn_dt_bias, gdn_out_norm, gdn_w_out, kv_norm, kv_w_down, kv_latent_norm, kv_w_up, mla_w_in, mla_q_latent_norm, mla_w_q_up, mla_w_out, loss_target, m_meta_tokens, m_pre_norm, m_post_norm, m_gdn_w_in, m_gdn_conv_w, m_gdn_a_log, m_gdn_dt_bias, m_gdn_out_norm, m_gdn_w_out, m_kv_norm, m_kv_w_down, m_kv_latent_norm, m_kv_w_up, m_mla_w_in, m_mla_q_latent_norm, m_mla_w_q_up, m_mla_w_out, v_meta_tokens, v_pre_norm, v_post_norm, v_gdn_w_in, v_gdn_conv_w, v_gdn_a_log, v_gdn_dt_bias, v_gdn_out_norm, v_gdn_w_out, v_kv_norm, v_kv_w_down, v_kv_latent_norm, v_kv_w_up, v_mla_w_in, v_mla_q_latent_norm, v_mla_w_q_up, v_mla_w_out):
    seq = x.shape[1]
    d = D_MODEL
    lp = -(-(ROW0 + seq) // ROW_ALIGN) * ROW_ALIGN
    tail = lp - ROW0 - seq

    big_names = ["meta_tokens", "gdn_conv_w", "gdn_w_out", "kv_w_down", "kv_w_up", "mla_w_in", "mla_w_q_up",
                 "mla_w_out"]
    big_axis = [1, 1, 0, 0, 1, 1, 1, 0]
    big_w = [meta_tokens, gdn_conv_w[0], gdn_w_out[0], kv_w_down, kv_w_up, mla_w_in[0], mla_w_q_up[0], mla_w_out[0]]
    big_m = [m_meta_tokens, m_gdn_conv_w[0], m_gdn_w_out[0], m_kv_w_down, m_kv_w_up, m_mla_w_in[0], m_mla_w_q_up[0],
             m_mla_w_out[0]]
    big_v = [v_meta_tokens, v_gdn_conv_w[0], v_gdn_w_out[0], v_kv_w_down, v_kv_w_up, v_mla_w_in[0], v_mla_w_q_up[0],
             v_mla_w_out[0]]
    layout = _big_layout(list(zip(big_w, big_axis)))
    w_flat = _pack_shards(big_w)
    meta_f, conv_w = _unpack_full(
        _gather_shards(_pack_shards(big_w[:2], row_multiple=16), "gather_meta_conv"), layout[:2])
    mm_shards = [w.astype(BF16) for w in big_w[2:6]] + [(big_w[6] * Q_PRESCALE).astype(BF16), big_w[7].astype(BF16)]
    (w_out0, kv_down, kv_up, w_in1, w_qup, w_out1) = _unpack_full(
        _gather_shards(_pack_shards(mm_shards), "gather_weights"), layout[2:])
    w_in0_shards = _gather_shards(gdn_w_in[0].astype(BF16), "gather_gdn_w_in")
    w_in0 = jnp.concatenate([w_in0_shards[s] for s in range(4)], axis=1)
    win_cols = gdn_w_in.shape[2]

    nv = GDN_V_HEADS
    w_qkv = w_in0[:, :GDN_CONV_W]
    w_z0 = w_in0[:, GDN_CONV_W:GDN_CONV_W + GDN_V_W]
    w_b = _pad_lanes(w_in0[:, GDN_CONV_W + GDN_V_W:GDN_CONV_W + GDN_V_W + nv])
    w_a = _pad_lanes(w_in0[:, GDN_CONV_W + GDN_V_W + nv:])
    w_ckv = kv_down[:, :MLA_KV_RANK]
    w_kr = _pad_lanes(kv_down[:, MLA_KV_RANK:])
    kvu = kv_up.reshape(MLA_KV_RANK, MLA_HEADS, 2 * LANE)
    w_kn = kvu[:, :, :LANE].reshape(MLA_KV_RANK, MLA_HEADS * LANE)
    w_v = kvu[:, :, LANE:].reshape(MLA_KV_RANK, MLA_HEADS * LANE)
    w_cq = w_in1[:, :MLA_Q_RANK]
    w_z1 = w_in1[:, MLA_Q_RANK:]
    qu = w_qup.reshape(MLA_Q_RANK, MLA_HEADS, MLA_QK)
    w_qn = qu[:, :, :MLA_NOPE].reshape(MLA_Q_RANK, MLA_HEADS * LANE)
    w_qr = _pad_lanes(qu[:, :, MLA_NOPE:]).reshape(MLA_Q_RANK, MLA_HEADS * LANE)

    pre0, pre1 = pre_norm[0:1], pre_norm[1:2]
    post0, post1 = post_norm[0:1], post_norm[1:2]
    a_log = _pad_lanes(gdn_a_log)
    dt_bias = _pad_lanes(gdn_dt_bias)
    kvn = kv_norm.reshape(1, d)
    kvl = kv_latent_norm.reshape(1, MLA_KV_RANK)
    qln = mla_q_latent_norm

    h0 = jnp.concatenate([jnp.zeros((FRONT, d), F32), meta_f, x[0], jnp.zeros((tail, d), F32)], axis=0)
    tgt = jnp.pad(loss_target[0], ((ROW0, tail), (0, 0)))
    pos = jnp.maximum(jnp.arange(lp, dtype=jnp.int32) - FRONT, 0).astype(F32)
    inv = ROPE_THETA ** (-jnp.arange(0, MLA_ROPE, 2, dtype=F32) / MLA_ROPE)
    ang = pos[:, None] * inv[None, :]
    zeros64 = jnp.zeros((lp, LANE - MLA_ROPE), F32)
    cos_t = jnp.concatenate([jnp.cos(ang), jnp.cos(ang), zeros64], axis=1)
    sin_t = jnp.concatenate([-jnp.sin(ang), jnp.sin(ang), zeros64], axis=1)

    def valid_rows(ridx):
        return jnp.logical_and(ridx >= FRONT, ridx < ROW0 + seq)

    def f_pre0(ridx, g, h, gain):
        return _rms(h, gain), h

    (hn0,) = _rowwise("pre0", lambda *a: f_pre0(*a)[:1], [_In(h0), _In(pre0, "const")],
                      [_Out("row", (lp, d), BF16)])
    qkv_raw = _mm(hn0, w_qkv, "nn", "gdn_in_qkv")
    z0 = _mm(hn0, w_z0, "nn", "gdn_in_z")
    b_raw = _mm(hn0, w_b, "nn", "gdn_in_b")
    a_raw = _mm(hn0, w_a, "nn", "gdn_in_a")

    def f_ba(ridx, g, b, a, alog, dtb):
        tr = b.shape[0]
        ok = valid_rows(ridx).astype(F32)
        beta = jax.nn.sigmoid(b) * ok
        gate = -jnp.exp(alog) * _softplus(a + dtb) * ok
        ii = lax.broadcasted_iota(jnp.int32, (tr, tr), 0)
        jj = lax.broadcasted_iota(jnp.int32, (tr, tr), 1)
        tri = jnp.logical_and((ii >> 6) == (jj >> 6), ii >= jj).astype(F32)
        return beta, _hdot(tri, gate)

    ba_ins = [_In(b_raw), _In(a_raw), _In(a_log, "const"), _In(dt_bias, "const")]
    beta, gc = _rowwise("gdn_gates", f_ba, ba_ins, [_Out("row", (lp, LANE)), _Out("row", (lp, LANE))])
    qkv = _conv_fwd(qkv_raw, conv_w, "gdn_conv")
    o0, ckpt = _gdn_fwd(qkv, beta, gc, "gdn_scan")

    def per_head(fn, *arrs):
        n = arrs[0].shape[1] // LANE
        return jnp.concatenate([fn(*[a[:, i * LANE:(i + 1) * LANE] for a in arrs]) for i in range(n)], axis=1)

    def f_gate0(ridx, g, o, z, gain):
        return (per_head(lambda oh, zh: _rms(oh, gain) * _silu(zh), o, z),)

    gate0_ins = [_In(o0), _In(z0), _In(gdn_out_norm, "const")]
    (gated0,) = _rowwise("gdn_gate", f_gate0, gate0_ins, [_Out("row", (lp, GDN_V_W), BF16)])
    y0 = _mm(gated0, w_out0, "nn", "gdn_out")

    def f_mid(ridx, g, h, y, g_post, g_pre, g_kv):
        h1 = h + _rms(y, g_post)
        return h1, _rms(h1, g_pre), _rms(h1, g_kv)

    mid_ins = [_In(h0), _In(y0), _In(post0, "const"), _In(pre1, "const"), _In(kvn, "const")]
    h1, hn1, hkv = _rowwise("mid", f_mid, mid_ins,
                            [_Out("row", (lp, d)), _Out("row", (lp, d), BF16), _Out("row", (lp, d), BF16)])

    ckv_raw = _mm(hkv, w_ckv, "nn", "kv_down_c")
    kr_raw = _mm(hkv, w_kr, "nn", "kv_down_r")

    def f_ckv(ridx, g, c, r, cs, sn, gain):
        return _rms(c, gain), _rope(r, cs, sn)

    ckv_ins = [_In(ckv_raw), _In(kr_raw), _In(cos_t), _In(sin_t), _In(kvl, "const")]
    ckv, kr = _rowwise("kv_latent", f_ckv, ckv_ins, [_Out("row", (lp, LANE)), _Out("row", (lp, LANE), BF16)],
                       tr=640)
    kn = _mm(ckv, w_kn, "nn", "kv_up_k", BF16)
    vv = _mm(ckv, w_v, "nn", "kv_up_v", BF16)
    cq_raw = _mm(hn1, w_cq, "nn", "mla_in_q")
    z1 = _mm(hn1, w_z1, "nn", "mla_in_z")

    def f_cq(ridx, g, c, gain):
        return (_rms(c, gain),)

    cq_ins = [_In(cq_raw), _In(qln, "const")]
    (cq,) = _rowwise("q_latent", f_cq, cq_ins, [_Out("row", (lp, MLA_Q_RANK))], tr=640)
    qn = _mm(cq, w_qn, "nn", "q_up_n", BF16)
    qr_raw = _mm(cq, w_qr, "nn", "q_up_r")

    def f_qrope(ridx, g, r, cs, sn):
        return (per_head(lambda rh: _rope(rh, cs, sn), r),)

    qr_ins = [_In(qr_raw), _In(cos_t), _In(sin_t)]
    (qr,) = _rowwise("q_rope", f_qrope, qr_ins, [_Out("row", (lp, MLA_HEADS * LANE), BF16)])
    o1, lse = _flash_fwd(qn, qr, kn, kr, vv, "attention")

    def f_gate1(ridx, g, o, z):
        return (o * _silu(z),)

    gate1_ins = [_In(o1), _In(z1)]
    (og,) = _rowwise("mla_gate", f_gate1, gate1_ins, [_Out("row", (lp, MLA_HEADS * LANE), BF16)])
    y1 = _mm(og, w_out1, "nn", "mla_out")

    def f_final(ridx, g, h, y, t, gain):
        ok = jnp.logical_and(ridx >= ROW0, ridx < ROW0 + seq).astype(F32)

        def rows_loss(h_, y_, gain_):
            err = (h_ + _rms(y_, gain_) - t) * ok
            return 0.5 * jnp.sum(jnp.sum(err * err, axis=1, keepdims=True), axis=0, keepdims=True) / d

        val, vjp = jax.vjp(rows_loss, h, y, gain)
        dh, dy, dgain = vjp(jnp.ones((1, 1), F32))
        return dh, dy, dgain, jnp.broadcast_to(val, (1, LANE))

    dh2, dy1, dpost1, loss_part = _rowwise(
        "loss_head", f_final, [_In(h1), _In(y1), _In(tgt), _In(post1, "const")],
        [_Out("row", (lp, d)), _Out("row", (lp, d)), _Out("acc", (1, d)), _Out("acc", (1, LANE))])

    dog = _mm(dy1, w_out1, "nt", "mla_out_dx")
    dw_out1 = _mm(og, dy1, "tn", "mla_out_dw")
    do1, dz1 = _rowwise_vjp("mla_gate_bwd", f_gate1, gate1_ins, [[dog]], [0, 1])
    dqn, dqr, dkn, dkr, dvv = _flash_bwd(qn, qr, kn, kr, vv, o1, do1, lse, "attention_bwd")
    (dqr_raw,) = _rowwise_vjp("q_rope_bwd", f_qrope, qr_ins, [[dqr]], [0])
    dcq_a = _mm(dqn, w_qn, "nt", "q_up_n_dx")
    dcq_b = _mm(dqr_raw, w_qr, "nt", "q_up_r_dx")
    dw_qn = _mm(cq, dqn, "tn", "q_up_n_dw") * Q_PRESCALE
    dw_qr = _mm(cq, dqr_raw, "tn", "q_up_r_dw") * Q_PRESCALE
    dcq_raw, dqln = _rowwise_vjp("q_latent_bwd", f_cq, cq_ins, [[dcq_a, dcq_b]], [0, 1], tr=640)
    dhn1_a = _mm(dcq_raw, w_cq, "nt", "mla_in_q_dx")
    dhn1_b = _mm(dz1, w_z1, "nt", "mla_in_z_dx")
    dw_cq = _mm(hn1, dcq_raw, "tn", "mla_in_q_dw")
    dw_z1 = _mm(hn1, dz1, "tn", "mla_in_z_dw")
    dckv_a = _mm(dkn, w_kn, "nt", "kv_up_k_dx")
    dckv_b = _mm(dvv, w_v, "nt", "kv_up_v_dx")
    dw_kn = _mm(ckv, dkn, "tn", "kv_up_k_dw")
    dw_v = _mm(ckv, dvv, "tn", "kv_up_v_dw")
    dckv_raw, dkr_raw, dkvl = _rowwise_vjp("kv_latent_bwd", f_ckv, ckv_ins, [[dckv_a, dckv_b], [dkr]], [0, 1, 4],
                                           tr=640)
    dhkv_a = _mm(dckv_raw, w_ckv, "nt", "kv_down_c_dx")
    dhkv_b = _mm(dkr_raw, w_kr, "nt", "kv_down_r_dx")
    dw_ckv = _mm(hkv, dckv_raw, "tn", "kv_down_c_dw")
    dw_kr = _mm(hkv, dkr_raw, "tn", "kv_down_r_dw")
    dh0_res, dy0, dpost0, dpre1, dkvn = _rowwise_vjp(
        "mid_bwd", f_mid, mid_ins, [[dh2], [dhn1_a, dhn1_b], [dhkv_a, dhkv_b]], [0, 1, 2, 3, 4])

    dgated0 = _mm(dy0, w_out0, "nt", "gdn_out_dx")
    dw_out0 = _mm(gated0, dy0, "tn", "gdn_out_dw")
    do0, dz0, doutn = _rowwise_vjp("gdn_gate_bwd", f_gate0, gate0_ins, [[dgated0]], [0, 1, 2], tr=160)
    dq0, dk0, dv0, dbeta, dgc = _gdn_bwd(qkv, beta, gc, ckpt, do0, "gdn_scan_bwd")
    db_raw, da_raw, dalog, ddtb = _rowwise_vjp("gdn_gates_bwd", f_ba, ba_ins, [[dbeta], [dgc]], [0, 1, 2, 3])
    dqkv_raw, dconv = _conv_bwd(qkv_raw, conv_w, dq0, dk0, dv0, "gdn_conv_bwd")
    dhn0_a = _mm(dqkv_raw, w_qkv, "nt", "gdn_in_qkv_dx")
    dhn0_b = _mm(dz0, w_z0, "nt", "gdn_in_z_dx")
    dhn0_c = _mm(db_raw, w_b, "nt", "gdn_in_b_dx")
    dhn0_d = _mm(da_raw, w_a, "nt", "gdn_in_a_dx")
    dw_qkv = _mm(hn0, dqkv_raw, "tn", "gdn_in_qkv_dw")
    dw_z0 = _mm(hn0, dz0, "tn", "gdn_in_z_dw")
    dw_b = _mm(hn0, db_raw, "tn", "gdn_in_b_dw")
    dw_a = _mm(hn0, da_raw, "tn", "gdn_in_a_dw")
    dh0, dpre0 = _rowwise_vjp("pre0_bwd", f_pre0, [_In(h0), _In(pre0, "const")],
                              [[dhn0_a, dhn0_b, dhn0_c, dhn0_d], [dh0_res]], [0, 1])

    grad_x = dh0[ROW0:ROW0 + seq][None]
    g_meta = dh0[FRONT:ROW0]
    g_w_in0 = jnp.concatenate([dw_qkv, dw_z0, dw_b[:, :nv], dw_a[:, :nv]], axis=1)
    g_kv_down = jnp.concatenate([dw_ckv, dw_kr[:, :MLA_ROPE]], axis=1)
    g_kv_up = jnp.concatenate([dw_kn.reshape(MLA_KV_RANK, MLA_HEADS, LANE), dw_v.reshape(MLA_KV_RANK, MLA_HEADS, LANE)],
                              axis=2).reshape(MLA_KV_RANK, MLA_HEADS * 2 * LANE)
    g_w_in1 = jnp.concatenate([dw_cq, dw_z1], axis=1)
    g_qup = jnp.concatenate([dw_qn.reshape(MLA_Q_RANK, MLA_HEADS, LANE),
                             dw_qr.reshape(MLA_Q_RANK, MLA_HEADS, LANE)[:, :, :MLA_ROPE]],
                            axis=2).reshape(MLA_Q_RANK, MLA_HEADS * MLA_QK)
    big_g = [g_meta, dconv, dw_out0, g_kv_down, g_kv_up, g_w_in1, g_qup, dw_out1]

    def reduce_to_shard(g_by_chip, tag):
        got = _sibling_split(g_by_chip, "grads_sibling_split" + tag)
        chip_part = _add_pair(g_by_chip, got, "grads_chip_sum" + tag)
        from_chips = _chip_scatter(chip_part, "grads_chip_scatter" + tag)
        half_sum = _sum_slots(from_chips, "grads_total" + tag)
        return _sibling_join(half_sum, "grads_sibling_join" + tag)

    g_flat = reduce_to_shard(_pack_full(big_g, layout), "")
    g_win = reduce_to_shard(jnp.stack([g_w_in0[:, s * win_cols:(s + 1) * win_cols] for s in range(4)]), "_gdn_w_in")

    small_shapes = [(2, d), (2, d), (1, nv), (1, nv), (1, GDN_DK), (d,), (MLA_KV_RANK,), (1, MLA_Q_RANK), (1, LANE)]
    small_part = _pack_small([jnp.concatenate([dpre0, dpre1], axis=0), jnp.concatenate([dpost0, dpost1], axis=0),
                              dalog[:, :nv], ddtb[:, :nv], doutn, dkvn, dkvl, dqln, loss_part])
    small_tot = _all_sum_small(small_part, "small_sum")
    small_g = _unpack_small(small_tot, small_shapes)
    loss = small_g[-1][0, 0]

    d_flat, m_flat, v_flat = _adamw(w_flat, g_flat, _pack_shards(big_m), _pack_shards(big_v), "adamw_sharded")
    win_step = _adamw(gdn_w_in[0], g_win, m_gdn_w_in[0], v_gdn_w_in[0], "adamw_gdn_w_in")
    small_w = [pre_norm, post_norm, gdn_a_log, gdn_dt_bias, gdn_out_norm, kv_norm, kv_latent_norm, mla_q_latent_norm]
    small_m = [m_pre_norm, m_post_norm, m_gdn_a_log, m_gdn_dt_bias, m_gdn_out_norm, m_kv_norm, m_kv_latent_norm,
               m_mla_q_latent_norm]
    small_v = [v_pre_norm, v_post_norm, v_gdn_a_log, v_gdn_dt_bias, v_gdn_out_norm, v_kv_norm, v_kv_latent_norm,
               v_mla_q_latent_norm]
    g_small_flat = _pack_small(small_g[:-1])
    ds_flat, ms_flat, vs_flat = _adamw(_pack_small(small_w), g_small_flat, _pack_small(small_m), _pack_small(small_v),
                                       "adamw_replicated")

    def assemble(big_flat, small_flat, win):
        bigs = dict(zip(big_names, [a.reshape(w.shape) for a, w in zip(
            _unpack_shards(big_flat, layout),
            [meta_tokens, gdn_conv_w, gdn_w_out, kv_w_down, kv_w_up, mla_w_in, mla_w_q_up, mla_w_out])]))
        smalls = dict(zip(["pre_norm", "post_norm", "gdn_a_log", "gdn_dt_bias", "gdn_out_norm", "kv_norm",
                           "kv_latent_norm", "mla_q_latent_norm"], _unpack_small(small_flat, small_shapes[:-1])))
        both = {**bigs, **smalls, "gdn_w_in": win[None]}
        order = ["meta_tokens", "pre_norm", "post_norm", "gdn_w_in", "gdn_conv_w", "gdn_a_log", "gdn_dt_bias",
                 "gdn_out_norm", "gdn_w_out", "kv_norm", "kv_w_down", "kv_latent_norm", "kv_w_up", "mla_w_in",
                 "mla_q_latent_norm", "mla_w_q_up", "mla_w_out"]
        return [both[n] for n in order]

    grads = assemble(g_flat, g_small_flat, g_win)
    deltas = assemble(d_flat, ds_flat, win_step[0])
    new_m = assemble(m_flat, ms_flat, win_step[1])
    new_v = assemble(v_flat, vs_flat, win_step[2])
    return (loss, grad_x, *grads, *deltas, *new_m, *new_v)
```

```python
import functools
import math

import jax
import jax.numpy as jnp
from jax import lax
from jax.experimental import pallas as pl
from jax.experimental.pallas import tpu as pltpu

F32 = jnp.float32
BF16 = jnp.bfloat16
MESH = pl.DeviceIdType.MESH

D_MODEL = 1024
N_META = 16
FRONT = 48
ROW0 = FRONT + N_META
ROW_ALIGN = 768
TR_FULL, TR_HALF, TR_QUARTER = ROW_ALIGN, ROW_ALIGN // 2, ROW_ALIGN // 4
NORM_EPS = 1e-6
LANE = 128

GDN_QK_HEADS = 8
GDN_V_HEADS = 16
GDN_DK = 128
GDN_CHUNK = 64
GDN_QK_W = 1024
GDN_V_W = 2048
GDN_CONV_W = 4096

MLA_HEADS = 16
MLA_NOPE = 128
MLA_ROPE = 64
MLA_QK = 192
MLA_Q_RANK = 256
MLA_KV_RANK = 128
ROPE_THETA = 10000.0

ADAM_LR = 0.001
ADAM_B1 = 0.9
ADAM_B2 = 0.999
ADAM_EPS = 1e-08
ADAM_WD = 0.01
ADAM_STEP = 10

VMEM_LIMIT_V7X = 56 * 1024 * 1024
NEG = -1e30

_NN = ((1,), (0,))
_NT = ((1,), (1,))
_TN = ((0,), (0,))
_HI = lax.Precision.HIGHEST
_X3 = lax.Precision.HIGH


def _pcall(body, **kw):
    return pl.pallas_call(body, **kw)


def _params(n_axes):
    return pltpu.CompilerParams(dimension_semantics=("arbitrary",) * n_axes, vmem_limit_bytes=VMEM_LIMIT_V7X)


def _dot(a, b, dims, prec=None):
    return lax.dot_general(a, b, (dims, ((), ())), precision=prec, preferred_element_type=F32)


def _bdot(a, b, dims):
    return _dot(a.astype(BF16), b.astype(BF16), dims)


def _hdot(a, b, dims=_NN):
    return _dot(a, b, dims, _HI)


def _fdot(a, b, dims):
    return _dot(a, b, dims)


SMALL_MATMUL_DIM = 256
SMALL_MATMUL_ROWS = 1408


def _tile(n):
    if n % ROW_ALIGN == 0:
        return ROW_ALIGN
    for t in (1024, 512, 256, 128):
        if n % t == 0:
            return t
    raise ValueError(n)


def _mm(a, b, mode, name, out_dtype=F32):
    if mode == "nn":
        (m, k), (k2, n) = a.shape, b.shape
    elif mode == "nt":
        (m, k), (n, k2) = a.shape, b.shape
    else:
        (k, m), (k2, n) = a.shape, b.shape
    assert k == k2, (a.shape, b.shape, mode)
    tm, tn, tk = _tile(m), _tile(n), _tile(k)
    if mode != "tn" and min(k, n) <= SMALL_MATMUL_DIM and m % SMALL_MATMUL_ROWS == 0:
        tm = SMALL_MATMUL_ROWS
    nk = k // tk
    dims = {"nn": _NN, "nt": _NT, "tn": _TN}[mode]

    def body(a_ref, b_ref, o_ref, acc):
        kk = pl.program_id(2)

        @pl.when(kk == 0)
        def _():
            acc[...] = jnp.zeros_like(acc)

        acc[...] += _bdot(a_ref[...], b_ref[...], dims)

        @pl.when(kk == nk - 1)
        def _():
            o_ref[...] = acc[...].astype(out_dtype)

    if mode == "tn":
        a_spec = pl.BlockSpec((tk, tm), lambda i, j, kk: (kk, i))
    else:
        a_spec = pl.BlockSpec((tm, tk), lambda i, j, kk: (i, kk))
    if mode == "nt":
        b_spec = pl.BlockSpec((tn, tk), lambda i, j, kk: (j, kk))
    else:
        b_spec = pl.BlockSpec((tk, tn), lambda i, j, kk: (kk, j))
    return _pcall(
        body, name=name, grid=(m // tm, n // tn, nk),
        in_specs=[a_spec, b_spec],
        out_specs=pl.BlockSpec((tm, tn), lambda i, j, kk: (i, j)),
        out_shape=jax.ShapeDtypeStruct((m, n), out_dtype),
        scratch_shapes=[pltpu.VMEM((tm, tn), F32)],
        compiler_params=_params(3),
    )(a, b)


class _In:
    def __init__(self, arr, kind="row", grouped=False, goff=0):
        self.arr, self.kind, self.grouped, self.goff = arr, kind, grouped, goff


class _Out:
    def __init__(self, kind, shape, dtype=F32, grouped=False):
        self.kind, self.shape, self.dtype, self.grouped = kind, shape, dtype, grouped


def _rowwise(name, fn, ins, outs, *, groups=1, tr=TR_HALF):
    lp = next(i.arr.shape[0] for i in ins if i.kind == "row")
    nr = lp // tr
    assert lp % tr == 0

    def in_spec(i):
        w = i.arr.shape[1]
        if i.kind == "row":
            if i.grouped:
                return pl.BlockSpec((tr, LANE), lambda g, r, o=i.goff: (r, g + o))
            return pl.BlockSpec((tr, w), lambda g, r: (r, 0))
        if i.grouped:
            return pl.BlockSpec((i.arr.shape[0], LANE), lambda g, r, o=i.goff: (0, g + o))
        return pl.BlockSpec(i.arr.shape, lambda g, r: (0, 0))

    def out_spec(o):
        if o.kind == "row":
            if o.grouped:
                return pl.BlockSpec((tr, LANE), lambda g, r: (r, g))
            assert groups == 1
            return pl.BlockSpec((tr, o.shape[1]), lambda g, r: (r, 0))
        if o.grouped:
            return pl.BlockSpec((o.shape[0], LANE), lambda g, r: (0, g))
        return pl.BlockSpec(o.shape, lambda g, r: (0, 0))

    n_in = len(ins)

    def body(*refs):
        g = pl.program_id(0)
        r = pl.program_id(1)
        ridx = r * tr + lax.broadcasted_iota(jnp.int32, (tr, 1), 0)
        res = fn(ridx, g, *[ref[...] for ref in refs[:n_in]])
        assert len(res) == len(outs), (name, len(res), len(outs))
        for o, ref, val in zip(outs, refs[n_in:], res):
            if o.kind == "row":
                ref[...] = val.astype(o.dtype)
            else:
                first = (r == 0) if o.grouped else jnp.logical_and(r == 0, g == 0)

                @pl.when(first)
                def _(ref=ref, val=val):
                    ref[...] = val.astype(F32)

                @pl.when(jnp.logical_not(first))
                def _(ref=ref, val=val):
                    ref[...] += val.astype(F32)

    res = _pcall(
        body, name=name, grid=(groups, nr),
        in_specs=[in_spec(i) for i in ins],
        out_specs=[out_spec(o) for o in outs],
        out_shape=[jax.ShapeDtypeStruct(o.shape, o.dtype) for o in outs],
        compiler_params=_params(2),
    )(*[i.arr for i in ins])
    return res


def _rowwise_vjp(name, fn, ins, cots, diff, *, groups=1, tr=TR_HALF):
    n_in = len(ins)
    grouped = groups > 1
    cot_ins = []
    counts = []
    for arrs in cots:
        counts.append(len(arrs))
        for a in arrs:
            cot_ins.append(_In(a, "row", grouped=grouped and a.shape[1] > LANE))
    lp = next(i.arr.shape[0] for i in ins if i.kind == "row")
    outs = []
    for d in diff:
        i = ins[d]
        if i.kind == "row":
            w = groups * LANE if i.grouped else i.arr.shape[1]
            outs.append(_Out("row", (lp, w), F32, grouped=i.grouped))
        else:
            outs.append(_Out("acc", i.arr.shape, F32, grouped=i.grouped))

    def bfn(ridx, g, *allvals):
        vals = list(allvals[:n_in])
        cvals = allvals[n_in:]

        def f(*dv):
            full = list(vals)
            for i, v in zip(diff, dv):
                full[i] = v
            return tuple(fn(ridx, g, *full))

        primal, vjp = jax.vjp(f, *[vals[i].astype(F32) for i in diff])
        cts = []
        pos = 0
        for k, cnt in enumerate(counts):
            if cnt == 0:
                cts.append(jnp.zeros_like(primal[k]))
            else:
                c = cvals[pos].astype(F32)
                for extra in cvals[pos + 1:pos + cnt]:
                    c = c + extra.astype(F32)
                w = primal[k].shape[1]
                if c.shape[1] != w:
                    c = functools.reduce(jnp.add, [c[:, i * w:(i + 1) * w] for i in range(c.shape[1] // w)])
                cts.append(c.astype(primal[k].dtype))
            pos += cnt
        return vjp(tuple(cts))

    return _rowwise(name, bfn, list(ins) + cot_ins, outs, groups=groups, tr=tr)


def _rms(x, g):
    return x * lax.rsqrt(jnp.mean(x * x, axis=-1, keepdims=True) + NORM_EPS) * g


def _silu(x):
    return x * jax.nn.sigmoid(x)


def _softplus(x):
    return jnp.maximum(x, 0.0) + jnp.log(1.0 + jnp.exp(-jnp.abs(x)))


def _swap_halves(x):
    lane = lax.broadcasted_iota(jnp.int32, x.shape, x.ndim - 1)
    return jnp.where(lane < 32, pltpu.roll(x, LANE - 32, x.ndim - 1), pltpu.roll(x, 32, x.ndim - 1))


@jax.custom_vjp
def _rope(x, c, s):
    return x * c + _swap_halves(x) * s


def _rope_fwd(x, c, s):
    return _rope(x, c, s), (c, s)


def _rope_bwd(res, dy):
    c, s = res
    return dy * c + _swap_halves(dy * s), jnp.zeros_like(c), jnp.zeros_like(s)


_rope.defvjp(_rope_fwd, _rope_bwd)


def _conv_post(c, g):
    s = _silu(c)
    n = s * lax.rsqrt(jnp.sum(s * s, axis=-1, keepdims=True) + NORM_EPS)
    return jnp.where(g < GDN_QK_HEADS, n * (GDN_DK ** -0.5), jnp.where(g < 2 * GDN_QK_HEADS, n, s))


def _conv_taps(xe, w):
    c = xe[8:] * w[3]
    for s in (1, 2, 3):
        c = c + pltpu.roll(xe, s, 0)[8:] * w[3 - s]
    return c


CONV_LANES = 512
CONV_HEADS = CONV_LANES // LANE


def _conv_post_block(c, g):
    return jnp.concatenate([_conv_post(c[:, i * LANE:(i + 1) * LANE], g * CONV_HEADS + i)
                            for i in range(CONV_HEADS)], axis=1)


def _conv_fwd(x, w, name, tr=TR_FULL):
    lp, width = x.shape
    cl = CONV_LANES
    nr = lp // tr

    def body(x_ref, prev_ref, w_ref, o_ref):
        g = pl.program_id(0)
        r = pl.program_id(1)
        prev = jnp.where(r > 0, prev_ref[...], 0.0)
        xe = jnp.concatenate([prev, x_ref[...]], axis=0)
        o_ref[...] = _conv_post_block(_conv_taps(xe, [w_ref[t:t + 1, :] for t in range(4)]), g)

    return _pcall(
        body, name=name, grid=(width // cl, nr),
        in_specs=[pl.BlockSpec((tr, cl), lambda g, r: (r, g)),
                  pl.BlockSpec((8, cl), lambda g, r: (jnp.maximum(r * (tr // 8) - 1, 0), g)),
                  pl.BlockSpec((4, cl), lambda g, r: (0, g))],
        out_specs=pl.BlockSpec((tr, cl), lambda g, r: (r, g)),
        out_shape=jax.ShapeDtypeStruct((lp, width), F32),
        compiler_params=_params(2),
    )(x, x, w)


def _conv_bwd(x, w, dq, dk, dv, name, tr=TR_FULL):
    lp, width = x.shape
    cl = CONV_LANES
    nr = lp // tr
    last8 = lp // 8 - 1
    nq = GDN_QK_W // cl

    def body(x_ref, prev_ref, next_ref, w_ref, q_ref, k_ref, v_ref, q_n, k_n, v_n, dx_ref, dw_ref):
        g = pl.program_id(0)
        r = pl.program_id(1)
        w = [w_ref[t:t + 1, :] for t in range(4)]
        not_last = r < nr - 1

        def pick(a, b, c):
            return jnp.where(g < nq, a[...], jnp.where(g < 2 * nq, b[...], c[...]))

        dy = pick(q_ref, k_ref, v_ref)
        dyn = jnp.where(not_last, pick(q_n, k_n, v_n), 0.0)
        prev = jnp.where(r > 0, prev_ref[...], 0.0)
        nxt = jnp.where(not_last, next_ref[...], 0.0)
        xe = jnp.concatenate([prev, x_ref[...], nxt], axis=0)
        ce = _conv_taps(xe, w)
        _, vjp = jax.vjp(lambda c: _conv_post_block(c, g), ce)
        (dce,) = vjp(jnp.concatenate([dy, dyn], axis=0))
        n = tr + 8
        dx = dce * w[3]
        for s in (1, 2, 3):
            dx = dx + pltpu.roll(dce, n - s, 0) * w[3 - s]
        dx_ref[...] = dx[:tr]
        dc = dce[:tr]
        row4 = lax.broadcasted_iota(jnp.int32, (4, cl), 0)
        dw = jnp.zeros((4, cl), F32)
        for s in (0, 1, 2, 3):
            xs = xe[8:8 + tr] if s == 0 else pltpu.roll(xe, s, 0)[8:8 + tr]
            dw = dw + jnp.where(row4 == 3 - s, jnp.sum(dc * xs, axis=0, keepdims=True), 0.0)

        @pl.when(r == 0)
        def _():
            dw_ref[...] = dw

        @pl.when(r > 0)
        def _():
            dw_ref[...] += dw

    def col_q(g):
        return jnp.minimum(g, nq - 1)

    def col_k(g):
        return jnp.clip(g - nq, 0, nq - 1)

    def col_v(g):
        return jnp.maximum(g - 2 * nq, 0)

    def blk(colf):
        return pl.BlockSpec((tr, cl), lambda g, r: (r, colf(g)))

    def nblk(colf):
        return pl.BlockSpec((8, cl), lambda g, r: (jnp.minimum((r + 1) * (tr // 8), last8), colf(g)))

    return _pcall(
        body, name=name, grid=(width // cl, nr),
        in_specs=[pl.BlockSpec((tr, cl), lambda g, r: (r, g)),
                  pl.BlockSpec((8, cl), lambda g, r: (jnp.maximum(r * (tr // 8) - 1, 0), g)),
                  pl.BlockSpec((8, cl), lambda g, r: (jnp.minimum((r + 1) * (tr // 8), last8), g)),
                  pl.BlockSpec((4, cl), lambda g, r: (0, g)),
                  blk(col_q), blk(col_k), blk(col_v), nblk(col_q), nblk(col_k), nblk(col_v)],
        out_specs=[pl.BlockSpec((tr, cl), lambda g, r: (r, g)),
                   pl.BlockSpec((4, cl), lambda g, r: (0, g))],
        out_shape=[jax.ShapeDtypeStruct((lp, width), F32), jax.ShapeDtypeStruct((4, width), F32)],
        compiler_params=_params(2),
    )(x, x, x, w, dq, dk, dv, dq, dk, dv)


def _bmm(a, b, dims, prec=None):
    (ca,), (cb,) = dims
    return lax.dot_general(a, b, (((ca + 1,), (cb + 1,)), ((0,), (0,))), precision=prec,
                           preferred_element_type=F32)


def _inv_impl(m):
    c = m.shape[-1]
    ii = lax.broadcasted_iota(jnp.int32, (c, c), 0)
    jj = lax.broadcasted_iota(jnp.int32, (c, c), 1)
    eye = (ii == jj).astype(F32)

    def same_block(shift):
        return (ii >> shift) == (jj >> shift)

    n1 = jnp.where(same_block(3), -m, 0.0)
    n2 = _bmm(n1, n1, _NN, _X3)
    n4 = _bmm(n2, n2, _NN, _X3)
    d = _bmm(_bmm(eye + n1, eye + n2, _NN, _X3), eye + n4, _NN, _X3)
    shift = 3
    while (1 << shift) < c:
        low = jnp.where(jnp.logical_and(same_block(shift + 1), jnp.logical_not(same_block(shift))), m, 0.0)
        d = d - _bmm(d, _bmm(low, d, _NN, _X3), _NN, _X3)
        shift += 1
    return d


@jax.custom_vjp
def _inv_unit_lower(m):
    return _inv_impl(m)


def _inv_f(m):
    t = _inv_impl(m)
    return t, t


def _inv_b(t, dt):
    c = t.shape[-1]
    ii = lax.broadcasted_iota(jnp.int32, (c, c), 0)
    jj = lax.broadcasted_iota(jnp.int32, (c, c), 1)
    gm = _bmm(t, _bmm(dt, t, _NT, _X3), _TN, _X3)
    return (jnp.where(ii > jj, -gm, 0.0),)


_inv_unit_lower.defvjp(_inv_f, _inv_b)


GDN_HEADS_PER_STEP = 16


def _gdn_group(q, k, v, beta_blk, gc_blk, states, h0):
    hp = GDN_HEADS_PER_STEP
    c = q.shape[0]
    lane = lax.broadcasted_iota(jnp.int32, (1, LANE), 1)
    row8 = lax.broadcasted_iota(jnp.int32, (max(8, hp), LANE), 0)
    lane8 = lax.broadcasted_iota(jnp.int32, (max(8, hp), LANE), 1)
    gcr_all = _hdot((lane8 == h0 + row8).astype(F32), gc_blk, _NT)
    betas, gccs = [], []
    for i in range(hp):
        onehot = (lane == h0 + i).astype(F32)
        betas.append(jnp.sum(beta_blk * onehot, axis=1, keepdims=True))
        gccs.append(jnp.sum(gc_blk * onehot, axis=1, keepdims=True))
    beta = jnp.stack(betas)
    gcc = jnp.stack(gccs)
    gcr = jnp.stack([gcr_all[i:i + 1] for i in range(hp)])
    qh = jnp.stack([q[:, (i // 2) * LANE:(i // 2 + 1) * LANE] for i in range(hp)])
    kh = jnp.stack([k[:, (i // 2) * LANE:(i // 2 + 1) * LANE] for i in range(hp)])
    vh = jnp.stack([v[:, i * LANE:(i + 1) * LANE] for i in range(hp)])
    state = jnp.stack(states)
    ii = lax.broadcasted_iota(jnp.int32, (c, c), 0)
    jj = lax.broadcasted_iota(jnp.int32, (c, c), 1)
    incl = ii >= jj
    dec = jnp.where(incl, jnp.exp(jnp.where(incl, gcc - gcr, 0.0)), 0.0)
    eg = jnp.exp(gcc)
    m = _bmm(kh, kh, _NT) * beta * jnp.where(ii > jj, dec, 0.0)
    t = _inv_unit_lower(m)
    u = _bmm(t, vh * beta, _NN, _X3)
    w = _bmm(t, kh * (beta * eg), _NN, _X3)
    attn = _bmm(qh, kh, _NT) * dec
    rows = lax.broadcasted_iota(jnp.int32, (c, 1), 0)
    gl = jnp.sum(jnp.where(rows == c - 1, gcc, 0.0), axis=1, keepdims=True)
    v_new = u - _bmm(w, state, _NN)
    o = _bmm(qh * eg, state, _NN) + _bmm(attn, v_new, _NN)
    new_state = state * jnp.exp(gl) + _bmm(kh * jnp.exp(gl - gcc), v_new, _TN)
    return jnp.concatenate([o[i] for i in range(hp)], axis=1), tuple(new_state[i] for i in range(hp))


def _gdn_specs(nc, rev):
    def cidx(n):
        return (nc - 1 - n) if rev else n
    hp = GDN_HEADS_PER_STEP
    nqk = GDN_QK_HEADS
    c = GDN_CHUNK
    nq = 2 * nqk // hp
    q_spec = pl.BlockSpec((c, hp // 2 * LANE), lambda n, g: (cidx(n), g))
    k_spec = pl.BlockSpec((c, hp // 2 * LANE), lambda n, g: (cidx(n), nq + g))
    v_spec = pl.BlockSpec((c, hp * LANE), lambda n, g: (cidx(n), nq + g))
    s_spec = pl.BlockSpec((c, LANE), lambda n, g: (cidx(n), 0))
    o_spec = pl.BlockSpec((c, hp * LANE), lambda n, g: (cidx(n), g))
    ck_spec = pl.BlockSpec((hp, 1, GDN_DK, LANE), lambda n, g: (g, cidx(n), 0, 0))
    return q_spec, k_spec, v_spec, s_spec, o_spec, ck_spec


def _gdn_fwd(qkv, beta, gc, name):
    lp = qkv.shape[0]
    nc = lp // GDN_CHUNK
    nh = GDN_V_HEADS
    hp = GDN_HEADS_PER_STEP
    q_spec, k_spec, v_spec, s_spec, o_spec, ck_spec = _gdn_specs(nc, False)

    def body(q_ref, k_ref, v_ref, b_ref, g_ref, o_ref, ck_ref, state):
        n = pl.program_id(0)
        g = pl.program_id(1)

        @pl.when(n == 0)
        def _():
            for i in range(hp):
                state[g * hp + i] = jnp.zeros((GDN_DK, LANE), F32)

        states = tuple(state[g * hp + i] for i in range(hp))
        for i in range(hp):
            ck_ref[i, 0] = states[i]
        o, new_states = _gdn_group(q_ref[...], k_ref[...], v_ref[...], b_ref[...], g_ref[...], states, g * hp)
        o_ref[...] = o
        for i in range(hp):
            state[g * hp + i] = new_states[i]

    return _pcall(
        body, name=name, grid=(nc, nh // hp),
        in_specs=[q_spec, k_spec, v_spec, s_spec, s_spec],
        out_specs=[o_spec, ck_spec],
        out_shape=[jax.ShapeDtypeStruct((lp, GDN_V_W), F32),
                   jax.ShapeDtypeStruct((nh, nc, GDN_DK, LANE), F32)],
        scratch_shapes=[pltpu.VMEM((nh, GDN_DK, LANE), F32)],
        compiler_params=_params(2),
    )(qkv, qkv, qkv, beta, gc)


def _gdn_bwd(qkv, beta, gc, ckpt, do, name):
    lp = qkv.shape[0]
    nc = lp // GDN_CHUNK
    nh = GDN_V_HEADS
    hp = GDN_HEADS_PER_STEP
    q_spec, k_spec, v_spec, s_spec, o_spec, ck_spec = _gdn_specs(nc, True)

    def body(q_ref, k_ref, v_ref, b_ref, g_ref, ck_ref, do_ref,
             dq_ref, dk_ref, dv_ref, db_ref, dg_ref, dstate):
        n = pl.program_id(0)
        g = pl.program_id(1)

        @pl.when(n == 0)
        def _():
            for i in range(hp):
                dstate[g * hp + i] = jnp.zeros((GDN_DK, LANE), F32)

        states = tuple(ck_ref[i, 0] for i in range(hp))
        _, vjp = jax.vjp(lambda q, k, v, b, gg, s: _gdn_group(q, k, v, b, gg, s, g * hp),
                         q_ref[...], k_ref[...], v_ref[...], b_ref[...], g_ref[...], states)
        dq, dk, dv, db, dg, ds = vjp((do_ref[...], tuple(dstate[g * hp + i] for i in range(hp))))
        dq_ref[...] = dq
        dk_ref[...] = dk
        dv_ref[...] = dv
        for i in range(hp):
            dstate[g * hp + i] = ds[i]

        @pl.when(g == 0)
        def _():
            db_ref[...] = db
            dg_ref[...] = dg

        @pl.when(g > 0)
        def _():
            db_ref[...] += db
            dg_ref[...] += dg

    qk_shape = jax.ShapeDtypeStruct((lp, GDN_QK_W), F32)
    big = jax.ShapeDtypeStruct((lp, GDN_V_W), F32)
    small = jax.ShapeDtypeStruct((lp, LANE), F32)
    dq_spec = pl.BlockSpec((GDN_CHUNK, hp // 2 * LANE), lambda n, g: (nc - 1 - n, g))
    return _pcall(
        body, name=name, grid=(nc, nh // hp),
        in_specs=[q_spec, k_spec, v_spec, s_spec, s_spec, ck_spec, o_spec],
        out_specs=[dq_spec, dq_spec, o_spec, s_spec, s_spec],
        out_shape=[qk_shape, qk_shape, big, small, small],
        scratch_shapes=[pltpu.VMEM((nh, GDN_DK, LANE), F32)],
        compiler_params=_params(2),
    )(qkv, qkv, qkv, beta, gc, ckpt, do)


LOG2E = 1.4426950408889634
LN2 = 0.6931471805599453
Q_PRESCALE = MLA_QK ** -0.5 * LOG2E


ATT_SUB = 128
ATT_HEADS_PER_STEP = 4
ATT_BWD_HEADS_PER_STEP = 2


def _att_mask(i, j, tb, transposed):
    r = lax.broadcasted_iota(jnp.int32, (tb, tb), 0)
    c = lax.broadcasted_iota(jnp.int32, (tb, tb), 1)
    qpos, kpos = (i * tb + c, j * tb + r) if transposed else (i * tb + r, j * tb + c)
    return jnp.logical_and(kpos <= qpos, kpos >= FRONT)


def _causal_pairs(nb, by_key):
    if by_key:
        pairs = [(i, j) for j in range(nb) for i in range(j, nb)]
    else:
        pairs = [(i, j) for i in range(nb) for j in range(i + 1)]
    return jnp.array([p[0] for p in pairs], jnp.int32), jnp.array([p[1] for p in pairs], jnp.int32)


def _masked_and_plain(i, j, step):
    edge = jnp.logical_or(j == i, j == 0)

    @pl.when(jnp.logical_and(edge, j <= i))
    def _():
        step(True)

    @pl.when(jnp.logical_and(jnp.logical_not(edge), j < i))
    def _():
        step(False)


def _cat(a_ref, b_ref):
    return jnp.concatenate([a_ref[...], b_ref[...]], axis=1)


def _flash_fwd(qn, qr, kn, kr, v, name, tb=ROW_ALIGN):
    lp = qn.shape[0]
    nb = lp // tb
    nh = MLA_HEADS
    hp = ATT_HEADS_PER_STEP
    qi, kj = _causal_pairs(nb, by_key=False)

    def body(qi_ref, kj_ref, qn_ref, qr_ref, kn_ref, kr_ref, v_ref, o_ref, lse_ref, m_s, l_s, acc):
        t = pl.program_id(1)
        i, j = qi_ref[t], kj_ref[t]

        @pl.when(j == 0)
        def _():
            m_s[...] = jnp.full_like(m_s, NEG)
            l_s[...] = jnp.zeros_like(l_s)
            acc[...] = jnp.zeros_like(acc)

        def step(masked):
            n_sub = tb // ATT_SUB
            kr = kr_ref[...]
            for e in range(hp):
                lanes = pl.ds(e * LANE, LANE)
                k = jnp.concatenate([kn_ref[:, lanes], kr], axis=1)
                v = v_ref[:, lanes]

                def scores(r, lanes=lanes, k=k):
                    rows = pl.ds(r * ATT_SUB, ATT_SUB)
                    return _dot(jnp.concatenate([qn_ref[rows, lanes], qr_ref[rows, lanes]], axis=1), k, _NT)

                s_next = scores(0)
                for r in range(n_sub):
                    s = s_next
                    if r + 1 < n_sub:
                        s_next = scores(r + 1)
                    rows = pl.ds(r * ATT_SUB, ATT_SUB)
                    if masked:
                        qpos = i * tb + r * ATT_SUB + lax.broadcasted_iota(jnp.int32, (ATT_SUB, tb), 0)
                        kpos = j * tb + lax.broadcasted_iota(jnp.int32, (ATT_SUB, tb), 1)
                        s = jnp.where(jnp.logical_and(kpos <= qpos, kpos >= FRONT), s, NEG)
                    m_old = m_s[e, rows, :]
                    m_new = jnp.maximum(m_old, jnp.max(s, axis=1, keepdims=True))
                    alpha = jnp.exp2(m_old - m_new)
                    p = jnp.exp2(s - m_new)
                    l_s[e, rows, :] = alpha * l_s[e, rows, :] + jnp.sum(p, axis=1, keepdims=True)
                    acc[e, rows, :] = alpha * acc[e, rows, :] + _dot(p.astype(BF16), v, _NN)
                    m_s[e, rows, :] = m_new

        _masked_and_plain(i, j, step)

        @pl.when(j == i)
        def _():
            for e in range(hp):
                lanes = pl.ds(e * LANE, LANE)
                o_ref[:, lanes] = acc[e] / l_s[e]
                lse_ref[:, lanes] = jnp.broadcast_to(m_s[e] + jnp.log(l_s[e]) * LOG2E, (tb, LANE))

    qspec = pl.BlockSpec((tb, hp * LANE), lambda h, t, qi_, kj_: (qi_[t], h))
    kspec = pl.BlockSpec((tb, hp * LANE), lambda h, t, qi_, kj_: (kj_[t], h))
    krspec = pl.BlockSpec((tb, LANE), lambda h, t, qi_, kj_: (kj_[t], 0))
    shp = jax.ShapeDtypeStruct((lp, nh * LANE), F32)
    return _pcall(
        body, name=name, out_shape=[shp, shp],
        grid_spec=pltpu.PrefetchScalarGridSpec(
            num_scalar_prefetch=2, grid=(nh // hp, qi.shape[0]),
            in_specs=[qspec, qspec, kspec, krspec, kspec], out_specs=[qspec, qspec],
            scratch_shapes=[pltpu.VMEM((hp, tb, 1), F32), pltpu.VMEM((hp, tb, 1), F32),
                            pltpu.VMEM((hp, tb, LANE), F32)]),
        compiler_params=_params(2),
    )(qi, kj, qn, qr, kn, kr, v)


def _flash_bwd(qn, qr, kn, kr, v, o, do, lse, name, tb=ROW_ALIGN):
    lp = qn.shape[0]
    nb = lp // tb
    nh = MLA_HEADS
    hp = ATT_BWD_HEADS_PER_STEP
    qi, kj = _causal_pairs(nb, by_key=True)
    n_pairs = qi.shape[0]
    knt, krt = kn.T, kr.T

    def body(qi_ref, kj_ref, qn_ref, qr_ref, kn_ref, kr_ref, knt_ref, krt_ref, v_ref, o_ref, do_ref, lse_ref,
             dqnt_hbm, dqrt_hbm, dkn_ref, dkr_ref, dv_ref, dk_acc, dv_acc, dqn_acc, dqr_acc, out_sems):
        g = pl.program_id(0)
        t = pl.program_id(1)
        i, j = qi_ref[t], kj_ref[t]

        @pl.when(t == 0)
        def _():
            dqn_acc[...] = jnp.zeros_like(dqn_acc)
            dqr_acc[...] = jnp.zeros_like(dqr_acc)

        @pl.when(i == j)
        def _():
            dk_acc[...] = jnp.zeros_like(dk_acc)
            dv_acc[...] = jnp.zeros_like(dv_acc)

        def step(masked):
            kr = kr_ref[...]
            krt_blk = krt_ref[...]
            lane = lax.broadcasted_iota(jnp.int32, (8, LANE), 1)
            for e in range(hp):
                lanes = pl.ds(e * LANE, LANE)
                q = jnp.concatenate([qn_ref[:, lanes], qr_ref[:, lanes]], axis=1)
                st = _dot(jnp.concatenate([kn_ref[:, lanes], kr], axis=1), q, _NT)
                if masked:
                    st = jnp.where(_att_mask(i, j, tb, True), st, NEG)
                do_blk = do_ref[:, lanes]
                lse_row = _hdot((lane == 0).astype(F32), lse_ref[:, lanes], _NT)[0:1]
                delta_row = _hdot(jnp.ones((8, LANE), F32), do_blk * o_ref[:, lanes], _NT)[0:1]
                pt = jnp.exp2(st - lse_row)
                do_b = do_blk.astype(BF16)
                dv_acc[e] += _dot(pt.astype(BF16), do_b, _NN)
                dpt = _dot(v_ref[:, lanes], do_b, _NT)
                dst = (pt * (dpt - delta_row)).astype(BF16)
                dk_acc[e] += _dot(dst, q, _NN)
                dqn_acc[e * nb + i] += _dot(knt_ref[pl.ds(e * LANE, LANE), :], dst, _NN) * LN2
                dqr_acc[e * nb + i] += _dot(krt_blk, dst, _NN) * LN2

        _masked_and_plain(i, j, step)

        @pl.when(i == nb - 1)
        def _():
            for e in range(hp):
                lanes = pl.ds(e * LANE, LANE)
                dkn_ref[:, lanes] = dk_acc[e, :, :LANE] * LN2
                dkr_ref[:, lanes] = dk_acc[e, :, LANE:] * LN2
                dv_ref[:, lanes] = dv_acc[e]

        @pl.when(t == n_pairs - 1)
        def _():
            dst_rows = pl.ds(g * (hp * nb), hp * nb)
            cn = pltpu.make_async_copy(dqn_acc, dqnt_hbm.at[dst_rows], out_sems.at[0])
            cr = pltpu.make_async_copy(dqr_acc, dqrt_hbm.at[dst_rows], out_sems.at[1])
            cn.start()
            cr.start()
            cn.wait()
            cr.wait()

    qspec = pl.BlockSpec((tb, hp * LANE), lambda h, t, qi_, kj_: (qi_[t], h))
    kspec = pl.BlockSpec((tb, hp * LANE), lambda h, t, qi_, kj_: (kj_[t], h))
    krspec = pl.BlockSpec((tb, LANE), lambda h, t, qi_, kj_: (kj_[t], 0))
    ktspec = pl.BlockSpec((hp * LANE, tb), lambda h, t, qi_, kj_: (h, kj_[t]))
    krtspec = pl.BlockSpec((LANE, tb), lambda h, t, qi_, kj_: (0, kj_[t]))
    shp = jax.ShapeDtypeStruct((lp, nh * LANE), F32)
    dqt_shape = jax.ShapeDtypeStruct((nh * nb, LANE, tb), F32)
    dqnt, dqrt, dkn, dkr, dv = _pcall(
        body, name=name, out_shape=[dqt_shape, dqt_shape, shp, shp, shp],
        grid_spec=pltpu.PrefetchScalarGridSpec(
            num_scalar_prefetch=2, grid=(nh // hp, n_pairs),
            in_specs=[qspec, qspec, kspec, krspec, ktspec, krtspec, kspec, qspec, qspec, qspec],
            out_specs=[_ANY, _ANY, kspec, kspec, kspec],
            scratch_shapes=[pltpu.VMEM((hp, tb, 2 * LANE), F32), pltpu.VMEM((hp, tb, LANE), F32),
                            pltpu.VMEM((hp * nb, LANE, tb), F32), pltpu.VMEM((hp * nb, LANE, tb), F32),
                            pltpu.SemaphoreType.DMA((2,))]),
        compiler_params=_params(2),
    )(qi, kj, qn, qr, kn, kr, knt, krt, v, o, do, lse)

    def rows_major(a):
        return a.reshape(nh, nb, LANE, tb).transpose(1, 3, 0, 2).reshape(lp, nh * LANE)

    return rows_major(dqnt), rows_major(dqrt), dkn, dkr, dv


ELEMENTWISE_BLOCK_BYTES = 1 << 20


def _row_tile(rows, width, copies=1):
    for t in (1024, 512, 256, 128, 64, 32, 16, 8):
        if rows % t == 0 and t * width * 4 * copies <= ELEMENTWISE_BLOCK_BYTES:
            return t
    return rows


def _adamw(w, g, m, v, name):
    rows, width = w.shape
    tr = _row_tile(rows, width)

    def body(w_ref, g_ref, m_ref, v_ref, d_ref, nm_ref, nv_ref):
        gg = g_ref[...]
        nm = ADAM_B1 * m_ref[...] + (1.0 - ADAM_B1) * gg
        nv = ADAM_B2 * v_ref[...] + (1.0 - ADAM_B2) * jnp.square(gg)
        m_hat = nm / (1.0 - ADAM_B1 ** ADAM_STEP)
        v_hat = nv / (1.0 - ADAM_B2 ** ADAM_STEP)
        d_ref[...] = -ADAM_LR * (m_hat / (jnp.sqrt(v_hat) + ADAM_EPS) + ADAM_WD * w_ref[...])
        nm_ref[...] = nm
        nv_ref[...] = nv

    spec = pl.BlockSpec((tr, width), lambda r: (r, 0))
    shp = jax.ShapeDtypeStruct((rows, width), F32)
    return _pcall(body, name=name, grid=(rows // tr,), in_specs=[spec] * 4, out_specs=[spec] * 3,
                  out_shape=[shp] * 3, compiler_params=_params(1))(w, g, m, v)


def _add_pair(a, b, name):
    s, rows, width = b.shape
    tr = _row_tile(rows, width)
    nt = rows // tr

    def body(c_ref, a_ref, b_ref, o_ref):
        o_ref[...] = a_ref[...] + b_ref[...]

    spec = pl.BlockSpec((1, tr, width), lambda i, r, c_ref: (i, r, 0))
    return _pcall(
        body, name=name, out_shape=jax.ShapeDtypeStruct(b.shape, F32),
        grid_spec=pltpu.PrefetchScalarGridSpec(
            num_scalar_prefetch=1, grid=(s, nt),
            in_specs=[pl.BlockSpec((1, tr, width), lambda i, r, c_ref: (i, c_ref[0] * nt + r, 0)), spec],
            out_specs=spec),
        compiler_params=_params(2),
    )(_core_index(), a, b)


def _sum_slots(a, name):
    s, rows, width = a.shape
    tr = _row_tile(rows, width, copies=s)

    def body(a_ref, o_ref):
        tot = a_ref[0]
        for k in range(1, s):
            tot = tot + a_ref[k]
        o_ref[...] = tot

    return _pcall(body, name=name, grid=(rows // tr,),
                  in_specs=[pl.BlockSpec((s, tr, width), lambda r: (0, r, 0))],
                  out_specs=pl.BlockSpec((tr, width), lambda r: (r, 0)),
                  out_shape=jax.ShapeDtypeStruct((rows, width), F32), compiler_params=_params(1))(a)


_ANY = pl.BlockSpec(memory_space=pl.ANY)


def _my_place():
    return lax.axis_index("x"), lax.axis_index("y"), lax.axis_index("c")


def _core_index():
    return lax.axis_index("c").astype(jnp.int32).reshape(1)


def _other_chips(x, y):
    return [(1 - x, y), (x, 1 - y), (1 - x, 1 - y)]


def _gather_shards(flat, name):
    rows, width = flat.shape

    def body(x_ref, out_ref, send_sems, recv_sems, local_sem):
        x, y, c = _my_place()
        mine = pltpu.make_async_copy(x_ref, out_ref.at[2 * x + y], local_sem)
        mine.start()
        sends = []
        for k, (px, py) in enumerate(_other_chips(x, y)):
            cp = pltpu.make_async_remote_copy(
                src_ref=x_ref, dst_ref=out_ref.at[2 * x + y], send_sem=send_sems.at[k], recv_sem=recv_sems.at[k],
                device_id=(px, py, c), device_id_type=MESH)
            cp.start()
            sends.append(cp)
        for k, (px, py) in enumerate(_other_chips(x, y)):
            pltpu.make_async_remote_copy(
                src_ref=x_ref, dst_ref=out_ref.at[2 * px + py], send_sem=send_sems.at[k], recv_sem=recv_sems.at[k],
                device_id=(px, py, c), device_id_type=MESH).wait_recv()
        for cp in sends:
            cp.wait_send()
        mine.wait()

    return _pcall(
        body, name=name, in_specs=[_ANY], out_specs=_ANY,
        out_shape=jax.ShapeDtypeStruct((4, rows, width), flat.dtype),
        scratch_shapes=[pltpu.SemaphoreType.DMA((3,)), pltpu.SemaphoreType.DMA((3,)), pltpu.SemaphoreType.DMA],
    )(flat)


def _sibling_split(g, name):
    s, rows, width = g.shape
    half = rows // 2
    tr = _row_tile(half, width)
    nt = half // tr

    def body(c_ref, g_blk, got_ref, send_sem, recv_sem):
        k = pl.program_id(0)
        t = pl.program_id(1)
        x, y, c = _my_place()
        cp = pltpu.make_async_remote_copy(
            src_ref=g_blk.at[0], dst_ref=got_ref.at[k, pl.ds(pl.multiple_of(t * tr, 8), tr), :],
            send_sem=send_sem, recv_sem=recv_sem, device_id=(x, y, 1 - c), device_id_type=MESH)
        cp.start()
        cp.wait_send()

        @pl.when(jnp.logical_and(k == s - 1, t == nt - 1))
        def _():
            pltpu.make_async_remote_copy(
                src_ref=got_ref, dst_ref=got_ref, send_sem=send_sem, recv_sem=recv_sem,
                device_id=(x, y, 1 - c), device_id_type=MESH).wait_recv()

    return _pcall(
        body, name=name, out_shape=jax.ShapeDtypeStruct((s, half, width), g.dtype),
        grid_spec=pltpu.PrefetchScalarGridSpec(
            num_scalar_prefetch=1, grid=(s, nt),
            in_specs=[pl.BlockSpec((1, tr, width), lambda k, t, c_ref: (k, (1 - c_ref[0]) * nt + t, 0))],
            out_specs=_ANY,
            scratch_shapes=[pltpu.SemaphoreType.DMA, pltpu.SemaphoreType.DMA]),
        compiler_params=_params(2),
    )(_core_index(), g)


def _chip_scatter(p, name):
    s, rows, width = p.shape

    def body(p_ref, out_ref, send_sems, recv_sems, local_sem):
        x, y, c = _my_place()
        me = 2 * x + y
        mine = pltpu.make_async_copy(p_ref.at[me], out_ref.at[me], local_sem)
        mine.start()
        sends = []
        for k, (px, py) in enumerate(_other_chips(x, y)):
            cp = pltpu.make_async_remote_copy(
                src_ref=p_ref.at[2 * px + py], dst_ref=out_ref.at[me], send_sem=send_sems.at[k],
                recv_sem=recv_sems.at[k], device_id=(px, py, c), device_id_type=MESH)
            cp.start()
            sends.append(cp)
        for k, (px, py) in enumerate(_other_chips(x, y)):
            pltpu.make_async_remote_copy(
                src_ref=p_ref.at[me], dst_ref=out_ref.at[2 * px + py], send_sem=send_sems.at[k],
                recv_sem=recv_sems.at[k], device_id=(px, py, c), device_id_type=MESH).wait_recv()
        for cp in sends:
            cp.wait_send()
        mine.wait()

    return _pcall(
        body, name=name, in_specs=[_ANY], out_specs=_ANY,
        out_shape=jax.ShapeDtypeStruct(p.shape, p.dtype),
        scratch_shapes=[pltpu.SemaphoreType.DMA((3,)), pltpu.SemaphoreType.DMA((3,)), pltpu.SemaphoreType.DMA],
    )(p)


def _sibling_join(qh, name):
    half, width = qh.shape
    tr = _row_tile(half, width)
    nt = half // tr

    def body(q_blk, out_ref, send_sem, recv_sem, local_sem):
        t = pl.program_id(0)
        x, y, c = _my_place()
        dst = out_ref.at[pl.ds(pl.multiple_of(c * half + t * tr, 8), tr), :]
        cp = pltpu.make_async_remote_copy(
            src_ref=q_blk, dst_ref=dst, send_sem=send_sem, recv_sem=recv_sem,
            device_id=(x, y, 1 - c), device_id_type=MESH)
        cp.start()
        mine = pltpu.make_async_copy(q_blk, dst, local_sem)
        mine.start()
        cp.wait_send()
        mine.wait()

        @pl.when(t == nt - 1)
        def _():
            theirs = out_ref.at[pl.ds(pl.multiple_of((1 - c) * half, 8), half), :]
            pltpu.make_async_remote_copy(
                src_ref=theirs, dst_ref=theirs, send_sem=send_sem, recv_sem=recv_sem,
                device_id=(x, y, 1 - c), device_id_type=MESH).wait_recv()

    return _pcall(
        body, name=name, grid=(nt,),
        in_specs=[pl.BlockSpec((tr, width), lambda t: (t, 0))], out_specs=_ANY,
        out_shape=jax.ShapeDtypeStruct((2 * half, width), qh.dtype),
        scratch_shapes=[pltpu.SemaphoreType.DMA, pltpu.SemaphoreType.DMA, pltpu.SemaphoreType.DMA],
        compiler_params=_params(1),
    )(qh)


def _all_sum_small(part, name):
    rows, width = part.shape

    def body(p_ref, out_ref, land, send_sems, recv_sems):
        x, y, c = _my_place()
        me = 4 * x + 2 * y + c
        land[me] = p_ref[...]
        sends = []
        for k in range(1, 8):
            peer = (x ^ (k >> 2), y ^ ((k >> 1) & 1), c ^ (k & 1))
            cp = pltpu.make_async_remote_copy(
                src_ref=p_ref, dst_ref=land.at[me], send_sem=send_sems.at[k - 1], recv_sem=recv_sems.at[k - 1],
                device_id=peer, device_id_type=MESH)
            cp.start()
            sends.append(cp)
        for k in range(1, 8):
            px, py, pc = x ^ (k >> 2), y ^ ((k >> 1) & 1), c ^ (k & 1)
            pltpu.make_async_remote_copy(
                src_ref=p_ref, dst_ref=land.at[4 * px + 2 * py + pc], send_sem=send_sems.at[k - 1],
                recv_sem=recv_sems.at[k - 1], device_id=(px, py, pc), device_id_type=MESH).wait_recv()
        for cp in sends:
            cp.wait_send()
        tot = land[0]
        for k in range(1, 8):
            tot = tot + land[k]
        out_ref[...] = tot

    vmem = pl.BlockSpec(memory_space=pltpu.VMEM)
    return _pcall(
        body, name=name, in_specs=[vmem], out_specs=vmem,
        out_shape=jax.ShapeDtypeStruct((rows, width), F32),
        scratch_shapes=[pltpu.VMEM((8, rows, width), F32), pltpu.SemaphoreType.DMA((7,)),
                        pltpu.SemaphoreType.DMA((7,))],
    )(part)


def _big_layout(shards):
    return [(a.shape[0], a.shape[1], ax) for a, ax in shards]


FLAT_ROW_MULTIPLE = 2048


def _pack_shards(arrs, row_multiple=FLAT_ROW_MULTIPLE):
    flat = jnp.concatenate([a.reshape(-1) for a in arrs])
    return jnp.pad(flat, (0, -flat.shape[0] % (row_multiple * LANE))).reshape(-1, LANE)


def _unpack_shards(flat, layout):
    flat = flat.reshape(-1)
    out, off = [], 0
    for r, c, _ in layout:
        out.append(flat[off:off + r * c].reshape(r, c))
        off += r * c
    return out


def _unpack_full(gathered, layout):
    g = gathered.reshape(4, -1)
    out, off = [], 0
    for r, c, ax in layout:
        seg = g[:, off:off + r * c].reshape(4, r, c)
        out.append(seg.transpose(1, 0, 2).reshape(r, 4 * c) if ax == 1 else seg.reshape(4 * r, c))
        off += r * c
    return out


def _pack_full(fulls, layout):
    parts = []
    for a, (r, c, ax) in zip(fulls, layout):
        if ax == 1:
            parts.append(a.reshape(r, 4, c).transpose(1, 0, 2).reshape(4, r * c))
        else:
            parts.append(a.reshape(4, r * c))
    flat = jnp.concatenate(parts, axis=1)
    return jnp.pad(flat, ((0, 0), (0, -flat.shape[1] % (FLAT_ROW_MULTIPLE * LANE)))).reshape(4, -1, LANE)


def _pad_lanes(a, width=LANE):
    return jnp.pad(a, [(0, 0)] * (a.ndim - 1) + [(0, width - a.shape[-1])])


def _pack_small(arrs):
    rows = [_pad_lanes(a.reshape(1, -1), -(-a.size // LANE) * LANE).reshape(-1, LANE) for a in arrs]
    flat = jnp.concatenate(rows, axis=0)
    return jnp.pad(flat, ((0, -flat.shape[0] % 8), (0, 0)))


def _unpack_small(flat, shapes):
    out, off = [], 0
    for shp in shapes:
        n = math.prod(shp)
        nr = -(-n // LANE)
        out.append(flat[off:off + nr].reshape(-1)[:n].reshape(shp))
        off += nr
    return out


def kernel(x, meta_tokens, pre_norm, post_norm, gdn_w_in, gdn_conv_w, gdn_a_log, gdn_dt_bias, gdn_out_norm, gdn_w_out, kv_norm, kv_w_down, kv_latent_norm, kv_w_up, mla_w_in, mla_q_latent_norm, mla_w_q_up, mla_w_out, loss_target, m_meta_tokens, m_pre_norm, m_post_norm, m_gdn_w_in, m_gdn_conv_w, m_gdn_a_log, m_gdn_dt_bias, m_gdn_out_norm, m_gdn_w_out, m_kv_norm, m_kv_w_down, m_kv_latent_norm, m_kv_w_up, m_mla_w_in, m_mla_q_latent_norm, m_mla_w_q_up, m_mla_w_out, v_meta_tokens, v_pre_norm, v_post_norm, v_gdn_w_in, v_gdn_conv_w, v_gdn_a_log, v_gdn_dt_bias, v_gdn_out_norm, v_gdn_w_out, v_kv_norm, v_kv_w_down, v_kv_latent_norm, v_kv_w_up, v_mla_w_in, v_mla_q_latent_norm, v_mla_w_q_up, v_mla_w_out):
    seq = x.shape[1]
    d = D_MODEL
    lp = -(-(ROW0 + seq) // ROW_ALIGN) * ROW_ALIGN
    tail = lp - ROW0 - seq

    big_names = ["meta_tokens", "gdn_conv_w", "gdn_w_out", "kv_w_down", "kv_w_up", "mla_w_in", "mla_w_q_up",
                 "mla_w_out"]
    big_axis = [1, 1, 0, 0, 1, 1, 1, 0]
    big_w = [meta_tokens, gdn_conv_w[0], gdn_w_out[0], kv_w_down, kv_w_up, mla_w_in[0], mla_w_q_up[0], mla_w_out[0]]
    big_m = [m_meta_tokens, m_gdn_conv_w[0], m_gdn_w_out[0], m_kv_w_down, m_kv_w_up, m_mla_w_in[0], m_mla_w_q_up[0],
             m_mla_w_out[0]]
    big_v = [v_meta_tokens, v_gdn_conv_w[0], v_gdn_w_out[0], v_kv_w_down, v_kv_w_up, v_mla_w_in[0], v_mla_w_q_up[0],
             v_mla_w_out[0]]
    layout = _big_layout(list(zip(big_w, big_axis)))
    w_flat = _pack_shards(big_w)
    meta_f, conv_w = _unpack_full(
        _gather_shards(_pack_shards(big_w[:2], row_multiple=16), "gather_meta_conv"), layout[:2])
    mm_shards = [w.astype(BF16) for w in big_w[2:6]] + [(big_w[6] * Q_PRESCALE).astype(BF16), big_w[7].astype(BF16)]
    (w_out0, kv_down, kv_up, w_in1, w_qup, w_out1) = _unpack_full(
        _gather_shards(_pack_shards(mm_shards), "gather_weights"), layout[2:])
    w_in0_shards = _gather_shards(gdn_w_in[0].astype(BF16), "gather_gdn_w_in")
    w_in0 = jnp.concatenate([w_in0_shards[s] for s in range(4)], axis=1)
    win_cols = gdn_w_in.shape[2]

    nv = GDN_V_HEADS
    w_qkv = w_in0[:, :GDN_CONV_W]
    w_z0 = w_in0[:, GDN_CONV_W:GDN_CONV_W + GDN_V_W]
    w_b = _pad_lanes(w_in0[:, GDN_CONV_W + GDN_V_W:GDN_CONV_W + GDN_V_W + nv])
    w_a = _pad_lanes(w_in0[:, GDN_CONV_W + GDN_V_W + nv:])
    w_ckv = kv_down[:, :MLA_KV_RANK]
    w_kr = _pad_lanes(kv_down[:, MLA_KV_RANK:])
    kvu = kv_up.reshape(MLA_KV_RANK, MLA_HEADS, 2 * LANE)
    w_kn = kvu[:, :, :LANE].reshape(MLA_KV_RANK, MLA_HEADS * LANE)
    w_v = kvu[:, :, LANE:].reshape(MLA_KV_RANK, MLA_HEADS * LANE)
    w_cq = w_in1[:, :MLA_Q_RANK]
    w_z1 = w_in1[:, MLA_Q_RANK:]
    qu = w_qup.reshape(MLA_Q_RANK, MLA_HEADS, MLA_QK)
    w_qn = qu[:, :, :MLA_NOPE].reshape(MLA_Q_RANK, MLA_HEADS * LANE)
    w_qr = _pad_lanes(qu[:, :, MLA_NOPE:]).reshape(MLA_Q_RANK, MLA_HEADS * LANE)

    pre0, pre1 = pre_norm[0:1], pre_norm[1:2]
    post0, post1 = post_norm[0:1], post_norm[1:2]
    a_log = _pad_lanes(gdn_a_log)
    dt_bias = _pad_lanes(gdn_dt_bias)
    kvn = kv_norm.reshape(1, d)
    kvl = kv_latent_norm.reshape(1, MLA_KV_RANK)
    qln = mla_q_latent_norm

    h0 = jnp.concatenate([jnp.zeros((FRONT, d), F32), meta_f, x[0], jnp.zeros((tail, d), F32)], axis=0)
    tgt = jnp.pad(loss_target[0], ((ROW0, tail), (0, 0)))
    pos = jnp.maximum(jnp.arange(lp, dtype=jnp.int32) - FRONT, 0).astype(F32)
    inv = ROPE_THETA ** (-jnp.arange(0, MLA_ROPE, 2, dtype=F32) / MLA_ROPE)
    ang = pos[:, None] * inv[None, :]
    zeros64 = jnp.zeros((lp, LANE - MLA_ROPE), F32)
    cos_t = jnp.concatenate([jnp.cos(ang), jnp.cos(ang), zeros64], axis=1)
    sin_t = jnp.concatenate([-jnp.sin(ang), jnp.sin(ang), zeros64], axis=1)

    def valid_rows(ridx):
        return jnp.logical_and(ridx >= FRONT, ridx < ROW0 + seq)

    def f_pre0(ridx, g, h, gain):
        return _rms(h, gain), h

    (hn0,) = _rowwise("pre0", lambda *a: f_pre0(*a)[:1], [_In(h0), _In(pre0, "const")],
                      [_Out("row", (lp, d), BF16)])
    qkv_raw = _mm(hn0, w_qkv, "nn", "gdn_in_qkv")
    z0 = _mm(hn0, w_z0, "nn", "gdn_in_z")
    b_raw = _mm(hn0, w_b, "nn", "gdn_in_b")
    a_raw = _mm(hn0, w_a, "nn", "gdn_in_a")

    def f_ba(ridx, g, b, a, alog, dtb):
        tr = b.shape[0]
        ok = valid_rows(ridx).astype(F32)
        beta = jax.nn.sigmoid(b) * ok
        gate = -jnp.exp(alog) * _softplus(a + dtb) * ok
        ii = lax.broadcasted_iota(jnp.int32, (tr, tr), 0)
        jj = lax.broadcasted_iota(jnp.int32, (tr, tr), 1)
        tri = jnp.logical_and((ii >> 6) == (jj >> 6), ii >= jj).astype(F32)
        return beta, _hdot(tri, gate)

    ba_ins = [_In(b_raw), _In(a_raw), _In(a_log, "const"), _In(dt_bias, "const")]
    beta, gc = _rowwise("gdn_gates", f_ba, ba_ins, [_Out("row", (lp, LANE)), _Out("row", (lp, LANE))])
    qkv = _conv_fwd(qkv_raw, conv_w, "gdn_conv")
    o0, ckpt = _gdn_fwd(qkv, beta, gc, "gdn_scan")

    def per_head(fn, *arrs):
        n = arrs[0].shape[1] // LANE
        return jnp.concatenate([fn(*[a[:, i * LANE:(i + 1) * LANE] for a in arrs]) for i in range(n)], axis=1)

    def f_gate0(ridx, g, o, z, gain):
        return (per_head(lambda oh, zh: _rms(oh, gain) * _silu(zh), o, z),)

    gate0_ins = [_In(o0), _In(z0), _In(gdn_out_norm, "const")]
    (gated0,) = _rowwise("gdn_gate", f_gate0, gate0_ins, [_Out("row", (lp, GDN_V_W), BF16)])
    y0 = _mm(gated0, w_out0, "nn", "gdn_out")

    def f_mid(ridx, g, h, y, g_post, g_pre, g_kv):
        h1 = h + _rms(y, g_post)
        return h1, _rms(h1, g_pre), _rms(h1, g_kv)

    mid_ins = [_In(h0), _In(y0), _In(post0, "const"), _In(pre1, "const"), _In(kvn, "const")]
    h1, hn1, hkv = _rowwise("mid", f_mid, mid_ins,
                            [_Out("row", (lp, d)), _Out("row", (lp, d), BF16), _Out("row", (lp, d), BF16)])

    ckv_raw = _mm(hkv, w_ckv, "nn", "kv_down_c")
    kr_raw = _mm(hkv, w_kr, "nn", "kv_down_r")

    def f_ckv(ridx, g, c, r, cs, sn, gain):
        return _rms(c, gain), _rope(r, cs, sn)

    ckv_ins = [_In(ckv_raw), _In(kr_raw), _In(cos_t), _In(sin_t), _In(kvl, "const")]
    ckv, kr = _rowwise("kv_latent", f_ckv, ckv_ins, [_Out("row", (lp, LANE)), _Out("row", (lp, LANE), BF16)],
                       tr=TR_FULL)
    kn = _mm(ckv, w_kn, "nn", "kv_up_k", BF16)
    vv = _mm(ckv, w_v, "nn", "kv_up_v", BF16)
    cq_raw = _mm(hn1, w_cq, "nn", "mla_in_q")
    z1 = _mm(hn1, w_z1, "nn", "mla_in_z")

    def f_cq(ridx, g, c, gain):
        return (_rms(c, gain),)

    cq_ins = [_In(cq_raw), _In(qln, "const")]
    (cq,) = _rowwise("q_latent", f_cq, cq_ins, [_Out("row", (lp, MLA_Q_RANK))], tr=TR_FULL)
    qn = _mm(cq, w_qn, "nn", "q_up_n", BF16)
    qr_raw = _mm(cq, w_qr, "nn", "q_up_r")

    def f_qrope(ridx, g, r, cs, sn):
        return (per_head(lambda rh: _rope(rh, cs, sn), r),)

    qr_ins = [_In(qr_raw), _In(cos_t), _In(sin_t)]
    (qr,) = _rowwise("q_rope", f_qrope, qr_ins, [_Out("row", (lp, MLA_HEADS * LANE), BF16)])
    o1, lse = _flash_fwd(qn, qr, kn, kr, vv, "attention")

    def f_gate1(ridx, g, o, z):
        return (o * _silu(z),)

    gate1_ins = [_In(o1), _In(z1)]
    (og,) = _rowwise("mla_gate", f_gate1, gate1_ins, [_Out("row", (lp, MLA_HEADS * LANE), BF16)])
    y1 = _mm(og, w_out1, "nn", "mla_out")

    def f_final(ridx, g, h, y, t, gain):
        ok = jnp.logical_and(ridx >= ROW0, ridx < ROW0 + seq).astype(F32)

        def rows_loss(h_, y_, gain_):
            err = (h_ + _rms(y_, gain_) - t) * ok
            return 0.5 * jnp.sum(jnp.sum(err * err, axis=1, keepdims=True), axis=0, keepdims=True) / d

        val, vjp = jax.vjp(rows_loss, h, y, gain)
        dh, dy, dgain = vjp(jnp.ones((1, 1), F32))
        return dh, dy, dgain, jnp.broadcast_to(val, (1, LANE))

    dh2, dy1, dpost1, loss_part = _rowwise(
        "loss_head", f_final, [_In(h1), _In(y1), _In(tgt), _In(post1, "const")],
        [_Out("row", (lp, d)), _Out("row", (lp, d)), _Out("acc", (1, d)), _Out("acc", (1, LANE))])

    dog = _mm(dy1, w_out1, "nt", "mla_out_dx")
    dw_out1 = _mm(og, dy1, "tn", "mla_out_dw")
    do1, dz1 = _rowwise_vjp("mla_gate_bwd", f_gate1, gate1_ins, [[dog]], [0, 1])
    dqn, dqr, dkn, dkr, dvv = _flash_bwd(qn, qr, kn, kr, vv, o1, do1, lse, "attention_bwd")
    (dqr_raw,) = _rowwise_vjp("q_rope_bwd", f_qrope, qr_ins, [[dqr]], [0])
    dcq_a = _mm(dqn, w_qn, "nt", "q_up_n_dx")
    dcq_b = _mm(dqr_raw, w_qr, "nt", "q_up_r_dx")
    dw_qn = _mm(cq, dqn, "tn", "q_up_n_dw") * Q_PRESCALE
    dw_qr = _mm(cq, dqr_raw, "tn", "q_up_r_dw") * Q_PRESCALE
    dcq_raw, dqln = _rowwise_vjp("q_latent_bwd", f_cq, cq_ins, [[dcq_a, dcq_b]], [0, 1], tr=TR_FULL)
    dhn1_a = _mm(dcq_raw, w_cq, "nt", "mla_in_q_dx")
    dhn1_b = _mm(dz1, w_z1, "nt", "mla_in_z_dx")
    dw_cq = _mm(hn1, dcq_raw, "tn", "mla_in_q_dw")
    dw_z1 = _mm(hn1, dz1, "tn", "mla_in_z_dw")
    dckv_a = _mm(dkn, w_kn, "nt", "kv_up_k_dx")
    dckv_b = _mm(dvv, w_v, "nt", "kv_up_v_dx")
    dw_kn = _mm(ckv, dkn, "tn", "kv_up_k_dw")
    dw_v = _mm(ckv, dvv, "tn", "kv_up_v_dw")
    dckv_raw, dkr_raw, dkvl = _rowwise_vjp("kv_latent_bwd", f_ckv, ckv_ins, [[dckv_a, dckv_b], [dkr]], [0, 1, 4],
                                           tr=TR_FULL)
    dhkv_a = _mm(dckv_raw, w_ckv, "nt", "kv_down_c_dx")
    dhkv_b = _mm(dkr_raw, w_kr, "nt", "kv_down_r_dx")
    dw_ckv = _mm(hkv, dckv_raw, "tn", "kv_down_c_dw")
    dw_kr = _mm(hkv, dkr_raw, "tn", "kv_down_r_dw")
    dh0_res, dy0, dpost0, dpre1, dkvn = _rowwise_vjp(
        "mid_bwd", f_mid, mid_ins, [[dh2], [dhn1_a, dhn1_b], [dhkv_a, dhkv_b]], [0, 1, 2, 3, 4])

    dgated0 = _mm(dy0, w_out0, "nt", "gdn_out_dx")
    dw_out0 = _mm(gated0, dy0, "tn", "gdn_out_dw")
    do0, dz0, doutn = _rowwise_vjp("gdn_gate_bwd", f_gate0, gate0_ins, [[dgated0]], [0, 1, 2], tr=TR_QUARTER)
    dq0, dk0, dv0, dbeta, dgc = _gdn_bwd(qkv, beta, gc, ckpt, do0, "gdn_scan_bwd")
    db_raw, da_raw, dalog, ddtb = _rowwise_vjp("gdn_gates_bwd", f_ba, ba_ins, [[dbeta], [dgc]], [0, 1, 2, 3])
    dqkv_raw, dconv = _conv_bwd(qkv_raw, conv_w, dq0, dk0, dv0, "gdn_conv_bwd")
    dhn0_a = _mm(dqkv_raw, w_qkv, "nt", "gdn_in_qkv_dx")
    dhn0_b = _mm(dz0, w_z0, "nt", "gdn_in_z_dx")
    dhn0_c = _mm(db_raw, w_b, "nt", "gdn_in_b_dx")
    dhn0_d = _mm(da_raw, w_a, "nt", "gdn_in_a_dx")
    dw_qkv = _mm(hn0, dqkv_raw, "tn", "gdn_in_qkv_dw")
    dw_z0 = _mm(hn0, dz0, "tn", "gdn_in_z_dw")
    dw_b = _mm(hn0, db_raw, "tn", "gdn_in_b_dw")
    dw_a = _mm(hn0, da_raw, "tn", "gdn_in_a_dw")
    dh0, dpre0 = _rowwise_vjp("pre0_bwd", f_pre0, [_In(h0), _In(pre0, "const")],
                              [[dhn0_a, dhn0_b, dhn0_c, dhn0_d], [dh0_res]], [0, 1])

    grad_x = dh0[ROW0:ROW0 + seq][None]
    g_meta = dh0[FRONT:ROW0]
    g_w_in0 = jnp.concatenate([dw_qkv, dw_z0, dw_b[:, :nv], dw_a[:, :nv]], axis=1)
    g_kv_down = jnp.concatenate([dw_ckv, dw_kr[:, :MLA_ROPE]], axis=1)
    g_kv_up = jnp.concatenate([dw_kn.reshape(MLA_KV_RANK, MLA_HEADS, LANE), dw_v.reshape(MLA_KV_RANK, MLA_HEADS, LANE)],
                              axis=2).reshape(MLA_KV_RANK, MLA_HEADS * 2 * LANE)
    g_w_in1 = jnp.concatenate([dw_cq, dw_z1], axis=1)
    g_qup = jnp.concatenate([dw_qn.reshape(MLA_Q_RANK, MLA_HEADS, LANE),
                             dw_qr.reshape(MLA_Q_RANK, MLA_HEADS, LANE)[:, :, :MLA_ROPE]],
                            axis=2).reshape(MLA_Q_RANK, MLA_HEADS * MLA_QK)
    big_g = [g_meta, dconv, dw_out0, g_kv_down, g_kv_up, g_w_in1, g_qup, dw_out1]

    def reduce_to_shard(g_by_chip, tag):
        got = _sibling_split(g_by_chip, "grads_sibling_split" + tag)
        chip_part = _add_pair(g_by_chip, got, "grads_chip_sum" + tag)
        from_chips = _chip_scatter(chip_part, "grads_chip_scatter" + tag)
        half_sum = _sum_slots(from_chips, "grads_total" + tag)
        return _sibling_join(half_sum, "grads_sibling_join" + tag)

    g_flat = reduce_to_shard(_pack_full(big_g, layout), "")
    g_win = reduce_to_shard(jnp.stack([g_w_in0[:, s * win_cols:(s + 1) * win_cols] for s in range(4)]), "_gdn_w_in")

    small_shapes = [(2, d), (2, d), (1, nv), (1, nv), (1, GDN_DK), (d,), (MLA_KV_RANK,), (1, MLA_Q_RANK), (1, LANE)]
    small_part = _pack_small([jnp.concatenate([dpre0, dpre1], axis=0), jnp.concatenate([dpost0, dpost1], axis=0),
                              dalog[:, :nv], ddtb[:, :nv], doutn, dkvn, dkvl, dqln, loss_part])
    small_tot = _all_sum_small(small_part, "small_sum")
    small_g = _unpack_small(small_tot, small_shapes)
    loss = small_g[-1][0, 0]

    d_flat, m_flat, v_flat = _adamw(w_flat, g_flat, _pack_shards(big_m), _pack_shards(big_v), "adamw_sharded")
    win_step = _adamw(gdn_w_in[0], g_win, m_gdn_w_in[0], v_gdn_w_in[0], "adamw_gdn_w_in")
    small_w = [pre_norm, post_norm, gdn_a_log, gdn_dt_bias, gdn_out_norm, kv_norm, kv_latent_norm, mla_q_latent_norm]
    small_m = [m_pre_norm, m_post_norm, m_gdn_a_log, m_gdn_dt_bias, m_gdn_out_norm, m_kv_norm, m_kv_latent_norm,
               m_mla_q_latent_norm]
    small_v = [v_pre_norm, v_post_norm, v_gdn_a_log, v_gdn_dt_bias, v_gdn_out_norm, v_kv_norm, v_kv_latent_norm,
               v_mla_q_latent_norm]
    g_small_flat = _pack_small(small_g[:-1])
    ds_flat, ms_flat, vs_flat = _adamw(_pack_small(small_w), g_small_flat, _pack_small(small_m), _pack_small(small_v),
                                       "adamw_replicated")

    def assemble(big_flat, small_flat, win):
        bigs = dict(zip(big_names, [a.reshape(w.shape) for a, w in zip(
            _unpack_shards(big_flat, layout),
            [meta_tokens, gdn_conv_w, gdn_w_out, kv_w_down, kv_w_up, mla_w_in, mla_w_q_up, mla_w_out])]))
        smalls = dict(zip(["pre_norm", "post_norm", "gdn_a_log", "gdn_dt_bias", "gdn_out_norm", "kv_norm",
                           "kv_latent_norm", "mla_q_latent_norm"], _unpack_small(small_flat, small_shapes[:-1])))
        both = {**bigs, **smalls, "gdn_w_in": win[None]}
        order = ["meta_tokens", "pre_norm", "post_norm", "gdn_w_in", "gdn_conv_w", "gdn_a_log", "gdn_dt_bias",
                 "gdn_out_norm", "gdn_w_out", "kv_norm", "kv_w_down", "kv_latent_norm", "kv_w_up", "mla_w_in",
                 "mla_q_latent_norm", "mla_w_q_up", "mla_w_out"]
        return [both[n] for n in order]

    grads = assemble(g_flat, g_small_flat, g_win)
    deltas = assemble(d_flat, ds_flat, win_step[0])
    new_m = assemble(m_flat, ms_flat, win_step[1])
    new_v = assemble(v_flat, vs_flat, win_step[2])
    return (loss, grad_x, *grads, *deltas, *new_m, *new_v)
```

```python
import functools
import math

import jax
import jax.numpy as jnp
from jax import lax
from jax.experimental import pallas as pl
from jax.experimental.pallas import tpu as pltpu

F32 = jnp.float32
BF16 = jnp.bfloat16
MESH = pl.DeviceIdType.MESH

D_MODEL = 1024
N_META = 16
FRONT = 48
ROW0 = FRONT + N_META
ROW_ALIGN = 768
TR_FULL, TR_HALF, TR_QUARTER = ROW_ALIGN, ROW_ALIGN // 2, ROW_ALIGN // 4
NORM_EPS = 1e-6
LANE = 128

GDN_QK_HEADS = 8
GDN_V_HEADS = 16
GDN_DK = 128
GDN_CHUNK = 64
GDN_QK_W = 1024
GDN_V_W = 2048
GDN_CONV_W = 4096

MLA_HEADS = 16
MLA_NOPE = 128
MLA_ROPE = 64
MLA_QK = 192
MLA_Q_RANK = 256
MLA_KV_RANK = 128
ROPE_THETA = 10000.0

ADAM_LR = 0.001
ADAM_B1 = 0.9
ADAM_B2 = 0.999
ADAM_EPS = 1e-08
ADAM_WD = 0.01
ADAM_STEP = 10

VMEM_LIMIT_V7X = 56 * 1024 * 1024
NEG = -1e30

_NN = ((1,), (0,))
_NT = ((1,), (1,))
_TN = ((0,), (0,))
_HI = lax.Precision.HIGHEST
_X3 = lax.Precision.HIGH


def _pcall(body, **kw):
    return pl.pallas_call(body, **kw)


def _params(n_axes):
    return pltpu.CompilerParams(dimension_semantics=("arbitrary",) * n_axes, vmem_limit_bytes=VMEM_LIMIT_V7X)


def _dot(a, b, dims, prec=None):
    return lax.dot_general(a, b, (dims, ((), ())), precision=prec, preferred_element_type=F32)


def _bdot(a, b, dims):
    return _dot(a.astype(BF16), b.astype(BF16), dims)


def _hdot(a, b, dims=_NN):
    return _dot(a, b, dims, _HI)


def _fdot(a, b, dims):
    return _dot(a, b, dims)


SMALL_MATMUL_DIM = 256
SMALL_MATMUL_ROWS = 1408


def _tile(n):
    if n % ROW_ALIGN == 0:
        return ROW_ALIGN
    for t in (1024, 512, 256, 128):
        if n % t == 0:
            return t
    raise ValueError(n)


def _mm(a, b, mode, name, out_dtype=F32):
    if mode == "nn":
        (m, k), (k2, n) = a.shape, b.shape
    elif mode == "nt":
        (m, k), (n, k2) = a.shape, b.shape
    else:
        (k, m), (k2, n) = a.shape, b.shape
    assert k == k2, (a.shape, b.shape, mode)
    tm, tn, tk = _tile(m), _tile(n), _tile(k)
    if mode != "tn" and min(k, n) <= SMALL_MATMUL_DIM and m % SMALL_MATMUL_ROWS == 0:
        tm = SMALL_MATMUL_ROWS
    nk = k // tk
    dims = {"nn": _NN, "nt": _NT, "tn": _TN}[mode]

    def body(a_ref, b_ref, o_ref, acc):
        kk = pl.program_id(2)

        @pl.when(kk == 0)
        def _():
            acc[...] = jnp.zeros_like(acc)

        acc[...] += _bdot(a_ref[...], b_ref[...], dims)

        @pl.when(kk == nk - 1)
        def _():
            o_ref[...] = acc[...].astype(out_dtype)

    if mode == "tn":
        a_spec = pl.BlockSpec((tk, tm), lambda i, j, kk: (kk, i))
    else:
        a_spec = pl.BlockSpec((tm, tk), lambda i, j, kk: (i, kk))
    if mode == "nt":
        b_spec = pl.BlockSpec((tn, tk), lambda i, j, kk: (j, kk))
    else:
        b_spec = pl.BlockSpec((tk, tn), lambda i, j, kk: (kk, j))
    return _pcall(
        body, name=name, grid=(m // tm, n // tn, nk),
        in_specs=[a_spec, b_spec],
        out_specs=pl.BlockSpec((tm, tn), lambda i, j, kk: (i, j)),
        out_shape=jax.ShapeDtypeStruct((m, n), out_dtype),
        scratch_shapes=[pltpu.VMEM((tm, tn), F32)],
        compiler_params=_params(3),
    )(a, b)


class _In:
    def __init__(self, arr, kind="row", grouped=False, goff=0):
        self.arr, self.kind, self.grouped, self.goff = arr, kind, grouped, goff


class _Out:
    def __init__(self, kind, shape, dtype=F32, grouped=False):
        self.kind, self.shape, self.dtype, self.grouped = kind, shape, dtype, grouped


def _rowwise(name, fn, ins, outs, *, groups=1, tr=TR_HALF):
    lp = next(i.arr.shape[0] for i in ins if i.kind == "row")
    nr = lp // tr
    assert lp % tr == 0

    def in_spec(i):
        w = i.arr.shape[1]
        if i.kind == "row":
            if i.grouped:
                return pl.BlockSpec((tr, LANE), lambda g, r, o=i.goff: (r, g + o))
            return pl.BlockSpec((tr, w), lambda g, r: (r, 0))
        if i.grouped:
            return pl.BlockSpec((i.arr.shape[0], LANE), lambda g, r, o=i.goff: (0, g + o))
        return pl.BlockSpec(i.arr.shape, lambda g, r: (0, 0))

    def out_spec(o):
        if o.kind == "row":
            if o.grouped:
                return pl.BlockSpec((tr, LANE), lambda g, r: (r, g))
            assert groups == 1
            return pl.BlockSpec((tr, o.shape[1]), lambda g, r: (r, 0))
        if o.grouped:
            return pl.BlockSpec((o.shape[0], LANE), lambda g, r: (0, g))
        return pl.BlockSpec(o.shape, lambda g, r: (0, 0))

    n_in = len(ins)

    def body(*refs):
        g = pl.program_id(0)
        r = pl.program_id(1)
        ridx = r * tr + lax.broadcasted_iota(jnp.int32, (tr, 1), 0)
        res = fn(ridx, g, *[ref[...] for ref in refs[:n_in]])
        assert len(res) == len(outs), (name, len(res), len(outs))
        for o, ref, val in zip(outs, refs[n_in:], res):
            if o.kind == "row":
                ref[...] = val.astype(o.dtype)
            else:
                first = (r == 0) if o.grouped else jnp.logical_and(r == 0, g == 0)

                @pl.when(first)
                def _(ref=ref, val=val):
                    ref[...] = val.astype(F32)

                @pl.when(jnp.logical_not(first))
                def _(ref=ref, val=val):
                    ref[...] += val.astype(F32)

    res = _pcall(
        body, name=name, grid=(groups, nr),
        in_specs=[in_spec(i) for i in ins],
        out_specs=[out_spec(o) for o in outs],
        out_shape=[jax.ShapeDtypeStruct(o.shape, o.dtype) for o in outs],
        compiler_params=_params(2),
    )(*[i.arr for i in ins])
    return res


def _rowwise_vjp(name, fn, ins, cots, diff, *, groups=1, tr=TR_HALF):
    n_in = len(ins)
    grouped = groups > 1
    cot_ins = []
    counts = []
    for arrs in cots:
        counts.append(len(arrs))
        for a in arrs:
            cot_ins.append(_In(a, "row", grouped=grouped and a.shape[1] > LANE))
    lp = next(i.arr.shape[0] for i in ins if i.kind == "row")
    outs = []
    for d in diff:
        i = ins[d]
        if i.kind == "row":
            w = groups * LANE if i.grouped else i.arr.shape[1]
            outs.append(_Out("row", (lp, w), F32, grouped=i.grouped))
        else:
            outs.append(_Out("acc", i.arr.shape, F32, grouped=i.grouped))

    def bfn(ridx, g, *allvals):
        vals = list(allvals[:n_in])
        cvals = allvals[n_in:]

        def f(*dv):
            full = list(vals)
            for i, v in zip(diff, dv):
                full[i] = v
            return tuple(fn(ridx, g, *full))

        primal, vjp = jax.vjp(f, *[vals[i].astype(F32) for i in diff])
        cts = []
        pos = 0
        for k, cnt in enumerate(counts):
            if cnt == 0:
                cts.append(jnp.zeros_like(primal[k]))
            else:
                c = cvals[pos].astype(F32)
                for extra in cvals[pos + 1:pos + cnt]:
                    c = c + extra.astype(F32)
                w = primal[k].shape[1]
                if c.shape[1] != w:
                    c = functools.reduce(jnp.add, [c[:, i * w:(i + 1) * w] for i in range(c.shape[1] // w)])
                cts.append(c.astype(primal[k].dtype))
            pos += cnt
        return vjp(tuple(cts))

    return _rowwise(name, bfn, list(ins) + cot_ins, outs, groups=groups, tr=tr)


def _rms(x, g):
    return x * lax.rsqrt(jnp.mean(x * x, axis=-1, keepdims=True) + NORM_EPS) * g


def _silu(x):
    return x * jax.nn.sigmoid(x)


def _softplus(x):
    return jnp.maximum(x, 0.0) + jnp.log(1.0 + jnp.exp(-jnp.abs(x)))


def _swap_halves(x):
    lane = lax.broadcasted_iota(jnp.int32, x.shape, x.ndim - 1)
    return jnp.where(lane < 32, pltpu.roll(x, LANE - 32, x.ndim - 1), pltpu.roll(x, 32, x.ndim - 1))


@jax.custom_vjp
def _rope(x, c, s):
    return x * c + _swap_halves(x) * s


def _rope_fwd(x, c, s):
    return _rope(x, c, s), (c, s)


def _rope_bwd(res, dy):
    c, s = res
    return dy * c + _swap_halves(dy * s), jnp.zeros_like(c), jnp.zeros_like(s)


_rope.defvjp(_rope_fwd, _rope_bwd)


def _conv_post(c, g):
    s = _silu(c)
    n = s * lax.rsqrt(jnp.sum(s * s, axis=-1, keepdims=True) + NORM_EPS)
    return jnp.where(g < GDN_QK_HEADS, n * (GDN_DK ** -0.5), jnp.where(g < 2 * GDN_QK_HEADS, n, s))


def _conv_taps(xe, w):
    c = xe[8:] * w[3]
    for s in (1, 2, 3):
        c = c + pltpu.roll(xe, s, 0)[8:] * w[3 - s]
    return c


CONV_LANES = 512
CONV_HEADS = CONV_LANES // LANE


def _conv_post_block(c, g):
    return jnp.concatenate([_conv_post(c[:, i * LANE:(i + 1) * LANE], g * CONV_HEADS + i)
                            for i in range(CONV_HEADS)], axis=1)


def _conv_fwd(x, w, name, tr=TR_FULL):
    lp, width = x.shape
    cl = CONV_LANES
    nr = lp // tr

    def body(x_ref, prev_ref, w_ref, o_ref):
        g = pl.program_id(0)
        r = pl.program_id(1)
        prev = jnp.where(r > 0, prev_ref[...], 0.0)
        xe = jnp.concatenate([prev, x_ref[...]], axis=0)
        o_ref[...] = _conv_post_block(_conv_taps(xe, [w_ref[t:t + 1, :] for t in range(4)]), g)

    return _pcall(
        body, name=name, grid=(width // cl, nr),
        in_specs=[pl.BlockSpec((tr, cl), lambda g, r: (r, g)),
                  pl.BlockSpec((8, cl), lambda g, r: (jnp.maximum(r * (tr // 8) - 1, 0), g)),
                  pl.BlockSpec((4, cl), lambda g, r: (0, g))],
        out_specs=pl.BlockSpec((tr, cl), lambda g, r: (r, g)),
        out_shape=jax.ShapeDtypeStruct((lp, width), F32),
        compiler_params=_params(2),
    )(x, x, w)


def _conv_bwd(x, w, dq, dk, dv, name, tr=TR_FULL):
    lp, width = x.shape
    cl = CONV_LANES
    nr = lp // tr
    last8 = lp // 8 - 1
    nq = GDN_QK_W // cl

    def body(x_ref, prev_ref, next_ref, w_ref, q_ref, k_ref, v_ref, q_n, k_n, v_n, dx_ref, dw_ref):
        g = pl.program_id(0)
        r = pl.program_id(1)
        w = [w_ref[t:t + 1, :] for t in range(4)]
        not_last = r < nr - 1

        def pick(a, b, c):
            return jnp.where(g < nq, a[...], jnp.where(g < 2 * nq, b[...], c[...]))

        dy = pick(q_ref, k_ref, v_ref)
        dyn = jnp.where(not_last, pick(q_n, k_n, v_n), 0.0)
        prev = jnp.where(r > 0, prev_ref[...], 0.0)
        nxt = jnp.where(not_last, next_ref[...], 0.0)
        xe = jnp.concatenate([prev, x_ref[...], nxt], axis=0)
        ce = _conv_taps(xe, w)
        _, vjp = jax.vjp(lambda c: _conv_post_block(c, g), ce)
        (dce,) = vjp(jnp.concatenate([dy, dyn], axis=0))
        n = tr + 8
        dx = dce * w[3]
        for s in (1, 2, 3):
            dx = dx + pltpu.roll(dce, n - s, 0) * w[3 - s]
        dx_ref[...] = dx[:tr]
        dc = dce[:tr]
        row4 = lax.broadcasted_iota(jnp.int32, (4, cl), 0)
        dw = jnp.zeros((4, cl), F32)
        for s in (0, 1, 2, 3):
            xs = xe[8:8 + tr] if s == 0 else pltpu.roll(xe, s, 0)[8:8 + tr]
            dw = dw + jnp.where(row4 == 3 - s, jnp.sum(dc * xs, axis=0, keepdims=True), 0.0)

        @pl.when(r == 0)
        def _():
            dw_ref[...] = dw

        @pl.when(r > 0)
        def _():
            dw_ref[...] += dw

    def col_q(g):
        return jnp.minimum(g, nq - 1)

    def col_k(g):
        return jnp.clip(g - nq, 0, nq - 1)

    def col_v(g):
        return jnp.maximum(g - 2 * nq, 0)

    def blk(colf):
        return pl.BlockSpec((tr, cl), lambda g, r: (r, colf(g)))

    def nblk(colf):
        return pl.BlockSpec((8, cl), lambda g, r: (jnp.minimum((r + 1) * (tr // 8), last8), colf(g)))

    return _pcall(
        body, name=name, grid=(width // cl, nr),
        in_specs=[pl.BlockSpec((tr, cl), lambda g, r: (r, g)),
                  pl.BlockSpec((8, cl), lambda g, r: (jnp.maximum(r * (tr // 8) - 1, 0), g)),
                  pl.BlockSpec((8, cl), lambda g, r: (jnp.minimum((r + 1) * (tr // 8), last8), g)),
                  pl.BlockSpec((4, cl), lambda g, r: (0, g)),
                  blk(col_q), blk(col_k), blk(col_v), nblk(col_q), nblk(col_k), nblk(col_v)],
        out_specs=[pl.BlockSpec((tr, cl), lambda g, r: (r, g)),
                   pl.BlockSpec((4, cl), lambda g, r: (0, g))],
        out_shape=[jax.ShapeDtypeStruct((lp, width), F32), jax.ShapeDtypeStruct((4, width), F32)],
        compiler_params=_params(2),
    )(x, x, x, w, dq, dk, dv, dq, dk, dv)


def _bmm(a, b, dims, prec=None):
    (ca,), (cb,) = dims
    return lax.dot_general(a, b, (((ca + 1,), (cb + 1,)), ((0,), (0,))), precision=prec,
                           preferred_element_type=F32)


def _inv_impl(m):
    c = m.shape[-1]
    ii = lax.broadcasted_iota(jnp.int32, (c, c), 0)
    jj = lax.broadcasted_iota(jnp.int32, (c, c), 1)
    eye = (ii == jj).astype(F32)

    def same_block(shift):
        return (ii >> shift) == (jj >> shift)

    n1 = jnp.where(same_block(3), -m, 0.0)
    n2 = _bmm(n1, n1, _NN, _X3)
    n4 = _bmm(n2, n2, _NN, _X3)
    d = _bmm(_bmm(eye + n1, eye + n2, _NN, _X3), eye + n4, _NN, _X3)
    shift = 3
    while (1 << shift) < c:
        low = jnp.where(jnp.logical_and(same_block(shift + 1), jnp.logical_not(same_block(shift))), m, 0.0)
        d = d - _bmm(d, _bmm(low, d, _NN, _X3), _NN, _X3)
        shift += 1
    return d


@jax.custom_vjp
def _inv_unit_lower(m):
    return _inv_impl(m)


def _inv_f(m):
    t = _inv_impl(m)
    return t, t


def _inv_b(t, dt):
    c = t.shape[-1]
    ii = lax.broadcasted_iota(jnp.int32, (c, c), 0)
    jj = lax.broadcasted_iota(jnp.int32, (c, c), 1)
    gm = _bmm(t, _bmm(dt, t, _NT, _X3), _TN, _X3)
    return (jnp.where(ii > jj, -gm, 0.0),)


_inv_unit_lower.defvjp(_inv_f, _inv_b)


GDN_HEADS_PER_STEP = 16


def _gdn_group(q, k, v, beta_blk, gc_blk, states, h0):
    hp = GDN_HEADS_PER_STEP
    c = q.shape[0]
    lane = lax.broadcasted_iota(jnp.int32, (1, LANE), 1)
    row8 = lax.broadcasted_iota(jnp.int32, (max(8, hp), LANE), 0)
    lane8 = lax.broadcasted_iota(jnp.int32, (max(8, hp), LANE), 1)
    gcr_all = _hdot((lane8 == h0 + row8).astype(F32), gc_blk, _NT)
    betas, gccs = [], []
    for i in range(hp):
        onehot = (lane == h0 + i).astype(F32)
        betas.append(jnp.sum(beta_blk * onehot, axis=1, keepdims=True))
        gccs.append(jnp.sum(gc_blk * onehot, axis=1, keepdims=True))
    beta = jnp.stack(betas)
    gcc = jnp.stack(gccs)
    gcr = jnp.stack([gcr_all[i:i + 1] for i in range(hp)])
    qh = jnp.stack([q[:, (i // 2) * LANE:(i // 2 + 1) * LANE] for i in range(hp)])
    kh = jnp.stack([k[:, (i // 2) * LANE:(i // 2 + 1) * LANE] for i in range(hp)])
    vh = jnp.stack([v[:, i * LANE:(i + 1) * LANE] for i in range(hp)])
    state = jnp.stack(states)
    ii = lax.broadcasted_iota(jnp.int32, (c, c), 0)
    jj = lax.broadcasted_iota(jnp.int32, (c, c), 1)
    incl = ii >= jj
    dec = jnp.where(incl, jnp.exp(jnp.where(incl, gcc - gcr, 0.0)), 0.0)
    eg = jnp.exp(gcc)
    m = _bmm(kh, kh, _NT) * beta * jnp.where(ii > jj, dec, 0.0)
    t = _inv_unit_lower(m)
    u = _bmm(t, vh * beta, _NN, _X3)
    w = _bmm(t, kh * (beta * eg), _NN, _X3)
    attn = _bmm(qh, kh, _NT) * dec
    rows = lax.broadcasted_iota(jnp.int32, (c, 1), 0)
    gl = jnp.sum(jnp.where(rows == c - 1, gcc, 0.0), axis=1, keepdims=True)
    v_new = u - _bmm(w, state, _NN)
    o = _bmm(qh * eg, state, _NN) + _bmm(attn, v_new, _NN)
    new_state = state * jnp.exp(gl) + _bmm(kh * jnp.exp(gl - gcc), v_new, _TN)
    return jnp.concatenate([o[i] for i in range(hp)], axis=1), tuple(new_state[i] for i in range(hp))


def _gdn_specs(nc, rev):
    def cidx(n):
        return (nc - 1 - n) if rev else n
    hp = GDN_HEADS_PER_STEP
    nqk = GDN_QK_HEADS
    c = GDN_CHUNK
    nq = 2 * nqk // hp
    q_spec = pl.BlockSpec((c, hp // 2 * LANE), lambda n, g: (cidx(n), g))
    k_spec = pl.BlockSpec((c, hp // 2 * LANE), lambda n, g: (cidx(n), nq + g))
    v_spec = pl.BlockSpec((c, hp * LANE), lambda n, g: (cidx(n), nq + g))
    s_spec = pl.BlockSpec((c, LANE), lambda n, g: (cidx(n), 0))
    o_spec = pl.BlockSpec((c, hp * LANE), lambda n, g: (cidx(n), g))
    ck_spec = pl.BlockSpec((hp, 1, GDN_DK, LANE), lambda n, g: (g, cidx(n), 0, 0))
    return q_spec, k_spec, v_spec, s_spec, o_spec, ck_spec


def _gdn_fwd(qkv, beta, gc, shard, name):
    lp = qkv.shape[0]
    nc = lp // GDN_CHUNK
    nh = GDN_V_HEADS
    hp = GDN_HEADS_PER_STEP
    ng = nh // hp
    q_spec, k_spec, v_spec, s_spec, o_spec, ck_spec = _gdn_specs(nc, False)

    def body(q_ref, k_ref, v_ref, b_ref, g_ref, x_ref, o_ref, ck_ref, all_ref, state, send_sems, recv_sems, local_sem):
        n = pl.program_id(0)
        g = pl.program_id(1)
        x, y, c = _my_place()

        def local_copy():
            return pltpu.make_async_copy(x_ref, all_ref.at[2 * x + y], local_sem)

        def remote_copy(k, px, py, slot):
            return pltpu.make_async_remote_copy(
                src_ref=x_ref, dst_ref=all_ref.at[slot], send_sem=send_sems.at[k], recv_sem=recv_sems.at[k],
                device_id=(px, py, c), device_id_type=MESH)

        @pl.when(jnp.logical_and(n == 0, g == 0))
        def _():
            local_copy().start()
            for k, (px, py) in enumerate(_other_chips(x, y)):
                remote_copy(k, px, py, 2 * x + y).start()

        @pl.when(n == 0)
        def _():
            for i in range(hp):
                state[g * hp + i] = jnp.zeros((GDN_DK, LANE), F32)

        states = tuple(state[g * hp + i] for i in range(hp))
        for i in range(hp):
            ck_ref[i, 0] = states[i]
        o, new_states = _gdn_group(q_ref[...], k_ref[...], v_ref[...], b_ref[...], g_ref[...], states, g * hp)
        o_ref[...] = o
        for i in range(hp):
            state[g * hp + i] = new_states[i]

        @pl.when(jnp.logical_and(n == nc - 1, g == ng - 1))
        def _():
            for k, (px, py) in enumerate(_other_chips(x, y)):
                remote_copy(k, px, py, 2 * px + py).wait_recv()
            for k, (px, py) in enumerate(_other_chips(x, y)):
                remote_copy(k, px, py, 2 * x + y).wait_send()
            local_copy().wait()

    return _pcall(
        body, name=name, grid=(nc, ng),
        in_specs=[q_spec, k_spec, v_spec, s_spec, s_spec, _ANY],
        out_specs=[o_spec, ck_spec, _ANY],
        out_shape=[jax.ShapeDtypeStruct((lp, GDN_V_W), F32),
                   jax.ShapeDtypeStruct((nh, nc, GDN_DK, LANE), F32),
                   jax.ShapeDtypeStruct((4,) + shard.shape, shard.dtype)],
        scratch_shapes=[pltpu.VMEM((nh, GDN_DK, LANE), F32), pltpu.SemaphoreType.DMA((3,)),
                        pltpu.SemaphoreType.DMA((3,)), pltpu.SemaphoreType.DMA],
        compiler_params=_params(2),
    )(qkv, qkv, qkv, beta, gc, shard)


def _gdn_bwd(qkv, beta, gc, ckpt, do, name):
    lp = qkv.shape[0]
    nc = lp // GDN_CHUNK
    nh = GDN_V_HEADS
    hp = GDN_HEADS_PER_STEP
    q_spec, k_spec, v_spec, s_spec, o_spec, ck_spec = _gdn_specs(nc, True)

    def body(q_ref, k_ref, v_ref, b_ref, g_ref, ck_ref, do_ref,
             dq_ref, dk_ref, dv_ref, db_ref, dg_ref, dstate):
        n = pl.program_id(0)
        g = pl.program_id(1)

        @pl.when(n == 0)
        def _():
            for i in range(hp):
                dstate[g * hp + i] = jnp.zeros((GDN_DK, LANE), F32)

        states = tuple(ck_ref[i, 0] for i in range(hp))
        _, vjp = jax.vjp(lambda q, k, v, b, gg, s: _gdn_group(q, k, v, b, gg, s, g * hp),
                         q_ref[...], k_ref[...], v_ref[...], b_ref[...], g_ref[...], states)
        dq, dk, dv, db, dg, ds = vjp((do_ref[...], tuple(dstate[g * hp + i] for i in range(hp))))
        dq_ref[...] = dq
        dk_ref[...] = dk
        dv_ref[...] = dv
        for i in range(hp):
            dstate[g * hp + i] = ds[i]

        @pl.when(g == 0)
        def _():
            db_ref[...] = db
            dg_ref[...] = dg

        @pl.when(g > 0)
        def _():
            db_ref[...] += db
            dg_ref[...] += dg

    qk_shape = jax.ShapeDtypeStruct((lp, GDN_QK_W), F32)
    big = jax.ShapeDtypeStruct((lp, GDN_V_W), F32)
    small = jax.ShapeDtypeStruct((lp, LANE), F32)
    dq_spec = pl.BlockSpec((GDN_CHUNK, hp // 2 * LANE), lambda n, g: (nc - 1 - n, g))
    return _pcall(
        body, name=name, grid=(nc, nh // hp),
        in_specs=[q_spec, k_spec, v_spec, s_spec, s_spec, ck_spec, o_spec],
        out_specs=[dq_spec, dq_spec, o_spec, s_spec, s_spec],
        out_shape=[qk_shape, qk_shape, big, small, small],
        scratch_shapes=[pltpu.VMEM((nh, GDN_DK, LANE), F32)],
        compiler_params=_params(2),
    )(qkv, qkv, qkv, beta, gc, ckpt, do)


LOG2E = 1.4426950408889634
LN2 = 0.6931471805599453
Q_PRESCALE = MLA_QK ** -0.5 * LOG2E


ATT_SUB = 128
ATT_HEADS_PER_STEP = 4
ATT_BWD_HEADS_PER_STEP = 2


def _att_mask(i, j, tb, transposed):
    r = lax.broadcasted_iota(jnp.int32, (tb, tb), 0)
    c = lax.broadcasted_iota(jnp.int32, (tb, tb), 1)
    qpos, kpos = (i * tb + c, j * tb + r) if transposed else (i * tb + r, j * tb + c)
    return jnp.logical_and(kpos <= qpos, kpos >= FRONT)


def _causal_pairs(nb, by_key):
    if by_key:
        pairs = [(i, j) for j in range(nb) for i in range(j, nb)]
    else:
        pairs = [(i, j) for i in range(nb) for j in range(i + 1)]
    return jnp.array([p[0] for p in pairs], jnp.int32), jnp.array([p[1] for p in pairs], jnp.int32)


def _masked_and_plain(i, j, step):
    edge = jnp.logical_or(j == i, j == 0)

    @pl.when(jnp.logical_and(edge, j <= i))
    def _():
        step(True)

    @pl.when(jnp.logical_and(jnp.logical_not(edge), j < i))
    def _():
        step(False)


def _cat(a_ref, b_ref):
    return jnp.concatenate([a_ref[...], b_ref[...]], axis=1)


def _flash_fwd(qn, qr, kn, kr, v, name, tb=ROW_ALIGN):
    lp = qn.shape[0]
    nb = lp // tb
    nh = MLA_HEADS
    hp = ATT_HEADS_PER_STEP
    qi, kj = _causal_pairs(nb, by_key=False)

    def body(qi_ref, kj_ref, qn_ref, qr_ref, kn_ref, kr_ref, v_ref, o_ref, lse_ref, m_s, l_s, acc):
        t = pl.program_id(1)
        i, j = qi_ref[t], kj_ref[t]

        @pl.when(j == 0)
        def _():
            m_s[...] = jnp.full_like(m_s, NEG)
            l_s[...] = jnp.zeros_like(l_s)
            acc[...] = jnp.zeros_like(acc)

        def step(masked):
            n_sub = tb // ATT_SUB
            kr = kr_ref[...]
            for e in range(hp):
                lanes = pl.ds(e * LANE, LANE)
                k = jnp.concatenate([kn_ref[:, lanes], kr], axis=1)
                v = v_ref[:, lanes]

                def scores(r, lanes=lanes, k=k):
                    rows = pl.ds(r * ATT_SUB, ATT_SUB)
                    return _dot(jnp.concatenate([qn_ref[rows, lanes], qr_ref[rows, lanes]], axis=1), k, _NT)

                s_next = scores(0)
                for r in range(n_sub):
                    s = s_next
                    if r + 1 < n_sub:
                        s_next = scores(r + 1)
                    rows = pl.ds(r * ATT_SUB, ATT_SUB)
                    if masked:
                        qpos = i * tb + r * ATT_SUB + lax.broadcasted_iota(jnp.int32, (ATT_SUB, tb), 0)
                        kpos = j * tb + lax.broadcasted_iota(jnp.int32, (ATT_SUB, tb), 1)
                        s = jnp.where(jnp.logical_and(kpos <= qpos, kpos >= FRONT), s, NEG)
                    m_old = m_s[e, rows, :]
                    m_new = jnp.maximum(m_old, jnp.max(s, axis=1, keepdims=True))
                    alpha = jnp.exp2(m_old - m_new)
                    p = jnp.exp2(s - m_new)
                    l_s[e, rows, :] = alpha * l_s[e, rows, :] + jnp.sum(p, axis=1, keepdims=True)
                    acc[e, rows, :] = alpha * acc[e, rows, :] + _dot(p.astype(BF16), v, _NN)
                    m_s[e, rows, :] = m_new

        _masked_and_plain(i, j, step)

        @pl.when(j == i)
        def _():
            for e in range(hp):
                lanes = pl.ds(e * LANE, LANE)
                o_ref[:, lanes] = acc[e] / l_s[e]
                lse_ref[:, lanes] = jnp.broadcast_to(m_s[e] + jnp.log(l_s[e]) * LOG2E, (tb, LANE))

    qspec = pl.BlockSpec((tb, hp * LANE), lambda h, t, qi_, kj_: (qi_[t], h))
    kspec = pl.BlockSpec((tb, hp * LANE), lambda h, t, qi_, kj_: (kj_[t], h))
    krspec = pl.BlockSpec((tb, LANE), lambda h, t, qi_, kj_: (kj_[t], 0))
    shp = jax.ShapeDtypeStruct((lp, nh * LANE), F32)
    return _pcall(
        body, name=name, out_shape=[shp, shp],
        grid_spec=pltpu.PrefetchScalarGridSpec(
            num_scalar_prefetch=2, grid=(nh // hp, qi.shape[0]),
            in_specs=[qspec, qspec, kspec, krspec, kspec], out_specs=[qspec, qspec],
            scratch_shapes=[pltpu.VMEM((hp, tb, 1), F32), pltpu.VMEM((hp, tb, 1), F32),
                            pltpu.VMEM((hp, tb, LANE), F32)]),
        compiler_params=_params(2),
    )(qi, kj, qn, qr, kn, kr, v)


def _flash_bwd(qn, qr, kn, kr, v, o, do, lse, name, tb=ROW_ALIGN):
    lp = qn.shape[0]
    nb = lp // tb
    nh = MLA_HEADS
    hp = ATT_BWD_HEADS_PER_STEP
    qi, kj = _causal_pairs(nb, by_key=True)
    n_pairs = qi.shape[0]
    knt, krt = kn.T, kr.T

    def body(qi_ref, kj_ref, qn_ref, qr_ref, kn_ref, kr_ref, knt_ref, krt_ref, v_ref, o_ref, do_ref, lse_ref,
             dqnt_hbm, dqrt_hbm, dkn_ref, dkr_ref, dv_ref, dk_acc, dv_acc, dqn_acc, dqr_acc, out_sems):
        g = pl.program_id(0)
        t = pl.program_id(1)
        i, j = qi_ref[t], kj_ref[t]

        @pl.when(t == 0)
        def _():
            dqn_acc[...] = jnp.zeros_like(dqn_acc)
            dqr_acc[...] = jnp.zeros_like(dqr_acc)

        @pl.when(i == j)
        def _():
            dk_acc[...] = jnp.zeros_like(dk_acc)
            dv_acc[...] = jnp.zeros_like(dv_acc)

        def step(masked):
            kr = kr_ref[...]
            krt_blk = krt_ref[...]
            lane = lax.broadcasted_iota(jnp.int32, (8, LANE), 1)
            for e in range(hp):
                lanes = pl.ds(e * LANE, LANE)
                q = jnp.concatenate([qn_ref[:, lanes], qr_ref[:, lanes]], axis=1)
                st = _dot(jnp.concatenate([kn_ref[:, lanes], kr], axis=1), q, _NT)
                if masked:
                    st = jnp.where(_att_mask(i, j, tb, True), st, NEG)
                do_blk = do_ref[:, lanes]
                lse_row = _hdot((lane == 0).astype(F32), lse_ref[:, lanes], _NT)[0:1]
                delta_row = _hdot(jnp.ones((8, LANE), F32), do_blk * o_ref[:, lanes], _NT)[0:1]
                pt = jnp.exp2(st - lse_row)
                do_b = do_blk.astype(BF16)
                dv_acc[e] += _dot(pt.astype(BF16), do_b, _NN)
                dpt = _dot(v_ref[:, lanes], do_b, _NT)
                dst = (pt * (dpt - delta_row)).astype(BF16)
                dk_acc[e] += _dot(dst, q, _NN)
                dqn_acc[e * nb + i] += _dot(knt_ref[pl.ds(e * LANE, LANE), :], dst, _NN) * LN2
                dqr_acc[e * nb + i] += _dot(krt_blk, dst, _NN) * LN2

        _masked_and_plain(i, j, step)

        @pl.when(i == nb - 1)
        def _():
            for e in range(hp):
                lanes = pl.ds(e * LANE, LANE)
                dkn_ref[:, lanes] = dk_acc[e, :, :LANE] * LN2
                dkr_ref[:, lanes] = dk_acc[e, :, LANE:] * LN2
                dv_ref[:, lanes] = dv_acc[e]

        @pl.when(t == n_pairs - 1)
        def _():
            dst_rows = pl.ds(g * (hp * nb), hp * nb)
            cn = pltpu.make_async_copy(dqn_acc, dqnt_hbm.at[dst_rows], out_sems.at[0])
            cr = pltpu.make_async_copy(dqr_acc, dqrt_hbm.at[dst_rows], out_sems.at[1])
            cn.start()
            cr.start()
            cn.wait()
            cr.wait()

    qspec = pl.BlockSpec((tb, hp * LANE), lambda h, t, qi_, kj_: (qi_[t], h))
    kspec = pl.BlockSpec((tb, hp * LANE), lambda h, t, qi_, kj_: (kj_[t], h))
    krspec = pl.BlockSpec((tb, LANE), lambda h, t, qi_, kj_: (kj_[t], 0))
    ktspec = pl.BlockSpec((hp * LANE, tb), lambda h, t, qi_, kj_: (h, kj_[t]))
    krtspec = pl.BlockSpec((LANE, tb), lambda h, t, qi_, kj_: (0, kj_[t]))
    shp = jax.ShapeDtypeStruct((lp, nh * LANE), F32)
    dqt_shape = jax.ShapeDtypeStruct((nh * nb, LANE, tb), F32)
    dqnt, dqrt, dkn, dkr, dv = _pcall(
        body, name=name, out_shape=[dqt_shape, dqt_shape, shp, shp, shp],
        grid_spec=pltpu.PrefetchScalarGridSpec(
            num_scalar_prefetch=2, grid=(nh // hp, n_pairs),
            in_specs=[qspec, qspec, kspec, krspec, ktspec, krtspec, kspec, qspec, qspec, qspec],
            out_specs=[_ANY, _ANY, kspec, kspec, kspec],
            scratch_shapes=[pltpu.VMEM((hp, tb, 2 * LANE), F32), pltpu.VMEM((hp, tb, LANE), F32),
                            pltpu.VMEM((hp * nb, LANE, tb), F32), pltpu.VMEM((hp * nb, LANE, tb), F32),
                            pltpu.SemaphoreType.DMA((2,))]),
        compiler_params=_params(2),
    )(qi, kj, qn, qr, kn, kr, knt, krt, v, o, do, lse)

    def rows_major(a):
        return a.reshape(nh, nb, LANE, tb).transpose(1, 3, 0, 2).reshape(lp, nh * LANE)

    return rows_major(dqnt), rows_major(dqrt), dkn, dkr, dv


ELEMENTWISE_BLOCK_BYTES = 1 << 20


def _row_tile(rows, width, copies=1):
    for t in (1024, 512, 256, 128, 64, 32, 16, 8):
        if rows % t == 0 and t * width * 4 * copies <= ELEMENTWISE_BLOCK_BYTES:
            return t
    return rows


def _adamw(w, g, m, v, name):
    rows, width = w.shape
    tr = _row_tile(rows, width)

    def body(w_ref, g_ref, m_ref, v_ref, d_ref, nm_ref, nv_ref):
        gg = g_ref[...]
        nm = ADAM_B1 * m_ref[...] + (1.0 - ADAM_B1) * gg
        nv = ADAM_B2 * v_ref[...] + (1.0 - ADAM_B2) * jnp.square(gg)
        m_hat = nm / (1.0 - ADAM_B1 ** ADAM_STEP)
        v_hat = nv / (1.0 - ADAM_B2 ** ADAM_STEP)
        d_ref[...] = -ADAM_LR * (m_hat / (jnp.sqrt(v_hat) + ADAM_EPS) + ADAM_WD * w_ref[...])
        nm_ref[...] = nm
        nv_ref[...] = nv

    spec = pl.BlockSpec((tr, width), lambda r: (r, 0))
    shp = jax.ShapeDtypeStruct((rows, width), F32)
    return _pcall(body, name=name, grid=(rows // tr,), in_specs=[spec] * 4, out_specs=[spec] * 3,
                  out_shape=[shp] * 3, compiler_params=_params(1))(w, g, m, v)


def _add_pair(a, b, name):
    s, rows, width = b.shape
    tr = _row_tile(rows, width)
    nt = rows // tr

    def body(c_ref, a_ref, b_ref, o_ref):
        o_ref[...] = a_ref[...] + b_ref[...]

    spec = pl.BlockSpec((1, tr, width), lambda i, r, c_ref: (i, r, 0))
    return _pcall(
        body, name=name, out_shape=jax.ShapeDtypeStruct(b.shape, F32),
        grid_spec=pltpu.PrefetchScalarGridSpec(
            num_scalar_prefetch=1, grid=(s, nt),
            in_specs=[pl.BlockSpec((1, tr, width), lambda i, r, c_ref: (i, c_ref[0] * nt + r, 0)), spec],
            out_specs=spec),
        compiler_params=_params(2),
    )(_core_index(), a, b)


def _sum_slots(a, name):
    s, rows, width = a.shape
    tr = _row_tile(rows, width, copies=s)

    def body(a_ref, o_ref):
        tot = a_ref[0]
        for k in range(1, s):
            tot = tot + a_ref[k]
        o_ref[...] = tot

    return _pcall(body, name=name, grid=(rows // tr,),
                  in_specs=[pl.BlockSpec((s, tr, width), lambda r: (0, r, 0))],
                  out_specs=pl.BlockSpec((tr, width), lambda r: (r, 0)),
                  out_shape=jax.ShapeDtypeStruct((rows, width), F32), compiler_params=_params(1))(a)


_ANY = pl.BlockSpec(memory_space=pl.ANY)


def _my_place():
    return lax.axis_index("x"), lax.axis_index("y"), lax.axis_index("c")


def _core_index():
    return lax.axis_index("c").astype(jnp.int32).reshape(1)


def _other_chips(x, y):
    return [(1 - x, y), (x, 1 - y), (1 - x, 1 - y)]


def _gather_shards(flat, name):
    rows, width = flat.shape

    def body(x_ref, out_ref, send_sems, recv_sems, local_sem):
        x, y, c = _my_place()
        mine = pltpu.make_async_copy(x_ref, out_ref.at[2 * x + y], local_sem)
        mine.start()
        sends = []
        for k, (px, py) in enumerate(_other_chips(x, y)):
            cp = pltpu.make_async_remote_copy(
                src_ref=x_ref, dst_ref=out_ref.at[2 * x + y], send_sem=send_sems.at[k], recv_sem=recv_sems.at[k],
                device_id=(px, py, c), device_id_type=MESH)
            cp.start()
            sends.append(cp)
        for k, (px, py) in enumerate(_other_chips(x, y)):
            pltpu.make_async_remote_copy(
                src_ref=x_ref, dst_ref=out_ref.at[2 * px + py], send_sem=send_sems.at[k], recv_sem=recv_sems.at[k],
                device_id=(px, py, c), device_id_type=MESH).wait_recv()
        for cp in sends:
            cp.wait_send()
        mine.wait()

    return _pcall(
        body, name=name, in_specs=[_ANY], out_specs=_ANY,
        out_shape=jax.ShapeDtypeStruct((4, rows, width), flat.dtype),
        scratch_shapes=[pltpu.SemaphoreType.DMA((3,)), pltpu.SemaphoreType.DMA((3,)), pltpu.SemaphoreType.DMA],
    )(flat)


def _sibling_split(g, name):
    s, rows, width = g.shape
    half = rows // 2
    tr = _row_tile(half, width)
    nt = half // tr

    def body(c_ref, g_blk, got_ref, send_sem, recv_sem):
        k = pl.program_id(0)
        t = pl.program_id(1)
        x, y, c = _my_place()
        cp = pltpu.make_async_remote_copy(
            src_ref=g_blk.at[0], dst_ref=got_ref.at[k, pl.ds(pl.multiple_of(t * tr, 8), tr), :],
            send_sem=send_sem, recv_sem=recv_sem, device_id=(x, y, 1 - c), device_id_type=MESH)
        cp.start()
        cp.wait_send()

        @pl.when(jnp.logical_and(k == s - 1, t == nt - 1))
        def _():
            pltpu.make_async_remote_copy(
                src_ref=got_ref, dst_ref=got_ref, send_sem=send_sem, recv_sem=recv_sem,
                device_id=(x, y, 1 - c), device_id_type=MESH).wait_recv()

    return _pcall(
        body, name=name, out_shape=jax.ShapeDtypeStruct((s, half, width), g.dtype),
        grid_spec=pltpu.PrefetchScalarGridSpec(
            num_scalar_prefetch=1, grid=(s, nt),
            in_specs=[pl.BlockSpec((1, tr, width), lambda k, t, c_ref: (k, (1 - c_ref[0]) * nt + t, 0))],
            out_specs=_ANY,
            scratch_shapes=[pltpu.SemaphoreType.DMA, pltpu.SemaphoreType.DMA]),
        compiler_params=_params(2),
    )(_core_index(), g)


def _chip_scatter(p, name):
    s, rows, width = p.shape

    def body(p_ref, out_ref, send_sems, recv_sems, local_sem):
        x, y, c = _my_place()
        me = 2 * x + y
        mine = pltpu.make_async_copy(p_ref.at[me], out_ref.at[me], local_sem)
        mine.start()
        sends = []
        for k, (px, py) in enumerate(_other_chips(x, y)):
            cp = pltpu.make_async_remote_copy(
                src_ref=p_ref.at[2 * px + py], dst_ref=out_ref.at[me], send_sem=send_sems.at[k],
                recv_sem=recv_sems.at[k], device_id=(px, py, c), device_id_type=MESH)
            cp.start()
            sends.append(cp)
        for k, (px, py) in enumerate(_other_chips(x, y)):
            pltpu.make_async_remote_copy(
                src_ref=p_ref.at[me], dst_ref=out_ref.at[2 * px + py], send_sem=send_sems.at[k],
                recv_sem=recv_sems.at[k], device_id=(px, py, c), device_id_type=MESH).wait_recv()
        for cp in sends:
            cp.wait_send()
        mine.wait()

    return _pcall(
        body, name=name, in_specs=[_ANY], out_specs=_ANY,
        out_shape=jax.ShapeDtypeStruct(p.shape, p.dtype),
        scratch_shapes=[pltpu.SemaphoreType.DMA((3,)), pltpu.SemaphoreType.DMA((3,)), pltpu.SemaphoreType.DMA],
    )(p)


def _sibling_join(qh, name):
    half, width = qh.shape
    tr = _row_tile(half, width)
    nt = half // tr

    def body(q_blk, out_ref, send_sem, recv_sem, local_sem):
        t = pl.program_id(0)
        x, y, c = _my_place()
        dst = out_ref.at[pl.ds(pl.multiple_of(c * half + t * tr, 8), tr), :]
        cp = pltpu.make_async_remote_copy(
            src_ref=q_blk, dst_ref=dst, send_sem=send_sem, recv_sem=recv_sem,
            device_id=(x, y, 1 - c), device_id_type=MESH)
        cp.start()
        mine = pltpu.make_async_copy(q_blk, dst, local_sem)
        mine.start()
        cp.wait_send()
        mine.wait()

        @pl.when(t == nt - 1)
        def _():
            theirs = out_ref.at[pl.ds(pl.multiple_of((1 - c) * half, 8), half), :]
            pltpu.make_async_remote_copy(
                src_ref=theirs, dst_ref=theirs, send_sem=send_sem, recv_sem=recv_sem,
                device_id=(x, y, 1 - c), device_id_type=MESH).wait_recv()

    return _pcall(
        body, name=name, grid=(nt,),
        in_specs=[pl.BlockSpec((tr, width), lambda t: (t, 0))], out_specs=_ANY,
        out_shape=jax.ShapeDtypeStruct((2 * half, width), qh.dtype),
        scratch_shapes=[pltpu.SemaphoreType.DMA, pltpu.SemaphoreType.DMA, pltpu.SemaphoreType.DMA],
        compiler_params=_params(1),
    )(qh)


def _all_sum_small(part, name):
    rows, width = part.shape

    def body(p_ref, out_ref, land, send_sems, recv_sems):
        x, y, c = _my_place()
        me = 4 * x + 2 * y + c
        land[me] = p_ref[...]
        sends = []
        for k in range(1, 8):
            peer = (x ^ (k >> 2), y ^ ((k >> 1) & 1), c ^ (k & 1))
            cp = pltpu.make_async_remote_copy(
                src_ref=p_ref, dst_ref=land.at[me], send_sem=send_sems.at[k - 1], recv_sem=recv_sems.at[k - 1],
                device_id=peer, device_id_type=MESH)
            cp.start()
            sends.append(cp)
        for k in range(1, 8):
            px, py, pc = x ^ (k >> 2), y ^ ((k >> 1) & 1), c ^ (k & 1)
            pltpu.make_async_remote_copy(
                src_ref=p_ref, dst_ref=land.at[4 * px + 2 * py + pc], send_sem=send_sems.at[k - 1],
                recv_sem=recv_sems.at[k - 1], device_id=(px, py, pc), device_id_type=MESH).wait_recv()
        for cp in sends:
            cp.wait_send()
        tot = land[0]
        for k in range(1, 8):
            tot = tot + land[k]
        out_ref[...] = tot

    vmem = pl.BlockSpec(memory_space=pltpu.VMEM)
    return _pcall(
        body, name=name, in_specs=[vmem], out_specs=vmem,
        out_shape=jax.ShapeDtypeStruct((rows, width), F32),
        scratch_shapes=[pltpu.VMEM((8, rows, width), F32), pltpu.SemaphoreType.DMA((7,)),
                        pltpu.SemaphoreType.DMA((7,))],
    )(part)


def _big_layout(shards):
    return [(a.shape[0], a.shape[1], ax) for a, ax in shards]


FLAT_ROW_MULTIPLE = 2048


def _pack_shards(arrs, row_multiple=FLAT_ROW_MULTIPLE):
    flat = jnp.concatenate([a.reshape(-1) for a in arrs])
    return jnp.pad(flat, (0, -flat.shape[0] % (row_multiple * LANE))).reshape(-1, LANE)


def _unpack_shards(flat, layout):
    flat = flat.reshape(-1)
    out, off = [], 0
    for r, c, _ in layout:
        out.append(flat[off:off + r * c].reshape(r, c))
        off += r * c
    return out


def _unpack_full(gathered, layout):
    g = gathered.reshape(4, -1)
    out, off = [], 0
    for r, c, ax in layout:
        seg = g[:, off:off + r * c].reshape(4, r, c)
        out.append(seg.transpose(1, 0, 2).reshape(r, 4 * c) if ax == 1 else seg.reshape(4 * r, c))
        off += r * c
    return out


def _pack_full(fulls, layout):
    parts = []
    for a, (r, c, ax) in zip(fulls, layout):
        if ax == 1:
            parts.append(a.reshape(r, 4, c).transpose(1, 0, 2).reshape(4, r * c))
        else:
            parts.append(a.reshape(4, r * c))
    flat = jnp.concatenate(parts, axis=1)
    return jnp.pad(flat, ((0, 0), (0, -flat.shape[1] % (FLAT_ROW_MULTIPLE * LANE)))).reshape(4, -1, LANE)


def _pad_lanes(a, width=LANE):
    return jnp.pad(a, [(0, 0)] * (a.ndim - 1) + [(0, width - a.shape[-1])])


def _pack_small(arrs):
    rows = [_pad_lanes(a.reshape(1, -1), -(-a.size // LANE) * LANE).reshape(-1, LANE) for a in arrs]
    flat = jnp.concatenate(rows, axis=0)
    return jnp.pad(flat, ((0, -flat.shape[0] % 8), (0, 0)))


def _unpack_small(flat, shapes):
    out, off = [], 0
    for shp in shapes:
        n = math.prod(shp)
        nr = -(-n // LANE)
        out.append(flat[off:off + nr].reshape(-1)[:n].reshape(shp))
        off += nr
    return out


def kernel(x, meta_tokens, pre_norm, post_norm, gdn_w_in, gdn_conv_w, gdn_a_log, gdn_dt_bias, gdn_out_norm, gdn_w_out, kv_norm, kv_w_down, kv_latent_norm, kv_w_up, mla_w_in, mla_q_latent_norm, mla_w_q_up, mla_w_out, loss_target, m_meta_tokens, m_pre_norm, m_post_norm, m_gdn_w_in, m_gdn_conv_w, m_gdn_a_log, m_gdn_dt_bias, m_gdn_out_norm, m_gdn_w_out, m_kv_norm, m_kv_w_down, m_kv_latent_norm, m_kv_w_up, m_mla_w_in, m_mla_q_latent_norm, m_mla_w_q_up, m_mla_w_out, v_meta_tokens, v_pre_norm, v_post_norm, v_gdn_w_in, v_gdn_conv_w, v_gdn_a_log, v_gdn_dt_bias, v_gdn_out_norm, v_gdn_w_out, v_kv_norm, v_kv_w_down, v_kv_latent_norm, v_kv_w_up, v_mla_w_in, v_mla_q_latent_norm, v_mla_w_q_up, v_mla_w_out):
    seq = x.shape[1]
    d = D_MODEL
    lp = -(-(ROW0 + seq) // ROW_ALIGN) * ROW_ALIGN
    tail = lp - ROW0 - seq

    big_names = ["meta_tokens", "gdn_conv_w", "gdn_w_out", "kv_w_down", "kv_w_up", "mla_w_in", "mla_w_q_up",
                 "mla_w_out"]
    big_axis = [1, 1, 0, 0, 1, 1, 1, 0]
    big_w = [meta_tokens, gdn_conv_w[0], gdn_w_out[0], kv_w_down, kv_w_up, mla_w_in[0], mla_w_q_up[0], mla_w_out[0]]
    big_m = [m_meta_tokens, m_gdn_conv_w[0], m_gdn_w_out[0], m_kv_w_down, m_kv_w_up, m_mla_w_in[0], m_mla_w_q_up[0],
             m_mla_w_out[0]]
    big_v = [v_meta_tokens, v_gdn_conv_w[0], v_gdn_w_out[0], v_kv_w_down, v_kv_w_up, v_mla_w_in[0], v_mla_w_q_up[0],
             v_mla_w_out[0]]
    layout = _big_layout(list(zip(big_w, big_axis)))
    w_flat = _pack_shards(big_w)
    meta_f, conv_w = _unpack_full(
        _gather_shards(_pack_shards(big_w[:2], row_multiple=16), "gather_meta_conv"), layout[:2])
    mm_shards = [w.astype(BF16) for w in big_w[2:6]] + [(big_w[6] * Q_PRESCALE).astype(BF16), big_w[7].astype(BF16)]
    mm_flat = _pack_shards(mm_shards)
    w_in0_shards = _gather_shards(gdn_w_in[0].astype(BF16), "gather_gdn_w_in")
    w_in0 = jnp.concatenate([w_in0_shards[s] for s in range(4)], axis=1)
    win_cols = gdn_w_in.shape[2]

    nv = GDN_V_HEADS
    w_qkv = w_in0[:, :GDN_CONV_W]
    w_z0 = w_in0[:, GDN_CONV_W:GDN_CONV_W + GDN_V_W]
    w_b = _pad_lanes(w_in0[:, GDN_CONV_W + GDN_V_W:GDN_CONV_W + GDN_V_W + nv])
    w_a = _pad_lanes(w_in0[:, GDN_CONV_W + GDN_V_W + nv:])

    pre0, pre1 = pre_norm[0:1], pre_norm[1:2]
    post0, post1 = post_norm[0:1], post_norm[1:2]
    a_log = _pad_lanes(gdn_a_log)
    dt_bias = _pad_lanes(gdn_dt_bias)
    kvn = kv_norm.reshape(1, d)
    kvl = kv_latent_norm.reshape(1, MLA_KV_RANK)
    qln = mla_q_latent_norm

    h0 = jnp.concatenate([jnp.zeros((FRONT, d), F32), meta_f, x[0], jnp.zeros((tail, d), F32)], axis=0)
    tgt = jnp.pad(loss_target[0], ((ROW0, tail), (0, 0)))
    pos = jnp.maximum(jnp.arange(lp, dtype=jnp.int32) - FRONT, 0).astype(F32)
    inv = ROPE_THETA ** (-jnp.arange(0, MLA_ROPE, 2, dtype=F32) / MLA_ROPE)
    ang = pos[:, None] * inv[None, :]
    zeros64 = jnp.zeros((lp, LANE - MLA_ROPE), F32)
    cos_t = jnp.concatenate([jnp.cos(ang), jnp.cos(ang), zeros64], axis=1)
    sin_t = jnp.concatenate([-jnp.sin(ang), jnp.sin(ang), zeros64], axis=1)

    def valid_rows(ridx):
        return jnp.logical_and(ridx >= FRONT, ridx < ROW0 + seq)

    def f_pre0(ridx, g, h, gain):
        return _rms(h, gain), h

    (hn0,) = _rowwise("pre0", lambda *a: f_pre0(*a)[:1], [_In(h0), _In(pre0, "const")],
                      [_Out("row", (lp, d), BF16)])
    qkv_raw = _mm(hn0, w_qkv, "nn", "gdn_in_qkv")
    z0 = _mm(hn0, w_z0, "nn", "gdn_in_z")
    b_raw = _mm(hn0, w_b, "nn", "gdn_in_b")
    a_raw = _mm(hn0, w_a, "nn", "gdn_in_a")

    def f_ba(ridx, g, b, a, alog, dtb):
        tr = b.shape[0]
        ok = valid_rows(ridx).astype(F32)
        beta = jax.nn.sigmoid(b) * ok
        gate = -jnp.exp(alog) * _softplus(a + dtb) * ok
        ii = lax.broadcasted_iota(jnp.int32, (tr, tr), 0)
        jj = lax.broadcasted_iota(jnp.int32, (tr, tr), 1)
        shift = GDN_CHUNK.bit_length() - 1
        tri = jnp.logical_and((ii >> shift) == (jj >> shift), ii >= jj).astype(F32)
        return beta, _hdot(tri, gate)

    ba_ins = [_In(b_raw), _In(a_raw), _In(a_log, "const"), _In(dt_bias, "const")]
    beta, gc = _rowwise("gdn_gates", f_ba, ba_ins, [_Out("row", (lp, LANE)), _Out("row", (lp, LANE))])
    qkv = _conv_fwd(qkv_raw, conv_w, "gdn_conv")
    o0, ckpt, mm_all = _gdn_fwd(qkv, beta, gc, mm_flat, "gdn_scan")
    (w_out0, kv_down, kv_up, w_in1, w_qup, w_out1) = _unpack_full(mm_all, layout[2:])
    w_ckv = kv_down[:, :MLA_KV_RANK]
    w_kr = _pad_lanes(kv_down[:, MLA_KV_RANK:])
    kvu = kv_up.reshape(MLA_KV_RANK, MLA_HEADS, 2 * LANE)
    w_kn = kvu[:, :, :LANE].reshape(MLA_KV_RANK, MLA_HEADS * LANE)
    w_v = kvu[:, :, LANE:].reshape(MLA_KV_RANK, MLA_HEADS * LANE)
    w_cq = w_in1[:, :MLA_Q_RANK]
    w_z1 = w_in1[:, MLA_Q_RANK:]
    qu = w_qup.reshape(MLA_Q_RANK, MLA_HEADS, MLA_QK)
    w_qn = qu[:, :, :MLA_NOPE].reshape(MLA_Q_RANK, MLA_HEADS * LANE)
    w_qr = _pad_lanes(qu[:, :, MLA_NOPE:]).reshape(MLA_Q_RANK, MLA_HEADS * LANE)

    def per_head(fn, *arrs):
        n = arrs[0].shape[1] // LANE
        return jnp.concatenate([fn(*[a[:, i * LANE:(i + 1) * LANE] for a in arrs]) for i in range(n)], axis=1)

    def f_gate0(ridx, g, o, z, gain):
        return (per_head(lambda oh, zh: _rms(oh, gain) * _silu(zh), o, z),)

    gate0_ins = [_In(o0), _In(z0), _In(gdn_out_norm, "const")]
    (gated0,) = _rowwise("gdn_gate", f_gate0, gate0_ins, [_Out("row", (lp, GDN_V_W), BF16)])
    y0 = _mm(gated0, w_out0, "nn", "gdn_out")

    def f_mid(ridx, g, h, y, g_post, g_pre, g_kv):
        h1 = h + _rms(y, g_post)
        return h1, _rms(h1, g_pre), _rms(h1, g_kv)

    mid_ins = [_In(h0), _In(y0), _In(post0, "const"), _In(pre1, "const"), _In(kvn, "const")]
    h1, hn1, hkv = _rowwise("mid", f_mid, mid_ins,
                            [_Out("row", (lp, d)), _Out("row", (lp, d), BF16), _Out("row", (lp, d), BF16)])

    ckv_raw = _mm(hkv, w_ckv, "nn", "kv_down_c")
    kr_raw = _mm(hkv, w_kr, "nn", "kv_down_r")

    def f_ckv(ridx, g, c, r, cs, sn, gain):
        return _rms(c, gain), _rope(r, cs, sn)

    ckv_ins = [_In(ckv_raw), _In(kr_raw), _In(cos_t), _In(sin_t), _In(kvl, "const")]
    ckv, kr = _rowwise("kv_latent", f_ckv, ckv_ins, [_Out("row", (lp, LANE)), _Out("row", (lp, LANE), BF16)],
                       tr=TR_FULL)
    kn = _mm(ckv, w_kn, "nn", "kv_up_k", BF16)
    vv = _mm(ckv, w_v, "nn", "kv_up_v", BF16)
    cq_raw = _mm(hn1, w_cq, "nn", "mla_in_q")
    z1 = _mm(hn1, w_z1, "nn", "mla_in_z")

    def f_cq(ridx, g, c, gain):
        return (_rms(c, gain),)

    cq_ins = [_In(cq_raw), _In(qln, "const")]
    (cq,) = _rowwise("q_latent", f_cq, cq_ins, [_Out("row", (lp, MLA_Q_RANK))], tr=TR_FULL)
    qn = _mm(cq, w_qn, "nn", "q_up_n", BF16)
    qr_raw = _mm(cq, w_qr, "nn", "q_up_r")

    def f_qrope(ridx, g, r, cs, sn):
        return (per_head(lambda rh: _rope(rh, cs, sn), r),)

    qr_ins = [_In(qr_raw), _In(cos_t), _In(sin_t)]
    (qr,) = _rowwise("q_rope", f_qrope, qr_ins, [_Out("row", (lp, MLA_HEADS * LANE), BF16)])
    o1, lse = _flash_fwd(qn, qr, kn, kr, vv, "attention")

    def f_gate1(ridx, g, o, z):
        return (o * _silu(z),)

    gate1_ins = [_In(o1), _In(z1)]
    (og,) = _rowwise("mla_gate", f_gate1, gate1_ins, [_Out("row", (lp, MLA_HEADS * LANE), BF16)])
    y1 = _mm(og, w_out1, "nn", "mla_out")

    def f_final(ridx, g, h, y, t, gain):
        ok = jnp.logical_and(ridx >= ROW0, ridx < ROW0 + seq).astype(F32)

        def rows_loss(h_, y_, gain_):
            err = (h_ + _rms(y_, gain_) - t) * ok
            return 0.5 * jnp.sum(jnp.sum(err * err, axis=1, keepdims=True), axis=0, keepdims=True) / d

        val, vjp = jax.vjp(rows_loss, h, y, gain)
        dh, dy, dgain = vjp(jnp.ones((1, 1), F32))
        return dh, dy, dgain, jnp.broadcast_to(val, (1, LANE))

    dh2, dy1, dpost1, loss_part = _rowwise(
        "loss_head", f_final, [_In(h1), _In(y1), _In(tgt), _In(post1, "const")],
        [_Out("row", (lp, d)), _Out("row", (lp, d)), _Out("acc", (1, d)), _Out("acc", (1, LANE))])

    dog = _mm(dy1, w_out1, "nt", "mla_out_dx")
    dw_out1 = _mm(og, dy1, "tn", "mla_out_dw")
    do1, dz1 = _rowwise_vjp("mla_gate_bwd", f_gate1, gate1_ins, [[dog]], [0, 1])
    dqn, dqr, dkn, dkr, dvv = _flash_bwd(qn, qr, kn, kr, vv, o1, do1, lse, "attention_bwd")
    (dqr_raw,) = _rowwise_vjp("q_rope_bwd", f_qrope, qr_ins, [[dqr]], [0])
    dcq_a = _mm(dqn, w_qn, "nt", "q_up_n_dx")
    dcq_b = _mm(dqr_raw, w_qr, "nt", "q_up_r_dx")
    dw_qn = _mm(cq, dqn, "tn", "q_up_n_dw") * Q_PRESCALE
    dw_qr = _mm(cq, dqr_raw, "tn", "q_up_r_dw") * Q_PRESCALE
    dcq_raw, dqln = _rowwise_vjp("q_latent_bwd", f_cq, cq_ins, [[dcq_a, dcq_b]], [0, 1], tr=TR_FULL)
    dhn1_a = _mm(dcq_raw, w_cq, "nt", "mla_in_q_dx")
    dhn1_b = _mm(dz1, w_z1, "nt", "mla_in_z_dx")
    dw_cq = _mm(hn1, dcq_raw, "tn", "mla_in_q_dw")
    dw_z1 = _mm(hn1, dz1, "tn", "mla_in_z_dw")
    dckv_a = _mm(dkn, w_kn, "nt", "kv_up_k_dx")
    dckv_b = _mm(dvv, w_v, "nt", "kv_up_v_dx")
    dw_kn = _mm(ckv, dkn, "tn", "kv_up_k_dw")
    dw_v = _mm(ckv, dvv, "tn", "kv_up_v_dw")
    dckv_raw, dkr_raw, dkvl = _rowwise_vjp("kv_latent_bwd", f_ckv, ckv_ins, [[dckv_a, dckv_b], [dkr]], [0, 1, 4],
                                           tr=TR_FULL)
    dhkv_a = _mm(dckv_raw, w_ckv, "nt", "kv_down_c_dx")
    dhkv_b = _mm(dkr_raw, w_kr, "nt", "kv_down_r_dx")
    dw_ckv = _mm(hkv, dckv_raw, "tn", "kv_down_c_dw")
    dw_kr = _mm(hkv, dkr_raw, "tn", "kv_down_r_dw")
    dh0_res, dy0, dpost0, dpre1, dkvn = _rowwise_vjp(
        "mid_bwd", f_mid, mid_ins, [[dh2], [dhn1_a, dhn1_b], [dhkv_a, dhkv_b]], [0, 1, 2, 3, 4])

    dgated0 = _mm(dy0, w_out0, "nt", "gdn_out_dx")
    dw_out0 = _mm(gated0, dy0, "tn", "gdn_out_dw")
    do0, dz0, doutn = _rowwise_vjp("gdn_gate_bwd", f_gate0, gate0_ins, [[dgated0]], [0, 1, 2], tr=TR_QUARTER)
    dq0, dk0, dv0, dbeta, dgc = _gdn_bwd(qkv, beta, gc, ckpt, do0, "gdn_scan_bwd")
    db_raw, da_raw, dalog, ddtb = _rowwise_vjp("gdn_gates_bwd", f_ba, ba_ins, [[dbeta], [dgc]], [0, 1, 2, 3])
    dqkv_raw, dconv = _conv_bwd(qkv_raw, conv_w, dq0, dk0, dv0, "gdn_conv_bwd")
    dhn0_a = _mm(dqkv_raw, w_qkv, "nt", "gdn_in_qkv_dx")
    dhn0_b = _mm(dz0, w_z0, "nt", "gdn_in_z_dx")
    dhn0_c = _mm(db_raw, w_b, "nt", "gdn_in_b_dx")
    dhn0_d = _mm(da_raw, w_a, "nt", "gdn_in_a_dx")
    dw_qkv = _mm(hn0, dqkv_raw, "tn", "gdn_in_qkv_dw")
    dw_z0 = _mm(hn0, dz0, "tn", "gdn_in_z_dw")
    dw_b = _mm(hn0, db_raw, "tn", "gdn_in_b_dw")
    dw_a = _mm(hn0, da_raw, "tn", "gdn_in_a_dw")
    dh0, dpre0 = _rowwise_vjp("pre0_bwd", f_pre0, [_In(h0), _In(pre0, "const")],
                              [[dhn0_a, dhn0_b, dhn0_c, dhn0_d], [dh0_res]], [0, 1])

    grad_x = dh0[ROW0:ROW0 + seq][None]
    g_meta = dh0[FRONT:ROW0]
    g_w_in0 = jnp.concatenate([dw_qkv, dw_z0, dw_b[:, :nv], dw_a[:, :nv]], axis=1)
    g_kv_down = jnp.concatenate([dw_ckv, dw_kr[:, :MLA_ROPE]], axis=1)
    g_kv_up = jnp.concatenate([dw_kn.reshape(MLA_KV_RANK, MLA_HEADS, LANE), dw_v.reshape(MLA_KV_RANK, MLA_HEADS, LANE)],
                              axis=2).reshape(MLA_KV_RANK, MLA_HEADS * 2 * LANE)
    g_w_in1 = jnp.concatenate([dw_cq, dw_z1], axis=1)
    g_qup = jnp.concatenate([dw_qn.reshape(MLA_Q_RANK, MLA_HEADS, LANE),
                             dw_qr.reshape(MLA_Q_RANK, MLA_HEADS, LANE)[:, :, :MLA_ROPE]],
                            axis=2).reshape(MLA_Q_RANK, MLA_HEADS * MLA_QK)
    big_g = [g_meta, dconv, dw_out0, g_kv_down, g_kv_up, g_w_in1, g_qup, dw_out1]

    def reduce_to_shard(g_by_chip, tag):
        got = _sibling_split(g_by_chip, "grads_sibling_split" + tag)
        chip_part = _add_pair(g_by_chip, got, "grads_chip_sum" + tag)
        from_chips = _chip_scatter(chip_part, "grads_chip_scatter" + tag)
        half_sum = _sum_slots(from_chips, "grads_total" + tag)
        return _sibling_join(half_sum, "grads_sibling_join" + tag)

    g_flat = reduce_to_shard(_pack_full(big_g, layout), "")
    g_win = reduce_to_shard(jnp.stack([g_w_in0[:, s * win_cols:(s + 1) * win_cols] for s in range(4)]), "_gdn_w_in")

    small_shapes = [(2, d), (2, d), (1, nv), (1, nv), (1, GDN_DK), (d,), (MLA_KV_RANK,), (1, MLA_Q_RANK), (1, LANE)]
    small_part = _pack_small([jnp.concatenate([dpre0, dpre1], axis=0), jnp.concatenate([dpost0, dpost1], axis=0),
                              dalog[:, :nv], ddtb[:, :nv], doutn, dkvn, dkvl, dqln, loss_part])
    small_tot = _all_sum_small(small_part, "small_sum")
    small_g = _unpack_small(small_tot, small_shapes)
    loss = small_g[-1][0, 0]

    d_flat, m_flat, v_flat = _adamw(w_flat, g_flat, _pack_shards(big_m), _pack_shards(big_v), "adamw_sharded")
    win_step = _adamw(gdn_w_in[0], g_win, m_gdn_w_in[0], v_gdn_w_in[0], "adamw_gdn_w_in")
    small_w = [pre_norm, post_norm, gdn_a_log, gdn_dt_bias, gdn_out_norm, kv_norm, kv_latent_norm, mla_q_latent_norm]
    small_m = [m_pre_norm, m_post_norm, m_gdn_a_log, m_gdn_dt_bias, m_gdn_out_norm, m_kv_norm, m_kv_latent_norm,
               m_mla_q_latent_norm]
    small_v = [v_pre_norm, v_post_norm, v_gdn_a_log, v_gdn_dt_bias, v_gdn_out_norm, v_kv_norm, v_kv_latent_norm,
               v_mla_q_latent_norm]
    g_small_flat = _pack_small(small_g[:-1])
    ds_flat, ms_flat, vs_flat = _adamw(_pack_small(small_w), g_small_flat, _pack_small(small_m), _pack_small(small_v),
                                       "adamw_replicated")

    def assemble(big_flat, small_flat, win):
        bigs = dict(zip(big_names, [a.reshape(w.shape) for a, w in zip(
            _unpack_shards(big_flat, layout),
            [meta_tokens, gdn_conv_w, gdn_w_out, kv_w_down, kv_w_up, mla_w_in, mla_w_q_up, mla_w_out])]))
        smalls = dict(zip(["pre_norm", "post_norm", "gdn_a_log", "gdn_dt_bias", "gdn_out_norm", "kv_norm",
                           "kv_latent_norm", "mla_q_latent_norm"], _unpack_small(small_flat, small_shapes[:-1])))
        both = {**bigs, **smalls, "gdn_w_in": win[None]}
        order = ["meta_tokens", "pre_norm", "post_norm", "gdn_w_in", "gdn_conv_w", "gdn_a_log", "gdn_dt_bias",
                 "gdn_out_norm", "gdn_w_out", "kv_norm", "kv_w_down", "kv_latent_norm", "kv_w_up", "mla_w_in",
                 "mla_q_latent_norm", "mla_w_q_up", "mla_w_out"]
        return [both[n] for n in order]

    grads = assemble(g_flat, g_small_flat, g_win)
    deltas = assemble(d_flat, ds_flat, win_step[0])
    new_m = assemble(m_flat, ms_flat, win_step[1])
    new_v = assemble(v_flat, vs_flat, win_step[2])
    return (loss, grad_x, *grads, *deltas, *new_m, *new_v)
```

```python
import functools
import math

import jax
import jax.numpy as jnp
from jax import lax
from jax.experimental import pallas as pl
from jax.experimental.pallas import tpu as pltpu

F32 = jnp.float32
BF16 = jnp.bfloat16
MESH = pl.DeviceIdType.MESH

D_MODEL = 1024
N_META = 16
FRONT = 48
ROW0 = FRONT + N_META
ROW_ALIGN = 768
TR_FULL, TR_HALF, TR_QUARTER = ROW_ALIGN, ROW_ALIGN // 2, ROW_ALIGN // 4
NORM_EPS = 1e-6
LANE = 128

GDN_QK_HEADS = 8
GDN_V_HEADS = 16
GDN_DK = 128
GDN_CHUNK = 64
GDN_QK_W = 1024
GDN_V_W = 2048
GDN_CONV_W = 4096

MLA_HEADS = 16
MLA_NOPE = 128
MLA_ROPE = 64
MLA_QK = 192
MLA_Q_RANK = 256
MLA_KV_RANK = 128
ROPE_THETA = 10000.0

ADAM_LR = 0.001
ADAM_B1 = 0.9
ADAM_B2 = 0.999
ADAM_EPS = 1e-08
ADAM_WD = 0.01
ADAM_STEP = 10

VMEM_LIMIT_V7X = 56 * 1024 * 1024
NEG = -1e30

_NN = ((1,), (0,))
_NT = ((1,), (1,))
_TN = ((0,), (0,))
_HI = lax.Precision.HIGHEST
_X3 = lax.Precision.HIGH


def _pcall(body, **kw):
    return pl.pallas_call(body, **kw)


def _params(n_axes):
    return pltpu.CompilerParams(dimension_semantics=("arbitrary",) * n_axes, vmem_limit_bytes=VMEM_LIMIT_V7X)


def _dot(a, b, dims, prec=None):
    return lax.dot_general(a, b, (dims, ((), ())), precision=prec, preferred_element_type=F32)


def _bdot(a, b, dims):
    return _dot(a.astype(BF16), b.astype(BF16), dims)


def _hdot(a, b, dims=_NN):
    return _dot(a, b, dims, _HI)


def _fdot(a, b, dims):
    return _dot(a, b, dims)


SMALL_MATMUL_DIM = 256
SMALL_MATMUL_ROWS = 1408


def _tile(n):
    if n % ROW_ALIGN == 0:
        return ROW_ALIGN
    for t in (1024, 512, 256, 128):
        if n % t == 0:
            return t
    raise ValueError(n)


def _mm(a, b, mode, name, out_dtype=F32):
    if mode == "nn":
        (m, k), (k2, n) = a.shape, b.shape
    elif mode == "nt":
        (m, k), (n, k2) = a.shape, b.shape
    else:
        (k, m), (k2, n) = a.shape, b.shape
    assert k == k2, (a.shape, b.shape, mode)
    tm, tn, tk = _tile(m), _tile(n), _tile(k)
    if mode != "tn" and min(k, n) <= SMALL_MATMUL_DIM and m % SMALL_MATMUL_ROWS == 0:
        tm = SMALL_MATMUL_ROWS
    nk = k // tk
    dims = {"nn": _NN, "nt": _NT, "tn": _TN}[mode]

    def body(a_ref, b_ref, o_ref, acc):
        kk = pl.program_id(2)

        @pl.when(kk == 0)
        def _():
            acc[...] = jnp.zeros_like(acc)

        acc[...] += _bdot(a_ref[...], b_ref[...], dims)

        @pl.when(kk == nk - 1)
        def _():
            o_ref[...] = acc[...].astype(out_dtype)

    if mode == "tn":
        a_spec = pl.BlockSpec((tk, tm), lambda i, j, kk: (kk, i))
    else:
        a_spec = pl.BlockSpec((tm, tk), lambda i, j, kk: (i, kk))
    if mode == "nt":
        b_spec = pl.BlockSpec((tn, tk), lambda i, j, kk: (j, kk))
    else:
        b_spec = pl.BlockSpec((tk, tn), lambda i, j, kk: (kk, j))
    return _pcall(
        body, name=name, grid=(m // tm, n // tn, nk),
        in_specs=[a_spec, b_spec],
        out_specs=pl.BlockSpec((tm, tn), lambda i, j, kk: (i, j)),
        out_shape=jax.ShapeDtypeStruct((m, n), out_dtype),
        scratch_shapes=[pltpu.VMEM((tm, tn), F32)],
        compiler_params=_params(3),
    )(a, b)


class _In:
    def __init__(self, arr, kind="row", grouped=False, goff=0):
        self.arr, self.kind, self.grouped, self.goff = arr, kind, grouped, goff


class _Out:
    def __init__(self, kind, shape, dtype=F32, grouped=False):
        self.kind, self.shape, self.dtype, self.grouped = kind, shape, dtype, grouped


def _rowwise(name, fn, ins, outs, *, groups=1, tr=TR_HALF):
    lp = next(i.arr.shape[0] for i in ins if i.kind == "row")
    nr = lp // tr
    assert lp % tr == 0

    def in_spec(i):
        w = i.arr.shape[1]
        if i.kind == "row":
            if i.grouped:
                return pl.BlockSpec((tr, LANE), lambda g, r, o=i.goff: (r, g + o))
            return pl.BlockSpec((tr, w), lambda g, r: (r, 0))
        if i.grouped:
            return pl.BlockSpec((i.arr.shape[0], LANE), lambda g, r, o=i.goff: (0, g + o))
        return pl.BlockSpec(i.arr.shape, lambda g, r: (0, 0))

    def out_spec(o):
        if o.kind == "row":
            if o.grouped:
                return pl.BlockSpec((tr, LANE), lambda g, r: (r, g))
            assert groups == 1
            return pl.BlockSpec((tr, o.shape[1]), lambda g, r: (r, 0))
        if o.grouped:
            return pl.BlockSpec((o.shape[0], LANE), lambda g, r: (0, g))
        return pl.BlockSpec(o.shape, lambda g, r: (0, 0))

    n_in = len(ins)

    def body(*refs):
        g = pl.program_id(0)
        r = pl.program_id(1)
        ridx = r * tr + lax.broadcasted_iota(jnp.int32, (tr, 1), 0)
        res = fn(ridx, g, *[ref[...] for ref in refs[:n_in]])
        assert len(res) == len(outs), (name, len(res), len(outs))
        for o, ref, val in zip(outs, refs[n_in:], res):
            if o.kind == "row":
                ref[...] = val.astype(o.dtype)
            else:
                first = (r == 0) if o.grouped else jnp.logical_and(r == 0, g == 0)

                @pl.when(first)
                def _(ref=ref, val=val):
                    ref[...] = val.astype(F32)

                @pl.when(jnp.logical_not(first))
                def _(ref=ref, val=val):
                    ref[...] += val.astype(F32)

    res = _pcall(
        body, name=name, grid=(groups, nr),
        in_specs=[in_spec(i) for i in ins],
        out_specs=[out_spec(o) for o in outs],
        out_shape=[jax.ShapeDtypeStruct(o.shape, o.dtype) for o in outs],
        compiler_params=_params(2),
    )(*[i.arr for i in ins])
    return res


def _rowwise_vjp(name, fn, ins, cots, diff, *, groups=1, tr=TR_HALF):
    n_in = len(ins)
    grouped = groups > 1
    cot_ins = []
    counts = []
    for arrs in cots:
        counts.append(len(arrs))
        for a in arrs:
            cot_ins.append(_In(a, "row", grouped=grouped and a.shape[1] > LANE))
    lp = next(i.arr.shape[0] for i in ins if i.kind == "row")
    outs = []
    for d in diff:
        i = ins[d]
        if i.kind == "row":
            w = groups * LANE if i.grouped else i.arr.shape[1]
            outs.append(_Out("row", (lp, w), F32, grouped=i.grouped))
        else:
            outs.append(_Out("acc", i.arr.shape, F32, grouped=i.grouped))

    def bfn(ridx, g, *allvals):
        vals = list(allvals[:n_in])
        cvals = allvals[n_in:]

        def f(*dv):
            full = list(vals)
            for i, v in zip(diff, dv):
                full[i] = v
            return tuple(fn(ridx, g, *full))

        primal, vjp = jax.vjp(f, *[vals[i].astype(F32) for i in diff])
        cts = []
        pos = 0
        for k, cnt in enumerate(counts):
            if cnt == 0:
                cts.append(jnp.zeros_like(primal[k]))
            else:
                c = cvals[pos].astype(F32)
                for extra in cvals[pos + 1:pos + cnt]:
                    c = c + extra.astype(F32)
                w = primal[k].shape[1]
                if c.shape[1] != w:
                    c = functools.reduce(jnp.add, [c[:, i * w:(i + 1) * w] for i in range(c.shape[1] // w)])
                cts.append(c.astype(primal[k].dtype))
            pos += cnt
        return vjp(tuple(cts))

    return _rowwise(name, bfn, list(ins) + cot_ins, outs, groups=groups, tr=tr)


def _rms(x, g):
    return x * lax.rsqrt(jnp.mean(x * x, axis=-1, keepdims=True) + NORM_EPS) * g


def _silu(x):
    return x * jax.nn.sigmoid(x)


def _softplus(x):
    return jnp.maximum(x, 0.0) + jnp.log(1.0 + jnp.exp(-jnp.abs(x)))


def _swap_halves(x):
    lane = lax.broadcasted_iota(jnp.int32, x.shape, x.ndim - 1)
    return jnp.where(lane < 32, pltpu.roll(x, LANE - 32, x.ndim - 1), pltpu.roll(x, 32, x.ndim - 1))


@jax.custom_vjp
def _rope(x, c, s):
    return x * c + _swap_halves(x) * s


def _rope_fwd(x, c, s):
    return _rope(x, c, s), (c, s)


def _rope_bwd(res, dy):
    c, s = res
    return dy * c + _swap_halves(dy * s), jnp.zeros_like(c), jnp.zeros_like(s)


_rope.defvjp(_rope_fwd, _rope_bwd)


def _conv_post(c, g):
    s = _silu(c)
    n = s * lax.rsqrt(jnp.sum(s * s, axis=-1, keepdims=True) + NORM_EPS)
    return jnp.where(g < GDN_QK_HEADS, n * (GDN_DK ** -0.5), jnp.where(g < 2 * GDN_QK_HEADS, n, s))


def _conv_taps(xe, w):
    c = xe[8:] * w[3]
    for s in (1, 2, 3):
        c = c + pltpu.roll(xe, s, 0)[8:] * w[3 - s]
    return c


CONV_LANES = 512
CONV_HEADS = CONV_LANES // LANE


def _conv_post_block(c, g):
    return jnp.concatenate([_conv_post(c[:, i * LANE:(i + 1) * LANE], g * CONV_HEADS + i)
                            for i in range(CONV_HEADS)], axis=1)


def _conv_fwd(x, w, name, tr=TR_FULL):
    lp, width = x.shape
    cl = CONV_LANES
    nr = lp // tr

    def body(x_ref, prev_ref, w_ref, o_ref):
        g = pl.program_id(0)
        r = pl.program_id(1)
        prev = jnp.where(r > 0, prev_ref[...], 0.0)
        xe = jnp.concatenate([prev, x_ref[...]], axis=0)
        o_ref[...] = _conv_post_block(_conv_taps(xe, [w_ref[t:t + 1, :] for t in range(4)]), g)

    return _pcall(
        body, name=name, grid=(width // cl, nr),
        in_specs=[pl.BlockSpec((tr, cl), lambda g, r: (r, g)),
                  pl.BlockSpec((8, cl), lambda g, r: (jnp.maximum(r * (tr // 8) - 1, 0), g)),
                  pl.BlockSpec((4, cl), lambda g, r: (0, g))],
        out_specs=pl.BlockSpec((tr, cl), lambda g, r: (r, g)),
        out_shape=jax.ShapeDtypeStruct((lp, width), F32),
        compiler_params=_params(2),
    )(x, x, w)


def _conv_bwd(x, w, dq, dk, dv, name, tr=TR_FULL):
    lp, width = x.shape
    cl = CONV_LANES
    nr = lp // tr
    last8 = lp // 8 - 1
    nq = GDN_QK_W // cl

    def body(x_ref, prev_ref, next_ref, w_ref, q_ref, k_ref, v_ref, q_n, k_n, v_n, dx_ref, dw_ref):
        g = pl.program_id(0)
        r = pl.program_id(1)
        w = [w_ref[t:t + 1, :] for t in range(4)]
        not_last = r < nr - 1

        def pick(a, b, c):
            return jnp.where(g < nq, a[...], jnp.where(g < 2 * nq, b[...], c[...]))

        dy = pick(q_ref, k_ref, v_ref)
        dyn = jnp.where(not_last, pick(q_n, k_n, v_n), 0.0)
        prev = jnp.where(r > 0, prev_ref[...], 0.0)
        nxt = jnp.where(not_last, next_ref[...], 0.0)
        xe = jnp.concatenate([prev, x_ref[...], nxt], axis=0)
        ce = _conv_taps(xe, w)
        _, vjp = jax.vjp(lambda c: _conv_post_block(c, g), ce)
        (dce,) = vjp(jnp.concatenate([dy, dyn], axis=0))
        n = tr + 8
        dx = dce * w[3]
        for s in (1, 2, 3):
            dx = dx + pltpu.roll(dce, n - s, 0) * w[3 - s]
        dx_ref[...] = dx[:tr]
        dc = dce[:tr]
        row4 = lax.broadcasted_iota(jnp.int32, (4, cl), 0)
        dw = jnp.zeros((4, cl), F32)
        for s in (0, 1, 2, 3):
            xs = xe[8:8 + tr] if s == 0 else pltpu.roll(xe, s, 0)[8:8 + tr]
            dw = dw + jnp.where(row4 == 3 - s, jnp.sum(dc * xs, axis=0, keepdims=True), 0.0)

        @pl.when(r == 0)
        def _():
            dw_ref[...] = dw

        @pl.when(r > 0)
        def _():
            dw_ref[...] += dw

    def col_q(g):
        return jnp.minimum(g, nq - 1)

    def col_k(g):
        return jnp.clip(g - nq, 0, nq - 1)

    def col_v(g):
        return jnp.maximum(g - 2 * nq, 0)

    def blk(colf):
        return pl.BlockSpec((tr, cl), lambda g, r: (r, colf(g)))

    def nblk(colf):
        return pl.BlockSpec((8, cl), lambda g, r: (jnp.minimum((r + 1) * (tr // 8), last8), colf(g)))

    return _pcall(
        body, name=name, grid=(width // cl, nr),
        in_specs=[pl.BlockSpec((tr, cl), lambda g, r: (r, g)),
                  pl.BlockSpec((8, cl), lambda g, r: (jnp.maximum(r * (tr // 8) - 1, 0), g)),
                  pl.BlockSpec((8, cl), lambda g, r: (jnp.minimum((r + 1) * (tr // 8), last8), g)),
                  pl.BlockSpec((4, cl), lambda g, r: (0, g)),
                  blk(col_q), blk(col_k), blk(col_v), nblk(col_q), nblk(col_k), nblk(col_v)],
        out_specs=[pl.BlockSpec((tr, cl), lambda g, r: (r, g)),
                   pl.BlockSpec((4, cl), lambda g, r: (0, g))],
        out_shape=[jax.ShapeDtypeStruct((lp, width), F32), jax.ShapeDtypeStruct((4, width), F32)],
        compiler_params=_params(2),
    )(x, x, x, w, dq, dk, dv, dq, dk, dv)


def _bmm(a, b, dims, prec=None):
    (ca,), (cb,) = dims
    return lax.dot_general(a, b, (((ca + 1,), (cb + 1,)), ((0,), (0,))), precision=prec,
                           preferred_element_type=F32)


def _inv_impl(m):
    c = m.shape[-1]
    ii = lax.broadcasted_iota(jnp.int32, (c, c), 0)
    jj = lax.broadcasted_iota(jnp.int32, (c, c), 1)
    eye = (ii == jj).astype(F32)

    def same_block(shift):
        return (ii >> shift) == (jj >> shift)

    n1 = jnp.where(same_block(3), -m, 0.0)
    n2 = _bmm(n1, n1, _NN, _X3)
    n4 = _bmm(n2, n2, _NN, _X3)
    d = _bmm(_bmm(eye + n1, eye + n2, _NN, _X3), eye + n4, _NN, _X3)
    shift = 3
    while (1 << shift) < c:
        low = jnp.where(jnp.logical_and(same_block(shift + 1), jnp.logical_not(same_block(shift))), m, 0.0)
        d = d - _bmm(d, _bmm(low, d, _NN, _X3), _NN, _X3)
        shift += 1
    return d


@jax.custom_vjp
def _inv_unit_lower(m):
    return _inv_impl(m)


def _inv_f(m):
    t = _inv_impl(m)
    return t, t


def _inv_b(t, dt):
    c = t.shape[-1]
    ii = lax.broadcasted_iota(jnp.int32, (c, c), 0)
    jj = lax.broadcasted_iota(jnp.int32, (c, c), 1)
    gm = _bmm(t, _bmm(dt, t, _NT, _X3), _TN, _X3)
    return (jnp.where(ii > jj, -gm, 0.0),)


_inv_unit_lower.defvjp(_inv_f, _inv_b)


GDN_HEADS_PER_STEP = 16


def _gdn_group(q, k, v, beta_blk, gc_blk, states, h0):
    hp = GDN_HEADS_PER_STEP
    c = q.shape[0]
    lane = lax.broadcasted_iota(jnp.int32, (1, LANE), 1)
    row8 = lax.broadcasted_iota(jnp.int32, (max(8, hp), LANE), 0)
    lane8 = lax.broadcasted_iota(jnp.int32, (max(8, hp), LANE), 1)
    gcr_all = _hdot((lane8 == h0 + row8).astype(F32), gc_blk, _NT)
    betas, gccs = [], []
    for i in range(hp):
        onehot = (lane == h0 + i).astype(F32)
        betas.append(jnp.sum(beta_blk * onehot, axis=1, keepdims=True))
        gccs.append(jnp.sum(gc_blk * onehot, axis=1, keepdims=True))
    def stack(xs):
        return jnp.concatenate([x[None] for x in xs], axis=0)

    beta = stack(betas)
    gcc = stack(gccs)
    gcr = stack([gcr_all[i:i + 1] for i in range(hp)])
    qh = stack([q[:, (i // 2) * LANE:(i // 2 + 1) * LANE] for i in range(hp)])
    kh = stack([k[:, (i // 2) * LANE:(i // 2 + 1) * LANE] for i in range(hp)])
    vh = stack([v[:, i * LANE:(i + 1) * LANE] for i in range(hp)])
    state = stack(states)
    ii = lax.broadcasted_iota(jnp.int32, (c, c), 0)
    jj = lax.broadcasted_iota(jnp.int32, (c, c), 1)
    incl = ii >= jj
    dec = jnp.where(incl, jnp.exp(jnp.where(incl, gcc - gcr, 0.0)), 0.0)
    eg = jnp.exp(gcc)
    m = _bmm(kh, kh, _NT) * beta * jnp.where(ii > jj, dec, 0.0)
    t = _inv_unit_lower(m)
    u = _bmm(t, vh * beta, _NN, _X3)
    w = _bmm(t, kh * (beta * eg), _NN, _X3)
    attn = _bmm(qh, kh, _NT) * dec
    rows = lax.broadcasted_iota(jnp.int32, (c, 1), 0)
    gl = jnp.sum(jnp.where(rows == c - 1, gcc, 0.0), axis=1, keepdims=True)
    v_new = u - _bmm(w, state, _NN)
    o = _bmm(qh * eg, state, _NN) + _bmm(attn, v_new, _NN)
    new_state = state * jnp.exp(gl) + _bmm(kh * jnp.exp(gl - gcc), v_new, _TN)
    return jnp.concatenate([o[i] for i in range(hp)], axis=1), tuple(new_state[i] for i in range(hp))


def _gdn_specs(nc, rev):
    def cidx(n):
        return (nc - 1 - n) if rev else n
    hp = GDN_HEADS_PER_STEP
    nqk = GDN_QK_HEADS
    c = GDN_CHUNK
    nq = 2 * nqk // hp
    q_spec = pl.BlockSpec((c, hp // 2 * LANE), lambda n, g: (cidx(n), g))
    k_spec = pl.BlockSpec((c, hp // 2 * LANE), lambda n, g: (cidx(n), nq + g))
    v_spec = pl.BlockSpec((c, hp * LANE), lambda n, g: (cidx(n), nq + g))
    s_spec = pl.BlockSpec((c, LANE), lambda n, g: (cidx(n), 0))
    o_spec = pl.BlockSpec((c, hp * LANE), lambda n, g: (cidx(n), g))
    ck_spec = pl.BlockSpec((hp, 1, GDN_DK, LANE), lambda n, g: (g, cidx(n), 0, 0))
    return q_spec, k_spec, v_spec, s_spec, o_spec, ck_spec


def _gdn_fwd(qkv, beta, gc, shard, name):
    lp = qkv.shape[0]
    nc = lp // GDN_CHUNK
    nh = GDN_V_HEADS
    hp = GDN_HEADS_PER_STEP
    ng = nh // hp
    q_spec, k_spec, v_spec, s_spec, o_spec, ck_spec = _gdn_specs(nc, False)

    def body(q_ref, k_ref, v_ref, b_ref, g_ref, x_ref, o_ref, ck_ref, all_ref, state, send_sems, recv_sems, local_sem):
        n = pl.program_id(0)
        g = pl.program_id(1)
        x, y, c = _my_place()

        def local_copy():
            return pltpu.make_async_copy(x_ref, all_ref.at[2 * x + y], local_sem)

        def remote_copy(k, px, py, slot):
            return pltpu.make_async_remote_copy(
                src_ref=x_ref, dst_ref=all_ref.at[slot], send_sem=send_sems.at[k], recv_sem=recv_sems.at[k],
                device_id=(px, py, c), device_id_type=MESH)

        @pl.when(jnp.logical_and(n == 0, g == 0))
        def _():
            local_copy().start()
            for k, (px, py) in enumerate(_other_chips(x, y)):
                remote_copy(k, px, py, 2 * x + y).start()

        @pl.when(n == 0)
        def _():
            for i in range(hp):
                state[g * hp + i] = jnp.zeros((GDN_DK, LANE), F32)

        states = tuple(state[g * hp + i] for i in range(hp))
        for i in range(hp):
            ck_ref[i, 0] = states[i]
        o, new_states = _gdn_group(q_ref[...], k_ref[...], v_ref[...], b_ref[...], g_ref[...], states, g * hp)
        o_ref[...] = o
        for i in range(hp):
            state[g * hp + i] = new_states[i]

        @pl.when(jnp.logical_and(n == nc - 1, g == ng - 1))
        def _():
            for k, (px, py) in enumerate(_other_chips(x, y)):
                remote_copy(k, px, py, 2 * px + py).wait_recv()
            for k, (px, py) in enumerate(_other_chips(x, y)):
                remote_copy(k, px, py, 2 * x + y).wait_send()
            local_copy().wait()

    return _pcall(
        body, name=name, grid=(nc, ng),
        in_specs=[q_spec, k_spec, v_spec, s_spec, s_spec, _ANY],
        out_specs=[o_spec, ck_spec, _ANY],
        out_shape=[jax.ShapeDtypeStruct((lp, GDN_V_W), F32),
                   jax.ShapeDtypeStruct((nh, nc, GDN_DK, LANE), F32),
                   jax.ShapeDtypeStruct((4,) + shard.shape, shard.dtype)],
        scratch_shapes=[pltpu.VMEM((nh, GDN_DK, LANE), F32), pltpu.SemaphoreType.DMA((3,)),
                        pltpu.SemaphoreType.DMA((3,)), pltpu.SemaphoreType.DMA],
        compiler_params=_params(2),
    )(qkv, qkv, qkv, beta, gc, shard)


def _gdn_bwd(qkv, beta, gc, ckpt, do, parts, name):
    lp = qkv.shape[0]
    nc = lp // GDN_CHUNK
    nh = GDN_V_HEADS
    hp = GDN_HEADS_PER_STEP
    ng = nh // hp
    q_spec, k_spec, v_spec, s_spec, o_spec, ck_spec = _gdn_specs(nc, True)

    def body(q_ref, k_ref, v_ref, b_ref, g_ref, ck_ref, do_ref, p_ref,
             dq_ref, dk_ref, dv_ref, db_ref, dg_ref, from_ref, dstate, send_sems, recv_sems, local_sem):
        n = pl.program_id(0)
        g = pl.program_id(1)
        x, y, c = _my_place()
        me = 2 * x + y

        def local_copy():
            return pltpu.make_async_copy(p_ref.at[me], from_ref.at[me], local_sem)

        def remote_copy(k, px, py, src_slot, dst_slot):
            return pltpu.make_async_remote_copy(
                src_ref=p_ref.at[src_slot], dst_ref=from_ref.at[dst_slot], send_sem=send_sems.at[k],
                recv_sem=recv_sems.at[k], device_id=(px, py, c), device_id_type=MESH)

        @pl.when(jnp.logical_and(n == 0, g == 0))
        def _():
            local_copy().start()
            for k, (px, py) in enumerate(_other_chips(x, y)):
                remote_copy(k, px, py, 2 * px + py, me).start()

        @pl.when(jnp.logical_and(n == nc - 1, g == ng - 1))
        def _():
            for k, (px, py) in enumerate(_other_chips(x, y)):
                remote_copy(k, px, py, me, 2 * px + py).wait_recv()
            for k, (px, py) in enumerate(_other_chips(x, y)):
                remote_copy(k, px, py, 2 * px + py, me).wait_send()
            local_copy().wait()

        @pl.when(n == 0)
        def _():
            for i in range(hp):
                dstate[g * hp + i] = jnp.zeros((GDN_DK, LANE), F32)

        states = tuple(ck_ref[i, 0] for i in range(hp))
        _, vjp = jax.vjp(lambda q, k, v, b, gg, s: _gdn_group(q, k, v, b, gg, s, g * hp),
                         q_ref[...], k_ref[...], v_ref[...], b_ref[...], g_ref[...], states)
        dq, dk, dv, db, dg, ds = vjp((do_ref[...], tuple(dstate[g * hp + i] for i in range(hp))))
        dq_ref[...] = dq
        dk_ref[...] = dk
        dv_ref[...] = dv
        for i in range(hp):
            dstate[g * hp + i] = ds[i]

        @pl.when(g == 0)
        def _():
            db_ref[...] = db
            dg_ref[...] = dg

        @pl.when(g > 0)
        def _():
            db_ref[...] += db
            dg_ref[...] += dg

    qk_shape = jax.ShapeDtypeStruct((lp, GDN_QK_W), F32)
    big = jax.ShapeDtypeStruct((lp, GDN_V_W), F32)
    small = jax.ShapeDtypeStruct((lp, LANE), F32)
    dq_spec = pl.BlockSpec((GDN_CHUNK, hp // 2 * LANE), lambda n, g: (nc - 1 - n, g))
    return _pcall(
        body, name=name, grid=(nc, ng),
        in_specs=[q_spec, k_spec, v_spec, s_spec, s_spec, ck_spec, o_spec, _ANY],
        out_specs=[dq_spec, dq_spec, o_spec, s_spec, s_spec, _ANY],
        out_shape=[qk_shape, qk_shape, big, small, small, jax.ShapeDtypeStruct(parts.shape, parts.dtype)],
        scratch_shapes=[pltpu.VMEM((nh, GDN_DK, LANE), F32), pltpu.SemaphoreType.DMA((3,)),
                        pltpu.SemaphoreType.DMA((3,)), pltpu.SemaphoreType.DMA],
        compiler_params=_params(2),
    )(qkv, qkv, qkv, beta, gc, ckpt, do, parts)


LOG2E = 1.4426950408889634
LN2 = 0.6931471805599453
Q_PRESCALE = MLA_QK ** -0.5 * LOG2E


ATT_SUB = 128
ATT_HEADS_PER_STEP = 4
ATT_BWD_HEADS_PER_STEP = 2


def _att_mask(i, j, tb, transposed):
    r = lax.broadcasted_iota(jnp.int32, (tb, tb), 0)
    c = lax.broadcasted_iota(jnp.int32, (tb, tb), 1)
    qpos, kpos = (i * tb + c, j * tb + r) if transposed else (i * tb + r, j * tb + c)
    return jnp.logical_and(kpos <= qpos, kpos >= FRONT)


def _causal_pairs(nb, by_key):
    if by_key:
        pairs = [(i, j) for j in range(nb) for i in range(j, nb)]
    else:
        pairs = [(i, j) for i in range(nb) for j in range(i + 1)]
    return jnp.array([p[0] for p in pairs], jnp.int32), jnp.array([p[1] for p in pairs], jnp.int32)


def _masked_and_plain(i, j, step):
    edge = jnp.logical_or(j == i, j == 0)

    @pl.when(jnp.logical_and(edge, j <= i))
    def _():
        step(True)

    @pl.when(jnp.logical_and(jnp.logical_not(edge), j < i))
    def _():
        step(False)


def _cat(a_ref, b_ref):
    return jnp.concatenate([a_ref[...], b_ref[...]], axis=1)


def _flash_fwd(qn, qr, kn, kr, v, name, tb=ROW_ALIGN):
    lp = qn.shape[0]
    nb = lp // tb
    nh = MLA_HEADS
    hp = ATT_HEADS_PER_STEP
    qi, kj = _causal_pairs(nb, by_key=False)

    def body(qi_ref, kj_ref, qn_ref, qr_ref, kn_ref, kr_ref, v_ref, o_ref, lse_ref, m_s, l_s, acc):
        t = pl.program_id(1)
        i, j = qi_ref[t], kj_ref[t]

        @pl.when(j == 0)
        def _():
            m_s[...] = jnp.full_like(m_s, NEG)
            l_s[...] = jnp.zeros_like(l_s)
            acc[...] = jnp.zeros_like(acc)

        def step(masked):
            n_sub = tb // ATT_SUB
            kr = kr_ref[...]
            for e in range(hp):
                lanes = pl.ds(e * LANE, LANE)
                k = jnp.concatenate([kn_ref[:, lanes], kr], axis=1)
                v = v_ref[:, lanes]

                def scores(r, lanes=lanes, k=k):
                    rows = pl.ds(r * ATT_SUB, ATT_SUB)
                    return _dot(jnp.concatenate([qn_ref[rows, lanes], qr_ref[rows, lanes]], axis=1), k, _NT)

                s_next = scores(0)
                for r in range(n_sub):
                    s = s_next
                    if r + 1 < n_sub:
                        s_next = scores(r + 1)
                    rows = pl.ds(r * ATT_SUB, ATT_SUB)
                    if masked:
                        qpos = i * tb + r * ATT_SUB + lax.broadcasted_iota(jnp.int32, (ATT_SUB, tb), 0)
                        kpos = j * tb + lax.broadcasted_iota(jnp.int32, (ATT_SUB, tb), 1)
                        s = jnp.where(jnp.logical_and(kpos <= qpos, kpos >= FRONT), s, NEG)
                    m_old = m_s[e, rows, :]
                    m_new = jnp.maximum(m_old, jnp.max(s, axis=1, keepdims=True))
                    alpha = jnp.exp2(m_old - m_new)
                    p = jnp.exp2(s - m_new)
                    l_s[e, rows, :] = alpha * l_s[e, rows, :] + jnp.sum(p, axis=1, keepdims=True)
                    acc[e, rows, :] = alpha * acc[e, rows, :] + _dot(p.astype(BF16), v, _NN)
                    m_s[e, rows, :] = m_new

        _masked_and_plain(i, j, step)

        @pl.when(j == i)
        def _():
            for e in range(hp):
                lanes = pl.ds(e * LANE, LANE)
                o_ref[:, lanes] = acc[e] / l_s[e]
                lse_ref[:, lanes] = jnp.broadcast_to(m_s[e] + jnp.log(l_s[e]) * LOG2E, (tb, LANE))

    qspec = pl.BlockSpec((tb, hp * LANE), lambda h, t, qi_, kj_: (qi_[t], h))
    kspec = pl.BlockSpec((tb, hp * LANE), lambda h, t, qi_, kj_: (kj_[t], h))
    krspec = pl.BlockSpec((tb, LANE), lambda h, t, qi_, kj_: (kj_[t], 0))
    shp = jax.ShapeDtypeStruct((lp, nh * LANE), F32)
    return _pcall(
        body, name=name, out_shape=[shp, shp],
        grid_spec=pltpu.PrefetchScalarGridSpec(
            num_scalar_prefetch=2, grid=(nh // hp, qi.shape[0]),
            in_specs=[qspec, qspec, kspec, krspec, kspec], out_specs=[qspec, qspec],
            scratch_shapes=[pltpu.VMEM((hp, tb, 1), F32), pltpu.VMEM((hp, tb, 1), F32),
                            pltpu.VMEM((hp, tb, LANE), F32)]),
        compiler_params=_params(2),
    )(qi, kj, qn, qr, kn, kr, v)


def _flash_bwd(qn, qr, kn, kr, v, o, do, lse, name, tb=ROW_ALIGN):
    lp = qn.shape[0]
    nb = lp // tb
    nh = MLA_HEADS
    hp = ATT_BWD_HEADS_PER_STEP
    qi, kj = _causal_pairs(nb, by_key=True)
    n_pairs = qi.shape[0]
    knt, krt = kn.T, kr.T

    def body(qi_ref, kj_ref, qn_ref, qr_ref, kn_ref, kr_ref, knt_ref, krt_ref, v_ref, o_ref, do_ref, lse_ref,
             dqnt_hbm, dqrt_hbm, dkn_ref, dkr_ref, dv_ref, dk_acc, dv_acc, dqn_acc, dqr_acc, out_sems):
        g = pl.program_id(0)
        t = pl.program_id(1)
        i, j = qi_ref[t], kj_ref[t]

        @pl.when(t == 0)
        def _():
            dqn_acc[...] = jnp.zeros_like(dqn_acc)
            dqr_acc[...] = jnp.zeros_like(dqr_acc)

        @pl.when(i == j)
        def _():
            dk_acc[...] = jnp.zeros_like(dk_acc)
            dv_acc[...] = jnp.zeros_like(dv_acc)

        def step(masked):
            kr = kr_ref[...]
            krt_blk = krt_ref[...]
            lane = lax.broadcasted_iota(jnp.int32, (8, LANE), 1)
            for e in range(hp):
                lanes = pl.ds(e * LANE, LANE)
                q = jnp.concatenate([qn_ref[:, lanes], qr_ref[:, lanes]], axis=1)
                st = _dot(jnp.concatenate([kn_ref[:, lanes], kr], axis=1), q, _NT)
                if masked:
                    st = jnp.where(_att_mask(i, j, tb, True), st, NEG)
                do_blk = do_ref[:, lanes]
                lse_row = _hdot((lane == 0).astype(F32), lse_ref[:, lanes], _NT)[0:1]
                delta_row = _hdot(jnp.ones((8, LANE), F32), do_blk * o_ref[:, lanes], _NT)[0:1]
                pt = jnp.exp2(st - lse_row)
                do_b = do_blk.astype(BF16)
                dv_acc[e] += _dot(pt.astype(BF16), do_b, _NN)
                dpt = _dot(v_ref[:, lanes], do_b, _NT)
                dst = (pt * (dpt - delta_row)).astype(BF16)
                dk_acc[e] += _dot(dst, q, _NN)
                dqn_acc[e * nb + i] += _dot(knt_ref[pl.ds(e * LANE, LANE), :], dst, _NN) * LN2
                dqr_acc[e * nb + i] += _dot(krt_blk, dst, _NN) * LN2

        _masked_and_plain(i, j, step)

        @pl.when(i == nb - 1)
        def _():
            for e in range(hp):
                lanes = pl.ds(e * LANE, LANE)
                dkn_ref[:, lanes] = dk_acc[e, :, :LANE] * LN2
                dkr_ref[:, lanes] = dk_acc[e, :, LANE:] * LN2
                dv_ref[:, lanes] = dv_acc[e]

        @pl.when(t == n_pairs - 1)
        def _():
            dst_rows = pl.ds(g * (hp * nb), hp * nb)
            cn = pltpu.make_async_copy(dqn_acc, dqnt_hbm.at[dst_rows], out_sems.at[0])
            cr = pltpu.make_async_copy(dqr_acc, dqrt_hbm.at[dst_rows], out_sems.at[1])
            cn.start()
            cr.start()
            cn.wait()
            cr.wait()

    qspec = pl.BlockSpec((tb, hp * LANE), lambda h, t, qi_, kj_: (qi_[t], h))
    kspec = pl.BlockSpec((tb, hp * LANE), lambda h, t, qi_, kj_: (kj_[t], h))
    krspec = pl.BlockSpec((tb, LANE), lambda h, t, qi_, kj_: (kj_[t], 0))
    ktspec = pl.BlockSpec((hp * LANE, tb), lambda h, t, qi_, kj_: (h, kj_[t]))
    krtspec = pl.BlockSpec((LANE, tb), lambda h, t, qi_, kj_: (0, kj_[t]))
    shp = jax.ShapeDtypeStruct((lp, nh * LANE), F32)
    dqt_shape = jax.ShapeDtypeStruct((nh * nb, LANE, tb), F32)
    dqnt, dqrt, dkn, dkr, dv = _pcall(
        body, name=name, out_shape=[dqt_shape, dqt_shape, shp, shp, shp],
        grid_spec=pltpu.PrefetchScalarGridSpec(
            num_scalar_prefetch=2, grid=(nh // hp, n_pairs),
            in_specs=[qspec, qspec, kspec, krspec, ktspec, krtspec, kspec, qspec, qspec, qspec],
            out_specs=[_ANY, _ANY, kspec, kspec, kspec],
            scratch_shapes=[pltpu.VMEM((hp, tb, 2 * LANE), F32), pltpu.VMEM((hp, tb, LANE), F32),
                            pltpu.VMEM((hp * nb, LANE, tb), F32), pltpu.VMEM((hp * nb, LANE, tb), F32),
                            pltpu.SemaphoreType.DMA((2,))]),
        compiler_params=_params(2),
    )(qi, kj, qn, qr, kn, kr, knt, krt, v, o, do, lse)

    def rows_major(a):
        return a.reshape(nh, nb, LANE, tb).transpose(1, 3, 0, 2).reshape(lp, nh * LANE)

    return rows_major(dqnt), rows_major(dqrt), dkn, dkr, dv


ELEMENTWISE_BLOCK_BYTES = 1 << 20


def _row_tile(rows, width, copies=1):
    for t in (1024, 512, 256, 128, 64, 32, 16, 8):
        if rows % t == 0 and t * width * 4 * copies <= ELEMENTWISE_BLOCK_BYTES:
            return t
    return rows


def _adamw(w, g, m, v, name):
    rows, width = w.shape
    tr = _row_tile(rows, width)

    def body(w_ref, g_ref, m_ref, v_ref, d_ref, nm_ref, nv_ref):
        gg = g_ref[...]
        nm = ADAM_B1 * m_ref[...] + (1.0 - ADAM_B1) * gg
        nv = ADAM_B2 * v_ref[...] + (1.0 - ADAM_B2) * jnp.square(gg)
        m_hat = nm / (1.0 - ADAM_B1 ** ADAM_STEP)
        v_hat = nv / (1.0 - ADAM_B2 ** ADAM_STEP)
        d_ref[...] = -ADAM_LR * (m_hat / (jnp.sqrt(v_hat) + ADAM_EPS) + ADAM_WD * w_ref[...])
        nm_ref[...] = nm
        nv_ref[...] = nv

    spec = pl.BlockSpec((tr, width), lambda r: (r, 0))
    shp = jax.ShapeDtypeStruct((rows, width), F32)
    return _pcall(body, name=name, grid=(rows // tr,), in_specs=[spec] * 4, out_specs=[spec] * 3,
                  out_shape=[shp] * 3, compiler_params=_params(1))(w, g, m, v)


def _add_pair(a, b, name):
    s, rows, width = b.shape
    tr = _row_tile(rows, width)
    nt = rows // tr

    def body(c_ref, a_ref, b_ref, o_ref):
        o_ref[...] = a_ref[...] + b_ref[...]

    spec = pl.BlockSpec((1, tr, width), lambda i, r, c_ref: (i, r, 0))
    return _pcall(
        body, name=name, out_shape=jax.ShapeDtypeStruct(b.shape, F32),
        grid_spec=pltpu.PrefetchScalarGridSpec(
            num_scalar_prefetch=1, grid=(s, nt),
            in_specs=[pl.BlockSpec((1, tr, width), lambda i, r, c_ref: (i, c_ref[0] * nt + r, 0)), spec],
            out_specs=spec),
        compiler_params=_params(2),
    )(_core_index(), a, b)


def _sum_slots(a, name):
    s, rows, width = a.shape
    tr = _row_tile(rows, width, copies=s)

    def body(a_ref, o_ref):
        tot = a_ref[0]
        for k in range(1, s):
            tot = tot + a_ref[k]
        o_ref[...] = tot

    return _pcall(body, name=name, grid=(rows // tr,),
                  in_specs=[pl.BlockSpec((s, tr, width), lambda r: (0, r, 0))],
                  out_specs=pl.BlockSpec((tr, width), lambda r: (r, 0)),
                  out_shape=jax.ShapeDtypeStruct((rows, width), F32), compiler_params=_params(1))(a)


_ANY = pl.BlockSpec(memory_space=pl.ANY)


def _my_place():
    return lax.axis_index("x"), lax.axis_index("y"), lax.axis_index("c")


def _core_index():
    return lax.axis_index("c").astype(jnp.int32).reshape(1)


def _other_chips(x, y):
    return [(1 - x, y), (x, 1 - y), (1 - x, 1 - y)]


def _gather_shards(flat, name):
    rows, width = flat.shape

    def body(x_ref, out_ref, send_sems, recv_sems, local_sem):
        x, y, c = _my_place()
        mine = pltpu.make_async_copy(x_ref, out_ref.at[2 * x + y], local_sem)
        mine.start()
        sends = []
        for k, (px, py) in enumerate(_other_chips(x, y)):
            cp = pltpu.make_async_remote_copy(
                src_ref=x_ref, dst_ref=out_ref.at[2 * x + y], send_sem=send_sems.at[k], recv_sem=recv_sems.at[k],
                device_id=(px, py, c), device_id_type=MESH)
            cp.start()
            sends.append(cp)
        for k, (px, py) in enumerate(_other_chips(x, y)):
            pltpu.make_async_remote_copy(
                src_ref=x_ref, dst_ref=out_ref.at[2 * px + py], send_sem=send_sems.at[k], recv_sem=recv_sems.at[k],
                device_id=(px, py, c), device_id_type=MESH).wait_recv()
        for cp in sends:
            cp.wait_send()
        mine.wait()

    return _pcall(
        body, name=name, in_specs=[_ANY], out_specs=_ANY,
        out_shape=jax.ShapeDtypeStruct((4, rows, width), flat.dtype),
        scratch_shapes=[pltpu.SemaphoreType.DMA((3,)), pltpu.SemaphoreType.DMA((3,)), pltpu.SemaphoreType.DMA],
    )(flat)


def _sibling_split(g, name):
    s, rows, width = g.shape
    half = rows // 2
    tr = _row_tile(half, width)
    nt = half // tr

    def body(c_ref, g_blk, got_ref, send_sem, recv_sem):
        k = pl.program_id(0)
        t = pl.program_id(1)
        x, y, c = _my_place()
        cp = pltpu.make_async_remote_copy(
            src_ref=g_blk.at[0], dst_ref=got_ref.at[k, pl.ds(pl.multiple_of(t * tr, 8), tr), :],
            send_sem=send_sem, recv_sem=recv_sem, device_id=(x, y, 1 - c), device_id_type=MESH)
        cp.start()
        cp.wait_send()

        @pl.when(jnp.logical_and(k == s - 1, t == nt - 1))
        def _():
            pltpu.make_async_remote_copy(
                src_ref=got_ref, dst_ref=got_ref, send_sem=send_sem, recv_sem=recv_sem,
                device_id=(x, y, 1 - c), device_id_type=MESH).wait_recv()

    return _pcall(
        body, name=name, out_shape=jax.ShapeDtypeStruct((s, half, width), g.dtype),
        grid_spec=pltpu.PrefetchScalarGridSpec(
            num_scalar_prefetch=1, grid=(s, nt),
            in_specs=[pl.BlockSpec((1, tr, width), lambda k, t, c_ref: (k, (1 - c_ref[0]) * nt + t, 0))],
            out_specs=_ANY,
            scratch_shapes=[pltpu.SemaphoreType.DMA, pltpu.SemaphoreType.DMA]),
        compiler_params=_params(2),
    )(_core_index(), g)


def _chip_scatter(p, name):
    s, rows, width = p.shape

    def body(p_ref, out_ref, send_sems, recv_sems, local_sem):
        x, y, c = _my_place()
        me = 2 * x + y
        mine = pltpu.make_async_copy(p_ref.at[me], out_ref.at[me], local_sem)
        mine.start()
        sends = []
        for k, (px, py) in enumerate(_other_chips(x, y)):
            cp = pltpu.make_async_remote_copy(
                src_ref=p_ref.at[2 * px + py], dst_ref=out_ref.at[me], send_sem=send_sems.at[k],
                recv_sem=recv_sems.at[k], device_id=(px, py, c), device_id_type=MESH)
            cp.start()
            sends.append(cp)
        for k, (px, py) in enumerate(_other_chips(x, y)):
            pltpu.make_async_remote_copy(
                src_ref=p_ref.at[me], dst_ref=out_ref.at[2 * px + py], send_sem=send_sems.at[k],
                recv_sem=recv_sems.at[k], device_id=(px, py, c), device_id_type=MESH).wait_recv()
        for cp in sends:
            cp.wait_send()
        mine.wait()

    return _pcall(
        body, name=name, in_specs=[_ANY], out_specs=_ANY,
        out_shape=jax.ShapeDtypeStruct(p.shape, p.dtype),
        scratch_shapes=[pltpu.SemaphoreType.DMA((3,)), pltpu.SemaphoreType.DMA((3,)), pltpu.SemaphoreType.DMA],
    )(p)


def _sibling_join(qh, name):
    half, width = qh.shape
    tr = _row_tile(half, width)
    nt = half // tr

    def body(q_blk, out_ref, send_sem, recv_sem, local_sem):
        t = pl.program_id(0)
        x, y, c = _my_place()
        dst = out_ref.at[pl.ds(pl.multiple_of(c * half + t * tr, 8), tr), :]
        cp = pltpu.make_async_remote_copy(
            src_ref=q_blk, dst_ref=dst, send_sem=send_sem, recv_sem=recv_sem,
            device_id=(x, y, 1 - c), device_id_type=MESH)
        cp.start()
        mine = pltpu.make_async_copy(q_blk, dst, local_sem)
        mine.start()
        cp.wait_send()
        mine.wait()

        @pl.when(t == nt - 1)
        def _():
            theirs = out_ref.at[pl.ds(pl.multiple_of((1 - c) * half, 8), half), :]
            pltpu.make_async_remote_copy(
                src_ref=theirs, dst_ref=theirs, send_sem=send_sem, recv_sem=recv_sem,
                device_id=(x, y, 1 - c), device_id_type=MESH).wait_recv()

    return _pcall(
        body, name=name, grid=(nt,),
        in_specs=[pl.BlockSpec((tr, width), lambda t: (t, 0))], out_specs=_ANY,
        out_shape=jax.ShapeDtypeStruct((2 * half, width), qh.dtype),
        scratch_shapes=[pltpu.SemaphoreType.DMA, pltpu.SemaphoreType.DMA, pltpu.SemaphoreType.DMA],
        compiler_params=_params(1),
    )(qh)


def _all_sum_small(part, name):
    rows, width = part.shape

    def body(p_ref, out_ref, land, send_sems, recv_sems):
        x, y, c = _my_place()
        me = 4 * x + 2 * y + c
        land[me] = p_ref[...]
        sends = []
        for k in range(1, 8):
            peer = (x ^ (k >> 2), y ^ ((k >> 1) & 1), c ^ (k & 1))
            cp = pltpu.make_async_remote_copy(
                src_ref=p_ref, dst_ref=land.at[me], send_sem=send_sems.at[k - 1], recv_sem=recv_sems.at[k - 1],
                device_id=peer, device_id_type=MESH)
            cp.start()
            sends.append(cp)
        for k in range(1, 8):
            px, py, pc = x ^ (k >> 2), y ^ ((k >> 1) & 1), c ^ (k & 1)
            pltpu.make_async_remote_copy(
                src_ref=p_ref, dst_ref=land.at[4 * px + 2 * py + pc], send_sem=send_sems.at[k - 1],
                recv_sem=recv_sems.at[k - 1], device_id=(px, py, pc), device_id_type=MESH).wait_recv()
        for cp in sends:
            cp.wait_send()
        tot = land[0]
        for k in range(1, 8):
            tot = tot + land[k]
        out_ref[...] = tot

    vmem = pl.BlockSpec(memory_space=pltpu.VMEM)
    return _pcall(
        body, name=name, in_specs=[vmem], out_specs=vmem,
        out_shape=jax.ShapeDtypeStruct((rows, width), F32),
        scratch_shapes=[pltpu.VMEM((8, rows, width), F32), pltpu.SemaphoreType.DMA((7,)),
                        pltpu.SemaphoreType.DMA((7,))],
    )(part)


def _big_layout(shards):
    return [(a.shape[0], a.shape[1], ax) for a, ax in shards]


FLAT_ROW_MULTIPLE = 2048


def _pack_shards(arrs, row_multiple=FLAT_ROW_MULTIPLE):
    flat = jnp.concatenate([a.reshape(-1) for a in arrs])
    return jnp.pad(flat, (0, -flat.shape[0] % (row_multiple * LANE))).reshape(-1, LANE)


def _unpack_shards(flat, layout):
    flat = flat.reshape(-1)
    out, off = [], 0
    for r, c, _ in layout:
        out.append(flat[off:off + r * c].reshape(r, c))
        off += r * c
    return out


def _unpack_full(gathered, layout):
    g = gathered.reshape(4, -1)
    out, off = [], 0
    for r, c, ax in layout:
        seg = g[:, off:off + r * c].reshape(4, r, c)
        out.append(seg.transpose(1, 0, 2).reshape(r, 4 * c) if ax == 1 else seg.reshape(4 * r, c))
        off += r * c
    return out


def _pack_full(fulls, layout):
    parts = []
    for a, (r, c, ax) in zip(fulls, layout):
        if ax == 1:
            parts.append(a.reshape(r, 4, c).transpose(1, 0, 2).reshape(4, r * c))
        else:
            parts.append(a.reshape(4, r * c))
    flat = jnp.concatenate(parts, axis=1)
    return jnp.pad(flat, ((0, 0), (0, -flat.shape[1] % (FLAT_ROW_MULTIPLE * LANE)))).reshape(4, -1, LANE)


def _pad_lanes(a, width=LANE):
    return jnp.pad(a, [(0, 0)] * (a.ndim - 1) + [(0, width - a.shape[-1])])


def _pack_small(arrs):
    rows = [_pad_lanes(a.reshape(1, -1), -(-a.size // LANE) * LANE).reshape(-1, LANE) for a in arrs]
    flat = jnp.concatenate(rows, axis=0)
    return jnp.pad(flat, ((0, -flat.shape[0] % 8), (0, 0)))


def _unpack_small(flat, shapes):
    out, off = [], 0
    for shp in shapes:
        n = math.prod(shp)
        nr = -(-n // LANE)
        out.append(flat[off:off + nr].reshape(-1)[:n].reshape(shp))
        off += nr
    return out


def kernel(x, meta_tokens, pre_norm, post_norm, gdn_w_in, gdn_conv_w, gdn_a_log, gdn_dt_bias, gdn_out_norm, gdn_w_out, kv_norm, kv_w_down, kv_latent_norm, kv_w_up, mla_w_in, mla_q_latent_norm, mla_w_q_up, mla_w_out, loss_target, m_meta_tokens, m_pre_norm, m_post_norm, m_gdn_w_in, m_gdn_conv_w, m_gdn_a_log, m_gdn_dt_bias, m_gdn_out_norm, m_gdn_w_out, m_kv_norm, m_kv_w_down, m_kv_latent_norm, m_kv_w_up, m_mla_w_in, m_mla_q_latent_norm, m_mla_w_q_up, m_mla_w_out, v_meta_tokens, v_pre_norm, v_post_norm, v_gdn_w_in, v_gdn_conv_w, v_gdn_a_log, v_gdn_dt_bias, v_gdn_out_norm, v_gdn_w_out, v_kv_norm, v_kv_w_down, v_kv_latent_norm, v_kv_w_up, v_mla_w_in, v_mla_q_latent_norm, v_mla_w_q_up, v_mla_w_out):
    seq = x.shape[1]
    d = D_MODEL
    lp = -(-(ROW0 + seq) // ROW_ALIGN) * ROW_ALIGN
    tail = lp - ROW0 - seq

    big_names = ["meta_tokens", "gdn_conv_w", "gdn_w_out", "kv_w_down", "kv_w_up", "mla_w_in", "mla_w_q_up",
                 "mla_w_out"]
    big_axis = [1, 1, 0, 0, 1, 1, 1, 0]
    big_w = [meta_tokens, gdn_conv_w[0], gdn_w_out[0], kv_w_down, kv_w_up, mla_w_in[0], mla_w_q_up[0], mla_w_out[0]]
    big_m = [m_meta_tokens, m_gdn_conv_w[0], m_gdn_w_out[0], m_kv_w_down, m_kv_w_up, m_mla_w_in[0], m_mla_w_q_up[0],
             m_mla_w_out[0]]
    big_v = [v_meta_tokens, v_gdn_conv_w[0], v_gdn_w_out[0], v_kv_w_down, v_kv_w_up, v_mla_w_in[0], v_mla_w_q_up[0],
             v_mla_w_out[0]]
    layout = _big_layout(list(zip(big_w, big_axis)))
    meta_f, conv_w = _unpack_full(
        _gather_shards(_pack_shards(big_w[:2], row_multiple=16), "gather_meta_conv"), layout[:2])
    mm_shards = [w.astype(BF16) for w in big_w[2:6]] + [(big_w[6] * Q_PRESCALE).astype(BF16), big_w[7].astype(BF16)]
    mm_flat = _pack_shards(mm_shards)
    w_in0_shards = _gather_shards(gdn_w_in[0].astype(BF16), "gather_gdn_w_in")
    w_in0 = jnp.concatenate([w_in0_shards[s] for s in range(4)], axis=1)
    win_cols = gdn_w_in.shape[2]

    nv = GDN_V_HEADS
    w_qkv = w_in0[:, :GDN_CONV_W]
    w_z0 = w_in0[:, GDN_CONV_W:GDN_CONV_W + GDN_V_W]
    w_b = _pad_lanes(w_in0[:, GDN_CONV_W + GDN_V_W:GDN_CONV_W + GDN_V_W + nv])
    w_a = _pad_lanes(w_in0[:, GDN_CONV_W + GDN_V_W + nv:])

    pre0, pre1 = pre_norm[0:1], pre_norm[1:2]
    post0, post1 = post_norm[0:1], post_norm[1:2]
    a_log = _pad_lanes(gdn_a_log)
    dt_bias = _pad_lanes(gdn_dt_bias)
    kvn = kv_norm.reshape(1, d)
    kvl = kv_latent_norm.reshape(1, MLA_KV_RANK)
    qln = mla_q_latent_norm

    h0 = jnp.concatenate([jnp.zeros((FRONT, d), F32), meta_f, x[0], jnp.zeros((tail, d), F32)], axis=0)
    tgt = jnp.pad(loss_target[0], ((ROW0, tail), (0, 0)))
    pos = jnp.maximum(jnp.arange(lp, dtype=jnp.int32) - FRONT, 0).astype(F32)
    inv = ROPE_THETA ** (-jnp.arange(0, MLA_ROPE, 2, dtype=F32) / MLA_ROPE)
    ang = pos[:, None] * inv[None, :]
    zeros64 = jnp.zeros((lp, LANE - MLA_ROPE), F32)
    cos_t = jnp.concatenate([jnp.cos(ang), jnp.cos(ang), zeros64], axis=1)
    sin_t = jnp.concatenate([-jnp.sin(ang), jnp.sin(ang), zeros64], axis=1)

    def valid_rows(ridx):
        return jnp.logical_and(ridx >= FRONT, ridx < ROW0 + seq)

    def f_pre0(ridx, g, h, gain):
        return _rms(h, gain), h

    (hn0,) = _rowwise("pre0", lambda *a: f_pre0(*a)[:1], [_In(h0), _In(pre0, "const")],
                      [_Out("row", (lp, d), BF16)])
    qkv_raw = _mm(hn0, w_qkv, "nn", "gdn_in_qkv")
    z0 = _mm(hn0, w_z0, "nn", "gdn_in_z")
    b_raw = _mm(hn0, w_b, "nn", "gdn_in_b")
    a_raw = _mm(hn0, w_a, "nn", "gdn_in_a")

    def f_ba(ridx, g, b, a, alog, dtb):
        tr = b.shape[0]
        ok = valid_rows(ridx).astype(F32)
        beta = jax.nn.sigmoid(b) * ok
        gate = -jnp.exp(alog) * _softplus(a + dtb) * ok
        ii = lax.broadcasted_iota(jnp.int32, (tr, tr), 0)
        jj = lax.broadcasted_iota(jnp.int32, (tr, tr), 1)
        shift = GDN_CHUNK.bit_length() - 1
        tri = jnp.logical_and((ii >> shift) == (jj >> shift), ii >= jj).astype(F32)
        return beta, _hdot(tri, gate)

    ba_ins = [_In(b_raw), _In(a_raw), _In(a_log, "const"), _In(dt_bias, "const")]
    beta, gc = _rowwise("gdn_gates", f_ba, ba_ins, [_Out("row", (lp, LANE)), _Out("row", (lp, LANE))])
    qkv = _conv_fwd(qkv_raw, conv_w, "gdn_conv")
    o0, ckpt, mm_all = _gdn_fwd(qkv, beta, gc, mm_flat, "gdn_scan")
    (w_out0, kv_down, kv_up, w_in1, w_qup, w_out1) = _unpack_full(mm_all, layout[2:])
    w_ckv = kv_down[:, :MLA_KV_RANK]
    w_kr = _pad_lanes(kv_down[:, MLA_KV_RANK:])
    kvu = kv_up.reshape(MLA_KV_RANK, MLA_HEADS, 2 * LANE)
    w_kn = kvu[:, :, :LANE].reshape(MLA_KV_RANK, MLA_HEADS * LANE)
    w_v = kvu[:, :, LANE:].reshape(MLA_KV_RANK, MLA_HEADS * LANE)
    w_cq = w_in1[:, :MLA_Q_RANK]
    w_z1 = w_in1[:, MLA_Q_RANK:]
    qu = w_qup.reshape(MLA_Q_RANK, MLA_HEADS, MLA_QK)
    w_qn = qu[:, :, :MLA_NOPE].reshape(MLA_Q_RANK, MLA_HEADS * LANE)
    w_qr = _pad_lanes(qu[:, :, MLA_NOPE:]).reshape(MLA_Q_RANK, MLA_HEADS * LANE)

    def per_head(fn, *arrs):
        n = arrs[0].shape[1] // LANE
        return jnp.concatenate([fn(*[a[:, i * LANE:(i + 1) * LANE] for a in arrs]) for i in range(n)], axis=1)

    def f_gate0(ridx, g, o, z, gain):
        return (per_head(lambda oh, zh: _rms(oh, gain) * _silu(zh), o, z),)

    gate0_ins = [_In(o0), _In(z0), _In(gdn_out_norm, "const")]
    (gated0,) = _rowwise("gdn_gate", f_gate0, gate0_ins, [_Out("row", (lp, GDN_V_W), BF16)])
    y0 = _mm(gated0, w_out0, "nn", "gdn_out")

    def f_mid(ridx, g, h, y, g_post, g_pre, g_kv):
        h1 = h + _rms(y, g_post)
        return h1, _rms(h1, g_pre), _rms(h1, g_kv)

    mid_ins = [_In(h0), _In(y0), _In(post0, "const"), _In(pre1, "const"), _In(kvn, "const")]
    h1, hn1, hkv = _rowwise("mid", f_mid, mid_ins,
                            [_Out("row", (lp, d)), _Out("row", (lp, d), BF16), _Out("row", (lp, d), BF16)])

    ckv_raw = _mm(hkv, w_ckv, "nn", "kv_down_c")
    kr_raw = _mm(hkv, w_kr, "nn", "kv_down_r")

    def f_ckv(ridx, g, c, r, cs, sn, gain):
        return _rms(c, gain), _rope(r, cs, sn)

    ckv_ins = [_In(ckv_raw), _In(kr_raw), _In(cos_t), _In(sin_t), _In(kvl, "const")]
    ckv, kr = _rowwise("kv_latent", f_ckv, ckv_ins, [_Out("row", (lp, LANE)), _Out("row", (lp, LANE), BF16)],
                       tr=TR_FULL)
    kn = _mm(ckv, w_kn, "nn", "kv_up_k", BF16)
    vv = _mm(ckv, w_v, "nn", "kv_up_v", BF16)
    cq_raw = _mm(hn1, w_cq, "nn", "mla_in_q")
    z1 = _mm(hn1, w_z1, "nn", "mla_in_z")

    def f_cq(ridx, g, c, gain):
        return (_rms(c, gain),)

    cq_ins = [_In(cq_raw), _In(qln, "const")]
    (cq,) = _rowwise("q_latent", f_cq, cq_ins, [_Out("row", (lp, MLA_Q_RANK))], tr=TR_FULL)
    qn = _mm(cq, w_qn, "nn", "q_up_n", BF16)
    qr_raw = _mm(cq, w_qr, "nn", "q_up_r")

    def f_qrope(ridx, g, r, cs, sn):
        return (per_head(lambda rh: _rope(rh, cs, sn), r),)

    qr_ins = [_In(qr_raw), _In(cos_t), _In(sin_t)]
    (qr,) = _rowwise("q_rope", f_qrope, qr_ins, [_Out("row", (lp, MLA_HEADS * LANE), BF16)])
    o1, lse = _flash_fwd(qn, qr, kn, kr, vv, "attention")

    def f_gate1(ridx, g, o, z):
        return (o * _silu(z),)

    gate1_ins = [_In(o1), _In(z1)]
    (og,) = _rowwise("mla_gate", f_gate1, gate1_ins, [_Out("row", (lp, MLA_HEADS * LANE), BF16)])
    y1 = _mm(og, w_out1, "nn", "mla_out")

    def f_final(ridx, g, h, y, t, gain):
        ok = jnp.logical_and(ridx >= ROW0, ridx < ROW0 + seq).astype(F32)

        def rows_loss(h_, y_, gain_):
            err = (h_ + _rms(y_, gain_) - t) * ok
            return 0.5 * jnp.sum(jnp.sum(err * err, axis=1, keepdims=True), axis=0, keepdims=True) / d

        val, vjp = jax.vjp(rows_loss, h, y, gain)
        dh, dy, dgain = vjp(jnp.ones((1, 1), F32))
        return dh, dy, dgain, jnp.broadcast_to(val, (1, LANE))

    dh2, dy1, dpost1, loss_part = _rowwise(
        "loss_head", f_final, [_In(h1), _In(y1), _In(tgt), _In(post1, "const")],
        [_Out("row", (lp, d)), _Out("row", (lp, d)), _Out("acc", (1, d)), _Out("acc", (1, LANE))])

    dog = _mm(dy1, w_out1, "nt", "mla_out_dx")
    dw_out1 = _mm(og, dy1, "tn", "mla_out_dw")
    do1, dz1 = _rowwise_vjp("mla_gate_bwd", f_gate1, gate1_ins, [[dog]], [0, 1])
    dqn, dqr, dkn, dkr, dvv = _flash_bwd(qn, qr, kn, kr, vv, o1, do1, lse, "attention_bwd")
    (dqr_raw,) = _rowwise_vjp("q_rope_bwd", f_qrope, qr_ins, [[dqr]], [0])
    dcq_a = _mm(dqn, w_qn, "nt", "q_up_n_dx")
    dcq_b = _mm(dqr_raw, w_qr, "nt", "q_up_r_dx")
    dw_qn = _mm(cq, dqn, "tn", "q_up_n_dw") * Q_PRESCALE
    dw_qr = _mm(cq, dqr_raw, "tn", "q_up_r_dw") * Q_PRESCALE
    dcq_raw, dqln = _rowwise_vjp("q_latent_bwd", f_cq, cq_ins, [[dcq_a, dcq_b]], [0, 1], tr=TR_FULL)
    dhn1_a = _mm(dcq_raw, w_cq, "nt", "mla_in_q_dx")
    dhn1_b = _mm(dz1, w_z1, "nt", "mla_in_z_dx")
    dw_cq = _mm(hn1, dcq_raw, "tn", "mla_in_q_dw")
    dw_z1 = _mm(hn1, dz1, "tn", "mla_in_z_dw")
    dckv_a = _mm(dkn, w_kn, "nt", "kv_up_k_dx")
    dckv_b = _mm(dvv, w_v, "nt", "kv_up_v_dx")
    dw_kn = _mm(ckv, dkn, "tn", "kv_up_k_dw")
    dw_v = _mm(ckv, dvv, "tn", "kv_up_v_dw")
    dckv_raw, dkr_raw, dkvl = _rowwise_vjp("kv_latent_bwd", f_ckv, ckv_ins, [[dckv_a, dckv_b], [dkr]], [0, 1, 4],
                                           tr=TR_FULL)
    dhkv_a = _mm(dckv_raw, w_ckv, "nt", "kv_down_c_dx")
    dhkv_b = _mm(dkr_raw, w_kr, "nt", "kv_down_r_dx")
    dw_ckv = _mm(hkv, dckv_raw, "tn", "kv_down_c_dw")
    dw_kr = _mm(hkv, dkr_raw, "tn", "kv_down_r_dw")
    dh0_res, dy0, dpost0, dpre1, dkvn = _rowwise_vjp(
        "mid_bwd", f_mid, mid_ins, [[dh2], [dhn1_a, dhn1_b], [dhkv_a, dhkv_b]], [0, 1, 2, 3, 4])

    dgated0 = _mm(dy0, w_out0, "nt", "gdn_out_dx")
    dw_out0 = _mm(gated0, dy0, "tn", "gdn_out_dw")
    do0, dz0, doutn = _rowwise_vjp("gdn_gate_bwd", f_gate0, gate0_ins, [[dgated0]], [0, 1, 2], tr=TR_QUARTER)

    g_kv_down = jnp.concatenate([dw_ckv, dw_kr[:, :MLA_ROPE]], axis=1)
    g_kv_up = jnp.concatenate([dw_kn.reshape(MLA_KV_RANK, MLA_HEADS, LANE), dw_v.reshape(MLA_KV_RANK, MLA_HEADS, LANE)],
                              axis=2).reshape(MLA_KV_RANK, MLA_HEADS * 2 * LANE)
    g_w_in1 = jnp.concatenate([dw_cq, dw_z1], axis=1)
    g_qup = jnp.concatenate([dw_qn.reshape(MLA_Q_RANK, MLA_HEADS, LANE),
                             dw_qr.reshape(MLA_Q_RANK, MLA_HEADS, LANE)[:, :, :MLA_ROPE]],
                            axis=2).reshape(MLA_Q_RANK, MLA_HEADS * MLA_QK)
    g_mm = _pack_full([dw_out0, g_kv_down, g_kv_up, g_w_in1, g_qup, dw_out1], layout[2:])
    mm_chip_part = _add_pair(g_mm, _sibling_split(g_mm, "grads_sibling_split"), "grads_chip_sum")
    dq0, dk0, dv0, dbeta, dgc, mm_from_chips = _gdn_bwd(qkv, beta, gc, ckpt, do0, mm_chip_part, "gdn_scan_bwd")
    g_flat = _sibling_join(_sum_slots(mm_from_chips, "grads_total"), "grads_sibling_join")
    db_raw, da_raw, dalog, ddtb = _rowwise_vjp("gdn_gates_bwd", f_ba, ba_ins, [[dbeta], [dgc]], [0, 1, 2, 3])
    dqkv_raw, dconv = _conv_bwd(qkv_raw, conv_w, dq0, dk0, dv0, "gdn_conv_bwd")
    dhn0_a = _mm(dqkv_raw, w_qkv, "nt", "gdn_in_qkv_dx")
    dhn0_b = _mm(dz0, w_z0, "nt", "gdn_in_z_dx")
    dhn0_c = _mm(db_raw, w_b, "nt", "gdn_in_b_dx")
    dhn0_d = _mm(da_raw, w_a, "nt", "gdn_in_a_dx")
    dw_qkv = _mm(hn0, dqkv_raw, "tn", "gdn_in_qkv_dw")
    dw_z0 = _mm(hn0, dz0, "tn", "gdn_in_z_dw")
    dw_b = _mm(hn0, db_raw, "tn", "gdn_in_b_dw")
    dw_a = _mm(hn0, da_raw, "tn", "gdn_in_a_dw")
    dh0, dpre0 = _rowwise_vjp("pre0_bwd", f_pre0, [_In(h0), _In(pre0, "const")],
                              [[dhn0_a, dhn0_b, dhn0_c, dhn0_d], [dh0_res]], [0, 1])

    grad_x = dh0[ROW0:ROW0 + seq][None]
    g_meta = dh0[FRONT:ROW0]
    g_w_in0 = jnp.concatenate([dw_qkv, dw_z0, dw_b[:, :nv], dw_a[:, :nv]], axis=1)

    g_win_by_chip = jnp.concatenate([g_w_in0[None, :, s * win_cols:(s + 1) * win_cols] for s in range(4)], axis=0)
    win_chip_part = _add_pair(g_win_by_chip, _sibling_split(g_win_by_chip, "grads_sibling_split_gdn_w_in"),
                              "grads_chip_sum_gdn_w_in")
    win_from_chips = _chip_scatter(win_chip_part, "grads_chip_scatter_gdn_w_in")
    g_win = _sibling_join(_sum_slots(win_from_chips, "grads_total_gdn_w_in"), "grads_sibling_join_gdn_w_in")

    small_shapes = [(2, d), (2, d), (1, nv), (1, nv), (1, GDN_DK), (d,), (MLA_KV_RANK,), (1, MLA_Q_RANK),
                    g_meta.shape, dconv.shape, (1, LANE)]
    small_part = _pack_small([jnp.concatenate([dpre0, dpre1], axis=0), jnp.concatenate([dpost0, dpost1], axis=0),
                              dalog[:, :nv], ddtb[:, :nv], doutn, dkvn, dkvl, dqln, g_meta, dconv, loss_part])
    small_tot = _all_sum_small(small_part, "small_sum")
    small_g = _unpack_small(small_tot, small_shapes)
    loss = small_g[-1][0, 0]
    chip = 2 * lax.axis_index("x") + lax.axis_index("y")
    meta_cols, conv_cols = meta_tokens.shape[1], gdn_conv_w.shape[2]
    g_meta_shard = lax.dynamic_slice(small_g[8], (0, chip * meta_cols), (small_g[8].shape[0], meta_cols))
    g_conv_shard = lax.dynamic_slice(small_g[9], (0, chip * conv_cols), (small_g[9].shape[0], conv_cols))

    d_flat, m_flat, v_flat = _adamw(_pack_shards(big_w[2:]), g_flat, _pack_shards(big_m[2:]), _pack_shards(big_v[2:]),
                                    "adamw_sharded")
    win_step = _adamw(gdn_w_in[0], g_win, m_gdn_w_in[0], v_gdn_w_in[0], "adamw_gdn_w_in")
    small_names = ["pre_norm", "post_norm", "gdn_a_log", "gdn_dt_bias", "gdn_out_norm", "kv_norm", "kv_latent_norm",
                   "mla_q_latent_norm", "meta_tokens", "gdn_conv_w"]
    small_w = [pre_norm, post_norm, gdn_a_log, gdn_dt_bias, gdn_out_norm, kv_norm, kv_latent_norm, mla_q_latent_norm,
               meta_tokens, gdn_conv_w]
    small_m = [m_pre_norm, m_post_norm, m_gdn_a_log, m_gdn_dt_bias, m_gdn_out_norm, m_kv_norm, m_kv_latent_norm,
               m_mla_q_latent_norm, m_meta_tokens, m_gdn_conv_w]
    small_v = [v_pre_norm, v_post_norm, v_gdn_a_log, v_gdn_dt_bias, v_gdn_out_norm, v_kv_norm, v_kv_latent_norm,
               v_mla_q_latent_norm, v_meta_tokens, v_gdn_conv_w]
    g_small_flat = _pack_small(small_g[:8] + [g_meta_shard, g_conv_shard])
    ds_flat, ms_flat, vs_flat = _adamw(_pack_small(small_w), g_small_flat, _pack_small(small_m), _pack_small(small_v),
                                       "adamw_replicated")

    def assemble(big_flat, small_flat, win):
        bigs = dict(zip(big_names[2:], [a.reshape(w.shape) for a, w in zip(
            _unpack_shards(big_flat, layout[2:]),
            [gdn_w_out, kv_w_down, kv_w_up, mla_w_in, mla_w_q_up, mla_w_out])]))
        smalls = dict(zip(small_names, _unpack_small(small_flat, [w.shape for w in small_w])))
        both = {**bigs, **smalls, "gdn_w_in": win[None]}
        order = ["meta_tokens", "pre_norm", "post_norm", "gdn_w_in", "gdn_conv_w", "gdn_a_log", "gdn_dt_bias",
                 "gdn_out_norm", "gdn_w_out", "kv_norm", "kv_w_down", "kv_latent_norm", "kv_w_up", "mla_w_in",
                 "mla_q_latent_norm", "mla_w_q_up", "mla_w_out"]
        return [both[n] for n in order]

    grads = assemble(g_flat, g_small_flat, g_win)
    deltas = assemble(d_flat, ds_flat, win_step[0])
    new_m = assemble(m_flat, ms_flat, win_step[1])
    new_v = assemble(v_flat, vs_flat, win_step[2])
    return (loss, grad_x, *grads, *deltas, *new_m, *new_v)
```

```python
import functools
import math

import jax
import jax.numpy as jnp
from jax import lax
from jax.experimental import pallas as pl
from jax.experimental.pallas import tpu as pltpu

F32 = jnp.float32
BF16 = jnp.bfloat16
MESH = pl.DeviceIdType.MESH

D_MODEL = 1024
N_META = 16
FRONT = 48
ROW0 = FRONT + N_META
ROW_ALIGN = 768
TR_FULL, TR_HALF, TR_QUARTER = ROW_ALIGN, ROW_ALIGN // 2, ROW_ALIGN // 4
NORM_EPS = 1e-6
LANE = 128

GDN_QK_HEADS = 8
GDN_V_HEADS = 16
GDN_DK = 128
GDN_CHUNK = 64
GDN_QK_W = 1024
GDN_V_W = 2048
GDN_CONV_W = 4096

MLA_HEADS = 16
MLA_NOPE = 128
MLA_ROPE = 64
MLA_QK = 192
MLA_Q_RANK = 256
MLA_KV_RANK = 128
ROPE_THETA = 10000.0

ADAM_LR = 0.001
ADAM_B1 = 0.9
ADAM_B2 = 0.999
ADAM_EPS = 1e-08
ADAM_WD = 0.01
ADAM_STEP = 10

VMEM_LIMIT_V7X = 56 * 1024 * 1024
NEG = -1e30

_NN = ((1,), (0,))
_NT = ((1,), (1,))
_TN = ((0,), (0,))
_HI = lax.Precision.HIGHEST
_X3 = lax.Precision.HIGH


def _pcall(body, **kw):
    return pl.pallas_call(body, **kw)


def _params(n_axes):
    return pltpu.CompilerParams(dimension_semantics=("arbitrary",) * n_axes, vmem_limit_bytes=VMEM_LIMIT_V7X)


def _dot(a, b, dims, prec=None):
    return lax.dot_general(a, b, (dims, ((), ())), precision=prec, preferred_element_type=F32)


def _bdot(a, b, dims):
    return _dot(a.astype(BF16), b.astype(BF16), dims)


def _hdot(a, b, dims=_NN):
    return _dot(a, b, dims, _HI)


def _fdot(a, b, dims):
    return _dot(a, b, dims)


SMALL_MATMUL_DIM = 256
SMALL_MATMUL_ROWS = 1408


def _tile(n):
    if n % ROW_ALIGN == 0:
        return ROW_ALIGN
    for t in (1024, 512, 256, 128):
        if n % t == 0:
            return t
    raise ValueError(n)


def _mm(a, b, mode, name, out_dtype=F32):
    if mode == "nn":
        (m, k), (k2, n) = a.shape, b.shape
    elif mode == "nt":
        (m, k), (n, k2) = a.shape, b.shape
    else:
        (k, m), (k2, n) = a.shape, b.shape
    assert k == k2, (a.shape, b.shape, mode)
    tm, tn, tk = _tile(m), _tile(n), _tile(k)
    if mode != "tn" and min(k, n) <= SMALL_MATMUL_DIM and m % SMALL_MATMUL_ROWS == 0:
        tm = SMALL_MATMUL_ROWS
    nk = k // tk
    dims = {"nn": _NN, "nt": _NT, "tn": _TN}[mode]

    def body(a_ref, b_ref, o_ref, acc):
        kk = pl.program_id(2)

        @pl.when(kk == 0)
        def _():
            acc[...] = jnp.zeros_like(acc)

        acc[...] += _bdot(a_ref[...], b_ref[...], dims)

        @pl.when(kk == nk - 1)
        def _():
            o_ref[...] = acc[...].astype(out_dtype)

    if mode == "tn":
        a_spec = pl.BlockSpec((tk, tm), lambda i, j, kk: (kk, i))
    else:
        a_spec = pl.BlockSpec((tm, tk), lambda i, j, kk: (i, kk))
    if mode == "nt":
        b_spec = pl.BlockSpec((tn, tk), lambda i, j, kk: (j, kk))
    else:
        b_spec = pl.BlockSpec((tk, tn), lambda i, j, kk: (kk, j))
    return _pcall(
        body, name=name, grid=(m // tm, n // tn, nk),
        in_specs=[a_spec, b_spec],
        out_specs=pl.BlockSpec((tm, tn), lambda i, j, kk: (i, j)),
        out_shape=jax.ShapeDtypeStruct((m, n), out_dtype),
        scratch_shapes=[pltpu.VMEM((tm, tn), F32)],
        compiler_params=_params(3),
    )(a, b)


class _In:
    def __init__(self, arr, kind="row", grouped=False, goff=0):
        self.arr, self.kind, self.grouped, self.goff = arr, kind, grouped, goff


class _Out:
    def __init__(self, kind, shape, dtype=F32, grouped=False):
        self.kind, self.shape, self.dtype, self.grouped = kind, shape, dtype, grouped


def _rowwise(name, fn, ins, outs, *, groups=1, tr=TR_HALF):
    lp = next(i.arr.shape[0] for i in ins if i.kind == "row")
    nr = lp // tr
    assert lp % tr == 0

    def in_spec(i):
        w = i.arr.shape[1]
        if i.kind == "row":
            if i.grouped:
                return pl.BlockSpec((tr, LANE), lambda g, r, o=i.goff: (r, g + o))
            return pl.BlockSpec((tr, w), lambda g, r: (r, 0))
        if i.grouped:
            return pl.BlockSpec((i.arr.shape[0], LANE), lambda g, r, o=i.goff: (0, g + o))
        return pl.BlockSpec(i.arr.shape, lambda g, r: (0, 0))

    def out_spec(o):
        if o.kind == "row":
            if o.grouped:
                return pl.BlockSpec((tr, LANE), lambda g, r: (r, g))
            assert groups == 1
            return pl.BlockSpec((tr, o.shape[1]), lambda g, r: (r, 0))
        if o.grouped:
            return pl.BlockSpec((o.shape[0], LANE), lambda g, r: (0, g))
        return pl.BlockSpec(o.shape, lambda g, r: (0, 0))

    n_in = len(ins)

    def body(*refs):
        g = pl.program_id(0)
        r = pl.program_id(1)
        ridx = r * tr + lax.broadcasted_iota(jnp.int32, (tr, 1), 0)
        res = fn(ridx, g, *[ref[...] for ref in refs[:n_in]])
        assert len(res) == len(outs), (name, len(res), len(outs))
        for o, ref, val in zip(outs, refs[n_in:], res):
            if o.kind == "row":
                ref[...] = val.astype(o.dtype)
            else:
                first = (r == 0) if o.grouped else jnp.logical_and(r == 0, g == 0)

                @pl.when(first)
                def _(ref=ref, val=val):
                    ref[...] = val.astype(F32)

                @pl.when(jnp.logical_not(first))
                def _(ref=ref, val=val):
                    ref[...] += val.astype(F32)

    res = _pcall(
        body, name=name, grid=(groups, nr),
        in_specs=[in_spec(i) for i in ins],
        out_specs=[out_spec(o) for o in outs],
        out_shape=[jax.ShapeDtypeStruct(o.shape, o.dtype) for o in outs],
        compiler_params=_params(2),
    )(*[i.arr for i in ins])
    return res


def _rowwise_vjp(name, fn, ins, cots, diff, *, groups=1, tr=TR_HALF):
    n_in = len(ins)
    grouped = groups > 1
    cot_ins = []
    counts = []
    for arrs in cots:
        counts.append(len(arrs))
        for a in arrs:
            cot_ins.append(_In(a, "row", grouped=grouped and a.shape[1] > LANE))
    lp = next(i.arr.shape[0] for i in ins if i.kind == "row")
    outs = []
    for d in diff:
        i = ins[d]
        if i.kind == "row":
            w = groups * LANE if i.grouped else i.arr.shape[1]
            outs.append(_Out("row", (lp, w), F32, grouped=i.grouped))
        else:
            outs.append(_Out("acc", i.arr.shape, F32, grouped=i.grouped))

    def bfn(ridx, g, *allvals):
        vals = list(allvals[:n_in])
        cvals = allvals[n_in:]

        def f(*dv):
            full = list(vals)
            for i, v in zip(diff, dv):
                full[i] = v
            return tuple(fn(ridx, g, *full))

        primal, vjp = jax.vjp(f, *[vals[i].astype(F32) for i in diff])
        cts = []
        pos = 0
        for k, cnt in enumerate(counts):
            if cnt == 0:
                cts.append(jnp.zeros_like(primal[k]))
            else:
                c = cvals[pos].astype(F32)
                for extra in cvals[pos + 1:pos + cnt]:
                    c = c + extra.astype(F32)
                w = primal[k].shape[1]
                if c.shape[1] != w:
                    c = functools.reduce(jnp.add, [c[:, i * w:(i + 1) * w] for i in range(c.shape[1] // w)])
                cts.append(c.astype(primal[k].dtype))
            pos += cnt
        return vjp(tuple(cts))

    return _rowwise(name, bfn, list(ins) + cot_ins, outs, groups=groups, tr=tr)


def _rms(x, g):
    return x * lax.rsqrt(jnp.mean(x * x, axis=-1, keepdims=True) + NORM_EPS) * g


def _silu(x):
    return x * jax.nn.sigmoid(x)


def _softplus(x):
    return jnp.maximum(x, 0.0) + jnp.log(1.0 + jnp.exp(-jnp.abs(x)))


def _swap_halves(x):
    lane = lax.broadcasted_iota(jnp.int32, x.shape, x.ndim - 1)
    return jnp.where(lane < 32, pltpu.roll(x, LANE - 32, x.ndim - 1), pltpu.roll(x, 32, x.ndim - 1))


@jax.custom_vjp
def _rope(x, c, s):
    return x * c + _swap_halves(x) * s


def _rope_fwd(x, c, s):
    return _rope(x, c, s), (c, s)


def _rope_bwd(res, dy):
    c, s = res
    return dy * c + _swap_halves(dy * s), jnp.zeros_like(c), jnp.zeros_like(s)


_rope.defvjp(_rope_fwd, _rope_bwd)


def _conv_post(c, g):
    s = _silu(c)
    n = s * lax.rsqrt(jnp.sum(s * s, axis=-1, keepdims=True) + NORM_EPS)
    return jnp.where(g < GDN_QK_HEADS, n * (GDN_DK ** -0.5), jnp.where(g < 2 * GDN_QK_HEADS, n, s))


def _conv_taps(xe, w):
    c = xe[8:] * w[3]
    for s in (1, 2, 3):
        c = c + pltpu.roll(xe, s, 0)[8:] * w[3 - s]
    return c


CONV_LANES = 512
CONV_HEADS = CONV_LANES // LANE


def _conv_post_block(c, g):
    return jnp.concatenate([_conv_post(c[:, i * LANE:(i + 1) * LANE], g * CONV_HEADS + i)
                            for i in range(CONV_HEADS)], axis=1)


def _conv_fwd(x, w, name, tr=TR_FULL):
    lp, width = x.shape
    cl = CONV_LANES
    nr = lp // tr

    def body(x_ref, prev_ref, w_ref, o_ref):
        g = pl.program_id(0)
        r = pl.program_id(1)
        prev = jnp.where(r > 0, prev_ref[...], 0.0)
        xe = jnp.concatenate([prev, x_ref[...]], axis=0)
        o_ref[...] = _conv_post_block(_conv_taps(xe, [w_ref[t:t + 1, :] for t in range(4)]), g)

    return _pcall(
        body, name=name, grid=(width // cl, nr),
        in_specs=[pl.BlockSpec((tr, cl), lambda g, r: (r, g)),
                  pl.BlockSpec((8, cl), lambda g, r: (jnp.maximum(r * (tr // 8) - 1, 0), g)),
                  pl.BlockSpec((4, cl), lambda g, r: (0, g))],
        out_specs=pl.BlockSpec((tr, cl), lambda g, r: (r, g)),
        out_shape=jax.ShapeDtypeStruct((lp, width), F32),
        compiler_params=_params(2),
    )(x, x, w)


def _conv_bwd(x, w, dq, dk, dv, name, tr=TR_FULL):
    lp, width = x.shape
    cl = CONV_LANES
    nr = lp // tr
    last8 = lp // 8 - 1
    nq = GDN_QK_W // cl

    def body(x_ref, prev_ref, next_ref, w_ref, q_ref, k_ref, v_ref, q_n, k_n, v_n, dx_ref, dw_ref):
        g = pl.program_id(0)
        r = pl.program_id(1)
        w = [w_ref[t:t + 1, :] for t in range(4)]
        not_last = r < nr - 1

        def pick(a, b, c):
            return jnp.where(g < nq, a[...], jnp.where(g < 2 * nq, b[...], c[...]))

        dy = pick(q_ref, k_ref, v_ref)
        dyn = jnp.where(not_last, pick(q_n, k_n, v_n), 0.0)
        prev = jnp.where(r > 0, prev_ref[...], 0.0)
        nxt = jnp.where(not_last, next_ref[...], 0.0)
        xe = jnp.concatenate([prev, x_ref[...], nxt], axis=0)
        ce = _conv_taps(xe, w)
        _, vjp = jax.vjp(lambda c: _conv_post_block(c, g), ce)
        (dce,) = vjp(jnp.concatenate([dy, dyn], axis=0))
        n = tr + 8
        dx = dce * w[3]
        for s in (1, 2, 3):
            dx = dx + pltpu.roll(dce, n - s, 0) * w[3 - s]
        dx_ref[...] = dx[:tr]
        dc = dce[:tr]
        row4 = lax.broadcasted_iota(jnp.int32, (4, cl), 0)
        dw = jnp.zeros((4, cl), F32)
        for s in (0, 1, 2, 3):
            xs = xe[8:8 + tr] if s == 0 else pltpu.roll(xe, s, 0)[8:8 + tr]
            dw = dw + jnp.where(row4 == 3 - s, jnp.sum(dc * xs, axis=0, keepdims=True), 0.0)

        @pl.when(r == 0)
        def _():
            dw_ref[...] = dw

        @pl.when(r > 0)
        def _():
            dw_ref[...] += dw

    def col_q(g):
        return jnp.minimum(g, nq - 1), g < nq

    def col_k(g):
        return jnp.clip(g - nq, 0, nq - 1), jnp.logical_and(g >= nq, g < 2 * nq)

    def col_v(g):
        return jnp.maximum(g - 2 * nq, 0), g >= 2 * nq

    def blk(colf):
        def index(g, r):
            col, used = colf(g)
            return jnp.where(used, r, 0), col
        return pl.BlockSpec((tr, cl), index)

    def nblk(colf):
        def index(g, r):
            col, used = colf(g)
            return jnp.where(used, jnp.minimum((r + 1) * (tr // 8), last8), 0), col
        return pl.BlockSpec((8, cl), index)

    return _pcall(
        body, name=name, grid=(width // cl, nr),
        in_specs=[pl.BlockSpec((tr, cl), lambda g, r: (r, g)),
                  pl.BlockSpec((8, cl), lambda g, r: (jnp.maximum(r * (tr // 8) - 1, 0), g)),
                  pl.BlockSpec((8, cl), lambda g, r: (jnp.minimum((r + 1) * (tr // 8), last8), g)),
                  pl.BlockSpec((4, cl), lambda g, r: (0, g)),
                  blk(col_q), blk(col_k), blk(col_v), nblk(col_q), nblk(col_k), nblk(col_v)],
        out_specs=[pl.BlockSpec((tr, cl), lambda g, r: (r, g)),
                   pl.BlockSpec((4, cl), lambda g, r: (0, g))],
        out_shape=[jax.ShapeDtypeStruct((lp, width), F32), jax.ShapeDtypeStruct((4, width), F32)],
        compiler_params=_params(2),
    )(x, x, x, w, dq, dk, dv, dq, dk, dv)


def _bmm(a, b, dims, prec=None):
    (ca,), (cb,) = dims
    return lax.dot_general(a, b, (((ca + 1,), (cb + 1,)), ((0,), (0,))), precision=prec,
                           preferred_element_type=F32)


def _inv_impl(m):
    c = m.shape[-1]
    ii = lax.broadcasted_iota(jnp.int32, (c, c), 0)
    jj = lax.broadcasted_iota(jnp.int32, (c, c), 1)
    eye = (ii == jj).astype(F32)

    def same_block(shift):
        return (ii >> shift) == (jj >> shift)

    n1 = jnp.where(same_block(3), -m, 0.0)
    n2 = _bmm(n1, n1, _NN, _X3)
    n4 = _bmm(n2, n2, _NN, _X3)
    d = _bmm(_bmm(eye + n1, eye + n2, _NN, _X3), eye + n4, _NN, _X3)
    shift = 3
    while (1 << shift) < c:
        low = jnp.where(jnp.logical_and(same_block(shift + 1), jnp.logical_not(same_block(shift))), m, 0.0)
        d = d - _bmm(d, _bmm(low, d, _NN, _X3), _NN, _X3)
        shift += 1
    return d


@jax.custom_vjp
def _inv_unit_lower(m):
    return _inv_impl(m)


def _inv_f(m):
    t = _inv_impl(m)
    return t, t


def _inv_b(t, dt):
    c = t.shape[-1]
    ii = lax.broadcasted_iota(jnp.int32, (c, c), 0)
    jj = lax.broadcasted_iota(jnp.int32, (c, c), 1)
    gm = _bmm(t, _bmm(dt, t, _NT, _X3), _TN, _X3)
    return (jnp.where(ii > jj, -gm, 0.0),)


_inv_unit_lower.defvjp(_inv_f, _inv_b)


GDN_HEADS_PER_STEP = 16


def _gdn_group(q, k, v, beta_blk, gc_blk, states, h0):
    hp = GDN_HEADS_PER_STEP
    c = q.shape[0]
    lane = lax.broadcasted_iota(jnp.int32, (1, LANE), 1)
    row8 = lax.broadcasted_iota(jnp.int32, (max(8, hp), LANE), 0)
    lane8 = lax.broadcasted_iota(jnp.int32, (max(8, hp), LANE), 1)
    gcr_all = _hdot((lane8 == h0 + row8).astype(F32), gc_blk, _NT)
    betas, gccs = [], []
    for i in range(hp):
        onehot = (lane == h0 + i).astype(F32)
        betas.append(jnp.sum(beta_blk * onehot, axis=1, keepdims=True))
        gccs.append(jnp.sum(gc_blk * onehot, axis=1, keepdims=True))
    def stack(xs):
        return jnp.concatenate([x[None] for x in xs], axis=0)

    beta = stack(betas)
    gcc = stack(gccs)
    gcr = stack([gcr_all[i:i + 1] for i in range(hp)])
    qh = stack([q[:, (i // 2) * LANE:(i // 2 + 1) * LANE] for i in range(hp)])
    kh = stack([k[:, (i // 2) * LANE:(i // 2 + 1) * LANE] for i in range(hp)])
    vh = stack([v[:, i * LANE:(i + 1) * LANE] for i in range(hp)])
    state = stack(states)
    ii = lax.broadcasted_iota(jnp.int32, (c, c), 0)
    jj = lax.broadcasted_iota(jnp.int32, (c, c), 1)
    incl = ii >= jj
    dec = jnp.where(incl, jnp.exp(jnp.where(incl, gcc - gcr, 0.0)), 0.0)
    eg = jnp.exp(gcc)
    m = _bmm(kh, kh, _NT) * beta * jnp.where(ii > jj, dec, 0.0)
    t = _inv_unit_lower(m)
    u = _bmm(t, vh * beta, _NN, _X3)
    w = _bmm(t, kh * (beta * eg), _NN, _X3)
    attn = _bmm(qh, kh, _NT) * dec
    rows = lax.broadcasted_iota(jnp.int32, (c, 1), 0)
    gl = jnp.sum(jnp.where(rows == c - 1, gcc, 0.0), axis=1, keepdims=True)
    v_new = u - _bmm(w, state, _NN)
    o = _bmm(qh * eg, state, _NN) + _bmm(attn, v_new, _NN)
    new_state = state * jnp.exp(gl) + _bmm(kh * jnp.exp(gl - gcc), v_new, _TN)
    return jnp.concatenate([o[i] for i in range(hp)], axis=1), tuple(new_state[i] for i in range(hp))


def _gdn_specs(nc, rev):
    def cidx(n):
        return (nc - 1 - n) if rev else n
    hp = GDN_HEADS_PER_STEP
    nqk = GDN_QK_HEADS
    c = GDN_CHUNK
    nq = 2 * nqk // hp
    q_spec = pl.BlockSpec((c, hp // 2 * LANE), lambda n, g: (cidx(n), g))
    k_spec = pl.BlockSpec((c, hp // 2 * LANE), lambda n, g: (cidx(n), nq + g))
    v_spec = pl.BlockSpec((c, hp * LANE), lambda n, g: (cidx(n), nq + g))
    s_spec = pl.BlockSpec((c, LANE), lambda n, g: (cidx(n), 0))
    o_spec = pl.BlockSpec((c, hp * LANE), lambda n, g: (cidx(n), g))
    ck_spec = pl.BlockSpec((hp, 1, GDN_DK, LANE), lambda n, g: (g, cidx(n), 0, 0))
    return q_spec, k_spec, v_spec, s_spec, o_spec, ck_spec


def _gdn_fwd(qkv, beta, gc, shard, name):
    lp = qkv.shape[0]
    nc = lp // GDN_CHUNK
    nh = GDN_V_HEADS
    hp = GDN_HEADS_PER_STEP
    ng = nh // hp
    q_spec, k_spec, v_spec, s_spec, o_spec, ck_spec = _gdn_specs(nc, False)

    def body(q_ref, k_ref, v_ref, b_ref, g_ref, x_ref, o_ref, ck_ref, all_ref, state, send_sems, recv_sems, local_sem):
        n = pl.program_id(0)
        g = pl.program_id(1)
        x, y, c = _my_place()

        def local_copy():
            return pltpu.make_async_copy(x_ref, all_ref.at[2 * x + y], local_sem)

        def remote_copy(k, px, py, slot):
            return pltpu.make_async_remote_copy(
                src_ref=x_ref, dst_ref=all_ref.at[slot], send_sem=send_sems.at[k], recv_sem=recv_sems.at[k],
                device_id=(px, py, c), device_id_type=MESH)

        @pl.when(jnp.logical_and(n == 0, g == 0))
        def _():
            local_copy().start()
            for k, (px, py) in enumerate(_other_chips(x, y)):
                remote_copy(k, px, py, 2 * x + y).start()

        @pl.when(n == 0)
        def _():
            for i in range(hp):
                state[g * hp + i] = jnp.zeros((GDN_DK, LANE), F32)

        states = tuple(state[g * hp + i] for i in range(hp))
        for i in range(hp):
            ck_ref[i, 0] = states[i]
        o, new_states = _gdn_group(q_ref[...], k_ref[...], v_ref[...], b_ref[...], g_ref[...], states, g * hp)
        o_ref[...] = o
        for i in range(hp):
            state[g * hp + i] = new_states[i]

        @pl.when(jnp.logical_and(n == nc - 1, g == ng - 1))
        def _():
            for k, (px, py) in enumerate(_other_chips(x, y)):
                remote_copy(k, px, py, 2 * px + py).wait_recv()
            for k, (px, py) in enumerate(_other_chips(x, y)):
                remote_copy(k, px, py, 2 * x + y).wait_send()
            local_copy().wait()

    return _pcall(
        body, name=name, grid=(nc, ng),
        in_specs=[q_spec, k_spec, v_spec, s_spec, s_spec, _ANY],
        out_specs=[o_spec, ck_spec, _ANY],
        out_shape=[jax.ShapeDtypeStruct((lp, GDN_V_W), F32),
                   jax.ShapeDtypeStruct((nh, nc, GDN_DK, LANE), F32),
                   jax.ShapeDtypeStruct((4,) + shard.shape, shard.dtype)],
        scratch_shapes=[pltpu.VMEM((nh, GDN_DK, LANE), F32), pltpu.SemaphoreType.DMA((3,)),
                        pltpu.SemaphoreType.DMA((3,)), pltpu.SemaphoreType.DMA],
        compiler_params=_params(2),
    )(qkv, qkv, qkv, beta, gc, shard)


def _gdn_bwd(qkv, beta, gc, ckpt, do, parts, name):
    lp = qkv.shape[0]
    nc = lp // GDN_CHUNK
    nh = GDN_V_HEADS
    hp = GDN_HEADS_PER_STEP
    ng = nh // hp
    q_spec, k_spec, v_spec, s_spec, o_spec, ck_spec = _gdn_specs(nc, True)

    def body(q_ref, k_ref, v_ref, b_ref, g_ref, ck_ref, do_ref, p_ref,
             dq_ref, dk_ref, dv_ref, db_ref, dg_ref, from_ref, dstate, send_sems, recv_sems, local_sem):
        n = pl.program_id(0)
        g = pl.program_id(1)
        x, y, c = _my_place()
        me = 2 * x + y

        def local_copy():
            return pltpu.make_async_copy(p_ref.at[me], from_ref.at[me], local_sem)

        def remote_copy(k, px, py, src_slot, dst_slot):
            return pltpu.make_async_remote_copy(
                src_ref=p_ref.at[src_slot], dst_ref=from_ref.at[dst_slot], send_sem=send_sems.at[k],
                recv_sem=recv_sems.at[k], device_id=(px, py, c), device_id_type=MESH)

        @pl.when(jnp.logical_and(n == 0, g == 0))
        def _():
            local_copy().start()
            for k, (px, py) in enumerate(_other_chips(x, y)):
                remote_copy(k, px, py, 2 * px + py, me).start()

        @pl.when(jnp.logical_and(n == nc - 1, g == ng - 1))
        def _():
            for k, (px, py) in enumerate(_other_chips(x, y)):
                remote_copy(k, px, py, me, 2 * px + py).wait_recv()
            for k, (px, py) in enumerate(_other_chips(x, y)):
                remote_copy(k, px, py, 2 * px + py, me).wait_send()
            local_copy().wait()

        @pl.when(n == 0)
        def _():
            for i in range(hp):
                dstate[g * hp + i] = jnp.zeros((GDN_DK, LANE), F32)

        states = tuple(ck_ref[i, 0] for i in range(hp))
        _, vjp = jax.vjp(lambda q, k, v, b, gg, s: _gdn_group(q, k, v, b, gg, s, g * hp),
                         q_ref[...], k_ref[...], v_ref[...], b_ref[...], g_ref[...], states)
        dq, dk, dv, db, dg, ds = vjp((do_ref[...], tuple(dstate[g * hp + i] for i in range(hp))))
        dq_ref[...] = dq
        dk_ref[...] = dk
        dv_ref[...] = dv
        for i in range(hp):
            dstate[g * hp + i] = ds[i]

        @pl.when(g == 0)
        def _():
            db_ref[...] = db
            dg_ref[...] = dg

        @pl.when(g > 0)
        def _():
            db_ref[...] += db
            dg_ref[...] += dg

    qk_shape = jax.ShapeDtypeStruct((lp, GDN_QK_W), F32)
    big = jax.ShapeDtypeStruct((lp, GDN_V_W), F32)
    small = jax.ShapeDtypeStruct((lp, LANE), F32)
    dq_spec = pl.BlockSpec((GDN_CHUNK, hp // 2 * LANE), lambda n, g: (nc - 1 - n, g))
    return _pcall(
        body, name=name, grid=(nc, ng),
        in_specs=[q_spec, k_spec, v_spec, s_spec, s_spec, ck_spec, o_spec, _ANY],
        out_specs=[dq_spec, dq_spec, o_spec, s_spec, s_spec, _ANY],
        out_shape=[qk_shape, qk_shape, big, small, small, jax.ShapeDtypeStruct(parts.shape, parts.dtype)],
        scratch_shapes=[pltpu.VMEM((nh, GDN_DK, LANE), F32), pltpu.SemaphoreType.DMA((3,)),
                        pltpu.SemaphoreType.DMA((3,)), pltpu.SemaphoreType.DMA],
        compiler_params=_params(2),
    )(qkv, qkv, qkv, beta, gc, ckpt, do, parts)


LOG2E = 1.4426950408889634
LN2 = 0.6931471805599453
Q_PRESCALE = MLA_QK ** -0.5 * LOG2E


ATT_SUB = 128
ATT_HEADS_PER_STEP = 8
ATT_BWD_HEADS_PER_STEP = 2


def _att_mask(i, j, tb, transposed):
    r = lax.broadcasted_iota(jnp.int32, (tb, tb), 0)
    c = lax.broadcasted_iota(jnp.int32, (tb, tb), 1)
    qpos, kpos = (i * tb + c, j * tb + r) if transposed else (i * tb + r, j * tb + c)
    return jnp.logical_and(kpos <= qpos, kpos >= FRONT)


def _causal_pairs(nb, by_key):
    if by_key:
        pairs = [(i, j) for j in range(nb) for i in range(j, nb)]
    else:
        pairs = [(i, j) for i in range(nb) for j in range(i + 1)]
    return jnp.array([p[0] for p in pairs], jnp.int32), jnp.array([p[1] for p in pairs], jnp.int32)


def _masked_and_plain(i, j, step):
    edge = jnp.logical_or(j == i, j == 0)

    @pl.when(jnp.logical_and(edge, j <= i))
    def _():
        step(True)

    @pl.when(jnp.logical_and(jnp.logical_not(edge), j < i))
    def _():
        step(False)


def _cat(a_ref, b_ref):
    return jnp.concatenate([a_ref[...], b_ref[...]], axis=1)


def _flash_fwd(qn, qr, kn, kr, v, name, tb=ROW_ALIGN):
    lp = qn.shape[0]
    nb = lp // tb
    nh = MLA_HEADS
    hp = ATT_HEADS_PER_STEP
    qi, kj = _causal_pairs(nb, by_key=False)

    def body(qi_ref, kj_ref, qn_ref, qr_ref, kn_ref, kr_ref, v_ref, o_ref, lse_ref, m_s, l_s, acc):
        t = pl.program_id(1)
        i, j = qi_ref[t], kj_ref[t]

        @pl.when(j == 0)
        def _():
            m_s[...] = jnp.full_like(m_s, NEG)
            l_s[...] = jnp.zeros_like(l_s)
            acc[...] = jnp.zeros_like(acc)

        def step(masked):
            n_sub = tb // ATT_SUB
            kr = kr_ref[...]
            for e in range(hp):
                lanes = pl.ds(e * LANE, LANE)
                k = jnp.concatenate([kn_ref[:, lanes], kr], axis=1)
                v = v_ref[:, lanes]

                def scores(r, lanes=lanes, k=k):
                    rows = pl.ds(r * ATT_SUB, ATT_SUB)
                    return _dot(jnp.concatenate([qn_ref[rows, lanes], qr_ref[rows, lanes]], axis=1), k, _NT)

                s_next = scores(0)
                for r in range(n_sub):
                    s = s_next
                    if r + 1 < n_sub:
                        s_next = scores(r + 1)
                    rows = pl.ds(r * ATT_SUB, ATT_SUB)
                    if masked:
                        qpos = i * tb + r * ATT_SUB + lax.broadcasted_iota(jnp.int32, (ATT_SUB, tb), 0)
                        kpos = j * tb + lax.broadcasted_iota(jnp.int32, (ATT_SUB, tb), 1)
                        s = jnp.where(jnp.logical_and(kpos <= qpos, kpos >= FRONT), s, NEG)
                    m_old = m_s[e, rows, :]
                    m_new = jnp.maximum(m_old, jnp.max(s, axis=1, keepdims=True))
                    alpha = jnp.exp2(m_old - m_new)
                    p = jnp.exp2(s - m_new)
                    l_s[e, rows, :] = alpha * l_s[e, rows, :] + jnp.sum(p, axis=1, keepdims=True)
                    acc[e, rows, :] = alpha * acc[e, rows, :] + _dot(p.astype(BF16), v, _NN)
                    m_s[e, rows, :] = m_new

        _masked_and_plain(i, j, step)

        @pl.when(j == i)
        def _():
            for e in range(hp):
                lanes = pl.ds(e * LANE, LANE)
                o_ref[:, lanes] = acc[e] / l_s[e]
                lse_ref[:, lanes] = jnp.broadcast_to(m_s[e] + jnp.log(l_s[e]) * LOG2E, (tb, LANE))

    qspec = pl.BlockSpec((tb, hp * LANE), lambda h, t, qi_, kj_: (qi_[t], h))
    kspec = pl.BlockSpec((tb, hp * LANE), lambda h, t, qi_, kj_: (kj_[t], h))
    krspec = pl.BlockSpec((tb, LANE), lambda h, t, qi_, kj_: (kj_[t], 0))
    shp = jax.ShapeDtypeStruct((lp, nh * LANE), F32)
    return _pcall(
        body, name=name, out_shape=[shp, shp],
        grid_spec=pltpu.PrefetchScalarGridSpec(
            num_scalar_prefetch=2, grid=(nh // hp, qi.shape[0]),
            in_specs=[qspec, qspec, kspec, krspec, kspec], out_specs=[qspec, qspec],
            scratch_shapes=[pltpu.VMEM((hp, tb, 1), F32), pltpu.VMEM((hp, tb, 1), F32),
                            pltpu.VMEM((hp, tb, LANE), F32)]),
        compiler_params=_params(2),
    )(qi, kj, qn, qr, kn, kr, v)


def _flash_bwd(qn, qr, kn, kr, v, o, do, lse, name, tb=ROW_ALIGN):
    lp = qn.shape[0]
    nb = lp // tb
    nh = MLA_HEADS
    hp = ATT_BWD_HEADS_PER_STEP
    qi, kj = _causal_pairs(nb, by_key=True)
    n_pairs = qi.shape[0]
    knt, krt = kn.T, kr.T

    def body(qi_ref, kj_ref, qn_ref, qr_ref, kn_ref, kr_ref, knt_ref, krt_ref, v_ref, o_ref, do_ref, lse_ref,
             dqnt_hbm, dqrt_hbm, dkn_ref, dkr_ref, dv_ref, dk_acc, dv_acc, dqn_acc, dqr_acc, out_sems):
        g = pl.program_id(0)
        t = pl.program_id(1)
        i, j = qi_ref[t], kj_ref[t]

        @pl.when(t == 0)
        def _():
            dqn_acc[...] = jnp.zeros_like(dqn_acc)
            dqr_acc[...] = jnp.zeros_like(dqr_acc)

        @pl.when(i == j)
        def _():
            dk_acc[...] = jnp.zeros_like(dk_acc)
            dv_acc[...] = jnp.zeros_like(dv_acc)

        def step(masked):
            kr = kr_ref[...]
            krt_blk = krt_ref[...]
            lane = lax.broadcasted_iota(jnp.int32, (8, LANE), 1)
            for e in range(hp):
                lanes = pl.ds(e * LANE, LANE)
                q = jnp.concatenate([qn_ref[:, lanes], qr_ref[:, lanes]], axis=1)
                st = _dot(jnp.concatenate([kn_ref[:, lanes], kr], axis=1), q, _NT)
                if masked:
                    st = jnp.where(_att_mask(i, j, tb, True), st, NEG)
                do_blk = do_ref[:, lanes]
                lse_row = _hdot((lane == 0).astype(F32), lse_ref[:, lanes], _NT)[0:1]
                delta_row = _hdot(jnp.ones((8, LANE), F32), do_blk * o_ref[:, lanes], _NT)[0:1]
                pt = jnp.exp2(st - lse_row)
                do_b = do_blk.astype(BF16)
                dv_acc[e] += _dot(pt.astype(BF16), do_b, _NN)
                dpt = _dot(v_ref[:, lanes], do_b, _NT)
                dst = (pt * (dpt - delta_row)).astype(BF16)
                dk_acc[e] += _dot(dst, q, _NN)
                dqn_acc[e * nb + i] += _dot(knt_ref[pl.ds(e * LANE, LANE), :], dst, _NN) * LN2
                dqr_acc[e * nb + i] += _dot(krt_blk, dst, _NN) * LN2

        _masked_and_plain(i, j, step)

        @pl.when(i == nb - 1)
        def _():
            for e in range(hp):
                lanes = pl.ds(e * LANE, LANE)
                dkn_ref[:, lanes] = dk_acc[e, :, :LANE] * LN2
                dkr_ref[:, lanes] = dk_acc[e, :, LANE:] * LN2
                dv_ref[:, lanes] = dv_acc[e]

        @pl.when(t == n_pairs - 1)
        def _():
            dst_rows = pl.ds(g * (hp * nb), hp * nb)
            cn = pltpu.make_async_copy(dqn_acc, dqnt_hbm.at[dst_rows], out_sems.at[0])
            cr = pltpu.make_async_copy(dqr_acc, dqrt_hbm.at[dst_rows], out_sems.at[1])
            cn.start()
            cr.start()
            cn.wait()
            cr.wait()

    qspec = pl.BlockSpec((tb, hp * LANE), lambda h, t, qi_, kj_: (qi_[t], h))
    kspec = pl.BlockSpec((tb, hp * LANE), lambda h, t, qi_, kj_: (kj_[t], h))
    krspec = pl.BlockSpec((tb, LANE), lambda h, t, qi_, kj_: (kj_[t], 0))
    ktspec = pl.BlockSpec((hp * LANE, tb), lambda h, t, qi_, kj_: (h, kj_[t]))
    krtspec = pl.BlockSpec((LANE, tb), lambda h, t, qi_, kj_: (0, kj_[t]))
    shp = jax.ShapeDtypeStruct((lp, nh * LANE), F32)
    dqt_shape = jax.ShapeDtypeStruct((nh * nb, LANE, tb), F32)
    dqnt, dqrt, dkn, dkr, dv = _pcall(
        body, name=name, out_shape=[dqt_shape, dqt_shape, shp, shp, shp],
        grid_spec=pltpu.PrefetchScalarGridSpec(
            num_scalar_prefetch=2, grid=(nh // hp, n_pairs),
            in_specs=[qspec, qspec, kspec, krspec, ktspec, krtspec, kspec, qspec, qspec, qspec],
            out_specs=[_ANY, _ANY, kspec, kspec, kspec],
            scratch_shapes=[pltpu.VMEM((hp, tb, 2 * LANE), F32), pltpu.VMEM((hp, tb, LANE), F32),
                            pltpu.VMEM((hp * nb, LANE, tb), F32), pltpu.VMEM((hp * nb, LANE, tb), F32),
                            pltpu.SemaphoreType.DMA((2,))]),
        compiler_params=_params(2),
    )(qi, kj, qn, qr, kn, kr, knt, krt, v, o, do, lse)

    def rows_major(a):
        return a.reshape(nh, nb, LANE, tb).transpose(1, 3, 0, 2).reshape(lp, nh * LANE)

    return rows_major(dqnt), rows_major(dqrt), dkn, dkr, dv


ELEMENTWISE_BLOCK_BYTES = 1 << 20


def _row_tile(rows, width, copies=1):
    for t in (1024, 512, 256, 128, 64, 32, 16, 8):
        if rows % t == 0 and t * width * 4 * copies <= ELEMENTWISE_BLOCK_BYTES:
            return t
    return rows


def _adamw(w, g, m, v, name):
    rows, width = w.shape
    tr = _row_tile(rows, width)

    def body(w_ref, g_ref, m_ref, v_ref, d_ref, nm_ref, nv_ref):
        gg = g_ref[...]
        nm = ADAM_B1 * m_ref[...] + (1.0 - ADAM_B1) * gg
        nv = ADAM_B2 * v_ref[...] + (1.0 - ADAM_B2) * jnp.square(gg)
        m_hat = nm / (1.0 - ADAM_B1 ** ADAM_STEP)
        v_hat = nv / (1.0 - ADAM_B2 ** ADAM_STEP)
        d_ref[...] = -ADAM_LR * (m_hat / (jnp.sqrt(v_hat) + ADAM_EPS) + ADAM_WD * w_ref[...])
        nm_ref[...] = nm
        nv_ref[...] = nv

    spec = pl.BlockSpec((tr, width), lambda r: (r, 0))
    shp = jax.ShapeDtypeStruct((rows, width), F32)
    return _pcall(body, name=name, grid=(rows // tr,), in_specs=[spec] * 4, out_specs=[spec] * 3,
                  out_shape=[shp] * 3, compiler_params=_params(1))(w, g, m, v)


def _add_pair(a, b, name):
    s, rows, width = b.shape
    tr = _row_tile(rows, width)
    nt = rows // tr

    def body(c_ref, a_ref, b_ref, o_ref):
        o_ref[...] = a_ref[...] + b_ref[...]

    spec = pl.BlockSpec((1, tr, width), lambda i, r, c_ref: (i, r, 0))
    return _pcall(
        body, name=name, out_shape=jax.ShapeDtypeStruct(b.shape, F32),
        grid_spec=pltpu.PrefetchScalarGridSpec(
            num_scalar_prefetch=1, grid=(s, nt),
            in_specs=[pl.BlockSpec((1, tr, width), lambda i, r, c_ref: (i, c_ref[0] * nt + r, 0)), spec],
            out_specs=spec),
        compiler_params=_params(2),
    )(_core_index(), a, b)


def _sum_slots(a, name):
    s, rows, width = a.shape
    tr = _row_tile(rows, width, copies=s)

    def body(a_ref, o_ref):
        tot = a_ref[0]
        for k in range(1, s):
            tot = tot + a_ref[k]
        o_ref[...] = tot

    return _pcall(body, name=name, grid=(rows // tr,),
                  in_specs=[pl.BlockSpec((s, tr, width), lambda r: (0, r, 0))],
                  out_specs=pl.BlockSpec((tr, width), lambda r: (r, 0)),
                  out_shape=jax.ShapeDtypeStruct((rows, width), F32), compiler_params=_params(1))(a)


_ANY = pl.BlockSpec(memory_space=pl.ANY)


def _my_place():
    return lax.axis_index("x"), lax.axis_index("y"), lax.axis_index("c")


def _core_index():
    return lax.axis_index("c").astype(jnp.int32).reshape(1)


def _other_chips(x, y):
    return [(1 - x, y), (x, 1 - y), (1 - x, 1 - y)]


def _gather_shards(flat, name):
    rows, width = flat.shape

    def body(x_ref, out_ref, send_sems, recv_sems, local_sem):
        x, y, c = _my_place()
        mine = pltpu.make_async_copy(x_ref, out_ref.at[2 * x + y], local_sem)
        mine.start()
        sends = []
        for k, (px, py) in enumerate(_other_chips(x, y)):
            cp = pltpu.make_async_remote_copy(
                src_ref=x_ref, dst_ref=out_ref.at[2 * x + y], send_sem=send_sems.at[k], recv_sem=recv_sems.at[k],
                device_id=(px, py, c), device_id_type=MESH)
            cp.start()
            sends.append(cp)
        for k, (px, py) in enumerate(_other_chips(x, y)):
            pltpu.make_async_remote_copy(
                src_ref=x_ref, dst_ref=out_ref.at[2 * px + py], send_sem=send_sems.at[k], recv_sem=recv_sems.at[k],
                device_id=(px, py, c), device_id_type=MESH).wait_recv()
        for cp in sends:
            cp.wait_send()
        mine.wait()

    return _pcall(
        body, name=name, in_specs=[_ANY], out_specs=_ANY,
        out_shape=jax.ShapeDtypeStruct((4, rows, width), flat.dtype),
        scratch_shapes=[pltpu.SemaphoreType.DMA((3,)), pltpu.SemaphoreType.DMA((3,)), pltpu.SemaphoreType.DMA],
    )(flat)


def _sibling_split(g, name):
    s, rows, width = g.shape
    half = rows // 2
    tr = _row_tile(half, width)
    nt = half // tr

    def body(c_ref, g_blk, got_ref, send_sem, recv_sem):
        k = pl.program_id(0)
        t = pl.program_id(1)
        x, y, c = _my_place()
        cp = pltpu.make_async_remote_copy(
            src_ref=g_blk.at[0], dst_ref=got_ref.at[k, pl.ds(pl.multiple_of(t * tr, 8), tr), :],
            send_sem=send_sem, recv_sem=recv_sem, device_id=(x, y, 1 - c), device_id_type=MESH)
        cp.start()
        cp.wait_send()

        @pl.when(jnp.logical_and(k == s - 1, t == nt - 1))
        def _():
            pltpu.make_async_remote_copy(
                src_ref=got_ref, dst_ref=got_ref, send_sem=send_sem, recv_sem=recv_sem,
                device_id=(x, y, 1 - c), device_id_type=MESH).wait_recv()

    return _pcall(
        body, name=name, out_shape=jax.ShapeDtypeStruct((s, half, width), g.dtype),
        grid_spec=pltpu.PrefetchScalarGridSpec(
            num_scalar_prefetch=1, grid=(s, nt),
            in_specs=[pl.BlockSpec((1, tr, width), lambda k, t, c_ref: (k, (1 - c_ref[0]) * nt + t, 0))],
            out_specs=_ANY,
            scratch_shapes=[pltpu.SemaphoreType.DMA, pltpu.SemaphoreType.DMA]),
        compiler_params=_params(2),
    )(_core_index(), g)


def _chip_scatter(p, name):
    s, rows, width = p.shape

    def body(p_ref, out_ref, send_sems, recv_sems, local_sem):
        x, y, c = _my_place()
        me = 2 * x + y
        mine = pltpu.make_async_copy(p_ref.at[me], out_ref.at[me], local_sem)
        mine.start()
        sends = []
        for k, (px, py) in enumerate(_other_chips(x, y)):
            cp = pltpu.make_async_remote_copy(
                src_ref=p_ref.at[2 * px + py], dst_ref=out_ref.at[me], send_sem=send_sems.at[k],
                recv_sem=recv_sems.at[k], device_id=(px, py, c), device_id_type=MESH)
            cp.start()
            sends.append(cp)
        for k, (px, py) in enumerate(_other_chips(x, y)):
            pltpu.make_async_remote_copy(
                src_ref=p_ref.at[me], dst_ref=out_ref.at[2 * px + py], send_sem=send_sems.at[k],
                recv_sem=recv_sems.at[k], device_id=(px, py, c), device_id_type=MESH).wait_recv()
        for cp in sends:
            cp.wait_send()
        mine.wait()

    return _pcall(
        body, name=name, in_specs=[_ANY], out_specs=_ANY,
        out_shape=jax.ShapeDtypeStruct(p.shape, p.dtype),
        scratch_shapes=[pltpu.SemaphoreType.DMA((3,)), pltpu.SemaphoreType.DMA((3,)), pltpu.SemaphoreType.DMA],
    )(p)


def _sibling_join(qh, name):
    half, width = qh.shape
    tr = _row_tile(half, width)
    nt = half // tr

    def body(q_blk, out_ref, send_sem, recv_sem, local_sem):
        t = pl.program_id(0)
        x, y, c = _my_place()
        dst = out_ref.at[pl.ds(pl.multiple_of(c * half + t * tr, 8), tr), :]
        cp = pltpu.make_async_remote_copy(
            src_ref=q_blk, dst_ref=dst, send_sem=send_sem, recv_sem=recv_sem,
            device_id=(x, y, 1 - c), device_id_type=MESH)
        cp.start()
        mine = pltpu.make_async_copy(q_blk, dst, local_sem)
        mine.start()
        cp.wait_send()
        mine.wait()

        @pl.when(t == nt - 1)
        def _():
            theirs = out_ref.at[pl.ds(pl.multiple_of((1 - c) * half, 8), half), :]
            pltpu.make_async_remote_copy(
                src_ref=theirs, dst_ref=theirs, send_sem=send_sem, recv_sem=recv_sem,
                device_id=(x, y, 1 - c), device_id_type=MESH).wait_recv()

    return _pcall(
        body, name=name, grid=(nt,),
        in_specs=[pl.BlockSpec((tr, width), lambda t: (t, 0))], out_specs=_ANY,
        out_shape=jax.ShapeDtypeStruct((2 * half, width), qh.dtype),
        scratch_shapes=[pltpu.SemaphoreType.DMA, pltpu.SemaphoreType.DMA, pltpu.SemaphoreType.DMA],
        compiler_params=_params(1),
    )(qh)


def _all_sum_small(part, name):
    rows, width = part.shape

    def body(p_ref, out_ref, land, send_sems, recv_sems):
        x, y, c = _my_place()
        me = 4 * x + 2 * y + c
        land[me] = p_ref[...]
        sends = []
        for k in range(1, 8):
            peer = (x ^ (k >> 2), y ^ ((k >> 1) & 1), c ^ (k & 1))
            cp = pltpu.make_async_remote_copy(
                src_ref=p_ref, dst_ref=land.at[me], send_sem=send_sems.at[k - 1], recv_sem=recv_sems.at[k - 1],
                device_id=peer, device_id_type=MESH)
            cp.start()
            sends.append(cp)
        for k in range(1, 8):
            px, py, pc = x ^ (k >> 2), y ^ ((k >> 1) & 1), c ^ (k & 1)
            pltpu.make_async_remote_copy(
                src_ref=p_ref, dst_ref=land.at[4 * px + 2 * py + pc], send_sem=send_sems.at[k - 1],
                recv_sem=recv_sems.at[k - 1], device_id=(px, py, pc), device_id_type=MESH).wait_recv()
        for cp in sends:
            cp.wait_send()
        tot = land[0]
        for k in range(1, 8):
            tot = tot + land[k]
        out_ref[...] = tot

    vmem = pl.BlockSpec(memory_space=pltpu.VMEM)
    return _pcall(
        body, name=name, in_specs=[vmem], out_specs=vmem,
        out_shape=jax.ShapeDtypeStruct((rows, width), F32),
        scratch_shapes=[pltpu.VMEM((8, rows, width), F32), pltpu.SemaphoreType.DMA((7,)),
                        pltpu.SemaphoreType.DMA((7,))],
    )(part)


def _big_layout(shards):
    return [(a.shape[0], a.shape[1], ax) for a, ax in shards]


FLAT_ROW_MULTIPLE = 2048


def _pack_shards(arrs, row_multiple=FLAT_ROW_MULTIPLE):
    flat = jnp.concatenate([a.reshape(-1) for a in arrs])
    return jnp.pad(flat, (0, -flat.shape[0] % (row_multiple * LANE))).reshape(-1, LANE)


def _unpack_shards(flat, layout):
    flat = flat.reshape(-1)
    out, off = [], 0
    for r, c, _ in layout:
        out.append(flat[off:off + r * c].reshape(r, c))
        off += r * c
    return out


def _unpack_full(gathered, layout):
    g = gathered.reshape(4, -1)
    out, off = [], 0
    for r, c, ax in layout:
        seg = g[:, off:off + r * c].reshape(4, r, c)
        out.append(seg.transpose(1, 0, 2).reshape(r, 4 * c) if ax == 1 else seg.reshape(4 * r, c))
        off += r * c
    return out


def _pack_full(fulls, layout):
    parts = []
    for a, (r, c, ax) in zip(fulls, layout):
        if ax == 1:
            parts.append(a.reshape(r, 4, c).transpose(1, 0, 2).reshape(4, r * c))
        else:
            parts.append(a.reshape(4, r * c))
    flat = jnp.concatenate(parts, axis=1)
    return jnp.pad(flat, ((0, 0), (0, -flat.shape[1] % (FLAT_ROW_MULTIPLE * LANE)))).reshape(4, -1, LANE)


def _pad_lanes(a, width=LANE):
    return jnp.pad(a, [(0, 0)] * (a.ndim - 1) + [(0, width - a.shape[-1])])


def _pack_small(arrs):
    rows = [_pad_lanes(a.reshape(1, -1), -(-a.size // LANE) * LANE).reshape(-1, LANE) for a in arrs]
    flat = jnp.concatenate(rows, axis=0)
    return jnp.pad(flat, ((0, -flat.shape[0] % 8), (0, 0)))


def _unpack_small(flat, shapes):
    out, off = [], 0
    for shp in shapes:
        n = math.prod(shp)
        nr = -(-n // LANE)
        out.append(flat[off:off + nr].reshape(-1)[:n].reshape(shp))
        off += nr
    return out


def kernel(x, meta_tokens, pre_norm, post_norm, gdn_w_in, gdn_conv_w, gdn_a_log, gdn_dt_bias, gdn_out_norm, gdn_w_out, kv_norm, kv_w_down, kv_latent_norm, kv_w_up, mla_w_in, mla_q_latent_norm, mla_w_q_up, mla_w_out, loss_target, m_meta_tokens, m_pre_norm, m_post_norm, m_gdn_w_in, m_gdn_conv_w, m_gdn_a_log, m_gdn_dt_bias, m_gdn_out_norm, m_gdn_w_out, m_kv_norm, m_kv_w_down, m_kv_latent_norm, m_kv_w_up, m_mla_w_in, m_mla_q_latent_norm, m_mla_w_q_up, m_mla_w_out, v_meta_tokens, v_pre_norm, v_post_norm, v_gdn_w_in, v_gdn_conv_w, v_gdn_a_log, v_gdn_dt_bias, v_gdn_out_norm, v_gdn_w_out, v_kv_norm, v_kv_w_down, v_kv_latent_norm, v_kv_w_up, v_mla_w_in, v_mla_q_latent_norm, v_mla_w_q_up, v_mla_w_out):
    seq = x.shape[1]
    d = D_MODEL
    lp = -(-(ROW0 + seq) // ROW_ALIGN) * ROW_ALIGN
    tail = lp - ROW0 - seq

    big_names = ["meta_tokens", "gdn_conv_w", "gdn_w_out", "kv_w_down", "kv_w_up", "mla_w_in", "mla_w_q_up",
                 "mla_w_out"]
    big_axis = [1, 1, 0, 0, 1, 1, 1, 0]
    big_w = [meta_tokens, gdn_conv_w[0], gdn_w_out[0], kv_w_down, kv_w_up, mla_w_in[0], mla_w_q_up[0], mla_w_out[0]]
    big_m = [m_meta_tokens, m_gdn_conv_w[0], m_gdn_w_out[0], m_kv_w_down, m_kv_w_up, m_mla_w_in[0], m_mla_w_q_up[0],
             m_mla_w_out[0]]
    big_v = [v_meta_tokens, v_gdn_conv_w[0], v_gdn_w_out[0], v_kv_w_down, v_kv_w_up, v_mla_w_in[0], v_mla_w_q_up[0],
             v_mla_w_out[0]]
    layout = _big_layout(list(zip(big_w, big_axis)))
    meta_f, conv_w = _unpack_full(
        _gather_shards(_pack_shards(big_w[:2], row_multiple=16), "gather_meta_conv"), layout[:2])
    mm_shards = [w.astype(BF16) for w in big_w[2:6]] + [(big_w[6] * Q_PRESCALE).astype(BF16), big_w[7].astype(BF16)]
    mm_flat = _pack_shards(mm_shards)
    w_in0_shards = _gather_shards(gdn_w_in[0].astype(BF16), "gather_gdn_w_in")
    w_in0 = jnp.concatenate([w_in0_shards[s] for s in range(4)], axis=1)
    win_cols = gdn_w_in.shape[2]

    nv = GDN_V_HEADS
    w_qkv = w_in0[:, :GDN_CONV_W]
    w_z0 = w_in0[:, GDN_CONV_W:GDN_CONV_W + GDN_V_W]
    w_b = _pad_lanes(w_in0[:, GDN_CONV_W + GDN_V_W:GDN_CONV_W + GDN_V_W + nv])
    w_a = _pad_lanes(w_in0[:, GDN_CONV_W + GDN_V_W + nv:])

    pre0, pre1 = pre_norm[0:1], pre_norm[1:2]
    post0, post1 = post_norm[0:1], post_norm[1:2]
    a_log = _pad_lanes(gdn_a_log)
    dt_bias = _pad_lanes(gdn_dt_bias)
    kvn = kv_norm.reshape(1, d)
    kvl = kv_latent_norm.reshape(1, MLA_KV_RANK)
    qln = mla_q_latent_norm

    h0 = jnp.concatenate([jnp.zeros((FRONT, d), F32), meta_f, x[0], jnp.zeros((tail, d), F32)], axis=0)
    tgt = jnp.pad(loss_target[0], ((ROW0, tail), (0, 0)))
    pos = jnp.maximum(jnp.arange(lp, dtype=jnp.int32) - FRONT, 0).astype(F32)
    inv = ROPE_THETA ** (-jnp.arange(0, MLA_ROPE, 2, dtype=F32) / MLA_ROPE)
    ang = pos[:, None] * inv[None, :]
    zeros64 = jnp.zeros((lp, LANE - MLA_ROPE), F32)
    cos_t = jnp.concatenate([jnp.cos(ang), jnp.cos(ang), zeros64], axis=1)
    sin_t = jnp.concatenate([-jnp.sin(ang), jnp.sin(ang), zeros64], axis=1)

    def valid_rows(ridx):
        return jnp.logical_and(ridx >= FRONT, ridx < ROW0 + seq)

    def f_pre0(ridx, g, h, gain):
        return _rms(h, gain), h

    (hn0,) = _rowwise("pre0", lambda *a: f_pre0(*a)[:1], [_In(h0), _In(pre0, "const")],
                      [_Out("row", (lp, d), BF16)])
    qkv_raw = _mm(hn0, w_qkv, "nn", "gdn_in_qkv")
    z0 = _mm(hn0, w_z0, "nn", "gdn_in_z")
    b_raw = _mm(hn0, w_b, "nn", "gdn_in_b")
    a_raw = _mm(hn0, w_a, "nn", "gdn_in_a")

    def f_ba(ridx, g, b, a, alog, dtb):
        tr = b.shape[0]
        ok = valid_rows(ridx).astype(F32)
        beta = jax.nn.sigmoid(b) * ok
        gate = -jnp.exp(alog) * _softplus(a + dtb) * ok
        ii = lax.broadcasted_iota(jnp.int32, (tr, tr), 0)
        jj = lax.broadcasted_iota(jnp.int32, (tr, tr), 1)
        shift = GDN_CHUNK.bit_length() - 1
        tri = jnp.logical_and((ii >> shift) == (jj >> shift), ii >= jj).astype(F32)
        return beta, _hdot(tri, gate)

    ba_ins = [_In(b_raw), _In(a_raw), _In(a_log, "const"), _In(dt_bias, "const")]
    beta, gc = _rowwise("gdn_gates", f_ba, ba_ins, [_Out("row", (lp, LANE)), _Out("row", (lp, LANE))])
    qkv = _conv_fwd(qkv_raw, conv_w, "gdn_conv")
    o0, ckpt, mm_all = _gdn_fwd(qkv, beta, gc, mm_flat, "gdn_scan")
    (w_out0, kv_down, kv_up, w_in1, w_qup, w_out1) = _unpack_full(mm_all, layout[2:])
    w_ckv = kv_down[:, :MLA_KV_RANK]
    w_kr = _pad_lanes(kv_down[:, MLA_KV_RANK:])
    kvu = kv_up.reshape(MLA_KV_RANK, MLA_HEADS, 2 * LANE)
    w_kn = kvu[:, :, :LANE].reshape(MLA_KV_RANK, MLA_HEADS * LANE)
    w_v = kvu[:, :, LANE:].reshape(MLA_KV_RANK, MLA_HEADS * LANE)
    w_cq = w_in1[:, :MLA_Q_RANK]
    w_z1 = w_in1[:, MLA_Q_RANK:]
    qu = w_qup.reshape(MLA_Q_RANK, MLA_HEADS, MLA_QK)
    w_qn = qu[:, :, :MLA_NOPE].reshape(MLA_Q_RANK, MLA_HEADS * LANE)
    w_qr = _pad_lanes(qu[:, :, MLA_NOPE:]).reshape(MLA_Q_RANK, MLA_HEADS * LANE)

    def per_head(fn, *arrs):
        n = arrs[0].shape[1] // LANE
        return jnp.concatenate([fn(*[a[:, i * LANE:(i + 1) * LANE] for a in arrs]) for i in range(n)], axis=1)

    def f_gate0(ridx, g, o, z, gain):
        return (per_head(lambda oh, zh: _rms(oh, gain) * _silu(zh), o, z),)

    gate0_ins = [_In(o0), _In(z0), _In(gdn_out_norm, "const")]
    (gated0,) = _rowwise("gdn_gate", f_gate0, gate0_ins, [_Out("row", (lp, GDN_V_W), BF16)])
    y0 = _mm(gated0, w_out0, "nn", "gdn_out")

    def f_mid(ridx, g, h, y, g_post, g_pre, g_kv):
        h1 = h + _rms(y, g_post)
        return h1, _rms(h1, g_pre), _rms(h1, g_kv)

    mid_ins = [_In(h0), _In(y0), _In(post0, "const"), _In(pre1, "const"), _In(kvn, "const")]
    h1, hn1, hkv = _rowwise("mid", f_mid, mid_ins,
                            [_Out("row", (lp, d)), _Out("row", (lp, d), BF16), _Out("row", (lp, d), BF16)])

    ckv_raw = _mm(hkv, w_ckv, "nn", "kv_down_c")
    kr_raw = _mm(hkv, w_kr, "nn", "kv_down_r")

    def f_ckv(ridx, g, c, r, cs, sn, gain):
        return _rms(c, gain), _rope(r, cs, sn)

    ckv_ins = [_In(ckv_raw), _In(kr_raw), _In(cos_t), _In(sin_t), _In(kvl, "const")]
    ckv, kr = _rowwise("kv_latent", f_ckv, ckv_ins, [_Out("row", (lp, LANE)), _Out("row", (lp, LANE), BF16)],
                       tr=TR_FULL)
    kn = _mm(ckv, w_kn, "nn", "kv_up_k", BF16)
    vv = _mm(ckv, w_v, "nn", "kv_up_v", BF16)
    cq_raw = _mm(hn1, w_cq, "nn", "mla_in_q")
    z1 = _mm(hn1, w_z1, "nn", "mla_in_z")

    def f_cq(ridx, g, c, gain):
        return (_rms(c, gain),)

    cq_ins = [_In(cq_raw), _In(qln, "const")]
    (cq,) = _rowwise("q_latent", f_cq, cq_ins, [_Out("row", (lp, MLA_Q_RANK))], tr=TR_FULL)
    qn = _mm(cq, w_qn, "nn", "q_up_n", BF16)
    qr_raw = _mm(cq, w_qr, "nn", "q_up_r")

    def f_qrope(ridx, g, r, cs, sn):
        return (per_head(lambda rh: _rope(rh, cs, sn), r),)

    qr_ins = [_In(qr_raw), _In(cos_t), _In(sin_t)]
    (qr,) = _rowwise("q_rope", f_qrope, qr_ins, [_Out("row", (lp, MLA_HEADS * LANE), BF16)])
    o1, lse = _flash_fwd(qn, qr, kn, kr, vv, "attention")

    def f_gate1(ridx, g, o, z):
        return (o * _silu(z),)

    gate1_ins = [_In(o1), _In(z1)]
    (og,) = _rowwise("mla_gate", f_gate1, gate1_ins, [_Out("row", (lp, MLA_HEADS * LANE), BF16)])
    y1 = _mm(og, w_out1, "nn", "mla_out")

    def f_final(ridx, g, h, y, t, gain):
        ok = jnp.logical_and(ridx >= ROW0, ridx < ROW0 + seq).astype(F32)

        def rows_loss(h_, y_, gain_):
            err = (h_ + _rms(y_, gain_) - t) * ok
            return 0.5 * jnp.sum(jnp.sum(err * err, axis=1, keepdims=True), axis=0, keepdims=True) / d

        val, vjp = jax.vjp(rows_loss, h, y, gain)
        dh, dy, dgain = vjp(jnp.ones((1, 1), F32))
        return dh, dy, dgain, jnp.broadcast_to(val, (1, LANE))

    dh2, dy1, dpost1, loss_part = _rowwise(
        "loss_head", f_final, [_In(h1), _In(y1), _In(tgt), _In(post1, "const")],
        [_Out("row", (lp, d)), _Out("row", (lp, d)), _Out("acc", (1, d)), _Out("acc", (1, LANE))])

    dog = _mm(dy1, w_out1, "nt", "mla_out_dx")
    dw_out1 = _mm(og, dy1, "tn", "mla_out_dw")
    do1, dz1 = _rowwise_vjp("mla_gate_bwd", f_gate1, gate1_ins, [[dog]], [0, 1])
    dqn, dqr, dkn, dkr, dvv = _flash_bwd(qn, qr, kn, kr, vv, o1, do1, lse, "attention_bwd")
    (dqr_raw,) = _rowwise_vjp("q_rope_bwd", f_qrope, qr_ins, [[dqr]], [0])
    dcq_a = _mm(dqn, w_qn, "nt", "q_up_n_dx")
    dcq_b = _mm(dqr_raw, w_qr, "nt", "q_up_r_dx")
    dw_qn = _mm(cq, dqn, "tn", "q_up_n_dw") * Q_PRESCALE
    dw_qr = _mm(cq, dqr_raw, "tn", "q_up_r_dw") * Q_PRESCALE
    dcq_raw, dqln = _rowwise_vjp("q_latent_bwd", f_cq, cq_ins, [[dcq_a, dcq_b]], [0, 1], tr=TR_FULL)
    dhn1_a = _mm(dcq_raw, w_cq, "nt", "mla_in_q_dx")
    dhn1_b = _mm(dz1, w_z1, "nt", "mla_in_z_dx")
    dw_cq = _mm(hn1, dcq_raw, "tn", "mla_in_q_dw")
    dw_z1 = _mm(hn1, dz1, "tn", "mla_in_z_dw")
    dckv_a = _mm(dkn, w_kn, "nt", "kv_up_k_dx")
    dckv_b = _mm(dvv, w_v, "nt", "kv_up_v_dx")
    dw_kn = _mm(ckv, dkn, "tn", "kv_up_k_dw")
    dw_v = _mm(ckv, dvv, "tn", "kv_up_v_dw")
    dckv_raw, dkr_raw, dkvl = _rowwise_vjp("kv_latent_bwd", f_ckv, ckv_ins, [[dckv_a, dckv_b], [dkr]], [0, 1, 4],
                                           tr=TR_FULL)
    dhkv_a = _mm(dckv_raw, w_ckv, "nt", "kv_down_c_dx")
    dhkv_b = _mm(dkr_raw, w_kr, "nt", "kv_down_r_dx")
    dw_ckv = _mm(hkv, dckv_raw, "tn", "kv_down_c_dw")
    dw_kr = _mm(hkv, dkr_raw, "tn", "kv_down_r_dw")
    dh0_res, dy0, dpost0, dpre1, dkvn = _rowwise_vjp(
        "mid_bwd", f_mid, mid_ins, [[dh2], [dhn1_a, dhn1_b], [dhkv_a, dhkv_b]], [0, 1, 2, 3, 4])

    dgated0 = _mm(dy0, w_out0, "nt", "gdn_out_dx")
    dw_out0 = _mm(gated0, dy0, "tn", "gdn_out_dw")
    do0, dz0, doutn = _rowwise_vjp("gdn_gate_bwd", f_gate0, gate0_ins, [[dgated0]], [0, 1, 2], tr=TR_QUARTER)

    g_kv_down = jnp.concatenate([dw_ckv, dw_kr[:, :MLA_ROPE]], axis=1)
    g_kv_up = jnp.concatenate([dw_kn.reshape(MLA_KV_RANK, MLA_HEADS, LANE), dw_v.reshape(MLA_KV_RANK, MLA_HEADS, LANE)],
                              axis=2).reshape(MLA_KV_RANK, MLA_HEADS * 2 * LANE)
    g_w_in1 = jnp.concatenate([dw_cq, dw_z1], axis=1)
    g_qup = jnp.concatenate([dw_qn.reshape(MLA_Q_RANK, MLA_HEADS, LANE),
                             dw_qr.reshape(MLA_Q_RANK, MLA_HEADS, LANE)[:, :, :MLA_ROPE]],
                            axis=2).reshape(MLA_Q_RANK, MLA_HEADS * MLA_QK)
    g_mm = _pack_full([dw_out0, g_kv_down, g_kv_up, g_w_in1, g_qup, dw_out1], layout[2:])
    mm_chip_part = _add_pair(g_mm, _sibling_split(g_mm, "grads_sibling_split"), "grads_chip_sum")
    dq0, dk0, dv0, dbeta, dgc, mm_from_chips = _gdn_bwd(qkv, beta, gc, ckpt, do0, mm_chip_part, "gdn_scan_bwd")
    g_flat = _sibling_join(_sum_slots(mm_from_chips, "grads_total"), "grads_sibling_join")
    db_raw, da_raw, dalog, ddtb = _rowwise_vjp("gdn_gates_bwd", f_ba, ba_ins, [[dbeta], [dgc]], [0, 1, 2, 3])
    dqkv_raw, dconv = _conv_bwd(qkv_raw, conv_w, dq0, dk0, dv0, "gdn_conv_bwd")
    dhn0_a = _mm(dqkv_raw, w_qkv, "nt", "gdn_in_qkv_dx")
    dhn0_b = _mm(dz0, w_z0, "nt", "gdn_in_z_dx")
    dhn0_c = _mm(db_raw, w_b, "nt", "gdn_in_b_dx")
    dhn0_d = _mm(da_raw, w_a, "nt", "gdn_in_a_dx")
    dw_qkv = _mm(hn0, dqkv_raw, "tn", "gdn_in_qkv_dw")
    dw_z0 = _mm(hn0, dz0, "tn", "gdn_in_z_dw")
    dw_b = _mm(hn0, db_raw, "tn", "gdn_in_b_dw")
    dw_a = _mm(hn0, da_raw, "tn", "gdn_in_a_dw")
    dh0, dpre0 = _rowwise_vjp("pre0_bwd", f_pre0, [_In(h0), _In(pre0, "const")],
                              [[dhn0_a, dhn0_b, dhn0_c, dhn0_d], [dh0_res]], [0, 1])

    grad_x = dh0[ROW0:ROW0 + seq][None]
    g_meta = dh0[FRONT:ROW0]
    g_w_in0 = jnp.concatenate([dw_qkv, dw_z0, dw_b[:, :nv], dw_a[:, :nv]], axis=1)

    g_win_by_chip = jnp.concatenate([g_w_in0[None, :, s * win_cols:(s + 1) * win_cols] for s in range(4)], axis=0)
    win_chip_part = _add_pair(g_win_by_chip, _sibling_split(g_win_by_chip, "grads_sibling_split_gdn_w_in"),
                              "grads_chip_sum_gdn_w_in")
    win_from_chips = _chip_scatter(win_chip_part, "grads_chip_scatter_gdn_w_in")
    g_win = _sibling_join(_sum_slots(win_from_chips, "grads_total_gdn_w_in"), "grads_sibling_join_gdn_w_in")

    small_shapes = [(2, d), (2, d), (1, nv), (1, nv), (1, GDN_DK), (d,), (MLA_KV_RANK,), (1, MLA_Q_RANK),
                    g_meta.shape, dconv.shape, (1, LANE)]
    small_part = _pack_small([jnp.concatenate([dpre0, dpre1], axis=0), jnp.concatenate([dpost0, dpost1], axis=0),
                              dalog[:, :nv], ddtb[:, :nv], doutn, dkvn, dkvl, dqln, g_meta, dconv, loss_part])
    small_tot = _all_sum_small(small_part, "small_sum")
    small_g = _unpack_small(small_tot, small_shapes)
    loss = small_g[-1][0, 0]
    chip = 2 * lax.axis_index("x") + lax.axis_index("y")
    meta_cols, conv_cols = meta_tokens.shape[1], gdn_conv_w.shape[2]
    g_meta_shard = lax.dynamic_slice(small_g[8], (0, chip * meta_cols), (small_g[8].shape[0], meta_cols))
    g_conv_shard = lax.dynamic_slice(small_g[9], (0, chip * conv_cols), (small_g[9].shape[0], conv_cols))

    d_flat, m_flat, v_flat = _adamw(_pack_shards(big_w[2:]), g_flat, _pack_shards(big_m[2:]), _pack_shards(big_v[2:]),
                                    "adamw_sharded")
    win_step = _adamw(gdn_w_in[0], g_win, m_gdn_w_in[0], v_gdn_w_in[0], "adamw_gdn_w_in")
    small_names = ["pre_norm", "post_norm", "gdn_a_log", "gdn_dt_bias", "gdn_out_norm", "kv_norm", "kv_latent_norm",
                   "mla_q_latent_norm", "meta_tokens", "gdn_conv_w"]
    small_w = [pre_norm, post_norm, gdn_a_log, gdn_dt_bias, gdn_out_norm, kv_norm, kv_latent_norm, mla_q_latent_norm,
               meta_tokens, gdn_conv_w]
    small_m = [m_pre_norm, m_post_norm, m_gdn_a_log, m_gdn_dt_bias, m_gdn_out_norm, m_kv_norm, m_kv_latent_norm,
               m_mla_q_latent_norm, m_meta_tokens, m_gdn_conv_w]
    small_v = [v_pre_norm, v_post_norm, v_gdn_a_log, v_gdn_dt_bias, v_gdn_out_norm, v_kv_norm, v_kv_latent_norm,
               v_mla_q_latent_norm, v_meta_tokens, v_gdn_conv_w]
    g_small_flat = _pack_small(small_g[:8] + [g_meta_shard, g_conv_shard])
    ds_flat, ms_flat, vs_flat = _adamw(_pack_small(small_w), g_small_flat, _pack_small(small_m), _pack_small(small_v),
                                       "adamw_replicated")

    def assemble(big_flat, small_flat, win):
        bigs = dict(zip(big_names[2:], [a.reshape(w.shape) for a, w in zip(
            _unpack_shards(big_flat, layout[2:]),
            [gdn_w_out, kv_w_down, kv_w_up, mla_w_in, mla_w_q_up, mla_w_out])]))
        smalls = dict(zip(small_names, _unpack_small(small_flat, [w.shape for w in small_w])))
        both = {**bigs, **smalls, "gdn_w_in": win[None]}
        order = ["meta_tokens", "pre_norm", "post_norm", "gdn_w_in", "gdn_conv_w", "gdn_a_log", "gdn_dt_bias",
                 "gdn_out_norm", "gdn_w_out", "kv_norm", "kv_w_down", "kv_latent_norm", "kv_w_up", "mla_w_in",
                 "mla_q_latent_norm", "mla_w_q_up", "mla_w_out"]
        return [both[n] for n in order]

    grads = assemble(g_flat, g_small_flat, g_win)
    deltas = assemble(d_flat, ds_flat, win_step[0])
    new_m = assemble(m_flat, ms_flat, win_step[1])
    new_v = assemble(v_flat, vs_flat, win_step[2])
    return (loss, grad_x, *grads, *deltas, *new_m, *new_v)
```

```python
import functools
import math

import jax
import jax.numpy as jnp
from jax import lax
from jax.experimental import pallas as pl
from jax.experimental.pallas import tpu as pltpu

F32 = jnp.float32
BF16 = jnp.bfloat16
MESH = pl.DeviceIdType.MESH

D_MODEL = 1024
N_META = 16
FRONT = 48
ROW0 = FRONT + N_META
ROW_ALIGN = 768
TR_FULL, TR_HALF, TR_QUARTER = ROW_ALIGN, ROW_ALIGN // 2, ROW_ALIGN // 4
NORM_EPS = 1e-6
LANE = 128

GDN_QK_HEADS = 8
GDN_V_HEADS = 16
GDN_DK = 128
GDN_CHUNK = 64
GDN_QK_W = 1024
GDN_V_W = 2048
GDN_CONV_W = 4096

MLA_HEADS = 16
MLA_NOPE = 128
MLA_ROPE = 64
MLA_QK = 192
MLA_Q_RANK = 256
MLA_KV_RANK = 128
ROPE_THETA = 10000.0

ADAM_LR = 0.001
ADAM_B1 = 0.9
ADAM_B2 = 0.999
ADAM_EPS = 1e-08
ADAM_WD = 0.01
ADAM_STEP = 10

VMEM_LIMIT_V7X = 56 * 1024 * 1024
NEG = -1e30

_NN = ((1,), (0,))
_NT = ((1,), (1,))
_TN = ((0,), (0,))
_HI = lax.Precision.HIGHEST
_X3 = lax.Precision.HIGH


def _pcall(body, **kw):
    return pl.pallas_call(body, **kw)


def _params(n_axes):
    return pltpu.CompilerParams(dimension_semantics=("arbitrary",) * n_axes, vmem_limit_bytes=VMEM_LIMIT_V7X)


def _dot(a, b, dims, prec=None):
    return lax.dot_general(a, b, (dims, ((), ())), precision=prec, preferred_element_type=F32)


def _bdot(a, b, dims):
    return _dot(a.astype(BF16), b.astype(BF16), dims)


def _hdot(a, b, dims=_NN):
    return _dot(a, b, dims, _HI)


def _fdot(a, b, dims):
    return _dot(a, b, dims)


SMALL_MATMUL_DIM = 256
SMALL_MATMUL_ROWS = 1408


def _tile(n):
    if n % ROW_ALIGN == 0:
        return ROW_ALIGN
    for t in (1024, 512, 256, 128):
        if n % t == 0:
            return t
    raise ValueError(n)


def _mm(a, b, mode, name, out_dtype=F32):
    if mode == "nn":
        (m, k), (k2, n) = a.shape, b.shape
    elif mode == "nt":
        (m, k), (n, k2) = a.shape, b.shape
    else:
        (k, m), (k2, n) = a.shape, b.shape
    assert k == k2, (a.shape, b.shape, mode)
    tm, tn, tk = _tile(m), _tile(n), _tile(k)
    if mode != "tn" and min(k, n) <= SMALL_MATMUL_DIM and m % SMALL_MATMUL_ROWS == 0:
        tm = SMALL_MATMUL_ROWS
    nk = k // tk
    dims = {"nn": _NN, "nt": _NT, "tn": _TN}[mode]

    def body(a_ref, b_ref, o_ref, acc):
        kk = pl.program_id(2)

        @pl.when(kk == 0)
        def _():
            acc[...] = jnp.zeros_like(acc)

        acc[...] += _bdot(a_ref[...], b_ref[...], dims)

        @pl.when(kk == nk - 1)
        def _():
            o_ref[...] = acc[...].astype(out_dtype)

    if mode == "tn":
        a_spec = pl.BlockSpec((tk, tm), lambda i, j, kk: (kk, i))
    else:
        a_spec = pl.BlockSpec((tm, tk), lambda i, j, kk: (i, kk))
    if mode == "nt":
        b_spec = pl.BlockSpec((tn, tk), lambda i, j, kk: (j, kk))
    else:
        b_spec = pl.BlockSpec((tk, tn), lambda i, j, kk: (kk, j))
    return _pcall(
        body, name=name, grid=(m // tm, n // tn, nk),
        in_specs=[a_spec, b_spec],
        out_specs=pl.BlockSpec((tm, tn), lambda i, j, kk: (i, j)),
        out_shape=jax.ShapeDtypeStruct((m, n), out_dtype),
        scratch_shapes=[pltpu.VMEM((tm, tn), F32)],
        compiler_params=_params(3),
    )(a, b)


class _In:
    def __init__(self, arr, kind="row", grouped=False, goff=0):
        self.arr, self.kind, self.grouped, self.goff = arr, kind, grouped, goff


class _Out:
    def __init__(self, kind, shape, dtype=F32, grouped=False):
        self.kind, self.shape, self.dtype, self.grouped = kind, shape, dtype, grouped


def _rowwise(name, fn, ins, outs, *, groups=1, tr=TR_HALF):
    lp = next(i.arr.shape[0] for i in ins if i.kind == "row")
    nr = lp // tr
    assert lp % tr == 0

    def in_spec(i):
        w = i.arr.shape[1]
        if i.kind == "row":
            if i.grouped:
                return pl.BlockSpec((tr, LANE), lambda g, r, o=i.goff: (r, g + o))
            return pl.BlockSpec((tr, w), lambda g, r: (r, 0))
        if i.grouped:
            return pl.BlockSpec((i.arr.shape[0], LANE), lambda g, r, o=i.goff: (0, g + o))
        return pl.BlockSpec(i.arr.shape, lambda g, r: (0, 0))

    def out_spec(o):
        if o.kind == "row":
            if o.grouped:
                return pl.BlockSpec((tr, LANE), lambda g, r: (r, g))
            assert groups == 1
            return pl.BlockSpec((tr, o.shape[1]), lambda g, r: (r, 0))
        if o.grouped:
            return pl.BlockSpec((o.shape[0], LANE), lambda g, r: (0, g))
        return pl.BlockSpec(o.shape, lambda g, r: (0, 0))

    n_in = len(ins)

    def body(*refs):
        g = pl.program_id(0)
        r = pl.program_id(1)
        ridx = r * tr + lax.broadcasted_iota(jnp.int32, (tr, 1), 0)
        res = fn(ridx, g, *[ref[...] for ref in refs[:n_in]])
        assert len(res) == len(outs), (name, len(res), len(outs))
        for o, ref, val in zip(outs, refs[n_in:], res):
            if o.kind == "row":
                ref[...] = val.astype(o.dtype)
            else:
                first = (r == 0) if o.grouped else jnp.logical_and(r == 0, g == 0)

                @pl.when(first)
                def _(ref=ref, val=val):
                    ref[...] = val.astype(F32)

                @pl.when(jnp.logical_not(first))
                def _(ref=ref, val=val):
                    ref[...] += val.astype(F32)

    res = _pcall(
        body, name=name, grid=(groups, nr),
        in_specs=[in_spec(i) for i in ins],
        out_specs=[out_spec(o) for o in outs],
        out_shape=[jax.ShapeDtypeStruct(o.shape, o.dtype) for o in outs],
        compiler_params=_params(2),
    )(*[i.arr for i in ins])
    return res


def _rowwise_vjp(name, fn, ins, cots, diff, *, groups=1, tr=TR_HALF):
    n_in = len(ins)
    grouped = groups > 1
    cot_ins = []
    counts = []
    for arrs in cots:
        counts.append(len(arrs))
        for a in arrs:
            cot_ins.append(_In(a, "row", grouped=grouped and a.shape[1] > LANE))
    lp = next(i.arr.shape[0] for i in ins if i.kind == "row")
    outs = []
    for d in diff:
        i = ins[d]
        if i.kind == "row":
            w = groups * LANE if i.grouped else i.arr.shape[1]
            outs.append(_Out("row", (lp, w), F32, grouped=i.grouped))
        else:
            outs.append(_Out("acc", i.arr.shape, F32, grouped=i.grouped))

    def bfn(ridx, g, *allvals):
        vals = list(allvals[:n_in])
        cvals = allvals[n_in:]

        def f(*dv):
            full = list(vals)
            for i, v in zip(diff, dv):
                full[i] = v
            return tuple(fn(ridx, g, *full))

        primal, vjp = jax.vjp(f, *[vals[i].astype(F32) for i in diff])
        cts = []
        pos = 0
        for k, cnt in enumerate(counts):
            if cnt == 0:
                cts.append(jnp.zeros_like(primal[k]))
            else:
                c = cvals[pos].astype(F32)
                for extra in cvals[pos + 1:pos + cnt]:
                    c = c + extra.astype(F32)
                w = primal[k].shape[1]
                if c.shape[1] != w:
                    c = functools.reduce(jnp.add, [c[:, i * w:(i + 1) * w] for i in range(c.shape[1] // w)])
                cts.append(c.astype(primal[k].dtype))
            pos += cnt
        return vjp(tuple(cts))

    return _rowwise(name, bfn, list(ins) + cot_ins, outs, groups=groups, tr=tr)


def _rms(x, g):
    return x * lax.rsqrt(jnp.mean(x * x, axis=-1, keepdims=True) + NORM_EPS) * g


def _silu(x):
    return x * jax.nn.sigmoid(x)


def _softplus(x):
    return jnp.maximum(x, 0.0) + jnp.log(1.0 + jnp.exp(-jnp.abs(x)))


def _swap_halves(x):
    lane = lax.broadcasted_iota(jnp.int32, x.shape, x.ndim - 1)
    return jnp.where(lane < 32, pltpu.roll(x, LANE - 32, x.ndim - 1), pltpu.roll(x, 32, x.ndim - 1))


@jax.custom_vjp
def _rope(x, c, s):
    return x * c + _swap_halves(x) * s


def _rope_fwd(x, c, s):
    return _rope(x, c, s), (c, s)


def _rope_bwd(res, dy):
    c, s = res
    return dy * c + _swap_halves(dy * s), jnp.zeros_like(c), jnp.zeros_like(s)


_rope.defvjp(_rope_fwd, _rope_bwd)


def _conv_post(c, g):
    s = _silu(c)
    n = s * lax.rsqrt(jnp.sum(s * s, axis=-1, keepdims=True) + NORM_EPS)
    return jnp.where(g < GDN_QK_HEADS, n * (GDN_DK ** -0.5), jnp.where(g < 2 * GDN_QK_HEADS, n, s))


def _conv_taps(xe, w):
    c = xe[8:] * w[3]
    for s in (1, 2, 3):
        c = c + pltpu.roll(xe, s, 0)[8:] * w[3 - s]
    return c


CONV_LANES = 512
CONV_HEADS = CONV_LANES // LANE


def _conv_post_block(c, g):
    return jnp.concatenate([_conv_post(c[:, i * LANE:(i + 1) * LANE], g * CONV_HEADS + i)
                            for i in range(CONV_HEADS)], axis=1)


def _conv_fwd(x, w, name, tr=TR_FULL):
    lp, width = x.shape
    cl = CONV_LANES
    nr = lp // tr

    def body(x_ref, prev_ref, w_ref, o_ref):
        g = pl.program_id(0)
        r = pl.program_id(1)
        prev = jnp.where(r > 0, prev_ref[...], 0.0)
        xe = jnp.concatenate([prev, x_ref[...]], axis=0)
        o_ref[...] = _conv_post_block(_conv_taps(xe, [w_ref[t:t + 1, :] for t in range(4)]), g)

    return _pcall(
        body, name=name, grid=(width // cl, nr),
        in_specs=[pl.BlockSpec((tr, cl), lambda g, r: (r, g)),
                  pl.BlockSpec((8, cl), lambda g, r: (jnp.maximum(r * (tr // 8) - 1, 0), g)),
                  pl.BlockSpec((4, cl), lambda g, r: (0, g))],
        out_specs=pl.BlockSpec((tr, cl), lambda g, r: (r, g)),
        out_shape=jax.ShapeDtypeStruct((lp, width), F32),
        compiler_params=_params(2),
    )(x, x, w)


def _conv_bwd(x, w, dq, dk, dv, name, tr=TR_FULL):
    lp, width = x.shape
    cl = CONV_LANES
    nr = lp // tr
    last8 = lp // 8 - 1
    nq = GDN_QK_W // cl

    def body(x_ref, prev_ref, next_ref, w_ref, q_ref, k_ref, v_ref, q_n, k_n, v_n, dx_ref, dw_ref):
        g = pl.program_id(0)
        r = pl.program_id(1)
        w = [w_ref[t:t + 1, :] for t in range(4)]
        not_last = r < nr - 1

        def pick(a, b, c):
            return jnp.where(g < nq, a[...], jnp.where(g < 2 * nq, b[...], c[...]))

        dy = pick(q_ref, k_ref, v_ref)
        dyn = jnp.where(not_last, pick(q_n, k_n, v_n), 0.0)
        prev = jnp.where(r > 0, prev_ref[...], 0.0)
        nxt = jnp.where(not_last, next_ref[...], 0.0)
        xe = jnp.concatenate([prev, x_ref[...], nxt], axis=0)
        ce = _conv_taps(xe, w)
        _, vjp = jax.vjp(lambda c: _conv_post_block(c, g), ce)
        (dce,) = vjp(jnp.concatenate([dy, dyn], axis=0))
        n = tr + 8
        dx = dce * w[3]
        for s in (1, 2, 3):
            dx = dx + pltpu.roll(dce, n - s, 0) * w[3 - s]
        dx_ref[...] = dx[:tr]
        dc = dce[:tr]
        row4 = lax.broadcasted_iota(jnp.int32, (4, cl), 0)
        dw = jnp.zeros((4, cl), F32)
        for s in (0, 1, 2, 3):
            xs = xe[8:8 + tr] if s == 0 else pltpu.roll(xe, s, 0)[8:8 + tr]
            dw = dw + jnp.where(row4 == 3 - s, jnp.sum(dc * xs, axis=0, keepdims=True), 0.0)

        @pl.when(r == 0)
        def _():
            dw_ref[...] = dw

        @pl.when(r > 0)
        def _():
            dw_ref[...] += dw

    def col_q(g):
        return jnp.minimum(g, nq - 1), g < nq

    def col_k(g):
        return jnp.clip(g - nq, 0, nq - 1), jnp.logical_and(g >= nq, g < 2 * nq)

    def col_v(g):
        return jnp.maximum(g - 2 * nq, 0), g >= 2 * nq

    def blk(colf):
        def index(g, r):
            col, used = colf(g)
            return jnp.where(used, r, 0), col
        return pl.BlockSpec((tr, cl), index)

    def nblk(colf):
        def index(g, r):
            col, used = colf(g)
            return jnp.where(used, jnp.minimum((r + 1) * (tr // 8), last8), 0), col
        return pl.BlockSpec((8, cl), index)

    return _pcall(
        body, name=name, grid=(width // cl, nr),
        in_specs=[pl.BlockSpec((tr, cl), lambda g, r: (r, g)),
                  pl.BlockSpec((8, cl), lambda g, r: (jnp.maximum(r * (tr // 8) - 1, 0), g)),
                  pl.BlockSpec((8, cl), lambda g, r: (jnp.minimum((r + 1) * (tr // 8), last8), g)),
                  pl.BlockSpec((4, cl), lambda g, r: (0, g)),
                  blk(col_q), blk(col_k), blk(col_v), nblk(col_q), nblk(col_k), nblk(col_v)],
        out_specs=[pl.BlockSpec((tr, cl), lambda g, r: (r, g)),
                   pl.BlockSpec((4, cl), lambda g, r: (0, g))],
        out_shape=[jax.ShapeDtypeStruct((lp, width), F32), jax.ShapeDtypeStruct((4, width), F32)],
        compiler_params=_params(2),
    )(x, x, x, w, dq, dk, dv, dq, dk, dv)


def _bmm(a, b, dims, prec=None):
    (ca,), (cb,) = dims
    return lax.dot_general(a, b, (((ca + 1,), (cb + 1,)), ((0,), (0,))), precision=prec,
                           preferred_element_type=F32)


def _inv_impl(m):
    c = m.shape[-1]
    ii = lax.broadcasted_iota(jnp.int32, (c, c), 0)
    jj = lax.broadcasted_iota(jnp.int32, (c, c), 1)
    eye = (ii == jj).astype(F32)

    def same_block(shift):
        return (ii >> shift) == (jj >> shift)

    n1 = jnp.where(same_block(3), -m, 0.0)
    n2 = _bmm(n1, n1, _NN, _X3)
    n4 = _bmm(n2, n2, _NN, _X3)
    d = _bmm(_bmm(eye + n1, eye + n2, _NN, _X3), eye + n4, _NN, _X3)
    shift = 3
    while (1 << shift) < c:
        low = jnp.where(jnp.logical_and(same_block(shift + 1), jnp.logical_not(same_block(shift))), m, 0.0)
        d = d - _bmm(d, _bmm(low, d, _NN, _X3), _NN, _X3)
        shift += 1
    return d


@jax.custom_vjp
def _inv_unit_lower(m):
    return _inv_impl(m)


def _inv_f(m):
    t = _inv_impl(m)
    return t, t


def _inv_b(t, dt):
    c = t.shape[-1]
    ii = lax.broadcasted_iota(jnp.int32, (c, c), 0)
    jj = lax.broadcasted_iota(jnp.int32, (c, c), 1)
    gm = _bmm(t, _bmm(dt, t, _NT, _X3), _TN, _X3)
    return (jnp.where(ii > jj, -gm, 0.0),)


_inv_unit_lower.defvjp(_inv_f, _inv_b)


@jax.custom_vjp
def _inv_known(m, t):
    return t


def _inv_known_f(m, t):
    return t, t


def _inv_known_b(t, dt):
    return _inv_b(t, dt) + (jnp.zeros_like(t),)


_inv_known.defvjp(_inv_known_f, _inv_known_b)


GDN_HEADS_PER_STEP = 16


def _gdn_group(q, k, v, beta_blk, gc_blk, states, h0, t_known=None):
    hp = GDN_HEADS_PER_STEP
    c = q.shape[0]
    lane = lax.broadcasted_iota(jnp.int32, (1, LANE), 1)
    row8 = lax.broadcasted_iota(jnp.int32, (max(8, hp), LANE), 0)
    lane8 = lax.broadcasted_iota(jnp.int32, (max(8, hp), LANE), 1)
    gcr_all = _hdot((lane8 == h0 + row8).astype(F32), gc_blk, _NT)
    betas, gccs = [], []
    for i in range(hp):
        onehot = (lane == h0 + i).astype(F32)
        betas.append(jnp.sum(beta_blk * onehot, axis=1, keepdims=True))
        gccs.append(jnp.sum(gc_blk * onehot, axis=1, keepdims=True))
    def stack(xs):
        return jnp.concatenate([x[None] for x in xs], axis=0)

    beta = stack(betas)
    gcc = stack(gccs)
    gcr = stack([gcr_all[i:i + 1] for i in range(hp)])
    qh = stack([q[:, (i // 2) * LANE:(i // 2 + 1) * LANE] for i in range(hp)])
    kh = stack([k[:, (i // 2) * LANE:(i // 2 + 1) * LANE] for i in range(hp)])
    vh = stack([v[:, i * LANE:(i + 1) * LANE] for i in range(hp)])
    state = stack(states)
    ii = lax.broadcasted_iota(jnp.int32, (c, c), 0)
    jj = lax.broadcasted_iota(jnp.int32, (c, c), 1)
    incl = ii >= jj
    dec = jnp.where(incl, jnp.exp(jnp.where(incl, gcc - gcr, 0.0)), 0.0)
    eg = jnp.exp(gcc)
    m = _bmm(kh, kh, _NT) * beta * jnp.where(ii > jj, dec, 0.0)
    t = _inv_unit_lower(m) if t_known is None else _inv_known(m, t_known)
    u = _bmm(t, vh * beta, _NN, _X3)
    w = _bmm(t, kh * (beta * eg), _NN, _X3)
    attn = _bmm(qh, kh, _NT) * dec
    rows = lax.broadcasted_iota(jnp.int32, (c, 1), 0)
    gl = jnp.sum(jnp.where(rows == c - 1, gcc, 0.0), axis=1, keepdims=True)
    v_new = u - _bmm(w, state, _NN)
    o = _bmm(qh * eg, state, _NN) + _bmm(attn, v_new, _NN)
    new_state = state * jnp.exp(gl) + _bmm(kh * jnp.exp(gl - gcc), v_new, _TN)
    return jnp.concatenate([o[i] for i in range(hp)], axis=1), tuple(new_state[i] for i in range(hp)), t


def _gdn_specs(nc, rev):
    def cidx(n):
        return (nc - 1 - n) if rev else n
    hp = GDN_HEADS_PER_STEP
    nqk = GDN_QK_HEADS
    c = GDN_CHUNK
    nq = 2 * nqk // hp
    q_spec = pl.BlockSpec((c, hp // 2 * LANE), lambda n, g: (cidx(n), g))
    k_spec = pl.BlockSpec((c, hp // 2 * LANE), lambda n, g: (cidx(n), nq + g))
    v_spec = pl.BlockSpec((c, hp * LANE), lambda n, g: (cidx(n), nq + g))
    s_spec = pl.BlockSpec((c, LANE), lambda n, g: (cidx(n), 0))
    o_spec = pl.BlockSpec((c, hp * LANE), lambda n, g: (cidx(n), g))
    ck_spec = pl.BlockSpec((hp, 1, GDN_DK, LANE), lambda n, g: (g, cidx(n), 0, 0))
    return q_spec, k_spec, v_spec, s_spec, o_spec, ck_spec


def _gdn_t_spec(nc, rev):
    return pl.BlockSpec((GDN_HEADS_PER_STEP, 1, GDN_CHUNK, GDN_CHUNK),
                        lambda n, g: (g, (nc - 1 - n) if rev else n, 0, 0))


def _gdn_fwd(qkv, beta, gc, shard, name):
    lp = qkv.shape[0]
    nc = lp // GDN_CHUNK
    nh = GDN_V_HEADS
    hp = GDN_HEADS_PER_STEP
    ng = nh // hp
    q_spec, k_spec, v_spec, s_spec, o_spec, ck_spec = _gdn_specs(nc, False)

    def body(q_ref, k_ref, v_ref, b_ref, g_ref, x_ref, o_ref, ck_ref, t_ref, all_ref,
             state, send_sems, recv_sems, local_sem):
        n = pl.program_id(0)
        g = pl.program_id(1)
        x, y, c = _my_place()

        def local_copy():
            return pltpu.make_async_copy(x_ref, all_ref.at[2 * x + y], local_sem)

        def remote_copy(k, px, py, slot):
            return pltpu.make_async_remote_copy(
                src_ref=x_ref, dst_ref=all_ref.at[slot], send_sem=send_sems.at[k], recv_sem=recv_sems.at[k],
                device_id=(px, py, c), device_id_type=MESH)

        @pl.when(jnp.logical_and(n == 0, g == 0))
        def _():
            local_copy().start()
            for k, (px, py) in enumerate(_other_chips(x, y)):
                remote_copy(k, px, py, 2 * x + y).start()

        @pl.when(n == 0)
        def _():
            for i in range(hp):
                state[g * hp + i] = jnp.zeros((GDN_DK, LANE), F32)

        states = tuple(state[g * hp + i] for i in range(hp))
        for i in range(hp):
            ck_ref[i, 0] = states[i]
        o, new_states, t = _gdn_group(q_ref[...], k_ref[...], v_ref[...], b_ref[...], g_ref[...], states, g * hp)
        o_ref[...] = o
        t_ref[:, 0] = t
        for i in range(hp):
            state[g * hp + i] = new_states[i]

        @pl.when(jnp.logical_and(n == nc - 1, g == ng - 1))
        def _():
            for k, (px, py) in enumerate(_other_chips(x, y)):
                remote_copy(k, px, py, 2 * px + py).wait_recv()
            for k, (px, py) in enumerate(_other_chips(x, y)):
                remote_copy(k, px, py, 2 * x + y).wait_send()
            local_copy().wait()

    return _pcall(
        body, name=name, grid=(nc, ng),
        in_specs=[q_spec, k_spec, v_spec, s_spec, s_spec, _ANY],
        out_specs=[o_spec, ck_spec, _gdn_t_spec(nc, False), _ANY],
        out_shape=[jax.ShapeDtypeStruct((lp, GDN_V_W), F32),
                   jax.ShapeDtypeStruct((nh, nc, GDN_DK, LANE), F32),
                   jax.ShapeDtypeStruct((nh, nc, GDN_CHUNK, GDN_CHUNK), F32),
                   jax.ShapeDtypeStruct((4,) + shard.shape, shard.dtype)],
        scratch_shapes=[pltpu.VMEM((nh, GDN_DK, LANE), F32), pltpu.SemaphoreType.DMA((3,)),
                        pltpu.SemaphoreType.DMA((3,)), pltpu.SemaphoreType.DMA],
        compiler_params=_params(2),
    )(qkv, qkv, qkv, beta, gc, shard)


def _gdn_bwd(qkv, beta, gc, ckpt, t_saved, do, parts, name):
    lp = qkv.shape[0]
    nc = lp // GDN_CHUNK
    nh = GDN_V_HEADS
    hp = GDN_HEADS_PER_STEP
    ng = nh // hp
    q_spec, k_spec, v_spec, s_spec, o_spec, ck_spec = _gdn_specs(nc, True)

    def body(q_ref, k_ref, v_ref, b_ref, g_ref, ck_ref, t_ref, do_ref, p_ref,
             dq_ref, dk_ref, dv_ref, db_ref, dg_ref, from_ref, dstate, send_sems, recv_sems, local_sem):
        n = pl.program_id(0)
        g = pl.program_id(1)
        x, y, c = _my_place()
        me = 2 * x + y

        def local_copy():
            return pltpu.make_async_copy(p_ref.at[me], from_ref.at[me], local_sem)

        def remote_copy(k, px, py, src_slot, dst_slot):
            return pltpu.make_async_remote_copy(
                src_ref=p_ref.at[src_slot], dst_ref=from_ref.at[dst_slot], send_sem=send_sems.at[k],
                recv_sem=recv_sems.at[k], device_id=(px, py, c), device_id_type=MESH)

        @pl.when(jnp.logical_and(n == 0, g == 0))
        def _():
            local_copy().start()
            for k, (px, py) in enumerate(_other_chips(x, y)):
                remote_copy(k, px, py, 2 * px + py, me).start()

        @pl.when(jnp.logical_and(n == nc - 1, g == ng - 1))
        def _():
            for k, (px, py) in enumerate(_other_chips(x, y)):
                remote_copy(k, px, py, me, 2 * px + py).wait_recv()
            for k, (px, py) in enumerate(_other_chips(x, y)):
                remote_copy(k, px, py, 2 * px + py, me).wait_send()
            local_copy().wait()

        @pl.when(n == 0)
        def _():
            for i in range(hp):
                dstate[g * hp + i] = jnp.zeros((GDN_DK, LANE), F32)

        states = tuple(ck_ref[i, 0] for i in range(hp))
        t_known = t_ref[:, 0]
        _, vjp = jax.vjp(lambda q, k, v, b, gg, s: _gdn_group(q, k, v, b, gg, s, g * hp, t_known)[:2],
                         q_ref[...], k_ref[...], v_ref[...], b_ref[...], g_ref[...], states)
        dq, dk, dv, db, dg, ds = vjp((do_ref[...], tuple(dstate[g * hp + i] for i in range(hp))))
        dq_ref[...] = dq
        dk_ref[...] = dk
        dv_ref[...] = dv
        for i in range(hp):
            dstate[g * hp + i] = ds[i]

        @pl.when(g == 0)
        def _():
            db_ref[...] = db
            dg_ref[...] = dg

        @pl.when(g > 0)
        def _():
            db_ref[...] += db
            dg_ref[...] += dg

    qk_shape = jax.ShapeDtypeStruct((lp, GDN_QK_W), F32)
    big = jax.ShapeDtypeStruct((lp, GDN_V_W), F32)
    small = jax.ShapeDtypeStruct((lp, LANE), F32)
    dq_spec = pl.BlockSpec((GDN_CHUNK, hp // 2 * LANE), lambda n, g: (nc - 1 - n, g))
    return _pcall(
        body, name=name, grid=(nc, ng),
        in_specs=[q_spec, k_spec, v_spec, s_spec, s_spec, ck_spec, _gdn_t_spec(nc, True), o_spec, _ANY],
        out_specs=[dq_spec, dq_spec, o_spec, s_spec, s_spec, _ANY],
        out_shape=[qk_shape, qk_shape, big, small, small, jax.ShapeDtypeStruct(parts.shape, parts.dtype)],
        scratch_shapes=[pltpu.VMEM((nh, GDN_DK, LANE), F32), pltpu.SemaphoreType.DMA((3,)),
                        pltpu.SemaphoreType.DMA((3,)), pltpu.SemaphoreType.DMA],
        compiler_params=_params(2),
    )(qkv, qkv, qkv, beta, gc, ckpt, t_saved, do, parts)


LOG2E = 1.4426950408889634
LN2 = 0.6931471805599453
Q_PRESCALE = MLA_QK ** -0.5 * LOG2E


ATT_SUB = 128
ATT_HEADS_PER_STEP = 8
ATT_BWD_HEADS_PER_STEP = 2


def _att_mask(i, j, tb, transposed):
    r = lax.broadcasted_iota(jnp.int32, (tb, tb), 0)
    c = lax.broadcasted_iota(jnp.int32, (tb, tb), 1)
    qpos, kpos = (i * tb + c, j * tb + r) if transposed else (i * tb + r, j * tb + c)
    return jnp.logical_and(kpos <= qpos, kpos >= FRONT)


def _causal_pairs(nb, by_key):
    if by_key:
        pairs = [(i, j) for j in range(nb) for i in range(j, nb)]
    else:
        pairs = [(i, j) for i in range(nb) for j in range(i + 1)]
    return jnp.array([p[0] for p in pairs], jnp.int32), jnp.array([p[1] for p in pairs], jnp.int32)


def _masked_and_plain(i, j, step):
    edge = jnp.logical_or(j == i, j == 0)

    @pl.when(jnp.logical_and(edge, j <= i))
    def _():
        step(True)

    @pl.when(jnp.logical_and(jnp.logical_not(edge), j < i))
    def _():
        step(False)


def _cat(a_ref, b_ref):
    return jnp.concatenate([a_ref[...], b_ref[...]], axis=1)


def _flash_fwd(qn, qr, kn, kr, v, name, tb=ROW_ALIGN):
    lp = qn.shape[0]
    nb = lp // tb
    nh = MLA_HEADS
    hp = ATT_HEADS_PER_STEP
    qi, kj = _causal_pairs(nb, by_key=False)

    def body(qi_ref, kj_ref, qn_ref, qr_ref, kn_ref, kr_ref, v_ref, o_ref, lse_ref, m_s, l_s, acc):
        t = pl.program_id(1)
        i, j = qi_ref[t], kj_ref[t]

        @pl.when(j == 0)
        def _():
            m_s[...] = jnp.full_like(m_s, NEG)
            l_s[...] = jnp.zeros_like(l_s)
            acc[...] = jnp.zeros_like(acc)

        def step(masked):
            n_sub = tb // ATT_SUB
            kr = kr_ref[...]
            for e in range(hp):
                lanes = pl.ds(e * LANE, LANE)
                k = jnp.concatenate([kn_ref[:, lanes], kr], axis=1)
                v = v_ref[:, lanes]

                def scores(r, lanes=lanes, k=k):
                    rows = pl.ds(r * ATT_SUB, ATT_SUB)
                    return _dot(jnp.concatenate([qn_ref[rows, lanes], qr_ref[rows, lanes]], axis=1), k, _NT)

                s_next = scores(0)
                for r in range(n_sub):
                    s = s_next
                    if r + 1 < n_sub:
                        s_next = scores(r + 1)
                    rows = pl.ds(r * ATT_SUB, ATT_SUB)
                    if masked:
                        qpos = i * tb + r * ATT_SUB + lax.broadcasted_iota(jnp.int32, (ATT_SUB, tb), 0)
                        kpos = j * tb + lax.broadcasted_iota(jnp.int32, (ATT_SUB, tb), 1)
                        s = jnp.where(jnp.logical_and(kpos <= qpos, kpos >= FRONT), s, NEG)
                    m_old = m_s[e, rows, :]
                    m_new = jnp.maximum(m_old, jnp.max(s, axis=1, keepdims=True))
                    alpha = jnp.exp2(m_old - m_new)
                    p = jnp.exp2(s - m_new)
                    l_s[e, rows, :] = alpha * l_s[e, rows, :] + jnp.sum(p, axis=1, keepdims=True)
                    acc[e, rows, :] = alpha * acc[e, rows, :] + _dot(p.astype(BF16), v, _NN)
                    m_s[e, rows, :] = m_new

        _masked_and_plain(i, j, step)

        @pl.when(j == i)
        def _():
            for e in range(hp):
                lanes = pl.ds(e * LANE, LANE)
                o_ref[:, lanes] = acc[e] / l_s[e]
                lse_ref[:, lanes] = jnp.broadcast_to(m_s[e] + jnp.log(l_s[e]) * LOG2E, (tb, LANE))

    qspec = pl.BlockSpec((tb, hp * LANE), lambda h, t, qi_, kj_: (qi_[t], h))
    kspec = pl.BlockSpec((tb, hp * LANE), lambda h, t, qi_, kj_: (kj_[t], h))
    krspec = pl.BlockSpec((tb, LANE), lambda h, t, qi_, kj_: (kj_[t], 0))
    shp = jax.ShapeDtypeStruct((lp, nh * LANE), F32)
    return _pcall(
        body, name=name, out_shape=[shp, shp],
        grid_spec=pltpu.PrefetchScalarGridSpec(
            num_scalar_prefetch=2, grid=(nh // hp, qi.shape[0]),
            in_specs=[qspec, qspec, kspec, krspec, kspec], out_specs=[qspec, qspec],
            scratch_shapes=[pltpu.VMEM((hp, tb, 1), F32), pltpu.VMEM((hp, tb, 1), F32),
                            pltpu.VMEM((hp, tb, LANE), F32)]),
        compiler_params=_params(2),
    )(qi, kj, qn, qr, kn, kr, v)


def _flash_bwd(qn, qr, kn, kr, v, o, do, lse, name, tb=ROW_ALIGN):
    lp = qn.shape[0]
    nb = lp // tb
    nh = MLA_HEADS
    hp = ATT_BWD_HEADS_PER_STEP
    qi, kj = _causal_pairs(nb, by_key=True)
    n_pairs = qi.shape[0]
    knt, krt = kn.T, kr.T

    def body(qi_ref, kj_ref, qn_ref, qr_ref, kn_ref, kr_ref, knt_ref, krt_ref, v_ref, o_ref, do_ref, lse_ref,
             dqnt_hbm, dqrt_hbm, dkn_ref, dkr_ref, dv_ref, dk_acc, dv_acc, dqn_acc, dqr_acc, out_sems):
        g = pl.program_id(0)
        t = pl.program_id(1)
        i, j = qi_ref[t], kj_ref[t]

        @pl.when(t == 0)
        def _():
            dqn_acc[...] = jnp.zeros_like(dqn_acc)
            dqr_acc[...] = jnp.zeros_like(dqr_acc)

        @pl.when(i == j)
        def _():
            dk_acc[...] = jnp.zeros_like(dk_acc)
            dv_acc[...] = jnp.zeros_like(dv_acc)

        def step(masked):
            kr = kr_ref[...]
            krt_blk = krt_ref[...]
            lane = lax.broadcasted_iota(jnp.int32, (8, LANE), 1)
            for e in range(hp):
                lanes = pl.ds(e * LANE, LANE)
                q = jnp.concatenate([qn_ref[:, lanes], qr_ref[:, lanes]], axis=1)
                st = _dot(jnp.concatenate([kn_ref[:, lanes], kr], axis=1), q, _NT)
                if masked:
                    st = jnp.where(_att_mask(i, j, tb, True), st, NEG)
                do_blk = do_ref[:, lanes]
                lse_row = _hdot((lane == 0).astype(F32), lse_ref[:, lanes], _NT)[0:1]
                delta_row = _hdot(jnp.ones((8, LANE), F32), do_blk * o_ref[:, lanes], _NT)[0:1]
                pt = jnp.exp2(st - lse_row)
                do_b = do_blk.astype(BF16)
                dv_acc[e] += _dot(pt.astype(BF16), do_b, _NN)
                dpt = _dot(v_ref[:, lanes], do_b, _NT)
                dst = (pt * (dpt - delta_row)).astype(BF16)
                dk_acc[e] += _dot(dst, q, _NN)
                dqn_acc[e * nb + i] += _dot(knt_ref[pl.ds(e * LANE, LANE), :], dst, _NN) * LN2
                dqr_acc[e * nb + i] += _dot(krt_blk, dst, _NN) * LN2

        _masked_and_plain(i, j, step)

        @pl.when(i == nb - 1)
        def _():
            for e in range(hp):
                lanes = pl.ds(e * LANE, LANE)
                dkn_ref[:, lanes] = dk_acc[e, :, :LANE] * LN2
                dkr_ref[:, lanes] = dk_acc[e, :, LANE:] * LN2
                dv_ref[:, lanes] = dv_acc[e]

        @pl.when(t == n_pairs - 1)
        def _():
            dst_rows = pl.ds(g * (hp * nb), hp * nb)
            cn = pltpu.make_async_copy(dqn_acc, dqnt_hbm.at[dst_rows], out_sems.at[0])
            cr = pltpu.make_async_copy(dqr_acc, dqrt_hbm.at[dst_rows], out_sems.at[1])
            cn.start()
            cr.start()
            cn.wait()
            cr.wait()

    qspec = pl.BlockSpec((tb, hp * LANE), lambda h, t, qi_, kj_: (qi_[t], h))
    kspec = pl.BlockSpec((tb, hp * LANE), lambda h, t, qi_, kj_: (kj_[t], h))
    krspec = pl.BlockSpec((tb, LANE), lambda h, t, qi_, kj_: (kj_[t], 0))
    ktspec = pl.BlockSpec((hp * LANE, tb), lambda h, t, qi_, kj_: (h, kj_[t]))
    krtspec = pl.BlockSpec((LANE, tb), lambda h, t, qi_, kj_: (0, kj_[t]))
    shp = jax.ShapeDtypeStruct((lp, nh * LANE), F32)
    dqt_shape = jax.ShapeDtypeStruct((nh * nb, LANE, tb), F32)
    dqnt, dqrt, dkn, dkr, dv = _pcall(
        body, name=name, out_shape=[dqt_shape, dqt_shape, shp, shp, shp],
        grid_spec=pltpu.PrefetchScalarGridSpec(
            num_scalar_prefetch=2, grid=(nh // hp, n_pairs),
            in_specs=[qspec, qspec, kspec, krspec, ktspec, krtspec, kspec, qspec, qspec, qspec],
            out_specs=[_ANY, _ANY, kspec, kspec, kspec],
            scratch_shapes=[pltpu.VMEM((hp, tb, 2 * LANE), F32), pltpu.VMEM((hp, tb, LANE), F32),
                            pltpu.VMEM((hp * nb, LANE, tb), F32), pltpu.VMEM((hp * nb, LANE, tb), F32),
                            pltpu.SemaphoreType.DMA((2,))]),
        compiler_params=_params(2),
    )(qi, kj, qn, qr, kn, kr, knt, krt, v, o, do, lse)

    def rows_major(a):
        return a.reshape(nh, nb, LANE, tb).transpose(1, 3, 0, 2).reshape(lp, nh * LANE)

    return rows_major(dqnt), rows_major(dqrt), dkn, dkr, dv


ELEMENTWISE_BLOCK_BYTES = 1 << 20


def _row_tile(rows, width, copies=1):
    for t in (1024, 512, 256, 128, 64, 32, 16, 8):
        if rows % t == 0 and t * width * 4 * copies <= ELEMENTWISE_BLOCK_BYTES:
            return t
    return rows


def _adamw(w, g, m, v, name):
    rows, width = w.shape
    tr = _row_tile(rows, width)

    def body(w_ref, g_ref, m_ref, v_ref, d_ref, nm_ref, nv_ref):
        gg = g_ref[...]
        nm = ADAM_B1 * m_ref[...] + (1.0 - ADAM_B1) * gg
        nv = ADAM_B2 * v_ref[...] + (1.0 - ADAM_B2) * jnp.square(gg)
        m_hat = nm / (1.0 - ADAM_B1 ** ADAM_STEP)
        v_hat = nv / (1.0 - ADAM_B2 ** ADAM_STEP)
        d_ref[...] = -ADAM_LR * (m_hat / (jnp.sqrt(v_hat) + ADAM_EPS) + ADAM_WD * w_ref[...])
        nm_ref[...] = nm
        nv_ref[...] = nv

    spec = pl.BlockSpec((tr, width), lambda r: (r, 0))
    shp = jax.ShapeDtypeStruct((rows, width), F32)
    return _pcall(body, name=name, grid=(rows // tr,), in_specs=[spec] * 4, out_specs=[spec] * 3,
                  out_shape=[shp] * 3, compiler_params=_params(1))(w, g, m, v)


def _add_pair(a, b, name):
    s, rows, width = b.shape
    tr = _row_tile(rows, width)
    nt = rows // tr

    def body(c_ref, a_ref, b_ref, o_ref):
        o_ref[...] = a_ref[...] + b_ref[...]

    spec = pl.BlockSpec((1, tr, width), lambda i, r, c_ref: (i, r, 0))
    return _pcall(
        body, name=name, out_shape=jax.ShapeDtypeStruct(b.shape, F32),
        grid_spec=pltpu.PrefetchScalarGridSpec(
            num_scalar_prefetch=1, grid=(s, nt),
            in_specs=[pl.BlockSpec((1, tr, width), lambda i, r, c_ref: (i, c_ref[0] * nt + r, 0)), spec],
            out_specs=spec),
        compiler_params=_params(2),
    )(_core_index(), a, b)


def _sum_slots(a, name):
    s, rows, width = a.shape
    tr = _row_tile(rows, width, copies=s)

    def body(a_ref, o_ref):
        tot = a_ref[0]
        for k in range(1, s):
            tot = tot + a_ref[k]
        o_ref[...] = tot

    return _pcall(body, name=name, grid=(rows // tr,),
                  in_specs=[pl.BlockSpec((s, tr, width), lambda r: (0, r, 0))],
                  out_specs=pl.BlockSpec((tr, width), lambda r: (r, 0)),
                  out_shape=jax.ShapeDtypeStruct((rows, width), F32), compiler_params=_params(1))(a)


_ANY = pl.BlockSpec(memory_space=pl.ANY)


def _my_place():
    return lax.axis_index("x"), lax.axis_index("y"), lax.axis_index("c")


def _core_index():
    return lax.axis_index("c").astype(jnp.int32).reshape(1)


def _other_chips(x, y):
    return [(1 - x, y), (x, 1 - y), (1 - x, 1 - y)]


def _gather_shards(flat, name):
    rows, width = flat.shape

    def body(x_ref, out_ref, send_sems, recv_sems, local_sem):
        x, y, c = _my_place()
        mine = pltpu.make_async_copy(x_ref, out_ref.at[2 * x + y], local_sem)
        mine.start()
        sends = []
        for k, (px, py) in enumerate(_other_chips(x, y)):
            cp = pltpu.make_async_remote_copy(
                src_ref=x_ref, dst_ref=out_ref.at[2 * x + y], send_sem=send_sems.at[k], recv_sem=recv_sems.at[k],
                device_id=(px, py, c), device_id_type=MESH)
            cp.start()
            sends.append(cp)
        for k, (px, py) in enumerate(_other_chips(x, y)):
            pltpu.make_async_remote_copy(
                src_ref=x_ref, dst_ref=out_ref.at[2 * px + py], send_sem=send_sems.at[k], recv_sem=recv_sems.at[k],
                device_id=(px, py, c), device_id_type=MESH).wait_recv()
        for cp in sends:
            cp.wait_send()
        mine.wait()

    return _pcall(
        body, name=name, in_specs=[_ANY], out_specs=_ANY,
        out_shape=jax.ShapeDtypeStruct((4, rows, width), flat.dtype),
        scratch_shapes=[pltpu.SemaphoreType.DMA((3,)), pltpu.SemaphoreType.DMA((3,)), pltpu.SemaphoreType.DMA],
    )(flat)


def _sibling_split(g, name):
    s, rows, width = g.shape
    half = rows // 2
    tr = _row_tile(half, width)
    nt = half // tr

    def body(c_ref, g_blk, got_ref, send_sem, recv_sem):
        k = pl.program_id(0)
        t = pl.program_id(1)
        x, y, c = _my_place()
        cp = pltpu.make_async_remote_copy(
            src_ref=g_blk.at[0], dst_ref=got_ref.at[k, pl.ds(pl.multiple_of(t * tr, 8), tr), :],
            send_sem=send_sem, recv_sem=recv_sem, device_id=(x, y, 1 - c), device_id_type=MESH)
        cp.start()
        cp.wait_send()

        @pl.when(jnp.logical_and(k == s - 1, t == nt - 1))
        def _():
            pltpu.make_async_remote_copy(
                src_ref=got_ref, dst_ref=got_ref, send_sem=send_sem, recv_sem=recv_sem,
                device_id=(x, y, 1 - c), device_id_type=MESH).wait_recv()

    return _pcall(
        body, name=name, out_shape=jax.ShapeDtypeStruct((s, half, width), g.dtype),
        grid_spec=pltpu.PrefetchScalarGridSpec(
            num_scalar_prefetch=1, grid=(s, nt),
            in_specs=[pl.BlockSpec((1, tr, width), lambda k, t, c_ref: (k, (1 - c_ref[0]) * nt + t, 0))],
            out_specs=_ANY,
            scratch_shapes=[pltpu.SemaphoreType.DMA, pltpu.SemaphoreType.DMA]),
        compiler_params=_params(2),
    )(_core_index(), g)


def _chip_scatter(p, name):
    s, rows, width = p.shape

    def body(p_ref, out_ref, send_sems, recv_sems, local_sem):
        x, y, c = _my_place()
        me = 2 * x + y
        mine = pltpu.make_async_copy(p_ref.at[me], out_ref.at[me], local_sem)
        mine.start()
        sends = []
        for k, (px, py) in enumerate(_other_chips(x, y)):
            cp = pltpu.make_async_remote_copy(
                src_ref=p_ref.at[2 * px + py], dst_ref=out_ref.at[me], send_sem=send_sems.at[k],
                recv_sem=recv_sems.at[k], device_id=(px, py, c), device_id_type=MESH)
            cp.start()
            sends.append(cp)
        for k, (px, py) in enumerate(_other_chips(x, y)):
            pltpu.make_async_remote_copy(
                src_ref=p_ref.at[me], dst_ref=out_ref.at[2 * px + py], send_sem=send_sems.at[k],
                recv_sem=recv_sems.at[k], device_id=(px, py, c), device_id_type=MESH).wait_recv()
        for cp in sends:
            cp.wait_send()
        mine.wait()

    return _pcall(
        body, name=name, in_specs=[_ANY], out_specs=_ANY,
        out_shape=jax.ShapeDtypeStruct(p.shape, p.dtype),
        scratch_shapes=[pltpu.SemaphoreType.DMA((3,)), pltpu.SemaphoreType.DMA((3,)), pltpu.SemaphoreType.DMA],
    )(p)


def _sibling_join(qh, name):
    half, width = qh.shape
    tr = _row_tile(half, width)
    nt = half // tr

    def body(q_blk, out_ref, send_sem, recv_sem, local_sem):
        t = pl.program_id(0)
        x, y, c = _my_place()
        dst = out_ref.at[pl.ds(pl.multiple_of(c * half + t * tr, 8), tr), :]
        cp = pltpu.make_async_remote_copy(
            src_ref=q_blk, dst_ref=dst, send_sem=send_sem, recv_sem=recv_sem,
            device_id=(x, y, 1 - c), device_id_type=MESH)
        cp.start()
        mine = pltpu.make_async_copy(q_blk, dst, local_sem)
        mine.start()
        cp.wait_send()
        mine.wait()

        @pl.when(t == nt - 1)
        def _():
            theirs = out_ref.at[pl.ds(pl.multiple_of((1 - c) * half, 8), half), :]
            pltpu.make_async_remote_copy(
                src_ref=theirs, dst_ref=theirs, send_sem=send_sem, recv_sem=recv_sem,
                device_id=(x, y, 1 - c), device_id_type=MESH).wait_recv()

    return _pcall(
        body, name=name, grid=(nt,),
        in_specs=[pl.BlockSpec((tr, width), lambda t: (t, 0))], out_specs=_ANY,
        out_shape=jax.ShapeDtypeStruct((2 * half, width), qh.dtype),
        scratch_shapes=[pltpu.SemaphoreType.DMA, pltpu.SemaphoreType.DMA, pltpu.SemaphoreType.DMA],
        compiler_params=_params(1),
    )(qh)


def _all_sum_small(part, name):
    rows, width = part.shape

    def body(p_ref, out_ref, land, send_sems, recv_sems):
        x, y, c = _my_place()
        me = 4 * x + 2 * y + c
        land[me] = p_ref[...]
        sends = []
        for k in range(1, 8):
            peer = (x ^ (k >> 2), y ^ ((k >> 1) & 1), c ^ (k & 1))
            cp = pltpu.make_async_remote_copy(
                src_ref=p_ref, dst_ref=land.at[me], send_sem=send_sems.at[k - 1], recv_sem=recv_sems.at[k - 1],
                device_id=peer, device_id_type=MESH)
            cp.start()
            sends.append(cp)
        for k in range(1, 8):
            px, py, pc = x ^ (k >> 2), y ^ ((k >> 1) & 1), c ^ (k & 1)
            pltpu.make_async_remote_copy(
                src_ref=p_ref, dst_ref=land.at[4 * px + 2 * py + pc], send_sem=send_sems.at[k - 1],
                recv_sem=recv_sems.at[k - 1], device_id=(px, py, pc), device_id_type=MESH).wait_recv()
        for cp in sends:
            cp.wait_send()
        tot = land[0]
        for k in range(1, 8):
            tot = tot + land[k]
        out_ref[...] = tot

    vmem = pl.BlockSpec(memory_space=pltpu.VMEM)
    return _pcall(
        body, name=name, in_specs=[vmem], out_specs=vmem,
        out_shape=jax.ShapeDtypeStruct((rows, width), F32),
        scratch_shapes=[pltpu.VMEM((8, rows, width), F32), pltpu.SemaphoreType.DMA((7,)),
                        pltpu.SemaphoreType.DMA((7,))],
    )(part)


def _big_layout(shards):
    return [(a.shape[0], a.shape[1], ax) for a, ax in shards]


FLAT_ROW_MULTIPLE = 2048


def _pack_shards(arrs, row_multiple=FLAT_ROW_MULTIPLE):
    flat = jnp.concatenate([a.reshape(-1) for a in arrs])
    return jnp.pad(flat, (0, -flat.shape[0] % (row_multiple * LANE))).reshape(-1, LANE)


def _unpack_shards(flat, layout):
    flat = flat.reshape(-1)
    out, off = [], 0
    for r, c, _ in layout:
        out.append(flat[off:off + r * c].reshape(r, c))
        off += r * c
    return out


def _unpack_full(gathered, layout):
    g = gathered.reshape(4, -1)
    out, off = [], 0
    for r, c, ax in layout:
        seg = g[:, off:off + r * c].reshape(4, r, c)
        out.append(seg.transpose(1, 0, 2).reshape(r, 4 * c) if ax == 1 else seg.reshape(4 * r, c))
        off += r * c
    return out


def _pack_full(fulls, layout):
    parts = []
    for a, (r, c, ax) in zip(fulls, layout):
        if ax == 1:
            parts.append(a.reshape(r, 4, c).transpose(1, 0, 2).reshape(4, r * c))
        else:
            parts.append(a.reshape(4, r * c))
    flat = jnp.concatenate(parts, axis=1)
    return jnp.pad(flat, ((0, 0), (0, -flat.shape[1] % (FLAT_ROW_MULTIPLE * LANE)))).reshape(4, -1, LANE)


def _pad_lanes(a, width=LANE):
    return jnp.pad(a, [(0, 0)] * (a.ndim - 1) + [(0, width - a.shape[-1])])


def _pack_small(arrs):
    rows = [_pad_lanes(a.reshape(1, -1), -(-a.size // LANE) * LANE).reshape(-1, LANE) for a in arrs]
    flat = jnp.concatenate(rows, axis=0)
    return jnp.pad(flat, ((0, -flat.shape[0] % 8), (0, 0)))


def _unpack_small(flat, shapes):
    out, off = [], 0
    for shp in shapes:
        n = math.prod(shp)
        nr = -(-n // LANE)
        out.append(flat[off:off + nr].reshape(-1)[:n].reshape(shp))
        off += nr
    return out


def kernel(x, meta_tokens, pre_norm, post_norm, gdn_w_in, gdn_conv_w, gdn_a_log, gdn_dt_bias, gdn_out_norm, gdn_w_out, kv_norm, kv_w_down, kv_latent_norm, kv_w_up, mla_w_in, mla_q_latent_norm, mla_w_q_up, mla_w_out, loss_target, m_meta_tokens, m_pre_norm, m_post_norm, m_gdn_w_in, m_gdn_conv_w, m_gdn_a_log, m_gdn_dt_bias, m_gdn_out_norm, m_gdn_w_out, m_kv_norm, m_kv_w_down, m_kv_latent_norm, m_kv_w_up, m_mla_w_in, m_mla_q_latent_norm, m_mla_w_q_up, m_mla_w_out, v_meta_tokens, v_pre_norm, v_post_norm, v_gdn_w_in, v_gdn_conv_w, v_gdn_a_log, v_gdn_dt_bias, v_gdn_out_norm, v_gdn_w_out, v_kv_norm, v_kv_w_down, v_kv_latent_norm, v_kv_w_up, v_mla_w_in, v_mla_q_latent_norm, v_mla_w_q_up, v_mla_w_out):
    seq = x.shape[1]
    d = D_MODEL
    lp = -(-(ROW0 + seq) // ROW_ALIGN) * ROW_ALIGN
    tail = lp - ROW0 - seq

    big_names = ["meta_tokens", "gdn_conv_w", "gdn_w_out", "kv_w_down", "kv_w_up", "mla_w_in", "mla_w_q_up",
                 "mla_w_out"]
    big_axis = [1, 1, 0, 0, 1, 1, 1, 0]
    big_w = [meta_tokens, gdn_conv_w[0], gdn_w_out[0], kv_w_down, kv_w_up, mla_w_in[0], mla_w_q_up[0], mla_w_out[0]]
    big_m = [m_meta_tokens, m_gdn_conv_w[0], m_gdn_w_out[0], m_kv_w_down, m_kv_w_up, m_mla_w_in[0], m_mla_w_q_up[0],
             m_mla_w_out[0]]
    big_v = [v_meta_tokens, v_gdn_conv_w[0], v_gdn_w_out[0], v_kv_w_down, v_kv_w_up, v_mla_w_in[0], v_mla_w_q_up[0],
             v_mla_w_out[0]]
    layout = _big_layout(list(zip(big_w, big_axis)))
    meta_f, conv_w = _unpack_full(
        _gather_shards(_pack_shards(big_w[:2], row_multiple=16), "gather_meta_conv"), layout[:2])
    mm_shards = [w.astype(BF16) for w in big_w[2:6]] + [(big_w[6] * Q_PRESCALE).astype(BF16), big_w[7].astype(BF16)]
    mm_flat = _pack_shards(mm_shards)
    w_in0_shards = _gather_shards(gdn_w_in[0].astype(BF16), "gather_gdn_w_in")
    w_in0 = jnp.concatenate([w_in0_shards[s] for s in range(4)], axis=1)
    win_cols = gdn_w_in.shape[2]

    nv = GDN_V_HEADS
    w_qkv = w_in0[:, :GDN_CONV_W]
    w_z0 = w_in0[:, GDN_CONV_W:GDN_CONV_W + GDN_V_W]
    w_b = _pad_lanes(w_in0[:, GDN_CONV_W + GDN_V_W:GDN_CONV_W + GDN_V_W + nv])
    w_a = _pad_lanes(w_in0[:, GDN_CONV_W + GDN_V_W + nv:])

    pre0, pre1 = pre_norm[0:1], pre_norm[1:2]
    post0, post1 = post_norm[0:1], post_norm[1:2]
    a_log = _pad_lanes(gdn_a_log)
    dt_bias = _pad_lanes(gdn_dt_bias)
    kvn = kv_norm.reshape(1, d)
    kvl = kv_latent_norm.reshape(1, MLA_KV_RANK)
    qln = mla_q_latent_norm

    h0 = jnp.concatenate([jnp.zeros((FRONT, d), F32), meta_f, x[0], jnp.zeros((tail, d), F32)], axis=0)
    tgt = jnp.pad(loss_target[0], ((ROW0, tail), (0, 0)))
    pos = jnp.maximum(jnp.arange(lp, dtype=jnp.int32) - FRONT, 0).astype(F32)
    inv = ROPE_THETA ** (-jnp.arange(0, MLA_ROPE, 2, dtype=F32) / MLA_ROPE)
    ang = pos[:, None] * inv[None, :]
    zeros64 = jnp.zeros((lp, LANE - MLA_ROPE), F32)
    cos_t = jnp.concatenate([jnp.cos(ang), jnp.cos(ang), zeros64], axis=1)
    sin_t = jnp.concatenate([-jnp.sin(ang), jnp.sin(ang), zeros64], axis=1)

    def valid_rows(ridx):
        return jnp.logical_and(ridx >= FRONT, ridx < ROW0 + seq)

    def f_pre0(ridx, g, h, gain):
        return _rms(h, gain), h

    (hn0,) = _rowwise("pre0", lambda *a: f_pre0(*a)[:1], [_In(h0), _In(pre0, "const")],
                      [_Out("row", (lp, d), BF16)])
    qkv_raw = _mm(hn0, w_qkv, "nn", "gdn_in_qkv")
    z0 = _mm(hn0, w_z0, "nn", "gdn_in_z")
    b_raw = _mm(hn0, w_b, "nn", "gdn_in_b")
    a_raw = _mm(hn0, w_a, "nn", "gdn_in_a")

    def f_ba(ridx, g, b, a, alog, dtb):
        tr = b.shape[0]
        ok = valid_rows(ridx).astype(F32)
        beta = jax.nn.sigmoid(b) * ok
        gate = -jnp.exp(alog) * _softplus(a + dtb) * ok
        ii = lax.broadcasted_iota(jnp.int32, (tr, tr), 0)
        jj = lax.broadcasted_iota(jnp.int32, (tr, tr), 1)
        shift = GDN_CHUNK.bit_length() - 1
        tri = jnp.logical_and((ii >> shift) == (jj >> shift), ii >= jj).astype(F32)
        return beta, _hdot(tri, gate)

    ba_ins = [_In(b_raw), _In(a_raw), _In(a_log, "const"), _In(dt_bias, "const")]
    beta, gc = _rowwise("gdn_gates", f_ba, ba_ins, [_Out("row", (lp, LANE)), _Out("row", (lp, LANE))])
    qkv = _conv_fwd(qkv_raw, conv_w, "gdn_conv")
    o0, ckpt, t_saved, mm_all = _gdn_fwd(qkv, beta, gc, mm_flat, "gdn_scan")
    (w_out0, kv_down, kv_up, w_in1, w_qup, w_out1) = _unpack_full(mm_all, layout[2:])
    w_ckv = kv_down[:, :MLA_KV_RANK]
    w_kr = _pad_lanes(kv_down[:, MLA_KV_RANK:])
    kvu = kv_up.reshape(MLA_KV_RANK, MLA_HEADS, 2 * LANE)
    w_kn = kvu[:, :, :LANE].reshape(MLA_KV_RANK, MLA_HEADS * LANE)
    w_v = kvu[:, :, LANE:].reshape(MLA_KV_RANK, MLA_HEADS * LANE)
    w_cq = w_in1[:, :MLA_Q_RANK]
    w_z1 = w_in1[:, MLA_Q_RANK:]
    qu = w_qup.reshape(MLA_Q_RANK, MLA_HEADS, MLA_QK)
    w_qn = qu[:, :, :MLA_NOPE].reshape(MLA_Q_RANK, MLA_HEADS * LANE)
    w_qr = _pad_lanes(qu[:, :, MLA_NOPE:]).reshape(MLA_Q_RANK, MLA_HEADS * LANE)

    def per_head(fn, *arrs):
        n = arrs[0].shape[1] // LANE
        return jnp.concatenate([fn(*[a[:, i * LANE:(i + 1) * LANE] for a in arrs]) for i in range(n)], axis=1)

    def f_gate0(ridx, g, o, z, gain):
        return (per_head(lambda oh, zh: _rms(oh, gain) * _silu(zh), o, z),)

    gate0_ins = [_In(o0), _In(z0), _In(gdn_out_norm, "const")]
    (gated0,) = _rowwise("gdn_gate", f_gate0, gate0_ins, [_Out("row", (lp, GDN_V_W), BF16)])
    y0 = _mm(gated0, w_out0, "nn", "gdn_out")

    def f_mid(ridx, g, h, y, g_post, g_pre, g_kv):
        h1 = h + _rms(y, g_post)
        return h1, _rms(h1, g_pre), _rms(h1, g_kv)

    mid_ins = [_In(h0), _In(y0), _In(post0, "const"), _In(pre1, "const"), _In(kvn, "const")]
    h1, hn1, hkv = _rowwise("mid", f_mid, mid_ins,
                            [_Out("row", (lp, d)), _Out("row", (lp, d), BF16), _Out("row", (lp, d), BF16)])

    ckv_raw = _mm(hkv, w_ckv, "nn", "kv_down_c")
    kr_raw = _mm(hkv, w_kr, "nn", "kv_down_r")

    def f_ckv(ridx, g, c, r, cs, sn, gain):
        return _rms(c, gain), _rope(r, cs, sn)

    ckv_ins = [_In(ckv_raw), _In(kr_raw), _In(cos_t), _In(sin_t), _In(kvl, "const")]
    ckv, kr = _rowwise("kv_latent", f_ckv, ckv_ins, [_Out("row", (lp, LANE)), _Out("row", (lp, LANE), BF16)],
                       tr=TR_FULL)
    kn = _mm(ckv, w_kn, "nn", "kv_up_k", BF16)
    vv = _mm(ckv, w_v, "nn", "kv_up_v", BF16)
    cq_raw = _mm(hn1, w_cq, "nn", "mla_in_q")
    z1 = _mm(hn1, w_z1, "nn", "mla_in_z")

    def f_cq(ridx, g, c, gain):
        return (_rms(c, gain),)

    cq_ins = [_In(cq_raw), _In(qln, "const")]
    (cq,) = _rowwise("q_latent", f_cq, cq_ins, [_Out("row", (lp, MLA_Q_RANK))], tr=TR_FULL)
    qn = _mm(cq, w_qn, "nn", "q_up_n", BF16)
    qr_raw = _mm(cq, w_qr, "nn", "q_up_r")

    def f_qrope(ridx, g, r, cs, sn):
        return (per_head(lambda rh: _rope(rh, cs, sn), r),)

    qr_ins = [_In(qr_raw), _In(cos_t), _In(sin_t)]
    (qr,) = _rowwise("q_rope", f_qrope, qr_ins, [_Out("row", (lp, MLA_HEADS * LANE), BF16)])
    o1, lse = _flash_fwd(qn, qr, kn, kr, vv, "attention")

    def f_gate1(ridx, g, o, z):
        return (o * _silu(z),)

    gate1_ins = [_In(o1), _In(z1)]
    (og,) = _rowwise("mla_gate", f_gate1, gate1_ins, [_Out("row", (lp, MLA_HEADS * LANE), BF16)])
    y1 = _mm(og, w_out1, "nn", "mla_out")

    def f_final(ridx, g, h, y, t, gain):
        ok = jnp.logical_and(ridx >= ROW0, ridx < ROW0 + seq).astype(F32)

        def rows_loss(h_, y_, gain_):
            err = (h_ + _rms(y_, gain_) - t) * ok
            return 0.5 * jnp.sum(jnp.sum(err * err, axis=1, keepdims=True), axis=0, keepdims=True) / d

        val, vjp = jax.vjp(rows_loss, h, y, gain)
        dh, dy, dgain = vjp(jnp.ones((1, 1), F32))
        return dh, dy, dgain, jnp.broadcast_to(val, (1, LANE))

    dh2, dy1, dpost1, loss_part = _rowwise(
        "loss_head", f_final, [_In(h1), _In(y1), _In(tgt), _In(post1, "const")],
        [_Out("row", (lp, d)), _Out("row", (lp, d)), _Out("acc", (1, d)), _Out("acc", (1, LANE))])

    dog = _mm(dy1, w_out1, "nt", "mla_out_dx")
    dw_out1 = _mm(og, dy1, "tn", "mla_out_dw")
    do1, dz1 = _rowwise_vjp("mla_gate_bwd", f_gate1, gate1_ins, [[dog]], [0, 1])
    dqn, dqr, dkn, dkr, dvv = _flash_bwd(qn, qr, kn, kr, vv, o1, do1, lse, "attention_bwd")
    (dqr_raw,) = _rowwise_vjp("q_rope_bwd", f_qrope, qr_ins, [[dqr]], [0])
    dcq_a = _mm(dqn, w_qn, "nt", "q_up_n_dx")
    dcq_b = _mm(dqr_raw, w_qr, "nt", "q_up_r_dx")
    dw_qn = _mm(cq, dqn, "tn", "q_up_n_dw") * Q_PRESCALE
    dw_qr = _mm(cq, dqr_raw, "tn", "q_up_r_dw") * Q_PRESCALE
    dcq_raw, dqln = _rowwise_vjp("q_latent_bwd", f_cq, cq_ins, [[dcq_a, dcq_b]], [0, 1], tr=TR_FULL)
    dhn1_a = _mm(dcq_raw, w_cq, "nt", "mla_in_q_dx")
    dhn1_b = _mm(dz1, w_z1, "nt", "mla_in_z_dx")
    dw_cq = _mm(hn1, dcq_raw, "tn", "mla_in_q_dw")
    dw_z1 = _mm(hn1, dz1, "tn", "mla_in_z_dw")
    dckv_a = _mm(dkn, w_kn, "nt", "kv_up_k_dx")
    dckv_b = _mm(dvv, w_v, "nt", "kv_up_v_dx")
    dw_kn = _mm(ckv, dkn, "tn", "kv_up_k_dw")
    dw_v = _mm(ckv, dvv, "tn", "kv_up_v_dw")
    dckv_raw, dkr_raw, dkvl = _rowwise_vjp("kv_latent_bwd", f_ckv, ckv_ins, [[dckv_a, dckv_b], [dkr]], [0, 1, 4],
                                           tr=TR_FULL)
    dhkv_a = _mm(dckv_raw, w_ckv, "nt", "kv_down_c_dx")
    dhkv_b = _mm(dkr_raw, w_kr, "nt", "kv_down_r_dx")
    dw_ckv = _mm(hkv, dckv_raw, "tn", "kv_down_c_dw")
    dw_kr = _mm(hkv, dkr_raw, "tn", "kv_down_r_dw")
    dh0_res, dy0, dpost0, dpre1, dkvn = _rowwise_vjp(
        "mid_bwd", f_mid, mid_ins, [[dh2], [dhn1_a, dhn1_b], [dhkv_a, dhkv_b]], [0, 1, 2, 3, 4])

    dgated0 = _mm(dy0, w_out0, "nt", "gdn_out_dx")
    dw_out0 = _mm(gated0, dy0, "tn", "gdn_out_dw")
    do0, dz0, doutn = _rowwise_vjp("gdn_gate_bwd", f_gate0, gate0_ins, [[dgated0]], [0, 1, 2], tr=TR_QUARTER)

    g_kv_down = jnp.concatenate([dw_ckv, dw_kr[:, :MLA_ROPE]], axis=1)
    g_kv_up = jnp.concatenate([dw_kn.reshape(MLA_KV_RANK, MLA_HEADS, LANE), dw_v.reshape(MLA_KV_RANK, MLA_HEADS, LANE)],
                              axis=2).reshape(MLA_KV_RANK, MLA_HEADS * 2 * LANE)
    g_w_in1 = jnp.concatenate([dw_cq, dw_z1], axis=1)
    g_qup = jnp.concatenate([dw_qn.reshape(MLA_Q_RANK, MLA_HEADS, LANE),
                             dw_qr.reshape(MLA_Q_RANK, MLA_HEADS, LANE)[:, :, :MLA_ROPE]],
                            axis=2).reshape(MLA_Q_RANK, MLA_HEADS * MLA_QK)
    g_mm = _pack_full([dw_out0, g_kv_down, g_kv_up, g_w_in1, g_qup, dw_out1], layout[2:])
    mm_chip_part = _add_pair(g_mm, _sibling_split(g_mm, "grads_sibling_split"), "grads_chip_sum")
    dq0, dk0, dv0, dbeta, dgc, mm_from_chips = _gdn_bwd(qkv, beta, gc, ckpt, t_saved, do0, mm_chip_part,
                                                        "gdn_scan_bwd")
    g_flat = _sibling_join(_sum_slots(mm_from_chips, "grads_total"), "grads_sibling_join")
    db_raw, da_raw, dalog, ddtb = _rowwise_vjp("gdn_gates_bwd", f_ba, ba_ins, [[dbeta], [dgc]], [0, 1, 2, 3])
    dqkv_raw, dconv = _conv_bwd(qkv_raw, conv_w, dq0, dk0, dv0, "gdn_conv_bwd")
    dhn0_a = _mm(dqkv_raw, w_qkv, "nt", "gdn_in_qkv_dx")
    dhn0_b = _mm(dz0, w_z0, "nt", "gdn_in_z_dx")
    dhn0_c = _mm(db_raw, w_b, "nt", "gdn_in_b_dx")
    dhn0_d = _mm(da_raw, w_a, "nt", "gdn_in_a_dx")
    dw_qkv = _mm(hn0, dqkv_raw, "tn", "gdn_in_qkv_dw")
    dw_z0 = _mm(hn0, dz0, "tn", "gdn_in_z_dw")
    dw_b = _mm(hn0, db_raw, "tn", "gdn_in_b_dw")
    dw_a = _mm(hn0, da_raw, "tn", "gdn_in_a_dw")
    dh0, dpre0 = _rowwise_vjp("pre0_bwd", f_pre0, [_In(h0), _In(pre0, "const")],
                              [[dhn0_a, dhn0_b, dhn0_c, dhn0_d], [dh0_res]], [0, 1])

    grad_x = dh0[ROW0:ROW0 + seq][None]
    g_meta = dh0[FRONT:ROW0]
    g_w_in0 = jnp.concatenate([dw_qkv, dw_z0, dw_b[:, :nv], dw_a[:, :nv]], axis=1)

    g_win_by_chip = jnp.concatenate([g_w_in0[None, :, s * win_cols:(s + 1) * win_cols] for s in range(4)], axis=0)
    win_chip_part = _add_pair(g_win_by_chip, _sibling_split(g_win_by_chip, "grads_sibling_split_gdn_w_in"),
                              "grads_chip_sum_gdn_w_in")
    win_from_chips = _chip_scatter(win_chip_part, "grads_chip_scatter_gdn_w_in")
    g_win = _sibling_join(_sum_slots(win_from_chips, "grads_total_gdn_w_in"), "grads_sibling_join_gdn_w_in")

    small_shapes = [(2, d), (2, d), (1, nv), (1, nv), (1, GDN_DK), (d,), (MLA_KV_RANK,), (1, MLA_Q_RANK),
                    g_meta.shape, dconv.shape, (1, LANE)]
    small_part = _pack_small([jnp.concatenate([dpre0, dpre1], axis=0), jnp.concatenate([dpost0, dpost1], axis=0),
                              dalog[:, :nv], ddtb[:, :nv], doutn, dkvn, dkvl, dqln, g_meta, dconv, loss_part])
    small_tot = _all_sum_small(small_part, "small_sum")
    small_g = _unpack_small(small_tot, small_shapes)
    loss = small_g[-1][0, 0]
    chip = 2 * lax.axis_index("x") + lax.axis_index("y")
    meta_cols, conv_cols = meta_tokens.shape[1], gdn_conv_w.shape[2]
    g_meta_shard = lax.dynamic_slice(small_g[8], (0, chip * meta_cols), (small_g[8].shape[0], meta_cols))
    g_conv_shard = lax.dynamic_slice(small_g[9], (0, chip * conv_cols), (small_g[9].shape[0], conv_cols))

    d_flat, m_flat, v_flat = _adamw(_pack_shards(big_w[2:]), g_flat, _pack_shards(big_m[2:]), _pack_shards(big_v[2:]),
                                    "adamw_sharded")
    win_step = _adamw(gdn_w_in[0], g_win, m_gdn_w_in[0], v_gdn_w_in[0], "adamw_gdn_w_in")
    small_names = ["pre_norm", "post_norm", "gdn_a_log", "gdn_dt_bias", "gdn_out_norm", "kv_norm", "kv_latent_norm",
                   "mla_q_latent_norm", "meta_tokens", "gdn_conv_w"]
    small_w = [pre_norm, post_norm, gdn_a_log, gdn_dt_bias, gdn_out_norm, kv_norm, kv_latent_norm, mla_q_latent_norm,
               meta_tokens, gdn_conv_w]
    small_m = [m_pre_norm, m_post_norm, m_gdn_a_log, m_gdn_dt_bias, m_gdn_out_norm, m_kv_norm, m_kv_latent_norm,
               m_mla_q_latent_norm, m_meta_tokens, m_gdn_conv_w]
    small_v = [v_pre_norm, v_post_norm, v_gdn_a_log, v_gdn_dt_bias, v_gdn_out_norm, v_kv_norm, v_kv_latent_norm,
               v_mla_q_latent_norm, v_meta_tokens, v_gdn_conv_w]
    g_small_flat = _pack_small(small_g[:8] + [g_meta_shard, g_conv_shard])
    ds_flat, ms_flat, vs_flat = _adamw(_pack_small(small_w), g_small_flat, _pack_small(small_m), _pack_small(small_v),
                                       "adamw_replicated")

    def assemble(big_flat, small_flat, win):
        bigs = dict(zip(big_names[2:], [a.reshape(w.shape) for a, w in zip(
            _unpack_shards(big_flat, layout[2:]),
            [gdn_w_out, kv_w_down, kv_w_up, mla_w_in, mla_w_q_up, mla_w_out])]))
        smalls = dict(zip(small_names, _unpack_small(small_flat, [w.shape for w in small_w])))
        both = {**bigs, **smalls, "gdn_w_in": win[None]}
        order = ["meta_tokens", "pre_norm", "post_norm", "gdn_w_in", "gdn_conv_w", "gdn_a_log", "gdn_dt_bias",
                 "gdn_out_norm", "gdn_w_out", "kv_norm", "kv_w_down", "kv_latent_norm", "kv_w_up", "mla_w_in",
                 "mla_q_latent_norm", "mla_w_q_up", "mla_w_out"]
        return [both[n] for n in order]

    grads = assemble(g_flat, g_small_flat, g_win)
    deltas = assemble(d_flat, ds_flat, win_step[0])
    new_m = assemble(m_flat, ms_flat, win_step[1])
    new_v = assemble(v_flat, vs_flat, win_step[2])
    return (loss, grad_x, *grads, *deltas, *new_m, *new_v)
```

```python
import functools
import math

import jax
import jax.numpy as jnp
from jax import lax
from jax.experimental import pallas as pl
from jax.experimental.pallas import tpu as pltpu

F32 = jnp.float32
BF16 = jnp.bfloat16
MESH = pl.DeviceIdType.MESH

D_MODEL = 1024
N_META = 16
FRONT = 48
ROW0 = FRONT + N_META
ROW_ALIGN = 768
TR_FULL, TR_HALF, TR_QUARTER = ROW_ALIGN, ROW_ALIGN // 2, ROW_ALIGN // 4
NORM_EPS = 1e-6
LANE = 128

GDN_QK_HEADS = 8
GDN_V_HEADS = 16
GDN_DK = 128
GDN_CHUNK = 64
GDN_QK_W = 1024
GDN_V_W = 2048
GDN_CONV_W = 4096

MLA_HEADS = 16
MLA_NOPE = 128
MLA_ROPE = 64
MLA_QK = 192
MLA_Q_RANK = 256
MLA_KV_RANK = 128
ROPE_THETA = 10000.0

ADAM_LR = 0.001
ADAM_B1 = 0.9
ADAM_B2 = 0.999
ADAM_EPS = 1e-08
ADAM_WD = 0.01
ADAM_STEP = 10

VMEM_LIMIT_V7X = 56 * 1024 * 1024
NEG = -1e30

_NN = ((1,), (0,))
_NT = ((1,), (1,))
_TN = ((0,), (0,))
_HI = lax.Precision.HIGHEST
_X3 = lax.Precision.HIGH


def _pcall(body, **kw):
    return pl.pallas_call(body, **kw)


def _params(n_axes):
    return pltpu.CompilerParams(dimension_semantics=("arbitrary",) * n_axes, vmem_limit_bytes=VMEM_LIMIT_V7X)


def _dot(a, b, dims, prec=None):
    return lax.dot_general(a, b, (dims, ((), ())), precision=prec, preferred_element_type=F32)


def _bdot(a, b, dims):
    return _dot(a.astype(BF16), b.astype(BF16), dims)


def _hdot(a, b, dims=_NN):
    return _dot(a, b, dims, _HI)


def _fdot(a, b, dims):
    return _dot(a, b, dims)


SMALL_MATMUL_DIM = 256
SMALL_MATMUL_ROWS = 1408


def _tile(n):
    if n % ROW_ALIGN == 0:
        return ROW_ALIGN
    for t in (1024, 512, 256, 128):
        if n % t == 0:
            return t
    raise ValueError(n)


def _mm(a, b, mode, name, out_dtype=F32):
    if mode == "nn":
        (m, k), (k2, n) = a.shape, b.shape
    elif mode == "nt":
        (m, k), (n, k2) = a.shape, b.shape
    else:
        (k, m), (k2, n) = a.shape, b.shape
    assert k == k2, (a.shape, b.shape, mode)
    tm, tn, tk = _tile(m), _tile(n), _tile(k)
    if mode != "tn" and min(k, n) <= SMALL_MATMUL_DIM and m % SMALL_MATMUL_ROWS == 0:
        tm = SMALL_MATMUL_ROWS
    nk = k // tk
    dims = {"nn": _NN, "nt": _NT, "tn": _TN}[mode]

    def body(a_ref, b_ref, o_ref, acc):
        kk = pl.program_id(2)

        @pl.when(kk == 0)
        def _():
            acc[...] = jnp.zeros_like(acc)

        acc[...] += _bdot(a_ref[...], b_ref[...], dims)

        @pl.when(kk == nk - 1)
        def _():
            o_ref[...] = acc[...].astype(out_dtype)

    if mode == "tn":
        a_spec = pl.BlockSpec((tk, tm), lambda i, j, kk: (kk, i))
    else:
        a_spec = pl.BlockSpec((tm, tk), lambda i, j, kk: (i, kk))
    if mode == "nt":
        b_spec = pl.BlockSpec((tn, tk), lambda i, j, kk: (j, kk))
    else:
        b_spec = pl.BlockSpec((tk, tn), lambda i, j, kk: (kk, j))
    return _pcall(
        body, name=name, grid=(m // tm, n // tn, nk),
        in_specs=[a_spec, b_spec],
        out_specs=pl.BlockSpec((tm, tn), lambda i, j, kk: (i, j)),
        out_shape=jax.ShapeDtypeStruct((m, n), out_dtype),
        scratch_shapes=[pltpu.VMEM((tm, tn), F32)],
        compiler_params=_params(3),
    )(a, b)


class _In:
    def __init__(self, arr, kind="row", grouped=False, goff=0):
        self.arr, self.kind, self.grouped, self.goff = arr, kind, grouped, goff


class _Out:
    def __init__(self, kind, shape, dtype=F32, grouped=False):
        self.kind, self.shape, self.dtype, self.grouped = kind, shape, dtype, grouped


def _rowwise(name, fn, ins, outs, *, groups=1, tr=TR_HALF):
    lp = next(i.arr.shape[0] for i in ins if i.kind == "row")
    nr = lp // tr
    assert lp % tr == 0

    def in_spec(i):
        w = i.arr.shape[1]
        if i.kind == "row":
            if i.grouped:
                return pl.BlockSpec((tr, LANE), lambda g, r, o=i.goff: (r, g + o))
            return pl.BlockSpec((tr, w), lambda g, r: (r, 0))
        if i.grouped:
            return pl.BlockSpec((i.arr.shape[0], LANE), lambda g, r, o=i.goff: (0, g + o))
        return pl.BlockSpec(i.arr.shape, lambda g, r: (0, 0))

    def out_spec(o):
        if o.kind == "row":
            if o.grouped:
                return pl.BlockSpec((tr, LANE), lambda g, r: (r, g))
            assert groups == 1
            return pl.BlockSpec((tr, o.shape[1]), lambda g, r: (r, 0))
        if o.grouped:
            return pl.BlockSpec((o.shape[0], LANE), lambda g, r: (0, g))
        return pl.BlockSpec(o.shape, lambda g, r: (0, 0))

    n_in = len(ins)

    def body(*refs):
        g = pl.program_id(0)
        r = pl.program_id(1)
        ridx = r * tr + lax.broadcasted_iota(jnp.int32, (tr, 1), 0)
        res = fn(ridx, g, *[ref[...] for ref in refs[:n_in]])
        assert len(res) == len(outs), (name, len(res), len(outs))
        for o, ref, val in zip(outs, refs[n_in:], res):
            if o.kind == "row":
                ref[...] = val.astype(o.dtype)
            else:
                first = (r == 0) if o.grouped else jnp.logical_and(r == 0, g == 0)

                @pl.when(first)
                def _(ref=ref, val=val):
                    ref[...] = val.astype(F32)

                @pl.when(jnp.logical_not(first))
                def _(ref=ref, val=val):
                    ref[...] += val.astype(F32)

    res = _pcall(
        body, name=name, grid=(groups, nr),
        in_specs=[in_spec(i) for i in ins],
        out_specs=[out_spec(o) for o in outs],
        out_shape=[jax.ShapeDtypeStruct(o.shape, o.dtype) for o in outs],
        compiler_params=_params(2),
    )(*[i.arr for i in ins])
    return res


def _rowwise_vjp(name, fn, ins, cots, diff, *, groups=1, tr=TR_HALF):
    n_in = len(ins)
    grouped = groups > 1
    cot_ins = []
    counts = []
    for arrs in cots:
        counts.append(len(arrs))
        for a in arrs:
            cot_ins.append(_In(a, "row", grouped=grouped and a.shape[1] > LANE))
    lp = next(i.arr.shape[0] for i in ins if i.kind == "row")
    outs = []
    for d in diff:
        i = ins[d]
        if i.kind == "row":
            w = groups * LANE if i.grouped else i.arr.shape[1]
            outs.append(_Out("row", (lp, w), F32, grouped=i.grouped))
        else:
            outs.append(_Out("acc", i.arr.shape, F32, grouped=i.grouped))

    def bfn(ridx, g, *allvals):
        vals = list(allvals[:n_in])
        cvals = allvals[n_in:]

        def f(*dv):
            full = list(vals)
            for i, v in zip(diff, dv):
                full[i] = v
            return tuple(fn(ridx, g, *full))

        primal, vjp = jax.vjp(f, *[vals[i].astype(F32) for i in diff])
        cts = []
        pos = 0
        for k, cnt in enumerate(counts):
            if cnt == 0:
                cts.append(jnp.zeros_like(primal[k]))
            else:
                c = cvals[pos].astype(F32)
                for extra in cvals[pos + 1:pos + cnt]:
                    c = c + extra.astype(F32)
                w = primal[k].shape[1]
                if c.shape[1] != w:
                    c = functools.reduce(jnp.add, [c[:, i * w:(i + 1) * w] for i in range(c.shape[1] // w)])
                cts.append(c.astype(primal[k].dtype))
            pos += cnt
        return vjp(tuple(cts))

    return _rowwise(name, bfn, list(ins) + cot_ins, outs, groups=groups, tr=tr)


def _rms(x, g):
    return x * lax.rsqrt(jnp.mean(x * x, axis=-1, keepdims=True) + NORM_EPS) * g


def _silu(x):
    return x * jax.nn.sigmoid(x)


def _softplus(x):
    return jnp.maximum(x, 0.0) + jnp.log(1.0 + jnp.exp(-jnp.abs(x)))


def _swap_halves(x):
    lane = lax.broadcasted_iota(jnp.int32, x.shape, x.ndim - 1)
    return jnp.where(lane < 32, pltpu.roll(x, LANE - 32, x.ndim - 1), pltpu.roll(x, 32, x.ndim - 1))


@jax.custom_vjp
def _rope(x, c, s):
    return x * c + _swap_halves(x) * s


def _rope_fwd(x, c, s):
    return _rope(x, c, s), (c, s)


def _rope_bwd(res, dy):
    c, s = res
    return dy * c + _swap_halves(dy * s), jnp.zeros_like(c), jnp.zeros_like(s)


_rope.defvjp(_rope_fwd, _rope_bwd)


def _conv_post(c, g):
    s = _silu(c)
    n = s * lax.rsqrt(jnp.sum(s * s, axis=-1, keepdims=True) + NORM_EPS)
    return jnp.where(g < GDN_QK_HEADS, n * (GDN_DK ** -0.5), jnp.where(g < 2 * GDN_QK_HEADS, n, s))


def _conv_taps(xe, w):
    c = xe[8:] * w[3]
    for s in (1, 2, 3):
        c = c + pltpu.roll(xe, s, 0)[8:] * w[3 - s]
    return c


CONV_LANES = 512
CONV_HEADS = CONV_LANES // LANE


def _conv_post_block(c, g):
    return jnp.concatenate([_conv_post(c[:, i * LANE:(i + 1) * LANE], g * CONV_HEADS + i)
                            for i in range(CONV_HEADS)], axis=1)


def _conv_fwd(x, w, name, tr=TR_FULL):
    lp, width = x.shape
    cl = CONV_LANES
    nr = lp // tr

    def body(x_ref, prev_ref, w_ref, o_ref):
        g = pl.program_id(0)
        r = pl.program_id(1)
        prev = jnp.where(r > 0, prev_ref[...], 0.0)
        xe = jnp.concatenate([prev, x_ref[...]], axis=0)
        o_ref[...] = _conv_post_block(_conv_taps(xe, [w_ref[t:t + 1, :] for t in range(4)]), g)

    return _pcall(
        body, name=name, grid=(width // cl, nr),
        in_specs=[pl.BlockSpec((tr, cl), lambda g, r: (r, g)),
                  pl.BlockSpec((8, cl), lambda g, r: (jnp.maximum(r * (tr // 8) - 1, 0), g)),
                  pl.BlockSpec((4, cl), lambda g, r: (0, g))],
        out_specs=pl.BlockSpec((tr, cl), lambda g, r: (r, g)),
        out_shape=jax.ShapeDtypeStruct((lp, width), F32),
        compiler_params=_params(2),
    )(x, x, w)


def _conv_bwd(x, w, dq, dk, dv, name, tr=TR_FULL):
    lp, width = x.shape
    cl = CONV_LANES
    nr = lp // tr
    last8 = lp // 8 - 1
    nq = GDN_QK_W // cl

    def body(x_ref, prev_ref, next_ref, w_ref, q_ref, k_ref, v_ref, q_n, k_n, v_n, dx_ref, dw_ref):
        g = pl.program_id(0)
        r = pl.program_id(1)
        w = [w_ref[t:t + 1, :] for t in range(4)]
        not_last = r < nr - 1

        def pick(a, b, c):
            return jnp.where(g < nq, a[...], jnp.where(g < 2 * nq, b[...], c[...]))

        dy = pick(q_ref, k_ref, v_ref)
        dyn = jnp.where(not_last, pick(q_n, k_n, v_n), 0.0)
        prev = jnp.where(r > 0, prev_ref[...], 0.0)
        nxt = jnp.where(not_last, next_ref[...], 0.0)
        xe = jnp.concatenate([prev, x_ref[...], nxt], axis=0)
        ce = _conv_taps(xe, w)
        _, vjp = jax.vjp(lambda c: _conv_post_block(c, g), ce)
        (dce,) = vjp(jnp.concatenate([dy, dyn], axis=0))
        n = tr + 8
        dx = dce * w[3]
        for s in (1, 2, 3):
            dx = dx + pltpu.roll(dce, n - s, 0) * w[3 - s]
        dx_ref[...] = dx[:tr]
        dc = dce[:tr]
        row4 = lax.broadcasted_iota(jnp.int32, (4, cl), 0)
        dw = jnp.zeros((4, cl), F32)
        for s in (0, 1, 2, 3):
            xs = xe[8:8 + tr] if s == 0 else pltpu.roll(xe, s, 0)[8:8 + tr]
            dw = dw + jnp.where(row4 == 3 - s, jnp.sum(dc * xs, axis=0, keepdims=True), 0.0)

        @pl.when(r == 0)
        def _():
            dw_ref[...] = dw

        @pl.when(r > 0)
        def _():
            dw_ref[...] += dw

    def col_q(g):
        return jnp.minimum(g, nq - 1), g < nq

    def col_k(g):
        return jnp.clip(g - nq, 0, nq - 1), jnp.logical_and(g >= nq, g < 2 * nq)

    def col_v(g):
        return jnp.maximum(g - 2 * nq, 0), g >= 2 * nq

    def blk(colf):
        def index(g, r):
            col, used = colf(g)
            return jnp.where(used, r, 0), col
        return pl.BlockSpec((tr, cl), index)

    def nblk(colf):
        def index(g, r):
            col, used = colf(g)
            return jnp.where(used, jnp.minimum((r + 1) * (tr // 8), last8), 0), col
        return pl.BlockSpec((8, cl), index)

    return _pcall(
        body, name=name, grid=(width // cl, nr),
        in_specs=[pl.BlockSpec((tr, cl), lambda g, r: (r, g)),
                  pl.BlockSpec((8, cl), lambda g, r: (jnp.maximum(r * (tr // 8) - 1, 0), g)),
                  pl.BlockSpec((8, cl), lambda g, r: (jnp.minimum((r + 1) * (tr // 8), last8), g)),
                  pl.BlockSpec((4, cl), lambda g, r: (0, g)),
                  blk(col_q), blk(col_k), blk(col_v), nblk(col_q), nblk(col_k), nblk(col_v)],
        out_specs=[pl.BlockSpec((tr, cl), lambda g, r: (r, g)),
                   pl.BlockSpec((4, cl), lambda g, r: (0, g))],
        out_shape=[jax.ShapeDtypeStruct((lp, width), F32), jax.ShapeDtypeStruct((4, width), F32)],
        compiler_params=_params(2),
    )(x, x, x, w, dq, dk, dv, dq, dk, dv)


def _bmm(a, b, dims, prec=None):
    (ca,), (cb,) = dims
    return lax.dot_general(a, b, (((ca + 1,), (cb + 1,)), ((0,), (0,))), precision=prec,
                           preferred_element_type=F32)


def _inv_impl(m):
    c = m.shape[-1]
    ii = lax.broadcasted_iota(jnp.int32, (c, c), 0)
    jj = lax.broadcasted_iota(jnp.int32, (c, c), 1)
    eye = (ii == jj).astype(F32)

    def same_block(shift):
        return (ii >> shift) == (jj >> shift)

    n1 = jnp.where(same_block(3), -m, 0.0)
    n2 = _bmm(n1, n1, _NN, _X3)
    n4 = _bmm(n2, n2, _NN, _X3)
    d = _bmm(_bmm(eye + n1, eye + n2, _NN, _X3), eye + n4, _NN, _X3)
    shift = 3
    while (1 << shift) < c:
        low = jnp.where(jnp.logical_and(same_block(shift + 1), jnp.logical_not(same_block(shift))), m, 0.0)
        d = d - _bmm(d, _bmm(low, d, _NN, _X3), _NN, _X3)
        shift += 1
    return d


@jax.custom_vjp
def _inv_unit_lower(m):
    return _inv_impl(m)


def _inv_f(m):
    t = _inv_impl(m)
    return t, t


def _inv_b(t, dt):
    c = t.shape[-1]
    ii = lax.broadcasted_iota(jnp.int32, (c, c), 0)
    jj = lax.broadcasted_iota(jnp.int32, (c, c), 1)
    gm = _bmm(t, _bmm(dt, t, _NT, _X3), _TN, _X3)
    return (jnp.where(ii > jj, -gm, 0.0),)


_inv_unit_lower.defvjp(_inv_f, _inv_b)


@jax.custom_vjp
def _inv_known(m, t):
    return t


def _inv_known_f(m, t):
    return t, t


def _inv_known_b(t, dt):
    return _inv_b(t, dt) + (jnp.zeros_like(t),)


_inv_known.defvjp(_inv_known_f, _inv_known_b)


GDN_HEADS_PER_STEP = 16


def _gdn_group(q, k, v, beta_blk, gc_blk, states, h0, t_known=None):
    hp = GDN_HEADS_PER_STEP
    c = q.shape[0]
    lane = lax.broadcasted_iota(jnp.int32, (1, LANE), 1)
    row8 = lax.broadcasted_iota(jnp.int32, (max(8, hp), LANE), 0)
    lane8 = lax.broadcasted_iota(jnp.int32, (max(8, hp), LANE), 1)
    gcr_all = _hdot((lane8 == h0 + row8).astype(F32), gc_blk, _NT)
    betas, gccs = [], []
    for i in range(hp):
        onehot = (lane == h0 + i).astype(F32)
        betas.append(jnp.sum(beta_blk * onehot, axis=1, keepdims=True))
        gccs.append(jnp.sum(gc_blk * onehot, axis=1, keepdims=True))
    def stack(xs):
        return jnp.concatenate([x[None] for x in xs], axis=0)

    beta = stack(betas)
    gcc = stack(gccs)
    gcr = stack([gcr_all[i:i + 1] for i in range(hp)])
    qh = stack([q[:, (i // 2) * LANE:(i // 2 + 1) * LANE] for i in range(hp)])
    kh = stack([k[:, (i // 2) * LANE:(i // 2 + 1) * LANE] for i in range(hp)])
    vh = stack([v[:, i * LANE:(i + 1) * LANE] for i in range(hp)])
    state = stack(states)
    ii = lax.broadcasted_iota(jnp.int32, (c, c), 0)
    jj = lax.broadcasted_iota(jnp.int32, (c, c), 1)
    incl = ii >= jj
    dec = jnp.where(incl, jnp.exp(jnp.where(incl, gcc - gcr, 0.0)), 0.0)
    eg = jnp.exp(gcc)
    m = _bmm(kh, kh, _NT) * beta * jnp.where(ii > jj, dec, 0.0)
    t = _inv_unit_lower(m) if t_known is None else _inv_known(m, t_known)
    u = _bmm(t, vh * beta, _NN, _X3)
    w = _bmm(t, kh * (beta * eg), _NN, _X3)
    attn = _bmm(qh, kh, _NT) * dec
    rows = lax.broadcasted_iota(jnp.int32, (c, 1), 0)
    gl = jnp.sum(jnp.where(rows == c - 1, gcc, 0.0), axis=1, keepdims=True)
    v_new = u - _bmm(w, state, _NN)
    o = _bmm(qh * eg, state, _NN) + _bmm(attn, v_new, _NN)
    new_state = state * jnp.exp(gl) + _bmm(kh * jnp.exp(gl - gcc), v_new, _TN)
    return jnp.concatenate([o[i] for i in range(hp)], axis=1), tuple(new_state[i] for i in range(hp)), t


def _gdn_specs(nc, rev):
    def cidx(n):
        return (nc - 1 - n) if rev else n
    hp = GDN_HEADS_PER_STEP
    nqk = GDN_QK_HEADS
    c = GDN_CHUNK
    nq = 2 * nqk // hp
    q_spec = pl.BlockSpec((c, hp // 2 * LANE), lambda n, g: (cidx(n), g))
    k_spec = pl.BlockSpec((c, hp // 2 * LANE), lambda n, g: (cidx(n), nq + g))
    v_spec = pl.BlockSpec((c, hp * LANE), lambda n, g: (cidx(n), nq + g))
    s_spec = pl.BlockSpec((c, LANE), lambda n, g: (cidx(n), 0))
    o_spec = pl.BlockSpec((c, hp * LANE), lambda n, g: (cidx(n), g))
    ck_spec = pl.BlockSpec((hp, 1, GDN_DK, LANE), lambda n, g: (g, cidx(n), 0, 0))
    return q_spec, k_spec, v_spec, s_spec, o_spec, ck_spec


def _gdn_t_spec(nc, rev):
    return pl.BlockSpec((GDN_HEADS_PER_STEP, 1, GDN_CHUNK, GDN_CHUNK),
                        lambda n, g: (g, (nc - 1 - n) if rev else n, 0, 0))


def _gdn_fwd(qkv, beta, gc, shard, name):
    lp = qkv.shape[0]
    nc = lp // GDN_CHUNK
    nh = GDN_V_HEADS
    hp = GDN_HEADS_PER_STEP
    ng = nh // hp
    q_spec, k_spec, v_spec, s_spec, o_spec, ck_spec = _gdn_specs(nc, False)

    def body(q_ref, k_ref, v_ref, b_ref, g_ref, x_ref, o_ref, ck_ref, t_ref, all_ref,
             state, send_sems, recv_sems, local_sem):
        n = pl.program_id(0)
        g = pl.program_id(1)
        x, y, c = _my_place()

        def local_copy():
            return pltpu.make_async_copy(x_ref, all_ref.at[2 * x + y], local_sem)

        def remote_copy(k, px, py, slot):
            return pltpu.make_async_remote_copy(
                src_ref=x_ref, dst_ref=all_ref.at[slot], send_sem=send_sems.at[k], recv_sem=recv_sems.at[k],
                device_id=(px, py, c), device_id_type=MESH)

        @pl.when(jnp.logical_and(n == 0, g == 0))
        def _():
            local_copy().start()
            for k, (px, py) in enumerate(_other_chips(x, y)):
                remote_copy(k, px, py, 2 * x + y).start()

        @pl.when(n == 0)
        def _():
            for i in range(hp):
                state[g * hp + i] = jnp.zeros((GDN_DK, LANE), F32)

        states = tuple(state[g * hp + i] for i in range(hp))
        for i in range(hp):
            ck_ref[i, 0] = states[i]
        o, new_states, t = _gdn_group(q_ref[...], k_ref[...], v_ref[...], b_ref[...], g_ref[...], states, g * hp)
        o_ref[...] = o
        t_ref[:, 0] = t
        for i in range(hp):
            state[g * hp + i] = new_states[i]

        @pl.when(jnp.logical_and(n == nc - 1, g == ng - 1))
        def _():
            for k, (px, py) in enumerate(_other_chips(x, y)):
                remote_copy(k, px, py, 2 * px + py).wait_recv()
            for k, (px, py) in enumerate(_other_chips(x, y)):
                remote_copy(k, px, py, 2 * x + y).wait_send()
            local_copy().wait()

    return _pcall(
        body, name=name, grid=(nc, ng),
        in_specs=[q_spec, k_spec, v_spec, s_spec, s_spec, _ANY],
        out_specs=[o_spec, ck_spec, _gdn_t_spec(nc, False), _ANY],
        out_shape=[jax.ShapeDtypeStruct((lp, GDN_V_W), F32),
                   jax.ShapeDtypeStruct((nh, nc, GDN_DK, LANE), F32),
                   jax.ShapeDtypeStruct((nh, nc, GDN_CHUNK, GDN_CHUNK), F32),
                   jax.ShapeDtypeStruct((4,) + shard.shape, shard.dtype)],
        scratch_shapes=[pltpu.VMEM((nh, GDN_DK, LANE), F32), pltpu.SemaphoreType.DMA((3,)),
                        pltpu.SemaphoreType.DMA((3,)), pltpu.SemaphoreType.DMA],
        compiler_params=_params(2),
    )(qkv, qkv, qkv, beta, gc, shard)


def _gdn_bwd(qkv, beta, gc, ckpt, t_saved, do, parts, name):
    lp = qkv.shape[0]
    nc = lp // GDN_CHUNK
    nh = GDN_V_HEADS
    hp = GDN_HEADS_PER_STEP
    ng = nh // hp
    q_spec, k_spec, v_spec, s_spec, o_spec, ck_spec = _gdn_specs(nc, True)

    def body(q_ref, k_ref, v_ref, b_ref, g_ref, ck_ref, t_ref, do_ref, p_ref,
             dq_ref, dk_ref, dv_ref, db_ref, dg_ref, from_ref, dstate, send_sems, recv_sems, local_sem):
        n = pl.program_id(0)
        g = pl.program_id(1)
        x, y, c = _my_place()
        me = 2 * x + y

        def local_copy():
            return pltpu.make_async_copy(p_ref.at[me], from_ref.at[me], local_sem)

        def remote_copy(k, px, py, src_slot, dst_slot):
            return pltpu.make_async_remote_copy(
                src_ref=p_ref.at[src_slot], dst_ref=from_ref.at[dst_slot], send_sem=send_sems.at[k],
                recv_sem=recv_sems.at[k], device_id=(px, py, c), device_id_type=MESH)

        @pl.when(jnp.logical_and(n == 0, g == 0))
        def _():
            local_copy().start()
            for k, (px, py) in enumerate(_other_chips(x, y)):
                remote_copy(k, px, py, 2 * px + py, me).start()

        @pl.when(jnp.logical_and(n == nc - 1, g == ng - 1))
        def _():
            for k, (px, py) in enumerate(_other_chips(x, y)):
                remote_copy(k, px, py, me, 2 * px + py).wait_recv()
            for k, (px, py) in enumerate(_other_chips(x, y)):
                remote_copy(k, px, py, 2 * px + py, me).wait_send()
            local_copy().wait()

        @pl.when(n == 0)
        def _():
            for i in range(hp):
                dstate[g * hp + i] = jnp.zeros((GDN_DK, LANE), F32)

        states = tuple(ck_ref[i, 0] for i in range(hp))
        t_known = t_ref[:, 0]
        _, vjp = jax.vjp(lambda q, k, v, b, gg, s: _gdn_group(q, k, v, b, gg, s, g * hp, t_known)[:2],
                         q_ref[...], k_ref[...], v_ref[...], b_ref[...], g_ref[...], states)
        dq, dk, dv, db, dg, ds = vjp((do_ref[...], tuple(dstate[g * hp + i] for i in range(hp))))
        dq_ref[...] = dq
        dk_ref[...] = dk
        dv_ref[...] = dv
        for i in range(hp):
            dstate[g * hp + i] = ds[i]

        @pl.when(g == 0)
        def _():
            db_ref[...] = db
            dg_ref[...] = dg

        @pl.when(g > 0)
        def _():
            db_ref[...] += db
            dg_ref[...] += dg

    qk_shape = jax.ShapeDtypeStruct((lp, GDN_QK_W), F32)
    big = jax.ShapeDtypeStruct((lp, GDN_V_W), F32)
    small = jax.ShapeDtypeStruct((lp, LANE), F32)
    dq_spec = pl.BlockSpec((GDN_CHUNK, hp // 2 * LANE), lambda n, g: (nc - 1 - n, g))
    return _pcall(
        body, name=name, grid=(nc, ng),
        in_specs=[q_spec, k_spec, v_spec, s_spec, s_spec, ck_spec, _gdn_t_spec(nc, True), o_spec, _ANY],
        out_specs=[dq_spec, dq_spec, o_spec, s_spec, s_spec, _ANY],
        out_shape=[qk_shape, qk_shape, big, small, small, jax.ShapeDtypeStruct(parts.shape, parts.dtype)],
        scratch_shapes=[pltpu.VMEM((nh, GDN_DK, LANE), F32), pltpu.SemaphoreType.DMA((3,)),
                        pltpu.SemaphoreType.DMA((3,)), pltpu.SemaphoreType.DMA],
        compiler_params=_params(2),
    )(qkv, qkv, qkv, beta, gc, ckpt, t_saved, do, parts)


LOG2E = 1.4426950408889634
LN2 = 0.6931471805599453
Q_PRESCALE = MLA_QK ** -0.5 * LOG2E


ATT_SUB = 256
ATT_HEADS_PER_STEP = 8
ATT_BWD_HEADS_PER_STEP = 2


def _att_mask(i, j, tb, transposed):
    r = lax.broadcasted_iota(jnp.int32, (tb, tb), 0)
    c = lax.broadcasted_iota(jnp.int32, (tb, tb), 1)
    qpos, kpos = (i * tb + c, j * tb + r) if transposed else (i * tb + r, j * tb + c)
    return jnp.logical_and(kpos <= qpos, kpos >= FRONT)


def _causal_pairs(nb, by_key):
    if by_key:
        pairs = [(i, j) for j in range(nb) for i in range(j, nb)]
    else:
        pairs = [(i, j) for i in range(nb) for j in range(i + 1)]
    return jnp.array([p[0] for p in pairs], jnp.int32), jnp.array([p[1] for p in pairs], jnp.int32)


def _masked_and_plain(i, j, step):
    edge = jnp.logical_or(j == i, j == 0)

    @pl.when(jnp.logical_and(edge, j <= i))
    def _():
        step(True)

    @pl.when(jnp.logical_and(jnp.logical_not(edge), j < i))
    def _():
        step(False)


def _cat(a_ref, b_ref):
    return jnp.concatenate([a_ref[...], b_ref[...]], axis=1)


def _flash_fwd(qn, qr, kn, kr, v, name, tb=ROW_ALIGN):
    lp = qn.shape[0]
    nb = lp // tb
    nh = MLA_HEADS
    hp = ATT_HEADS_PER_STEP
    qi, kj = _causal_pairs(nb, by_key=False)

    def body(qi_ref, kj_ref, qn_ref, qr_ref, kn_ref, kr_ref, v_ref, o_ref, lse_ref, m_s, l_s, acc):
        t = pl.program_id(1)
        i, j = qi_ref[t], kj_ref[t]

        @pl.when(j == 0)
        def _():
            m_s[...] = jnp.full_like(m_s, NEG)
            l_s[...] = jnp.zeros_like(l_s)
            acc[...] = jnp.zeros_like(acc)

        def step(masked):
            n_sub = tb // ATT_SUB
            kr = kr_ref[...]
            for e in range(hp):
                lanes = pl.ds(e * LANE, LANE)
                k = jnp.concatenate([kn_ref[:, lanes], kr], axis=1)
                v = v_ref[:, lanes]

                def scores(r, lanes=lanes, k=k):
                    rows = pl.ds(r * ATT_SUB, ATT_SUB)
                    return _dot(jnp.concatenate([qn_ref[rows, lanes], qr_ref[rows, lanes]], axis=1), k, _NT)

                s_next = scores(0)
                for r in range(n_sub):
                    s = s_next
                    if r + 1 < n_sub:
                        s_next = scores(r + 1)
                    rows = pl.ds(r * ATT_SUB, ATT_SUB)
                    if masked:
                        qpos = i * tb + r * ATT_SUB + lax.broadcasted_iota(jnp.int32, (ATT_SUB, tb), 0)
                        kpos = j * tb + lax.broadcasted_iota(jnp.int32, (ATT_SUB, tb), 1)
                        s = jnp.where(jnp.logical_and(kpos <= qpos, kpos >= FRONT), s, NEG)
                    m_old = m_s[e, rows, :]
                    m_new = jnp.maximum(m_old, jnp.max(s, axis=1, keepdims=True))
                    alpha = jnp.exp2(m_old - m_new)
                    p = jnp.exp2(s - m_new)
                    l_s[e, rows, :] = alpha * l_s[e, rows, :] + jnp.sum(p, axis=1, keepdims=True)
                    acc[e, rows, :] = alpha * acc[e, rows, :] + _dot(p.astype(BF16), v, _NN)
                    m_s[e, rows, :] = m_new

        _masked_and_plain(i, j, step)

        @pl.when(j == i)
        def _():
            for e in range(hp):
                lanes = pl.ds(e * LANE, LANE)
                o_ref[:, lanes] = acc[e] / l_s[e]
                lse_ref[:, lanes] = jnp.broadcast_to(m_s[e] + jnp.log(l_s[e]) * LOG2E, (tb, LANE))

    qspec = pl.BlockSpec((tb, hp * LANE), lambda h, t, qi_, kj_: (qi_[t], h))
    kspec = pl.BlockSpec((tb, hp * LANE), lambda h, t, qi_, kj_: (kj_[t], h))
    krspec = pl.BlockSpec((tb, LANE), lambda h, t, qi_, kj_: (kj_[t], 0))
    shp = jax.ShapeDtypeStruct((lp, nh * LANE), F32)
    return _pcall(
        body, name=name, out_shape=[shp, shp],
        grid_spec=pltpu.PrefetchScalarGridSpec(
            num_scalar_prefetch=2, grid=(nh // hp, qi.shape[0]),
            in_specs=[qspec, qspec, kspec, krspec, kspec], out_specs=[qspec, qspec],
            scratch_shapes=[pltpu.VMEM((hp, tb, 1), F32), pltpu.VMEM((hp, tb, 1), F32),
                            pltpu.VMEM((hp, tb, LANE), F32)]),
        compiler_params=_params(2),
    )(qi, kj, qn, qr, kn, kr, v)


def _flash_bwd(qn, qr, kn, kr, v, o, do, lse, name, tb=ROW_ALIGN):
    lp = qn.shape[0]
    nb = lp // tb
    nh = MLA_HEADS
    hp = ATT_BWD_HEADS_PER_STEP
    qi, kj = _causal_pairs(nb, by_key=True)
    n_pairs = qi.shape[0]
    knt, krt = kn.T, kr.T

    def body(qi_ref, kj_ref, qn_ref, qr_ref, kn_ref, kr_ref, knt_ref, krt_ref, v_ref, o_ref, do_ref, lse_ref,
             dqnt_hbm, dqrt_hbm, dkn_ref, dkr_ref, dv_ref, dk_acc, dv_acc, dqn_acc, dqr_acc, out_sems):
        g = pl.program_id(0)
        t = pl.program_id(1)
        i, j = qi_ref[t], kj_ref[t]

        @pl.when(t == 0)
        def _():
            dqn_acc[...] = jnp.zeros_like(dqn_acc)
            dqr_acc[...] = jnp.zeros_like(dqr_acc)

        @pl.when(i == j)
        def _():
            dk_acc[...] = jnp.zeros_like(dk_acc)
            dv_acc[...] = jnp.zeros_like(dv_acc)

        def step(masked):
            kr = kr_ref[...]
            krt_blk = krt_ref[...]
            lane = lax.broadcasted_iota(jnp.int32, (8, LANE), 1)
            for e in range(hp):
                lanes = pl.ds(e * LANE, LANE)
                q = jnp.concatenate([qn_ref[:, lanes], qr_ref[:, lanes]], axis=1)
                st = _dot(jnp.concatenate([kn_ref[:, lanes], kr], axis=1), q, _NT)
                if masked:
                    st = jnp.where(_att_mask(i, j, tb, True), st, NEG)
                do_blk = do_ref[:, lanes]
                lse_row = _hdot((lane == 0).astype(F32), lse_ref[:, lanes], _NT)[0:1]
                delta_row = _hdot(jnp.ones((8, LANE), F32), do_blk * o_ref[:, lanes], _NT)[0:1]
                pt = jnp.exp2(st - lse_row)
                do_b = do_blk.astype(BF16)
                dv_acc[e] += _dot(pt.astype(BF16), do_b, _NN)
                dpt = _dot(v_ref[:, lanes], do_b, _NT)
                dst = (pt * (dpt - delta_row)).astype(BF16)
                dk_acc[e] += _dot(dst, q, _NN)
                dqn_acc[e * nb + i] += _dot(knt_ref[pl.ds(e * LANE, LANE), :], dst, _NN) * LN2
                dqr_acc[e * nb + i] += _dot(krt_blk, dst, _NN) * LN2

        _masked_and_plain(i, j, step)

        @pl.when(i == nb - 1)
        def _():
            for e in range(hp):
                lanes = pl.ds(e * LANE, LANE)
                dkn_ref[:, lanes] = dk_acc[e, :, :LANE] * LN2
                dkr_ref[:, lanes] = dk_acc[e, :, LANE:] * LN2
                dv_ref[:, lanes] = dv_acc[e]

        @pl.when(t == n_pairs - 1)
        def _():
            dst_rows = pl.ds(g * (hp * nb), hp * nb)
            cn = pltpu.make_async_copy(dqn_acc, dqnt_hbm.at[dst_rows], out_sems.at[0])
            cr = pltpu.make_async_copy(dqr_acc, dqrt_hbm.at[dst_rows], out_sems.at[1])
            cn.start()
            cr.start()
            cn.wait()
            cr.wait()

    qspec = pl.BlockSpec((tb, hp * LANE), lambda h, t, qi_, kj_: (qi_[t], h))
    kspec = pl.BlockSpec((tb, hp * LANE), lambda h, t, qi_, kj_: (kj_[t], h))
    krspec = pl.BlockSpec((tb, LANE), lambda h, t, qi_, kj_: (kj_[t], 0))
    ktspec = pl.BlockSpec((hp * LANE, tb), lambda h, t, qi_, kj_: (h, kj_[t]))
    krtspec = pl.BlockSpec((LANE, tb), lambda h, t, qi_, kj_: (0, kj_[t]))
    shp = jax.ShapeDtypeStruct((lp, nh * LANE), F32)
    dqt_shape = jax.ShapeDtypeStruct((nh * nb, LANE, tb), F32)
    dqnt, dqrt, dkn, dkr, dv = _pcall(
        body, name=name, out_shape=[dqt_shape, dqt_shape, shp, shp, shp],
        grid_spec=pltpu.PrefetchScalarGridSpec(
            num_scalar_prefetch=2, grid=(nh // hp, n_pairs),
            in_specs=[qspec, qspec, kspec, krspec, ktspec, krtspec, kspec, qspec, qspec, qspec],
            out_specs=[_ANY, _ANY, kspec, kspec, kspec],
            scratch_shapes=[pltpu.VMEM((hp, tb, 2 * LANE), F32), pltpu.VMEM((hp, tb, LANE), F32),
                            pltpu.VMEM((hp * nb, LANE, tb), F32), pltpu.VMEM((hp * nb, LANE, tb), F32),
                            pltpu.SemaphoreType.DMA((2,))]),
        compiler_params=_params(2),
    )(qi, kj, qn, qr, kn, kr, knt, krt, v, o, do, lse)

    def rows_major(a):
        return a.reshape(nh, nb, LANE, tb).transpose(1, 3, 0, 2).reshape(lp, nh * LANE)

    return rows_major(dqnt), rows_major(dqrt), dkn, dkr, dv


ELEMENTWISE_BLOCK_BYTES = 1 << 20


def _row_tile(rows, width, copies=1):
    for t in (1024, 512, 256, 128, 64, 32, 16, 8):
        if rows % t == 0 and t * width * 4 * copies <= ELEMENTWISE_BLOCK_BYTES:
            return t
    return rows


def _adamw(w, g, m, v, name):
    rows, width = w.shape
    tr = _row_tile(rows, width)

    def body(w_ref, g_ref, m_ref, v_ref, d_ref, nm_ref, nv_ref):
        gg = g_ref[...]
        nm = ADAM_B1 * m_ref[...] + (1.0 - ADAM_B1) * gg
        nv = ADAM_B2 * v_ref[...] + (1.0 - ADAM_B2) * jnp.square(gg)
        m_hat = nm / (1.0 - ADAM_B1 ** ADAM_STEP)
        v_hat = nv / (1.0 - ADAM_B2 ** ADAM_STEP)
        d_ref[...] = -ADAM_LR * (m_hat / (jnp.sqrt(v_hat) + ADAM_EPS) + ADAM_WD * w_ref[...])
        nm_ref[...] = nm
        nv_ref[...] = nv

    spec = pl.BlockSpec((tr, width), lambda r: (r, 0))
    shp = jax.ShapeDtypeStruct((rows, width), F32)
    return _pcall(body, name=name, grid=(rows // tr,), in_specs=[spec] * 4, out_specs=[spec] * 3,
                  out_shape=[shp] * 3, compiler_params=_params(1))(w, g, m, v)


def _add_pair(a, b, name):
    s, rows, width = b.shape
    tr = _row_tile(rows, width)
    nt = rows // tr

    def body(c_ref, a_ref, b_ref, o_ref):
        o_ref[...] = a_ref[...] + b_ref[...]

    spec = pl.BlockSpec((1, tr, width), lambda i, r, c_ref: (i, r, 0))
    return _pcall(
        body, name=name, out_shape=jax.ShapeDtypeStruct(b.shape, F32),
        grid_spec=pltpu.PrefetchScalarGridSpec(
            num_scalar_prefetch=1, grid=(s, nt),
            in_specs=[pl.BlockSpec((1, tr, width), lambda i, r, c_ref: (i, c_ref[0] * nt + r, 0)), spec],
            out_specs=spec),
        compiler_params=_params(2),
    )(_core_index(), a, b)


def _sum_slots(a, name):
    s, rows, width = a.shape
    tr = _row_tile(rows, width, copies=s)

    def body(a_ref, o_ref):
        tot = a_ref[0]
        for k in range(1, s):
            tot = tot + a_ref[k]
        o_ref[...] = tot

    return _pcall(body, name=name, grid=(rows // tr,),
                  in_specs=[pl.BlockSpec((s, tr, width), lambda r: (0, r, 0))],
                  out_specs=pl.BlockSpec((tr, width), lambda r: (r, 0)),
                  out_shape=jax.ShapeDtypeStruct((rows, width), F32), compiler_params=_params(1))(a)


_ANY = pl.BlockSpec(memory_space=pl.ANY)


def _my_place():
    return lax.axis_index("x"), lax.axis_index("y"), lax.axis_index("c")


def _core_index():
    return lax.axis_index("c").astype(jnp.int32).reshape(1)


def _other_chips(x, y):
    return [(1 - x, y), (x, 1 - y), (1 - x, 1 - y)]


def _gather_shards(flat, name):
    rows, width = flat.shape

    def body(x_ref, out_ref, send_sems, recv_sems, local_sem):
        x, y, c = _my_place()
        mine = pltpu.make_async_copy(x_ref, out_ref.at[2 * x + y], local_sem)
        mine.start()
        sends = []
        for k, (px, py) in enumerate(_other_chips(x, y)):
            cp = pltpu.make_async_remote_copy(
                src_ref=x_ref, dst_ref=out_ref.at[2 * x + y], send_sem=send_sems.at[k], recv_sem=recv_sems.at[k],
                device_id=(px, py, c), device_id_type=MESH)
            cp.start()
            sends.append(cp)
        for k, (px, py) in enumerate(_other_chips(x, y)):
            pltpu.make_async_remote_copy(
                src_ref=x_ref, dst_ref=out_ref.at[2 * px + py], send_sem=send_sems.at[k], recv_sem=recv_sems.at[k],
                device_id=(px, py, c), device_id_type=MESH).wait_recv()
        for cp in sends:
            cp.wait_send()
        mine.wait()

    return _pcall(
        body, name=name, in_specs=[_ANY], out_specs=_ANY,
        out_shape=jax.ShapeDtypeStruct((4, rows, width), flat.dtype),
        scratch_shapes=[pltpu.SemaphoreType.DMA((3,)), pltpu.SemaphoreType.DMA((3,)), pltpu.SemaphoreType.DMA],
    )(flat)


def _sibling_split(g, name):
    s, rows, width = g.shape
    half = rows // 2
    tr = _row_tile(half, width)
    nt = half // tr

    def body(c_ref, g_blk, got_ref, send_sem, recv_sem):
        k = pl.program_id(0)
        t = pl.program_id(1)
        x, y, c = _my_place()
        cp = pltpu.make_async_remote_copy(
            src_ref=g_blk.at[0], dst_ref=got_ref.at[k, pl.ds(pl.multiple_of(t * tr, 8), tr), :],
            send_sem=send_sem, recv_sem=recv_sem, device_id=(x, y, 1 - c), device_id_type=MESH)
        cp.start()
        cp.wait_send()

        @pl.when(jnp.logical_and(k == s - 1, t == nt - 1))
        def _():
            pltpu.make_async_remote_copy(
                src_ref=got_ref, dst_ref=got_ref, send_sem=send_sem, recv_sem=recv_sem,
                device_id=(x, y, 1 - c), device_id_type=MESH).wait_recv()

    return _pcall(
        body, name=name, out_shape=jax.ShapeDtypeStruct((s, half, width), g.dtype),
        grid_spec=pltpu.PrefetchScalarGridSpec(
            num_scalar_prefetch=1, grid=(s, nt),
            in_specs=[pl.BlockSpec((1, tr, width), lambda k, t, c_ref: (k, (1 - c_ref[0]) * nt + t, 0))],
            out_specs=_ANY,
            scratch_shapes=[pltpu.SemaphoreType.DMA, pltpu.SemaphoreType.DMA]),
        compiler_params=_params(2),
    )(_core_index(), g)


def _chip_scatter(p, name):
    s, rows, width = p.shape

    def body(p_ref, out_ref, send_sems, recv_sems, local_sem):
        x, y, c = _my_place()
        me = 2 * x + y
        mine = pltpu.make_async_copy(p_ref.at[me], out_ref.at[me], local_sem)
        mine.start()
        sends = []
        for k, (px, py) in enumerate(_other_chips(x, y)):
            cp = pltpu.make_async_remote_copy(
                src_ref=p_ref.at[2 * px + py], dst_ref=out_ref.at[me], send_sem=send_sems.at[k],
                recv_sem=recv_sems.at[k], device_id=(px, py, c), device_id_type=MESH)
            cp.start()
            sends.append(cp)
        for k, (px, py) in enumerate(_other_chips(x, y)):
            pltpu.make_async_remote_copy(
                src_ref=p_ref.at[me], dst_ref=out_ref.at[2 * px + py], send_sem=send_sems.at[k],
                recv_sem=recv_sems.at[k], device_id=(px, py, c), device_id_type=MESH).wait_recv()
        for cp in sends:
            cp.wait_send()
        mine.wait()

    return _pcall(
        body, name=name, in_specs=[_ANY], out_specs=_ANY,
        out_shape=jax.ShapeDtypeStruct(p.shape, p.dtype),
        scratch_shapes=[pltpu.SemaphoreType.DMA((3,)), pltpu.SemaphoreType.DMA((3,)), pltpu.SemaphoreType.DMA],
    )(p)


def _sibling_join(qh, name):
    half, width = qh.shape
    tr = _row_tile(half, width)
    nt = half // tr

    def body(q_blk, out_ref, send_sem, recv_sem, local_sem):
        t = pl.program_id(0)
        x, y, c = _my_place()
        dst = out_ref.at[pl.ds(pl.multiple_of(c * half + t * tr, 8), tr), :]
        cp = pltpu.make_async_remote_copy(
            src_ref=q_blk, dst_ref=dst, send_sem=send_sem, recv_sem=recv_sem,
            device_id=(x, y, 1 - c), device_id_type=MESH)
        cp.start()
        mine = pltpu.make_async_copy(q_blk, dst, local_sem)
        mine.start()
        cp.wait_send()
        mine.wait()

        @pl.when(t == nt - 1)
        def _():
            theirs = out_ref.at[pl.ds(pl.multiple_of((1 - c) * half, 8), half), :]
            pltpu.make_async_remote_copy(
                src_ref=theirs, dst_ref=theirs, send_sem=send_sem, recv_sem=recv_sem,
                device_id=(x, y, 1 - c), device_id_type=MESH).wait_recv()

    return _pcall(
        body, name=name, grid=(nt,),
        in_specs=[pl.BlockSpec((tr, width), lambda t: (t, 0))], out_specs=_ANY,
        out_shape=jax.ShapeDtypeStruct((2 * half, width), qh.dtype),
        scratch_shapes=[pltpu.SemaphoreType.DMA, pltpu.SemaphoreType.DMA, pltpu.SemaphoreType.DMA],
        compiler_params=_params(1),
    )(qh)


def _all_sum_small(part, name):
    rows, width = part.shape

    def body(p_ref, out_ref, land, send_sems, recv_sems):
        x, y, c = _my_place()
        me = 4 * x + 2 * y + c
        land[me] = p_ref[...]
        sends = []
        for k in range(1, 8):
            peer = (x ^ (k >> 2), y ^ ((k >> 1) & 1), c ^ (k & 1))
            cp = pltpu.make_async_remote_copy(
                src_ref=p_ref, dst_ref=land.at[me], send_sem=send_sems.at[k - 1], recv_sem=recv_sems.at[k - 1],
                device_id=peer, device_id_type=MESH)
            cp.start()
            sends.append(cp)
        for k in range(1, 8):
            px, py, pc = x ^ (k >> 2), y ^ ((k >> 1) & 1), c ^ (k & 1)
            pltpu.make_async_remote_copy(
                src_ref=p_ref, dst_ref=land.at[4 * px + 2 * py + pc], send_sem=send_sems.at[k - 1],
                recv_sem=recv_sems.at[k - 1], device_id=(px, py, pc), device_id_type=MESH).wait_recv()
        for cp in sends:
            cp.wait_send()
        tot = land[0]
        for k in range(1, 8):
            tot = tot + land[k]
        out_ref[...] = tot

    vmem = pl.BlockSpec(memory_space=pltpu.VMEM)
    return _pcall(
        body, name=name, in_specs=[vmem], out_specs=vmem,
        out_shape=jax.ShapeDtypeStruct((rows, width), F32),
        scratch_shapes=[pltpu.VMEM((8, rows, width), F32), pltpu.SemaphoreType.DMA((7,)),
                        pltpu.SemaphoreType.DMA((7,))],
    )(part)


def _big_layout(shards):
    return [(a.shape[0], a.shape[1], ax) for a, ax in shards]


FLAT_ROW_MULTIPLE = 2048


def _pack_shards(arrs, row_multiple=FLAT_ROW_MULTIPLE):
    flat = jnp.concatenate([a.reshape(-1) for a in arrs])
    return jnp.pad(flat, (0, -flat.shape[0] % (row_multiple * LANE))).reshape(-1, LANE)


def _unpack_shards(flat, layout):
    flat = flat.reshape(-1)
    out, off = [], 0
    for r, c, _ in layout:
        out.append(flat[off:off + r * c].reshape(r, c))
        off += r * c
    return out


def _unpack_full(gathered, layout):
    g = gathered.reshape(4, -1)
    out, off = [], 0
    for r, c, ax in layout:
        seg = g[:, off:off + r * c].reshape(4, r, c)
        out.append(seg.transpose(1, 0, 2).reshape(r, 4 * c) if ax == 1 else seg.reshape(4 * r, c))
        off += r * c
    return out


def _pack_full(fulls, layout):
    parts = []
    for a, (r, c, ax) in zip(fulls, layout):
        if ax == 1:
            parts.append(a.reshape(r, 4, c).transpose(1, 0, 2).reshape(4, r * c))
        else:
            parts.append(a.reshape(4, r * c))
    flat = jnp.concatenate(parts, axis=1)
    return jnp.pad(flat, ((0, 0), (0, -flat.shape[1] % (FLAT_ROW_MULTIPLE * LANE)))).reshape(4, -1, LANE)


def _pad_lanes(a, width=LANE):
    return jnp.pad(a, [(0, 0)] * (a.ndim - 1) + [(0, width - a.shape[-1])])


def _pack_small(arrs):
    rows = [_pad_lanes(a.reshape(1, -1), -(-a.size // LANE) * LANE).reshape(-1, LANE) for a in arrs]
    flat = jnp.concatenate(rows, axis=0)
    return jnp.pad(flat, ((0, -flat.shape[0] % 8), (0, 0)))


def _unpack_small(flat, shapes):
    out, off = [], 0
    for shp in shapes:
        n = math.prod(shp)
        nr = -(-n // LANE)
        out.append(flat[off:off + nr].reshape(-1)[:n].reshape(shp))
        off += nr
    return out


def kernel(x, meta_tokens, pre_norm, post_norm, gdn_w_in, gdn_conv_w, gdn_a_log, gdn_dt_bias, gdn_out_norm, gdn_w_out, kv_norm, kv_w_down, kv_latent_norm, kv_w_up, mla_w_in, mla_q_latent_norm, mla_w_q_up, mla_w_out, loss_target, m_meta_tokens, m_pre_norm, m_post_norm, m_gdn_w_in, m_gdn_conv_w, m_gdn_a_log, m_gdn_dt_bias, m_gdn_out_norm, m_gdn_w_out, m_kv_norm, m_kv_w_down, m_kv_latent_norm, m_kv_w_up, m_mla_w_in, m_mla_q_latent_norm, m_mla_w_q_up, m_mla_w_out, v_meta_tokens, v_pre_norm, v_post_norm, v_gdn_w_in, v_gdn_conv_w, v_gdn_a_log, v_gdn_dt_bias, v_gdn_out_norm, v_gdn_w_out, v_kv_norm, v_kv_w_down, v_kv_latent_norm, v_kv_w_up, v_mla_w_in, v_mla_q_latent_norm, v_mla_w_q_up, v_mla_w_out):
    seq = x.shape[1]
    d = D_MODEL
    lp = -(-(ROW0 + seq) // ROW_ALIGN) * ROW_ALIGN
    tail = lp - ROW0 - seq

    big_names = ["meta_tokens", "gdn_conv_w", "gdn_w_out", "kv_w_down", "kv_w_up", "mla_w_in", "mla_w_q_up",
                 "mla_w_out"]
    big_axis = [1, 1, 0, 0, 1, 1, 1, 0]
    big_w = [meta_tokens, gdn_conv_w[0], gdn_w_out[0], kv_w_down, kv_w_up, mla_w_in[0], mla_w_q_up[0], mla_w_out[0]]
    big_m = [m_meta_tokens, m_gdn_conv_w[0], m_gdn_w_out[0], m_kv_w_down, m_kv_w_up, m_mla_w_in[0], m_mla_w_q_up[0],
             m_mla_w_out[0]]
    big_v = [v_meta_tokens, v_gdn_conv_w[0], v_gdn_w_out[0], v_kv_w_down, v_kv_w_up, v_mla_w_in[0], v_mla_w_q_up[0],
             v_mla_w_out[0]]
    layout = _big_layout(list(zip(big_w, big_axis)))
    meta_f, conv_w = _unpack_full(
        _gather_shards(_pack_shards(big_w[:2], row_multiple=16), "gather_meta_conv"), layout[:2])
    mm_shards = [w.astype(BF16) for w in big_w[2:6]] + [(big_w[6] * Q_PRESCALE).astype(BF16), big_w[7].astype(BF16)]
    mm_flat = _pack_shards(mm_shards)
    w_in0_shards = _gather_shards(gdn_w_in[0].astype(BF16), "gather_gdn_w_in")
    w_in0 = jnp.concatenate([w_in0_shards[s] for s in range(4)], axis=1)
    win_cols = gdn_w_in.shape[2]

    nv = GDN_V_HEADS
    w_qkv = w_in0[:, :GDN_CONV_W]
    w_z0 = w_in0[:, GDN_CONV_W:GDN_CONV_W + GDN_V_W]
    w_b = _pad_lanes(w_in0[:, GDN_CONV_W + GDN_V_W:GDN_CONV_W + GDN_V_W + nv])
    w_a = _pad_lanes(w_in0[:, GDN_CONV_W + GDN_V_W + nv:])

    pre0, pre1 = pre_norm[0:1], pre_norm[1:2]
    post0, post1 = post_norm[0:1], post_norm[1:2]
    a_log = _pad_lanes(gdn_a_log)
    dt_bias = _pad_lanes(gdn_dt_bias)
    kvn = kv_norm.reshape(1, d)
    kvl = kv_latent_norm.reshape(1, MLA_KV_RANK)
    qln = mla_q_latent_norm

    h0 = jnp.concatenate([jnp.zeros((FRONT, d), F32), meta_f, x[0], jnp.zeros((tail, d), F32)], axis=0)
    tgt = jnp.pad(loss_target[0], ((ROW0, tail), (0, 0)))
    pos = jnp.maximum(jnp.arange(lp, dtype=jnp.int32) - FRONT, 0).astype(F32)
    inv = ROPE_THETA ** (-jnp.arange(0, MLA_ROPE, 2, dtype=F32) / MLA_ROPE)
    ang = pos[:, None] * inv[None, :]
    zeros64 = jnp.zeros((lp, LANE - MLA_ROPE), F32)
    cos_t = jnp.concatenate([jnp.cos(ang), jnp.cos(ang), zeros64], axis=1)
    sin_t = jnp.concatenate([-jnp.sin(ang), jnp.sin(ang), zeros64], axis=1)

    def valid_rows(ridx):
        return jnp.logical_and(ridx >= FRONT, ridx < ROW0 + seq)

    def f_pre0(ridx, g, h, gain):
        return _rms(h, gain), h

    (hn0,) = _rowwise("pre0", lambda *a: f_pre0(*a)[:1], [_In(h0), _In(pre0, "const")],
                      [_Out("row", (lp, d), BF16)])
    qkv_raw = _mm(hn0, w_qkv, "nn", "gdn_in_qkv")
    z0 = _mm(hn0, w_z0, "nn", "gdn_in_z")
    b_raw = _mm(hn0, w_b, "nn", "gdn_in_b")
    a_raw = _mm(hn0, w_a, "nn", "gdn_in_a")

    def f_ba(ridx, g, b, a, alog, dtb):
        tr = b.shape[0]
        ok = valid_rows(ridx).astype(F32)
        beta = jax.nn.sigmoid(b) * ok
        gate = -jnp.exp(alog) * _softplus(a + dtb) * ok
        ii = lax.broadcasted_iota(jnp.int32, (tr, tr), 0)
        jj = lax.broadcasted_iota(jnp.int32, (tr, tr), 1)
        shift = GDN_CHUNK.bit_length() - 1
        tri = jnp.logical_and((ii >> shift) == (jj >> shift), ii >= jj).astype(F32)
        return beta, _hdot(tri, gate)

    ba_ins = [_In(b_raw), _In(a_raw), _In(a_log, "const"), _In(dt_bias, "const")]
    beta, gc = _rowwise("gdn_gates", f_ba, ba_ins, [_Out("row", (lp, LANE)), _Out("row", (lp, LANE))])
    qkv = _conv_fwd(qkv_raw, conv_w, "gdn_conv")
    o0, ckpt, t_saved, mm_all = _gdn_fwd(qkv, beta, gc, mm_flat, "gdn_scan")
    (w_out0, kv_down, kv_up, w_in1, w_qup, w_out1) = _unpack_full(mm_all, layout[2:])
    w_ckv = kv_down[:, :MLA_KV_RANK]
    w_kr = _pad_lanes(kv_down[:, MLA_KV_RANK:])
    kvu = kv_up.reshape(MLA_KV_RANK, MLA_HEADS, 2 * LANE)
    w_kn = kvu[:, :, :LANE].reshape(MLA_KV_RANK, MLA_HEADS * LANE)
    w_v = kvu[:, :, LANE:].reshape(MLA_KV_RANK, MLA_HEADS * LANE)
    w_cq = w_in1[:, :MLA_Q_RANK]
    w_z1 = w_in1[:, MLA_Q_RANK:]
    qu = w_qup.reshape(MLA_Q_RANK, MLA_HEADS, MLA_QK)
    w_qn = qu[:, :, :MLA_NOPE].reshape(MLA_Q_RANK, MLA_HEADS * LANE)
    w_qr = _pad_lanes(qu[:, :, MLA_NOPE:]).reshape(MLA_Q_RANK, MLA_HEADS * LANE)

    def per_head(fn, *arrs):
        n = arrs[0].shape[1] // LANE
        return jnp.concatenate([fn(*[a[:, i * LANE:(i + 1) * LANE] for a in arrs]) for i in range(n)], axis=1)

    def f_gate0(ridx, g, o, z, gain):
        return (per_head(lambda oh, zh: _rms(oh, gain) * _silu(zh), o, z),)

    gate0_ins = [_In(o0), _In(z0), _In(gdn_out_norm, "const")]
    (gated0,) = _rowwise("gdn_gate", f_gate0, gate0_ins, [_Out("row", (lp, GDN_V_W), BF16)])
    y0 = _mm(gated0, w_out0, "nn", "gdn_out")

    def f_mid(ridx, g, h, y, g_post, g_pre, g_kv):
        h1 = h + _rms(y, g_post)
        return h1, _rms(h1, g_pre), _rms(h1, g_kv)

    mid_ins = [_In(h0), _In(y0), _In(post0, "const"), _In(pre1, "const"), _In(kvn, "const")]
    h1, hn1, hkv = _rowwise("mid", f_mid, mid_ins,
                            [_Out("row", (lp, d)), _Out("row", (lp, d), BF16), _Out("row", (lp, d), BF16)])

    ckv_raw = _mm(hkv, w_ckv, "nn", "kv_down_c")
    kr_raw = _mm(hkv, w_kr, "nn", "kv_down_r")

    def f_ckv(ridx, g, c, r, cs, sn, gain):
        return _rms(c, gain), _rope(r, cs, sn)

    ckv_ins = [_In(ckv_raw), _In(kr_raw), _In(cos_t), _In(sin_t), _In(kvl, "const")]
    ckv, kr = _rowwise("kv_latent", f_ckv, ckv_ins, [_Out("row", (lp, LANE)), _Out("row", (lp, LANE), BF16)],
                       tr=TR_FULL)
    kn = _mm(ckv, w_kn, "nn", "kv_up_k", BF16)
    vv = _mm(ckv, w_v, "nn", "kv_up_v", BF16)
    cq_raw = _mm(hn1, w_cq, "nn", "mla_in_q")
    z1 = _mm(hn1, w_z1, "nn", "mla_in_z")

    def f_cq(ridx, g, c, gain):
        return (_rms(c, gain),)

    cq_ins = [_In(cq_raw), _In(qln, "const")]
    (cq,) = _rowwise("q_latent", f_cq, cq_ins, [_Out("row", (lp, MLA_Q_RANK))], tr=TR_FULL)
    qn = _mm(cq, w_qn, "nn", "q_up_n", BF16)
    qr_raw = _mm(cq, w_qr, "nn", "q_up_r")

    def f_qrope(ridx, g, r, cs, sn):
        return (per_head(lambda rh: _rope(rh, cs, sn), r),)

    qr_ins = [_In(qr_raw), _In(cos_t), _In(sin_t)]
    (qr,) = _rowwise("q_rope", f_qrope, qr_ins, [_Out("row", (lp, MLA_HEADS * LANE), BF16)])
    o1, lse = _flash_fwd(qn, qr, kn, kr, vv, "attention")

    def f_gate1(ridx, g, o, z):
        return (o * _silu(z),)

    gate1_ins = [_In(o1), _In(z1)]
    (og,) = _rowwise("mla_gate", f_gate1, gate1_ins, [_Out("row", (lp, MLA_HEADS * LANE), BF16)])
    y1 = _mm(og, w_out1, "nn", "mla_out")

    def f_final(ridx, g, h, y, t, gain):
        ok = jnp.logical_and(ridx >= ROW0, ridx < ROW0 + seq).astype(F32)

        def rows_loss(h_, y_, gain_):
            err = (h_ + _rms(y_, gain_) - t) * ok
            return 0.5 * jnp.sum(jnp.sum(err * err, axis=1, keepdims=True), axis=0, keepdims=True) / d

        val, vjp = jax.vjp(rows_loss, h, y, gain)
        dh, dy, dgain = vjp(jnp.ones((1, 1), F32))
        return dh, dy, dgain, jnp.broadcast_to(val, (1, LANE))

    dh2, dy1, dpost1, loss_part = _rowwise(
        "loss_head", f_final, [_In(h1), _In(y1), _In(tgt), _In(post1, "const")],
        [_Out("row", (lp, d)), _Out("row", (lp, d)), _Out("acc", (1, d)), _Out("acc", (1, LANE))])

    dog = _mm(dy1, w_out1, "nt", "mla_out_dx")
    dw_out1 = _mm(og, dy1, "tn", "mla_out_dw")
    do1, dz1 = _rowwise_vjp("mla_gate_bwd", f_gate1, gate1_ins, [[dog]], [0, 1])
    dqn, dqr, dkn, dkr, dvv = _flash_bwd(qn, qr, kn, kr, vv, o1, do1, lse, "attention_bwd")
    (dqr_raw,) = _rowwise_vjp("q_rope_bwd", f_qrope, qr_ins, [[dqr]], [0])
    dcq_a = _mm(dqn, w_qn, "nt", "q_up_n_dx")
    dcq_b = _mm(dqr_raw, w_qr, "nt", "q_up_r_dx")
    dw_qn = _mm(cq, dqn, "tn", "q_up_n_dw") * Q_PRESCALE
    dw_qr = _mm(cq, dqr_raw, "tn", "q_up_r_dw") * Q_PRESCALE
    dcq_raw, dqln = _rowwise_vjp("q_latent_bwd", f_cq, cq_ins, [[dcq_a, dcq_b]], [0, 1], tr=TR_FULL)
    dhn1_a = _mm(dcq_raw, w_cq, "nt", "mla_in_q_dx")
    dhn1_b = _mm(dz1, w_z1, "nt", "mla_in_z_dx")
    dw_cq = _mm(hn1, dcq_raw, "tn", "mla_in_q_dw")
    dw_z1 = _mm(hn1, dz1, "tn", "mla_in_z_dw")
    dckv_a = _mm(dkn, w_kn, "nt", "kv_up_k_dx")
    dckv_b = _mm(dvv, w_v, "nt", "kv_up_v_dx")
    dw_kn = _mm(ckv, dkn, "tn", "kv_up_k_dw")
    dw_v = _mm(ckv, dvv, "tn", "kv_up_v_dw")
    dckv_raw, dkr_raw, dkvl = _rowwise_vjp("kv_latent_bwd", f_ckv, ckv_ins, [[dckv_a, dckv_b], [dkr]], [0, 1, 4],
                                           tr=TR_FULL)
    dhkv_a = _mm(dckv_raw, w_ckv, "nt", "kv_down_c_dx")
    dhkv_b = _mm(dkr_raw, w_kr, "nt", "kv_down_r_dx")
    dw_ckv = _mm(hkv, dckv_raw, "tn", "kv_down_c_dw")
    dw_kr = _mm(hkv, dkr_raw, "tn", "kv_down_r_dw")
    dh0_res, dy0, dpost0, dpre1, dkvn = _rowwise_vjp(
        "mid_bwd", f_mid, mid_ins, [[dh2], [dhn1_a, dhn1_b], [dhkv_a, dhkv_b]], [0, 1, 2, 3, 4])

    dgated0 = _mm(dy0, w_out0, "nt", "gdn_out_dx")
    dw_out0 = _mm(gated0, dy0, "tn", "gdn_out_dw")
    do0, dz0, doutn = _rowwise_vjp("gdn_gate_bwd", f_gate0, gate0_ins, [[dgated0]], [0, 1, 2], tr=TR_QUARTER)

    g_kv_down = jnp.concatenate([dw_ckv, dw_kr[:, :MLA_ROPE]], axis=1)
    g_kv_up = jnp.concatenate([dw_kn.reshape(MLA_KV_RANK, MLA_HEADS, LANE), dw_v.reshape(MLA_KV_RANK, MLA_HEADS, LANE)],
                              axis=2).reshape(MLA_KV_RANK, MLA_HEADS * 2 * LANE)
    g_w_in1 = jnp.concatenate([dw_cq, dw_z1], axis=1)
    g_qup = jnp.concatenate([dw_qn.reshape(MLA_Q_RANK, MLA_HEADS, LANE),
                             dw_qr.reshape(MLA_Q_RANK, MLA_HEADS, LANE)[:, :, :MLA_ROPE]],
                            axis=2).reshape(MLA_Q_RANK, MLA_HEADS * MLA_QK)
    g_mm = _pack_full([dw_out0, g_kv_down, g_kv_up, g_w_in1, g_qup, dw_out1], layout[2:])
    mm_chip_part = _add_pair(g_mm, _sibling_split(g_mm, "grads_sibling_split"), "grads_chip_sum")
    dq0, dk0, dv0, dbeta, dgc, mm_from_chips = _gdn_bwd(qkv, beta, gc, ckpt, t_saved, do0, mm_chip_part,
                                                        "gdn_scan_bwd")
    g_flat = _sibling_join(_sum_slots(mm_from_chips, "grads_total"), "grads_sibling_join")
    db_raw, da_raw, dalog, ddtb = _rowwise_vjp("gdn_gates_bwd", f_ba, ba_ins, [[dbeta], [dgc]], [0, 1, 2, 3])
    dqkv_raw, dconv = _conv_bwd(qkv_raw, conv_w, dq0, dk0, dv0, "gdn_conv_bwd")
    dhn0_a = _mm(dqkv_raw, w_qkv, "nt", "gdn_in_qkv_dx")
    dhn0_b = _mm(dz0, w_z0, "nt", "gdn_in_z_dx")
    dhn0_c = _mm(db_raw, w_b, "nt", "gdn_in_b_dx")
    dhn0_d = _mm(da_raw, w_a, "nt", "gdn_in_a_dx")
    dw_qkv = _mm(hn0, dqkv_raw, "tn", "gdn_in_qkv_dw")
    dw_z0 = _mm(hn0, dz0, "tn", "gdn_in_z_dw")
    dw_b = _mm(hn0, db_raw, "tn", "gdn_in_b_dw")
    dw_a = _mm(hn0, da_raw, "tn", "gdn_in_a_dw")
    dh0, dpre0 = _rowwise_vjp("pre0_bwd", f_pre0, [_In(h0), _In(pre0, "const")],
                              [[dhn0_a, dhn0_b, dhn0_c, dhn0_d], [dh0_res]], [0, 1])

    grad_x = dh0[ROW0:ROW0 + seq][None]
    g_meta = dh0[FRONT:ROW0]
    g_w_in0 = jnp.concatenate([dw_qkv, dw_z0, dw_b[:, :nv], dw_a[:, :nv]], axis=1)

    g_win_by_chip = jnp.concatenate([g_w_in0[None, :, s * win_cols:(s + 1) * win_cols] for s in range(4)], axis=0)
    win_chip_part = _add_pair(g_win_by_chip, _sibling_split(g_win_by_chip, "grads_sibling_split_gdn_w_in"),
                              "grads_chip_sum_gdn_w_in")
    win_from_chips = _chip_scatter(win_chip_part, "grads_chip_scatter_gdn_w_in")
    g_win = _sibling_join(_sum_slots(win_from_chips, "grads_total_gdn_w_in"), "grads_sibling_join_gdn_w_in")

    small_shapes = [(2, d), (2, d), (1, nv), (1, nv), (1, GDN_DK), (d,), (MLA_KV_RANK,), (1, MLA_Q_RANK),
                    g_meta.shape, dconv.shape, (1, LANE)]
    small_part = _pack_small([jnp.concatenate([dpre0, dpre1], axis=0), jnp.concatenate([dpost0, dpost1], axis=0),
                              dalog[:, :nv], ddtb[:, :nv], doutn, dkvn, dkvl, dqln, g_meta, dconv, loss_part])
    small_tot = _all_sum_small(small_part, "small_sum")
    small_g = _unpack_small(small_tot, small_shapes)
    loss = small_g[-1][0, 0]
    chip = 2 * lax.axis_index("x") + lax.axis_index("y")
    meta_cols, conv_cols = meta_tokens.shape[1], gdn_conv_w.shape[2]
    g_meta_shard = lax.dynamic_slice(small_g[8], (0, chip * meta_cols), (small_g[8].shape[0], meta_cols))
    g_conv_shard = lax.dynamic_slice(small_g[9], (0, chip * conv_cols), (small_g[9].shape[0], conv_cols))

    d_flat, m_flat, v_flat = _adamw(_pack_shards(big_w[2:]), g_flat, _pack_shards(big_m[2:]), _pack_shards(big_v[2:]),
                                    "adamw_sharded")
    win_step = _adamw(gdn_w_in[0], g_win, m_gdn_w_in[0], v_gdn_w_in[0], "adamw_gdn_w_in")
    small_names = ["pre_norm", "post_norm", "gdn_a_log", "gdn_dt_bias", "gdn_out_norm", "kv_norm", "kv_latent_norm",
                   "mla_q_latent_norm", "meta_tokens", "gdn_conv_w"]
    small_w = [pre_norm, post_norm, gdn_a_log, gdn_dt_bias, gdn_out_norm, kv_norm, kv_latent_norm, mla_q_latent_norm,
               meta_tokens, gdn_conv_w]
    small_m = [m_pre_norm, m_post_norm, m_gdn_a_log, m_gdn_dt_bias, m_gdn_out_norm, m_kv_norm, m_kv_latent_norm,
               m_mla_q_latent_norm, m_meta_tokens, m_gdn_conv_w]
    small_v = [v_pre_norm, v_post_norm, v_gdn_a_log, v_gdn_dt_bias, v_gdn_out_norm, v_kv_norm, v_kv_latent_norm,
               v_mla_q_latent_norm, v_meta_tokens, v_gdn_conv_w]
    g_small_flat = _pack_small(small_g[:8] + [g_meta_shard, g_conv_shard])
    ds_flat, ms_flat, vs_flat = _adamw(_pack_small(small_w), g_small_flat, _pack_small(small_m), _pack_small(small_v),
                                       "adamw_replicated")

    def assemble(big_flat, small_flat, win):
        bigs = dict(zip(big_names[2:], [a.reshape(w.shape) for a, w in zip(
            _unpack_shards(big_flat, layout[2:]),
            [gdn_w_out, kv_w_down, kv_w_up, mla_w_in, mla_w_q_up, mla_w_out])]))
        smalls = dict(zip(small_names, _unpack_small(small_flat, [w.shape for w in small_w])))
        both = {**bigs, **smalls, "gdn_w_in": win[None]}
        order = ["meta_tokens", "pre_norm", "post_norm", "gdn_w_in", "gdn_conv_w", "gdn_a_log", "gdn_dt_bias",
                 "gdn_out_norm", "gdn_w_out", "kv_norm", "kv_w_down", "kv_latent_norm", "kv_w_up", "mla_w_in",
                 "mla_q_latent_norm", "mla_w_q_up", "mla_w_out"]
        return [both[n] for n in order]

    grads = assemble(g_flat, g_small_flat, g_win)
    deltas = assemble(d_flat, ds_flat, win_step[0])
    new_m = assemble(m_flat, ms_flat, win_step[1])
    new_v = assemble(v_flat, vs_flat, win_step[2])
    return (loss, grad_x, *grads, *deltas, *new_m, *new_v)
```

```python
import functools
import math

import jax
import jax.numpy as jnp
from jax import lax
from jax.experimental import pallas as pl
from jax.experimental.pallas import tpu as pltpu

F32 = jnp.float32
BF16 = jnp.bfloat16
MESH = pl.DeviceIdType.MESH

D_MODEL = 1024
N_META = 16
FRONT = 48
ROW0 = FRONT + N_META
ROW_ALIGN = 768
TR_FULL, TR_HALF, TR_QUARTER = ROW_ALIGN, ROW_ALIGN // 2, ROW_ALIGN // 4
NORM_EPS = 1e-6
LANE = 128

GDN_QK_HEADS = 8
GDN_V_HEADS = 16
GDN_DK = 128
GDN_CHUNK = 64
GDN_QK_W = 1024
GDN_V_W = 2048
GDN_CONV_W = 4096

MLA_HEADS = 16
MLA_NOPE = 128
MLA_ROPE = 64
MLA_QK = 192
MLA_Q_RANK = 256
MLA_KV_RANK = 128
ROPE_THETA = 10000.0

ADAM_LR = 0.001
ADAM_B1 = 0.9
ADAM_B2 = 0.999
ADAM_EPS = 1e-08
ADAM_WD = 0.01
ADAM_STEP = 10

VMEM_LIMIT_V7X = 56 * 1024 * 1024
NEG = -1e30

_NN = ((1,), (0,))
_NT = ((1,), (1,))
_TN = ((0,), (0,))
_HI = lax.Precision.HIGHEST
_X3 = lax.Precision.HIGH


def _pcall(body, **kw):
    return pl.pallas_call(body, **kw)


def _params(n_axes):
    return pltpu.CompilerParams(dimension_semantics=("arbitrary",) * n_axes, vmem_limit_bytes=VMEM_LIMIT_V7X)


def _dot(a, b, dims, prec=None):
    return lax.dot_general(a, b, (dims, ((), ())), precision=prec, preferred_element_type=F32)


def _bdot(a, b, dims):
    return _dot(a.astype(BF16), b.astype(BF16), dims)


def _hdot(a, b, dims=_NN):
    return _dot(a, b, dims, _HI)


def _fdot(a, b, dims):
    return _dot(a, b, dims)


SMALL_MATMUL_DIM = 256
SMALL_MATMUL_ROWS = 1408


def _tile(n):
    if n % ROW_ALIGN == 0:
        return SMALL_MATMUL_ROWS if n % SMALL_MATMUL_ROWS == 0 else ROW_ALIGN
    for t in (1024, 512, 256, 128):
        if n % t == 0:
            return t
    raise ValueError(n)


def _mm(a, b, mode, name, out_dtype=F32):
    if mode == "nn":
        (m, k), (k2, n) = a.shape, b.shape
    elif mode == "nt":
        (m, k), (n, k2) = a.shape, b.shape
    else:
        (k, m), (k2, n) = a.shape, b.shape
    assert k == k2, (a.shape, b.shape, mode)
    tm, tn, tk = _tile(m), _tile(n), _tile(k)
    if mode != "tn" and min(k, n) <= SMALL_MATMUL_DIM and m % SMALL_MATMUL_ROWS == 0:
        tm = SMALL_MATMUL_ROWS
    nk = k // tk
    dims = {"nn": _NN, "nt": _NT, "tn": _TN}[mode]

    def body(a_ref, b_ref, o_ref, acc):
        kk = pl.program_id(2)

        @pl.when(kk == 0)
        def _():
            acc[...] = jnp.zeros_like(acc)

        acc[...] += _bdot(a_ref[...], b_ref[...], dims)

        @pl.when(kk == nk - 1)
        def _():
            o_ref[...] = acc[...].astype(out_dtype)

    if mode == "tn":
        a_spec = pl.BlockSpec((tk, tm), lambda i, j, kk: (kk, i))
    else:
        a_spec = pl.BlockSpec((tm, tk), lambda i, j, kk: (i, kk))
    if mode == "nt":
        b_spec = pl.BlockSpec((tn, tk), lambda i, j, kk: (j, kk))
    else:
        b_spec = pl.BlockSpec((tk, tn), lambda i, j, kk: (kk, j))
    return _pcall(
        body, name=name, grid=(m // tm, n // tn, nk),
        in_specs=[a_spec, b_spec],
        out_specs=pl.BlockSpec((tm, tn), lambda i, j, kk: (i, j)),
        out_shape=jax.ShapeDtypeStruct((m, n), out_dtype),
        scratch_shapes=[pltpu.VMEM((tm, tn), F32)],
        compiler_params=_params(3),
    )(a, b)


class _In:
    def __init__(self, arr, kind="row", grouped=False, goff=0):
        self.arr, self.kind, self.grouped, self.goff = arr, kind, grouped, goff


class _Out:
    def __init__(self, kind, shape, dtype=F32, grouped=False):
        self.kind, self.shape, self.dtype, self.grouped = kind, shape, dtype, grouped


def _rowwise(name, fn, ins, outs, *, groups=1, tr=TR_HALF):
    lp = next(i.arr.shape[0] for i in ins if i.kind == "row")
    nr = lp // tr
    assert lp % tr == 0

    def in_spec(i):
        w = i.arr.shape[1]
        if i.kind == "row":
            if i.grouped:
                return pl.BlockSpec((tr, LANE), lambda g, r, o=i.goff: (r, g + o))
            return pl.BlockSpec((tr, w), lambda g, r: (r, 0))
        if i.grouped:
            return pl.BlockSpec((i.arr.shape[0], LANE), lambda g, r, o=i.goff: (0, g + o))
        return pl.BlockSpec(i.arr.shape, lambda g, r: (0, 0))

    def out_spec(o):
        if o.kind == "row":
            if o.grouped:
                return pl.BlockSpec((tr, LANE), lambda g, r: (r, g))
            assert groups == 1
            return pl.BlockSpec((tr, o.shape[1]), lambda g, r: (r, 0))
        if o.grouped:
            return pl.BlockSpec((o.shape[0], LANE), lambda g, r: (0, g))
        return pl.BlockSpec(o.shape, lambda g, r: (0, 0))

    n_in = len(ins)

    def body(*refs):
        g = pl.program_id(0)
        r = pl.program_id(1)
        ridx = r * tr + lax.broadcasted_iota(jnp.int32, (tr, 1), 0)
        res = fn(ridx, g, *[ref[...] for ref in refs[:n_in]])
        assert len(res) == len(outs), (name, len(res), len(outs))
        for o, ref, val in zip(outs, refs[n_in:], res):
            if o.kind == "row":
                ref[...] = val.astype(o.dtype)
            else:
                first = (r == 0) if o.grouped else jnp.logical_and(r == 0, g == 0)

                @pl.when(first)
                def _(ref=ref, val=val):
                    ref[...] = val.astype(F32)

                @pl.when(jnp.logical_not(first))
                def _(ref=ref, val=val):
                    ref[...] += val.astype(F32)

    res = _pcall(
        body, name=name, grid=(groups, nr),
        in_specs=[in_spec(i) for i in ins],
        out_specs=[out_spec(o) for o in outs],
        out_shape=[jax.ShapeDtypeStruct(o.shape, o.dtype) for o in outs],
        compiler_params=_params(2),
    )(*[i.arr for i in ins])
    return res


def _rowwise_vjp(name, fn, ins, cots, diff, *, groups=1, tr=TR_HALF):
    n_in = len(ins)
    grouped = groups > 1
    cot_ins = []
    counts = []
    for arrs in cots:
        counts.append(len(arrs))
        for a in arrs:
            cot_ins.append(_In(a, "row", grouped=grouped and a.shape[1] > LANE))
    lp = next(i.arr.shape[0] for i in ins if i.kind == "row")
    outs = []
    for d in diff:
        i = ins[d]
        if i.kind == "row":
            w = groups * LANE if i.grouped else i.arr.shape[1]
            outs.append(_Out("row", (lp, w), F32, grouped=i.grouped))
        else:
            outs.append(_Out("acc", i.arr.shape, F32, grouped=i.grouped))

    def bfn(ridx, g, *allvals):
        vals = list(allvals[:n_in])
        cvals = allvals[n_in:]

        def f(*dv):
            full = list(vals)
            for i, v in zip(diff, dv):
                full[i] = v
            return tuple(fn(ridx, g, *full))

        primal, vjp = jax.vjp(f, *[vals[i].astype(F32) for i in diff])
        cts = []
        pos = 0
        for k, cnt in enumerate(counts):
            if cnt == 0:
                cts.append(jnp.zeros_like(primal[k]))
            else:
                c = cvals[pos].astype(F32)
                for extra in cvals[pos + 1:pos + cnt]:
                    c = c + extra.astype(F32)
                w = primal[k].shape[1]
                if c.shape[1] != w:
                    c = functools.reduce(jnp.add, [c[:, i * w:(i + 1) * w] for i in range(c.shape[1] // w)])
                cts.append(c.astype(primal[k].dtype))
            pos += cnt
        return vjp(tuple(cts))

    return _rowwise(name, bfn, list(ins) + cot_ins, outs, groups=groups, tr=tr)


def _rms(x, g):
    return x * lax.rsqrt(jnp.mean(x * x, axis=-1, keepdims=True) + NORM_EPS) * g


def _silu(x):
    return x * jax.nn.sigmoid(x)


def _softplus(x):
    return jnp.maximum(x, 0.0) + jnp.log(1.0 + jnp.exp(-jnp.abs(x)))


def _swap_halves(x):
    lane = lax.broadcasted_iota(jnp.int32, x.shape, x.ndim - 1)
    return jnp.where(lane < 32, pltpu.roll(x, LANE - 32, x.ndim - 1), pltpu.roll(x, 32, x.ndim - 1))


@jax.custom_vjp
def _rope(x, c, s):
    return x * c + _swap_halves(x) * s


def _rope_fwd(x, c, s):
    return _rope(x, c, s), (c, s)


def _rope_bwd(res, dy):
    c, s = res
    return dy * c + _swap_halves(dy * s), jnp.zeros_like(c), jnp.zeros_like(s)


_rope.defvjp(_rope_fwd, _rope_bwd)


def _conv_post(c, g):
    s = _silu(c)
    n = s * lax.rsqrt(jnp.sum(s * s, axis=-1, keepdims=True) + NORM_EPS)
    return jnp.where(g < GDN_QK_HEADS, n * (GDN_DK ** -0.5), jnp.where(g < 2 * GDN_QK_HEADS, n, s))


def _conv_taps(xe, w):
    c = xe[8:] * w[3]
    for s in (1, 2, 3):
        c = c + pltpu.roll(xe, s, 0)[8:] * w[3 - s]
    return c


CONV_LANES = 512
CONV_HEADS = CONV_LANES // LANE


def _conv_post_block(c, g):
    return jnp.concatenate([_conv_post(c[:, i * LANE:(i + 1) * LANE], g * CONV_HEADS + i)
                            for i in range(CONV_HEADS)], axis=1)


def _conv_fwd(x, w, name, tr=TR_FULL):
    lp, width = x.shape
    cl = CONV_LANES
    nr = lp // tr

    def body(x_ref, prev_ref, w_ref, o_ref):
        g = pl.program_id(0)
        r = pl.program_id(1)
        prev = jnp.where(r > 0, prev_ref[...], 0.0)
        xe = jnp.concatenate([prev, x_ref[...]], axis=0)
        o_ref[...] = _conv_post_block(_conv_taps(xe, [w_ref[t:t + 1, :] for t in range(4)]), g)

    return _pcall(
        body, name=name, grid=(width // cl, nr),
        in_specs=[pl.BlockSpec((tr, cl), lambda g, r: (r, g)),
                  pl.BlockSpec((8, cl), lambda g, r: (jnp.maximum(r * (tr // 8) - 1, 0), g)),
                  pl.BlockSpec((4, cl), lambda g, r: (0, g))],
        out_specs=pl.BlockSpec((tr, cl), lambda g, r: (r, g)),
        out_shape=jax.ShapeDtypeStruct((lp, width), F32),
        compiler_params=_params(2),
    )(x, x, w)


def _conv_bwd(x, w, dq, dk, dv, name, tr=TR_FULL):
    lp, width = x.shape
    cl = CONV_LANES
    nr = lp // tr
    last8 = lp // 8 - 1
    nq = GDN_QK_W // cl

    def body(x_ref, prev_ref, next_ref, w_ref, q_ref, k_ref, v_ref, q_n, k_n, v_n, dx_ref, dw_ref):
        g = pl.program_id(0)
        r = pl.program_id(1)
        w = [w_ref[t:t + 1, :] for t in range(4)]
        not_last = r < nr - 1

        def pick(a, b, c):
            return jnp.where(g < nq, a[...], jnp.where(g < 2 * nq, b[...], c[...]))

        dy = pick(q_ref, k_ref, v_ref)
        dyn = jnp.where(not_last, pick(q_n, k_n, v_n), 0.0)
        prev = jnp.where(r > 0, prev_ref[...], 0.0)
        nxt = jnp.where(not_last, next_ref[...], 0.0)
        xe = jnp.concatenate([prev, x_ref[...], nxt], axis=0)
        ce = _conv_taps(xe, w)
        _, vjp = jax.vjp(lambda c: _conv_post_block(c, g), ce)
        (dce,) = vjp(jnp.concatenate([dy, dyn], axis=0))
        n = tr + 8
        dx = dce * w[3]
        for s in (1, 2, 3):
            dx = dx + pltpu.roll(dce, n - s, 0) * w[3 - s]
        dx_ref[...] = dx[:tr]
        dc = dce[:tr]
        row4 = lax.broadcasted_iota(jnp.int32, (4, cl), 0)
        dw = jnp.zeros((4, cl), F32)
        for s in (0, 1, 2, 3):
            xs = xe[8:8 + tr] if s == 0 else pltpu.roll(xe, s, 0)[8:8 + tr]
            dw = dw + jnp.where(row4 == 3 - s, jnp.sum(dc * xs, axis=0, keepdims=True), 0.0)

        @pl.when(r == 0)
        def _():
            dw_ref[...] = dw

        @pl.when(r > 0)
        def _():
            dw_ref[...] += dw

    def col_q(g):
        return jnp.minimum(g, nq - 1), g < nq

    def col_k(g):
        return jnp.clip(g - nq, 0, nq - 1), jnp.logical_and(g >= nq, g < 2 * nq)

    def col_v(g):
        return jnp.maximum(g - 2 * nq, 0), g >= 2 * nq

    def blk(colf):
        def index(g, r):
            col, used = colf(g)
            return jnp.where(used, r, 0), col
        return pl.BlockSpec((tr, cl), index)

    def nblk(colf):
        def index(g, r):
            col, used = colf(g)
            return jnp.where(used, jnp.minimum((r + 1) * (tr // 8), last8), 0), col
        return pl.BlockSpec((8, cl), index)

    return _pcall(
        body, name=name, grid=(width // cl, nr),
        in_specs=[pl.BlockSpec((tr, cl), lambda g, r: (r, g)),
                  pl.BlockSpec((8, cl), lambda g, r: (jnp.maximum(r * (tr // 8) - 1, 0), g)),
                  pl.BlockSpec((8, cl), lambda g, r: (jnp.minimum((r + 1) * (tr // 8), last8), g)),
                  pl.BlockSpec((4, cl), lambda g, r: (0, g)),
                  blk(col_q), blk(col_k), blk(col_v), nblk(col_q), nblk(col_k), nblk(col_v)],
        out_specs=[pl.BlockSpec((tr, cl), lambda g, r: (r, g)),
                   pl.BlockSpec((4, cl), lambda g, r: (0, g))],
        out_shape=[jax.ShapeDtypeStruct((lp, width), F32), jax.ShapeDtypeStruct((4, width), F32)],
        compiler_params=_params(2),
    )(x, x, x, w, dq, dk, dv, dq, dk, dv)


def _bmm(a, b, dims, prec=None):
    (ca,), (cb,) = dims
    return lax.dot_general(a, b, (((ca + 1,), (cb + 1,)), ((0,), (0,))), precision=prec,
                           preferred_element_type=F32)


def _inv_impl(m):
    c = m.shape[-1]
    ii = lax.broadcasted_iota(jnp.int32, (c, c), 0)
    jj = lax.broadcasted_iota(jnp.int32, (c, c), 1)
    eye = (ii == jj).astype(F32)

    def same_block(shift):
        return (ii >> shift) == (jj >> shift)

    n1 = jnp.where(same_block(3), -m, 0.0)
    n2 = _bmm(n1, n1, _NN, _X3)
    n4 = _bmm(n2, n2, _NN, _X3)
    d = _bmm(_bmm(eye + n1, eye + n2, _NN, _X3), eye + n4, _NN, _X3)
    shift = 3
    while (1 << shift) < c:
        low = jnp.where(jnp.logical_and(same_block(shift + 1), jnp.logical_not(same_block(shift))), m, 0.0)
        d = d - _bmm(d, _bmm(low, d, _NN, _X3), _NN, _X3)
        shift += 1
    return d


@jax.custom_vjp
def _inv_unit_lower(m):
    return _inv_impl(m)


def _inv_f(m):
    t = _inv_impl(m)
    return t, t


def _inv_b(t, dt):
    c = t.shape[-1]
    ii = lax.broadcasted_iota(jnp.int32, (c, c), 0)
    jj = lax.broadcasted_iota(jnp.int32, (c, c), 1)
    gm = _bmm(t, _bmm(dt, t, _NT, _X3), _TN, _X3)
    return (jnp.where(ii > jj, -gm, 0.0),)


_inv_unit_lower.defvjp(_inv_f, _inv_b)


@jax.custom_vjp
def _inv_known(m, t):
    return t


def _inv_known_f(m, t):
    return t, t


def _inv_known_b(t, dt):
    return _inv_b(t, dt) + (jnp.zeros_like(t),)


_inv_known.defvjp(_inv_known_f, _inv_known_b)


GDN_HEADS_PER_STEP = 16


def _gdn_group(q, k, v, beta_blk, gc_blk, states, h0, t_known=None):
    hp = GDN_HEADS_PER_STEP
    c = q.shape[0]
    lane = lax.broadcasted_iota(jnp.int32, (1, LANE), 1)
    row8 = lax.broadcasted_iota(jnp.int32, (max(8, hp), LANE), 0)
    lane8 = lax.broadcasted_iota(jnp.int32, (max(8, hp), LANE), 1)
    gcr_all = _hdot((lane8 == h0 + row8).astype(F32), gc_blk, _NT)
    betas, gccs = [], []
    for i in range(hp):
        onehot = (lane == h0 + i).astype(F32)
        betas.append(jnp.sum(beta_blk * onehot, axis=1, keepdims=True))
        gccs.append(jnp.sum(gc_blk * onehot, axis=1, keepdims=True))
    def stack(xs):
        return jnp.concatenate([x[None] for x in xs], axis=0)

    beta = stack(betas)
    gcc = stack(gccs)
    gcr = stack([gcr_all[i:i + 1] for i in range(hp)])
    qh = stack([q[:, (i // 2) * LANE:(i // 2 + 1) * LANE] for i in range(hp)])
    kh = stack([k[:, (i // 2) * LANE:(i // 2 + 1) * LANE] for i in range(hp)])
    vh = stack([v[:, i * LANE:(i + 1) * LANE] for i in range(hp)])
    state = stack(states)
    ii = lax.broadcasted_iota(jnp.int32, (c, c), 0)
    jj = lax.broadcasted_iota(jnp.int32, (c, c), 1)
    incl = ii >= jj
    dec = jnp.where(incl, jnp.exp(jnp.where(incl, gcc - gcr, 0.0)), 0.0)
    eg = jnp.exp(gcc)
    m = _bmm(kh, kh, _NT) * beta * jnp.where(ii > jj, dec, 0.0)
    t = _inv_unit_lower(m) if t_known is None else _inv_known(m, t_known)
    u = _bmm(t, vh * beta, _NN, _X3)
    w = _bmm(t, kh * (beta * eg), _NN, _X3)
    attn = _bmm(qh, kh, _NT) * dec
    rows = lax.broadcasted_iota(jnp.int32, (c, 1), 0)
    gl = jnp.sum(jnp.where(rows == c - 1, gcc, 0.0), axis=1, keepdims=True)
    v_new = u - _bmm(w, state, _NN)
    o = _bmm(qh * eg, state, _NN) + _bmm(attn, v_new, _NN)
    new_state = state * jnp.exp(gl) + _bmm(kh * jnp.exp(gl - gcc), v_new, _TN)
    return jnp.concatenate([o[i] for i in range(hp)], axis=1), tuple(new_state[i] for i in range(hp)), t


def _gdn_specs(nc, rev):
    def cidx(n):
        return (nc - 1 - n) if rev else n
    hp = GDN_HEADS_PER_STEP
    nqk = GDN_QK_HEADS
    c = GDN_CHUNK
    nq = 2 * nqk // hp
    q_spec = pl.BlockSpec((c, hp // 2 * LANE), lambda n, g: (cidx(n), g))
    k_spec = pl.BlockSpec((c, hp // 2 * LANE), lambda n, g: (cidx(n), nq + g))
    v_spec = pl.BlockSpec((c, hp * LANE), lambda n, g: (cidx(n), nq + g))
    s_spec = pl.BlockSpec((c, LANE), lambda n, g: (cidx(n), 0))
    o_spec = pl.BlockSpec((c, hp * LANE), lambda n, g: (cidx(n), g))
    ck_spec = pl.BlockSpec((hp, 1, GDN_DK, LANE), lambda n, g: (g, cidx(n), 0, 0))
    return q_spec, k_spec, v_spec, s_spec, o_spec, ck_spec


def _gdn_t_spec(nc, rev):
    return pl.BlockSpec((GDN_HEADS_PER_STEP, 1, GDN_CHUNK, GDN_CHUNK),
                        lambda n, g: (g, (nc - 1 - n) if rev else n, 0, 0))


def _gdn_fwd(qkv, beta, gc, shard, name):
    lp = qkv.shape[0]
    nc = lp // GDN_CHUNK
    nh = GDN_V_HEADS
    hp = GDN_HEADS_PER_STEP
    ng = nh // hp
    q_spec, k_spec, v_spec, s_spec, o_spec, ck_spec = _gdn_specs(nc, False)

    def body(q_ref, k_ref, v_ref, b_ref, g_ref, x_ref, o_ref, ck_ref, t_ref, all_ref,
             state, send_sems, recv_sems, local_sem):
        n = pl.program_id(0)
        g = pl.program_id(1)
        x, y, c = _my_place()

        def local_copy():
            return pltpu.make_async_copy(x_ref, all_ref.at[2 * x + y], local_sem)

        def remote_copy(k, px, py, slot):
            return pltpu.make_async_remote_copy(
                src_ref=x_ref, dst_ref=all_ref.at[slot], send_sem=send_sems.at[k], recv_sem=recv_sems.at[k],
                device_id=(px, py, c), device_id_type=MESH)

        @pl.when(jnp.logical_and(n == 0, g == 0))
        def _():
            local_copy().start()
            for k, (px, py) in enumerate(_other_chips(x, y)):
                remote_copy(k, px, py, 2 * x + y).start()

        @pl.when(n == 0)
        def _():
            for i in range(hp):
                state[g * hp + i] = jnp.zeros((GDN_DK, LANE), F32)

        states = tuple(state[g * hp + i] for i in range(hp))
        for i in range(hp):
            ck_ref[i, 0] = states[i]
        o, new_states, t = _gdn_group(q_ref[...], k_ref[...], v_ref[...], b_ref[...], g_ref[...], states, g * hp)
        o_ref[...] = o
        t_ref[:, 0] = t
        for i in range(hp):
            state[g * hp + i] = new_states[i]

        @pl.when(jnp.logical_and(n == nc - 1, g == ng - 1))
        def _():
            for k, (px, py) in enumerate(_other_chips(x, y)):
                remote_copy(k, px, py, 2 * px + py).wait_recv()
            for k, (px, py) in enumerate(_other_chips(x, y)):
                remote_copy(k, px, py, 2 * x + y).wait_send()
            local_copy().wait()

    return _pcall(
        body, name=name, grid=(nc, ng),
        in_specs=[q_spec, k_spec, v_spec, s_spec, s_spec, _ANY],
        out_specs=[o_spec, ck_spec, _gdn_t_spec(nc, False), _ANY],
        out_shape=[jax.ShapeDtypeStruct((lp, GDN_V_W), F32),
                   jax.ShapeDtypeStruct((nh, nc, GDN_DK, LANE), F32),
                   jax.ShapeDtypeStruct((nh, nc, GDN_CHUNK, GDN_CHUNK), F32),
                   jax.ShapeDtypeStruct((4,) + shard.shape, shard.dtype)],
        scratch_shapes=[pltpu.VMEM((nh, GDN_DK, LANE), F32), pltpu.SemaphoreType.DMA((3,)),
                        pltpu.SemaphoreType.DMA((3,)), pltpu.SemaphoreType.DMA],
        compiler_params=_params(2),
    )(qkv, qkv, qkv, beta, gc, shard)


def _gdn_bwd(qkv, beta, gc, ckpt, t_saved, do, parts, name):
    lp = qkv.shape[0]
    nc = lp // GDN_CHUNK
    nh = GDN_V_HEADS
    hp = GDN_HEADS_PER_STEP
    ng = nh // hp
    q_spec, k_spec, v_spec, s_spec, o_spec, ck_spec = _gdn_specs(nc, True)

    def body(q_ref, k_ref, v_ref, b_ref, g_ref, ck_ref, t_ref, do_ref, p_ref,
             dq_ref, dk_ref, dv_ref, db_ref, dg_ref, from_ref, dstate, send_sems, recv_sems, local_sem):
        n = pl.program_id(0)
        g = pl.program_id(1)
        x, y, c = _my_place()
        me = 2 * x + y

        def local_copy():
            return pltpu.make_async_copy(p_ref.at[me], from_ref.at[me], local_sem)

        def remote_copy(k, px, py, src_slot, dst_slot):
            return pltpu.make_async_remote_copy(
                src_ref=p_ref.at[src_slot], dst_ref=from_ref.at[dst_slot], send_sem=send_sems.at[k],
                recv_sem=recv_sems.at[k], device_id=(px, py, c), device_id_type=MESH)

        @pl.when(jnp.logical_and(n == 0, g == 0))
        def _():
            local_copy().start()
            for k, (px, py) in enumerate(_other_chips(x, y)):
                remote_copy(k, px, py, 2 * px + py, me).start()

        @pl.when(jnp.logical_and(n == nc - 1, g == ng - 1))
        def _():
            for k, (px, py) in enumerate(_other_chips(x, y)):
                remote_copy(k, px, py, me, 2 * px + py).wait_recv()
            for k, (px, py) in enumerate(_other_chips(x, y)):
                remote_copy(k, px, py, 2 * px + py, me).wait_send()
            local_copy().wait()

        @pl.when(n == 0)
        def _():
            for i in range(hp):
                dstate[g * hp + i] = jnp.zeros((GDN_DK, LANE), F32)

        states = tuple(ck_ref[i, 0] for i in range(hp))
        t_known = t_ref[:, 0]
        _, vjp = jax.vjp(lambda q, k, v, b, gg, s: _gdn_group(q, k, v, b, gg, s, g * hp, t_known)[:2],
                         q_ref[...], k_ref[...], v_ref[...], b_ref[...], g_ref[...], states)
        dq, dk, dv, db, dg, ds = vjp((do_ref[...], tuple(dstate[g * hp + i] for i in range(hp))))
        dq_ref[...] = dq
        dk_ref[...] = dk
        dv_ref[...] = dv
        for i in range(hp):
            dstate[g * hp + i] = ds[i]

        @pl.when(g == 0)
        def _():
            db_ref[...] = db
            dg_ref[...] = dg

        @pl.when(g > 0)
        def _():
            db_ref[...] += db
            dg_ref[...] += dg

    qk_shape = jax.ShapeDtypeStruct((lp, GDN_QK_W), F32)
    big = jax.ShapeDtypeStruct((lp, GDN_V_W), F32)
    small = jax.ShapeDtypeStruct((lp, LANE), F32)
    dq_spec = pl.BlockSpec((GDN_CHUNK, hp // 2 * LANE), lambda n, g: (nc - 1 - n, g))
    return _pcall(
        body, name=name, grid=(nc, ng),
        in_specs=[q_spec, k_spec, v_spec, s_spec, s_spec, ck_spec, _gdn_t_spec(nc, True), o_spec, _ANY],
        out_specs=[dq_spec, dq_spec, o_spec, s_spec, s_spec, _ANY],
        out_shape=[qk_shape, qk_shape, big, small, small, jax.ShapeDtypeStruct(parts.shape, parts.dtype)],
        scratch_shapes=[pltpu.VMEM((nh, GDN_DK, LANE), F32), pltpu.SemaphoreType.DMA((3,)),
                        pltpu.SemaphoreType.DMA((3,)), pltpu.SemaphoreType.DMA],
        compiler_params=_params(2),
    )(qkv, qkv, qkv, beta, gc, ckpt, t_saved, do, parts)


LOG2E = 1.4426950408889634
LN2 = 0.6931471805599453
Q_PRESCALE = MLA_QK ** -0.5 * LOG2E


ATT_SUB = 256
ATT_HEADS_PER_STEP = 8
ATT_BWD_HEADS_PER_STEP = 2


def _att_mask(i, j, tb, transposed):
    r = lax.broadcasted_iota(jnp.int32, (tb, tb), 0)
    c = lax.broadcasted_iota(jnp.int32, (tb, tb), 1)
    qpos, kpos = (i * tb + c, j * tb + r) if transposed else (i * tb + r, j * tb + c)
    return jnp.logical_and(kpos <= qpos, kpos >= FRONT)


def _causal_pairs(nb, by_key):
    if by_key:
        pairs = [(i, j) for j in range(nb) for i in range(j, nb)]
    else:
        pairs = [(i, j) for i in range(nb) for j in range(i + 1)]
    return jnp.array([p[0] for p in pairs], jnp.int32), jnp.array([p[1] for p in pairs], jnp.int32)


def _masked_and_plain(i, j, step):
    edge = jnp.logical_or(j == i, j == 0)

    @pl.when(jnp.logical_and(edge, j <= i))
    def _():
        step(True)

    @pl.when(jnp.logical_and(jnp.logical_not(edge), j < i))
    def _():
        step(False)


def _cat(a_ref, b_ref):
    return jnp.concatenate([a_ref[...], b_ref[...]], axis=1)


def _flash_fwd(qn, qr, kn, kr, v, name, tb=ROW_ALIGN):
    lp = qn.shape[0]
    nb = lp // tb
    nh = MLA_HEADS
    hp = ATT_HEADS_PER_STEP
    qi, kj = _causal_pairs(nb, by_key=False)

    def body(qi_ref, kj_ref, qn_ref, qr_ref, kn_ref, kr_ref, v_ref, o_ref, lse_ref, m_s, l_s, acc):
        t = pl.program_id(1)
        i, j = qi_ref[t], kj_ref[t]

        @pl.when(j == 0)
        def _():
            m_s[...] = jnp.full_like(m_s, NEG)
            l_s[...] = jnp.zeros_like(l_s)
            acc[...] = jnp.zeros_like(acc)

        def step(masked):
            n_sub = tb // ATT_SUB
            kr = kr_ref[...]
            for e in range(hp):
                lanes = pl.ds(e * LANE, LANE)
                k = jnp.concatenate([kn_ref[:, lanes], kr], axis=1)
                v = v_ref[:, lanes]

                def scores(r, lanes=lanes, k=k):
                    rows = pl.ds(r * ATT_SUB, ATT_SUB)
                    return _dot(jnp.concatenate([qn_ref[rows, lanes], qr_ref[rows, lanes]], axis=1), k, _NT)

                s_next = scores(0)
                for r in range(n_sub):
                    s = s_next
                    if r + 1 < n_sub:
                        s_next = scores(r + 1)
                    rows = pl.ds(r * ATT_SUB, ATT_SUB)
                    if masked:
                        qpos = i * tb + r * ATT_SUB + lax.broadcasted_iota(jnp.int32, (ATT_SUB, tb), 0)
                        kpos = j * tb + lax.broadcasted_iota(jnp.int32, (ATT_SUB, tb), 1)
                        s = jnp.where(jnp.logical_and(kpos <= qpos, kpos >= FRONT), s, NEG)
                    m_old = m_s[e, rows, :]
                    m_new = jnp.maximum(m_old, jnp.max(s, axis=1, keepdims=True))
                    alpha = jnp.exp2(m_old - m_new)
                    p = jnp.exp2(s - m_new)
                    l_s[e, rows, :] = alpha * l_s[e, rows, :] + jnp.sum(p, axis=1, keepdims=True)
                    acc[e, rows, :] = alpha * acc[e, rows, :] + _dot(p.astype(BF16), v, _NN)
                    m_s[e, rows, :] = m_new

        _masked_and_plain(i, j, step)

        @pl.when(j == i)
        def _():
            for e in range(hp):
                lanes = pl.ds(e * LANE, LANE)
                o_ref[:, lanes] = acc[e] / l_s[e]
                lse_ref[:, lanes] = jnp.broadcast_to(m_s[e] + jnp.log(l_s[e]) * LOG2E, (tb, LANE))

    qspec = pl.BlockSpec((tb, hp * LANE), lambda h, t, qi_, kj_: (qi_[t], h))
    kspec = pl.BlockSpec((tb, hp * LANE), lambda h, t, qi_, kj_: (kj_[t], h))
    krspec = pl.BlockSpec((tb, LANE), lambda h, t, qi_, kj_: (kj_[t], 0))
    shp = jax.ShapeDtypeStruct((lp, nh * LANE), F32)
    return _pcall(
        body, name=name, out_shape=[shp, shp],
        grid_spec=pltpu.PrefetchScalarGridSpec(
            num_scalar_prefetch=2, grid=(nh // hp, qi.shape[0]),
            in_specs=[qspec, qspec, kspec, krspec, kspec], out_specs=[qspec, qspec],
            scratch_shapes=[pltpu.VMEM((hp, tb, 1), F32), pltpu.VMEM((hp, tb, 1), F32),
                            pltpu.VMEM((hp, tb, LANE), F32)]),
        compiler_params=_params(2),
    )(qi, kj, qn, qr, kn, kr, v)


def _flash_bwd(qn, qr, kn, kr, v, o, do, lse, name, tb=ROW_ALIGN):
    lp = qn.shape[0]
    nb = lp // tb
    nh = MLA_HEADS
    hp = ATT_BWD_HEADS_PER_STEP
    qi, kj = _causal_pairs(nb, by_key=True)
    n_pairs = qi.shape[0]
    knt, krt = kn.T, kr.T

    def body(qi_ref, kj_ref, qn_ref, qr_ref, kn_ref, kr_ref, knt_ref, krt_ref, v_ref, o_ref, do_ref, lse_ref,
             dqnt_hbm, dqrt_hbm, dkn_ref, dkr_ref, dv_ref, dk_acc, dv_acc, dqn_acc, dqr_acc, out_sems):
        g = pl.program_id(0)
        t = pl.program_id(1)
        i, j = qi_ref[t], kj_ref[t]

        @pl.when(t == 0)
        def _():
            dqn_acc[...] = jnp.zeros_like(dqn_acc)
            dqr_acc[...] = jnp.zeros_like(dqr_acc)

        @pl.when(i == j)
        def _():
            dk_acc[...] = jnp.zeros_like(dk_acc)
            dv_acc[...] = jnp.zeros_like(dv_acc)

        def step(masked):
            kr = kr_ref[...]
            krt_blk = krt_ref[...]
            lane = lax.broadcasted_iota(jnp.int32, (8, LANE), 1)
            for e in range(hp):
                lanes = pl.ds(e * LANE, LANE)
                q = jnp.concatenate([qn_ref[:, lanes], qr_ref[:, lanes]], axis=1)
                st = _dot(jnp.concatenate([kn_ref[:, lanes], kr], axis=1), q, _NT)
                if masked:
                    st = jnp.where(_att_mask(i, j, tb, True), st, NEG)
                do_blk = do_ref[:, lanes]
                lse_row = _hdot((lane == 0).astype(F32), lse_ref[:, lanes], _NT)[0:1]
                delta_row = _hdot(jnp.ones((8, LANE), F32), do_blk * o_ref[:, lanes], _NT)[0:1]
                pt = jnp.exp2(st - lse_row)
                do_b = do_blk.astype(BF16)
                dv_acc[e] += _dot(pt.astype(BF16), do_b, _NN)
                dpt = _dot(v_ref[:, lanes], do_b, _NT)
                dst = (pt * (dpt - delta_row)).astype(BF16)
                dk_acc[e] += _dot(dst, q, _NN)
                dqn_acc[e * nb + i] += _dot(knt_ref[pl.ds(e * LANE, LANE), :], dst, _NN) * LN2
                dqr_acc[e * nb + i] += _dot(krt_blk, dst, _NN) * LN2

        _masked_and_plain(i, j, step)

        @pl.when(i == nb - 1)
        def _():
            for e in range(hp):
                lanes = pl.ds(e * LANE, LANE)
                dkn_ref[:, lanes] = dk_acc[e, :, :LANE] * LN2
                dkr_ref[:, lanes] = dk_acc[e, :, LANE:] * LN2
                dv_ref[:, lanes] = dv_acc[e]

        @pl.when(t == n_pairs - 1)
        def _():
            dst_rows = pl.ds(g * (hp * nb), hp * nb)
            cn = pltpu.make_async_copy(dqn_acc, dqnt_hbm.at[dst_rows], out_sems.at[0])
            cr = pltpu.make_async_copy(dqr_acc, dqrt_hbm.at[dst_rows], out_sems.at[1])
            cn.start()
            cr.start()
            cn.wait()
            cr.wait()

    qspec = pl.BlockSpec((tb, hp * LANE), lambda h, t, qi_, kj_: (qi_[t], h))
    kspec = pl.BlockSpec((tb, hp * LANE), lambda h, t, qi_, kj_: (kj_[t], h))
    krspec = pl.BlockSpec((tb, LANE), lambda h, t, qi_, kj_: (kj_[t], 0))
    ktspec = pl.BlockSpec((hp * LANE, tb), lambda h, t, qi_, kj_: (h, kj_[t]))
    krtspec = pl.BlockSpec((LANE, tb), lambda h, t, qi_, kj_: (0, kj_[t]))
    shp = jax.ShapeDtypeStruct((lp, nh * LANE), F32)
    dqt_shape = jax.ShapeDtypeStruct((nh * nb, LANE, tb), F32)
    dqnt, dqrt, dkn, dkr, dv = _pcall(
        body, name=name, out_shape=[dqt_shape, dqt_shape, shp, shp, shp],
        grid_spec=pltpu.PrefetchScalarGridSpec(
            num_scalar_prefetch=2, grid=(nh // hp, n_pairs),
            in_specs=[qspec, qspec, kspec, krspec, ktspec, krtspec, kspec, qspec, qspec, qspec],
            out_specs=[_ANY, _ANY, kspec, kspec, kspec],
            scratch_shapes=[pltpu.VMEM((hp, tb, 2 * LANE), F32), pltpu.VMEM((hp, tb, LANE), F32),
                            pltpu.VMEM((hp * nb, LANE, tb), F32), pltpu.VMEM((hp * nb, LANE, tb), F32),
                            pltpu.SemaphoreType.DMA((2,))]),
        compiler_params=_params(2),
    )(qi, kj, qn, qr, kn, kr, knt, krt, v, o, do, lse)

    def rows_major(a):
        return a.reshape(nh, nb, LANE, tb).transpose(1, 3, 0, 2).reshape(lp, nh * LANE)

    return rows_major(dqnt), rows_major(dqrt), dkn, dkr, dv


ELEMENTWISE_BLOCK_BYTES = 1 << 20


def _row_tile(rows, width, copies=1):
    for t in (1024, 512, 256, 128, 64, 32, 16, 8):
        if rows % t == 0 and t * width * 4 * copies <= ELEMENTWISE_BLOCK_BYTES:
            return t
    return rows


def _adamw(w, g, m, v, name):
    rows, width = w.shape
    tr = _row_tile(rows, width)

    def body(w_ref, g_ref, m_ref, v_ref, d_ref, nm_ref, nv_ref):
        gg = g_ref[...]
        nm = ADAM_B1 * m_ref[...] + (1.0 - ADAM_B1) * gg
        nv = ADAM_B2 * v_ref[...] + (1.0 - ADAM_B2) * jnp.square(gg)
        m_hat = nm / (1.0 - ADAM_B1 ** ADAM_STEP)
        v_hat = nv / (1.0 - ADAM_B2 ** ADAM_STEP)
        d_ref[...] = -ADAM_LR * (m_hat / (jnp.sqrt(v_hat) + ADAM_EPS) + ADAM_WD * w_ref[...])
        nm_ref[...] = nm
        nv_ref[...] = nv

    spec = pl.BlockSpec((tr, width), lambda r: (r, 0))
    shp = jax.ShapeDtypeStruct((rows, width), F32)
    return _pcall(body, name=name, grid=(rows // tr,), in_specs=[spec] * 4, out_specs=[spec] * 3,
                  out_shape=[shp] * 3, compiler_params=_params(1))(w, g, m, v)


def _add_pair(a, b, name):
    s, rows, width = b.shape
    tr = _row_tile(rows, width)
    nt = rows // tr

    def body(c_ref, a_ref, b_ref, o_ref):
        o_ref[...] = a_ref[...] + b_ref[...]

    spec = pl.BlockSpec((1, tr, width), lambda i, r, c_ref: (i, r, 0))
    return _pcall(
        body, name=name, out_shape=jax.ShapeDtypeStruct(b.shape, F32),
        grid_spec=pltpu.PrefetchScalarGridSpec(
            num_scalar_prefetch=1, grid=(s, nt),
            in_specs=[pl.BlockSpec((1, tr, width), lambda i, r, c_ref: (i, c_ref[0] * nt + r, 0)), spec],
            out_specs=spec),
        compiler_params=_params(2),
    )(_core_index(), a, b)


def _sum_slots(a, name):
    s, rows, width = a.shape
    tr = _row_tile(rows, width, copies=s)

    def body(a_ref, o_ref):
        tot = a_ref[0]
        for k in range(1, s):
            tot = tot + a_ref[k]
        o_ref[...] = tot

    return _pcall(body, name=name, grid=(rows // tr,),
                  in_specs=[pl.BlockSpec((s, tr, width), lambda r: (0, r, 0))],
                  out_specs=pl.BlockSpec((tr, width), lambda r: (r, 0)),
                  out_shape=jax.ShapeDtypeStruct((rows, width), F32), compiler_params=_params(1))(a)


_ANY = pl.BlockSpec(memory_space=pl.ANY)


def _my_place():
    return lax.axis_index("x"), lax.axis_index("y"), lax.axis_index("c")


def _core_index():
    return lax.axis_index("c").astype(jnp.int32).reshape(1)


def _other_chips(x, y):
    return [(1 - x, y), (x, 1 - y), (1 - x, 1 - y)]


def _gather_shards(flat, name):
    rows, width = flat.shape

    def body(x_ref, out_ref, send_sems, recv_sems, local_sem):
        x, y, c = _my_place()
        mine = pltpu.make_async_copy(x_ref, out_ref.at[2 * x + y], local_sem)
        mine.start()
        sends = []
        for k, (px, py) in enumerate(_other_chips(x, y)):
            cp = pltpu.make_async_remote_copy(
                src_ref=x_ref, dst_ref=out_ref.at[2 * x + y], send_sem=send_sems.at[k], recv_sem=recv_sems.at[k],
                device_id=(px, py, c), device_id_type=MESH)
            cp.start()
            sends.append(cp)
        for k, (px, py) in enumerate(_other_chips(x, y)):
            pltpu.make_async_remote_copy(
                src_ref=x_ref, dst_ref=out_ref.at[2 * px + py], send_sem=send_sems.at[k], recv_sem=recv_sems.at[k],
                device_id=(px, py, c), device_id_type=MESH).wait_recv()
        for cp in sends:
            cp.wait_send()
        mine.wait()

    return _pcall(
        body, name=name, in_specs=[_ANY], out_specs=_ANY,
        out_shape=jax.ShapeDtypeStruct((4, rows, width), flat.dtype),
        scratch_shapes=[pltpu.SemaphoreType.DMA((3,)), pltpu.SemaphoreType.DMA((3,)), pltpu.SemaphoreType.DMA],
    )(flat)


def _sibling_split(g, name):
    s, rows, width = g.shape
    half = rows // 2
    tr = _row_tile(half, width)
    nt = half // tr

    def body(c_ref, g_blk, got_ref, send_sem, recv_sem):
        k = pl.program_id(0)
        t = pl.program_id(1)
        x, y, c = _my_place()
        cp = pltpu.make_async_remote_copy(
            src_ref=g_blk.at[0], dst_ref=got_ref.at[k, pl.ds(pl.multiple_of(t * tr, 8), tr), :],
            send_sem=send_sem, recv_sem=recv_sem, device_id=(x, y, 1 - c), device_id_type=MESH)
        cp.start()
        cp.wait_send()

        @pl.when(jnp.logical_and(k == s - 1, t == nt - 1))
        def _():
            pltpu.make_async_remote_copy(
                src_ref=got_ref, dst_ref=got_ref, send_sem=send_sem, recv_sem=recv_sem,
                device_id=(x, y, 1 - c), device_id_type=MESH).wait_recv()

    return _pcall(
        body, name=name, out_shape=jax.ShapeDtypeStruct((s, half, width), g.dtype),
        grid_spec=pltpu.PrefetchScalarGridSpec(
            num_scalar_prefetch=1, grid=(s, nt),
            in_specs=[pl.BlockSpec((1, tr, width), lambda k, t, c_ref: (k, (1 - c_ref[0]) * nt + t, 0))],
            out_specs=_ANY,
            scratch_shapes=[pltpu.SemaphoreType.DMA, pltpu.SemaphoreType.DMA]),
        compiler_params=_params(2),
    )(_core_index(), g)


def _chip_scatter(p, name):
    s, rows, width = p.shape

    def body(p_ref, out_ref, send_sems, recv_sems, local_sem):
        x, y, c = _my_place()
        me = 2 * x + y
        mine = pltpu.make_async_copy(p_ref.at[me], out_ref.at[me], local_sem)
        mine.start()
        sends = []
        for k, (px, py) in enumerate(_other_chips(x, y)):
            cp = pltpu.make_async_remote_copy(
                src_ref=p_ref.at[2 * px + py], dst_ref=out_ref.at[me], send_sem=send_sems.at[k],
                recv_sem=recv_sems.at[k], device_id=(px, py, c), device_id_type=MESH)
            cp.start()
            sends.append(cp)
        for k, (px, py) in enumerate(_other_chips(x, y)):
            pltpu.make_async_remote_copy(
                src_ref=p_ref.at[me], dst_ref=out_ref.at[2 * px + py], send_sem=send_sems.at[k],
                recv_sem=recv_sems.at[k], device_id=(px, py, c), device_id_type=MESH).wait_recv()
        for cp in sends:
            cp.wait_send()
        mine.wait()

    return _pcall(
        body, name=name, in_specs=[_ANY], out_specs=_ANY,
        out_shape=jax.ShapeDtypeStruct(p.shape, p.dtype),
        scratch_shapes=[pltpu.SemaphoreType.DMA((3,)), pltpu.SemaphoreType.DMA((3,)), pltpu.SemaphoreType.DMA],
    )(p)


def _sibling_join(qh, name):
    half, width = qh.shape
    tr = _row_tile(half, width)
    nt = half // tr

    def body(q_blk, out_ref, send_sem, recv_sem, local_sem):
        t = pl.program_id(0)
        x, y, c = _my_place()
        dst = out_ref.at[pl.ds(pl.multiple_of(c * half + t * tr, 8), tr), :]
        cp = pltpu.make_async_remote_copy(
            src_ref=q_blk, dst_ref=dst, send_sem=send_sem, recv_sem=recv_sem,
            device_id=(x, y, 1 - c), device_id_type=MESH)
        cp.start()
        mine = pltpu.make_async_copy(q_blk, dst, local_sem)
        mine.start()
        cp.wait_send()
        mine.wait()

        @pl.when(t == nt - 1)
        def _():
            theirs = out_ref.at[pl.ds(pl.multiple_of((1 - c) * half, 8), half), :]
            pltpu.make_async_remote_copy(
                src_ref=theirs, dst_ref=theirs, send_sem=send_sem, recv_sem=recv_sem,
                device_id=(x, y, 1 - c), device_id_type=MESH).wait_recv()

    return _pcall(
        body, name=name, grid=(nt,),
        in_specs=[pl.BlockSpec((tr, width), lambda t: (t, 0))], out_specs=_ANY,
        out_shape=jax.ShapeDtypeStruct((2 * half, width), qh.dtype),
        scratch_shapes=[pltpu.SemaphoreType.DMA, pltpu.SemaphoreType.DMA, pltpu.SemaphoreType.DMA],
        compiler_params=_params(1),
    )(qh)


def _all_sum_small(part, name):
    rows, width = part.shape

    def body(p_ref, out_ref, land, send_sems, recv_sems):
        x, y, c = _my_place()
        me = 4 * x + 2 * y + c
        land[me] = p_ref[...]
        sends = []
        for k in range(1, 8):
            peer = (x ^ (k >> 2), y ^ ((k >> 1) & 1), c ^ (k & 1))
            cp = pltpu.make_async_remote_copy(
                src_ref=p_ref, dst_ref=land.at[me], send_sem=send_sems.at[k - 1], recv_sem=recv_sems.at[k - 1],
                device_id=peer, device_id_type=MESH)
            cp.start()
            sends.append(cp)
        for k in range(1, 8):
            px, py, pc = x ^ (k >> 2), y ^ ((k >> 1) & 1), c ^ (k & 1)
            pltpu.make_async_remote_copy(
                src_ref=p_ref, dst_ref=land.at[4 * px + 2 * py + pc], send_sem=send_sems.at[k - 1],
                recv_sem=recv_sems.at[k - 1], device_id=(px, py, pc), device_id_type=MESH).wait_recv()
        for cp in sends:
            cp.wait_send()
        tot = land[0]
        for k in range(1, 8):
            tot = tot + land[k]
        out_ref[...] = tot

    vmem = pl.BlockSpec(memory_space=pltpu.VMEM)
    return _pcall(
        body, name=name, in_specs=[vmem], out_specs=vmem,
        out_shape=jax.ShapeDtypeStruct((rows, width), F32),
        scratch_shapes=[pltpu.VMEM((8, rows, width), F32), pltpu.SemaphoreType.DMA((7,)),
                        pltpu.SemaphoreType.DMA((7,))],
    )(part)


def _big_layout(shards):
    return [(a.shape[0], a.shape[1], ax) for a, ax in shards]


FLAT_ROW_MULTIPLE = 2048


def _pack_shards(arrs, row_multiple=FLAT_ROW_MULTIPLE):
    flat = jnp.concatenate([a.reshape(-1) for a in arrs])
    return jnp.pad(flat, (0, -flat.shape[0] % (row_multiple * LANE))).reshape(-1, LANE)


def _unpack_shards(flat, layout):
    flat = flat.reshape(-1)
    out, off = [], 0
    for r, c, _ in layout:
        out.append(flat[off:off + r * c].reshape(r, c))
        off += r * c
    return out


def _unpack_full(gathered, layout):
    g = gathered.reshape(4, -1)
    out, off = [], 0
    for r, c, ax in layout:
        seg = g[:, off:off + r * c].reshape(4, r, c)
        out.append(seg.transpose(1, 0, 2).reshape(r, 4 * c) if ax == 1 else seg.reshape(4 * r, c))
        off += r * c
    return out


def _pack_full(fulls, layout):
    parts = []
    for a, (r, c, ax) in zip(fulls, layout):
        if ax == 1:
            parts.append(a.reshape(r, 4, c).transpose(1, 0, 2).reshape(4, r * c))
        else:
            parts.append(a.reshape(4, r * c))
    flat = jnp.concatenate(parts, axis=1)
    return jnp.pad(flat, ((0, 0), (0, -flat.shape[1] % (FLAT_ROW_MULTIPLE * LANE)))).reshape(4, -1, LANE)


def _pad_lanes(a, width=LANE):
    return jnp.pad(a, [(0, 0)] * (a.ndim - 1) + [(0, width - a.shape[-1])])


def _pack_small(arrs):
    rows = [_pad_lanes(a.reshape(1, -1), -(-a.size // LANE) * LANE).reshape(-1, LANE) for a in arrs]
    flat = jnp.concatenate(rows, axis=0)
    return jnp.pad(flat, ((0, -flat.shape[0] % 8), (0, 0)))


def _unpack_small(flat, shapes):
    out, off = [], 0
    for shp in shapes:
        n = math.prod(shp)
        nr = -(-n // LANE)
        out.append(flat[off:off + nr].reshape(-1)[:n].reshape(shp))
        off += nr
    return out


def kernel(x, meta_tokens, pre_norm, post_norm, gdn_w_in, gdn_conv_w, gdn_a_log, gdn_dt_bias, gdn_out_norm, gdn_w_out, kv_norm, kv_w_down, kv_latent_norm, kv_w_up, mla_w_in, mla_q_latent_norm, mla_w_q_up, mla_w_out, loss_target, m_meta_tokens, m_pre_norm, m_post_norm, m_gdn_w_in, m_gdn_conv_w, m_gdn_a_log, m_gdn_dt_bias, m_gdn_out_norm, m_gdn_w_out, m_kv_norm, m_kv_w_down, m_kv_latent_norm, m_kv_w_up, m_mla_w_in, m_mla_q_latent_norm, m_mla_w_q_up, m_mla_w_out, v_meta_tokens, v_pre_norm, v_post_norm, v_gdn_w_in, v_gdn_conv_w, v_gdn_a_log, v_gdn_dt_bias, v_gdn_out_norm, v_gdn_w_out, v_kv_norm, v_kv_w_down, v_kv_latent_norm, v_kv_w_up, v_mla_w_in, v_mla_q_latent_norm, v_mla_w_q_up, v_mla_w_out):
    seq = x.shape[1]
    d = D_MODEL
    lp = -(-(ROW0 + seq) // ROW_ALIGN) * ROW_ALIGN
    tail = lp - ROW0 - seq

    big_names = ["meta_tokens", "gdn_conv_w", "gdn_w_out", "kv_w_down", "kv_w_up", "mla_w_in", "mla_w_q_up",
                 "mla_w_out"]
    big_axis = [1, 1, 0, 0, 1, 1, 1, 0]
    big_w = [meta_tokens, gdn_conv_w[0], gdn_w_out[0], kv_w_down, kv_w_up, mla_w_in[0], mla_w_q_up[0], mla_w_out[0]]
    big_m = [m_meta_tokens, m_gdn_conv_w[0], m_gdn_w_out[0], m_kv_w_down, m_kv_w_up, m_mla_w_in[0], m_mla_w_q_up[0],
             m_mla_w_out[0]]
    big_v = [v_meta_tokens, v_gdn_conv_w[0], v_gdn_w_out[0], v_kv_w_down, v_kv_w_up, v_mla_w_in[0], v_mla_w_q_up[0],
             v_mla_w_out[0]]
    layout = _big_layout(list(zip(big_w, big_axis)))
    meta_f, conv_w = _unpack_full(
        _gather_shards(_pack_shards(big_w[:2], row_multiple=16), "gather_meta_conv"), layout[:2])
    mm_shards = [w.astype(BF16) for w in big_w[2:6]] + [(big_w[6] * Q_PRESCALE).astype(BF16), big_w[7].astype(BF16)]
    mm_flat = _pack_shards(mm_shards)
    w_in0_shards = _gather_shards(gdn_w_in[0].astype(BF16), "gather_gdn_w_in")
    w_in0 = jnp.concatenate([w_in0_shards[s] for s in range(4)], axis=1)
    win_cols = gdn_w_in.shape[2]

    nv = GDN_V_HEADS
    w_qkv = w_in0[:, :GDN_CONV_W]
    w_z0 = w_in0[:, GDN_CONV_W:GDN_CONV_W + GDN_V_W]
    w_b = _pad_lanes(w_in0[:, GDN_CONV_W + GDN_V_W:GDN_CONV_W + GDN_V_W + nv])
    w_a = _pad_lanes(w_in0[:, GDN_CONV_W + GDN_V_W + nv:])

    pre0, pre1 = pre_norm[0:1], pre_norm[1:2]
    post0, post1 = post_norm[0:1], post_norm[1:2]
    a_log = _pad_lanes(gdn_a_log)
    dt_bias = _pad_lanes(gdn_dt_bias)
    kvn = kv_norm.reshape(1, d)
    kvl = kv_latent_norm.reshape(1, MLA_KV_RANK)
    qln = mla_q_latent_norm

    h0 = jnp.concatenate([jnp.zeros((FRONT, d), F32), meta_f, x[0], jnp.zeros((tail, d), F32)], axis=0)
    tgt = jnp.pad(loss_target[0], ((ROW0, tail), (0, 0)))
    pos = jnp.maximum(jnp.arange(lp, dtype=jnp.int32) - FRONT, 0).astype(F32)
    inv = ROPE_THETA ** (-jnp.arange(0, MLA_ROPE, 2, dtype=F32) / MLA_ROPE)
    ang = pos[:, None] * inv[None, :]
    zeros64 = jnp.zeros((lp, LANE - MLA_ROPE), F32)
    cos_t = jnp.concatenate([jnp.cos(ang), jnp.cos(ang), zeros64], axis=1)
    sin_t = jnp.concatenate([-jnp.sin(ang), jnp.sin(ang), zeros64], axis=1)

    def valid_rows(ridx):
        return jnp.logical_and(ridx >= FRONT, ridx < ROW0 + seq)

    def f_pre0(ridx, g, h, gain):
        return _rms(h, gain), h

    (hn0,) = _rowwise("pre0", lambda *a: f_pre0(*a)[:1], [_In(h0), _In(pre0, "const")],
                      [_Out("row", (lp, d), BF16)])
    qkv_raw = _mm(hn0, w_qkv, "nn", "gdn_in_qkv")
    z0 = _mm(hn0, w_z0, "nn", "gdn_in_z")
    b_raw = _mm(hn0, w_b, "nn", "gdn_in_b")
    a_raw = _mm(hn0, w_a, "nn", "gdn_in_a")

    def f_ba(ridx, g, b, a, alog, dtb):
        tr = b.shape[0]
        ok = valid_rows(ridx).astype(F32)
        beta = jax.nn.sigmoid(b) * ok
        gate = -jnp.exp(alog) * _softplus(a + dtb) * ok
        ii = lax.broadcasted_iota(jnp.int32, (tr, tr), 0)
        jj = lax.broadcasted_iota(jnp.int32, (tr, tr), 1)
        shift = GDN_CHUNK.bit_length() - 1
        tri = jnp.logical_and((ii >> shift) == (jj >> shift), ii >= jj).astype(F32)
        return beta, _hdot(tri, gate)

    ba_ins = [_In(b_raw), _In(a_raw), _In(a_log, "const"), _In(dt_bias, "const")]
    beta, gc = _rowwise("gdn_gates", f_ba, ba_ins, [_Out("row", (lp, LANE)), _Out("row", (lp, LANE))])
    qkv = _conv_fwd(qkv_raw, conv_w, "gdn_conv")
    o0, ckpt, t_saved, mm_all = _gdn_fwd(qkv, beta, gc, mm_flat, "gdn_scan")
    (w_out0, kv_down, kv_up, w_in1, w_qup, w_out1) = _unpack_full(mm_all, layout[2:])
    w_ckv = kv_down[:, :MLA_KV_RANK]
    w_kr = _pad_lanes(kv_down[:, MLA_KV_RANK:])
    kvu = kv_up.reshape(MLA_KV_RANK, MLA_HEADS, 2 * LANE)
    w_kn = kvu[:, :, :LANE].reshape(MLA_KV_RANK, MLA_HEADS * LANE)
    w_v = kvu[:, :, LANE:].reshape(MLA_KV_RANK, MLA_HEADS * LANE)
    w_cq = w_in1[:, :MLA_Q_RANK]
    w_z1 = w_in1[:, MLA_Q_RANK:]
    qu = w_qup.reshape(MLA_Q_RANK, MLA_HEADS, MLA_QK)
    w_qn = qu[:, :, :MLA_NOPE].reshape(MLA_Q_RANK, MLA_HEADS * LANE)
    w_qr = _pad_lanes(qu[:, :, MLA_NOPE:]).reshape(MLA_Q_RANK, MLA_HEADS * LANE)

    def per_head(fn, *arrs):
        n = arrs[0].shape[1] // LANE
        return jnp.concatenate([fn(*[a[:, i * LANE:(i + 1) * LANE] for a in arrs]) for i in range(n)], axis=1)

    def f_gate0(ridx, g, o, z, gain):
        return (per_head(lambda oh, zh: _rms(oh, gain) * _silu(zh), o, z),)

    gate0_ins = [_In(o0), _In(z0), _In(gdn_out_norm, "const")]
    (gated0,) = _rowwise("gdn_gate", f_gate0, gate0_ins, [_Out("row", (lp, GDN_V_W), BF16)])
    y0 = _mm(gated0, w_out0, "nn", "gdn_out")

    def f_mid(ridx, g, h, y, g_post, g_pre, g_kv):
        h1 = h + _rms(y, g_post)
        return h1, _rms(h1, g_pre), _rms(h1, g_kv)

    mid_ins = [_In(h0), _In(y0), _In(post0, "const"), _In(pre1, "const"), _In(kvn, "const")]
    h1, hn1, hkv = _rowwise("mid", f_mid, mid_ins,
                            [_Out("row", (lp, d)), _Out("row", (lp, d), BF16), _Out("row", (lp, d), BF16)])

    ckv_raw = _mm(hkv, w_ckv, "nn", "kv_down_c")
    kr_raw = _mm(hkv, w_kr, "nn", "kv_down_r")

    def f_ckv(ridx, g, c, r, cs, sn, gain):
        return _rms(c, gain), _rope(r, cs, sn)

    ckv_ins = [_In(ckv_raw), _In(kr_raw), _In(cos_t), _In(sin_t), _In(kvl, "const")]
    ckv, kr = _rowwise("kv_latent", f_ckv, ckv_ins, [_Out("row", (lp, LANE)), _Out("row", (lp, LANE), BF16)],
                       tr=TR_FULL)
    kn = _mm(ckv, w_kn, "nn", "kv_up_k", BF16)
    vv = _mm(ckv, w_v, "nn", "kv_up_v", BF16)
    cq_raw = _mm(hn1, w_cq, "nn", "mla_in_q")
    z1 = _mm(hn1, w_z1, "nn", "mla_in_z")

    def f_cq(ridx, g, c, gain):
        return (_rms(c, gain),)

    cq_ins = [_In(cq_raw), _In(qln, "const")]
    (cq,) = _rowwise("q_latent", f_cq, cq_ins, [_Out("row", (lp, MLA_Q_RANK))], tr=TR_FULL)
    qn = _mm(cq, w_qn, "nn", "q_up_n", BF16)
    qr_raw = _mm(cq, w_qr, "nn", "q_up_r")

    def f_qrope(ridx, g, r, cs, sn):
        return (per_head(lambda rh: _rope(rh, cs, sn), r),)

    qr_ins = [_In(qr_raw), _In(cos_t), _In(sin_t)]
    (qr,) = _rowwise("q_rope", f_qrope, qr_ins, [_Out("row", (lp, MLA_HEADS * LANE), BF16)])
    o1, lse = _flash_fwd(qn, qr, kn, kr, vv, "attention")

    def f_gate1(ridx, g, o, z):
        return (o * _silu(z),)

    gate1_ins = [_In(o1), _In(z1)]
    (og,) = _rowwise("mla_gate", f_gate1, gate1_ins, [_Out("row", (lp, MLA_HEADS * LANE), BF16)])
    y1 = _mm(og, w_out1, "nn", "mla_out")

    def f_final(ridx, g, h, y, t, gain):
        ok = jnp.logical_and(ridx >= ROW0, ridx < ROW0 + seq).astype(F32)

        def rows_loss(h_, y_, gain_):
            err = (h_ + _rms(y_, gain_) - t) * ok
            return 0.5 * jnp.sum(jnp.sum(err * err, axis=1, keepdims=True), axis=0, keepdims=True) / d

        val, vjp = jax.vjp(rows_loss, h, y, gain)
        dh, dy, dgain = vjp(jnp.ones((1, 1), F32))
        return dh, dy, dgain, jnp.broadcast_to(val, (1, LANE))

    dh2, dy1, dpost1, loss_part = _rowwise(
        "loss_head", f_final, [_In(h1), _In(y1), _In(tgt), _In(post1, "const")],
        [_Out("row", (lp, d)), _Out("row", (lp, d)), _Out("acc", (1, d)), _Out("acc", (1, LANE))])

    dog = _mm(dy1, w_out1, "nt", "mla_out_dx")
    dw_out1 = _mm(og, dy1, "tn", "mla_out_dw")
    do1, dz1 = _rowwise_vjp("mla_gate_bwd", f_gate1, gate1_ins, [[dog]], [0, 1])
    dqn, dqr, dkn, dkr, dvv = _flash_bwd(qn, qr, kn, kr, vv, o1, do1, lse, "attention_bwd")
    (dqr_raw,) = _rowwise_vjp("q_rope_bwd", f_qrope, qr_ins, [[dqr]], [0])
    dcq_a = _mm(dqn, w_qn, "nt", "q_up_n_dx")
    dcq_b = _mm(dqr_raw, w_qr, "nt", "q_up_r_dx")
    dw_qn = _mm(cq, dqn, "tn", "q_up_n_dw") * Q_PRESCALE
    dw_qr = _mm(cq, dqr_raw, "tn", "q_up_r_dw") * Q_PRESCALE
    dcq_raw, dqln = _rowwise_vjp("q_latent_bwd", f_cq, cq_ins, [[dcq_a, dcq_b]], [0, 1], tr=TR_FULL)
    dhn1_a = _mm(dcq_raw, w_cq, "nt", "mla_in_q_dx")
    dhn1_b = _mm(dz1, w_z1, "nt", "mla_in_z_dx")
    dw_cq = _mm(hn1, dcq_raw, "tn", "mla_in_q_dw")
    dw_z1 = _mm(hn1, dz1, "tn", "mla_in_z_dw")
    dckv_a = _mm(dkn, w_kn, "nt", "kv_up_k_dx")
    dckv_b = _mm(dvv, w_v, "nt", "kv_up_v_dx")
    dw_kn = _mm(ckv, dkn, "tn", "kv_up_k_dw")
    dw_v = _mm(ckv, dvv, "tn", "kv_up_v_dw")
    dckv_raw, dkr_raw, dkvl = _rowwise_vjp("kv_latent_bwd", f_ckv, ckv_ins, [[dckv_a, dckv_b], [dkr]], [0, 1, 4],
                                           tr=TR_FULL)
    dhkv_a = _mm(dckv_raw, w_ckv, "nt", "kv_down_c_dx")
    dhkv_b = _mm(dkr_raw, w_kr, "nt", "kv_down_r_dx")
    dw_ckv = _mm(hkv, dckv_raw, "tn", "kv_down_c_dw")
    dw_kr = _mm(hkv, dkr_raw, "tn", "kv_down_r_dw")
    dh0_res, dy0, dpost0, dpre1, dkvn = _rowwise_vjp(
        "mid_bwd", f_mid, mid_ins, [[dh2], [dhn1_a, dhn1_b], [dhkv_a, dhkv_b]], [0, 1, 2, 3, 4])

    dgated0 = _mm(dy0, w_out0, "nt", "gdn_out_dx")
    dw_out0 = _mm(gated0, dy0, "tn", "gdn_out_dw")
    do0, dz0, doutn = _rowwise_vjp("gdn_gate_bwd", f_gate0, gate0_ins, [[dgated0]], [0, 1, 2], tr=TR_QUARTER)

    g_kv_down = jnp.concatenate([dw_ckv, dw_kr[:, :MLA_ROPE]], axis=1)
    g_kv_up = jnp.concatenate([dw_kn.reshape(MLA_KV_RANK, MLA_HEADS, LANE), dw_v.reshape(MLA_KV_RANK, MLA_HEADS, LANE)],
                              axis=2).reshape(MLA_KV_RANK, MLA_HEADS * 2 * LANE)
    g_w_in1 = jnp.concatenate([dw_cq, dw_z1], axis=1)
    g_qup = jnp.concatenate([dw_qn.reshape(MLA_Q_RANK, MLA_HEADS, LANE),
                             dw_qr.reshape(MLA_Q_RANK, MLA_HEADS, LANE)[:, :, :MLA_ROPE]],
                            axis=2).reshape(MLA_Q_RANK, MLA_HEADS * MLA_QK)
    g_mm = _pack_full([dw_out0, g_kv_down, g_kv_up, g_w_in1, g_qup, dw_out1], layout[2:])
    mm_chip_part = _add_pair(g_mm, _sibling_split(g_mm, "grads_sibling_split"), "grads_chip_sum")
    dq0, dk0, dv0, dbeta, dgc, mm_from_chips = _gdn_bwd(qkv, beta, gc, ckpt, t_saved, do0, mm_chip_part,
                                                        "gdn_scan_bwd")
    g_flat = _sibling_join(_sum_slots(mm_from_chips, "grads_total"), "grads_sibling_join")
    db_raw, da_raw, dalog, ddtb = _rowwise_vjp("gdn_gates_bwd", f_ba, ba_ins, [[dbeta], [dgc]], [0, 1, 2, 3])
    dqkv_raw, dconv = _conv_bwd(qkv_raw, conv_w, dq0, dk0, dv0, "gdn_conv_bwd")
    dhn0_a = _mm(dqkv_raw, w_qkv, "nt", "gdn_in_qkv_dx")
    dhn0_b = _mm(dz0, w_z0, "nt", "gdn_in_z_dx")
    dhn0_c = _mm(db_raw, w_b, "nt", "gdn_in_b_dx")
    dhn0_d = _mm(da_raw, w_a, "nt", "gdn_in_a_dx")
    dw_qkv = _mm(hn0, dqkv_raw, "tn", "gdn_in_qkv_dw")
    dw_z0 = _mm(hn0, dz0, "tn", "gdn_in_z_dw")
    dw_b = _mm(hn0, db_raw, "tn", "gdn_in_b_dw")
    dw_a = _mm(hn0, da_raw, "tn", "gdn_in_a_dw")
    dh0, dpre0 = _rowwise_vjp("pre0_bwd", f_pre0, [_In(h0), _In(pre0, "const")],
                              [[dhn0_a, dhn0_b, dhn0_c, dhn0_d], [dh0_res]], [0, 1])

    grad_x = dh0[ROW0:ROW0 + seq][None]
    g_meta = dh0[FRONT:ROW0]
    g_w_in0 = jnp.concatenate([dw_qkv, dw_z0, dw_b[:, :nv], dw_a[:, :nv]], axis=1)

    g_win_by_chip = jnp.concatenate([g_w_in0[None, :, s * win_cols:(s + 1) * win_cols] for s in range(4)], axis=0)
    win_chip_part = _add_pair(g_win_by_chip, _sibling_split(g_win_by_chip, "grads_sibling_split_gdn_w_in"),
                              "grads_chip_sum_gdn_w_in")
    win_from_chips = _chip_scatter(win_chip_part, "grads_chip_scatter_gdn_w_in")
    g_win = _sibling_join(_sum_slots(win_from_chips, "grads_total_gdn_w_in"), "grads_sibling_join_gdn_w_in")

    small_shapes = [(2, d), (2, d), (1, nv), (1, nv), (1, GDN_DK), (d,), (MLA_KV_RANK,), (1, MLA_Q_RANK),
                    g_meta.shape, dconv.shape, (1, LANE)]
    small_part = _pack_small([jnp.concatenate([dpre0, dpre1], axis=0), jnp.concatenate([dpost0, dpost1], axis=0),
                              dalog[:, :nv], ddtb[:, :nv], doutn, dkvn, dkvl, dqln, g_meta, dconv, loss_part])
    small_tot = _all_sum_small(small_part, "small_sum")
    small_g = _unpack_small(small_tot, small_shapes)
    loss = small_g[-1][0, 0]
    chip = 2 * lax.axis_index("x") + lax.axis_index("y")
    meta_cols, conv_cols = meta_tokens.shape[1], gdn_conv_w.shape[2]
    g_meta_shard = lax.dynamic_slice(small_g[8], (0, chip * meta_cols), (small_g[8].shape[0], meta_cols))
    g_conv_shard = lax.dynamic_slice(small_g[9], (0, chip * conv_cols), (small_g[9].shape[0], conv_cols))

    d_flat, m_flat, v_flat = _adamw(_pack_shards(big_w[2:]), g_flat, _pack_shards(big_m[2:]), _pack_shards(big_v[2:]),
                                    "adamw_sharded")
    win_step = _adamw(gdn_w_in[0], g_win, m_gdn_w_in[0], v_gdn_w_in[0], "adamw_gdn_w_in")
    small_names = ["pre_norm", "post_norm", "gdn_a_log", "gdn_dt_bias", "gdn_out_norm", "kv_norm", "kv_latent_norm",
                   "mla_q_latent_norm", "meta_tokens", "gdn_conv_w"]
    small_w = [pre_norm, post_norm, gdn_a_log, gdn_dt_bias, gdn_out_norm, kv_norm, kv_latent_norm, mla_q_latent_norm,
               meta_tokens, gdn_conv_w]
    small_m = [m_pre_norm, m_post_norm, m_gdn_a_log, m_gdn_dt_bias, m_gdn_out_norm, m_kv_norm, m_kv_latent_norm,
               m_mla_q_latent_norm, m_meta_tokens, m_gdn_conv_w]
    small_v = [v_pre_norm, v_post_norm, v_gdn_a_log, v_gdn_dt_bias, v_gdn_out_norm, v_kv_norm, v_kv_latent_norm,
               v_mla_q_latent_norm, v_meta_tokens, v_gdn_conv_w]
    g_small_flat = _pack_small(small_g[:8] + [g_meta_shard, g_conv_shard])
    ds_flat, ms_flat, vs_flat = _adamw(_pack_small(small_w), g_small_flat, _pack_small(small_m), _pack_small(small_v),
                                       "adamw_replicated")

    def assemble(big_flat, small_flat, win):
        bigs = dict(zip(big_names[2:], [a.reshape(w.shape) for a, w in zip(
            _unpack_shards(big_flat, layout[2:]),
            [gdn_w_out, kv_w_down, kv_w_up, mla_w_in, mla_w_q_up, mla_w_out])]))
        smalls = dict(zip(small_names, _unpack_small(small_flat, [w.shape for w in small_w])))
        both = {**bigs, **smalls, "gdn_w_in": win[None]}
        order = ["meta_tokens", "pre_norm", "post_norm", "gdn_w_in", "gdn_conv_w", "gdn_a_log", "gdn_dt_bias",
                 "gdn_out_norm", "gdn_w_out", "kv_norm", "kv_w_down", "kv_latent_norm", "kv_w_up", "mla_w_in",
                 "mla_q_latent_norm", "mla_w_q_up", "mla_w_out"]
        return [both[n] for n in order]

    grads = assemble(g_flat, g_small_flat, g_win)
    deltas = assemble(d_flat, ds_flat, win_step[0])
    new_m = assemble(m_flat, ms_flat, win_step[1])
    new_v = assemble(v_flat, vs_flat, win_step[2])
    return (loss, grad_x, *grads, *deltas, *new_m, *new_v)
```

```python
import functools
import math

import jax
import jax.numpy as jnp
from jax import lax
from jax.experimental import pallas as pl
from jax.experimental.pallas import tpu as pltpu

F32 = jnp.float32
BF16 = jnp.bfloat16
MESH = pl.DeviceIdType.MESH

D_MODEL = 1024
N_META = 16
FRONT = 48
ROW0 = FRONT + N_META
ROW_ALIGN = 768
TR_FULL, TR_HALF, TR_QUARTER = ROW_ALIGN, ROW_ALIGN // 2, ROW_ALIGN // 4
NORM_EPS = 1e-6
LANE = 128

GDN_QK_HEADS = 8
GDN_V_HEADS = 16
GDN_DK = 128
GDN_CHUNK = 64
GDN_QK_W = 1024
GDN_V_W = 2048
GDN_CONV_W = 4096

MLA_HEADS = 16
MLA_NOPE = 128
MLA_ROPE = 64
MLA_QK = 192
MLA_Q_RANK = 256
MLA_KV_RANK = 128
ROPE_THETA = 10000.0

ADAM_LR = 0.001
ADAM_B1 = 0.9
ADAM_B2 = 0.999
ADAM_EPS = 1e-08
ADAM_WD = 0.01
ADAM_STEP = 10

VMEM_LIMIT_V7X = 56 * 1024 * 1024
NEG = -1e30

_NN = ((1,), (0,))
_NT = ((1,), (1,))
_TN = ((0,), (0,))
_HI = lax.Precision.HIGHEST
_X3 = lax.Precision.HIGH


def _pcall(body, **kw):
    return pl.pallas_call(body, **kw)


def _params(n_axes):
    return pltpu.CompilerParams(dimension_semantics=("arbitrary",) * n_axes, vmem_limit_bytes=VMEM_LIMIT_V7X)


def _dot(a, b, dims, prec=None):
    return lax.dot_general(a, b, (dims, ((), ())), precision=prec, preferred_element_type=F32)


def _bdot(a, b, dims):
    return _dot(a.astype(BF16), b.astype(BF16), dims)


def _hdot(a, b, dims=_NN):
    return _dot(a, b, dims, _HI)


def _fdot(a, b, dims):
    return _dot(a, b, dims)


SMALL_MATMUL_DIM = 256
SMALL_MATMUL_ROWS = 1408


def _tile(n):
    if n % ROW_ALIGN == 0:
        return SMALL_MATMUL_ROWS if n % SMALL_MATMUL_ROWS == 0 else ROW_ALIGN
    for t in (1024, 512, 256, 128):
        if n % t == 0:
            return t
    raise ValueError(n)


def _mm(a, b, mode, name, out_dtype=F32):
    if mode == "nn":
        (m, k), (k2, n) = a.shape, b.shape
    elif mode == "nt":
        (m, k), (n, k2) = a.shape, b.shape
    else:
        (k, m), (k2, n) = a.shape, b.shape
    assert k == k2, (a.shape, b.shape, mode)
    tm, tn, tk = _tile(m), _tile(n), _tile(k)
    if mode != "tn" and min(k, n) <= SMALL_MATMUL_DIM and m % SMALL_MATMUL_ROWS == 0:
        tm = SMALL_MATMUL_ROWS
    nk = k // tk
    dims = {"nn": _NN, "nt": _NT, "tn": _TN}[mode]

    def body(a_ref, b_ref, o_ref, acc):
        kk = pl.program_id(2)

        @pl.when(kk == 0)
        def _():
            acc[...] = jnp.zeros_like(acc)

        acc[...] += _bdot(a_ref[...], b_ref[...], dims)

        @pl.when(kk == nk - 1)
        def _():
            o_ref[...] = acc[...].astype(out_dtype)

    if mode == "tn":
        a_spec = pl.BlockSpec((tk, tm), lambda i, j, kk: (kk, i))
    else:
        a_spec = pl.BlockSpec((tm, tk), lambda i, j, kk: (i, kk))
    if mode == "nt":
        b_spec = pl.BlockSpec((tn, tk), lambda i, j, kk: (j, kk))
    else:
        b_spec = pl.BlockSpec((tk, tn), lambda i, j, kk: (kk, j))
    return _pcall(
        body, name=name, grid=(m // tm, n // tn, nk),
        in_specs=[a_spec, b_spec],
        out_specs=pl.BlockSpec((tm, tn), lambda i, j, kk: (i, j)),
        out_shape=jax.ShapeDtypeStruct((m, n), out_dtype),
        scratch_shapes=[pltpu.VMEM((tm, tn), F32)],
        compiler_params=_params(3),
    )(a, b)


class _In:
    def __init__(self, arr, kind="row", grouped=False, goff=0):
        self.arr, self.kind, self.grouped, self.goff = arr, kind, grouped, goff


class _Out:
    def __init__(self, kind, shape, dtype=F32, grouped=False):
        self.kind, self.shape, self.dtype, self.grouped = kind, shape, dtype, grouped


def _rowwise(name, fn, ins, outs, *, groups=1, tr=TR_HALF):
    lp = next(i.arr.shape[0] for i in ins if i.kind == "row")
    nr = lp // tr
    assert lp % tr == 0

    def in_spec(i):
        w = i.arr.shape[1]
        if i.kind == "row":
            if i.grouped:
                return pl.BlockSpec((tr, LANE), lambda g, r, o=i.goff: (r, g + o))
            return pl.BlockSpec((tr, w), lambda g, r: (r, 0))
        if i.grouped:
            return pl.BlockSpec((i.arr.shape[0], LANE), lambda g, r, o=i.goff: (0, g + o))
        return pl.BlockSpec(i.arr.shape, lambda g, r: (0, 0))

    def out_spec(o):
        if o.kind == "row":
            if o.grouped:
                return pl.BlockSpec((tr, LANE), lambda g, r: (r, g))
            assert groups == 1
            return pl.BlockSpec((tr, o.shape[1]), lambda g, r: (r, 0))
        if o.grouped:
            return pl.BlockSpec((o.shape[0], LANE), lambda g, r: (0, g))
        return pl.BlockSpec(o.shape, lambda g, r: (0, 0))

    n_in = len(ins)

    def body(*refs):
        g = pl.program_id(0)
        r = pl.program_id(1)
        ridx = r * tr + lax.broadcasted_iota(jnp.int32, (tr, 1), 0)
        res = fn(ridx, g, *[ref[...] for ref in refs[:n_in]])
        assert len(res) == len(outs), (name, len(res), len(outs))
        for o, ref, val in zip(outs, refs[n_in:], res):
            if o.kind == "row":
                ref[...] = val.astype(o.dtype)
            else:
                first = (r == 0) if o.grouped else jnp.logical_and(r == 0, g == 0)

                @pl.when(first)
                def _(ref=ref, val=val):
                    ref[...] = val.astype(F32)

                @pl.when(jnp.logical_not(first))
                def _(ref=ref, val=val):
                    ref[...] += val.astype(F32)

    res = _pcall(
        body, name=name, grid=(groups, nr),
        in_specs=[in_spec(i) for i in ins],
        out_specs=[out_spec(o) for o in outs],
        out_shape=[jax.ShapeDtypeStruct(o.shape, o.dtype) for o in outs],
        compiler_params=_params(2),
    )(*[i.arr for i in ins])
    return res


def _rowwise_vjp(name, fn, ins, cots, diff, *, groups=1, tr=TR_HALF):
    n_in = len(ins)
    grouped = groups > 1
    cot_ins = []
    counts = []
    for arrs in cots:
        counts.append(len(arrs))
        for a in arrs:
            cot_ins.append(_In(a, "row", grouped=grouped and a.shape[1] > LANE))
    lp = next(i.arr.shape[0] for i in ins if i.kind == "row")
    outs = []
    for d in diff:
        i = ins[d]
        if i.kind == "row":
            w = groups * LANE if i.grouped else i.arr.shape[1]
            outs.append(_Out("row", (lp, w), F32, grouped=i.grouped))
        else:
            outs.append(_Out("acc", i.arr.shape, F32, grouped=i.grouped))

    def bfn(ridx, g, *allvals):
        vals = list(allvals[:n_in])
        cvals = allvals[n_in:]

        def f(*dv):
            full = list(vals)
            for i, v in zip(diff, dv):
                full[i] = v
            return tuple(fn(ridx, g, *full))

        primal, vjp = jax.vjp(f, *[vals[i].astype(F32) for i in diff])
        cts = []
        pos = 0
        for k, cnt in enumerate(counts):
            if cnt == 0:
                cts.append(jnp.zeros_like(primal[k]))
            else:
                c = cvals[pos].astype(F32)
                for extra in cvals[pos + 1:pos + cnt]:
                    c = c + extra.astype(F32)
                w = primal[k].shape[1]
                if c.shape[1] != w:
                    c = functools.reduce(jnp.add, [c[:, i * w:(i + 1) * w] for i in range(c.shape[1] // w)])
                cts.append(c.astype(primal[k].dtype))
            pos += cnt
        return vjp(tuple(cts))

    return _rowwise(name, bfn, list(ins) + cot_ins, outs, groups=groups, tr=tr)


def _rms(x, g):
    return x * lax.rsqrt(jnp.mean(x * x, axis=-1, keepdims=True) + NORM_EPS) * g


def _silu(x):
    return x * jax.nn.sigmoid(x)


def _softplus(x):
    return jnp.maximum(x, 0.0) + jnp.log(1.0 + jnp.exp(-jnp.abs(x)))


def _swap_halves(x):
    lane = lax.broadcasted_iota(jnp.int32, x.shape, x.ndim - 1)
    return jnp.where(lane < 32, pltpu.roll(x, LANE - 32, x.ndim - 1), pltpu.roll(x, 32, x.ndim - 1))


@jax.custom_vjp
def _rope(x, c, s):
    return x * c + _swap_halves(x) * s


def _rope_fwd(x, c, s):
    return _rope(x, c, s), (c, s)


def _rope_bwd(res, dy):
    c, s = res
    return dy * c + _swap_halves(dy * s), jnp.zeros_like(c), jnp.zeros_like(s)


_rope.defvjp(_rope_fwd, _rope_bwd)


def _conv_post(c, g):
    s = _silu(c)
    n = s * lax.rsqrt(jnp.sum(s * s, axis=-1, keepdims=True) + NORM_EPS)
    return jnp.where(g < GDN_QK_HEADS, n * (GDN_DK ** -0.5), jnp.where(g < 2 * GDN_QK_HEADS, n, s))


def _conv_taps(xe, w):
    c = xe[8:] * w[3]
    for s in (1, 2, 3):
        c = c + pltpu.roll(xe, s, 0)[8:] * w[3 - s]
    return c


CONV_LANES = 512
CONV_HEADS = CONV_LANES // LANE


def _conv_post_block(c, g):
    return jnp.concatenate([_conv_post(c[:, i * LANE:(i + 1) * LANE], g * CONV_HEADS + i)
                            for i in range(CONV_HEADS)], axis=1)


def _conv_fwd(x, w, name, tr=TR_FULL):
    lp, width = x.shape
    cl = CONV_LANES
    nr = lp // tr

    def body(x_ref, prev_ref, w_ref, o_ref):
        g = pl.program_id(0)
        r = pl.program_id(1)
        prev = jnp.where(r > 0, prev_ref[...], 0.0)
        xe = jnp.concatenate([prev, x_ref[...]], axis=0)
        o_ref[...] = _conv_post_block(_conv_taps(xe, [w_ref[t:t + 1, :] for t in range(4)]), g)

    return _pcall(
        body, name=name, grid=(width // cl, nr),
        in_specs=[pl.BlockSpec((tr, cl), lambda g, r: (r, g)),
                  pl.BlockSpec((8, cl), lambda g, r: (jnp.maximum(r * (tr // 8) - 1, 0), g)),
                  pl.BlockSpec((4, cl), lambda g, r: (0, g))],
        out_specs=pl.BlockSpec((tr, cl), lambda g, r: (r, g)),
        out_shape=jax.ShapeDtypeStruct((lp, width), F32),
        compiler_params=_params(2),
    )(x, x, w)


def _conv_bwd(x, w, dq, dk, dv, name, tr=TR_FULL):
    lp, width = x.shape
    cl = CONV_LANES
    nr = lp // tr
    last8 = lp // 8 - 1
    nq = GDN_QK_W // cl

    def body(x_ref, prev_ref, next_ref, w_ref, q_ref, k_ref, v_ref, q_n, k_n, v_n, dx_ref, dw_ref):
        g = pl.program_id(0)
        r = pl.program_id(1)
        w = [w_ref[t:t + 1, :] for t in range(4)]
        not_last = r < nr - 1

        def pick(a, b, c):
            return jnp.where(g < nq, a[...], jnp.where(g < 2 * nq, b[...], c[...]))

        dy = pick(q_ref, k_ref, v_ref)
        dyn = jnp.where(not_last, pick(q_n, k_n, v_n), 0.0)
        prev = jnp.where(r > 0, prev_ref[...], 0.0)
        nxt = jnp.where(not_last, next_ref[...], 0.0)
        xe = jnp.concatenate([prev, x_ref[...], nxt], axis=0)
        ce = _conv_taps(xe, w)
        _, vjp = jax.vjp(lambda c: _conv_post_block(c, g), ce)
        (dce,) = vjp(jnp.concatenate([dy, dyn], axis=0))
        n = tr + 8
        dx = dce * w[3]
        for s in (1, 2, 3):
            dx = dx + pltpu.roll(dce, n - s, 0) * w[3 - s]
        dx_ref[...] = dx[:tr]
        dc = dce[:tr]
        row4 = lax.broadcasted_iota(jnp.int32, (4, cl), 0)
        dw = jnp.zeros((4, cl), F32)
        for s in (0, 1, 2, 3):
            xs = xe[8:8 + tr] if s == 0 else pltpu.roll(xe, s, 0)[8:8 + tr]
            dw = dw + jnp.where(row4 == 3 - s, jnp.sum(dc * xs, axis=0, keepdims=True), 0.0)

        @pl.when(r == 0)
        def _():
            dw_ref[...] = dw

        @pl.when(r > 0)
        def _():
            dw_ref[...] += dw

    def col_q(g):
        return jnp.minimum(g, nq - 1), g < nq

    def col_k(g):
        return jnp.clip(g - nq, 0, nq - 1), jnp.logical_and(g >= nq, g < 2 * nq)

    def col_v(g):
        return jnp.maximum(g - 2 * nq, 0), g >= 2 * nq

    def blk(colf):
        def index(g, r):
            col, used = colf(g)
            return jnp.where(used, r, 0), col
        return pl.BlockSpec((tr, cl), index)

    def nblk(colf):
        def index(g, r):
            col, used = colf(g)
            return jnp.where(used, jnp.minimum((r + 1) * (tr // 8), last8), 0), col
        return pl.BlockSpec((8, cl), index)

    return _pcall(
        body, name=name, grid=(width // cl, nr),
        in_specs=[pl.BlockSpec((tr, cl), lambda g, r: (r, g)),
                  pl.BlockSpec((8, cl), lambda g, r: (jnp.maximum(r * (tr // 8) - 1, 0), g)),
                  pl.BlockSpec((8, cl), lambda g, r: (jnp.minimum((r + 1) * (tr // 8), last8), g)),
                  pl.BlockSpec((4, cl), lambda g, r: (0, g)),
                  blk(col_q), blk(col_k), blk(col_v), nblk(col_q), nblk(col_k), nblk(col_v)],
        out_specs=[pl.BlockSpec((tr, cl), lambda g, r: (r, g)),
                   pl.BlockSpec((4, cl), lambda g, r: (0, g))],
        out_shape=[jax.ShapeDtypeStruct((lp, width), F32), jax.ShapeDtypeStruct((4, width), F32)],
        compiler_params=_params(2),
    )(x, x, x, w, dq, dk, dv, dq, dk, dv)


def _bmm(a, b, dims, prec=None):
    (ca,), (cb,) = dims
    return lax.dot_general(a, b, (((ca + 1,), (cb + 1,)), ((0,), (0,))), precision=prec,
                           preferred_element_type=F32)


def _inv_impl(m):
    c = m.shape[-1]
    ii = lax.broadcasted_iota(jnp.int32, (c, c), 0)
    jj = lax.broadcasted_iota(jnp.int32, (c, c), 1)
    eye = (ii == jj).astype(F32)

    def same_block(shift):
        return (ii >> shift) == (jj >> shift)

    n1 = jnp.where(same_block(3), -m, 0.0)
    n2 = _bmm(n1, n1, _NN, _X3)
    n4 = _bmm(n2, n2, _NN, _X3)
    d = _bmm(_bmm(eye + n1, eye + n2, _NN, _X3), eye + n4, _NN, _X3)
    shift = 3
    while (1 << shift) < c:
        low = jnp.where(jnp.logical_and(same_block(shift + 1), jnp.logical_not(same_block(shift))), m, 0.0)
        d = d - _bmm(d, _bmm(low, d, _NN, _X3), _NN, _X3)
        shift += 1
    return d


@jax.custom_vjp
def _inv_unit_lower(m):
    return _inv_impl(m)


def _inv_f(m):
    t = _inv_impl(m)
    return t, t


def _inv_b(t, dt):
    c = t.shape[-1]
    ii = lax.broadcasted_iota(jnp.int32, (c, c), 0)
    jj = lax.broadcasted_iota(jnp.int32, (c, c), 1)
    gm = _bmm(t, _bmm(dt, t, _NT, _X3), _TN, _X3)
    return (jnp.where(ii > jj, -gm, 0.0),)


_inv_unit_lower.defvjp(_inv_f, _inv_b)


@jax.custom_vjp
def _inv_known(m, t):
    return t


def _inv_known_f(m, t):
    return t, t


def _inv_known_b(t, dt):
    return _inv_b(t, dt) + (jnp.zeros_like(t),)


_inv_known.defvjp(_inv_known_f, _inv_known_b)


GDN_HEADS_PER_STEP = 16


def _gdn_group(q, k, v, beta_blk, gc_blk, states, h0, t_known=None):
    hp = GDN_HEADS_PER_STEP
    c = q.shape[0]
    lane = lax.broadcasted_iota(jnp.int32, (1, LANE), 1)
    row8 = lax.broadcasted_iota(jnp.int32, (max(8, hp), LANE), 0)
    lane8 = lax.broadcasted_iota(jnp.int32, (max(8, hp), LANE), 1)
    gcr_all = _hdot((lane8 == h0 + row8).astype(F32), gc_blk, _NT)
    betas, gccs = [], []
    for i in range(hp):
        onehot = (lane == h0 + i).astype(F32)
        betas.append(jnp.sum(beta_blk * onehot, axis=1, keepdims=True))
        gccs.append(jnp.sum(gc_blk * onehot, axis=1, keepdims=True))
    def stack(xs):
        return jnp.concatenate([x[None] for x in xs], axis=0)

    beta = stack(betas)
    gcc = stack(gccs)
    gcr = stack([gcr_all[i:i + 1] for i in range(hp)])
    qh = stack([q[:, (i // 2) * LANE:(i // 2 + 1) * LANE] for i in range(hp)])
    kh = stack([k[:, (i // 2) * LANE:(i // 2 + 1) * LANE] for i in range(hp)])
    vh = stack([v[:, i * LANE:(i + 1) * LANE] for i in range(hp)])
    state = stack(states)
    ii = lax.broadcasted_iota(jnp.int32, (c, c), 0)
    jj = lax.broadcasted_iota(jnp.int32, (c, c), 1)
    incl = ii >= jj
    dec = jnp.where(incl, jnp.exp(jnp.where(incl, gcc - gcr, 0.0)), 0.0)
    eg = jnp.exp(gcc)
    m = _bmm(kh, kh, _NT) * beta * jnp.where(ii > jj, dec, 0.0)
    t = _inv_unit_lower(m) if t_known is None else _inv_known(m, t_known)
    u = _bmm(t, vh * beta, _NN, _X3)
    w = _bmm(t, kh * (beta * eg), _NN, _X3)
    attn = _bmm(qh, kh, _NT) * dec
    rows = lax.broadcasted_iota(jnp.int32, (c, 1), 0)
    gl = jnp.sum(jnp.where(rows == c - 1, gcc, 0.0), axis=1, keepdims=True)
    v_new = u - _bmm(w, state, _NN)
    o = _bmm(qh * eg, state, _NN) + _bmm(attn, v_new, _NN)
    new_state = state * jnp.exp(gl) + _bmm(kh * jnp.exp(gl - gcc), v_new, _TN)
    return jnp.concatenate([o[i] for i in range(hp)], axis=1), tuple(new_state[i] for i in range(hp)), t


def _gdn_specs(nc, rev):
    def cidx(n):
        return (nc - 1 - n) if rev else n
    hp = GDN_HEADS_PER_STEP
    nqk = GDN_QK_HEADS
    c = GDN_CHUNK
    nq = 2 * nqk // hp
    q_spec = pl.BlockSpec((c, hp // 2 * LANE), lambda n, g: (cidx(n), g))
    k_spec = pl.BlockSpec((c, hp // 2 * LANE), lambda n, g: (cidx(n), nq + g))
    v_spec = pl.BlockSpec((c, hp * LANE), lambda n, g: (cidx(n), nq + g))
    s_spec = pl.BlockSpec((c, LANE), lambda n, g: (cidx(n), 0))
    o_spec = pl.BlockSpec((c, hp * LANE), lambda n, g: (cidx(n), g))
    ck_spec = pl.BlockSpec((hp, 1, GDN_DK, LANE), lambda n, g: (g, cidx(n), 0, 0))
    return q_spec, k_spec, v_spec, s_spec, o_spec, ck_spec


def _gdn_t_spec(nc, rev):
    return pl.BlockSpec((GDN_HEADS_PER_STEP, 1, GDN_CHUNK, GDN_CHUNK),
                        lambda n, g: (g, (nc - 1 - n) if rev else n, 0, 0))


def _gdn_fwd(qkv, beta, gc, shard, name):
    lp = qkv.shape[0]
    nc = lp // GDN_CHUNK
    nh = GDN_V_HEADS
    hp = GDN_HEADS_PER_STEP
    ng = nh // hp
    q_spec, k_spec, v_spec, s_spec, o_spec, ck_spec = _gdn_specs(nc, False)

    def body(q_ref, k_ref, v_ref, b_ref, g_ref, x_ref, o_ref, ck_ref, t_ref, all_ref,
             state, send_sems, recv_sems, local_sem):
        n = pl.program_id(0)
        g = pl.program_id(1)
        x, y, c = _my_place()

        def local_copy():
            return pltpu.make_async_copy(x_ref, all_ref.at[2 * x + y], local_sem)

        def remote_copy(k, px, py, slot):
            return pltpu.make_async_remote_copy(
                src_ref=x_ref, dst_ref=all_ref.at[slot], send_sem=send_sems.at[k], recv_sem=recv_sems.at[k],
                device_id=(px, py, c), device_id_type=MESH)

        @pl.when(jnp.logical_and(n == 0, g == 0))
        def _():
            local_copy().start()
            for k, (px, py) in enumerate(_other_chips(x, y)):
                remote_copy(k, px, py, 2 * x + y).start()

        @pl.when(n == 0)
        def _():
            for i in range(hp):
                state[g * hp + i] = jnp.zeros((GDN_DK, LANE), F32)

        states = tuple(state[g * hp + i] for i in range(hp))
        for i in range(hp):
            ck_ref[i, 0] = states[i]
        o, new_states, t = _gdn_group(q_ref[...], k_ref[...], v_ref[...], b_ref[...], g_ref[...], states, g * hp)
        o_ref[...] = o
        t_ref[:, 0] = t
        for i in range(hp):
            state[g * hp + i] = new_states[i]

        @pl.when(jnp.logical_and(n == nc - 1, g == ng - 1))
        def _():
            for k, (px, py) in enumerate(_other_chips(x, y)):
                remote_copy(k, px, py, 2 * px + py).wait_recv()
            for k, (px, py) in enumerate(_other_chips(x, y)):
                remote_copy(k, px, py, 2 * x + y).wait_send()
            local_copy().wait()

    return _pcall(
        body, name=name, grid=(nc, ng),
        in_specs=[q_spec, k_spec, v_spec, s_spec, s_spec, _ANY],
        out_specs=[o_spec, ck_spec, _gdn_t_spec(nc, False), _ANY],
        out_shape=[jax.ShapeDtypeStruct((lp, GDN_V_W), F32),
                   jax.ShapeDtypeStruct((nh, nc, GDN_DK, LANE), F32),
                   jax.ShapeDtypeStruct((nh, nc, GDN_CHUNK, GDN_CHUNK), F32),
                   jax.ShapeDtypeStruct((4,) + shard.shape, shard.dtype)],
        scratch_shapes=[pltpu.VMEM((nh, GDN_DK, LANE), F32), pltpu.SemaphoreType.DMA((3,)),
                        pltpu.SemaphoreType.DMA((3,)), pltpu.SemaphoreType.DMA],
        compiler_params=_params(2),
    )(qkv, qkv, qkv, beta, gc, shard)


def _gdn_bwd(qkv, beta, gc, ckpt, t_saved, do, parts, name):
    lp = qkv.shape[0]
    nc = lp // GDN_CHUNK
    nh = GDN_V_HEADS
    hp = GDN_HEADS_PER_STEP
    ng = nh // hp
    q_spec, k_spec, v_spec, s_spec, o_spec, ck_spec = _gdn_specs(nc, True)

    def body(q_ref, k_ref, v_ref, b_ref, g_ref, ck_ref, t_ref, do_ref, p_ref,
             dq_ref, dk_ref, dv_ref, db_ref, dg_ref, from_ref, dstate, send_sems, recv_sems, local_sem):
        n = pl.program_id(0)
        g = pl.program_id(1)
        x, y, c = _my_place()
        me = 2 * x + y

        def local_copy():
            return pltpu.make_async_copy(p_ref.at[me], from_ref.at[me], local_sem)

        def remote_copy(k, px, py, src_slot, dst_slot):
            return pltpu.make_async_remote_copy(
                src_ref=p_ref.at[src_slot], dst_ref=from_ref.at[dst_slot], send_sem=send_sems.at[k],
                recv_sem=recv_sems.at[k], device_id=(px, py, c), device_id_type=MESH)

        @pl.when(jnp.logical_and(n == 0, g == 0))
        def _():
            local_copy().start()
            for k, (px, py) in enumerate(_other_chips(x, y)):
                remote_copy(k, px, py, 2 * px + py, me).start()

        @pl.when(jnp.logical_and(n == nc - 1, g == ng - 1))
        def _():
            for k, (px, py) in enumerate(_other_chips(x, y)):
                remote_copy(k, px, py, me, 2 * px + py).wait_recv()
            for k, (px, py) in enumerate(_other_chips(x, y)):
                remote_copy(k, px, py, 2 * px + py, me).wait_send()
            local_copy().wait()

        @pl.when(n == 0)
        def _():
            for i in range(hp):
                dstate[g * hp + i] = jnp.zeros((GDN_DK, LANE), F32)

        states = tuple(ck_ref[i, 0] for i in range(hp))
        t_known = t_ref[:, 0]
        _, vjp = jax.vjp(lambda q, k, v, b, gg, s: _gdn_group(q, k, v, b, gg, s, g * hp, t_known)[:2],
                         q_ref[...], k_ref[...], v_ref[...], b_ref[...], g_ref[...], states)
        dq, dk, dv, db, dg, ds = vjp((do_ref[...], tuple(dstate[g * hp + i] for i in range(hp))))
        dq_ref[...] = dq
        dk_ref[...] = dk
        dv_ref[...] = dv
        for i in range(hp):
            dstate[g * hp + i] = ds[i]

        @pl.when(g == 0)
        def _():
            db_ref[...] = db
            dg_ref[...] = dg

        @pl.when(g > 0)
        def _():
            db_ref[...] += db
            dg_ref[...] += dg

    qk_shape = jax.ShapeDtypeStruct((lp, GDN_QK_W), F32)
    big = jax.ShapeDtypeStruct((lp, GDN_V_W), F32)
    small = jax.ShapeDtypeStruct((lp, LANE), F32)
    dq_spec = pl.BlockSpec((GDN_CHUNK, hp // 2 * LANE), lambda n, g: (nc - 1 - n, g))
    return _pcall(
        body, name=name, grid=(nc, ng),
        in_specs=[q_spec, k_spec, v_spec, s_spec, s_spec, ck_spec, _gdn_t_spec(nc, True), o_spec, _ANY],
        out_specs=[dq_spec, dq_spec, o_spec, s_spec, s_spec, _ANY],
        out_shape=[qk_shape, qk_shape, big, small, small, jax.ShapeDtypeStruct(parts.shape, parts.dtype)],
        scratch_shapes=[pltpu.VMEM((nh, GDN_DK, LANE), F32), pltpu.SemaphoreType.DMA((3,)),
                        pltpu.SemaphoreType.DMA((3,)), pltpu.SemaphoreType.DMA],
        compiler_params=_params(2),
    )(qkv, qkv, qkv, beta, gc, ckpt, t_saved, do, parts)


LOG2E = 1.4426950408889634
LN2 = 0.6931471805599453
Q_PRESCALE = MLA_QK ** -0.5 * LOG2E


ATT_SUB = 256
ATT_HEADS_PER_STEP = 8
ATT_BWD_HEADS_PER_STEP = 2


def _att_mask(i, j, tb, transposed):
    r = lax.broadcasted_iota(jnp.int32, (tb, tb), 0)
    c = lax.broadcasted_iota(jnp.int32, (tb, tb), 1)
    qpos, kpos = (i * tb + c, j * tb + r) if transposed else (i * tb + r, j * tb + c)
    return jnp.logical_and(kpos <= qpos, kpos >= FRONT)


def _causal_pairs(nb, by_key):
    if by_key:
        pairs = [(i, j) for j in range(nb) for i in range(j, nb)]
    else:
        pairs = [(i, j) for i in range(nb) for j in range(i + 1)]
    return jnp.array([p[0] for p in pairs], jnp.int32), jnp.array([p[1] for p in pairs], jnp.int32)


def _masked_and_plain(i, j, step):
    edge = jnp.logical_or(j == i, j == 0)

    @pl.when(jnp.logical_and(edge, j <= i))
    def _():
        step(True)

    @pl.when(jnp.logical_and(jnp.logical_not(edge), j < i))
    def _():
        step(False)


def _cat(a_ref, b_ref):
    return jnp.concatenate([a_ref[...], b_ref[...]], axis=1)


def _flash_fwd(qn, qr, kn, kr, v, name, tb=ROW_ALIGN):
    lp = qn.shape[0]
    nb = lp // tb
    nh = MLA_HEADS
    hp = ATT_HEADS_PER_STEP
    qi, kj = _causal_pairs(nb, by_key=False)

    def body(qi_ref, kj_ref, qn_ref, qr_ref, kn_ref, kr_ref, v_ref, o_ref, lse_ref, m_s, l_s, acc):
        t = pl.program_id(1)
        i, j = qi_ref[t], kj_ref[t]

        @pl.when(j == 0)
        def _():
            m_s[...] = jnp.full_like(m_s, NEG)
            l_s[...] = jnp.zeros_like(l_s)
            acc[...] = jnp.zeros_like(acc)

        def step(masked):
            n_sub = tb // ATT_SUB
            kr = kr_ref[...]
            for e in range(hp):
                lanes = pl.ds(e * LANE, LANE)
                k = jnp.concatenate([kn_ref[:, lanes], kr], axis=1)
                v = v_ref[:, lanes]

                def scores(r, lanes=lanes, k=k):
                    rows = pl.ds(r * ATT_SUB, ATT_SUB)
                    return _dot(jnp.concatenate([qn_ref[rows, lanes], qr_ref[rows, lanes]], axis=1), k, _NT)

                s_next = scores(0)
                for r in range(n_sub):
                    s = s_next
                    if r + 1 < n_sub:
                        s_next = scores(r + 1)
                    rows = pl.ds(r * ATT_SUB, ATT_SUB)
                    if masked:
                        qpos = i * tb + r * ATT_SUB + lax.broadcasted_iota(jnp.int32, (ATT_SUB, tb), 0)
                        kpos = j * tb + lax.broadcasted_iota(jnp.int32, (ATT_SUB, tb), 1)
                        s = jnp.where(jnp.logical_and(kpos <= qpos, kpos >= FRONT), s, NEG)
                    m_old = m_s[e, rows, :]
                    m_new = jnp.maximum(m_old, jnp.max(s, axis=1, keepdims=True))
                    alpha = jnp.exp2(m_old - m_new)
                    p = jnp.exp2(s - m_new)
                    l_s[e, rows, :] = alpha * l_s[e, rows, :] + jnp.sum(p, axis=1, keepdims=True)
                    acc[e, rows, :] = alpha * acc[e, rows, :] + _dot(p.astype(BF16), v, _NN)
                    m_s[e, rows, :] = m_new

        _masked_and_plain(i, j, step)

        @pl.when(j == i)
        def _():
            for e in range(hp):
                lanes = pl.ds(e * LANE, LANE)
                o_ref[:, lanes] = acc[e] / l_s[e]
                lse_ref[:, lanes] = jnp.broadcast_to(m_s[e] + jnp.log(l_s[e]) * LOG2E, (tb, LANE))

    qspec = pl.BlockSpec((tb, hp * LANE), lambda h, t, qi_, kj_: (qi_[t], h))
    kspec = pl.BlockSpec((tb, hp * LANE), lambda h, t, qi_, kj_: (kj_[t], h))
    krspec = pl.BlockSpec((tb, LANE), lambda h, t, qi_, kj_: (kj_[t], 0))
    shp = jax.ShapeDtypeStruct((lp, nh * LANE), F32)
    return _pcall(
        body, name=name, out_shape=[shp, shp],
        grid_spec=pltpu.PrefetchScalarGridSpec(
            num_scalar_prefetch=2, grid=(nh // hp, qi.shape[0]),
            in_specs=[qspec, qspec, kspec, krspec, kspec], out_specs=[qspec, qspec],
            scratch_shapes=[pltpu.VMEM((hp, tb, 1), F32), pltpu.VMEM((hp, tb, 1), F32),
                            pltpu.VMEM((hp, tb, LANE), F32)]),
        compiler_params=_params(2),
    )(qi, kj, qn, qr, kn, kr, v)


def _flash_bwd(qn, qr, kn, kr, v, o, do, lse, name, tb=ROW_ALIGN):
    lp = qn.shape[0]
    nb = lp // tb
    nh = MLA_HEADS
    hp = ATT_BWD_HEADS_PER_STEP
    qi, kj = _causal_pairs(nb, by_key=True)
    n_pairs = qi.shape[0]
    knt, krt = kn.T, kr.T

    def body(qi_ref, kj_ref, qn_ref, qr_ref, kn_ref, kr_ref, knt_ref, krt_ref, v_ref, o_ref, do_ref, lse_ref,
             dqnt_hbm, dqrt_hbm, dkn_ref, dkr_ref, dv_ref, dk_acc, dv_acc, dqn_acc, dqr_acc, out_sems):
        g = pl.program_id(0)
        t = pl.program_id(1)
        i, j = qi_ref[t], kj_ref[t]

        @pl.when(t == 0)
        def _():
            dqn_acc[...] = jnp.zeros_like(dqn_acc)
            dqr_acc[...] = jnp.zeros_like(dqr_acc)

        @pl.when(i == j)
        def _():
            dk_acc[...] = jnp.zeros_like(dk_acc)
            dv_acc[...] = jnp.zeros_like(dv_acc)

        def step(masked):
            kr = kr_ref[...]
            krt_blk = krt_ref[...]
            lane = lax.broadcasted_iota(jnp.int32, (8, LANE), 1)
            for e in range(hp):
                lanes = pl.ds(e * LANE, LANE)
                q = jnp.concatenate([qn_ref[:, lanes], qr_ref[:, lanes]], axis=1)
                st = _dot(jnp.concatenate([kn_ref[:, lanes], kr], axis=1), q, _NT)
                if masked:
                    st = jnp.where(_att_mask(i, j, tb, True), st, NEG)
                do_blk = do_ref[:, lanes]
                lse_row = _hdot((lane == 0).astype(F32), lse_ref[:, lanes], _NT)[0:1]
                delta_row = _hdot(jnp.ones((8, LANE), F32), do_blk * o_ref[:, lanes], _NT)[0:1]
                pt = jnp.exp2(st - lse_row)
                do_b = do_blk.astype(BF16)
                dv_acc[e] += _dot(pt.astype(BF16), do_b, _NN)
                dpt = _dot(v_ref[:, lanes], do_b, _NT)
                dst = (pt * (dpt - delta_row)).astype(BF16)
                dk_acc[e] += _dot(dst, q, _NN)
                dqn_acc[e * nb + i] += _dot(knt_ref[pl.ds(e * LANE, LANE), :], dst, _NN) * LN2
                dqr_acc[e * nb + i] += _dot(krt_blk, dst, _NN) * LN2

        _masked_and_plain(i, j, step)

        @pl.when(i == nb - 1)
        def _():
            for e in range(hp):
                lanes = pl.ds(e * LANE, LANE)
                dkn_ref[:, lanes] = dk_acc[e, :, :LANE] * LN2
                dkr_ref[:, lanes] = dk_acc[e, :, LANE:] * LN2
                dv_ref[:, lanes] = dv_acc[e]

        @pl.when(t == n_pairs - 1)
        def _():
            dst_rows = pl.ds(g * (hp * nb), hp * nb)
            cn = pltpu.make_async_copy(dqn_acc, dqnt_hbm.at[dst_rows], out_sems.at[0])
            cr = pltpu.make_async_copy(dqr_acc, dqrt_hbm.at[dst_rows], out_sems.at[1])
            cn.start()
            cr.start()
            cn.wait()
            cr.wait()

    qspec = pl.BlockSpec((tb, hp * LANE), lambda h, t, qi_, kj_: (qi_[t], h))
    kspec = pl.BlockSpec((tb, hp * LANE), lambda h, t, qi_, kj_: (kj_[t], h))
    krspec = pl.BlockSpec((tb, LANE), lambda h, t, qi_, kj_: (kj_[t], 0))
    ktspec = pl.BlockSpec((hp * LANE, tb), lambda h, t, qi_, kj_: (h, kj_[t]))
    krtspec = pl.BlockSpec((LANE, tb), lambda h, t, qi_, kj_: (0, kj_[t]))
    shp = jax.ShapeDtypeStruct((lp, nh * LANE), F32)
    dqt_shape = jax.ShapeDtypeStruct((nh * nb, LANE, tb), F32)
    dqnt, dqrt, dkn, dkr, dv = _pcall(
        body, name=name, out_shape=[dqt_shape, dqt_shape, shp, shp, shp],
        grid_spec=pltpu.PrefetchScalarGridSpec(
            num_scalar_prefetch=2, grid=(nh // hp, n_pairs),
            in_specs=[qspec, qspec, kspec, krspec, ktspec, krtspec, kspec, qspec, qspec, qspec],
            out_specs=[_ANY, _ANY, kspec, kspec, kspec],
            scratch_shapes=[pltpu.VMEM((hp, tb, 2 * LANE), F32), pltpu.VMEM((hp, tb, LANE), F32),
                            pltpu.VMEM((hp * nb, LANE, tb), F32), pltpu.VMEM((hp * nb, LANE, tb), F32),
                            pltpu.SemaphoreType.DMA((2,))]),
        compiler_params=_params(2),
    )(qi, kj, qn, qr, kn, kr, knt, krt, v, o, do, lse)

    def rows_major(a):
        return a.reshape(nh, nb, LANE, tb).transpose(1, 3, 0, 2).reshape(lp, nh * LANE)

    return rows_major(dqnt), rows_major(dqrt), dkn, dkr, dv


ELEMENTWISE_BLOCK_BYTES = 1 << 20


def _row_tile(rows, width, copies=1):
    for t in (1024, 512, 256, 128, 64, 32, 16, 8):
        if rows % t == 0 and t * width * 4 * copies <= ELEMENTWISE_BLOCK_BYTES:
            return t
    return rows


def _adamw(w, g, m, v, name):
    rows, width = w.shape
    tr = _row_tile(rows, width)

    def body(w_ref, g_ref, m_ref, v_ref, d_ref, nm_ref, nv_ref):
        gg = g_ref[...]
        nm = ADAM_B1 * m_ref[...] + (1.0 - ADAM_B1) * gg
        nv = ADAM_B2 * v_ref[...] + (1.0 - ADAM_B2) * jnp.square(gg)
        m_hat = nm / (1.0 - ADAM_B1 ** ADAM_STEP)
        v_hat = nv / (1.0 - ADAM_B2 ** ADAM_STEP)
        d_ref[...] = -ADAM_LR * (m_hat / (jnp.sqrt(v_hat) + ADAM_EPS) + ADAM_WD * w_ref[...])
        nm_ref[...] = nm
        nv_ref[...] = nv

    spec = pl.BlockSpec((tr, width), lambda r: (r, 0))
    shp = jax.ShapeDtypeStruct((rows, width), F32)
    return _pcall(body, name=name, grid=(rows // tr,), in_specs=[spec] * 4, out_specs=[spec] * 3,
                  out_shape=[shp] * 3, compiler_params=_params(1))(w, g, m, v)


def _add_pair(a, b, name):
    s, rows, width = b.shape
    tr = _row_tile(rows, width)
    nt = rows // tr

    def body(c_ref, a_ref, b_ref, o_ref):
        o_ref[...] = a_ref[...] + b_ref[...]

    spec = pl.BlockSpec((1, tr, width), lambda i, r, c_ref: (i, r, 0))
    return _pcall(
        body, name=name, out_shape=jax.ShapeDtypeStruct(b.shape, F32),
        grid_spec=pltpu.PrefetchScalarGridSpec(
            num_scalar_prefetch=1, grid=(s, nt),
            in_specs=[pl.BlockSpec((1, tr, width), lambda i, r, c_ref: (i, c_ref[0] * nt + r, 0)), spec],
            out_specs=spec),
        compiler_params=_params(2),
    )(_core_index(), a, b)


def _sum_slots(a, name):
    s, rows, width = a.shape
    tr = _row_tile(rows, width, copies=s)

    def body(a_ref, o_ref):
        tot = a_ref[0]
        for k in range(1, s):
            tot = tot + a_ref[k]
        o_ref[...] = tot

    return _pcall(body, name=name, grid=(rows // tr,),
                  in_specs=[pl.BlockSpec((s, tr, width), lambda r: (0, r, 0))],
                  out_specs=pl.BlockSpec((tr, width), lambda r: (r, 0)),
                  out_shape=jax.ShapeDtypeStruct((rows, width), F32), compiler_params=_params(1))(a)


_ANY = pl.BlockSpec(memory_space=pl.ANY)


def _my_place():
    return lax.axis_index("x"), lax.axis_index("y"), lax.axis_index("c")


def _core_index():
    return lax.axis_index("c").astype(jnp.int32).reshape(1)


def _other_chips(x, y):
    return [(1 - x, y), (x, 1 - y), (1 - x, 1 - y)]


def _gather_shards(flat, name):
    rows, width = flat.shape

    def body(x_ref, out_ref, send_sems, recv_sems, local_sem):
        x, y, c = _my_place()
        mine = pltpu.make_async_copy(x_ref, out_ref.at[2 * x + y], local_sem)
        mine.start()
        sends = []
        for k, (px, py) in enumerate(_other_chips(x, y)):
            cp = pltpu.make_async_remote_copy(
                src_ref=x_ref, dst_ref=out_ref.at[2 * x + y], send_sem=send_sems.at[k], recv_sem=recv_sems.at[k],
                device_id=(px, py, c), device_id_type=MESH)
            cp.start()
            sends.append(cp)
        for k, (px, py) in enumerate(_other_chips(x, y)):
            pltpu.make_async_remote_copy(
                src_ref=x_ref, dst_ref=out_ref.at[2 * px + py], send_sem=send_sems.at[k], recv_sem=recv_sems.at[k],
                device_id=(px, py, c), device_id_type=MESH).wait_recv()
        for cp in sends:
            cp.wait_send()
        mine.wait()

    return _pcall(
        body, name=name, in_specs=[_ANY], out_specs=_ANY,
        out_shape=jax.ShapeDtypeStruct((4, rows, width), flat.dtype),
        scratch_shapes=[pltpu.SemaphoreType.DMA((3,)), pltpu.SemaphoreType.DMA((3,)), pltpu.SemaphoreType.DMA],
    )(flat)


def _gather_half_shards(shard, name):
    rows, width = shard.shape
    half = rows // 2

    def body(x_ref, out_ref, send_sems, recv_sems, local_sem):
        x, y, c = _my_place()
        mine_rows = x_ref.at[pl.ds(pl.multiple_of(c * half, 8), half), :]
        mine = pltpu.make_async_copy(mine_rows, out_ref.at[2 * x + y], local_sem)
        mine.start()
        sends = []
        for k, (px, py) in enumerate(_other_chips(x, y)):
            cp = pltpu.make_async_remote_copy(
                src_ref=mine_rows, dst_ref=out_ref.at[2 * x + y], send_sem=send_sems.at[k], recv_sem=recv_sems.at[k],
                device_id=(px, py, c), device_id_type=MESH)
            cp.start()
            sends.append(cp)
        for k, (px, py) in enumerate(_other_chips(x, y)):
            pltpu.make_async_remote_copy(
                src_ref=mine_rows, dst_ref=out_ref.at[2 * px + py], send_sem=send_sems.at[k],
                recv_sem=recv_sems.at[k], device_id=(px, py, c), device_id_type=MESH).wait_recv()
        for cp in sends:
            cp.wait_send()
        mine.wait()

    return _pcall(
        body, name=name, in_specs=[_ANY], out_specs=_ANY,
        out_shape=jax.ShapeDtypeStruct((4, half, width), shard.dtype),
        scratch_shapes=[pltpu.SemaphoreType.DMA((3,)), pltpu.SemaphoreType.DMA((3,)), pltpu.SemaphoreType.DMA],
    )(shard)


def _sibling_swap(p, name):
    s, rows, width = p.shape
    tr = _row_tile(rows, width)
    nt = rows // tr

    def body(p_blk, got_ref, send_sem, recv_sem):
        k = pl.program_id(0)
        t = pl.program_id(1)
        x, y, c = _my_place()
        cp = pltpu.make_async_remote_copy(
            src_ref=p_blk.at[0], dst_ref=got_ref.at[k, pl.ds(pl.multiple_of(t * tr, 8), tr), :],
            send_sem=send_sem, recv_sem=recv_sem, device_id=(x, y, 1 - c), device_id_type=MESH)
        cp.start()
        cp.wait_send()

        @pl.when(jnp.logical_and(k == s - 1, t == nt - 1))
        def _():
            pltpu.make_async_remote_copy(
                src_ref=got_ref, dst_ref=got_ref, send_sem=send_sem, recv_sem=recv_sem,
                device_id=(x, y, 1 - c), device_id_type=MESH).wait_recv()

    return _pcall(
        body, name=name, grid=(s, nt),
        in_specs=[pl.BlockSpec((1, tr, width), lambda k, t: (k, t, 0))], out_specs=_ANY,
        out_shape=jax.ShapeDtypeStruct(p.shape, p.dtype),
        scratch_shapes=[pltpu.SemaphoreType.DMA, pltpu.SemaphoreType.DMA],
        compiler_params=_params(2),
    )(p)


def _sibling_split(g, name):
    s, rows, width = g.shape
    half = rows // 2
    tr = _row_tile(half, width)
    nt = half // tr

    def body(c_ref, g_blk, got_ref, send_sem, recv_sem):
        k = pl.program_id(0)
        t = pl.program_id(1)
        x, y, c = _my_place()
        cp = pltpu.make_async_remote_copy(
            src_ref=g_blk.at[0], dst_ref=got_ref.at[k, pl.ds(pl.multiple_of(t * tr, 8), tr), :],
            send_sem=send_sem, recv_sem=recv_sem, device_id=(x, y, 1 - c), device_id_type=MESH)
        cp.start()
        cp.wait_send()

        @pl.when(jnp.logical_and(k == s - 1, t == nt - 1))
        def _():
            pltpu.make_async_remote_copy(
                src_ref=got_ref, dst_ref=got_ref, send_sem=send_sem, recv_sem=recv_sem,
                device_id=(x, y, 1 - c), device_id_type=MESH).wait_recv()

    return _pcall(
        body, name=name, out_shape=jax.ShapeDtypeStruct((s, half, width), g.dtype),
        grid_spec=pltpu.PrefetchScalarGridSpec(
            num_scalar_prefetch=1, grid=(s, nt),
            in_specs=[pl.BlockSpec((1, tr, width), lambda k, t, c_ref: (k, (1 - c_ref[0]) * nt + t, 0))],
            out_specs=_ANY,
            scratch_shapes=[pltpu.SemaphoreType.DMA, pltpu.SemaphoreType.DMA]),
        compiler_params=_params(2),
    )(_core_index(), g)


def _chip_scatter(p, name):
    s, rows, width = p.shape

    def body(p_ref, out_ref, send_sems, recv_sems, local_sem):
        x, y, c = _my_place()
        me = 2 * x + y
        mine = pltpu.make_async_copy(p_ref.at[me], out_ref.at[me], local_sem)
        mine.start()
        sends = []
        for k, (px, py) in enumerate(_other_chips(x, y)):
            cp = pltpu.make_async_remote_copy(
                src_ref=p_ref.at[2 * px + py], dst_ref=out_ref.at[me], send_sem=send_sems.at[k],
                recv_sem=recv_sems.at[k], device_id=(px, py, c), device_id_type=MESH)
            cp.start()
            sends.append(cp)
        for k, (px, py) in enumerate(_other_chips(x, y)):
            pltpu.make_async_remote_copy(
                src_ref=p_ref.at[me], dst_ref=out_ref.at[2 * px + py], send_sem=send_sems.at[k],
                recv_sem=recv_sems.at[k], device_id=(px, py, c), device_id_type=MESH).wait_recv()
        for cp in sends:
            cp.wait_send()
        mine.wait()

    return _pcall(
        body, name=name, in_specs=[_ANY], out_specs=_ANY,
        out_shape=jax.ShapeDtypeStruct(p.shape, p.dtype),
        scratch_shapes=[pltpu.SemaphoreType.DMA((3,)), pltpu.SemaphoreType.DMA((3,)), pltpu.SemaphoreType.DMA],
    )(p)


def _sibling_join(qh, name):
    half, width = qh.shape
    tr = _row_tile(half, width)
    nt = half // tr

    def body(q_blk, out_ref, send_sem, recv_sem, local_sem):
        t = pl.program_id(0)
        x, y, c = _my_place()
        dst = out_ref.at[pl.ds(pl.multiple_of(c * half + t * tr, 8), tr), :]
        cp = pltpu.make_async_remote_copy(
            src_ref=q_blk, dst_ref=dst, send_sem=send_sem, recv_sem=recv_sem,
            device_id=(x, y, 1 - c), device_id_type=MESH)
        cp.start()
        mine = pltpu.make_async_copy(q_blk, dst, local_sem)
        mine.start()
        cp.wait_send()
        mine.wait()

        @pl.when(t == nt - 1)
        def _():
            theirs = out_ref.at[pl.ds(pl.multiple_of((1 - c) * half, 8), half), :]
            pltpu.make_async_remote_copy(
                src_ref=theirs, dst_ref=theirs, send_sem=send_sem, recv_sem=recv_sem,
                device_id=(x, y, 1 - c), device_id_type=MESH).wait_recv()

    return _pcall(
        body, name=name, grid=(nt,),
        in_specs=[pl.BlockSpec((tr, width), lambda t: (t, 0))], out_specs=_ANY,
        out_shape=jax.ShapeDtypeStruct((2 * half, width), qh.dtype),
        scratch_shapes=[pltpu.SemaphoreType.DMA, pltpu.SemaphoreType.DMA, pltpu.SemaphoreType.DMA],
        compiler_params=_params(1),
    )(qh)


def _all_sum_small(part, name):
    rows, width = part.shape

    def body(p_ref, out_ref, land, send_sems, recv_sems):
        x, y, c = _my_place()
        me = 4 * x + 2 * y + c
        land[me] = p_ref[...]
        sends = []
        for k in range(1, 8):
            peer = (x ^ (k >> 2), y ^ ((k >> 1) & 1), c ^ (k & 1))
            cp = pltpu.make_async_remote_copy(
                src_ref=p_ref, dst_ref=land.at[me], send_sem=send_sems.at[k - 1], recv_sem=recv_sems.at[k - 1],
                device_id=peer, device_id_type=MESH)
            cp.start()
            sends.append(cp)
        for k in range(1, 8):
            px, py, pc = x ^ (k >> 2), y ^ ((k >> 1) & 1), c ^ (k & 1)
            pltpu.make_async_remote_copy(
                src_ref=p_ref, dst_ref=land.at[4 * px + 2 * py + pc], send_sem=send_sems.at[k - 1],
                recv_sem=recv_sems.at[k - 1], device_id=(px, py, pc), device_id_type=MESH).wait_recv()
        for cp in sends:
            cp.wait_send()
        tot = land[0]
        for k in range(1, 8):
            tot = tot + land[k]
        out_ref[...] = tot

    vmem = pl.BlockSpec(memory_space=pltpu.VMEM)
    return _pcall(
        body, name=name, in_specs=[vmem], out_specs=vmem,
        out_shape=jax.ShapeDtypeStruct((rows, width), F32),
        scratch_shapes=[pltpu.VMEM((8, rows, width), F32), pltpu.SemaphoreType.DMA((7,)),
                        pltpu.SemaphoreType.DMA((7,))],
    )(part)


def _big_layout(shards):
    return [(a.shape[0], a.shape[1], ax) for a, ax in shards]


FLAT_ROW_MULTIPLE = 2048


def _pack_shards(arrs, row_multiple=FLAT_ROW_MULTIPLE):
    flat = jnp.concatenate([a.reshape(-1) for a in arrs])
    return jnp.pad(flat, (0, -flat.shape[0] % (row_multiple * LANE))).reshape(-1, LANE)


def _unpack_shards(flat, layout):
    flat = flat.reshape(-1)
    out, off = [], 0
    for r, c, _ in layout:
        out.append(flat[off:off + r * c].reshape(r, c))
        off += r * c
    return out


def _unpack_full(gathered, layout):
    g = gathered.reshape(4, -1)
    out, off = [], 0
    for r, c, ax in layout:
        seg = g[:, off:off + r * c].reshape(4, r, c)
        out.append(seg.transpose(1, 0, 2).reshape(r, 4 * c) if ax == 1 else seg.reshape(4 * r, c))
        off += r * c
    return out


def _pack_full(fulls, layout):
    parts = []
    for a, (r, c, ax) in zip(fulls, layout):
        if ax == 1:
            parts.append(a.reshape(r, 4, c).transpose(1, 0, 2).reshape(4, r * c))
        else:
            parts.append(a.reshape(4, r * c))
    flat = jnp.concatenate(parts, axis=1)
    return jnp.pad(flat, ((0, 0), (0, -flat.shape[1] % (FLAT_ROW_MULTIPLE * LANE)))).reshape(4, -1, LANE)


def _pad_lanes(a, width=LANE):
    return jnp.pad(a, [(0, 0)] * (a.ndim - 1) + [(0, width - a.shape[-1])])


def _pack_small(arrs):
    rows = [_pad_lanes(a.reshape(1, -1), -(-a.size // LANE) * LANE).reshape(-1, LANE) for a in arrs]
    flat = jnp.concatenate(rows, axis=0)
    return jnp.pad(flat, ((0, -flat.shape[0] % 8), (0, 0)))


def _unpack_small(flat, shapes):
    out, off = [], 0
    for shp in shapes:
        n = math.prod(shp)
        nr = -(-n // LANE)
        out.append(flat[off:off + nr].reshape(-1)[:n].reshape(shp))
        off += nr
    return out


def kernel(x, meta_tokens, pre_norm, post_norm, gdn_w_in, gdn_conv_w, gdn_a_log, gdn_dt_bias, gdn_out_norm, gdn_w_out, kv_norm, kv_w_down, kv_latent_norm, kv_w_up, mla_w_in, mla_q_latent_norm, mla_w_q_up, mla_w_out, loss_target, m_meta_tokens, m_pre_norm, m_post_norm, m_gdn_w_in, m_gdn_conv_w, m_gdn_a_log, m_gdn_dt_bias, m_gdn_out_norm, m_gdn_w_out, m_kv_norm, m_kv_w_down, m_kv_latent_norm, m_kv_w_up, m_mla_w_in, m_mla_q_latent_norm, m_mla_w_q_up, m_mla_w_out, v_meta_tokens, v_pre_norm, v_post_norm, v_gdn_w_in, v_gdn_conv_w, v_gdn_a_log, v_gdn_dt_bias, v_gdn_out_norm, v_gdn_w_out, v_kv_norm, v_kv_w_down, v_kv_latent_norm, v_kv_w_up, v_mla_w_in, v_mla_q_latent_norm, v_mla_w_q_up, v_mla_w_out):
    seq = x.shape[1]
    d = D_MODEL
    lp = -(-(ROW0 + seq) // ROW_ALIGN) * ROW_ALIGN
    tail = lp - ROW0 - seq

    big_names = ["meta_tokens", "gdn_conv_w", "gdn_w_out", "kv_w_down", "kv_w_up", "mla_w_in", "mla_w_q_up",
                 "mla_w_out"]
    big_axis = [1, 1, 0, 0, 1, 1, 1, 0]
    big_w = [meta_tokens, gdn_conv_w[0], gdn_w_out[0], kv_w_down, kv_w_up, mla_w_in[0], mla_w_q_up[0], mla_w_out[0]]
    big_m = [m_meta_tokens, m_gdn_conv_w[0], m_gdn_w_out[0], m_kv_w_down, m_kv_w_up, m_mla_w_in[0], m_mla_w_q_up[0],
             m_mla_w_out[0]]
    big_v = [v_meta_tokens, v_gdn_conv_w[0], v_gdn_w_out[0], v_kv_w_down, v_kv_w_up, v_mla_w_in[0], v_mla_w_q_up[0],
             v_mla_w_out[0]]
    layout = _big_layout(list(zip(big_w, big_axis)))
    meta_f, conv_w = _unpack_full(
        _gather_shards(_pack_shards(big_w[:2], row_multiple=16), "gather_meta_conv"), layout[:2])
    mm_shards = [w.astype(BF16) for w in big_w[2:6]] + [(big_w[6] * Q_PRESCALE).astype(BF16), big_w[7].astype(BF16)]
    mm_flat = _pack_shards(mm_shards)
    win_mine = _gather_half_shards(gdn_w_in[0].astype(BF16), "gather_gdn_w_in")
    win_theirs = _sibling_swap(win_mine, "gather_gdn_w_in_sibling")
    south = lax.axis_index("c") == 0
    win_lo = jnp.where(south, win_mine, win_theirs)
    win_hi = jnp.where(south, win_theirs, win_mine)
    w_in0 = jnp.concatenate([jnp.concatenate([win_lo[s], win_hi[s]], axis=0) for s in range(4)], axis=1)
    win_cols = gdn_w_in.shape[2]

    nv = GDN_V_HEADS
    w_qkv = w_in0[:, :GDN_CONV_W]
    w_z0 = w_in0[:, GDN_CONV_W:GDN_CONV_W + GDN_V_W]
    w_b = _pad_lanes(w_in0[:, GDN_CONV_W + GDN_V_W:GDN_CONV_W + GDN_V_W + nv])
    w_a = _pad_lanes(w_in0[:, GDN_CONV_W + GDN_V_W + nv:])

    pre0, pre1 = pre_norm[0:1], pre_norm[1:2]
    post0, post1 = post_norm[0:1], post_norm[1:2]
    a_log = _pad_lanes(gdn_a_log)
    dt_bias = _pad_lanes(gdn_dt_bias)
    kvn = kv_norm.reshape(1, d)
    kvl = kv_latent_norm.reshape(1, MLA_KV_RANK)
    qln = mla_q_latent_norm

    h0 = jnp.concatenate([jnp.zeros((FRONT, d), F32), meta_f, x[0], jnp.zeros((tail, d), F32)], axis=0)
    tgt = jnp.pad(loss_target[0], ((ROW0, tail), (0, 0)))
    pos = jnp.maximum(jnp.arange(lp, dtype=jnp.int32) - FRONT, 0).astype(F32)
    inv = ROPE_THETA ** (-jnp.arange(0, MLA_ROPE, 2, dtype=F32) / MLA_ROPE)
    ang = pos[:, None] * inv[None, :]
    zeros64 = jnp.zeros((lp, LANE - MLA_ROPE), F32)
    cos_t = jnp.concatenate([jnp.cos(ang), jnp.cos(ang), zeros64], axis=1)
    sin_t = jnp.concatenate([-jnp.sin(ang), jnp.sin(ang), zeros64], axis=1)

    def valid_rows(ridx):
        return jnp.logical_and(ridx >= FRONT, ridx < ROW0 + seq)

    def f_pre0(ridx, g, h, gain):
        return _rms(h, gain), h

    (hn0,) = _rowwise("pre0", lambda *a: f_pre0(*a)[:1], [_In(h0), _In(pre0, "const")],
                      [_Out("row", (lp, d), BF16)])
    qkv_raw = _mm(hn0, w_qkv, "nn", "gdn_in_qkv")
    z0 = _mm(hn0, w_z0, "nn", "gdn_in_z")
    b_raw = _mm(hn0, w_b, "nn", "gdn_in_b")
    a_raw = _mm(hn0, w_a, "nn", "gdn_in_a")

    def f_ba(ridx, g, b, a, alog, dtb):
        tr = b.shape[0]
        ok = valid_rows(ridx).astype(F32)
        beta = jax.nn.sigmoid(b) * ok
        gate = -jnp.exp(alog) * _softplus(a + dtb) * ok
        ii = lax.broadcasted_iota(jnp.int32, (tr, tr), 0)
        jj = lax.broadcasted_iota(jnp.int32, (tr, tr), 1)
        shift = GDN_CHUNK.bit_length() - 1
        tri = jnp.logical_and((ii >> shift) == (jj >> shift), ii >= jj).astype(F32)
        return beta, _hdot(tri, gate)

    ba_ins = [_In(b_raw), _In(a_raw), _In(a_log, "const"), _In(dt_bias, "const")]
    beta, gc = _rowwise("gdn_gates", f_ba, ba_ins, [_Out("row", (lp, LANE)), _Out("row", (lp, LANE))])
    qkv = _conv_fwd(qkv_raw, conv_w, "gdn_conv")
    o0, ckpt, t_saved, mm_all = _gdn_fwd(qkv, beta, gc, mm_flat, "gdn_scan")
    (w_out0, kv_down, kv_up, w_in1, w_qup, w_out1) = _unpack_full(mm_all, layout[2:])
    w_ckv = kv_down[:, :MLA_KV_RANK]
    w_kr = _pad_lanes(kv_down[:, MLA_KV_RANK:])
    kvu = kv_up.reshape(MLA_KV_RANK, MLA_HEADS, 2 * LANE)
    w_kn = kvu[:, :, :LANE].reshape(MLA_KV_RANK, MLA_HEADS * LANE)
    w_v = kvu[:, :, LANE:].reshape(MLA_KV_RANK, MLA_HEADS * LANE)
    w_cq = w_in1[:, :MLA_Q_RANK]
    w_z1 = w_in1[:, MLA_Q_RANK:]
    qu = w_qup.reshape(MLA_Q_RANK, MLA_HEADS, MLA_QK)
    w_qn = qu[:, :, :MLA_NOPE].reshape(MLA_Q_RANK, MLA_HEADS * LANE)
    w_qr = _pad_lanes(qu[:, :, MLA_NOPE:]).reshape(MLA_Q_RANK, MLA_HEADS * LANE)

    def per_head(fn, *arrs):
        n = arrs[0].shape[1] // LANE
        return jnp.concatenate([fn(*[a[:, i * LANE:(i + 1) * LANE] for a in arrs]) for i in range(n)], axis=1)

    def f_gate0(ridx, g, o, z, gain):
        return (per_head(lambda oh, zh: _rms(oh, gain) * _silu(zh), o, z),)

    gate0_ins = [_In(o0), _In(z0), _In(gdn_out_norm, "const")]
    (gated0,) = _rowwise("gdn_gate", f_gate0, gate0_ins, [_Out("row", (lp, GDN_V_W), BF16)])
    y0 = _mm(gated0, w_out0, "nn", "gdn_out")

    def f_mid(ridx, g, h, y, g_post, g_pre, g_kv):
        h1 = h + _rms(y, g_post)
        return h1, _rms(h1, g_pre), _rms(h1, g_kv)

    mid_ins = [_In(h0), _In(y0), _In(post0, "const"), _In(pre1, "const"), _In(kvn, "const")]
    h1, hn1, hkv = _rowwise("mid", f_mid, mid_ins,
                            [_Out("row", (lp, d)), _Out("row", (lp, d), BF16), _Out("row", (lp, d), BF16)])

    ckv_raw = _mm(hkv, w_ckv, "nn", "kv_down_c")
    kr_raw = _mm(hkv, w_kr, "nn", "kv_down_r")

    def f_ckv(ridx, g, c, r, cs, sn, gain):
        return _rms(c, gain), _rope(r, cs, sn)

    ckv_ins = [_In(ckv_raw), _In(kr_raw), _In(cos_t), _In(sin_t), _In(kvl, "const")]
    ckv, kr = _rowwise("kv_latent", f_ckv, ckv_ins, [_Out("row", (lp, LANE)), _Out("row", (lp, LANE), BF16)],
                       tr=TR_FULL)
    kn = _mm(ckv, w_kn, "nn", "kv_up_k", BF16)
    vv = _mm(ckv, w_v, "nn", "kv_up_v", BF16)
    cq_raw = _mm(hn1, w_cq, "nn", "mla_in_q")
    z1 = _mm(hn1, w_z1, "nn", "mla_in_z")

    def f_cq(ridx, g, c, gain):
        return (_rms(c, gain),)

    cq_ins = [_In(cq_raw), _In(qln, "const")]
    (cq,) = _rowwise("q_latent", f_cq, cq_ins, [_Out("row", (lp, MLA_Q_RANK))], tr=TR_FULL)
    qn = _mm(cq, w_qn, "nn", "q_up_n", BF16)
    qr_raw = _mm(cq, w_qr, "nn", "q_up_r")

    def f_qrope(ridx, g, r, cs, sn):
        return (per_head(lambda rh: _rope(rh, cs, sn), r),)

    qr_ins = [_In(qr_raw), _In(cos_t), _In(sin_t)]
    (qr,) = _rowwise("q_rope", f_qrope, qr_ins, [_Out("row", (lp, MLA_HEADS * LANE), BF16)])
    o1, lse = _flash_fwd(qn, qr, kn, kr, vv, "attention")

    def f_gate1(ridx, g, o, z):
        return (o * _silu(z),)

    gate1_ins = [_In(o1), _In(z1)]
    (og,) = _rowwise("mla_gate", f_gate1, gate1_ins, [_Out("row", (lp, MLA_HEADS * LANE), BF16)])
    y1 = _mm(og, w_out1, "nn", "mla_out")

    def f_final(ridx, g, h, y, t, gain):
        ok = jnp.logical_and(ridx >= ROW0, ridx < ROW0 + seq).astype(F32)

        def rows_loss(h_, y_, gain_):
            err = (h_ + _rms(y_, gain_) - t) * ok
            return 0.5 * jnp.sum(jnp.sum(err * err, axis=1, keepdims=True), axis=0, keepdims=True) / d

        val, vjp = jax.vjp(rows_loss, h, y, gain)
        dh, dy, dgain = vjp(jnp.ones((1, 1), F32))
        return dh, dy, dgain, jnp.broadcast_to(val, (1, LANE))

    dh2, dy1, dpost1, loss_part = _rowwise(
        "loss_head", f_final, [_In(h1), _In(y1), _In(tgt), _In(post1, "const")],
        [_Out("row", (lp, d)), _Out("row", (lp, d)), _Out("acc", (1, d)), _Out("acc", (1, LANE))])

    dog = _mm(dy1, w_out1, "nt", "mla_out_dx")
    dw_out1 = _mm(og, dy1, "tn", "mla_out_dw")
    do1, dz1 = _rowwise_vjp("mla_gate_bwd", f_gate1, gate1_ins, [[dog]], [0, 1])
    dqn, dqr, dkn, dkr, dvv = _flash_bwd(qn, qr, kn, kr, vv, o1, do1, lse, "attention_bwd")
    (dqr_raw,) = _rowwise_vjp("q_rope_bwd", f_qrope, qr_ins, [[dqr]], [0])
    dcq_a = _mm(dqn, w_qn, "nt", "q_up_n_dx")
    dcq_b = _mm(dqr_raw, w_qr, "nt", "q_up_r_dx")
    dw_qn = _mm(cq, dqn, "tn", "q_up_n_dw") * Q_PRESCALE
    dw_qr = _mm(cq, dqr_raw, "tn", "q_up_r_dw") * Q_PRESCALE
    dcq_raw, dqln = _rowwise_vjp("q_latent_bwd", f_cq, cq_ins, [[dcq_a, dcq_b]], [0, 1], tr=TR_FULL)
    dhn1_a = _mm(dcq_raw, w_cq, "nt", "mla_in_q_dx")
    dhn1_b = _mm(dz1, w_z1, "nt", "mla_in_z_dx")
    dw_cq = _mm(hn1, dcq_raw, "tn", "mla_in_q_dw")
    dw_z1 = _mm(hn1, dz1, "tn", "mla_in_z_dw")
    dckv_a = _mm(dkn, w_kn, "nt", "kv_up_k_dx")
    dckv_b = _mm(dvv, w_v, "nt", "kv_up_v_dx")
    dw_kn = _mm(ckv, dkn, "tn", "kv_up_k_dw")
    dw_v = _mm(ckv, dvv, "tn", "kv_up_v_dw")
    dckv_raw, dkr_raw, dkvl = _rowwise_vjp("kv_latent_bwd", f_ckv, ckv_ins, [[dckv_a, dckv_b], [dkr]], [0, 1, 4],
                                           tr=TR_FULL)
    dhkv_a = _mm(dckv_raw, w_ckv, "nt", "kv_down_c_dx")
    dhkv_b = _mm(dkr_raw, w_kr, "nt", "kv_down_r_dx")
    dw_ckv = _mm(hkv, dckv_raw, "tn", "kv_down_c_dw")
    dw_kr = _mm(hkv, dkr_raw, "tn", "kv_down_r_dw")
    dh0_res, dy0, dpost0, dpre1, dkvn = _rowwise_vjp(
        "mid_bwd", f_mid, mid_ins, [[dh2], [dhn1_a, dhn1_b], [dhkv_a, dhkv_b]], [0, 1, 2, 3, 4])

    dgated0 = _mm(dy0, w_out0, "nt", "gdn_out_dx")
    dw_out0 = _mm(gated0, dy0, "tn", "gdn_out_dw")
    do0, dz0, doutn = _rowwise_vjp("gdn_gate_bwd", f_gate0, gate0_ins, [[dgated0]], [0, 1, 2], tr=TR_QUARTER)

    g_kv_down = jnp.concatenate([dw_ckv, dw_kr[:, :MLA_ROPE]], axis=1)
    g_kv_up = jnp.concatenate([dw_kn.reshape(MLA_KV_RANK, MLA_HEADS, LANE), dw_v.reshape(MLA_KV_RANK, MLA_HEADS, LANE)],
                              axis=2).reshape(MLA_KV_RANK, MLA_HEADS * 2 * LANE)
    g_w_in1 = jnp.concatenate([dw_cq, dw_z1], axis=1)
    g_qup = jnp.concatenate([dw_qn.reshape(MLA_Q_RANK, MLA_HEADS, LANE),
                             dw_qr.reshape(MLA_Q_RANK, MLA_HEADS, LANE)[:, :, :MLA_ROPE]],
                            axis=2).reshape(MLA_Q_RANK, MLA_HEADS * MLA_QK)
    g_mm = _pack_full([dw_out0, g_kv_down, g_kv_up, g_w_in1, g_qup, dw_out1], layout[2:])
    mm_chip_part = _add_pair(g_mm, _sibling_split(g_mm, "grads_sibling_split"), "grads_chip_sum")
    dq0, dk0, dv0, dbeta, dgc, mm_from_chips = _gdn_bwd(qkv, beta, gc, ckpt, t_saved, do0, mm_chip_part,
                                                        "gdn_scan_bwd")
    g_flat = _sibling_join(_sum_slots(mm_from_chips, "grads_total"), "grads_sibling_join")
    db_raw, da_raw, dalog, ddtb = _rowwise_vjp("gdn_gates_bwd", f_ba, ba_ins, [[dbeta], [dgc]], [0, 1, 2, 3])
    dqkv_raw, dconv = _conv_bwd(qkv_raw, conv_w, dq0, dk0, dv0, "gdn_conv_bwd")
    dhn0_a = _mm(dqkv_raw, w_qkv, "nt", "gdn_in_qkv_dx")
    dhn0_b = _mm(dz0, w_z0, "nt", "gdn_in_z_dx")
    dhn0_c = _mm(db_raw, w_b, "nt", "gdn_in_b_dx")
    dhn0_d = _mm(da_raw, w_a, "nt", "gdn_in_a_dx")
    dw_qkv = _mm(hn0, dqkv_raw, "tn", "gdn_in_qkv_dw")
    dw_z0 = _mm(hn0, dz0, "tn", "gdn_in_z_dw")
    dw_b = _mm(hn0, db_raw, "tn", "gdn_in_b_dw")
    dw_a = _mm(hn0, da_raw, "tn", "gdn_in_a_dw")
    dh0, dpre0 = _rowwise_vjp("pre0_bwd", f_pre0, [_In(h0), _In(pre0, "const")],
                              [[dhn0_a, dhn0_b, dhn0_c, dhn0_d], [dh0_res]], [0, 1])

    grad_x = dh0[ROW0:ROW0 + seq][None]
    g_meta = dh0[FRONT:ROW0]
    g_w_in0 = jnp.concatenate([dw_qkv, dw_z0, dw_b[:, :nv], dw_a[:, :nv]], axis=1)

    g_win_by_chip = jnp.concatenate([g_w_in0[None, :, s * win_cols:(s + 1) * win_cols] for s in range(4)], axis=0)
    win_chip_part = _add_pair(g_win_by_chip, _sibling_split(g_win_by_chip, "grads_sibling_split_gdn_w_in"),
                              "grads_chip_sum_gdn_w_in")
    win_from_chips = _chip_scatter(win_chip_part, "grads_chip_scatter_gdn_w_in")
    g_win = _sibling_join(_sum_slots(win_from_chips, "grads_total_gdn_w_in"), "grads_sibling_join_gdn_w_in")

    small_shapes = [(2, d), (2, d), (1, nv), (1, nv), (1, GDN_DK), (d,), (MLA_KV_RANK,), (1, MLA_Q_RANK),
                    g_meta.shape, dconv.shape, (1, LANE)]
    small_part = _pack_small([jnp.concatenate([dpre0, dpre1], axis=0), jnp.concatenate([dpost0, dpost1], axis=0),
                              dalog[:, :nv], ddtb[:, :nv], doutn, dkvn, dkvl, dqln, g_meta, dconv, loss_part])
    small_tot = _all_sum_small(small_part, "small_sum")
    small_g = _unpack_small(small_tot, small_shapes)
    loss = small_g[-1][0, 0]
    chip = 2 * lax.axis_index("x") + lax.axis_index("y")
    meta_cols, conv_cols = meta_tokens.shape[1], gdn_conv_w.shape[2]
    g_meta_shard = lax.dynamic_slice(small_g[8], (0, chip * meta_cols), (small_g[8].shape[0], meta_cols))
    g_conv_shard = lax.dynamic_slice(small_g[9], (0, chip * conv_cols), (small_g[9].shape[0], conv_cols))

    d_flat, m_flat, v_flat = _adamw(_pack_shards(big_w[2:]), g_flat, _pack_shards(big_m[2:]), _pack_shards(big_v[2:]),
                                    "adamw_sharded")
    win_step = _adamw(gdn_w_in[0], g_win, m_gdn_w_in[0], v_gdn_w_in[0], "adamw_gdn_w_in")
    small_names = ["pre_norm", "post_norm", "gdn_a_log", "gdn_dt_bias", "gdn_out_norm", "kv_norm", "kv_latent_norm",
                   "mla_q_latent_norm", "meta_tokens", "gdn_conv_w"]
    small_w = [pre_norm, post_norm, gdn_a_log, gdn_dt_bias, gdn_out_norm, kv_norm, kv_latent_norm, mla_q_latent_norm,
               meta_tokens, gdn_conv_w]
    small_m = [m_pre_norm, m_post_norm, m_gdn_a_log, m_gdn_dt_bias, m_gdn_out_norm, m_kv_norm, m_kv_latent_norm,
               m_mla_q_latent_norm, m_meta_tokens, m_gdn_conv_w]
    small_v = [v_pre_norm, v_post_norm, v_gdn_a_log, v_gdn_dt_bias, v_gdn_out_norm, v_kv_norm, v_kv_latent_norm,
               v_mla_q_latent_norm, v_meta_tokens, v_gdn_conv_w]
    g_small_flat = _pack_small(small_g[:8] + [g_meta_shard, g_conv_shard])
    ds_flat, ms_flat, vs_flat = _adamw(_pack_small(small_w), g_small_flat, _pack_small(small_m), _pack_small(small_v),
                                       "adamw_replicated")

    def assemble(big_flat, small_flat, win):
        bigs = dict(zip(big_names[2:], [a.reshape(w.shape) for a, w in zip(
            _unpack_shards(big_flat, layout[2:]),
            [gdn_w_out, kv_w_down, kv_w_up, mla_w_in, mla_w_q_up, mla_w_out])]))
        smalls = dict(zip(small_names, _unpack_small(small_flat, [w.shape for w in small_w])))
        both = {**bigs, **smalls, "gdn_w_in": win[None]}
        order = ["meta_tokens", "pre_norm", "post_norm", "gdn_w_in", "gdn_conv_w", "gdn_a_log", "gdn_dt_bias",
                 "gdn_out_norm", "gdn_w_out", "kv_norm", "kv_w_down", "kv_latent_norm", "kv_w_up", "mla_w_in",
                 "mla_q_latent_norm", "mla_w_q_up", "mla_w_out"]
        return [both[n] for n in order]

    grads = assemble(g_flat, g_small_flat, g_win)
    deltas = assemble(d_flat, ds_flat, win_step[0])
    new_m = assemble(m_flat, ms_flat, win_step[1])
    new_v = assemble(v_flat, vs_flat, win_step[2])
    return (loss, grad_x, *grads, *deltas, *new_m, *new_v)
```
